```python
import jax, jax.numpy as jnp
from jax import lax
import numpy as np

D_MODEL = 1024
BATCH = 8
SEQ = 4096
DEPTH = 2

A_WIDTH = D_MODEL // 2
B_WIDTH = D_MODEL - A_WIDTH
A_HEADS = 8
B_HEADS = 8
A_CONV = 31
B_CONV = 3
IN_WIDTH = 2 * A_WIDTH + 3 * B_WIDTH
POOL_WINDOWS = (2, 4, 8, 16)
POOL_GROUPS = len(POOL_WINDOWS)
POOL_GROUP = D_MODEL // POOL_GROUPS
D_FF = ((8 * D_MODEL // 3 + 255) // 256) * 256
FFN_CONV = 3
RMS_EPS = 1e-6
LN_EPS = 1e-5
N_EVEN = (DEPTH + 1) // 2
N_ODD = DEPTH // 2

kernel_name = "hybrid_conv_pool_convffn_trunk"


def rmsnorm(x, g):
    xf = x.astype(jnp.float32)
    y = xf * lax.rsqrt(jnp.mean(xf * xf, axis=-1, keepdims=True) + RMS_EPS)
    return (y * g.astype(jnp.float32)).astype(x.dtype)


def layernorm(x, g, b):
    xf = x.astype(jnp.float32)
    mu = jnp.mean(xf, axis=-1, keepdims=True)
    xc = xf - mu
    var = jnp.mean(xc * xc, axis=-1, keepdims=True)
    y = xc * lax.rsqrt(var + LN_EPS) * g.astype(jnp.float32) + b.astype(jnp.float32)
    return y.astype(x.dtype)


def causal_dwconv(x, w):
    k, c = w.shape
    return lax.conv_general_dilated(
        x, w[:, None, :].astype(x.dtype), window_strides=(1,), padding=[(k - 1, 0)],
        dimension_numbers=("NWC", "WIO", "NWC"), feature_group_count=c)


def conv_mixer(h, w_in, conv_a, ln_a_g, ln_a_b, conv_b, w_out):
    z = h @ w_in
    a_val, a_gate, b_gate, c_gate, bc_val = jnp.split(
        z, [A_WIDTH, 2 * A_WIDTH, 2 * A_WIDTH + B_WIDTH, 2 * A_WIDTH + 2 * B_WIDTH], axis=-1)
    a = causal_dwconv(a_val * jax.nn.sigmoid(a_gate), conv_a)
    a = jax.nn.silu(layernorm(a, ln_a_g, ln_a_b))
    b = b_gate * causal_dwconv(c_gate * bc_val, conv_b)
    return jnp.concatenate([a, b], axis=-1) @ w_out


def pool_mixer(h, w_pool, pool_scale):
    s = h.shape[1]
    hf = h.astype(jnp.float32)
    cs = jnp.cumsum(hf, axis=1)
    t = jnp.arange(1, s + 1, dtype=jnp.float32)[:, None]
    outs = []
    for g, w in enumerate(POOL_WINDOWS):
        sl = slice(g * POOL_GROUP, (g + 1) * POOL_GROUP)
        c = cs[..., sl]
        prev = jnp.pad(c, ((0, 0), (w, 0), (0, 0)))[:, :s]
        mean = (c - prev) / jnp.minimum(t, float(w))
        outs.append(mean - hf[..., sl])
    p = jnp.stack(outs, axis=2).astype(h.dtype)
    y = jnp.einsum("bsgc,gcd->bsgd", p, w_pool).reshape(h.shape)
    return y * pool_scale


def conv_ffn(h, w_up, w_conv, w_down):
    u = causal_dwconv(h @ w_up, w_conv)
    g, v = jnp.split(u, 2, axis=-1)
    return (jax.nn.silu(g) * v) @ w_down


def _fwd_setup_inputs(seed: int = 0) -> dict:
    key = jax.random.key(seed)
    ks = jax.random.split(key, 20)
    f32 = jnp.float32
    nrm = lambda k, shape, scale: jax.random.normal(k, shape, f32) * scale
    return {
        "x": nrm(ks[0], (BATCH, SEQ, D_MODEL), 1.0),
        "norm_mix_even": 1.0 + nrm(ks[1], (N_EVEN, D_MODEL), 0.02),
        "w_in": nrm(ks[2], (N_EVEN, D_MODEL, IN_WIDTH), D_MODEL ** -0.5),
        "conv_a": nrm(ks[3], (N_EVEN, A_CONV, A_WIDTH), A_CONV ** -0.5),
        "ln_a_g": 1.0 + nrm(ks[4], (N_EVEN, A_WIDTH), 0.02),
        "ln_a_b": nrm(ks[5], (N_EVEN, A_WIDTH), 0.02),
        "conv_b": nrm(ks[6], (N_EVEN, B_CONV, B_WIDTH), B_CONV ** -0.5),
        "w_out": nrm(ks[7], (N_EVEN, D_MODEL, D_MODEL), D_MODEL ** -0.5),
        "norm_mix_odd": 1.0 + nrm(ks[8], (N_ODD, D_MODEL), 0.02),
        "w_pool": nrm(ks[9], (N_ODD, POOL_GROUPS, POOL_GROUP, POOL_GROUP), POOL_GROUP ** -0.5),
        "pool_scale": 1.0 + nrm(ks[10], (N_ODD, D_MODEL), 0.1),
        "norm_ffn": 1.0 + nrm(ks[11], (DEPTH, D_MODEL), 0.02),
        "w_up": nrm(ks[12], (DEPTH, D_MODEL, 2 * D_FF), D_MODEL ** -0.5),
        "conv_ffn_w": nrm(ks[13], (DEPTH, FFN_CONV, 2 * D_FF), FFN_CONV ** -0.5),
        "w_down": nrm(ks[14], (DEPTH, D_FF, D_MODEL), D_FF ** -0.5),
        "norm_final": 1.0 + nrm(ks[15], (D_MODEL,), 0.02),
    }


def _fwd_reference(x, norm_mix_even, w_in, conv_a, ln_a_g, ln_a_b, conv_b, w_out,
              norm_mix_odd, w_pool, pool_scale, norm_ffn, w_up, conv_ffn_w, w_down,
              norm_final):
    for layer in range(DEPTH):
        i = layer // 2
        if layer % 2 == 0:
            h = rmsnorm(x, norm_mix_even[i])
            x = x + conv_mixer(h, w_in[i], conv_a[i], ln_a_g[i], ln_a_b[i], conv_b[i], w_out[i])
        else:
            h = rmsnorm(x, norm_mix_odd[i])
            x = x + pool_mixer(h, w_pool[i], pool_scale[i])
        x = x + conv_ffn(rmsnorm(x, norm_ffn[layer]), w_up[layer], conv_ffn_w[layer], w_down[layer])
    return rmsnorm(x, norm_final)


import jax as _jax
import jax.numpy as _jnp

TWIN_FORMAT = 'train_step'
FWD_PARAMS = ['x', 'norm_mix_even', 'w_in', 'conv_a', 'ln_a_g', 'ln_a_b', 'conv_b', 'w_out', 'norm_mix_odd', 'w_pool', 'pool_scale', 'norm_ffn', 'w_up', 'conv_ffn_w', 'w_down', 'norm_final']
TWIN_WEIGHTS = ['norm_mix_even', 'w_in', 'conv_a', 'ln_a_g', 'ln_a_b', 'conv_b', 'w_out', 'norm_mix_odd', 'w_pool', 'pool_scale', 'norm_ffn', 'w_up', 'conv_ffn_w', 'w_down', 'norm_final']
TWIN_DIFF_INPUT = 'x'
TWIN_INPUTS = ['x', 'norm_mix_even', 'w_in', 'conv_a', 'ln_a_g', 'ln_a_b', 'conv_b', 'w_out', 'norm_mix_odd', 'w_pool', 'pool_scale', 'norm_ffn', 'w_up', 'conv_ffn_w', 'w_down', 'norm_final', 'loss_target', 'm_norm_mix_even', 'm_w_in', 'm_conv_a', 'm_ln_a_g', 'm_ln_a_b', 'm_conv_b', 'm_w_out', 'm_norm_mix_odd', 'm_w_pool', 'm_pool_scale', 'm_norm_ffn', 'm_w_up', 'm_conv_ffn_w', 'm_w_down', 'm_norm_final', 'v_norm_mix_even', 'v_w_in', 'v_conv_a', 'v_ln_a_g', 'v_ln_a_b', 'v_conv_b', 'v_w_out', 'v_norm_mix_odd', 'v_w_pool', 'v_pool_scale', 'v_norm_ffn', 'v_w_up', 'v_conv_ffn_w', 'v_w_down', 'v_norm_final']
TWIN_OUTPUTS = ['loss', 'grad_x', 'grad_norm_mix_even', 'grad_w_in', 'grad_conv_a', 'grad_ln_a_g', 'grad_ln_a_b', 'grad_conv_b', 'grad_w_out', 'grad_norm_mix_odd', 'grad_w_pool', 'grad_pool_scale', 'grad_norm_ffn', 'grad_w_up', 'grad_conv_ffn_w', 'grad_w_down', 'grad_norm_final', 'delta_norm_mix_even', 'delta_w_in', 'delta_conv_a', 'delta_ln_a_g', 'delta_ln_a_b', 'delta_conv_b', 'delta_w_out', 'delta_norm_mix_odd', 'delta_w_pool', 'delta_pool_scale', 'delta_norm_ffn', 'delta_w_up', 'delta_conv_ffn_w', 'delta_w_down', 'delta_norm_final', 'new_m_norm_mix_even', 'new_m_w_in', 'new_m_conv_a', 'new_m_ln_a_g', 'new_m_ln_a_b', 'new_m_conv_b', 'new_m_w_out', 'new_m_norm_mix_odd', 'new_m_w_pool', 'new_m_pool_scale', 'new_m_norm_ffn', 'new_m_w_up', 'new_m_conv_ffn_w', 'new_m_w_down', 'new_m_norm_final', 'new_v_norm_mix_even', 'new_v_w_in', 'new_v_conv_a', 'new_v_ln_a_g', 'new_v_ln_a_b', 'new_v_conv_b', 'new_v_w_out', 'new_v_norm_mix_odd', 'new_v_w_pool', 'new_v_pool_scale', 'new_v_norm_ffn', 'new_v_w_up', 'new_v_conv_ffn_w', 'new_v_w_down', 'new_v_norm_final']
TWIN_LEAF_KINDS = {'loss': 'loss', 'grad_x': 'grad_x', 'grad_norm_mix_even': 'grad_w', 'grad_w_in': 'grad_w', 'grad_conv_a': 'grad_w', 'grad_ln_a_g': 'grad_w', 'grad_ln_a_b': 'grad_w', 'grad_conv_b': 'grad_w', 'grad_w_out': 'grad_w', 'grad_norm_mix_odd': 'grad_w', 'grad_w_pool': 'grad_w', 'grad_pool_scale': 'grad_w', 'grad_norm_ffn': 'grad_w', 'grad_w_up': 'grad_w', 'grad_conv_ffn_w': 'grad_w', 'grad_w_down': 'grad_w', 'grad_norm_final': 'grad_w', 'delta_norm_mix_even': 'delta_w', 'delta_w_in': 'delta_w', 'delta_conv_a': 'delta_w', 'delta_ln_a_g': 'delta_w', 'delta_ln_a_b': 'delta_w', 'delta_conv_b': 'delta_w', 'delta_w_out': 'delta_w', 'delta_norm_mix_odd': 'delta_w', 'delta_w_pool': 'delta_w', 'delta_pool_scale': 'delta_w', 'delta_norm_ffn': 'delta_w', 'delta_w_up': 'delta_w', 'delta_conv_ffn_w': 'delta_w', 'delta_w_down': 'delta_w', 'delta_norm_final': 'delta_w', 'new_m_norm_mix_even': 'new_m', 'new_m_w_in': 'new_m', 'new_m_conv_a': 'new_m', 'new_m_ln_a_g': 'new_m', 'new_m_ln_a_b': 'new_m', 'new_m_conv_b': 'new_m', 'new_m_w_out': 'new_m', 'new_m_norm_mix_odd': 'new_m', 'new_m_w_pool': 'new_m', 'new_m_pool_scale': 'new_m', 'new_m_norm_ffn': 'new_m', 'new_m_w_up': 'new_m', 'new_m_conv_ffn_w': 'new_m', 'new_m_w_down': 'new_m', 'new_m_norm_final': 'new_m', 'new_v_norm_mix_even': 'new_v', 'new_v_w_in': 'new_v', 'new_v_conv_a': 'new_v', 'new_v_ln_a_g': 'new_v', 'new_v_ln_a_b': 'new_v', 'new_v_conv_b': 'new_v', 'new_v_w_out': 'new_v', 'new_v_norm_mix_odd': 'new_v', 'new_v_w_pool': 'new_v', 'new_v_pool_scale': 'new_v', 'new_v_norm_ffn': 'new_v', 'new_v_w_up': 'new_v', 'new_v_conv_ffn_w': 'new_v', 'new_v_w_down': 'new_v', 'new_v_norm_final': 'new_v'}


def _forward(args):
    return _fwd_reference(*[args[k] for k in FWD_PARAMS])


def _output_shape():
    out = _jax.eval_shape(lambda: _forward(_fwd_setup_inputs(0)))
    return out.shape, out.dtype

N_MICROBATCH = 1
ADAM_LR = 0.001
ADAM_B1 = 0.9
ADAM_B2 = 0.999
ADAM_EPS = 1e-08
ADAM_WD = 0.01
ADAM_STEP = 10
PER_EXAMPLE_BATCH_AXIS = {'x': 0, 'loss_target': 0}
SHARED_INPUTS = []
_WEIGHT_DTYPES = {'norm_mix_even': _jnp.float32, 'w_in': _jnp.float32, 'conv_a': _jnp.float32, 'ln_a_g': _jnp.float32, 'ln_a_b': _jnp.float32, 'conv_b': _jnp.float32, 'w_out': _jnp.float32, 'norm_mix_odd': _jnp.float32, 'w_pool': _jnp.float32, 'pool_scale': _jnp.float32, 'norm_ffn': _jnp.float32, 'w_up': _jnp.float32, 'conv_ffn_w': _jnp.float32, 'w_down': _jnp.float32, 'norm_final': _jnp.float32}
MOMENT_SCALE = {'norm_mix_even': 2.091541e-01, 'w_in': 1.328028e-01, 'conv_a': 9.856732e-02, 'ln_a_g': 1.285373e-01, 'ln_a_b': 8.917511e-02, 'conv_b': 1.617676e-01, 'w_out': 1.308386e-01, 'norm_mix_odd': 1.065751e-01, 'w_pool': 9.929432e-02, 'pool_scale': 2.291514e-01, 'norm_ffn': 1.060647e-01, 'w_up': 4.422581e-02, 'conv_ffn_w': 4.410786e-02, 'w_down': 7.221180e-02, 'norm_final': 3.206186e+01}


def _to_microbatches(a, axis):
    t = _jnp.moveaxis(a, axis, 0)
    t = t.reshape((N_MICROBATCH, t.shape[0] // N_MICROBATCH) + t.shape[1:])
    return _jnp.moveaxis(t, 1, axis + 1)


def setup_inputs(seed: int = 0) -> dict:
    inp = _fwd_setup_inputs(seed)
    key = _jax.random.fold_in(_jax.random.key(seed), 7919)
    shape, _ = _output_shape()
    out = dict(inp)
    out["loss_target"] = _jax.random.normal(_jax.random.fold_in(key, 0), shape, _jnp.float32)
    for i, name in enumerate(TWIN_WEIGHTS):
        w = inp[name].astype(_jnp.float32)
        if MOMENT_SCALE is None:
            s = _jnp.sqrt(_jnp.mean(_jnp.square(w)) + 1e-30)
        else:
            s = MOMENT_SCALE[name]
        km, kv = _jax.random.split(_jax.random.fold_in(key, i + 1))
        out[name] = w
        out["m_" + name] = s * _jax.random.normal(km, w.shape, _jnp.float32)
        out["v_" + name] = (s * s) * _jax.random.uniform(kv, w.shape, _jnp.float32, 0.5, 1.5)
    if N_MICROBATCH > 1:
        for name, axis in PER_EXAMPLE_BATCH_AXIS.items():
            out[name] = _to_microbatches(out[name], axis)
    return {'x': out['x'], 'norm_mix_even': out['norm_mix_even'], 'w_in': out['w_in'], 'conv_a': out['conv_a'], 'ln_a_g': out['ln_a_g'], 'ln_a_b': out['ln_a_b'], 'conv_b': out['conv_b'], 'w_out': out['w_out'], 'norm_mix_odd': out['norm_mix_odd'], 'w_pool': out['w_pool'], 'pool_scale': out['pool_scale'], 'norm_ffn': out['norm_ffn'], 'w_up': out['w_up'], 'conv_ffn_w': out['conv_ffn_w'], 'w_down': out['w_down'], 'norm_final': out['norm_final'], 'loss_target': out['loss_target'], 'm_norm_mix_even': out['m_norm_mix_even'], 'm_w_in': out['m_w_in'], 'm_conv_a': out['m_conv_a'], 'm_ln_a_g': out['m_ln_a_g'], 'm_ln_a_b': out['m_ln_a_b'], 'm_conv_b': out['m_conv_b'], 'm_w_out': out['m_w_out'], 'm_norm_mix_odd': out['m_norm_mix_odd'], 'm_w_pool': out['m_w_pool'], 'm_pool_scale': out['m_pool_scale'], 'm_norm_ffn': out['m_norm_ffn'], 'm_w_up': out['m_w_up'], 'm_conv_ffn_w': out['m_conv_ffn_w'], 'm_w_down': out['m_w_down'], 'm_norm_final': out['m_norm_final'], 'v_norm_mix_even': out['v_norm_mix_even'], 'v_w_in': out['v_w_in'], 'v_conv_a': out['v_conv_a'], 'v_ln_a_g': out['v_ln_a_g'], 'v_ln_a_b': out['v_ln_a_b'], 'v_conv_b': out['v_conv_b'], 'v_w_out': out['v_w_out'], 'v_norm_mix_odd': out['v_norm_mix_odd'], 'v_w_pool': out['v_w_pool'], 'v_pool_scale': out['v_pool_scale'], 'v_norm_ffn': out['v_norm_ffn'], 'v_w_up': out['v_w_up'], 'v_conv_ffn_w': out['v_conv_ffn_w'], 'v_w_down': out['v_w_down'], 'v_norm_final': out['v_norm_final']}


def _loss(weights, diff, rest, loss_target):
    with _jax.named_scope("forward"):
        args = {**rest, TWIN_DIFF_INPUT: diff, **{k: w.astype(_WEIGHT_DTYPES[k]) for k, w in weights.items()}}
        y = _forward(args)
    with _jax.named_scope("loss_head"):
        err = _jnp.square(y.astype(_jnp.float32) - loss_target)
        return 0.5 * _jnp.sum(_jnp.mean(err, axis=-1)) if err.ndim else 0.5 * err


def _adamw(w, g, m, v):
    m = ADAM_B1 * m + (1.0 - ADAM_B1) * g
    v = ADAM_B2 * v + (1.0 - ADAM_B2) * _jnp.square(g)
    m_hat = m / (1.0 - ADAM_B1 ** ADAM_STEP)
    v_hat = v / (1.0 - ADAM_B2 ** ADAM_STEP)
    delta = -ADAM_LR * (m_hat / (_jnp.sqrt(v_hat) + ADAM_EPS) + ADAM_WD * w)
    return delta, m, v


def reference(x, norm_mix_even, w_in, conv_a, ln_a_g, ln_a_b, conv_b, w_out, norm_mix_odd, w_pool, pool_scale, norm_ffn, w_up, conv_ffn_w, w_down, norm_final, loss_target, m_norm_mix_even, m_w_in, m_conv_a, m_ln_a_g, m_ln_a_b, m_conv_b, m_w_out, m_norm_mix_odd, m_w_pool, m_pool_scale, m_norm_ffn, m_w_up, m_conv_ffn_w, m_w_down, m_norm_final, v_norm_mix_even, v_w_in, v_conv_a, v_ln_a_g, v_ln_a_b, v_conv_b, v_w_out, v_norm_mix_odd, v_w_pool, v_pool_scale, v_norm_ffn, v_w_up, v_conv_ffn_w, v_w_down, v_norm_final):
    given = dict(x=x, norm_mix_even=norm_mix_even, w_in=w_in, conv_a=conv_a, ln_a_g=ln_a_g, ln_a_b=ln_a_b, conv_b=conv_b, w_out=w_out, norm_mix_odd=norm_mix_odd, w_pool=w_pool, pool_scale=pool_scale, norm_ffn=norm_ffn, w_up=w_up, conv_ffn_w=conv_ffn_w, w_down=w_down, norm_final=norm_final, loss_target=loss_target, m_norm_mix_even=m_norm_mix_even, m_w_in=m_w_in, m_conv_a=m_conv_a, m_ln_a_g=m_ln_a_g, m_ln_a_b=m_ln_a_b, m_conv_b=m_conv_b, m_w_out=m_w_out, m_norm_mix_odd=m_norm_mix_odd, m_w_pool=m_w_pool, m_pool_scale=m_pool_scale, m_norm_ffn=m_norm_ffn, m_w_up=m_w_up, m_conv_ffn_w=m_conv_ffn_w, m_w_down=m_w_down, m_norm_final=m_norm_final, v_norm_mix_even=v_norm_mix_even, v_w_in=v_w_in, v_conv_a=v_conv_a, v_ln_a_g=v_ln_a_g, v_ln_a_b=v_ln_a_b, v_conv_b=v_conv_b, v_w_out=v_w_out, v_norm_mix_odd=v_norm_mix_odd, v_w_pool=v_w_pool, v_pool_scale=v_pool_scale, v_norm_ffn=v_norm_ffn, v_w_up=v_w_up, v_conv_ffn_w=v_conv_ffn_w, v_w_down=v_w_down, v_norm_final=v_norm_final)
    weights = {n: given[n] for n in TWIN_WEIGHTS}
    shared = {n: given[n] for n in SHARED_INPUTS}
    per_example = {n: given[n] for n in ['x']}
    grad_fn = _jax.value_and_grad(_loss, argnums=(0, 1))

    def one_microbatch(ex, loss_target):
        ex = dict(ex)
        diff = ex.pop(TWIN_DIFF_INPUT)
        return grad_fn(weights, diff, {**shared, **ex}, loss_target)

    if N_MICROBATCH == 1:
        loss, (grad_w, grad_x) = one_microbatch(per_example, given["loss_target"])
    else:
        def body(carry, xs):
            loss_sum, grad_sum = carry
            l_k, (gw_k, gx_k) = one_microbatch(xs[0], xs[1])
            with _jax.named_scope("update"):
                return (loss_sum + l_k, _jax.tree.map(_jnp.add, grad_sum, gw_k)), gx_k

        init = (_jnp.zeros((), _jnp.float32), _jax.tree.map(_jnp.zeros_like, weights))
        (loss, grad_w), grad_x = _jax.lax.scan(body, init, (per_example, given["loss_target"]))
    with _jax.named_scope("update"):
        delta_w, new_m, new_v = {}, {}, {}
        for n in TWIN_WEIGHTS:
            delta_w[n], new_m[n], new_v[n] = _adamw(weights[n], grad_w[n], given["m_" + n], given["v_" + n])
    return (loss, grad_x, *[grad_w[n] for n in TWIN_WEIGHTS], *[delta_w[n] for n in TWIN_WEIGHTS],
            *[new_m[n] for n in TWIN_WEIGHTS], *[new_v[n] for n in TWIN_WEIGHTS])
```

```python
import functools

import jax
import jax.numpy as jnp
from jax import lax
from jax.experimental import pallas as pl
from jax.experimental.pallas import tpu as pltpu

F32, BF16 = jnp.float32, jnp.bfloat16

D = 1024
A = 512
NZ = 5 * A
FF = 2816
FF2 = 2 * FF
NCHIP = 4
NDEV = 8
K_A, K_S = 31, 3
POOL_WINDOWS = (2, 4, 8, 16)
PG = D // len(POOL_WINDOWS)
RMS_EPS, LN_EPS = 1e-6, 1e-5
ADAM_LR, ADAM_B1, ADAM_B2, ADAM_EPS, ADAM_WD, ADAM_STEP = 0.001, 0.9, 0.999, 1e-08, 0.01, 10

HALO_A, HALO_S, HALO_P = 32, 8, 16
SUBLANES = 8
LANES = 128
VMEM_LIMIT_BYTES = 56 * 1024 * 1024

TS_MIX = 512
TS_MIXB = 256
TS_FFN = 256
TS_POOL = 512
TS_MM = 512
R_CHUNK = 64

MESH = pl.DeviceIdType.MESH
ANY = pl.BlockSpec(memory_space=pl.ANY)
NT_DIMS = (((1,), (1,)), ((), ()))
TN_DIMS = (((0,), (0,)), ((), ()))


def _cparams(n_axes):
    return pltpu.CompilerParams(dimension_semantics=("arbitrary",) * n_axes, vmem_limit_bytes=VMEM_LIMIT_BYTES)


def _const(shape):
    nd = len(shape)
    return pl.BlockSpec(shape, lambda *_: (0,) * nd, pipeline_mode=pl.Buffered(1))


def _sigmoid(v):
    return 1.0 / (1.0 + jnp.exp(-v))


def _rsqrt_mean_sq(x):
    return lax.rsqrt(jnp.mean(x * x, axis=-1, keepdims=True) + RMS_EPS)


def _rms_bwd(dh, xh, r, g):
    dxh = dh * g
    return r * (dxh - xh * jnp.mean(dxh * xh, axis=-1, keepdims=True))


def _shifted(buf_ref, row0, rows, col0, width, offsets):
    lo = (min(offsets) // SUBLANES) * SUBLANES
    hi = -(-(max(offsets) + rows) // SUBLANES) * SUBLANES
    win = buf_ref[pl.ds(pl.multiple_of(row0 + lo, SUBLANES), hi - lo), col0:col0 + width]
    out = {}
    for res in sorted({(o - lo) % SUBLANES for o in offsets}):
        qs = {o: (o - lo) // SUBLANES for o in offsets if (o - lo) % SUBLANES == res}
        base = win[res:res + rows + SUBLANES * max(qs.values()), :]
        for o, q in qs.items():
            out[o] = base[SUBLANES * q:SUBLANES * q + rows, :]
    return out


def _rowsum8(v):
    acc = v[0:SUBLANES, :]
    for r in range(SUBLANES, v.shape[0], SUBLANES):
        acc = acc + v[r:r + SUBLANES, :]
    return acc


def _taps(w_ref, sh, offsets, col0, width):
    acc = None
    for k, o in enumerate(offsets):
        term = w_ref[k:k + 1, col0:col0 + width] * sh[o]
        acc = term if acc is None else acc + term
    return acc


def _finish_tap_sums(acc_ref, out_ref, n_taps):
    for k in range(n_taps):
        out_ref[k:k + 1, :] = jnp.sum(acc_ref[SUBLANES * k:SUBLANES * (k + 1), :], axis=0, keepdims=True)


def _mix0_fwd(x, g, w_in, conv_a, ln_g, ln_b, conv_b, w_out):
    s = x.shape[0]
    ts = min(TS_MIX, s)
    n = s // ts
    bw = NZ // NCHIP
    offs_a = [HALO_A - (K_A - 1) + k for k in range(K_A)]
    offs_s = [HALO_S - (K_S - 1) + k for k in range(K_S)]

    def body(x_ref, g_ref, win_ref, ca_ref, lg_ref, lb_ref, cb_ref, wout_ref,
             h_ref, z_ref, ac_ref, cat_ref, x1_ref, glu_buf, cv_buf, bconv_buf):
        i = pl.program_id(0)

        @pl.when(i == 0)
        def _():
            glu_buf[0:HALO_A, :] = jnp.zeros((HALO_A, A), F32)
            cv_buf[0:HALO_S, :] = jnp.zeros((HALO_S, A), F32)

        xv = x_ref[...]
        h = (xv * _rsqrt_mean_sq(xv) * g_ref[...]).astype(BF16)
        h_ref[...] = h
        for j in range(NCHIP):
            z_ref[:, j * bw:(j + 1) * bw] = jnp.dot(h, win_ref[j], preferred_element_type=F32)
        glu_buf[HALO_A:HALO_A + ts, :] = z_ref[:, 0:A] * _sigmoid(z_ref[:, A:2 * A])
        cv_buf[HALO_S:HALO_S + ts, :] = z_ref[:, 3 * A:4 * A] * z_ref[:, 4 * A:5 * A]

        def chunk(ci, carry):
            r0 = pl.multiple_of(ci * R_CHUNK, R_CHUNK)
            for c0 in range(0, A, LANES):
                sh = _shifted(glu_buf, r0, R_CHUNK, c0, LANES, offs_a)
                ac_ref[pl.ds(r0, R_CHUNK), c0:c0 + LANES] = _taps(ca_ref, sh, offs_a, c0, LANES)
                sh = _shifted(cv_buf, r0, R_CHUNK, c0, LANES, offs_s)
                bconv_buf[pl.ds(r0, R_CHUNK), c0:c0 + LANES] = _taps(cb_ref, sh, offs_s, c0, LANES)
            return carry

        lax.fori_loop(0, ts // R_CHUNK, chunk, 0)
        glu_buf[0:HALO_A, :] = glu_buf[ts:ts + HALO_A, :]
        cv_buf[0:HALO_S, :] = cv_buf[ts:ts + HALO_S, :]

        ac = ac_ref[...]
        xc = ac - jnp.mean(ac, axis=-1, keepdims=True)
        xn = xc * lax.rsqrt(jnp.mean(xc * xc, axis=-1, keepdims=True) + LN_EPS)
        ln = xn * lg_ref[...] + lb_ref[...]
        cat_ref[:, 0:A] = (ln * _sigmoid(ln)).astype(BF16)
        cat_ref[:, A:2 * A] = (z_ref[:, 2 * A:3 * A] * bconv_buf[...]).astype(BF16)
        x1_ref[...] = xv + jnp.dot(cat_ref[...], wout_ref[...], preferred_element_type=F32)

    tile = lambda w: pl.BlockSpec((ts, w), lambda i: (i, 0))
    return pl.pallas_call(
        body, name="mix0_fwd", grid=(n,),
        in_specs=[tile(D), _const((1, D)), _const((NCHIP, D, bw)), _const((K_A, A)), _const((1, A)), _const((1, A)),
                  _const((K_S, A)), _const((2 * A, D))],
        out_specs=[tile(D), tile(NZ), tile(A), tile(2 * A), tile(D)],
        out_shape=[jax.ShapeDtypeStruct((s, D), BF16), jax.ShapeDtypeStruct((s, NZ), F32), jax.ShapeDtypeStruct((s, A), F32),
                   jax.ShapeDtypeStruct((s, 2 * A), BF16), jax.ShapeDtypeStruct((s, D), F32)],
        scratch_shapes=[pltpu.VMEM((HALO_A + ts, A), F32), pltpu.VMEM((HALO_S + ts, A), F32), pltpu.VMEM((ts, A), F32)],
        compiler_params=_cparams(1),
    )(x, g, w_in, conv_a, ln_g, ln_b, conv_b, w_out)


def _ffn_fwd(x, g, w_up, wc, w_down, name):
    s = x.shape[0]
    ts = min(TS_FFN, s)
    n = s // ts
    bw = FF2 // NCHIP
    lw = 2 * LANES
    rows = 32
    offs = [HALO_S - (K_S - 1) + k for k in range(K_S)]

    def body(x_ref, g_ref, wup_ref, wc_ref, wdn_ref, h_ref, u0_ref, act_ref, xo_ref, cbuf):
        i = pl.program_id(0)

        @pl.when(i == 0)
        def _():
            cbuf[0:HALO_S, :] = jnp.zeros((HALO_S, FF2), F32)

        xv = x_ref[...]
        h = (xv * _rsqrt_mean_sq(xv) * g_ref[...]).astype(BF16)
        h_ref[...] = h
        for j in range(NCHIP):
            zc = jnp.dot(h, wup_ref[j], preferred_element_type=F32)
            u0_ref[:, j * bw:(j + 1) * bw] = zc
            cbuf[HALO_S:HALO_S + ts, j * bw:(j + 1) * bw] = zc

        def chunk(ci, carry):
            r0 = pl.multiple_of(ci * rows, rows)
            for c0 in range(0, FF, lw):
                ug = _taps(wc_ref, _shifted(cbuf, r0, rows, c0, lw, offs), offs, c0, lw)
                uv = _taps(wc_ref, _shifted(cbuf, r0, rows, FF + c0, lw, offs), offs, FF + c0, lw)
                act_ref[pl.ds(r0, rows), c0:c0 + lw] = (ug * _sigmoid(ug) * uv).astype(BF16)
            return carry

        lax.fori_loop(0, ts // rows, chunk, 0)
        cbuf[0:HALO_S, :] = cbuf[ts:ts + HALO_S, :]
        xo_ref[...] = xv + jnp.dot(act_ref[...], wdn_ref[...], preferred_element_type=F32)

    tile = lambda w: pl.BlockSpec((ts, w), lambda i: (i, 0))
    return pl.pallas_call(
        body, name=name, grid=(n,),
        in_specs=[tile(D), _const((1, D)), _const((NCHIP, D, bw)), _const((K_S, FF2)), _const((FF, D))],
        out_specs=[tile(D), tile(FF2), tile(FF), tile(D)],
        out_shape=[jax.ShapeDtypeStruct((s, D), BF16), jax.ShapeDtypeStruct((s, FF2), F32),
                   jax.ShapeDtypeStruct((s, FF), BF16), jax.ShapeDtypeStruct((s, D), F32)],
        scratch_shapes=[pltpu.VMEM((HALO_S + ts, FF2), F32)],
        compiler_params=_cparams(1),
    )(x, g, w_up, wc, w_down)


def _pool_windows(hbuf, pbuf, tile_row0, ts):
    def chunk(ci, carry):
        r0 = pl.multiple_of(ci * R_CHUNK, R_CHUNK)
        t1 = (tile_row0 + r0 + lax.broadcasted_iota(jnp.int32, (R_CHUNK, 1), 0) + 1).astype(F32)
        for gi, w in enumerate(POOL_WINDOWS):
            cnt = jnp.minimum(t1, float(w))
            offs = [HALO_P - jj for jj in range(w)]
            for c0 in range(gi * PG, (gi + 1) * PG, LANES):
                sh = _shifted(hbuf, r0, R_CHUNK, c0, LANES, offs)
                tot = sh[offs[0]]
                for o in offs[1:]:
                    tot = tot + sh[o]
                pbuf[pl.ds(r0, R_CHUNK), c0:c0 + LANES] = (tot / cnt - sh[HALO_P]).astype(BF16)
        return carry

    lax.fori_loop(0, ts // R_CHUNK, chunk, 0)


def _assemble_wpool(wp_ref, wps):
    rb = PG // NCHIP
    for gi in range(len(POOL_WINDOWS)):
        for j in range(NCHIP):
            wps[gi, j * rb:(j + 1) * rb, :] = wp_ref[j, gi]


def _pool_fwd(x, g, w_pool, scale):
    s = x.shape[0]
    ts = min(TS_POOL, s)
    n = s // ts
    ng = len(POOL_WINDOWS)

    def body(x_ref, g_ref, wp_ref, sc_ref, xo_ref, hbuf, pbuf, wps):
        i = pl.program_id(0)

        @pl.when(i == 0)
        def _():
            hbuf[0:HALO_P, :] = jnp.zeros((HALO_P, D), F32)
            _assemble_wpool(wp_ref, wps)

        xv = x_ref[...]
        hbuf[HALO_P:HALO_P + ts, :] = xv * _rsqrt_mean_sq(xv) * g_ref[...]
        _pool_windows(hbuf, pbuf, i * ts, ts)
        hbuf[0:HALO_P, :] = hbuf[ts:ts + HALO_P, :]
        for gi in range(ng):
            cols = slice(gi * PG, (gi + 1) * PG)
            y = jnp.dot(pbuf[:, cols], wps[gi], preferred_element_type=F32)
            xo_ref[:, cols] = xv[:, cols] + y * sc_ref[:, cols]

    tile = pl.BlockSpec((ts, D), lambda i: (i, 0))
    return pl.pallas_call(
        body, name="pool_fwd", grid=(n,),
        in_specs=[tile, _const((1, D)), _const((NCHIP, ng, PG // NCHIP, PG)), _const((1, D))],
        out_specs=tile, out_shape=jax.ShapeDtypeStruct((s, D), F32),
        scratch_shapes=[pltpu.VMEM((HALO_P + ts, D), F32), pltpu.VMEM((ts, D), BF16), pltpu.VMEM((ng, PG, PG), BF16)],
        compiler_params=_cparams(1),
    )(x, g, w_pool, scale)


def _final_loss(x, g, target):
    s = x.shape[0]
    ts = min(TS_MM, s)
    n = s // ts

    def body(x_ref, g_ref, t_ref, dx_ref, dg_ref, loss_ref):
        i = pl.program_id(0)

        @pl.when(i == 0)
        def _():
            dg_ref[...] = jnp.zeros((1, D), F32)
            loss_ref[...] = jnp.zeros((1, LANES), F32)

        xv = x_ref[...]
        r = _rsqrt_mean_sq(xv)
        xh = xv * r
        gv = g_ref[...]
        err = xh * gv - t_ref[...]
        sq = jnp.sum(jnp.sum(err * err, axis=1, keepdims=True), axis=0, keepdims=True)
        loss_ref[...] += sq * (0.5 / D)
        dy = err * (1.0 / D)
        dg_ref[...] += jnp.sum(dy * xh, axis=0, keepdims=True)
        dx_ref[...] = _rms_bwd(dy, xh, r, gv)

    tile = pl.BlockSpec((ts, D), lambda i: (i, 0))
    return pl.pallas_call(
        body, name="final_loss", grid=(n,),
        in_specs=[tile, _const((1, D)), tile],
        out_specs=[tile, pl.BlockSpec((1, D), lambda i: (0, 0)), pl.BlockSpec((1, LANES), lambda i: (0, 0))],
        out_shape=[jax.ShapeDtypeStruct((s, D), F32), jax.ShapeDtypeStruct((1, D), F32), jax.ShapeDtypeStruct((1, LANES), F32)],
        compiler_params=_cparams(1),
    )(x, g, target)


def _ffn_bwd_a(dxo, u0, wc, w_down, name):
    s = dxo.shape[0]
    ts = min(TS_FFN, s)
    n = s // ts
    cw = FF2 // NCHIP
    rows = 32
    lw2 = 2 * LANES
    offs = [HALO_S - (K_S - 1) + k for k in range(K_S)]
    boffs = [K_S - 1 - k for k in range(K_S)]

    def body(dxo_ref, u0_ref, halo_ref, wc_ref, wdn_ref, du0_ref, dwc_ref, ubuf, dubuf, dact, dwacc):
        i = pl.program_id(0)
        j = n - 1 - i

        @pl.when(i == 0)
        def _():
            dubuf[ts:ts + HALO_S, :] = jnp.zeros((HALO_S, FF2), F32)
            dwacc[...] = jnp.zeros(dwacc.shape, F32)

        ubuf[0:HALO_S, :] = jnp.where(j == 0, 0.0, halo_ref[...])
        ubuf[HALO_S:HALO_S + ts, :] = u0_ref[...]
        df = dxo_ref[...].astype(BF16)
        for cg in range(0, FF, cw):
            dact[...] = lax.dot_general(df, wdn_ref[cg:cg + cw, :], NT_DIMS, preferred_element_type=F32)

            def chunk(ci, carry, cg=cg):
                r0 = pl.multiple_of(ci * rows, rows)
                rs = pl.ds(r0, rows)
                for c in range(cg, cg + cw, LANES):
                    shg = _shifted(ubuf, r0, rows, c, LANES, offs)
                    shv = _shifted(ubuf, r0, rows, FF + c, LANES, offs)
                    ug = _taps(wc_ref, shg, offs, c, LANES)
                    uv = _taps(wc_ref, shv, offs, FF + c, LANES)
                    sg = _sigmoid(ug)
                    da = dact[rs, c - cg:c - cg + LANES]
                    dgv = da * uv * sg * (1.0 + ug * (1.0 - sg))
                    dvv = da * ug * sg
                    dubuf[rs, c:c + LANES] = dgv
                    dubuf[rs, FF + c:FF + c + LANES] = dvv
                    for k, o in enumerate(offs):
                        ks = slice(SUBLANES * k, SUBLANES * (k + 1))
                        dwacc[ks, c:c + LANES] += _rowsum8(dgv * shg[o])
                        dwacc[ks, FF + c:FF + c + LANES] += _rowsum8(dvv * shv[o])
                return carry

            lax.fori_loop(0, ts // rows, chunk, 0)

        def chunk2(ci, carry):
            r0 = pl.multiple_of(ci * rows, rows)
            for c in range(0, FF2, lw2):
                sh = _shifted(dubuf, r0, rows, c, lw2, boffs)
                du0_ref[pl.ds(r0, rows), c:c + lw2] = _taps(wc_ref, sh, boffs, c, lw2).astype(BF16)
            return carry

        lax.fori_loop(0, ts // rows, chunk2, 0)
        dubuf[ts:ts + HALO_S, :] = dubuf[0:HALO_S, :]

        @pl.when(i == n - 1)
        def _():
            _finish_tap_sums(dwacc, dwc_ref, K_S)

    rev = lambda w: pl.BlockSpec((ts, w), lambda i: (n - 1 - i, 0))
    halo = pl.BlockSpec((HALO_S, FF2), lambda i: (jnp.maximum((n - 1 - i) * (ts // HALO_S) - 1, 0), 0))
    return pl.pallas_call(
        body, name=name, grid=(n,),
        in_specs=[rev(D), rev(FF2), halo, _const((K_S, FF2)), _const((FF, D))],
        out_specs=[rev(FF2), pl.BlockSpec((K_S, FF2), lambda i: (0, 0))],
        out_shape=[jax.ShapeDtypeStruct((s, FF2), BF16), jax.ShapeDtypeStruct((K_S, FF2), F32)],
        scratch_shapes=[pltpu.VMEM((HALO_S + ts, FF2), F32), pltpu.VMEM((ts + HALO_S, FF2), F32), pltpu.VMEM((ts, cw), F32),
                        pltpu.VMEM((SUBLANES * K_S, FF2), F32)],
        compiler_params=_cparams(1),
    )(dxo, u0, u0, wc, w_down)


def _nt_rms_bwd(dy, w, x, g, dres, name):
    s = x.shape[0]
    ts = min(TS_MM, s)
    n = s // ts
    nw = dy.shape[1]
    bw = nw // NCHIP

    def body(dy_ref, w_ref, x_ref, g_ref, dres_ref, dx_ref, dg_ref):
        i = pl.program_id(0)

        @pl.when(i == 0)
        def _():
            dg_ref[...] = jnp.zeros((1, D), F32)

        dh = lax.dot_general(dy_ref[:, 0:bw], w_ref[0], NT_DIMS, preferred_element_type=F32)
        for j in range(1, NCHIP):
            dh = dh + lax.dot_general(dy_ref[:, j * bw:(j + 1) * bw], w_ref[j], NT_DIMS, preferred_element_type=F32)
        xv = x_ref[...]
        r = _rsqrt_mean_sq(xv)
        xh = xv * r
        dg_ref[...] += jnp.sum(dh * xh, axis=0, keepdims=True)
        dx_ref[...] = dres_ref[...] + _rms_bwd(dh, xh, r, g_ref[...])

    tile = lambda wd: pl.BlockSpec((ts, wd), lambda i: (i, 0))
    return pl.pallas_call(
        body, name=name, grid=(n,),
        in_specs=[tile(nw), _const((NCHIP, D, bw)), tile(D), _const((1, D)), tile(D)],
        out_specs=[tile(D), pl.BlockSpec((1, D), lambda i: (0, 0))],
        out_shape=[jax.ShapeDtypeStruct((s, D), F32), jax.ShapeDtypeStruct((1, D), F32)],
        compiler_params=_cparams(1),
    )(dy, w, x, g, dres)


def _wgrad(a, b, bn, out_shape, out_block, out_index, name, prev=None):
    s, m = a.shape
    nb = b.shape[1] // bn
    tk = min(TS_MM, s)
    nk = s // tk

    def body(*refs):
        a_ref, b_ref, o_ref = refs[0], refs[1], refs[-1]
        k = pl.program_id(1)

        @pl.when(k == 0)
        def _():
            o_ref[...] = jnp.zeros(o_ref.shape, F32)

        o_ref[...] += lax.dot_general(a_ref[...], b_ref[...].astype(BF16), TN_DIMS, preferred_element_type=F32)

    in_specs = [pl.BlockSpec((tk, m), lambda jn, k: (k, 0)), pl.BlockSpec((tk, bn), lambda jn, k: (k, jn))]
    args = [a, b]
    aliases = {}
    if prev is not None:
        in_specs.append(ANY)
        args.append(prev)
        aliases = {2: 0}
    return pl.pallas_call(
        body, name=name, grid=(nb, nk), in_specs=in_specs,
        out_specs=pl.BlockSpec(out_block, lambda jn, k: out_index(jn)),
        out_shape=jax.ShapeDtypeStruct(out_shape, F32), input_output_aliases=aliases,
        compiler_params=_cparams(2),
    )(*args)


def _pool_bwd(dxo, x, g, w_pool, scale):
    s = x.shape[0]
    ts = min(TS_POOL, s)
    n = s // ts
    ng = len(POOL_WINDOWS)
    rb = PG // NCHIP

    def body(dxo_ref, x_ref, halo_ref, g_ref, wp_ref, sc_ref, dx_ref, dwp_ref, dsc_ref, dg_ref,
             hbuf, pbuf, qbuf, dhbuf, wps, dwacc):
        i = pl.program_id(0)
        j = n - 1 - i

        @pl.when(i == 0)
        def _():
            qbuf[ts:ts + HALO_P, :] = jnp.zeros((HALO_P, D), F32)
            dwacc[...] = jnp.zeros(dwacc.shape, F32)
            dsc_ref[...] = jnp.zeros((1, D), F32)
            dg_ref[...] = jnp.zeros((1, D), F32)
            _assemble_wpool(wp_ref, wps)

        gv = g_ref[...]
        xl = halo_ref[...]
        hbuf[0:HALO_P, :] = jnp.where(j == 0, 0.0, xl * _rsqrt_mean_sq(xl) * gv)
        xv = x_ref[...]
        r = _rsqrt_mean_sq(xv)
        xh = xv * r
        hbuf[HALO_P:HALO_P + ts, :] = xh * gv
        _pool_windows(hbuf, pbuf, j * ts, ts)

        dy = dxo_ref[...]
        t1 = (j * ts + lax.broadcasted_iota(jnp.int32, (ts, 1), 0) + 1).astype(F32)
        for gi, w in enumerate(POOL_WINDOWS):
            cols = slice(gi * PG, (gi + 1) * PG)
            p = pbuf[:, cols]
            y = jnp.dot(p, wps[gi], preferred_element_type=F32)
            dsc_ref[:, cols] += jnp.sum(dy[:, cols] * y, axis=0, keepdims=True)
            dq = (dy[:, cols] * sc_ref[:, cols]).astype(BF16)
            dwacc[gi] += lax.dot_general(p, dq, TN_DIMS, preferred_element_type=F32)
            dp = lax.dot_general(dq, wps[gi], NT_DIMS, preferred_element_type=F32)
            qbuf[0:ts, cols] = dp / jnp.minimum(t1, float(w))

        def chunk(ci, carry):
            r0 = pl.multiple_of(ci * R_CHUNK, R_CHUNK)
            tc = (j * ts + r0 + lax.broadcasted_iota(jnp.int32, (R_CHUNK, 1), 0) + 1).astype(F32)
            for gi, w in enumerate(POOL_WINDOWS):
                cnt = jnp.minimum(tc, float(w))
                offs = list(range(w))
                for c0 in range(gi * PG, (gi + 1) * PG, LANES):
                    sh = _shifted(qbuf, r0, R_CHUNK, c0, LANES, offs)
                    tot = sh[0]
                    for o in offs[1:]:
                        tot = tot + sh[o]
                    dhbuf[pl.ds(r0, R_CHUNK), c0:c0 + LANES] = tot - sh[0] * cnt
            return carry

        lax.fori_loop(0, ts // R_CHUNK, chunk, 0)
        qbuf[ts:ts + HALO_P, :] = qbuf[0:HALO_P, :]
        dh = dhbuf[...]
        dg_ref[...] += jnp.sum(dh * xh, axis=0, keepdims=True)
        dx_ref[...] = dy + _rms_bwd(dh, xh, r, gv)

        @pl.when(i == n - 1)
        def _():
            for gi in range(ng):
                for jj in range(NCHIP):
                    dwp_ref[jj, gi] = dwacc[gi, jj * rb:(jj + 1) * rb, :]

    rev = pl.BlockSpec((ts, D), lambda i: (n - 1 - i, 0))
    halo = pl.BlockSpec((HALO_P, D), lambda i: (jnp.maximum((n - 1 - i) * (ts // HALO_P) - 1, 0), 0))
    vec = pl.BlockSpec((1, D), lambda i: (0, 0))
    return pl.pallas_call(
        body, name="pool_bwd", grid=(n,),
        in_specs=[rev, rev, halo, _const((1, D)), _const((NCHIP, ng, rb, PG)), _const((1, D))],
        out_specs=[rev, pl.BlockSpec((NCHIP, ng, rb, PG), lambda i: (0, 0, 0, 0)), vec, vec],
        out_shape=[jax.ShapeDtypeStruct((s, D), F32), jax.ShapeDtypeStruct((NCHIP, ng, rb, PG), F32),
                   jax.ShapeDtypeStruct((1, D), F32), jax.ShapeDtypeStruct((1, D), F32)],
        scratch_shapes=[pltpu.VMEM((HALO_P + ts, D), F32), pltpu.VMEM((ts, D), BF16), pltpu.VMEM((ts + HALO_P, D), F32),
                        pltpu.VMEM((ts, D), F32), pltpu.VMEM((ng, PG, PG), BF16), pltpu.VMEM((ng, PG, PG), F32)],
        compiler_params=_cparams(1),
    )(dxo, x, x, g, w_pool, scale)


def _mix0_bwd_a(dx1, z, ac, w_out, conv_a, ln_g, ln_b, conv_b):
    s = dx1.shape[0]
    ts = min(TS_MIXB, s)
    n = s // ts
    rows = 32
    offs_a = [HALO_A - (K_A - 1) + k for k in range(K_A)]
    offs_s = [HALO_S - (K_S - 1) + k for k in range(K_S)]
    boffs_a = [K_A - 1 - k for k in range(K_A)]
    boffs_s = [K_S - 1 - k for k in range(K_S)]

    def body(dx1_ref, z_ref, zh_ref, ac_ref, wout_ref, ca_ref, lg_ref, lb_ref, cb_ref,
             dz_ref, dca_ref, dlg_ref, dlb_ref, dcb_ref,
             glu_buf, cv_buf, dac_buf, dbc_buf, db_buf, dca_acc, dcb_acc):
        i = pl.program_id(0)
        j = n - 1 - i

        @pl.when(i == 0)
        def _():
            dac_buf[ts:ts + HALO_A, :] = jnp.zeros((HALO_A, A), F32)
            dbc_buf[ts:ts + HALO_S, :] = jnp.zeros((HALO_S, A), F32)
            dca_acc[...] = jnp.zeros(dca_acc.shape, F32)
            dcb_acc[...] = jnp.zeros(dcb_acc.shape, F32)
            dlg_ref[...] = jnp.zeros((1, A), F32)
            dlb_ref[...] = jnp.zeros((1, A), F32)

        glu_h = zh_ref[:, 0:A] * _sigmoid(zh_ref[:, A:2 * A])
        glu_buf[0:HALO_A, :] = jnp.where(j == 0, 0.0, glu_h)
        hs = slice(HALO_A - HALO_S, HALO_A)
        cv_buf[0:HALO_S, :] = jnp.where(j == 0, 0.0, zh_ref[hs, 3 * A:4 * A] * zh_ref[hs, 4 * A:5 * A])
        glu_buf[HALO_A:HALO_A + ts, :] = z_ref[:, 0:A] * _sigmoid(z_ref[:, A:2 * A])
        cv_buf[HALO_S:HALO_S + ts, :] = z_ref[:, 3 * A:4 * A] * z_ref[:, 4 * A:5 * A]

        dcat = lax.dot_general(dx1_ref[...].astype(BF16), wout_ref[...], NT_DIMS, preferred_element_type=F32)
        db_buf[...] = dcat[:, A:2 * A]
        ac = ac_ref[...]
        xc = ac - jnp.mean(ac, axis=-1, keepdims=True)
        rstd = lax.rsqrt(jnp.mean(xc * xc, axis=-1, keepdims=True) + LN_EPS)
        xn = xc * rstd
        lg = lg_ref[...]
        ln = xn * lg + lb_ref[...]
        sl = _sigmoid(ln)
        dln = dcat[:, 0:A] * sl * (1.0 + ln * (1.0 - sl))
        dlg_ref[...] += jnp.sum(dln * xn, axis=0, keepdims=True)
        dlb_ref[...] += jnp.sum(dln, axis=0, keepdims=True)
        dxn = dln * lg
        dac_buf[0:ts, :] = rstd * (dxn - jnp.mean(dxn, axis=-1, keepdims=True)
                                   - xn * jnp.mean(dxn * xn, axis=-1, keepdims=True))

        def chunk(ci, carry):
            r0 = pl.multiple_of(ci * rows, rows)
            rs = pl.ds(r0, rows)
            for c0 in range(0, A, LANES):
                cs = slice(c0, c0 + LANES)
                shc = _shifted(cv_buf, r0, rows, c0, LANES, offs_s)
                bconv = _taps(cb_ref, shc, offs_s, c0, LANES)
                dbv = db_buf[rs, cs]
                dz_ref[rs, 2 * A + c0:2 * A + c0 + LANES] = (dbv * bconv).astype(BF16)
                dbconv = dbv * z_ref[rs, 2 * A + c0:2 * A + c0 + LANES]
                dbc_buf[rs, cs] = dbconv
                for k, o in enumerate(offs_s):
                    dcb_acc[SUBLANES * k:SUBLANES * (k + 1), cs] += _rowsum8(dbconv * shc[o])
                shg = _shifted(glu_buf, r0, rows, c0, LANES, offs_a)
                dacv = dac_buf[rs, cs]
                for k, o in enumerate(offs_a):
                    dca_acc[SUBLANES * k:SUBLANES * (k + 1), cs] += _rowsum8(dacv * shg[o])
            return carry

        lax.fori_loop(0, ts // rows, chunk, 0)

        def chunk2(ci, carry):
            r0 = pl.multiple_of(ci * rows, rows)
            rs = pl.ds(r0, rows)
            for c0 in range(0, A, LANES):
                col = lambda grp: slice(grp * A + c0, grp * A + c0 + LANES)
                dglu = _taps(ca_ref, _shifted(dac_buf, r0, rows, c0, LANES, boffs_a), boffs_a, c0, LANES)
                sg = _sigmoid(z_ref[rs, col(1)])
                dz_ref[rs, col(0)] = (dglu * sg).astype(BF16)
                dz_ref[rs, col(1)] = (dglu * z_ref[rs, col(0)] * sg * (1.0 - sg)).astype(BF16)
                dcv = _taps(cb_ref, _shifted(dbc_buf, r0, rows, c0, LANES, boffs_s), boffs_s, c0, LANES)
                dz_ref[rs, col(3)] = (dcv * z_ref[rs, col(4)]).astype(BF16)
                dz_ref[rs, col(4)] = (dcv * z_ref[rs, col(3)]).astype(BF16)
            return carry

        lax.fori_loop(0, ts // rows, chunk2, 0)
        dac_buf[ts:ts + HALO_A, :] = dac_buf[0:HALO_A, :]
        dbc_buf[ts:ts + HALO_S, :] = dbc_buf[0:HALO_S, :]

        @pl.when(i == n - 1)
        def _():
            _finish_tap_sums(dca_acc, dca_ref, K_A)
            _finish_tap_sums(dcb_acc, dcb_ref, K_S)

    rev = lambda w: pl.BlockSpec((ts, w), lambda i: (n - 1 - i, 0))
    halo = pl.BlockSpec((HALO_A, NZ), lambda i: (jnp.maximum((n - 1 - i) * (ts // HALO_A) - 1, 0), 0))
    full = lambda r, c: pl.BlockSpec((r, c), lambda i: (0, 0))
    return pl.pallas_call(
        body, name="mix0_bwd_a", grid=(n,),
        in_specs=[rev(D), rev(NZ), halo, rev(A), _const((2 * A, D)), _const((K_A, A)), _const((1, A)), _const((1, A)),
                  _const((K_S, A))],
        out_specs=[rev(NZ), full(K_A, A), full(1, A), full(1, A), full(K_S, A)],
        out_shape=[jax.ShapeDtypeStruct((s, NZ), BF16), jax.ShapeDtypeStruct((K_A, A), F32), jax.ShapeDtypeStruct((1, A), F32),
                   jax.ShapeDtypeStruct((1, A), F32), jax.ShapeDtypeStruct((K_S, A), F32)],
        scratch_shapes=[pltpu.VMEM((HALO_A + ts, A), F32), pltpu.VMEM((HALO_S + ts, A), F32), pltpu.VMEM((ts + HALO_A, A), F32),
                        pltpu.VMEM((ts + HALO_S, A), F32), pltpu.VMEM((ts, A), F32), pltpu.VMEM((SUBLANES * K_A, A), F32),
                        pltpu.VMEM((SUBLANES * K_S, A), F32)],
        compiler_params=_cparams(1),
    )(dx1, z, z, ac, w_out, conv_a, ln_g, ln_b, conv_b)


def _local_step(x, target, wts):
    h0, z, ac, cat, x1 = _mix0_fwd(x, wts["norm_mix_even"], wts["w_in"], wts["conv_a"], wts["ln_a_g"], wts["ln_a_b"],
                                   wts["conv_b"], wts["w_out"])
    hf0, u00, act0, x2 = _ffn_fwd(x1, wts["norm_ffn"][0:1], wts["w_up"][0], wts["conv_ffn_w"][0], wts["w_down"][0], "ffn0_fwd")
    x3 = _pool_fwd(x2, wts["norm_mix_odd"], wts["w_pool"], wts["pool_scale"])
    hf1, u01, act1, x4 = _ffn_fwd(x3, wts["norm_ffn"][1:2], wts["w_up"][1], wts["conv_ffn_w"][1], wts["w_down"][1], "ffn1_fwd")
    dx4, g_norm_final, loss = _final_loss(x4, wts["norm_final"], target)

    bw_up = FF2 // NCHIP
    up_shape, up_block = (2, NCHIP, D, bw_up), (None, None, D, bw_up)
    dn_shape, dn_block = (2, FF, D), (None, FF, D // 2)

    du01, g_wc1 = _ffn_bwd_a(dx4, u01, wts["conv_ffn_w"][1], wts["w_down"][1], "ffn1_bwd_a")
    dx3, g_nf1 = _nt_rms_bwd(du01, wts["w_up"][1], x3, wts["norm_ffn"][1:2], dx4, "ffn1_bwd_b")
    dx2, g_wpool, g_scale, g_nmo = _pool_bwd(dx3, x2, wts["norm_mix_odd"], wts["w_pool"], wts["pool_scale"])
    du00, g_wc0 = _ffn_bwd_a(dx2, u00, wts["conv_ffn_w"][0], wts["w_down"][0], "ffn0_bwd_a")
    dx1, g_nf0 = _nt_rms_bwd(du00, wts["w_up"][0], x1, wts["norm_ffn"][0:1], dx2, "ffn0_bwd_b")
    dz, g_ca, g_lg, g_lb, g_cb = _mix0_bwd_a(dx1, z, ac, wts["w_out"], wts["conv_a"], wts["ln_a_g"], wts["ln_a_b"], wts["conv_b"])
    grad_x, g_nme = _nt_rms_bwd(dz, wts["w_in"], x, wts["norm_mix_even"], dx1, "mix0_bwd_b")

    g_up = _wgrad(hf1, du01, bw_up, up_shape, up_block, lambda jn: (1, jn, 0, 0), "wgrad_up1")
    g_up = _wgrad(hf0, du00, bw_up, up_shape, up_block, lambda jn: (0, jn, 0, 0), "wgrad_up0", prev=g_up)
    g_dn = _wgrad(act1, dx4, D // 2, dn_shape, dn_block, lambda jn: (1, 0, jn), "wgrad_down1")
    g_dn = _wgrad(act0, dx2, D // 2, dn_shape, dn_block, lambda jn: (0, 0, jn), "wgrad_down0", prev=g_dn)
    g_out = _wgrad(cat, dx1, D, (1, D, D), (None, D, D), lambda jn: (0, 0, 0), "wgrad_out")
    g_in = _wgrad(h0, dz, NZ // NCHIP, (1, NCHIP, D, NZ // NCHIP), (None, None, D, NZ // NCHIP), lambda jn: (0, jn, 0, 0),
                  "wgrad_in")

    big = {"w_in": g_in, "w_out": g_out, "w_pool": g_wpool[None], "w_up": g_up, "w_down": g_dn}
    small = {"norm_mix_even": g_nme, "conv_a": g_ca, "ln_a_g": g_lg, "ln_a_b": g_lb, "conv_b": g_cb, "norm_mix_odd": g_nmo,
             "pool_scale": g_scale, "norm_ffn": jnp.concatenate([g_nf0, g_nf1], axis=0),
             "conv_ffn_w": jnp.stack([g_wc0, g_wc1]), "norm_final": g_norm_final}
    return loss, grad_x, big, small


def _adamw_math(w, g, m, v):
    m = ADAM_B1 * m + (1.0 - ADAM_B1) * g
    v = ADAM_B2 * v + (1.0 - ADAM_B2) * (g * g)
    m_hat = m / (1.0 - ADAM_B1 ** ADAM_STEP)
    v_hat = v / (1.0 - ADAM_B2 ** ADAM_STEP)
    return -ADAM_LR * (m_hat / (jnp.sqrt(v_hat) + ADAM_EPS) + ADAM_WD * w), m, v


def _adamw_big(w, g, m, v, tr, name):
    nl, rows, cols = w.shape

    def body(w_ref, g_ref, m_ref, v_ref, d_ref, m2_ref, v2_ref):
        d_ref[...], m2_ref[...], v2_ref[...] = _adamw_math(w_ref[...], g_ref[...], m_ref[...], v_ref[...])

    spec = pl.BlockSpec((None, tr, cols), lambda l, r: (l, r, 0))
    return pl.pallas_call(
        body, name=name, grid=(nl, rows // tr), in_specs=[spec] * 4, out_specs=[spec] * 3,
        out_shape=[jax.ShapeDtypeStruct(w.shape, F32)] * 3, compiler_params=_cparams(2),
    )(w, g, m, v)


def _adamw_small(ws, gs, ms, vs):
    n = len(ws)

    def body(*refs):
        for p in range(n):
            w_ref, g_ref, m_ref, v_ref = (refs[q * n + p] for q in range(4))
            d_ref, m2_ref, v2_ref = (refs[(4 + q) * n + p] for q in range(3))
            d_ref[...], m2_ref[...], v2_ref[...] = _adamw_math(w_ref[...], g_ref[...], m_ref[...], v_ref[...])

    whole = lambda a: pl.BlockSpec(a.shape, lambda: (0,) * a.ndim)
    outs = pl.pallas_call(
        body, name="adamw_small", in_specs=[whole(a) for a in ws] * 4, out_specs=[whole(a) for a in ws] * 3,
        out_shape=[jax.ShapeDtypeStruct(a.shape, F32) for a in ws] * 3,
        compiler_params=pltpu.CompilerParams(vmem_limit_bytes=VMEM_LIMIT_BYTES),
    )(*ws, *gs, *ms, *vs)
    return outs[0:n], outs[n:2 * n], outs[2 * n:3 * n]


def _place():
    x, y, c = lax.axis_index("x"), lax.axis_index("y"), lax.axis_index("c")
    chips = [(x, 1 - y), (1 - x, y), (1 - x, 1 - y)]
    blocks = [2 * cx + cy for cx, cy in chips]
    return x, y, c, 2 * x + y, chips, blocks


def _ag_weights(shards, small):
    na = len(shards)
    nk = NCHIP - 1

    def body(*refs):
        ins, sm_in = refs[:na], refs[na]
        outs, sm_out = refs[na + 1:2 * na + 1], refs[2 * na + 1]
        ici_send, ici_recv, fwd_send, fwd_recv, loc = refs[2 * na + 2:]
        x, y, c, j, chips, blocks = _place()
        sib = (x, y, 1 - c)

        local = [pltpu.make_async_copy(ins[a], outs[a].at[j], loc.at[a]) for a in range(na)]
        local.append(pltpu.make_async_copy(sm_in, sm_out.at[j], loc.at[na]))
        for cp in local:
            cp.start()

        def ici(a, k, src):
            dst = outs[a].at[j, c] if src is not None else outs[a].at[blocks[k], c]
            return pltpu.make_async_remote_copy(
                src_ref=dst if src is None else src, dst_ref=dst, send_sem=ici_send.at[a * nk + k],
                recv_sem=ici_recv.at[a * nk + k], device_id=(*chips[k], c), device_id_type=MESH)

        def small_copy(k):
            return pltpu.make_async_remote_copy(
                src_ref=sm_in, dst_ref=sm_out.at[j], send_sem=ici_send.at[na * nk + k], recv_sem=ici_recv.at[na * nk + k],
                device_id=(*chips[k], c), device_id_type=MESH)

        def fwd(a, k, half):
            ref = outs[a].at[blocks[k], half]
            return pltpu.make_async_remote_copy(
                src_ref=ref, dst_ref=ref, send_sem=fwd_send.at[a * nk + k], recv_sem=fwd_recv.at[a * nk + k],
                device_id=sib, device_id_type=MESH)

        sends = [ici(a, k, ins[a].at[c]) for a in range(na) for k in range(nk)] + [small_copy(k) for k in range(nk)]
        for cp in sends:
            cp.start()
        passed = []
        for a in range(na):
            for k in range(nk):
                ici(a, k, None).wait_recv()
                passed.append(fwd(a, k, c))
                passed[-1].start()
        for k in range(nk):
            small_copy(k).wait_recv()
        for a in range(na):
            for k in range(nk):
                fwd(a, k, 1 - c).wait_recv()
        for cp in sends + passed:
            cp.wait_send()
        for cp in local:
            cp.wait()

    out_shape = [jax.ShapeDtypeStruct((NCHIP,) + s.shape, s.dtype) for s in shards]
    out_shape.append(jax.ShapeDtypeStruct((NCHIP,) + small.shape, small.dtype))
    outs = pl.pallas_call(
        body, name="ag_weights", in_specs=[ANY] * (na + 1), out_specs=[ANY] * (na + 1), out_shape=out_shape,
        scratch_shapes=[pltpu.SemaphoreType.DMA((na * nk + nk,)), pltpu.SemaphoreType.DMA((na * nk + nk,)),
                        pltpu.SemaphoreType.DMA((na * nk,)), pltpu.SemaphoreType.DMA((na * nk,)),
                        pltpu.SemaphoreType.DMA((na + 1,))],
    )(*shards, small)
    return outs[:na], outs[na]


def _rs_pair(grads):
    items = [(a, l) for a, g in enumerate(grads) for l in range(g.shape[0])]
    na = len(grads)

    def body(*refs):
        ins, outs, send_sem, recv_sem = refs[:na], refs[na:2 * na], refs[2 * na], refs[2 * na + 1]
        x, y, c, _, _, _ = _place()
        copies = [pltpu.make_async_remote_copy(
            src_ref=ins[a].at[l, :, 1 - c], dst_ref=outs[a].at[l], send_sem=send_sem.at[q], recv_sem=recv_sem.at[q],
            device_id=(x, y, 1 - c), device_id_type=MESH) for q, (a, l) in enumerate(items)]
        for cp in copies:
            cp.start()
        for cp in copies:
            cp.wait()

    return pl.pallas_call(
        body, name="rs_pair", in_specs=[ANY] * na, out_specs=[ANY] * na,
        out_shape=[jax.ShapeDtypeStruct(g.shape[:2] + g.shape[3:], F32) for g in grads],
        scratch_shapes=[pltpu.SemaphoreType.DMA((len(items),)), pltpu.SemaphoreType.DMA((len(items),))],
    )(*grads)


def _rs_chips(sums, small):
    items = [(a, l, k) for a, g in enumerate(sums) for l in range(g.shape[0]) for k in range(NCHIP - 1)]
    na = len(sums)
    n_peer = NDEV - 1

    def body(*refs):
        ins, sm_in = refs[:na], refs[na]
        outs, sm_out = refs[na + 1:2 * na + 1], refs[2 * na + 1]
        send_sem, recv_sem, sm_send, sm_recv, loc = refs[2 * na + 2:]
        x, y, c, _, chips, blocks = _place()
        me = 4 * x + 2 * y + c
        mine = pltpu.make_async_copy(sm_in, sm_out.at[me], loc)
        mine.start()
        copies = [pltpu.make_async_remote_copy(
            src_ref=ins[a].at[l, blocks[k]], dst_ref=outs[a].at[l, k], send_sem=send_sem.at[q], recv_sem=recv_sem.at[q],
            device_id=(*chips[k], c), device_id_type=MESH) for q, (a, l, k) in enumerate(items)]
        for r in range(1, NDEV):
            peer = (1 - x if r & 4 else x, 1 - y if r & 2 else y, 1 - c if r & 1 else c)
            copies.append(pltpu.make_async_remote_copy(
                src_ref=sm_in, dst_ref=sm_out.at[me], send_sem=sm_send.at[r - 1], recv_sem=sm_recv.at[r - 1],
                device_id=peer, device_id_type=MESH))
        for cp in copies:
            cp.start()
        for cp in copies:
            cp.wait()
        mine.wait()

    out_shape = [jax.ShapeDtypeStruct((g.shape[0], NCHIP - 1) + g.shape[2:], BF16) for g in sums]
    out_shape.append(jax.ShapeDtypeStruct((NDEV,) + small.shape, F32))
    outs = pl.pallas_call(
        body, name="rs_chips", in_specs=[ANY] * (na + 1), out_specs=[ANY] * (na + 1), out_shape=out_shape,
        scratch_shapes=[pltpu.SemaphoreType.DMA((len(items),)), pltpu.SemaphoreType.DMA((len(items),)),
                        pltpu.SemaphoreType.DMA((n_peer,)), pltpu.SemaphoreType.DMA((n_peer,)), pltpu.SemaphoreType.DMA],
    )(*sums, small)
    return outs[:na], outs[na]


def _rs_swap(totals):
    na = len(totals)

    def body(*refs):
        ins, outs, send_sem, recv_sem, loc = refs[:na], refs[na:2 * na], refs[2 * na], refs[2 * na + 1], refs[2 * na + 2]
        x, y, c, _, _, _ = _place()
        local = [pltpu.make_async_copy(ins[a], outs[a].at[:, c], loc.at[a]) for a in range(na)]
        copies = [pltpu.make_async_remote_copy(
            src_ref=ins[a], dst_ref=outs[a].at[:, c], send_sem=send_sem.at[a], recv_sem=recv_sem.at[a],
            device_id=(x, y, 1 - c), device_id_type=MESH) for a in range(na)]
        for cp in local + copies:
            cp.start()
        for cp in copies + local:
            cp.wait()

    return pl.pallas_call(
        body, name="rs_swap", in_specs=[ANY] * na, out_specs=[ANY] * na,
        out_shape=[jax.ShapeDtypeStruct((t.shape[0], 2) + t.shape[1:], F32) for t in totals],
        scratch_shapes=[pltpu.SemaphoreType.DMA((na,)), pltpu.SemaphoreType.DMA((na,)), pltpu.SemaphoreType.DMA((na,))],
    )(*totals)


def _pair_sum(place, grad, landed, name):
    nl, _, _, rows, cols = grad.shape

    def body(place_ref, g_ref, l_ref, o_ref):
        o_ref[...] = (g_ref[...] + l_ref[...]).astype(BF16)

    return pl.pallas_call(
        body, name=name,
        grid_spec=pltpu.PrefetchScalarGridSpec(
            num_scalar_prefetch=1, grid=(nl, NCHIP),
            in_specs=[pl.BlockSpec((None, None, None, rows, cols), lambda l, b, p: (l, b, p[1], 0, 0)),
                      pl.BlockSpec((None, None, rows, cols), lambda l, b, p: (l, b, 0, 0))],
            out_specs=pl.BlockSpec((None, None, rows, cols), lambda l, b, p: (l, b, 0, 0))),
        out_shape=jax.ShapeDtypeStruct((nl, NCHIP, rows, cols), BF16), compiler_params=_cparams(2),
    )(place, grad, landed)


def _chip_sum(place, grad, landed, parts, name):
    nl, _, _, rows, cols = grad.shape

    def body(place_ref, g_ref, l_ref, p_ref, o_ref):
        tot = g_ref[...] + l_ref[...]
        for k in range(NCHIP - 1):
            tot = tot + p_ref[k].astype(F32)
        o_ref[...] = tot

    return pl.pallas_call(
        body, name=name,
        grid_spec=pltpu.PrefetchScalarGridSpec(
            num_scalar_prefetch=1, grid=(nl,),
            in_specs=[pl.BlockSpec((None, None, None, rows, cols), lambda l, p: (l, p[0], p[1], 0, 0)),
                      pl.BlockSpec((None, None, rows, cols), lambda l, p: (l, p[0], 0, 0)),
                      pl.BlockSpec((None, NCHIP - 1, rows, cols), lambda l, p: (l, 0, 0, 0))],
            out_specs=pl.BlockSpec((None, rows, cols), lambda l, p: (l, 0, 0))),
        out_shape=jax.ShapeDtypeStruct((nl, rows, cols), F32), compiler_params=_cparams(1),
    )(place, grad, landed, parts)


def _sum_devices(parts):
    def body(p_ref, o_ref):
        tot = p_ref[0]
        for d in range(1, NDEV):
            tot = tot + p_ref[d]
        o_ref[...] = tot

    return pl.pallas_call(
        body, name="sum_small", in_specs=[pl.BlockSpec(parts.shape, lambda: (0, 0, 0))],
        out_specs=pl.BlockSpec(parts.shape[1:], lambda: (0, 0)), out_shape=jax.ShapeDtypeStruct(parts.shape[1:], F32),
    )(parts)


def _pack(parts):
    rows = []
    for p in parts:
        p = p.reshape(-1, LANES)
        rows.append(jnp.pad(p, ((0, -p.shape[0] % SUBLANES), (0, 0))))
    return jnp.concatenate(rows, axis=0)


def _unpack(buf, shapes, lead=()):
    out, r0 = [], 0
    nl = len(lead)
    for shp in shapes:
        nrow = 1
        for d in shp:
            nrow *= d
        nrow //= LANES
        out.append(buf[(slice(None),) * nl + (slice(r0, r0 + nrow),)].reshape(lead + tuple(shp)))
        r0 += nrow + (-nrow % SUBLANES)
    return out


WEIGHT_ORDER = ("norm_mix_even", "w_in", "conv_a", "ln_a_g", "ln_a_b", "conv_b", "w_out", "norm_mix_odd", "w_pool",
                "pool_scale", "norm_ffn", "w_up", "conv_ffn_w", "w_down", "norm_final")
BIG = ("w_in", "w_out", "w_pool", "w_up", "w_down")
HALF = {"w_in": (D // 2, NZ // NCHIP), "w_out": (2 * A // NCHIP // 2, D), "w_pool": (PG // 2, PG),
        "w_up": (D // 2, FF2 // NCHIP), "w_down": (FF // NCHIP // 2, D)}
SMALL_SHARDED = ("conv_a", "conv_b", "conv_ffn_w", "norm_mix_odd", "pool_scale")
SMALL_ALL = ("norm_mix_even", "conv_a", "ln_a_g", "ln_a_b", "conv_b", "norm_mix_odd", "pool_scale", "norm_ffn", "conv_ffn_w",
             "norm_final")


def kernel(x, norm_mix_even, w_in, conv_a, ln_a_g, ln_a_b, conv_b, w_out, norm_mix_odd, w_pool, pool_scale, norm_ffn, w_up, conv_ffn_w, w_down, norm_final, loss_target, m_norm_mix_even, m_w_in, m_conv_a, m_ln_a_g, m_ln_a_b, m_conv_b, m_w_out, m_norm_mix_odd, m_w_pool, m_pool_scale, m_norm_ffn, m_w_up, m_conv_ffn_w, m_w_down, m_norm_final, v_norm_mix_even, v_w_in, v_conv_a, v_ln_a_g, v_ln_a_b, v_conv_b, v_w_out, v_norm_mix_odd, v_w_pool, v_pool_scale, v_norm_ffn, v_w_up, v_conv_ffn_w, v_w_down, v_norm_final):
    w = dict(norm_mix_even=norm_mix_even, w_in=w_in, conv_a=conv_a, ln_a_g=ln_a_g, ln_a_b=ln_a_b, conv_b=conv_b, w_out=w_out,
             norm_mix_odd=norm_mix_odd, w_pool=w_pool, pool_scale=pool_scale, norm_ffn=norm_ffn, w_up=w_up,
             conv_ffn_w=conv_ffn_w, w_down=w_down, norm_final=norm_final)
    m = dict(norm_mix_even=m_norm_mix_even, w_in=m_w_in, conv_a=m_conv_a, ln_a_g=m_ln_a_g, ln_a_b=m_ln_a_b, conv_b=m_conv_b,
             w_out=m_w_out, norm_mix_odd=m_norm_mix_odd, w_pool=m_w_pool, pool_scale=m_pool_scale, norm_ffn=m_norm_ffn,
             w_up=m_w_up, conv_ffn_w=m_conv_ffn_w, w_down=m_w_down, norm_final=m_norm_final)
    v = dict(norm_mix_even=v_norm_mix_even, w_in=v_w_in, conv_a=v_conv_a, ln_a_g=v_ln_a_g, ln_a_b=v_ln_a_b, conv_b=v_conv_b,
             w_out=v_w_out, norm_mix_odd=v_norm_mix_odd, w_pool=v_w_pool, pool_scale=v_pool_scale, norm_ffn=v_norm_ffn,
             w_up=v_w_up, conv_ffn_w=v_conv_ffn_w, w_down=v_w_down, norm_final=v_norm_final)
    chip = 2 * lax.axis_index("x") + lax.axis_index("y")
    place = jnp.stack([chip, lax.axis_index("c")]).astype(jnp.int32)

    half = lambda a, name: a.astype(BF16).reshape((2,) + HALF[name])
    shards = [half(w_in[0], "w_in"), half(w_out[0], "w_out"), half(w_up[0], "w_up"), half(w_down[0], "w_down"),
              half(w_pool[0], "w_pool"), half(w_up[1], "w_up"), half(w_down[1], "w_down")]
    small_shapes = [w[k].shape[-2:] if w[k].ndim == 3 and k != "conv_ffn_w" else (w[k].size // w[k].shape[-1], w[k].shape[-1])
                    for k in SMALL_SHARDED]
    (g_in, g_out, g_up0, g_dn0, g_pool, g_up1, g_dn1), small_g = _ag_weights(shards, _pack([w[k] for k in SMALL_SHARDED]))
    whole = {}
    for k, part in zip(SMALL_SHARDED, _unpack(small_g, small_shapes, lead=(NCHIP,))):
        whole[k] = jnp.moveaxis(part, 0, 1).reshape(part.shape[1], NCHIP * part.shape[2])
    wts = dict(
        w_in=g_in.reshape(NCHIP, D, NZ // NCHIP), w_out=g_out.reshape(2 * A, D),
        w_pool=g_pool.reshape(NCHIP, len(POOL_WINDOWS), PG // NCHIP, PG),
        w_up=[g_up0.reshape(NCHIP, D, FF2 // NCHIP), g_up1.reshape(NCHIP, D, FF2 // NCHIP)],
        w_down=[g_dn0.reshape(FF, D), g_dn1.reshape(FF, D)],
        norm_mix_even=norm_mix_even, conv_a=whole["conv_a"], ln_a_g=ln_a_g, ln_a_b=ln_a_b, conv_b=whole["conv_b"],
        norm_mix_odd=whole["norm_mix_odd"], pool_scale=whole["pool_scale"], norm_ffn=norm_ffn,
        conv_ffn_w=whole["conv_ffn_w"].reshape(2, K_S, FF2), norm_final=norm_final.reshape(1, D))

    loss_part, grad_x, big, small = _local_step(x[0], loss_target[0], wts)

    grads5 = [big[k].reshape(big[k].shape[0], NCHIP, 2, *HALF[k]) for k in BIG]
    landed = _rs_pair(grads5)
    sums = [_pair_sum(place, g5, ld, "pair_sum_" + k) for k, g5, ld in zip(BIG, grads5, landed)]
    small_full_shapes = [small[k].shape for k in SMALL_ALL] + [(1, LANES)]
    parts, small_all = _rs_chips(sums, _pack([small[k] for k in SMALL_ALL] + [loss_part]))
    totals = [_chip_sum(place, g5, ld, p, "chip_sum_" + k) for k, g5, ld, p in zip(BIG, grads5, landed, parts)]
    swapped = _rs_swap(totals)
    small_sum = _unpack(_sum_devices(small_all), small_full_shapes)
    loss = small_sum[-1][0, 0]

    grad = {k: s.reshape(w[k].shape) for k, s in zip(BIG, swapped)}
    for k, gsum in zip(SMALL_ALL, small_sum):
        if k in SMALL_SHARDED:
            cols = w[k].shape[-1]
            gsum = lax.dynamic_slice_in_dim(gsum, chip * cols, cols, axis=gsum.ndim - 1)
        grad[k] = gsum.reshape(w[k].shape)

    delta, new_m, new_v = {}, {}, {}
    rows_per_step = {"w_in": 512, "w_out": 256, "w_pool": 256, "w_up": 256, "w_down": 352}
    for k in BIG:
        as3 = lambda a: a.reshape(a.shape[0], -1, a.shape[-1])
        d3, m3, v3 = _adamw_big(as3(w[k]), as3(grad[k]), as3(m[k]), as3(v[k]), rows_per_step[k], "adamw_" + k)
        delta[k], new_m[k], new_v[k] = (a.reshape(w[k].shape) for a in (d3, m3, v3))
    as2 = lambda a: a.reshape(-1, a.shape[-1])
    ds, ms, vs = _adamw_small(*[[as2(t[k]) for k in SMALL_ALL] for t in (w, grad, m, v)])
    for k, d2, m2, v2 in zip(SMALL_ALL, ds, ms, vs):
        delta[k], new_m[k], new_v[k] = (a.reshape(w[k].shape) for a in (d2, m2, v2))

    return (loss, grad_x[None], *[grad[k] for k in WEIGHT_ORDER], *[delta[k] for k in WEIGHT_ORDER],
            *[new_m[k] for k in WEIGHT_ORDER], *[new_v[k] for k in WEIGHT_ORDER])
```

```python
import functools

import jax
import jax.numpy as jnp
from jax import lax
from jax.experimental import pallas as pl
from jax.experimental.pallas import tpu as pltpu

F32, BF16 = jnp.float32, jnp.bfloat16

D = 1024
A = 512
NZ = 5 * A
FF = 2816
FF2 = 2 * FF
NCHIP = 4
NDEV = 8
K_A, K_S = 31, 3
POOL_WINDOWS = (2, 4, 8, 16)
PG = D // len(POOL_WINDOWS)
RMS_EPS, LN_EPS = 1e-6, 1e-5
ADAM_LR, ADAM_B1, ADAM_B2, ADAM_EPS, ADAM_WD, ADAM_STEP = 0.001, 0.9, 0.999, 1e-08, 0.01, 10

HALO_A, HALO_S, HALO_P = 32, 8, 16
SUBLANES = 8
LANES = 128
VMEM_LIMIT_BYTES = 56 * 1024 * 1024

TS_MIX = 512
TS_MIXB = 256
TS_FFN = 256
TS_POOL = 512
TS_MM = 512
R_CHUNK = 64

MESH = pl.DeviceIdType.MESH
ANY = pl.BlockSpec(memory_space=pl.ANY)
NT_DIMS = (((1,), (1,)), ((), ()))
TN_DIMS = (((0,), (0,)), ((), ()))


def _cparams(n_axes):
    return pltpu.CompilerParams(dimension_semantics=("arbitrary",) * n_axes, vmem_limit_bytes=VMEM_LIMIT_BYTES)


def _const(shape):
    nd = len(shape)
    return pl.BlockSpec(shape, lambda *_: (0,) * nd, pipeline_mode=pl.Buffered(1))


def _sigmoid(v):
    return 1.0 / (1.0 + jnp.exp(-v))


def _rsqrt_mean_sq(x):
    return lax.rsqrt(jnp.mean(x * x, axis=-1, keepdims=True) + RMS_EPS)


def _rms_bwd(dh, xh, r, g):
    dxh = dh * g
    return r * (dxh - xh * jnp.mean(dxh * xh, axis=-1, keepdims=True))


def _shifted(buf_ref, row0, rows, col0, width, offsets):
    lo = (min(offsets) // SUBLANES) * SUBLANES
    hi = -(-(max(offsets) + rows) // SUBLANES) * SUBLANES
    win = buf_ref[pl.ds(pl.multiple_of(row0 + lo, SUBLANES), hi - lo), col0:col0 + width]
    out = {}
    for res in sorted({(o - lo) % SUBLANES for o in offsets}):
        qs = {o: (o - lo) // SUBLANES for o in offsets if (o - lo) % SUBLANES == res}
        base = win[res:res + rows + SUBLANES * max(qs.values()), :]
        for o, q in qs.items():
            out[o] = base[SUBLANES * q:SUBLANES * q + rows, :]
    return out


def _rowsum8(v):
    acc = v[0:SUBLANES, :]
    for r in range(SUBLANES, v.shape[0], SUBLANES):
        acc = acc + v[r:r + SUBLANES, :]
    return acc


def _taps(w_ref, sh, offsets, col0, width):
    acc = None
    for k, o in enumerate(offsets):
        term = w_ref[k:k + 1, col0:col0 + width] * sh[o]
        acc = term if acc is None else acc + term
    return acc


def _finish_tap_sums(acc_ref, out_ref, n_taps):
    for k in range(n_taps):
        out_ref[k:k + 1, :] = jnp.sum(acc_ref[SUBLANES * k:SUBLANES * (k + 1), :], axis=0, keepdims=True)


def _mix0_fwd(x, g, w_in, conv_a, ln_g, ln_b, conv_b, w_out):
    s = x.shape[0]
    ts = min(TS_MIX, s)
    n = s // ts
    bw = NZ // NCHIP
    offs_a = [HALO_A - (K_A - 1) + k for k in range(K_A)]
    offs_s = [HALO_S - (K_S - 1) + k for k in range(K_S)]

    def body(x_ref, g_ref, win_ref, ca_ref, lg_ref, lb_ref, cb_ref, wout_ref,
             h_ref, z_ref, ac_ref, cat_ref, x1_ref, glu_buf, cv_buf, bconv_buf):
        i = pl.program_id(0)

        @pl.when(i == 0)
        def _():
            glu_buf[0:HALO_A, :] = jnp.zeros((HALO_A, A), F32)
            cv_buf[0:HALO_S, :] = jnp.zeros((HALO_S, A), F32)

        xv = x_ref[...]
        h = (xv * _rsqrt_mean_sq(xv) * g_ref[...]).astype(BF16)
        h_ref[...] = h
        for j in range(NCHIP):
            z_ref[:, j * bw:(j + 1) * bw] = jnp.dot(h, win_ref[j], preferred_element_type=F32)
        glu_buf[HALO_A:HALO_A + ts, :] = z_ref[:, 0:A] * _sigmoid(z_ref[:, A:2 * A])
        cv_buf[HALO_S:HALO_S + ts, :] = z_ref[:, 3 * A:4 * A] * z_ref[:, 4 * A:5 * A]

        def chunk(ci, carry):
            r0 = pl.multiple_of(ci * R_CHUNK, R_CHUNK)
            for c0 in range(0, A, LANES):
                sh = _shifted(glu_buf, r0, R_CHUNK, c0, LANES, offs_a)
                ac_ref[pl.ds(r0, R_CHUNK), c0:c0 + LANES] = _taps(ca_ref, sh, offs_a, c0, LANES)
                sh = _shifted(cv_buf, r0, R_CHUNK, c0, LANES, offs_s)
                bconv_buf[pl.ds(r0, R_CHUNK), c0:c0 + LANES] = _taps(cb_ref, sh, offs_s, c0, LANES)
            return carry

        lax.fori_loop(0, ts // R_CHUNK, chunk, 0)
        glu_buf[0:HALO_A, :] = glu_buf[ts:ts + HALO_A, :]
        cv_buf[0:HALO_S, :] = cv_buf[ts:ts + HALO_S, :]

        ac = ac_ref[...]
        xc = ac - jnp.mean(ac, axis=-1, keepdims=True)
        xn = xc * lax.rsqrt(jnp.mean(xc * xc, axis=-1, keepdims=True) + LN_EPS)
        ln = xn * lg_ref[...] + lb_ref[...]
        cat_ref[:, 0:A] = (ln * _sigmoid(ln)).astype(BF16)
        cat_ref[:, A:2 * A] = (z_ref[:, 2 * A:3 * A] * bconv_buf[...]).astype(BF16)
        x1_ref[...] = xv + jnp.dot(cat_ref[...], wout_ref[...], preferred_element_type=F32)

    tile = lambda w: pl.BlockSpec((ts, w), lambda i: (i, 0))
    return pl.pallas_call(
        body, name="mix0_fwd", grid=(n,),
        in_specs=[tile(D), _const((1, D)), _const((NCHIP, D, bw)), _const((K_A, A)), _const((1, A)), _const((1, A)),
                  _const((K_S, A)), _const((2 * A, D))],
        out_specs=[tile(D), tile(NZ), tile(A), tile(2 * A), tile(D)],
        out_shape=[jax.ShapeDtypeStruct((s, D), BF16), jax.ShapeDtypeStruct((s, NZ), F32), jax.ShapeDtypeStruct((s, A), F32),
                   jax.ShapeDtypeStruct((s, 2 * A), BF16), jax.ShapeDtypeStruct((s, D), F32)],
        scratch_shapes=[pltpu.VMEM((HALO_A + ts, A), F32), pltpu.VMEM((HALO_S + ts, A), F32), pltpu.VMEM((ts, A), F32)],
        compiler_params=_cparams(1),
    )(x, g, w_in, conv_a, ln_g, ln_b, conv_b, w_out)


def _ffn_fwd(x, g, w_up, wc, w_down, name):
    s = x.shape[0]
    ts = min(TS_FFN, s)
    n = s // ts
    bw = FF2 // NCHIP
    lw = 2 * LANES
    rows = 32
    offs = [HALO_S - (K_S - 1) + k for k in range(K_S)]

    def body(x_ref, g_ref, wup_ref, wc_ref, wdn_ref, h_ref, u0_ref, act_ref, xo_ref, cbuf):
        i = pl.program_id(0)

        @pl.when(i == 0)
        def _():
            cbuf[0:HALO_S, :] = jnp.zeros((HALO_S, FF2), F32)

        xv = x_ref[...]
        h = (xv * _rsqrt_mean_sq(xv) * g_ref[...]).astype(BF16)
        h_ref[...] = h
        for j in range(NCHIP):
            zc = jnp.dot(h, wup_ref[j], preferred_element_type=F32)
            u0_ref[:, j * bw:(j + 1) * bw] = zc
            cbuf[HALO_S:HALO_S + ts, j * bw:(j + 1) * bw] = zc

        def chunk(ci, carry):
            r0 = pl.multiple_of(ci * rows, rows)
            for c0 in range(0, FF, lw):
                ug = _taps(wc_ref, _shifted(cbuf, r0, rows, c0, lw, offs), offs, c0, lw)
                uv = _taps(wc_ref, _shifted(cbuf, r0, rows, FF + c0, lw, offs), offs, FF + c0, lw)
                act_ref[pl.ds(r0, rows), c0:c0 + lw] = (ug * _sigmoid(ug) * uv).astype(BF16)
            return carry

        lax.fori_loop(0, ts // rows, chunk, 0)
        cbuf[0:HALO_S, :] = cbuf[ts:ts + HALO_S, :]
        xo_ref[...] = xv + jnp.dot(act_ref[...], wdn_ref[...], preferred_element_type=F32)

    tile = lambda w: pl.BlockSpec((ts, w), lambda i: (i, 0))
    return pl.pallas_call(
        body, name=name, grid=(n,),
        in_specs=[tile(D), _const((1, D)), _const((NCHIP, D, bw)), _const((K_S, FF2)), _const((FF, D))],
        out_specs=[tile(D), tile(FF2), tile(FF), tile(D)],
        out_shape=[jax.ShapeDtypeStruct((s, D), BF16), jax.ShapeDtypeStruct((s, FF2), F32),
                   jax.ShapeDtypeStruct((s, FF), BF16), jax.ShapeDtypeStruct((s, D), F32)],
        scratch_shapes=[pltpu.VMEM((HALO_S + ts, FF2), F32)],
        compiler_params=_cparams(1),
    )(x, g, w_up, wc, w_down)


def _pool_windows(hbuf, pbuf, tile_row0, ts):
    def chunk(ci, carry):
        r0 = pl.multiple_of(ci * R_CHUNK, R_CHUNK)
        t1 = (tile_row0 + r0 + lax.broadcasted_iota(jnp.int32, (R_CHUNK, 1), 0) + 1).astype(F32)
        for gi, w in enumerate(POOL_WINDOWS):
            cnt = jnp.minimum(t1, float(w))
            offs = [HALO_P - jj for jj in range(w)]
            for c0 in range(gi * PG, (gi + 1) * PG, LANES):
                sh = _shifted(hbuf, r0, R_CHUNK, c0, LANES, offs)
                tot = sh[offs[0]]
                for o in offs[1:]:
                    tot = tot + sh[o]
                pbuf[pl.ds(r0, R_CHUNK), c0:c0 + LANES] = (tot / cnt - sh[HALO_P]).astype(BF16)
        return carry

    lax.fori_loop(0, ts // R_CHUNK, chunk, 0)


def _assemble_wpool(wp_ref, wps):
    rb = PG // NCHIP
    for gi in range(len(POOL_WINDOWS)):
        for j in range(NCHIP):
            wps[gi, j * rb:(j + 1) * rb, :] = wp_ref[j, gi]


def _pool_fwd(x, g, w_pool, scale):
    s = x.shape[0]
    ts = min(TS_POOL, s)
    n = s // ts
    ng = len(POOL_WINDOWS)

    def body(x_ref, g_ref, wp_ref, sc_ref, xo_ref, hbuf, pbuf, wps):
        i = pl.program_id(0)

        @pl.when(i == 0)
        def _():
            hbuf[0:HALO_P, :] = jnp.zeros((HALO_P, D), F32)
            _assemble_wpool(wp_ref, wps)

        xv = x_ref[...]
        hbuf[HALO_P:HALO_P + ts, :] = xv * _rsqrt_mean_sq(xv) * g_ref[...]
        _pool_windows(hbuf, pbuf, i * ts, ts)
        hbuf[0:HALO_P, :] = hbuf[ts:ts + HALO_P, :]
        for gi in range(ng):
            cols = slice(gi * PG, (gi + 1) * PG)
            y = jnp.dot(pbuf[:, cols], wps[gi], preferred_element_type=F32)
            xo_ref[:, cols] = xv[:, cols] + y * sc_ref[:, cols]

    tile = pl.BlockSpec((ts, D), lambda i: (i, 0))
    return pl.pallas_call(
        body, name="pool_fwd", grid=(n,),
        in_specs=[tile, _const((1, D)), _const((NCHIP, ng, PG // NCHIP, PG)), _const((1, D))],
        out_specs=tile, out_shape=jax.ShapeDtypeStruct((s, D), F32),
        scratch_shapes=[pltpu.VMEM((HALO_P + ts, D), F32), pltpu.VMEM((ts, D), BF16), pltpu.VMEM((ng, PG, PG), BF16)],
        compiler_params=_cparams(1),
    )(x, g, w_pool, scale)


def _final_loss(x, g, target):
    s = x.shape[0]
    ts = min(TS_MM, s)
    n = s // ts

    def body(x_ref, g_ref, t_ref, dx_ref, dg_ref, loss_ref):
        i = pl.program_id(0)

        @pl.when(i == 0)
        def _():
            dg_ref[...] = jnp.zeros((1, D), F32)
            loss_ref[...] = jnp.zeros((1, LANES), F32)

        xv = x_ref[...]
        r = _rsqrt_mean_sq(xv)
        xh = xv * r
        gv = g_ref[...]
        err = xh * gv - t_ref[...]
        sq = jnp.sum(jnp.sum(err * err, axis=1, keepdims=True), axis=0, keepdims=True)
        loss_ref[...] += sq * (0.5 / D)
        dy = err * (1.0 / D)
        dg_ref[...] += jnp.sum(dy * xh, axis=0, keepdims=True)
        dx_ref[...] = _rms_bwd(dy, xh, r, gv)

    tile = pl.BlockSpec((ts, D), lambda i: (i, 0))
    return pl.pallas_call(
        body, name="final_loss", grid=(n,),
        in_specs=[tile, _const((1, D)), tile],
        out_specs=[tile, pl.BlockSpec((1, D), lambda i: (0, 0)), pl.BlockSpec((1, LANES), lambda i: (0, 0))],
        out_shape=[jax.ShapeDtypeStruct((s, D), F32), jax.ShapeDtypeStruct((1, D), F32), jax.ShapeDtypeStruct((1, LANES), F32)],
        compiler_params=_cparams(1),
    )(x, g, target)


def _ffn_bwd_a(dxo, u0, wc, w_down, name):
    s = dxo.shape[0]
    ts = min(TS_FFN, s)
    n = s // ts
    cw = FF2 // NCHIP
    rows = 32
    lw2 = 2 * LANES
    offs = [HALO_S - (K_S - 1) + k for k in range(K_S)]
    boffs = [K_S - 1 - k for k in range(K_S)]

    def body(dxo_ref, u0_ref, halo_ref, wc_ref, wdn_ref, du0_ref, dwc_ref, ubuf, dubuf, dact, dwacc):
        i = pl.program_id(0)
        j = n - 1 - i

        @pl.when(i == 0)
        def _():
            dubuf[ts:ts + HALO_S, :] = jnp.zeros((HALO_S, FF2), F32)
            dwacc[...] = jnp.zeros(dwacc.shape, F32)

        ubuf[0:HALO_S, :] = jnp.where(j == 0, 0.0, halo_ref[...])
        ubuf[HALO_S:HALO_S + ts, :] = u0_ref[...]
        df = dxo_ref[...].astype(BF16)
        for cg in range(0, FF, cw):
            dact[...] = lax.dot_general(df, wdn_ref[cg:cg + cw, :], NT_DIMS, preferred_element_type=F32)

            def chunk(ci, carry, cg=cg):
                r0 = pl.multiple_of(ci * rows, rows)
                rs = pl.ds(r0, rows)
                for c in range(cg, cg + cw, LANES):
                    shg = _shifted(ubuf, r0, rows, c, LANES, offs)
                    shv = _shifted(ubuf, r0, rows, FF + c, LANES, offs)
                    ug = _taps(wc_ref, shg, offs, c, LANES)
                    uv = _taps(wc_ref, shv, offs, FF + c, LANES)
                    sg = _sigmoid(ug)
                    da = dact[rs, c - cg:c - cg + LANES]
                    dgv = da * uv * sg * (1.0 + ug * (1.0 - sg))
                    dvv = da * ug * sg
                    dubuf[rs, c:c + LANES] = dgv
                    dubuf[rs, FF + c:FF + c + LANES] = dvv
                    for k, o in enumerate(offs):
                        ks = slice(SUBLANES * k, SUBLANES * (k + 1))
                        dwacc[ks, c:c + LANES] += _rowsum8(dgv * shg[o])
                        dwacc[ks, FF + c:FF + c + LANES] += _rowsum8(dvv * shv[o])
                return carry

            lax.fori_loop(0, ts // rows, chunk, 0)

        def chunk2(ci, carry):
            r0 = pl.multiple_of(ci * rows, rows)
            for c in range(0, FF2, lw2):
                sh = _shifted(dubuf, r0, rows, c, lw2, boffs)
                du0_ref[pl.ds(r0, rows), c:c + lw2] = _taps(wc_ref, sh, boffs, c, lw2).astype(BF16)
            return carry

        lax.fori_loop(0, ts // rows, chunk2, 0)
        dubuf[ts:ts + HALO_S, :] = dubuf[0:HALO_S, :]

        @pl.when(i == n - 1)
        def _():
            _finish_tap_sums(dwacc, dwc_ref, K_S)

    rev = lambda w: pl.BlockSpec((ts, w), lambda i: (n - 1 - i, 0))
    halo = pl.BlockSpec((HALO_S, FF2), lambda i: (jnp.maximum((n - 1 - i) * (ts // HALO_S) - 1, 0), 0))
    return pl.pallas_call(
        body, name=name, grid=(n,),
        in_specs=[rev(D), rev(FF2), halo, _const((K_S, FF2)), _const((FF, D))],
        out_specs=[rev(FF2), pl.BlockSpec((K_S, FF2), lambda i: (0, 0))],
        out_shape=[jax.ShapeDtypeStruct((s, FF2), BF16), jax.ShapeDtypeStruct((K_S, FF2), F32)],
        scratch_shapes=[pltpu.VMEM((HALO_S + ts, FF2), F32), pltpu.VMEM((ts + HALO_S, FF2), F32), pltpu.VMEM((ts, cw), F32),
                        pltpu.VMEM((SUBLANES * K_S, FF2), F32)],
        compiler_params=_cparams(1),
    )(dxo, u0, u0, wc, w_down)


def _nt_rms_bwd(dy, w, x, g, dres, name):
    s = x.shape[0]
    ts = min(TS_MM, s)
    n = s // ts
    nw = dy.shape[1]
    bw = nw // NCHIP

    def body(dy_ref, w_ref, x_ref, g_ref, dres_ref, dx_ref, dg_ref):
        i = pl.program_id(0)

        @pl.when(i == 0)
        def _():
            dg_ref[...] = jnp.zeros((1, D), F32)

        dh = lax.dot_general(dy_ref[:, 0:bw], w_ref[0], NT_DIMS, preferred_element_type=F32)
        for j in range(1, NCHIP):
            dh = dh + lax.dot_general(dy_ref[:, j * bw:(j + 1) * bw], w_ref[j], NT_DIMS, preferred_element_type=F32)
        xv = x_ref[...]
        r = _rsqrt_mean_sq(xv)
        xh = xv * r
        dg_ref[...] += jnp.sum(dh * xh, axis=0, keepdims=True)
        dx_ref[...] = dres_ref[...] + _rms_bwd(dh, xh, r, g_ref[...])

    tile = lambda wd: pl.BlockSpec((ts, wd), lambda i: (i, 0))
    return pl.pallas_call(
        body, name=name, grid=(n,),
        in_specs=[tile(nw), _const((NCHIP, D, bw)), tile(D), _const((1, D)), tile(D)],
        out_specs=[tile(D), pl.BlockSpec((1, D), lambda i: (0, 0))],
        out_shape=[jax.ShapeDtypeStruct((s, D), F32), jax.ShapeDtypeStruct((1, D), F32)],
        compiler_params=_cparams(1),
    )(dy, w, x, g, dres)


def _wgrad(a, b, bn, out_shape, out_block, out_index, name, prev=None):
    s, m = a.shape
    nb = b.shape[1] // bn
    tk = min(TS_MM, s)
    nk = s // tk

    def body(*refs):
        a_ref, b_ref, o_ref = refs[0], refs[1], refs[-1]
        k = pl.program_id(1)

        @pl.when(k == 0)
        def _():
            o_ref[...] = jnp.zeros(o_ref.shape, F32)

        o_ref[...] += lax.dot_general(a_ref[...], b_ref[...].astype(BF16), TN_DIMS, preferred_element_type=F32)

    in_specs = [pl.BlockSpec((tk, m), lambda jn, k: (k, 0)), pl.BlockSpec((tk, bn), lambda jn, k: (k, jn))]
    args = [a, b]
    aliases = {}
    if prev is not None:
        in_specs.append(ANY)
        args.append(prev)
        aliases = {2: 0}
    return pl.pallas_call(
        body, name=name, grid=(nb, nk), in_specs=in_specs,
        out_specs=pl.BlockSpec(out_block, lambda jn, k: out_index(jn)),
        out_shape=jax.ShapeDtypeStruct(out_shape, F32), input_output_aliases=aliases,
        compiler_params=_cparams(2),
    )(*args)


def _pool_bwd(dxo, x, g, w_pool, scale):
    s = x.shape[0]
    ts = min(TS_POOL, s)
    n = s // ts
    ng = len(POOL_WINDOWS)
    rb = PG // NCHIP

    def body(dxo_ref, x_ref, halo_ref, g_ref, wp_ref, sc_ref, dx_ref, dwp_ref, dsc_ref, dg_ref,
             hbuf, pbuf, qbuf, dhbuf, wps, dwacc):
        i = pl.program_id(0)
        j = n - 1 - i

        @pl.when(i == 0)
        def _():
            qbuf[ts:ts + HALO_P, :] = jnp.zeros((HALO_P, D), F32)
            dwacc[...] = jnp.zeros(dwacc.shape, F32)
            dsc_ref[...] = jnp.zeros((1, D), F32)
            dg_ref[...] = jnp.zeros((1, D), F32)
            _assemble_wpool(wp_ref, wps)

        gv = g_ref[...]
        xl = halo_ref[...]
        hbuf[0:HALO_P, :] = jnp.where(j == 0, 0.0, xl * _rsqrt_mean_sq(xl) * gv)
        xv = x_ref[...]
        r = _rsqrt_mean_sq(xv)
        xh = xv * r
        hbuf[HALO_P:HALO_P + ts, :] = xh * gv
        _pool_windows(hbuf, pbuf, j * ts, ts)

        dy = dxo_ref[...]
        t1 = (j * ts + lax.broadcasted_iota(jnp.int32, (ts, 1), 0) + 1).astype(F32)
        for gi, w in enumerate(POOL_WINDOWS):
            cols = slice(gi * PG, (gi + 1) * PG)
            p = pbuf[:, cols]
            y = jnp.dot(p, wps[gi], preferred_element_type=F32)
            dsc_ref[:, cols] += jnp.sum(dy[:, cols] * y, axis=0, keepdims=True)
            dq = (dy[:, cols] * sc_ref[:, cols]).astype(BF16)
            dwacc[gi] += lax.dot_general(p, dq, TN_DIMS, preferred_element_type=F32)
            dp = lax.dot_general(dq, wps[gi], NT_DIMS, preferred_element_type=F32)
            qbuf[0:ts, cols] = dp / jnp.minimum(t1, float(w))

        def chunk(ci, carry):
            r0 = pl.multiple_of(ci * R_CHUNK, R_CHUNK)
            tc = (j * ts + r0 + lax.broadcasted_iota(jnp.int32, (R_CHUNK, 1), 0) + 1).astype(F32)
            for gi, w in enumerate(POOL_WINDOWS):
                cnt = jnp.minimum(tc, float(w))
                offs = list(range(w))
                for c0 in range(gi * PG, (gi + 1) * PG, LANES):
                    sh = _shifted(qbuf, r0, R_CHUNK, c0, LANES, offs)
                    tot = sh[0]
                    for o in offs[1:]:
                        tot = tot + sh[o]
                    dhbuf[pl.ds(r0, R_CHUNK), c0:c0 + LANES] = tot - sh[0] * cnt
            return carry

        lax.fori_loop(0, ts // R_CHUNK, chunk, 0)
        qbuf[ts:ts + HALO_P, :] = qbuf[0:HALO_P, :]
        dh = dhbuf[...]
        dg_ref[...] += jnp.sum(dh * xh, axis=0, keepdims=True)
        dx_ref[...] = dy + _rms_bwd(dh, xh, r, gv)

        @pl.when(i == n - 1)
        def _():
            for gi in range(ng):
                for jj in range(NCHIP):
                    dwp_ref[jj, gi] = dwacc[gi, jj * rb:(jj + 1) * rb, :]

    rev = pl.BlockSpec((ts, D), lambda i: (n - 1 - i, 0))
    halo = pl.BlockSpec((HALO_P, D), lambda i: (jnp.maximum((n - 1 - i) * (ts // HALO_P) - 1, 0), 0))
    vec = pl.BlockSpec((1, D), lambda i: (0, 0))
    return pl.pallas_call(
        body, name="pool_bwd", grid=(n,),
        in_specs=[rev, rev, halo, _const((1, D)), _const((NCHIP, ng, rb, PG)), _const((1, D))],
        out_specs=[rev, pl.BlockSpec((NCHIP, ng, rb, PG), lambda i: (0, 0, 0, 0)), vec, vec],
        out_shape=[jax.ShapeDtypeStruct((s, D), F32), jax.ShapeDtypeStruct((NCHIP, ng, rb, PG), F32),
                   jax.ShapeDtypeStruct((1, D), F32), jax.ShapeDtypeStruct((1, D), F32)],
        scratch_shapes=[pltpu.VMEM((HALO_P + ts, D), F32), pltpu.VMEM((ts, D), BF16), pltpu.VMEM((ts + HALO_P, D), F32),
                        pltpu.VMEM((ts, D), F32), pltpu.VMEM((ng, PG, PG), BF16), pltpu.VMEM((ng, PG, PG), F32)],
        compiler_params=_cparams(1),
    )(dxo, x, x, g, w_pool, scale)


def _mix0_bwd_a(dx1, z, ac, w_out, conv_a, ln_g, ln_b, conv_b):
    s = dx1.shape[0]
    ts = min(TS_MIXB, s)
    n = s // ts
    rows = 32
    offs_a = [HALO_A - (K_A - 1) + k for k in range(K_A)]
    offs_s = [HALO_S - (K_S - 1) + k for k in range(K_S)]
    boffs_a = [K_A - 1 - k for k in range(K_A)]
    boffs_s = [K_S - 1 - k for k in range(K_S)]

    def body(dx1_ref, z_ref, zh_ref, ac_ref, wout_ref, ca_ref, lg_ref, lb_ref, cb_ref,
             dz_ref, dca_ref, dlg_ref, dlb_ref, dcb_ref,
             glu_buf, cv_buf, dac_buf, dbc_buf, db_buf, dca_acc, dcb_acc):
        i = pl.program_id(0)
        j = n - 1 - i

        @pl.when(i == 0)
        def _():
            dac_buf[ts:ts + HALO_A, :] = jnp.zeros((HALO_A, A), F32)
            dbc_buf[ts:ts + HALO_S, :] = jnp.zeros((HALO_S, A), F32)
            dca_acc[...] = jnp.zeros(dca_acc.shape, F32)
            dcb_acc[...] = jnp.zeros(dcb_acc.shape, F32)
            dlg_ref[...] = jnp.zeros((1, A), F32)
            dlb_ref[...] = jnp.zeros((1, A), F32)

        glu_h = zh_ref[:, 0:A] * _sigmoid(zh_ref[:, A:2 * A])
        glu_buf[0:HALO_A, :] = jnp.where(j == 0, 0.0, glu_h)
        hs = slice(HALO_A - HALO_S, HALO_A)
        cv_buf[0:HALO_S, :] = jnp.where(j == 0, 0.0, zh_ref[hs, 3 * A:4 * A] * zh_ref[hs, 4 * A:5 * A])
        glu_buf[HALO_A:HALO_A + ts, :] = z_ref[:, 0:A] * _sigmoid(z_ref[:, A:2 * A])
        cv_buf[HALO_S:HALO_S + ts, :] = z_ref[:, 3 * A:4 * A] * z_ref[:, 4 * A:5 * A]

        dcat = lax.dot_general(dx1_ref[...].astype(BF16), wout_ref[...], NT_DIMS, preferred_element_type=F32)
        db_buf[...] = dcat[:, A:2 * A]
        ac = ac_ref[...]
        xc = ac - jnp.mean(ac, axis=-1, keepdims=True)
        rstd = lax.rsqrt(jnp.mean(xc * xc, axis=-1, keepdims=True) + LN_EPS)
        xn = xc * rstd
        lg = lg_ref[...]
        ln = xn * lg + lb_ref[...]
        sl = _sigmoid(ln)
        dln = dcat[:, 0:A] * sl * (1.0 + ln * (1.0 - sl))
        dlg_ref[...] += jnp.sum(dln * xn, axis=0, keepdims=True)
        dlb_ref[...] += jnp.sum(dln, axis=0, keepdims=True)
        dxn = dln * lg
        dac_buf[0:ts, :] = rstd * (dxn - jnp.mean(dxn, axis=-1, keepdims=True)
                                   - xn * jnp.mean(dxn * xn, axis=-1, keepdims=True))

        def chunk(ci, carry):
            r0 = pl.multiple_of(ci * rows, rows)
            rs = pl.ds(r0, rows)
            for c0 in range(0, A, LANES):
                cs = slice(c0, c0 + LANES)
                shc = _shifted(cv_buf, r0, rows, c0, LANES, offs_s)
                bconv = _taps(cb_ref, shc, offs_s, c0, LANES)
                dbv = db_buf[rs, cs]
                dz_ref[rs, 2 * A + c0:2 * A + c0 + LANES] = (dbv * bconv).astype(BF16)
                dbconv = dbv * z_ref[rs, 2 * A + c0:2 * A + c0 + LANES]
                dbc_buf[rs, cs] = dbconv
                for k, o in enumerate(offs_s):
                    dcb_acc[SUBLANES * k:SUBLANES * (k + 1), cs] += _rowsum8(dbconv * shc[o])
                shg = _shifted(glu_buf, r0, rows, c0, LANES, offs_a)
                dacv = dac_buf[rs, cs]
                for k, o in enumerate(offs_a):
                    dca_acc[SUBLANES * k:SUBLANES * (k + 1), cs] += _rowsum8(dacv * shg[o])
            return carry

        lax.fori_loop(0, ts // rows, chunk, 0)

        def chunk2(ci, carry):
            r0 = pl.multiple_of(ci * rows, rows)
            rs = pl.ds(r0, rows)
            for c0 in range(0, A, LANES):
                col = lambda grp: slice(grp * A + c0, grp * A + c0 + LANES)
                dglu = _taps(ca_ref, _shifted(dac_buf, r0, rows, c0, LANES, boffs_a), boffs_a, c0, LANES)
                sg = _sigmoid(z_ref[rs, col(1)])
                dz_ref[rs, col(0)] = (dglu * sg).astype(BF16)
                dz_ref[rs, col(1)] = (dglu * z_ref[rs, col(0)] * sg * (1.0 - sg)).astype(BF16)
                dcv = _taps(cb_ref, _shifted(dbc_buf, r0, rows, c0, LANES, boffs_s), boffs_s, c0, LANES)
                dz_ref[rs, col(3)] = (dcv * z_ref[rs, col(4)]).astype(BF16)
                dz_ref[rs, col(4)] = (dcv * z_ref[rs, col(3)]).astype(BF16)
            return carry

        lax.fori_loop(0, ts // rows, chunk2, 0)
        dac_buf[ts:ts + HALO_A, :] = dac_buf[0:HALO_A, :]
        dbc_buf[ts:ts + HALO_S, :] = dbc_buf[0:HALO_S, :]

        @pl.when(i == n - 1)
        def _():
            _finish_tap_sums(dca_acc, dca_ref, K_A)
            _finish_tap_sums(dcb_acc, dcb_ref, K_S)

    rev = lambda w: pl.BlockSpec((ts, w), lambda i: (n - 1 - i, 0))
    halo = pl.BlockSpec((HALO_A, NZ), lambda i: (jnp.maximum((n - 1 - i) * (ts // HALO_A) - 1, 0), 0))
    full = lambda r, c: pl.BlockSpec((r, c), lambda i: (0, 0))
    return pl.pallas_call(
        body, name="mix0_bwd_a", grid=(n,),
        in_specs=[rev(D), rev(NZ), halo, rev(A), _const((2 * A, D)), _const((K_A, A)), _const((1, A)), _const((1, A)),
                  _const((K_S, A))],
        out_specs=[rev(NZ), full(K_A, A), full(1, A), full(1, A), full(K_S, A)],
        out_shape=[jax.ShapeDtypeStruct((s, NZ), BF16), jax.ShapeDtypeStruct((K_A, A), F32), jax.ShapeDtypeStruct((1, A), F32),
                   jax.ShapeDtypeStruct((1, A), F32), jax.ShapeDtypeStruct((K_S, A), F32)],
        scratch_shapes=[pltpu.VMEM((HALO_A + ts, A), F32), pltpu.VMEM((HALO_S + ts, A), F32), pltpu.VMEM((ts + HALO_A, A), F32),
                        pltpu.VMEM((ts + HALO_S, A), F32), pltpu.VMEM((ts, A), F32), pltpu.VMEM((SUBLANES * K_A, A), F32),
                        pltpu.VMEM((SUBLANES * K_S, A), F32)],
        compiler_params=_cparams(1),
    )(dx1, z, z, ac, w_out, conv_a, ln_g, ln_b, conv_b)


def _local_step(x, target, wts):
    h0, z, ac, cat, x1 = _mix0_fwd(x, wts["norm_mix_even"], wts["w_in"], wts["conv_a"], wts["ln_a_g"], wts["ln_a_b"],
                                   wts["conv_b"], wts["w_out"])
    hf0, u00, act0, x2 = _ffn_fwd(x1, wts["norm_ffn"][0:1], wts["w_up"][0], wts["conv_ffn_w"][0], wts["w_down"][0], "ffn0_fwd")
    x3 = _pool_fwd(x2, wts["norm_mix_odd"], wts["w_pool"], wts["pool_scale"])
    hf1, u01, act1, x4 = _ffn_fwd(x3, wts["norm_ffn"][1:2], wts["w_up"][1], wts["conv_ffn_w"][1], wts["w_down"][1], "ffn1_fwd")
    dx4, g_norm_final, loss = _final_loss(x4, wts["norm_final"], target)

    bw_up = FF2 // NCHIP
    up_shape, up_block = (2, NCHIP, D, bw_up), (None, None, D, bw_up)
    dn_shape, dn_block = (2, FF, D), (None, FF, D // 2)

    du01, g_wc1 = _ffn_bwd_a(dx4, u01, wts["conv_ffn_w"][1], wts["w_down"][1], "ffn1_bwd_a")
    dx3, g_nf1 = _nt_rms_bwd(du01, wts["w_up"][1], x3, wts["norm_ffn"][1:2], dx4, "ffn1_bwd_b")
    dx2, g_wpool, g_scale, g_nmo = _pool_bwd(dx3, x2, wts["norm_mix_odd"], wts["w_pool"], wts["pool_scale"])
    du00, g_wc0 = _ffn_bwd_a(dx2, u00, wts["conv_ffn_w"][0], wts["w_down"][0], "ffn0_bwd_a")
    dx1, g_nf0 = _nt_rms_bwd(du00, wts["w_up"][0], x1, wts["norm_ffn"][0:1], dx2, "ffn0_bwd_b")
    dz, g_ca, g_lg, g_lb, g_cb = _mix0_bwd_a(dx1, z, ac, wts["w_out"], wts["conv_a"], wts["ln_a_g"], wts["ln_a_b"], wts["conv_b"])
    grad_x, g_nme = _nt_rms_bwd(dz, wts["w_in"], x, wts["norm_mix_even"], dx1, "mix0_bwd_b")

    g_up = _wgrad(hf1, du01, bw_up, up_shape, up_block, lambda jn: (1, jn, 0, 0), "wgrad_up1")
    g_up = _wgrad(hf0, du00, bw_up, up_shape, up_block, lambda jn: (0, jn, 0, 0), "wgrad_up0", prev=g_up)
    g_dn = _wgrad(act1, dx4, D // 2, dn_shape, dn_block, lambda jn: (1, 0, jn), "wgrad_down1")
    g_dn = _wgrad(act0, dx2, D // 2, dn_shape, dn_block, lambda jn: (0, 0, jn), "wgrad_down0", prev=g_dn)
    g_out = _wgrad(cat, dx1, D, (1, D, D), (None, D, D), lambda jn: (0, 0, 0), "wgrad_out")
    g_in = _wgrad(h0, dz, NZ // NCHIP, (1, NCHIP, D, NZ // NCHIP), (None, None, D, NZ // NCHIP), lambda jn: (0, jn, 0, 0),
                  "wgrad_in")

    big = {"w_in": g_in, "w_out": g_out, "w_pool": g_wpool[None], "w_up": g_up, "w_down": g_dn}
    small = {"norm_mix_even": g_nme, "conv_a": g_ca, "ln_a_g": g_lg, "ln_a_b": g_lb, "conv_b": g_cb, "norm_mix_odd": g_nmo,
             "pool_scale": g_scale, "norm_ffn": jnp.concatenate([g_nf0, g_nf1], axis=0),
             "conv_ffn_w": jnp.stack([g_wc0, g_wc1]), "norm_final": g_norm_final}
    return loss, grad_x, big, small


def _adamw_math(w, g, m, v):
    m = ADAM_B1 * m + (1.0 - ADAM_B1) * g
    v = ADAM_B2 * v + (1.0 - ADAM_B2) * (g * g)
    m_hat = m / (1.0 - ADAM_B1 ** ADAM_STEP)
    v_hat = v / (1.0 - ADAM_B2 ** ADAM_STEP)
    return -ADAM_LR * (m_hat / (jnp.sqrt(v_hat) + ADAM_EPS) + ADAM_WD * w), m, v


def _adamw_big(w, g, m, v, tr, name):
    nl, rows, cols = w.shape

    def body(w_ref, g_ref, m_ref, v_ref, d_ref, m2_ref, v2_ref):
        d_ref[...], m2_ref[...], v2_ref[...] = _adamw_math(w_ref[...], g_ref[...], m_ref[...], v_ref[...])

    spec = pl.BlockSpec((None, tr, cols), lambda l, r: (l, r, 0))
    return pl.pallas_call(
        body, name=name, grid=(nl, rows // tr), in_specs=[spec] * 4, out_specs=[spec] * 3,
        out_shape=[jax.ShapeDtypeStruct(w.shape, F32)] * 3, compiler_params=_cparams(2),
    )(w, g, m, v)


def _adamw_small(ws, gs, ms, vs):
    n = len(ws)

    def body(*refs):
        for p in range(n):
            w_ref, g_ref, m_ref, v_ref = (refs[q * n + p] for q in range(4))
            d_ref, m2_ref, v2_ref = (refs[(4 + q) * n + p] for q in range(3))
            d_ref[...], m2_ref[...], v2_ref[...] = _adamw_math(w_ref[...], g_ref[...], m_ref[...], v_ref[...])

    whole = lambda a: pl.BlockSpec(a.shape, lambda: (0,) * a.ndim)
    outs = pl.pallas_call(
        body, name="adamw_small", in_specs=[whole(a) for a in ws] * 4, out_specs=[whole(a) for a in ws] * 3,
        out_shape=[jax.ShapeDtypeStruct(a.shape, F32) for a in ws] * 3,
        compiler_params=pltpu.CompilerParams(vmem_limit_bytes=VMEM_LIMIT_BYTES),
    )(*ws, *gs, *ms, *vs)
    return outs[0:n], outs[n:2 * n], outs[2 * n:3 * n]


def _place():
    x, y, c = lax.axis_index("x"), lax.axis_index("y"), lax.axis_index("c")
    chips = [(x, 1 - y), (1 - x, y), (1 - x, 1 - y)]
    blocks = [2 * cx + cy for cx, cy in chips]
    return x, y, c, 2 * x + y, chips, blocks


def _ag_weights(shards, small):
    na = len(shards)
    nk = NCHIP - 1

    def body(*refs):
        ins, sm_in = refs[:na], refs[na]
        outs, sm_out = refs[na + 1:2 * na + 1], refs[2 * na + 1]
        ici_send, ici_recv, fwd_send, fwd_recv, own_send, own_recv, loc = refs[2 * na + 2:]
        x, y, c, j, chips, blocks = _place()
        sib = (x, y, 1 - c)

        own = [pltpu.make_async_remote_copy(src_ref=ins[a], dst_ref=outs[a].at[j], send_sem=own_send.at[a],
                                            recv_sem=own_recv.at[a], device_id=sib, device_id_type=MESH) for a in range(na)]
        local = pltpu.make_async_copy(sm_in, sm_out.at[j], loc)
        local.start()
        for cp in own:
            cp.start()

        def ici(a, k, src):
            dst = outs[a].at[j, c] if src is not None else outs[a].at[blocks[k], c]
            return pltpu.make_async_remote_copy(
                src_ref=dst if src is None else src, dst_ref=dst, send_sem=ici_send.at[a * nk + k],
                recv_sem=ici_recv.at[a * nk + k], device_id=(*chips[k], c), device_id_type=MESH)

        def small_copy(k):
            return pltpu.make_async_remote_copy(
                src_ref=sm_in, dst_ref=sm_out.at[j], send_sem=ici_send.at[na * nk + k], recv_sem=ici_recv.at[na * nk + k],
                device_id=(*chips[k], c), device_id_type=MESH)

        def fwd(a, k, half):
            ref = outs[a].at[blocks[k], half]
            return pltpu.make_async_remote_copy(
                src_ref=ref, dst_ref=ref, send_sem=fwd_send.at[a * nk + k], recv_sem=fwd_recv.at[a * nk + k],
                device_id=sib, device_id_type=MESH)

        sends = [ici(a, k, ins[a].at[c]) for a in range(na) for k in range(nk)] + [small_copy(k) for k in range(nk)]
        for cp in sends:
            cp.start()
        passed = []
        for a in range(na):
            for k in range(nk):
                ici(a, k, None).wait_recv()
                passed.append(fwd(a, k, c))
                passed[-1].start()
        for k in range(nk):
            small_copy(k).wait_recv()
        for a in range(na):
            for k in range(nk):
                fwd(a, k, 1 - c).wait_recv()
        for cp in own:
            cp.wait()
        for cp in sends + passed:
            cp.wait_send()
        local.wait()

    out_shape = [jax.ShapeDtypeStruct((NCHIP,) + s.shape, s.dtype) for s in shards]
    out_shape.append(jax.ShapeDtypeStruct((NCHIP,) + small.shape, small.dtype))
    outs = pl.pallas_call(
        body, name="ag_weights", in_specs=[ANY] * (na + 1), out_specs=[ANY] * (na + 1), out_shape=out_shape,
        scratch_shapes=[pltpu.SemaphoreType.DMA((na * nk + nk,)), pltpu.SemaphoreType.DMA((na * nk + nk,)),
                        pltpu.SemaphoreType.DMA((na * nk,)), pltpu.SemaphoreType.DMA((na * nk,)),
                        pltpu.SemaphoreType.DMA((na,)), pltpu.SemaphoreType.DMA((na,)), pltpu.SemaphoreType.DMA],
    )(*shards, small)
    return outs[:na], outs[na]


def _rs_pair(grads):
    items = [(a, l) for a, g in enumerate(grads) for l in range(g.shape[0])]
    na = len(grads)

    def body(*refs):
        ins, outs, send_sem, recv_sem = refs[:na], refs[na:2 * na], refs[2 * na], refs[2 * na + 1]
        x, y, c, _, _, _ = _place()
        copies = [pltpu.make_async_remote_copy(
            src_ref=ins[a].at[l, :, 1 - c], dst_ref=outs[a].at[l], send_sem=send_sem.at[q], recv_sem=recv_sem.at[q],
            device_id=(x, y, 1 - c), device_id_type=MESH) for q, (a, l) in enumerate(items)]
        for cp in copies:
            cp.start()
        for cp in copies:
            cp.wait()

    return pl.pallas_call(
        body, name="rs_pair", in_specs=[ANY] * na, out_specs=[ANY] * na,
        out_shape=[jax.ShapeDtypeStruct(g.shape[:2] + g.shape[3:], F32) for g in grads],
        scratch_shapes=[pltpu.SemaphoreType.DMA((len(items),)), pltpu.SemaphoreType.DMA((len(items),))],
    )(*grads)


def _rs_chips(sums, small):
    items = [(a, l, k) for a, g in enumerate(sums) for l in range(g.shape[0]) for k in range(NCHIP - 1)]
    na = len(sums)
    n_peer = NDEV - 1

    def body(*refs):
        ins, sm_in = refs[:na], refs[na]
        outs, sm_out = refs[na + 1:2 * na + 1], refs[2 * na + 1]
        send_sem, recv_sem, sm_send, sm_recv, loc = refs[2 * na + 2:]
        x, y, c, _, chips, blocks = _place()
        me = 4 * x + 2 * y + c
        mine = pltpu.make_async_copy(sm_in, sm_out.at[me], loc)
        mine.start()
        copies = [pltpu.make_async_remote_copy(
            src_ref=ins[a].at[l, blocks[k]], dst_ref=outs[a].at[l, k], send_sem=send_sem.at[q], recv_sem=recv_sem.at[q],
            device_id=(*chips[k], c), device_id_type=MESH) for q, (a, l, k) in enumerate(items)]
        for r in range(1, NDEV):
            peer = (1 - x if r & 4 else x, 1 - y if r & 2 else y, 1 - c if r & 1 else c)
            copies.append(pltpu.make_async_remote_copy(
                src_ref=sm_in, dst_ref=sm_out.at[me], send_sem=sm_send.at[r - 1], recv_sem=sm_recv.at[r - 1],
                device_id=peer, device_id_type=MESH))
        for cp in copies:
            cp.start()
        for cp in copies:
            cp.wait()
        mine.wait()

    out_shape = [jax.ShapeDtypeStruct((g.shape[0], NCHIP - 1) + g.shape[2:], BF16) for g in sums]
    out_shape.append(jax.ShapeDtypeStruct((NDEV,) + small.shape, F32))
    outs = pl.pallas_call(
        body, name="rs_chips", in_specs=[ANY] * (na + 1), out_specs=[ANY] * (na + 1), out_shape=out_shape,
        scratch_shapes=[pltpu.SemaphoreType.DMA((len(items),)), pltpu.SemaphoreType.DMA((len(items),)),
                        pltpu.SemaphoreType.DMA((n_peer,)), pltpu.SemaphoreType.DMA((n_peer,)), pltpu.SemaphoreType.DMA],
    )(*sums, small)
    return outs[:na], outs[na]


def _rs_swap(totals):
    na = len(totals)

    def body(*refs):
        outs, send_sem, recv_sem = refs[na:2 * na], refs[2 * na], refs[2 * na + 1]
        x, y, c, _, _, _ = _place()
        copies = [pltpu.make_async_remote_copy(
            src_ref=outs[a].at[:, c], dst_ref=outs[a].at[:, c], send_sem=send_sem.at[a], recv_sem=recv_sem.at[a],
            device_id=(x, y, 1 - c), device_id_type=MESH) for a in range(na)]
        for cp in copies:
            cp.start()
        for cp in copies:
            cp.wait()

    return pl.pallas_call(
        body, name="rs_swap", in_specs=[ANY] * na, out_specs=[ANY] * na,
        out_shape=[jax.ShapeDtypeStruct(t.shape, F32) for t in totals], input_output_aliases={a: a for a in range(na)},
        scratch_shapes=[pltpu.SemaphoreType.DMA((na,)), pltpu.SemaphoreType.DMA((na,))],
    )(*totals)


def _pair_sum(place, grad, landed, name):
    nl, _, _, rows, cols = grad.shape

    def body(place_ref, g_ref, l_ref, o_ref):
        o_ref[...] = (g_ref[...] + l_ref[...]).astype(BF16)

    return pl.pallas_call(
        body, name=name,
        grid_spec=pltpu.PrefetchScalarGridSpec(
            num_scalar_prefetch=1, grid=(nl, NCHIP),
            in_specs=[pl.BlockSpec((None, None, None, rows, cols), lambda l, b, p: (l, b, p[1], 0, 0)),
                      pl.BlockSpec((None, None, rows, cols), lambda l, b, p: (l, b, 0, 0))],
            out_specs=pl.BlockSpec((None, None, rows, cols), lambda l, b, p: (l, b, 0, 0))),
        out_shape=jax.ShapeDtypeStruct((nl, NCHIP, rows, cols), BF16), compiler_params=_cparams(2),
    )(place, grad, landed)


def _chip_sum(place, grad, landed, parts, name):
    nl, _, _, rows, cols = grad.shape

    def body(place_ref, g_ref, l_ref, p_ref, o_ref):
        tot = g_ref[...] + l_ref[...]
        for k in range(NCHIP - 1):
            tot = tot + p_ref[k].astype(F32)
        o_ref[...] = tot

    return pl.pallas_call(
        body, name=name,
        grid_spec=pltpu.PrefetchScalarGridSpec(
            num_scalar_prefetch=1, grid=(nl,),
            in_specs=[pl.BlockSpec((None, None, None, rows, cols), lambda l, p: (l, p[0], p[1], 0, 0)),
                      pl.BlockSpec((None, None, rows, cols), lambda l, p: (l, p[0], 0, 0)),
                      pl.BlockSpec((None, NCHIP - 1, rows, cols), lambda l, p: (l, 0, 0, 0))],
            out_specs=pl.BlockSpec((None, None, rows, cols), lambda l, p: (l, p[1], 0, 0))),
        out_shape=jax.ShapeDtypeStruct((nl, 2, rows, cols), F32), compiler_params=_cparams(1),
    )(place, grad, landed, parts)


def _sum_devices(parts):
    def body(p_ref, o_ref):
        tot = p_ref[0]
        for d in range(1, NDEV):
            tot = tot + p_ref[d]
        o_ref[...] = tot

    return pl.pallas_call(
        body, name="sum_small", in_specs=[pl.BlockSpec(parts.shape, lambda: (0, 0, 0))],
        out_specs=pl.BlockSpec(parts.shape[1:], lambda: (0, 0)), out_shape=jax.ShapeDtypeStruct(parts.shape[1:], F32),
    )(parts)


def _pack(parts):
    rows = []
    for p in parts:
        p = p.reshape(-1, LANES)
        rows.append(jnp.pad(p, ((0, -p.shape[0] % SUBLANES), (0, 0))))
    return jnp.concatenate(rows, axis=0)


def _unpack(buf, shapes, lead=()):
    out, r0 = [], 0
    nl = len(lead)
    for shp in shapes:
        nrow = 1
        for d in shp:
            nrow *= d
        nrow //= LANES
        out.append(buf[(slice(None),) * nl + (slice(r0, r0 + nrow),)].reshape(lead + tuple(shp)))
        r0 += nrow + (-nrow % SUBLANES)
    return out


WEIGHT_ORDER = ("norm_mix_even", "w_in", "conv_a", "ln_a_g", "ln_a_b", "conv_b", "w_out", "norm_mix_odd", "w_pool",
                "pool_scale", "norm_ffn", "w_up", "conv_ffn_w", "w_down", "norm_final")
BIG = ("w_in", "w_out", "w_pool", "w_up", "w_down")
HALF = {"w_in": (D // 2, NZ // NCHIP), "w_out": (2 * A // NCHIP // 2, D), "w_pool": (PG // 2, PG),
        "w_up": (D // 2, FF2 // NCHIP), "w_down": (FF // NCHIP // 2, D)}
SMALL_SHARDED = ("conv_a", "conv_b", "conv_ffn_w", "norm_mix_odd", "pool_scale")
SMALL_ALL = ("norm_mix_even", "conv_a", "ln_a_g", "ln_a_b", "conv_b", "norm_mix_odd", "pool_scale", "norm_ffn", "conv_ffn_w",
             "norm_final")


def kernel(x, norm_mix_even, w_in, conv_a, ln_a_g, ln_a_b, conv_b, w_out, norm_mix_odd, w_pool, pool_scale, norm_ffn, w_up, conv_ffn_w, w_down, norm_final, loss_target, m_norm_mix_even, m_w_in, m_conv_a, m_ln_a_g, m_ln_a_b, m_conv_b, m_w_out, m_norm_mix_odd, m_w_pool, m_pool_scale, m_norm_ffn, m_w_up, m_conv_ffn_w, m_w_down, m_norm_final, v_norm_mix_even, v_w_in, v_conv_a, v_ln_a_g, v_ln_a_b, v_conv_b, v_w_out, v_norm_mix_odd, v_w_pool, v_pool_scale, v_norm_ffn, v_w_up, v_conv_ffn_w, v_w_down, v_norm_final):
    w = dict(norm_mix_even=norm_mix_even, w_in=w_in, conv_a=conv_a, ln_a_g=ln_a_g, ln_a_b=ln_a_b, conv_b=conv_b, w_out=w_out,
             norm_mix_odd=norm_mix_odd, w_pool=w_pool, pool_scale=pool_scale, norm_ffn=norm_ffn, w_up=w_up,
             conv_ffn_w=conv_ffn_w, w_down=w_down, norm_final=norm_final)
    m = dict(norm_mix_even=m_norm_mix_even, w_in=m_w_in, conv_a=m_conv_a, ln_a_g=m_ln_a_g, ln_a_b=m_ln_a_b, conv_b=m_conv_b,
             w_out=m_w_out, norm_mix_odd=m_norm_mix_odd, w_pool=m_w_pool, pool_scale=m_pool_scale, norm_ffn=m_norm_ffn,
             w_up=m_w_up, conv_ffn_w=m_conv_ffn_w, w_down=m_w_down, norm_final=m_norm_final)
    v = dict(norm_mix_even=v_norm_mix_even, w_in=v_w_in, conv_a=v_conv_a, ln_a_g=v_ln_a_g, ln_a_b=v_ln_a_b, conv_b=v_conv_b,
             w_out=v_w_out, norm_mix_odd=v_norm_mix_odd, w_pool=v_w_pool, pool_scale=v_pool_scale, norm_ffn=v_norm_ffn,
             w_up=v_w_up, conv_ffn_w=v_conv_ffn_w, w_down=v_w_down, norm_final=v_norm_final)
    chip = 2 * lax.axis_index("x") + lax.axis_index("y")
    place = jnp.stack([chip, lax.axis_index("c")]).astype(jnp.int32)

    half = lambda a, name: a.astype(BF16).reshape((2,) + HALF[name])
    shards = [half(w_in[0], "w_in"), half(w_out[0], "w_out"), half(w_up[0], "w_up"), half(w_down[0], "w_down"),
              half(w_pool[0], "w_pool"), half(w_up[1], "w_up"), half(w_down[1], "w_down")]
    small_shapes = [w[k].shape[-2:] if w[k].ndim == 3 and k != "conv_ffn_w" else (w[k].size // w[k].shape[-1], w[k].shape[-1])
                    for k in SMALL_SHARDED]
    (g_in, g_out, g_up0, g_dn0, g_pool, g_up1, g_dn1), small_g = _ag_weights(shards, _pack([w[k] for k in SMALL_SHARDED]))
    whole = {}
    for k, part in zip(SMALL_SHARDED, _unpack(small_g, small_shapes, lead=(NCHIP,))):
        whole[k] = jnp.moveaxis(part, 0, 1).reshape(part.shape[1], NCHIP * part.shape[2])
    wts = dict(
        w_in=g_in.reshape(NCHIP, D, NZ // NCHIP), w_out=g_out.reshape(2 * A, D),
        w_pool=g_pool.reshape(NCHIP, len(POOL_WINDOWS), PG // NCHIP, PG),
        w_up=[g_up0.reshape(NCHIP, D, FF2 // NCHIP), g_up1.reshape(NCHIP, D, FF2 // NCHIP)],
        w_down=[g_dn0.reshape(FF, D), g_dn1.reshape(FF, D)],
        norm_mix_even=norm_mix_even, conv_a=whole["conv_a"], ln_a_g=ln_a_g, ln_a_b=ln_a_b, conv_b=whole["conv_b"],
        norm_mix_odd=whole["norm_mix_odd"], pool_scale=whole["pool_scale"], norm_ffn=norm_ffn,
        conv_ffn_w=whole["conv_ffn_w"].reshape(2, K_S, FF2), norm_final=norm_final.reshape(1, D))

    loss_part, grad_x, big, small = _local_step(x[0], loss_target[0], wts)

    grads5 = [big[k].reshape(big[k].shape[0], NCHIP, 2, *HALF[k]) for k in BIG]
    landed = _rs_pair(grads5)
    sums = [_pair_sum(place, g5, ld, "pair_sum_" + k) for k, g5, ld in zip(BIG, grads5, landed)]
    small_full_shapes = [small[k].shape for k in SMALL_ALL] + [(1, LANES)]
    parts, small_all = _rs_chips(sums, _pack([small[k] for k in SMALL_ALL] + [loss_part]))
    totals = [_chip_sum(place, g5, ld, p, "chip_sum_" + k) for k, g5, ld, p in zip(BIG, grads5, landed, parts)]
    swapped = _rs_swap(totals)
    small_sum = _unpack(_sum_devices(small_all), small_full_shapes)
    loss = small_sum[-1][0, 0]

    grad = {k: s.reshape(w[k].shape) for k, s in zip(BIG, swapped)}
    for k, gsum in zip(SMALL_ALL, small_sum):
        if k in SMALL_SHARDED:
            cols = w[k].shape[-1]
            gsum = lax.dynamic_slice_in_dim(gsum, chip * cols, cols, axis=gsum.ndim - 1)
        grad[k] = gsum.reshape(w[k].shape)

    delta, new_m, new_v = {}, {}, {}
    rows_per_step = {"w_in": 512, "w_out": 256, "w_pool": 256, "w_up": 256, "w_down": 352}
    for k in BIG:
        as3 = lambda a: a.reshape(a.shape[0], -1, a.shape[-1])
        d3, m3, v3 = _adamw_big(as3(w[k]), as3(grad[k]), as3(m[k]), as3(v[k]), rows_per_step[k], "adamw_" + k)
        delta[k], new_m[k], new_v[k] = (a.reshape(w[k].shape) for a in (d3, m3, v3))
    as2 = lambda a: a.reshape(-1, a.shape[-1])
    ds, ms, vs = _adamw_small(*[[as2(t[k]) for k in SMALL_ALL] for t in (w, grad, m, v)])
    for k, d2, m2, v2 in zip(SMALL_ALL, ds, ms, vs):
        delta[k], new_m[k], new_v[k] = (a.reshape(w[k].shape) for a in (d2, m2, v2))

    return (loss, grad_x[None], *[grad[k] for k in WEIGHT_ORDER], *[delta[k] for k in WEIGHT_ORDER],
            *[new_m[k] for k in WEIGHT_ORDER], *[new_v[k] for k in WEIGHT_ORDER])
```

```python
import functools

import jax
import jax.numpy as jnp
from jax import lax
from jax.experimental import pallas as pl
from jax.experimental.pallas import tpu as pltpu

F32, BF16 = jnp.float32, jnp.bfloat16

D = 1024
A = 512
NZ = 5 * A
FF = 2816
FF2 = 2 * FF
NCHIP = 4
NDEV = 8
K_A, K_S = 31, 3
POOL_WINDOWS = (2, 4, 8, 16)
PG = D // len(POOL_WINDOWS)
RMS_EPS, LN_EPS = 1e-6, 1e-5
ADAM_LR, ADAM_B1, ADAM_B2, ADAM_EPS, ADAM_WD, ADAM_STEP = 0.001, 0.9, 0.999, 1e-08, 0.01, 10

HALO_A, HALO_S, HALO_P = 32, 8, 16
SUBLANES = 8
LANES = 128
VMEM_LIMIT_BYTES = 56 * 1024 * 1024

TS_MIX = 512
TS_MIXB = 256
TS_FFN = 256
TS_POOL = 512
TS_MM = 512
R_CHUNK = 64

MESH = pl.DeviceIdType.MESH
ANY = pl.BlockSpec(memory_space=pl.ANY)
NT_DIMS = (((1,), (1,)), ((), ()))
TN_DIMS = (((0,), (0,)), ((), ()))


def _cparams(n_axes):
    return pltpu.CompilerParams(dimension_semantics=("arbitrary",) * n_axes, vmem_limit_bytes=VMEM_LIMIT_BYTES)


def _const(shape):
    nd = len(shape)
    return pl.BlockSpec(shape, lambda *_: (0,) * nd, pipeline_mode=pl.Buffered(1))


def _sigmoid(v):
    return 1.0 / (1.0 + jnp.exp(-v))


def _rsqrt_mean_sq(x):
    return lax.rsqrt(jnp.mean(x * x, axis=-1, keepdims=True) + RMS_EPS)


def _rms_bwd(dh, xh, r, g):
    dxh = dh * g
    return r * (dxh - xh * jnp.mean(dxh * xh, axis=-1, keepdims=True))


def _shifted(buf_ref, row0, rows, col0, width, offsets):
    lo = (min(offsets) // SUBLANES) * SUBLANES
    hi = -(-(max(offsets) + rows) // SUBLANES) * SUBLANES
    win = buf_ref[pl.ds(pl.multiple_of(row0 + lo, SUBLANES), hi - lo), col0:col0 + width]
    out = {}
    for res in sorted({(o - lo) % SUBLANES for o in offsets}):
        qs = {o: (o - lo) // SUBLANES for o in offsets if (o - lo) % SUBLANES == res}
        base = win[res:res + rows + SUBLANES * max(qs.values()), :]
        for o, q in qs.items():
            out[o] = base[SUBLANES * q:SUBLANES * q + rows, :]
    return out


def _rowsum8(v):
    acc = v[0:SUBLANES, :]
    for r in range(SUBLANES, v.shape[0], SUBLANES):
        acc = acc + v[r:r + SUBLANES, :]
    return acc


def _taps(w_ref, sh, offsets, col0, width):
    acc = None
    for k, o in enumerate(offsets):
        term = w_ref[k:k + 1, col0:col0 + width] * sh[o]
        acc = term if acc is None else acc + term
    return acc


def _finish_tap_sums(acc_ref, out_ref, n_taps):
    for k in range(n_taps):
        out_ref[k:k + 1, :] = jnp.sum(acc_ref[SUBLANES * k:SUBLANES * (k + 1), :], axis=0, keepdims=True)


class _Comm:
    def __init__(self, inputs, out_shapes, sems, start, finish, aliases=None):
        self.inputs, self.out_shapes, self.sems = list(inputs), list(out_shapes), list(sems)
        self.start, self.finish, self.aliases = start, finish, dict(aliases or {})


def _call(body, *, name, grid, in_specs, out_specs, out_shape, args, scratch=(), comm=None):
    n_in, n_out, n_scr, n_axes = len(in_specs), len(out_specs), len(scratch), len(grid)
    params = pltpu.CompilerParams(dimension_semantics=("arbitrary",) * n_axes, vmem_limit_bytes=VMEM_LIMIT_BYTES)
    if comm is None:
        outs = pl.pallas_call(body, name=name, grid=grid, in_specs=list(in_specs), out_specs=list(out_specs),
                              out_shape=list(out_shape), scratch_shapes=list(scratch), compiler_params=params)(*args)
        return list(outs), []
    ci, co = len(comm.inputs), len(comm.out_shapes)

    def wrapped(*refs):
        k_in, c_in = refs[:n_in], refs[n_in:n_in + ci]
        o0 = n_in + ci
        k_out, c_out = refs[o0:o0 + n_out], refs[o0 + n_out:o0 + n_out + co]
        s0 = o0 + n_out + co
        k_scr, c_sem = refs[s0:s0 + n_scr], refs[s0 + n_scr:]
        first = pl.program_id(0) == 0
        last = pl.program_id(0) == grid[0] - 1
        for ax in range(1, n_axes):
            first = jnp.logical_and(first, pl.program_id(ax) == 0)
            last = jnp.logical_and(last, pl.program_id(ax) == grid[ax] - 1)

        @pl.when(first)
        def _():
            comm.start(c_in, c_out, c_sem)

        body(*k_in, *k_out, *k_scr)

        @pl.when(last)
        def _():
            comm.finish(c_in, c_out, c_sem)

    outs = pl.pallas_call(
        wrapped, name=name, grid=grid, in_specs=list(in_specs) + [ANY] * ci, out_specs=list(out_specs) + [ANY] * co,
        out_shape=list(out_shape) + comm.out_shapes, scratch_shapes=list(scratch) + comm.sems,
        input_output_aliases={n_in + i: n_out + o for i, o in comm.aliases.items()}, compiler_params=params,
    )(*args, *comm.inputs)
    return list(outs[:n_out]), list(outs[n_out:])


def _run_comm(comm, name):
    ci, co = len(comm.inputs), len(comm.out_shapes)

    def body(*refs):
        c_in, c_out, c_sem = refs[:ci], refs[ci:ci + co], refs[ci + co:]
        comm.start(c_in, c_out, c_sem)
        comm.finish(c_in, c_out, c_sem)

    return list(pl.pallas_call(body, name=name, in_specs=[ANY] * ci, out_specs=[ANY] * co, out_shape=comm.out_shapes,
                               scratch_shapes=comm.sems, input_output_aliases=comm.aliases)(*comm.inputs))


def _mix0_fwd(x, g, w_in, conv_a, ln_g, ln_b, conv_b, w_out, comm=None):
    s = x.shape[0]
    ts = min(TS_MIX, s)
    n = s // ts
    bw = NZ // NCHIP
    offs_a = [HALO_A - (K_A - 1) + k for k in range(K_A)]
    offs_s = [HALO_S - (K_S - 1) + k for k in range(K_S)]

    def body(x_ref, g_ref, win_ref, ca_ref, lg_ref, lb_ref, cb_ref, wout_ref,
             h_ref, z_ref, ac_ref, cat_ref, x1_ref, glu_buf, cv_buf, bconv_buf):
        i = pl.program_id(0)

        @pl.when(i == 0)
        def _():
            glu_buf[0:HALO_A, :] = jnp.zeros((HALO_A, A), F32)
            cv_buf[0:HALO_S, :] = jnp.zeros((HALO_S, A), F32)

        xv = x_ref[...]
        h = (xv * _rsqrt_mean_sq(xv) * g_ref[...]).astype(BF16)
        h_ref[...] = h
        for j in range(NCHIP):
            z_ref[:, j * bw:(j + 1) * bw] = jnp.dot(h, win_ref[j], preferred_element_type=F32)
        glu_buf[HALO_A:HALO_A + ts, :] = z_ref[:, 0:A] * _sigmoid(z_ref[:, A:2 * A])
        cv_buf[HALO_S:HALO_S + ts, :] = z_ref[:, 3 * A:4 * A] * z_ref[:, 4 * A:5 * A]

        def chunk(ci, carry):
            r0 = pl.multiple_of(ci * R_CHUNK, R_CHUNK)
            for c0 in range(0, A, LANES):
                sh = _shifted(glu_buf, r0, R_CHUNK, c0, LANES, offs_a)
                ac_ref[pl.ds(r0, R_CHUNK), c0:c0 + LANES] = _taps(ca_ref, sh, offs_a, c0, LANES)
                sh = _shifted(cv_buf, r0, R_CHUNK, c0, LANES, offs_s)
                bconv_buf[pl.ds(r0, R_CHUNK), c0:c0 + LANES] = _taps(cb_ref, sh, offs_s, c0, LANES)
            return carry

        lax.fori_loop(0, ts // R_CHUNK, chunk, 0)
        glu_buf[0:HALO_A, :] = glu_buf[ts:ts + HALO_A, :]
        cv_buf[0:HALO_S, :] = cv_buf[ts:ts + HALO_S, :]

        ac = ac_ref[...]
        xc = ac - jnp.mean(ac, axis=-1, keepdims=True)
        xn = xc * lax.rsqrt(jnp.mean(xc * xc, axis=-1, keepdims=True) + LN_EPS)
        ln = xn * lg_ref[...] + lb_ref[...]
        cat_ref[:, 0:A] = (ln * _sigmoid(ln)).astype(BF16)
        cat_ref[:, A:2 * A] = (z_ref[:, 2 * A:3 * A] * bconv_buf[...]).astype(BF16)
        x1_ref[...] = xv + jnp.dot(cat_ref[...], wout_ref[...], preferred_element_type=F32)

    tile = lambda w: pl.BlockSpec((ts, w), lambda i: (i, 0))
    return _call(
        body, name="mix0_fwd", grid=(n,), comm=comm, args=(x, g, w_in, conv_a, ln_g, ln_b, conv_b, w_out),
        in_specs=[tile(D), _const((1, D)), _const((NCHIP, D, bw)), _const((K_A, A)), _const((1, A)), _const((1, A)),
                  _const((K_S, A)), _const((2 * A, D))],
        out_specs=[tile(D), tile(NZ), tile(A), tile(2 * A), tile(D)],
        out_shape=[jax.ShapeDtypeStruct((s, D), BF16), jax.ShapeDtypeStruct((s, NZ), F32), jax.ShapeDtypeStruct((s, A), F32),
                   jax.ShapeDtypeStruct((s, 2 * A), BF16), jax.ShapeDtypeStruct((s, D), F32)],
        scratch=[pltpu.VMEM((HALO_A + ts, A), F32), pltpu.VMEM((HALO_S + ts, A), F32), pltpu.VMEM((ts, A), F32)])


def _ffn_fwd(x, g, w_up, wc, w_down, name, comm=None):
    s = x.shape[0]
    ts = min(TS_FFN, s)
    n = s // ts
    bw = FF2 // NCHIP
    lw = 2 * LANES
    rows = 32
    offs = [HALO_S - (K_S - 1) + k for k in range(K_S)]

    def body(x_ref, g_ref, wup_ref, wc_ref, wdn_ref, h_ref, u0_ref, act_ref, xo_ref, cbuf):
        i = pl.program_id(0)

        @pl.when(i == 0)
        def _():
            cbuf[0:HALO_S, :] = jnp.zeros((HALO_S, FF2), F32)

        xv = x_ref[...]
        h = (xv * _rsqrt_mean_sq(xv) * g_ref[...]).astype(BF16)
        h_ref[...] = h
        for j in range(NCHIP):
            zc = jnp.dot(h, wup_ref[j], preferred_element_type=F32)
            u0_ref[:, j * bw:(j + 1) * bw] = zc
            cbuf[HALO_S:HALO_S + ts, j * bw:(j + 1) * bw] = zc

        def chunk(ci, carry):
            r0 = pl.multiple_of(ci * rows, rows)
            for c0 in range(0, FF, lw):
                ug = _taps(wc_ref, _shifted(cbuf, r0, rows, c0, lw, offs), offs, c0, lw)
                uv = _taps(wc_ref, _shifted(cbuf, r0, rows, FF + c0, lw, offs), offs, FF + c0, lw)
                act_ref[pl.ds(r0, rows), c0:c0 + lw] = (ug * _sigmoid(ug) * uv).astype(BF16)
            return carry

        lax.fori_loop(0, ts // rows, chunk, 0)
        cbuf[0:HALO_S, :] = cbuf[ts:ts + HALO_S, :]
        xo_ref[...] = xv + jnp.dot(act_ref[...], wdn_ref[...], preferred_element_type=F32)

    tile = lambda w: pl.BlockSpec((ts, w), lambda i: (i, 0))
    return _call(
        body, name=name, grid=(n,), comm=comm, args=(x, g, w_up, wc, w_down),
        in_specs=[tile(D), _const((1, D)), _const((NCHIP, D, bw)), _const((K_S, FF2)), _const((FF, D))],
        out_specs=[tile(D), tile(FF2), tile(FF), tile(D)],
        out_shape=[jax.ShapeDtypeStruct((s, D), BF16), jax.ShapeDtypeStruct((s, FF2), F32),
                   jax.ShapeDtypeStruct((s, FF), BF16), jax.ShapeDtypeStruct((s, D), F32)],
        scratch=[pltpu.VMEM((HALO_S + ts, FF2), F32)])


def _pool_windows(hbuf, pbuf, tile_row0, ts):
    def chunk(ci, carry):
        r0 = pl.multiple_of(ci * R_CHUNK, R_CHUNK)
        t1 = (tile_row0 + r0 + lax.broadcasted_iota(jnp.int32, (R_CHUNK, 1), 0) + 1).astype(F32)
        for gi, w in enumerate(POOL_WINDOWS):
            cnt = jnp.minimum(t1, float(w))
            offs = [HALO_P - jj for jj in range(w)]
            for c0 in range(gi * PG, (gi + 1) * PG, LANES):
                sh = _shifted(hbuf, r0, R_CHUNK, c0, LANES, offs)
                tot = sh[offs[0]]
                for o in offs[1:]:
                    tot = tot + sh[o]
                pbuf[pl.ds(r0, R_CHUNK), c0:c0 + LANES] = (tot / cnt - sh[HALO_P]).astype(BF16)
        return carry

    lax.fori_loop(0, ts // R_CHUNK, chunk, 0)


def _assemble_wpool(wp_ref, wps):
    rb = PG // NCHIP
    for gi in range(len(POOL_WINDOWS)):
        for j in range(NCHIP):
            wps[gi, j * rb:(j + 1) * rb, :] = wp_ref[j, gi]


def _pool_fwd(x, g, w_pool, scale):
    s = x.shape[0]
    ts = min(TS_POOL, s)
    n = s // ts
    ng = len(POOL_WINDOWS)

    def body(x_ref, g_ref, wp_ref, sc_ref, xo_ref, hbuf, pbuf, wps):
        i = pl.program_id(0)

        @pl.when(i == 0)
        def _():
            hbuf[0:HALO_P, :] = jnp.zeros((HALO_P, D), F32)
            _assemble_wpool(wp_ref, wps)

        xv = x_ref[...]
        hbuf[HALO_P:HALO_P + ts, :] = xv * _rsqrt_mean_sq(xv) * g_ref[...]
        _pool_windows(hbuf, pbuf, i * ts, ts)
        hbuf[0:HALO_P, :] = hbuf[ts:ts + HALO_P, :]
        for gi in range(ng):
            cols = slice(gi * PG, (gi + 1) * PG)
            y = jnp.dot(pbuf[:, cols], wps[gi], preferred_element_type=F32)
            xo_ref[:, cols] = xv[:, cols] + y * sc_ref[:, cols]

    tile = pl.BlockSpec((ts, D), lambda i: (i, 0))
    return pl.pallas_call(
        body, name="pool_fwd", grid=(n,),
        in_specs=[tile, _const((1, D)), _const((NCHIP, ng, PG // NCHIP, PG)), _const((1, D))],
        out_specs=tile, out_shape=jax.ShapeDtypeStruct((s, D), F32),
        scratch_shapes=[pltpu.VMEM((HALO_P + ts, D), F32), pltpu.VMEM((ts, D), BF16), pltpu.VMEM((ng, PG, PG), BF16)],
        compiler_params=_cparams(1),
    )(x, g, w_pool, scale)


def _final_loss(x, g, target):
    s = x.shape[0]
    ts = min(TS_MM, s)
    n = s // ts

    def body(x_ref, g_ref, t_ref, dx_ref, dg_ref, loss_ref):
        i = pl.program_id(0)

        @pl.when(i == 0)
        def _():
            dg_ref[...] = jnp.zeros((1, D), F32)
            loss_ref[...] = jnp.zeros((1, LANES), F32)

        xv = x_ref[...]
        r = _rsqrt_mean_sq(xv)
        xh = xv * r
        gv = g_ref[...]
        err = xh * gv - t_ref[...]
        sq = jnp.sum(jnp.sum(err * err, axis=1, keepdims=True), axis=0, keepdims=True)
        loss_ref[...] += sq * (0.5 / D)
        dy = err * (1.0 / D)
        dg_ref[...] += jnp.sum(dy * xh, axis=0, keepdims=True)
        dx_ref[...] = _rms_bwd(dy, xh, r, gv)

    tile = pl.BlockSpec((ts, D), lambda i: (i, 0))
    return pl.pallas_call(
        body, name="final_loss", grid=(n,),
        in_specs=[tile, _const((1, D)), tile],
        out_specs=[tile, pl.BlockSpec((1, D), lambda i: (0, 0)), pl.BlockSpec((1, LANES), lambda i: (0, 0))],
        out_shape=[jax.ShapeDtypeStruct((s, D), F32), jax.ShapeDtypeStruct((1, D), F32), jax.ShapeDtypeStruct((1, LANES), F32)],
        compiler_params=_cparams(1),
    )(x, g, target)


def _ffn_bwd_a(dxo, u0, wc, w_down, name, comm=None):
    s = dxo.shape[0]
    ts = min(TS_FFN, s)
    n = s // ts
    cw = FF2 // NCHIP
    rows = 32
    lw2 = 2 * LANES
    offs = [HALO_S - (K_S - 1) + k for k in range(K_S)]
    boffs = [K_S - 1 - k for k in range(K_S)]

    def body(dxo_ref, u0_ref, halo_ref, wc_ref, wdn_ref, du0_ref, dwc_ref, ubuf, dubuf, dact, dwacc):
        i = pl.program_id(0)
        j = n - 1 - i

        @pl.when(i == 0)
        def _():
            dubuf[ts:ts + HALO_S, :] = jnp.zeros((HALO_S, FF2), F32)
            dwacc[...] = jnp.zeros(dwacc.shape, F32)

        ubuf[0:HALO_S, :] = jnp.where(j == 0, 0.0, halo_ref[...])
        ubuf[HALO_S:HALO_S + ts, :] = u0_ref[...]
        df = dxo_ref[...].astype(BF16)
        for cg in range(0, FF, cw):
            dact[...] = lax.dot_general(df, wdn_ref[cg:cg + cw, :], NT_DIMS, preferred_element_type=F32)

            def chunk(ci, carry, cg=cg):
                r0 = pl.multiple_of(ci * rows, rows)
                rs = pl.ds(r0, rows)
                for c in range(cg, cg + cw, LANES):
                    shg = _shifted(ubuf, r0, rows, c, LANES, offs)
                    shv = _shifted(ubuf, r0, rows, FF + c, LANES, offs)
                    ug = _taps(wc_ref, shg, offs, c, LANES)
                    uv = _taps(wc_ref, shv, offs, FF + c, LANES)
                    sg = _sigmoid(ug)
                    da = dact[rs, c - cg:c - cg + LANES]
                    dgv = da * uv * sg * (1.0 + ug * (1.0 - sg))
                    dvv = da * ug * sg
                    dubuf[rs, c:c + LANES] = dgv
                    dubuf[rs, FF + c:FF + c + LANES] = dvv
                    for k, o in enumerate(offs):
                        ks = slice(SUBLANES * k, SUBLANES * (k + 1))
                        dwacc[ks, c:c + LANES] += _rowsum8(dgv * shg[o])
                        dwacc[ks, FF + c:FF + c + LANES] += _rowsum8(dvv * shv[o])
                return carry

            lax.fori_loop(0, ts // rows, chunk, 0)

        def chunk2(ci, carry):
            r0 = pl.multiple_of(ci * rows, rows)
            for c in range(0, FF2, lw2):
                sh = _shifted(dubuf, r0, rows, c, lw2, boffs)
                du0_ref[pl.ds(r0, rows), c:c + lw2] = _taps(wc_ref, sh, boffs, c, lw2).astype(BF16)
            return carry

        lax.fori_loop(0, ts // rows, chunk2, 0)
        dubuf[ts:ts + HALO_S, :] = dubuf[0:HALO_S, :]

        @pl.when(i == n - 1)
        def _():
            _finish_tap_sums(dwacc, dwc_ref, K_S)

    rev = lambda w: pl.BlockSpec((ts, w), lambda i: (n - 1 - i, 0))
    halo = pl.BlockSpec((HALO_S, FF2), lambda i: (jnp.maximum((n - 1 - i) * (ts // HALO_S) - 1, 0), 0))
    return _call(
        body, name=name, grid=(n,), comm=comm, args=(dxo, u0, u0, wc, w_down),
        in_specs=[rev(D), rev(FF2), halo, _const((K_S, FF2)), _const((FF, D))],
        out_specs=[rev(FF2), pl.BlockSpec((K_S, FF2), lambda i: (0, 0))],
        out_shape=[jax.ShapeDtypeStruct((s, FF2), BF16), jax.ShapeDtypeStruct((K_S, FF2), F32)],
        scratch=[pltpu.VMEM((HALO_S + ts, FF2), F32), pltpu.VMEM((ts + HALO_S, FF2), F32), pltpu.VMEM((ts, cw), F32),
                 pltpu.VMEM((SUBLANES * K_S, FF2), F32)])


def _nt_rms_bwd(dy, w, x, g, dres, name, comm=None):
    s = x.shape[0]
    ts = min(TS_MM, s)
    n = s // ts
    nw = dy.shape[1]
    bw = nw // NCHIP

    def body(dy_ref, w_ref, x_ref, g_ref, dres_ref, dx_ref, dg_ref):
        i = pl.program_id(0)

        @pl.when(i == 0)
        def _():
            dg_ref[...] = jnp.zeros((1, D), F32)

        dh = lax.dot_general(dy_ref[:, 0:bw], w_ref[0], NT_DIMS, preferred_element_type=F32)
        for j in range(1, NCHIP):
            dh = dh + lax.dot_general(dy_ref[:, j * bw:(j + 1) * bw], w_ref[j], NT_DIMS, preferred_element_type=F32)
        xv = x_ref[...]
        r = _rsqrt_mean_sq(xv)
        xh = xv * r
        dg_ref[...] += jnp.sum(dh * xh, axis=0, keepdims=True)
        dx_ref[...] = dres_ref[...] + _rms_bwd(dh, xh, r, g_ref[...])

    tile = lambda wd: pl.BlockSpec((ts, wd), lambda i: (i, 0))
    return _call(
        body, name=name, grid=(n,), comm=comm, args=(dy, w, x, g, dres),
        in_specs=[tile(nw), _const((NCHIP, D, bw)), tile(D), _const((1, D)), tile(D)],
        out_specs=[tile(D), pl.BlockSpec((1, D), lambda i: (0, 0))],
        out_shape=[jax.ShapeDtypeStruct((s, D), F32), jax.ShapeDtypeStruct((1, D), F32)])


def _wgrad(a, b, bn, out_shape, out_block, out_index, name, comm=None):
    s, m = a.shape
    nb = b.shape[1] // bn
    tk = min(TS_MM, s)
    nk = s // tk

    def body(a_ref, b_ref, o_ref):
        k = pl.program_id(1)

        @pl.when(k == 0)
        def _():
            o_ref[...] = jnp.zeros(o_ref.shape, F32)

        o_ref[...] += lax.dot_general(a_ref[...], b_ref[...].astype(BF16), TN_DIMS, preferred_element_type=F32)

    (out,), comm_out = _call(
        body, name=name, grid=(nb, nk), comm=comm, args=(a, b),
        in_specs=[pl.BlockSpec((tk, m), lambda jn, k: (k, 0)), pl.BlockSpec((tk, bn), lambda jn, k: (k, jn))],
        out_specs=[pl.BlockSpec(out_block, lambda jn, k: out_index(jn))], out_shape=[jax.ShapeDtypeStruct(out_shape, F32)])
    return out, comm_out


def _pool_bwd(dxo, x, g, w_pool, scale):
    s = x.shape[0]
    ts = min(TS_POOL, s)
    n = s // ts
    ng = len(POOL_WINDOWS)
    rb = PG // NCHIP

    def body(dxo_ref, x_ref, halo_ref, g_ref, wp_ref, sc_ref, dx_ref, dwp_ref, dsc_ref, dg_ref,
             hbuf, pbuf, qbuf, dhbuf, wps, dwacc):
        i = pl.program_id(0)
        j = n - 1 - i

        @pl.when(i == 0)
        def _():
            qbuf[ts:ts + HALO_P, :] = jnp.zeros((HALO_P, D), F32)
            dwacc[...] = jnp.zeros(dwacc.shape, F32)
            dsc_ref[...] = jnp.zeros((1, D), F32)
            dg_ref[...] = jnp.zeros((1, D), F32)
            _assemble_wpool(wp_ref, wps)

        gv = g_ref[...]
        xl = halo_ref[...]
        hbuf[0:HALO_P, :] = jnp.where(j == 0, 0.0, xl * _rsqrt_mean_sq(xl) * gv)
        xv = x_ref[...]
        r = _rsqrt_mean_sq(xv)
        xh = xv * r
        hbuf[HALO_P:HALO_P + ts, :] = xh * gv
        _pool_windows(hbuf, pbuf, j * ts, ts)

        dy = dxo_ref[...]
        t1 = (j * ts + lax.broadcasted_iota(jnp.int32, (ts, 1), 0) + 1).astype(F32)
        for gi, w in enumerate(POOL_WINDOWS):
            cols = slice(gi * PG, (gi + 1) * PG)
            p = pbuf[:, cols]
            y = jnp.dot(p, wps[gi], preferred_element_type=F32)
            dsc_ref[:, cols] += jnp.sum(dy[:, cols] * y, axis=0, keepdims=True)
            dq = (dy[:, cols] * sc_ref[:, cols]).astype(BF16)
            dwacc[gi] += lax.dot_general(p, dq, TN_DIMS, preferred_element_type=F32)
            dp = lax.dot_general(dq, wps[gi], NT_DIMS, preferred_element_type=F32)
            qbuf[0:ts, cols] = dp / jnp.minimum(t1, float(w))

        def chunk(ci, carry):
            r0 = pl.multiple_of(ci * R_CHUNK, R_CHUNK)
            tc = (j * ts + r0 + lax.broadcasted_iota(jnp.int32, (R_CHUNK, 1), 0) + 1).astype(F32)
            for gi, w in enumerate(POOL_WINDOWS):
                cnt = jnp.minimum(tc, float(w))
                offs = list(range(w))
                for c0 in range(gi * PG, (gi + 1) * PG, LANES):
                    sh = _shifted(qbuf, r0, R_CHUNK, c0, LANES, offs)
                    tot = sh[0]
                    for o in offs[1:]:
                        tot = tot + sh[o]
                    dhbuf[pl.ds(r0, R_CHUNK), c0:c0 + LANES] = tot - sh[0] * cnt
            return carry

        lax.fori_loop(0, ts // R_CHUNK, chunk, 0)
        qbuf[ts:ts + HALO_P, :] = qbuf[0:HALO_P, :]
        dh = dhbuf[...]
        dg_ref[...] += jnp.sum(dh * xh, axis=0, keepdims=True)
        dx_ref[...] = dy + _rms_bwd(dh, xh, r, gv)

        @pl.when(i == n - 1)
        def _():
            for gi in range(ng):
                for jj in range(NCHIP):
                    dwp_ref[jj, gi] = dwacc[gi, jj * rb:(jj + 1) * rb, :]

    rev = pl.BlockSpec((ts, D), lambda i: (n - 1 - i, 0))
    halo = pl.BlockSpec((HALO_P, D), lambda i: (jnp.maximum((n - 1 - i) * (ts // HALO_P) - 1, 0), 0))
    vec = pl.BlockSpec((1, D), lambda i: (0, 0))
    return pl.pallas_call(
        body, name="pool_bwd", grid=(n,),
        in_specs=[rev, rev, halo, _const((1, D)), _const((NCHIP, ng, rb, PG)), _const((1, D))],
        out_specs=[rev, pl.BlockSpec((NCHIP, ng, rb, PG), lambda i: (0, 0, 0, 0)), vec, vec],
        out_shape=[jax.ShapeDtypeStruct((s, D), F32), jax.ShapeDtypeStruct((NCHIP, ng, rb, PG), F32),
                   jax.ShapeDtypeStruct((1, D), F32), jax.ShapeDtypeStruct((1, D), F32)],
        scratch_shapes=[pltpu.VMEM((HALO_P + ts, D), F32), pltpu.VMEM((ts, D), BF16), pltpu.VMEM((ts + HALO_P, D), F32),
                        pltpu.VMEM((ts, D), F32), pltpu.VMEM((ng, PG, PG), BF16), pltpu.VMEM((ng, PG, PG), F32)],
        compiler_params=_cparams(1),
    )(dxo, x, x, g, w_pool, scale)


def _mix0_bwd_a(dx1, z, ac, w_out, conv_a, ln_g, ln_b, conv_b, comm=None):
    s = dx1.shape[0]
    ts = min(TS_MIXB, s)
    n = s // ts
    rows = 32
    offs_a = [HALO_A - (K_A - 1) + k for k in range(K_A)]
    offs_s = [HALO_S - (K_S - 1) + k for k in range(K_S)]
    boffs_a = [K_A - 1 - k for k in range(K_A)]
    boffs_s = [K_S - 1 - k for k in range(K_S)]

    def body(dx1_ref, z_ref, zh_ref, ac_ref, wout_ref, ca_ref, lg_ref, lb_ref, cb_ref,
             dz_ref, dca_ref, dlg_ref, dlb_ref, dcb_ref,
             glu_buf, cv_buf, dac_buf, dbc_buf, db_buf, dca_acc, dcb_acc):
        i = pl.program_id(0)
        j = n - 1 - i

        @pl.when(i == 0)
        def _():
            dac_buf[ts:ts + HALO_A, :] = jnp.zeros((HALO_A, A), F32)
            dbc_buf[ts:ts + HALO_S, :] = jnp.zeros((HALO_S, A), F32)
            dca_acc[...] = jnp.zeros(dca_acc.shape, F32)
            dcb_acc[...] = jnp.zeros(dcb_acc.shape, F32)
            dlg_ref[...] = jnp.zeros((1, A), F32)
            dlb_ref[...] = jnp.zeros((1, A), F32)

        glu_h = zh_ref[:, 0:A] * _sigmoid(zh_ref[:, A:2 * A])
        glu_buf[0:HALO_A, :] = jnp.where(j == 0, 0.0, glu_h)
        hs = slice(HALO_A - HALO_S, HALO_A)
        cv_buf[0:HALO_S, :] = jnp.where(j == 0, 0.0, zh_ref[hs, 3 * A:4 * A] * zh_ref[hs, 4 * A:5 * A])
        glu_buf[HALO_A:HALO_A + ts, :] = z_ref[:, 0:A] * _sigmoid(z_ref[:, A:2 * A])
        cv_buf[HALO_S:HALO_S + ts, :] = z_ref[:, 3 * A:4 * A] * z_ref[:, 4 * A:5 * A]

        dcat = lax.dot_general(dx1_ref[...].astype(BF16), wout_ref[...], NT_DIMS, preferred_element_type=F32)
        db_buf[...] = dcat[:, A:2 * A]
        ac = ac_ref[...]
        xc = ac - jnp.mean(ac, axis=-1, keepdims=True)
        rstd = lax.rsqrt(jnp.mean(xc * xc, axis=-1, keepdims=True) + LN_EPS)
        xn = xc * rstd
        lg = lg_ref[...]
        ln = xn * lg + lb_ref[...]
        sl = _sigmoid(ln)
        dln = dcat[:, 0:A] * sl * (1.0 + ln * (1.0 - sl))
        dlg_ref[...] += jnp.sum(dln * xn, axis=0, keepdims=True)
        dlb_ref[...] += jnp.sum(dln, axis=0, keepdims=True)
        dxn = dln * lg
        dac_buf[0:ts, :] = rstd * (dxn - jnp.mean(dxn, axis=-1, keepdims=True)
                                   - xn * jnp.mean(dxn * xn, axis=-1, keepdims=True))

        def chunk(ci, carry):
            r0 = pl.multiple_of(ci * rows, rows)
            rs = pl.ds(r0, rows)
            for c0 in range(0, A, LANES):
                cs = slice(c0, c0 + LANES)
                shc = _shifted(cv_buf, r0, rows, c0, LANES, offs_s)
                bconv = _taps(cb_ref, shc, offs_s, c0, LANES)
                dbv = db_buf[rs, cs]
                dz_ref[rs, 2 * A + c0:2 * A + c0 + LANES] = (dbv * bconv).astype(BF16)
                dbconv = dbv * z_ref[rs, 2 * A + c0:2 * A + c0 + LANES]
                dbc_buf[rs, cs] = dbconv
                for k, o in enumerate(offs_s):
                    dcb_acc[SUBLANES * k:SUBLANES * (k + 1), cs] += _rowsum8(dbconv * shc[o])
                shg = _shifted(glu_buf, r0, rows, c0, LANES, offs_a)
                dacv = dac_buf[rs, cs]
                for k, o in enumerate(offs_a):
                    dca_acc[SUBLANES * k:SUBLANES * (k + 1), cs] += _rowsum8(dacv * shg[o])
            return carry

        lax.fori_loop(0, ts // rows, chunk, 0)

        def chunk2(ci, carry):
            r0 = pl.multiple_of(ci * rows, rows)
            rs = pl.ds(r0, rows)
            for c0 in range(0, A, LANES):
                col = lambda grp: slice(grp * A + c0, grp * A + c0 + LANES)
                dglu = _taps(ca_ref, _shifted(dac_buf, r0, rows, c0, LANES, boffs_a), boffs_a, c0, LANES)
                sg = _sigmoid(z_ref[rs, col(1)])
                dz_ref[rs, col(0)] = (dglu * sg).astype(BF16)
                dz_ref[rs, col(1)] = (dglu * z_ref[rs, col(0)] * sg * (1.0 - sg)).astype(BF16)
                dcv = _taps(cb_ref, _shifted(dbc_buf, r0, rows, c0, LANES, boffs_s), boffs_s, c0, LANES)
                dz_ref[rs, col(3)] = (dcv * z_ref[rs, col(4)]).astype(BF16)
                dz_ref[rs, col(4)] = (dcv * z_ref[rs, col(3)]).astype(BF16)
            return carry

        lax.fori_loop(0, ts // rows, chunk2, 0)
        dac_buf[ts:ts + HALO_A, :] = dac_buf[0:HALO_A, :]
        dbc_buf[ts:ts + HALO_S, :] = dbc_buf[0:HALO_S, :]

        @pl.when(i == n - 1)
        def _():
            _finish_tap_sums(dca_acc, dca_ref, K_A)
            _finish_tap_sums(dcb_acc, dcb_ref, K_S)

    rev = lambda w: pl.BlockSpec((ts, w), lambda i: (n - 1 - i, 0))
    halo = pl.BlockSpec((HALO_A, NZ), lambda i: (jnp.maximum((n - 1 - i) * (ts // HALO_A) - 1, 0), 0))
    full = lambda r, c: pl.BlockSpec((r, c), lambda i: (0, 0))
    return _call(
        body, name="mix0_bwd_a", grid=(n,), comm=comm, args=(dx1, z, z, ac, w_out, conv_a, ln_g, ln_b, conv_b),
        in_specs=[rev(D), rev(NZ), halo, rev(A), _const((2 * A, D)), _const((K_A, A)), _const((1, A)), _const((1, A)),
                  _const((K_S, A))],
        out_specs=[rev(NZ), full(K_A, A), full(1, A), full(1, A), full(K_S, A)],
        out_shape=[jax.ShapeDtypeStruct((s, NZ), BF16), jax.ShapeDtypeStruct((K_A, A), F32), jax.ShapeDtypeStruct((1, A), F32),
                   jax.ShapeDtypeStruct((1, A), F32), jax.ShapeDtypeStruct((K_S, A), F32)],
        scratch=[pltpu.VMEM((HALO_A + ts, A), F32), pltpu.VMEM((HALO_S + ts, A), F32), pltpu.VMEM((ts + HALO_A, A), F32),
                 pltpu.VMEM((ts + HALO_S, A), F32), pltpu.VMEM((ts, A), F32), pltpu.VMEM((SUBLANES * K_A, A), F32),
                 pltpu.VMEM((SUBLANES * K_S, A), F32)])


def _train_step(x, target, place, shard, rep, small_pack, small_shapes):
    bw_up, bw_in = FF2 // NCHIP, NZ // NCHIP
    five = lambda a, k: a.reshape(NCHIP, 2, *HALF[k])

    g_in, g_out, small_g = _run_comm(_gather_comm([shard["w_in"], shard["w_out"]], small_pack), "ag_first")
    whole = {}
    for k, part in zip(SMALL_SHARDED, _unpack(small_g, small_shapes, lead=(NCHIP,))):
        whole[k] = jnp.moveaxis(part, 0, 1).reshape(part.shape[1], NCHIP * part.shape[2])
    w_in, w_out = g_in.reshape(NCHIP, D, bw_in), g_out.reshape(2 * A, D)
    conv_ffn = whole["conv_ffn_w"].reshape(2, K_S, FF2)
    nffn = [rep["norm_ffn"][0:1], rep["norm_ffn"][1:2]]

    (h0, z, ac, cat, x1), (g_up0, g_dn0) = _mix0_fwd(
        x, rep["norm_mix_even"], w_in, whole["conv_a"], rep["ln_a_g"], rep["ln_a_b"], whole["conv_b"], w_out,
        comm=_gather_comm([shard["w_up0"], shard["w_down0"]]))
    w_up0, w_dn0 = g_up0.reshape(NCHIP, D, bw_up), g_dn0.reshape(FF, D)
    (hf0, u00, act0, x2), (g_pool, g_up1, g_dn1) = _ffn_fwd(
        x1, nffn[0], w_up0, conv_ffn[0], w_dn0, "ffn0_fwd",
        comm=_gather_comm([shard["w_pool"], shard["w_up1"], shard["w_down1"]]))
    w_pool = g_pool.reshape(NCHIP, len(POOL_WINDOWS), PG // NCHIP, PG)
    w_up1, w_dn1 = g_up1.reshape(NCHIP, D, bw_up), g_dn1.reshape(FF, D)
    x3 = _pool_fwd(x2, whole["norm_mix_odd"], w_pool, whole["pool_scale"])
    (hf1, u01, act1, x4), _ = _ffn_fwd(x3, nffn[1], w_up1, conv_ffn[1], w_dn1, "ffn1_fwd")
    dx4, g_nfin, loss_part = _final_loss(x4, rep["norm_final"], target)

    def wgrads_ffn(hf, du0, act, dxo, tag):
        g_up, _ = _wgrad(hf, du0, bw_up, (NCHIP, D, bw_up), (None, D, bw_up), lambda jn: (jn, 0, 0), "wgrad_up" + tag)
        g_dn, _ = _wgrad(act, dxo, D // 2, (FF, D), (FF, D // 2), lambda jn: (0, jn), "wgrad_down" + tag)
        return five(g_up, "w_up"), five(g_dn, "w_down")

    def pair_sums(keys, grads, landed, tag):
        return [_pair_sum(place, g, ld, "pair_sum_" + k + tag) for k, g, ld in zip(keys, grads, landed)]

    (du01, g_wc1), _ = _ffn_bwd_a(dx4, u01, conv_ffn[1], w_dn1, "ffn1_bwd_a")
    grads1 = wgrads_ffn(hf1, du01, act1, dx4, "1")
    (dx3, g_nf1), landed1 = _nt_rms_bwd(du01, w_up1, x3, nffn[1], dx4, "ffn1_bwd_b", comm=_pair_comm(grads1))
    sums1 = pair_sums(("w_up", "w_down"), grads1, landed1, "1")
    dx2, g_wpool, g_scale, g_nmo = _pool_bwd(dx3, x2, whole["norm_mix_odd"], w_pool, whole["pool_scale"])
    (du00, g_wc0), parts1 = _ffn_bwd_a(dx2, u00, conv_ffn[0], w_dn0, "ffn0_bwd_a", comm=_chips_comm(sums1))
    grads0 = wgrads_ffn(hf0, du00, act0, dx2, "0") + (five(g_wpool, "w_pool"),)
    (dx1, g_nf0), landed0 = _nt_rms_bwd(du00, w_up0, x1, nffn[0], dx2, "ffn0_bwd_b", comm=_pair_comm(grads0))
    sums0 = pair_sums(("w_up", "w_down", "w_pool"), grads0, landed0, "0")
    (dz, g_ca, g_lg, g_lb, g_cb), parts0 = _mix0_bwd_a(
        dx1, z, ac, w_out, whole["conv_a"], rep["ln_a_g"], rep["ln_a_b"], whole["conv_b"], comm=_chips_comm(sums0))
    gr_out, _ = _wgrad(cat, dx1, D, (D, D), (D, D), lambda jn: (0, 0), "wgrad_out")
    gr_in, _ = _wgrad(h0, dz, bw_in, (NCHIP, D, bw_in), (None, D, bw_in), lambda jn: (jn, 0, 0), "wgrad_in")
    gradsm = (five(gr_in, "w_in"), five(gr_out, "w_out"))
    (grad_x, g_nme), landedm = _nt_rms_bwd(dz, w_in, x, rep["norm_mix_even"], dx1, "mix0_bwd_b", comm=_pair_comm(gradsm))
    sumsm = pair_sums(("w_in", "w_out"), gradsm, landedm, "")

    small = {"norm_mix_even": g_nme, "conv_a": g_ca, "ln_a_g": g_lg, "ln_a_b": g_lb, "conv_b": g_cb, "norm_mix_odd": g_nmo,
             "pool_scale": g_scale, "norm_ffn": jnp.concatenate([g_nf0, g_nf1], axis=0),
             "conv_ffn_w": jnp.stack([g_wc0, g_wc1]), "norm_final": g_nfin}
    *partsm, small_all = _run_comm(_chips_comm(sumsm, _pack([small[k] for k in SMALL_ALL] + [loss_part])), "rs_last")

    tot = lambda k, g, ld, p, layer=0, nl=1, prev=None: _chip_sum(place, g, ld, p, layer, nl, "chip_sum_%s%d" % (k, layer), prev)
    t_in = tot("w_in", gradsm[0], landedm[0], partsm[0])
    t_out = tot("w_out", gradsm[1], landedm[1], partsm[1])
    t_pool = tot("w_pool", grads0[2], landed0[2], parts0[2])
    t_up = tot("w_up", grads1[0], landed1[0], parts1[0], 1, 2)
    t_up = tot("w_up", grads0[0], landed0[0], parts0[0], 0, 2, t_up)
    t_dn = tot("w_down", grads1[1], landed1[1], parts1[1], 1, 2)
    t_dn = tot("w_down", grads0[1], landed0[1], parts0[1], 0, 2, t_dn)
    swapped = _run_comm(_swap_comm([t_in, t_out, t_pool, t_up, t_dn]), "rs_swap")
    return grad_x, dict(zip(BIG, swapped)), _sum_devices(small_all), [small[k].shape for k in SMALL_ALL] + [(1, LANES)]


def _adamw_math(w, g, m, v):
    m = ADAM_B1 * m + (1.0 - ADAM_B1) * g
    v = ADAM_B2 * v + (1.0 - ADAM_B2) * (g * g)
    m_hat = m / (1.0 - ADAM_B1 ** ADAM_STEP)
    v_hat = v / (1.0 - ADAM_B2 ** ADAM_STEP)
    return -ADAM_LR * (m_hat / (jnp.sqrt(v_hat) + ADAM_EPS) + ADAM_WD * w), m, v


def _adamw_big(w, g, m, v, tr, name):
    nl, rows, cols = w.shape

    def body(w_ref, g_ref, m_ref, v_ref, d_ref, m2_ref, v2_ref):
        d_ref[...], m2_ref[...], v2_ref[...] = _adamw_math(w_ref[...], g_ref[...], m_ref[...], v_ref[...])

    spec = pl.BlockSpec((None, tr, cols), lambda l, r: (l, r, 0))
    return pl.pallas_call(
        body, name=name, grid=(nl, rows // tr), in_specs=[spec] * 4, out_specs=[spec] * 3,
        out_shape=[jax.ShapeDtypeStruct(w.shape, F32)] * 3, compiler_params=_cparams(2),
    )(w, g, m, v)


def _adamw_small(ws, gs, ms, vs):
    n = len(ws)

    def body(*refs):
        for p in range(n):
            w_ref, g_ref, m_ref, v_ref = (refs[q * n + p] for q in range(4))
            d_ref, m2_ref, v2_ref = (refs[(4 + q) * n + p] for q in range(3))
            d_ref[...], m2_ref[...], v2_ref[...] = _adamw_math(w_ref[...], g_ref[...], m_ref[...], v_ref[...])

    whole = lambda a: pl.BlockSpec(a.shape, lambda: (0,) * a.ndim)
    outs = pl.pallas_call(
        body, name="adamw_small", in_specs=[whole(a) for a in ws] * 4, out_specs=[whole(a) for a in ws] * 3,
        out_shape=[jax.ShapeDtypeStruct(a.shape, F32) for a in ws] * 3,
        compiler_params=pltpu.CompilerParams(vmem_limit_bytes=VMEM_LIMIT_BYTES),
    )(*ws, *gs, *ms, *vs)
    return outs[0:n], outs[n:2 * n], outs[2 * n:3 * n]


def _place():
    x, y, c = lax.axis_index("x"), lax.axis_index("y"), lax.axis_index("c")
    chips = [(x, 1 - y), (1 - x, y), (1 - x, 1 - y)]
    blocks = [2 * cx + cy for cx, cy in chips]
    return x, y, c, 2 * x + y, chips, blocks


def _gather_comm(shards, small=None):
    na = len(shards)
    nk = NCHIP - 1
    ns = 0 if small is None else 1

    def copies(ins, outs, sems):
        ici_send, ici_recv, fwd_send, fwd_recv, own_send, own_recv = sems[:6]
        x, y, c, j, chips, blocks = _place()
        sib = (x, y, 1 - c)

        def ici(a, k, arrival):
            dst = outs[a].at[blocks[k], c] if arrival else outs[a].at[j, c]
            return pltpu.make_async_remote_copy(
                src_ref=dst if arrival else ins[a].at[c], dst_ref=dst, send_sem=ici_send.at[a * nk + k],
                recv_sem=ici_recv.at[a * nk + k], device_id=(*chips[k], c), device_id_type=MESH)

        def fwd(a, k, half):
            ref = outs[a].at[blocks[k], half]
            return pltpu.make_async_remote_copy(
                src_ref=ref, dst_ref=ref, send_sem=fwd_send.at[a * nk + k], recv_sem=fwd_recv.at[a * nk + k],
                device_id=sib, device_id_type=MESH)

        own = [pltpu.make_async_remote_copy(src_ref=ins[a], dst_ref=outs[a].at[j], send_sem=own_send.at[a],
                                            recv_sem=own_recv.at[a], device_id=sib, device_id_type=MESH) for a in range(na)]
        small_copies = [pltpu.make_async_remote_copy(
            src_ref=ins[na], dst_ref=outs[na].at[j], send_sem=ici_send.at[na * nk + k], recv_sem=ici_recv.at[na * nk + k],
            device_id=(*chips[k], c), device_id_type=MESH) for k in range(nk * ns)]
        local = [pltpu.make_async_copy(ins[na], outs[na].at[j], sems[6])] if ns else []
        return ici, fwd, own, small_copies, local, c

    def start(ins, outs, sems):
        ici, _, own, small_copies, local, _ = copies(ins, outs, sems)
        for cp in local + own + [ici(a, k, False) for a in range(na) for k in range(nk)] + small_copies:
            cp.start()

    def finish(ins, outs, sems):
        ici, fwd, own, small_copies, local, c = copies(ins, outs, sems)
        for a in range(na):
            for k in range(nk):
                ici(a, k, True).wait_recv()
                fwd(a, k, c).start()
        for cp in small_copies:
            cp.wait()
        for a in range(na):
            for k in range(nk):
                ici(a, k, False).wait_send()
                fwd(a, k, c).wait_send()
                fwd(a, k, 1 - c).wait_recv()
        for cp in own + local:
            cp.wait()

    out_shapes = [jax.ShapeDtypeStruct((NCHIP,) + s.shape, s.dtype) for s in shards]
    sems = [pltpu.SemaphoreType.DMA((na * nk + nk * ns,)), pltpu.SemaphoreType.DMA((na * nk + nk * ns,)),
            pltpu.SemaphoreType.DMA((na * nk,)), pltpu.SemaphoreType.DMA((na * nk,)),
            pltpu.SemaphoreType.DMA((na,)), pltpu.SemaphoreType.DMA((na,))]
    if ns:
        out_shapes.append(jax.ShapeDtypeStruct((NCHIP,) + small.shape, small.dtype))
        sems.append(pltpu.SemaphoreType.DMA)
    return _Comm(list(shards) + [small] * ns, out_shapes, sems, start, finish)


def _simple_comm(inputs, out_shapes, make_copies, n_sems, aliases=None):
    def start(ins, outs, sems):
        for cp in make_copies(ins, outs, sems):
            cp.start()

    def finish(ins, outs, sems):
        for cp in make_copies(ins, outs, sems):
            cp.wait()

    return _Comm(inputs, out_shapes, [pltpu.SemaphoreType.DMA((n,)) for n in n_sems], start, finish, aliases)


def _pair_comm(grads):
    def make_copies(ins, outs, sems):
        x, y, c, _, _, _ = _place()
        return [pltpu.make_async_remote_copy(
            src_ref=ins[a].at[:, 1 - c], dst_ref=outs[a], send_sem=sems[0].at[a], recv_sem=sems[1].at[a],
            device_id=(x, y, 1 - c), device_id_type=MESH) for a in range(len(grads))]

    out_shapes = [jax.ShapeDtypeStruct(g.shape[:1] + g.shape[2:], F32) for g in grads]
    return _simple_comm(grads, out_shapes, make_copies, [len(grads)] * 2)


def _chips_comm(sums, small=None):
    na = len(sums)
    nk = NCHIP - 1

    def make_copies(ins, outs, sems):
        x, y, c, _, chips, blocks = _place()
        copies = [pltpu.make_async_remote_copy(
            src_ref=ins[a].at[blocks[k]], dst_ref=outs[a].at[k], send_sem=sems[0].at[a * nk + k],
            recv_sem=sems[1].at[a * nk + k], device_id=(*chips[k], c), device_id_type=MESH)
            for a in range(na) for k in range(nk)]
        if small is not None:
            me = 4 * x + 2 * y + c
            for r in range(1, NDEV):
                peer = (1 - x if r & 4 else x, 1 - y if r & 2 else y, 1 - c if r & 1 else c)
                copies.append(pltpu.make_async_remote_copy(
                    src_ref=ins[na], dst_ref=outs[na].at[me], send_sem=sems[2].at[r - 1], recv_sem=sems[3].at[r - 1],
                    device_id=peer, device_id_type=MESH))
            copies.append(pltpu.make_async_copy(ins[na], outs[na].at[me], sems[4].at[0]))
        return copies

    out_shapes = [jax.ShapeDtypeStruct((nk,) + g.shape[1:], BF16) for g in sums]
    if small is None:
        return _simple_comm(sums, out_shapes, make_copies, [na * nk] * 2)
    out_shapes.append(jax.ShapeDtypeStruct((NDEV,) + small.shape, F32))
    return _simple_comm(list(sums) + [small], out_shapes, make_copies, [na * nk] * 2 + [NDEV - 1] * 2 + [1])


def _swap_comm(totals):
    def make_copies(ins, outs, sems):
        x, y, c, _, _, _ = _place()
        return [pltpu.make_async_remote_copy(
            src_ref=outs[a].at[:, c], dst_ref=outs[a].at[:, c], send_sem=sems[0].at[a], recv_sem=sems[1].at[a],
            device_id=(x, y, 1 - c), device_id_type=MESH) for a in range(len(totals))]

    out_shapes = [jax.ShapeDtypeStruct(t.shape, F32) for t in totals]
    return _simple_comm(totals, out_shapes, make_copies, [len(totals)] * 2, aliases={a: a for a in range(len(totals))})


def _pair_sum(place, grad, landed, name):
    _, _, rows, cols = grad.shape

    def body(place_ref, g_ref, l_ref, o_ref):
        o_ref[...] = (g_ref[...] + l_ref[...]).astype(BF16)

    return pl.pallas_call(
        body, name=name,
        grid_spec=pltpu.PrefetchScalarGridSpec(
            num_scalar_prefetch=1, grid=(NCHIP,),
            in_specs=[pl.BlockSpec((None, None, rows, cols), lambda b, p: (b, p[1], 0, 0)),
                      pl.BlockSpec((None, rows, cols), lambda b, p: (b, 0, 0))],
            out_specs=pl.BlockSpec((None, rows, cols), lambda b, p: (b, 0, 0))),
        out_shape=jax.ShapeDtypeStruct((NCHIP, rows, cols), BF16), compiler_params=_cparams(1),
    )(place, grad, landed)


def _chip_sum(place, grad, landed, parts, layer, n_layers, name, prev=None):
    _, _, rows, cols = grad.shape

    def body(*refs):
        g_ref, l_ref, p_ref, o_ref = refs[1], refs[2], refs[3], refs[-1]
        tot = g_ref[...] + l_ref[...]
        for k in range(NCHIP - 1):
            tot = tot + p_ref[k].astype(F32)
        o_ref[...] = tot

    in_specs = [pl.BlockSpec((None, None, rows, cols), lambda i, p: (p[0], p[1], 0, 0)),
                pl.BlockSpec((None, rows, cols), lambda i, p: (p[0], 0, 0)),
                pl.BlockSpec((NCHIP - 1, rows, cols), lambda i, p: (0, 0, 0))]
    args = [place, grad, landed, parts]
    if prev is not None:
        in_specs.append(ANY)
        args.append(prev)
    return pl.pallas_call(
        body, name=name,
        grid_spec=pltpu.PrefetchScalarGridSpec(
            num_scalar_prefetch=1, grid=(1,), in_specs=in_specs,
            out_specs=pl.BlockSpec((None, None, rows, cols), lambda i, p: (layer, p[1], 0, 0))),
        out_shape=jax.ShapeDtypeStruct((n_layers, 2, rows, cols), F32),
        input_output_aliases={} if prev is None else {4: 0}, compiler_params=_cparams(1),
    )(*args)


def _sum_devices(parts):
    def body(p_ref, o_ref):
        tot = p_ref[0]
        for d in range(1, NDEV):
            tot = tot + p_ref[d]
        o_ref[...] = tot

    return pl.pallas_call(
        body, name="sum_small", in_specs=[pl.BlockSpec(parts.shape, lambda: (0, 0, 0))],
        out_specs=pl.BlockSpec(parts.shape[1:], lambda: (0, 0)), out_shape=jax.ShapeDtypeStruct(parts.shape[1:], F32),
    )(parts)


def _pack(parts):
    rows = []
    for p in parts:
        p = p.reshape(-1, LANES)
        rows.append(jnp.pad(p, ((0, -p.shape[0] % SUBLANES), (0, 0))))
    return jnp.concatenate(rows, axis=0)


def _unpack(buf, shapes, lead=()):
    out, r0 = [], 0
    nl = len(lead)
    for shp in shapes:
        nrow = 1
        for d in shp:
            nrow *= d
        nrow //= LANES
        out.append(buf[(slice(None),) * nl + (slice(r0, r0 + nrow),)].reshape(lead + tuple(shp)))
        r0 += nrow + (-nrow % SUBLANES)
    return out


WEIGHT_ORDER = ("norm_mix_even", "w_in", "conv_a", "ln_a_g", "ln_a_b", "conv_b", "w_out", "norm_mix_odd", "w_pool",
                "pool_scale", "norm_ffn", "w_up", "conv_ffn_w", "w_down", "norm_final")
BIG = ("w_in", "w_out", "w_pool", "w_up", "w_down")
HALF = {"w_in": (D // 2, NZ // NCHIP), "w_out": (2 * A // NCHIP // 2, D), "w_pool": (PG // 2, PG),
        "w_up": (D // 2, FF2 // NCHIP), "w_down": (FF // NCHIP // 2, D)}
SMALL_SHARDED = ("conv_a", "conv_b", "conv_ffn_w", "norm_mix_odd", "pool_scale")
SMALL_ALL = ("norm_mix_even", "conv_a", "ln_a_g", "ln_a_b", "conv_b", "norm_mix_odd", "pool_scale", "norm_ffn", "conv_ffn_w",
             "norm_final")


def kernel(x, norm_mix_even, w_in, conv_a, ln_a_g, ln_a_b, conv_b, w_out, norm_mix_odd, w_pool, pool_scale, norm_ffn, w_up, conv_ffn_w, w_down, norm_final, loss_target, m_norm_mix_even, m_w_in, m_conv_a, m_ln_a_g, m_ln_a_b, m_conv_b, m_w_out, m_norm_mix_odd, m_w_pool, m_pool_scale, m_norm_ffn, m_w_up, m_conv_ffn_w, m_w_down, m_norm_final, v_norm_mix_even, v_w_in, v_conv_a, v_ln_a_g, v_ln_a_b, v_conv_b, v_w_out, v_norm_mix_odd, v_w_pool, v_pool_scale, v_norm_ffn, v_w_up, v_conv_ffn_w, v_w_down, v_norm_final):
    w = dict(norm_mix_even=norm_mix_even, w_in=w_in, conv_a=conv_a, ln_a_g=ln_a_g, ln_a_b=ln_a_b, conv_b=conv_b, w_out=w_out,
             norm_mix_odd=norm_mix_odd, w_pool=w_pool, pool_scale=pool_scale, norm_ffn=norm_ffn, w_up=w_up,
             conv_ffn_w=conv_ffn_w, w_down=w_down, norm_final=norm_final)
    m = dict(norm_mix_even=m_norm_mix_even, w_in=m_w_in, conv_a=m_conv_a, ln_a_g=m_ln_a_g, ln_a_b=m_ln_a_b, conv_b=m_conv_b,
             w_out=m_w_out, norm_mix_odd=m_norm_mix_odd, w_pool=m_w_pool, pool_scale=m_pool_scale, norm_ffn=m_norm_ffn,
             w_up=m_w_up, conv_ffn_w=m_conv_ffn_w, w_down=m_w_down, norm_final=m_norm_final)
    v = dict(norm_mix_even=v_norm_mix_even, w_in=v_w_in, conv_a=v_conv_a, ln_a_g=v_ln_a_g, ln_a_b=v_ln_a_b, conv_b=v_conv_b,
             w_out=v_w_out, norm_mix_odd=v_norm_mix_odd, w_pool=v_w_pool, pool_scale=v_pool_scale, norm_ffn=v_norm_ffn,
             w_up=v_w_up, conv_ffn_w=v_conv_ffn_w, w_down=v_w_down, norm_final=v_norm_final)
    chip = 2 * lax.axis_index("x") + lax.axis_index("y")
    place = jnp.stack([chip, lax.axis_index("c")]).astype(jnp.int32)

    half = lambda a, name: a.astype(BF16).reshape((2,) + HALF[name])
    shard = {"w_in": half(w_in[0], "w_in"), "w_out": half(w_out[0], "w_out"), "w_pool": half(w_pool[0], "w_pool"),
             "w_up0": half(w_up[0], "w_up"), "w_up1": half(w_up[1], "w_up"),
             "w_down0": half(w_down[0], "w_down"), "w_down1": half(w_down[1], "w_down")}
    small_shapes = [w[k].shape[-2:] if w[k].ndim == 3 and k != "conv_ffn_w" else (w[k].size // w[k].shape[-1], w[k].shape[-1])
                    for k in SMALL_SHARDED]
    rep = dict(norm_mix_even=norm_mix_even, ln_a_g=ln_a_g, ln_a_b=ln_a_b, norm_ffn=norm_ffn, norm_final=norm_final.reshape(1, D))
    grad_x, swapped, small_packed, small_full_shapes = _train_step(
        x[0], loss_target[0], place, shard, rep, _pack([w[k] for k in SMALL_SHARDED]), small_shapes)
    small_sum = _unpack(small_packed, small_full_shapes)
    loss = small_sum[-1][0, 0]

    grad = {k: swapped[k].reshape(w[k].shape) for k in BIG}
    for k, gsum in zip(SMALL_ALL, small_sum):
        if k in SMALL_SHARDED:
            cols = w[k].shape[-1]
            gsum = lax.dynamic_slice_in_dim(gsum, chip * cols, cols, axis=gsum.ndim - 1)
        grad[k] = gsum.reshape(w[k].shape)

    delta, new_m, new_v = {}, {}, {}
    rows_per_step = {"w_in": 512, "w_out": 256, "w_pool": 256, "w_up": 256, "w_down": 352}
    for k in BIG:
        as3 = lambda a: a.reshape(a.shape[0], -1, a.shape[-1])
        d3, m3, v3 = _adamw_big(as3(w[k]), as3(grad[k]), as3(m[k]), as3(v[k]), rows_per_step[k], "adamw_" + k)
        delta[k], new_m[k], new_v[k] = (a.reshape(w[k].shape) for a in (d3, m3, v3))
    as2 = lambda a: a.reshape(-1, a.shape[-1])
    ds, ms, vs = _adamw_small(*[[as2(t[k]) for k in SMALL_ALL] for t in (w, grad, m, v)])
    for k, d2, m2, v2 in zip(SMALL_ALL, ds, ms, vs):
        delta[k], new_m[k], new_v[k] = (a.reshape(w[k].shape) for a in (d2, m2, v2))

    return (loss, grad_x[None], *[grad[k] for k in WEIGHT_ORDER], *[delta[k] for k in WEIGHT_ORDER],
            *[new_m[k] for k in WEIGHT_ORDER], *[new_v[k] for k in WEIGHT_ORDER])
```

```python
import functools

import jax
import jax.numpy as jnp
from jax import lax
from jax.experimental import pallas as pl
from jax.experimental.pallas import tpu as pltpu

F32, BF16 = jnp.float32, jnp.bfloat16

D = 1024
A = 512
NZ = 5 * A
FF = 2816
FF2 = 2 * FF
NCHIP = 4
NDEV = 8
K_A, K_S = 31, 3
POOL_WINDOWS = (2, 4, 8, 16)
PG = D // len(POOL_WINDOWS)
RMS_EPS, LN_EPS = 1e-6, 1e-5
ADAM_LR, ADAM_B1, ADAM_B2, ADAM_EPS, ADAM_WD, ADAM_STEP = 0.001, 0.9, 0.999, 1e-08, 0.01, 10

HALO_A, HALO_S, HALO_P = 32, 8, 16
SUBLANES = 8
LANES = 128
VMEM_LIMIT_BYTES = 56 * 1024 * 1024

TS_MIX = 512
TS_MIXB = 256
TS_FFN = 256
TS_POOL = 512
TS_MM = 512
R_CHUNK = 64

MESH = pl.DeviceIdType.MESH
ANY = pl.BlockSpec(memory_space=pl.ANY)
NT_DIMS = (((1,), (1,)), ((), ()))
TN_DIMS = (((0,), (0,)), ((), ()))


def _cparams(n_axes):
    return pltpu.CompilerParams(dimension_semantics=("arbitrary",) * n_axes, vmem_limit_bytes=VMEM_LIMIT_BYTES)


def _const(shape):
    nd = len(shape)
    return pl.BlockSpec(shape, lambda *_: (0,) * nd, pipeline_mode=pl.Buffered(1))


def _sigmoid(v):
    return 1.0 / (1.0 + jnp.exp(-v))


def _rsqrt_mean_sq(x):
    return lax.rsqrt(jnp.mean(x * x, axis=-1, keepdims=True) + RMS_EPS)


def _rms_bwd(dh, xh, r, g):
    dxh = dh * g
    return r * (dxh - xh * jnp.mean(dxh * xh, axis=-1, keepdims=True))


def _shifted(buf_ref, row0, rows, col0, width, offsets):
    lo = (min(offsets) // SUBLANES) * SUBLANES
    hi = -(-(max(offsets) + rows) // SUBLANES) * SUBLANES
    start = row0 + lo if isinstance(row0, int) else pl.multiple_of(row0 + lo, SUBLANES)
    win = buf_ref[pl.ds(start, hi - lo), col0:col0 + width]
    out = {}
    for res in sorted({(o - lo) % SUBLANES for o in offsets}):
        qs = {o: (o - lo) // SUBLANES for o in offsets if (o - lo) % SUBLANES == res}
        base = win[res:res + rows + SUBLANES * max(qs.values()), :]
        for o, q in qs.items():
            out[o] = base[SUBLANES * q:SUBLANES * q + rows, :]
    return out


def _rowsum8(v):
    acc = v[0:SUBLANES, :]
    for r in range(SUBLANES, v.shape[0], SUBLANES):
        acc = acc + v[r:r + SUBLANES, :]
    return acc


def _taps(w_ref, sh, offsets, col0, width):
    acc = None
    for k, o in enumerate(offsets):
        term = w_ref[k:k + 1, col0:col0 + width] * sh[o]
        acc = term if acc is None else acc + term
    return acc


def _finish_tap_sums(acc_ref, out_ref, n_taps):
    for k in range(n_taps):
        out_ref[k:k + 1, :] = jnp.sum(acc_ref[SUBLANES * k:SUBLANES * (k + 1), :], axis=0, keepdims=True)


class _Comm:
    def __init__(self, inputs, out_shapes, sems, start, finish, aliases=None):
        self.inputs, self.out_shapes, self.sems = list(inputs), list(out_shapes), list(sems)
        self.start, self.finish, self.aliases = start, finish, dict(aliases or {})


def _call(body, *, name, grid, in_specs, out_specs, out_shape, args, scratch=(), comm=None, aliases=None):
    n_in, n_out, n_scr, n_axes = len(in_specs), len(out_specs), len(scratch), len(grid)
    params = pltpu.CompilerParams(dimension_semantics=("arbitrary",) * n_axes, vmem_limit_bytes=VMEM_LIMIT_BYTES)
    aliases = dict(aliases or {})
    if comm is None:
        outs = pl.pallas_call(body, name=name, grid=grid, in_specs=list(in_specs), out_specs=list(out_specs),
                              out_shape=list(out_shape), scratch_shapes=list(scratch), input_output_aliases=aliases,
                              compiler_params=params)(*args)
        return list(outs), []
    ci, co = len(comm.inputs), len(comm.out_shapes)

    def wrapped(*refs):
        k_in, c_in = refs[:n_in], refs[n_in:n_in + ci]
        o0 = n_in + ci
        k_out, c_out = refs[o0:o0 + n_out], refs[o0 + n_out:o0 + n_out + co]
        s0 = o0 + n_out + co
        k_scr, c_sem = refs[s0:s0 + n_scr], refs[s0 + n_scr:]
        first = pl.program_id(0) == 0
        last = pl.program_id(0) == grid[0] - 1
        for ax in range(1, n_axes):
            first = jnp.logical_and(first, pl.program_id(ax) == 0)
            last = jnp.logical_and(last, pl.program_id(ax) == grid[ax] - 1)

        @pl.when(first)
        def _():
            comm.start(c_in, c_out, c_sem)

        body(*k_in, *k_out, *k_scr)

        @pl.when(last)
        def _():
            comm.finish(c_in, c_out, c_sem)

    outs = pl.pallas_call(
        wrapped, name=name, grid=grid, in_specs=list(in_specs) + [ANY] * ci, out_specs=list(out_specs) + [ANY] * co,
        out_shape=list(out_shape) + comm.out_shapes, scratch_shapes=list(scratch) + comm.sems,
        input_output_aliases={**aliases, **{n_in + i: n_out + o for i, o in comm.aliases.items()}}, compiler_params=params,
    )(*args, *comm.inputs)
    return list(outs[:n_out]), list(outs[n_out:])


def _run_comm(comm, name):
    ci, co = len(comm.inputs), len(comm.out_shapes)

    def body(*refs):
        c_in, c_out, c_sem = refs[:ci], refs[ci:ci + co], refs[ci + co:]
        comm.start(c_in, c_out, c_sem)
        comm.finish(c_in, c_out, c_sem)

    return list(pl.pallas_call(body, name=name, in_specs=[ANY] * ci, out_specs=[ANY] * co, out_shape=comm.out_shapes,
                               scratch_shapes=comm.sems, input_output_aliases=comm.aliases)(*comm.inputs))


def _mix0_fwd(x, g, w_in, conv_a, ln_g, ln_b, conv_b, w_out, comm=None):
    s = x.shape[0]
    ts = min(TS_MIX, s)
    n = s // ts
    bw = NZ // NCHIP
    offs_a = [HALO_A - (K_A - 1) + k for k in range(K_A)]
    offs_s = [HALO_S - (K_S - 1) + k for k in range(K_S)]

    def body(x_ref, g_ref, win_ref, ca_ref, lg_ref, lb_ref, cb_ref, wout_ref,
             h_ref, z_ref, ac_ref, cat_ref, x1_ref, glu_buf, cv_buf, bconv_buf):
        i = pl.program_id(0)

        @pl.when(i == 0)
        def _():
            glu_buf[0:HALO_A, :] = jnp.zeros((HALO_A, A), F32)
            cv_buf[0:HALO_S, :] = jnp.zeros((HALO_S, A), F32)

        xv = x_ref[...]
        h = (xv * _rsqrt_mean_sq(xv) * g_ref[...]).astype(BF16)
        h_ref[...] = h
        for j in range(NCHIP):
            z_ref[:, j * bw:(j + 1) * bw] = jnp.dot(h, win_ref[j], preferred_element_type=F32)
        glu_buf[HALO_A:HALO_A + ts, :] = z_ref[:, 0:A] * _sigmoid(z_ref[:, A:2 * A])
        cv_buf[HALO_S:HALO_S + ts, :] = z_ref[:, 3 * A:4 * A] * z_ref[:, 4 * A:5 * A]

        def chunk(ci, carry):
            r0 = pl.multiple_of(ci * R_CHUNK, R_CHUNK)
            for c0 in range(0, A, LANES):
                sh = _shifted(glu_buf, r0, R_CHUNK, c0, LANES, offs_a)
                ac_ref[pl.ds(r0, R_CHUNK), c0:c0 + LANES] = _taps(ca_ref, sh, offs_a, c0, LANES)
                sh = _shifted(cv_buf, r0, R_CHUNK, c0, LANES, offs_s)
                bconv_buf[pl.ds(r0, R_CHUNK), c0:c0 + LANES] = _taps(cb_ref, sh, offs_s, c0, LANES)
            return carry

        lax.fori_loop(0, ts // R_CHUNK, chunk, 0)
        glu_buf[0:HALO_A, :] = glu_buf[ts:ts + HALO_A, :]
        cv_buf[0:HALO_S, :] = cv_buf[ts:ts + HALO_S, :]

        ac = ac_ref[...]
        xc = ac - jnp.mean(ac, axis=-1, keepdims=True)
        xn = xc * lax.rsqrt(jnp.mean(xc * xc, axis=-1, keepdims=True) + LN_EPS)
        ln = xn * lg_ref[...] + lb_ref[...]
        cat_ref[:, 0:A] = (ln * _sigmoid(ln)).astype(BF16)
        cat_ref[:, A:2 * A] = (z_ref[:, 2 * A:3 * A] * bconv_buf[...]).astype(BF16)
        x1_ref[...] = xv + jnp.dot(cat_ref[...], wout_ref[...], preferred_element_type=F32)

    tile = lambda w: pl.BlockSpec((ts, w), lambda i: (i, 0))
    return _call(
        body, name="mix0_fwd", grid=(n,), comm=comm, args=(x, g, w_in, conv_a, ln_g, ln_b, conv_b, w_out),
        in_specs=[tile(D), _const((1, D)), _const((NCHIP, D, bw)), _const((K_A, A)), _const((1, A)), _const((1, A)),
                  _const((K_S, A)), _const((2 * A, D))],
        out_specs=[tile(D), tile(NZ), tile(A), tile(2 * A), tile(D)],
        out_shape=[jax.ShapeDtypeStruct((s, D), BF16), jax.ShapeDtypeStruct((s, NZ), F32), jax.ShapeDtypeStruct((s, A), F32),
                   jax.ShapeDtypeStruct((s, 2 * A), BF16), jax.ShapeDtypeStruct((s, D), F32)],
        scratch=[pltpu.VMEM((HALO_A + ts, A), F32), pltpu.VMEM((HALO_S + ts, A), F32), pltpu.VMEM((ts, A), F32)])


def _ffn_fwd(x, g, w_up, wc, w_down, name, comm=None):
    s = x.shape[0]
    ts = min(TS_FFN, s)
    n = s // ts
    bw = FF2 // NCHIP
    rows = 32
    offs = [HALO_S - (K_S - 1) + k for k in range(K_S)]

    def body(x_ref, g_ref, wup_ref, wc_ref, wdn_ref, h_ref, u0_ref, xo_ref, cbuf, act_ref):
        i = pl.program_id(0)

        @pl.when(i == 0)
        def _():
            cbuf[0:HALO_S, :] = jnp.zeros((HALO_S, FF2), F32)

        xv = x_ref[...]
        h = (xv * _rsqrt_mean_sq(xv) * g_ref[...]).astype(BF16)
        h_ref[...] = h
        f = None
        for p in range(NCHIP // 2):
            for j in (p, NCHIP // 2 + p):
                zc = jnp.dot(h, wup_ref[j], preferred_element_type=F32)
                u0_ref[:, j * bw:(j + 1) * bw] = zc
                cbuf[HALO_S:HALO_S + ts, j * bw:(j + 1) * bw] = zc
            for r0 in range(0, ts, rows):
                for c0 in range(p * bw, (p + 1) * bw, LANES):
                    ug = _taps(wc_ref, _shifted(cbuf, r0, rows, c0, LANES, offs), offs, c0, LANES)
                    uv = _taps(wc_ref, _shifted(cbuf, r0, rows, FF + c0, LANES, offs), offs, FF + c0, LANES)
                    act_ref[r0:r0 + rows, c0:c0 + LANES] = (ug * _sigmoid(ug) * uv).astype(BF16)
            fp = jnp.dot(act_ref[:, p * bw:(p + 1) * bw], wdn_ref[p * bw:(p + 1) * bw, :], preferred_element_type=F32)
            f = fp if f is None else f + fp
        cbuf[0:HALO_S, :] = cbuf[ts:ts + HALO_S, :]
        xo_ref[...] = xv + f

    tile = lambda w: pl.BlockSpec((ts, w), lambda i: (i, 0))
    return _call(
        body, name=name, grid=(n,), comm=comm, args=(x, g, w_up, wc, w_down),
        in_specs=[tile(D), _const((1, D)), _const((NCHIP, D, bw)), _const((K_S, FF2)), _const((FF, D))],
        out_specs=[tile(D), tile(FF2), tile(D)],
        out_shape=[jax.ShapeDtypeStruct((s, D), BF16), jax.ShapeDtypeStruct((s, FF2), F32), jax.ShapeDtypeStruct((s, D), F32)],
        scratch=[pltpu.VMEM((HALO_S + ts, FF2), F32), pltpu.VMEM((ts, FF), BF16)])


def _pool_windows(hbuf, pbuf, tile_row0, ts):
    def chunk(ci, carry):
        r0 = pl.multiple_of(ci * R_CHUNK, R_CHUNK)
        t1 = (tile_row0 + r0 + lax.broadcasted_iota(jnp.int32, (R_CHUNK, 1), 0) + 1).astype(F32)
        for gi, w in enumerate(POOL_WINDOWS):
            cnt = jnp.minimum(t1, float(w))
            offs = [HALO_P - jj for jj in range(w)]
            for c0 in range(gi * PG, (gi + 1) * PG, LANES):
                sh = _shifted(hbuf, r0, R_CHUNK, c0, LANES, offs)
                tot = sh[offs[0]]
                for o in offs[1:]:
                    tot = tot + sh[o]
                pbuf[pl.ds(r0, R_CHUNK), c0:c0 + LANES] = (tot / cnt - sh[HALO_P]).astype(BF16)
        return carry

    lax.fori_loop(0, ts // R_CHUNK, chunk, 0)


def _assemble_wpool(wp_ref, wps):
    rb = PG // NCHIP
    for gi in range(len(POOL_WINDOWS)):
        for j in range(NCHIP):
            wps[gi, j * rb:(j + 1) * rb, :] = wp_ref[j, gi]


def _pool_fwd(x, g, w_pool, scale):
    s = x.shape[0]
    ts = min(TS_POOL, s)
    n = s // ts
    ng = len(POOL_WINDOWS)

    def body(x_ref, g_ref, wp_ref, sc_ref, xo_ref, hbuf, pbuf, wps):
        i = pl.program_id(0)

        @pl.when(i == 0)
        def _():
            hbuf[0:HALO_P, :] = jnp.zeros((HALO_P, D), F32)
            _assemble_wpool(wp_ref, wps)

        xv = x_ref[...]
        hbuf[HALO_P:HALO_P + ts, :] = xv * _rsqrt_mean_sq(xv) * g_ref[...]
        _pool_windows(hbuf, pbuf, i * ts, ts)
        hbuf[0:HALO_P, :] = hbuf[ts:ts + HALO_P, :]
        for gi in range(ng):
            cols = slice(gi * PG, (gi + 1) * PG)
            y = jnp.dot(pbuf[:, cols], wps[gi], preferred_element_type=F32)
            xo_ref[:, cols] = xv[:, cols] + y * sc_ref[:, cols]

    tile = pl.BlockSpec((ts, D), lambda i: (i, 0))
    return pl.pallas_call(
        body, name="pool_fwd", grid=(n,),
        in_specs=[tile, _const((1, D)), _const((NCHIP, ng, PG // NCHIP, PG)), _const((1, D))],
        out_specs=tile, out_shape=jax.ShapeDtypeStruct((s, D), F32),
        scratch_shapes=[pltpu.VMEM((HALO_P + ts, D), F32), pltpu.VMEM((ts, D), BF16), pltpu.VMEM((ng, PG, PG), BF16)],
        compiler_params=_cparams(1),
    )(x, g, w_pool, scale)


def _final_loss(x, g, target):
    s = x.shape[0]
    ts = min(TS_MM, s)
    n = s // ts

    def body(x_ref, g_ref, t_ref, dx_ref, dg_ref, loss_ref):
        i = pl.program_id(0)

        @pl.when(i == 0)
        def _():
            dg_ref[...] = jnp.zeros((1, D), F32)
            loss_ref[...] = jnp.zeros((1, LANES), F32)

        xv = x_ref[...]
        r = _rsqrt_mean_sq(xv)
        xh = xv * r
        gv = g_ref[...]
        err = xh * gv - t_ref[...]
        sq = jnp.sum(jnp.sum(err * err, axis=1, keepdims=True), axis=0, keepdims=True)
        loss_ref[...] += sq * (0.5 / D)
        dy = err * (1.0 / D)
        dg_ref[...] += jnp.sum(dy * xh, axis=0, keepdims=True)
        dx_ref[...] = _rms_bwd(dy, xh, r, gv)

    tile = pl.BlockSpec((ts, D), lambda i: (i, 0))
    return pl.pallas_call(
        body, name="final_loss", grid=(n,),
        in_specs=[tile, _const((1, D)), tile],
        out_specs=[tile, pl.BlockSpec((1, D), lambda i: (0, 0)), pl.BlockSpec((1, LANES), lambda i: (0, 0))],
        out_shape=[jax.ShapeDtypeStruct((s, D), F32), jax.ShapeDtypeStruct((1, D), F32), jax.ShapeDtypeStruct((1, LANES), F32)],
        compiler_params=_cparams(1),
    )(x, g, target)


FFN_GROUPS = (512, 512, 384)


def _ffn_bwd(dxo, h, u0, wc, w_up, w_down, p, name, first=None, x=None, g=None, comm=None):
    s = dxo.shape[0]
    ts = min(TS_FFN, s)
    n = s // ts
    bw = FF2 // NCHIP
    rows = 32
    offs = [HALO_S - (K_S - 1) + k for k in range(K_S)]
    boffs = [K_S - 1 - k for k in range(K_S)]
    second = first is not None
    n_in = 10 + (5 if second else 0)

    def body(*refs):
        dxo_ref, h_ref, ug_ref, uv_ref, hg_ref, hv_ref, wcg_ref, wcv_ref, wup_ref, wdn_ref = refs[:10]
        if second:
            dh0_ref, x_ref, g_ref = refs[10:13]
        outs = refs[n_in:]
        dx_ref = outs[0]
        dg_ref = outs[1] if second else None
        dwcg_ref, dwcv_ref, dwup_ref, dwdn_ref = outs[1 + second:5 + second]
        ubuf, dubuf, dact, du0, act, dwacc = outs[5 + second:]
        i = pl.program_id(0)
        j = n - 1 - i

        @pl.when(i == 0)
        def _():
            dubuf[ts:ts + HALO_S, :] = jnp.zeros((HALO_S, 2 * bw), F32)
            dwacc[...] = jnp.zeros(dwacc.shape, F32)
            dwup_ref[...] = jnp.zeros(dwup_ref.shape, F32)
            dwdn_ref[...] = jnp.zeros(dwdn_ref.shape, F32)
            if second:
                dg_ref[...] = jnp.zeros((1, D), F32)

        ubuf[0:HALO_S, 0:bw] = jnp.where(j == 0, 0.0, hg_ref[...])
        ubuf[0:HALO_S, bw:2 * bw] = jnp.where(j == 0, 0.0, hv_ref[...])
        ubuf[HALO_S:HALO_S + ts, 0:bw] = ug_ref[...]
        ubuf[HALO_S:HALO_S + ts, bw:2 * bw] = uv_ref[...]
        df = dxo_ref[...].astype(BF16)
        ht = h_ref[...].T
        wcs = (wcg_ref, wcv_ref)
        dh = None
        c0 = 0
        for gw in FFN_GROUPS:
            gs = slice(c0, c0 + gw)
            dact[:, gs] = lax.dot_general(df, wdn_ref[gs, :], NT_DIMS, preferred_element_type=F32)
            for r0 in range(0, ts, rows):
                rs = slice(r0, r0 + rows)
                for c in range(c0, c0 + gw, LANES):
                    cs, vs = slice(c, c + LANES), slice(bw + c, bw + c + LANES)
                    shg = _shifted(ubuf, r0, rows, c, LANES, offs)
                    shv = _shifted(ubuf, r0, rows, bw + c, LANES, offs)
                    ug = _taps(wcg_ref, shg, offs, c, LANES)
                    uv = _taps(wcv_ref, shv, offs, c, LANES)
                    sg = _sigmoid(ug)
                    da = dact[rs, cs]
                    act[rs, cs] = (ug * sg * uv).astype(BF16)
                    dgv = da * uv * sg * (1.0 + ug * (1.0 - sg))
                    dvv = da * ug * sg
                    dubuf[rs, cs] = dgv
                    dubuf[rs, vs] = dvv
                    for k, o in enumerate(offs):
                        ks = slice(SUBLANES * k, SUBLANES * (k + 1))
                        dwacc[ks, cs] += _rowsum8(dgv * shg[o])
                        dwacc[ks, vs] += _rowsum8(dvv * shv[o])
            for part in range(2):
                for r0 in range(0, ts, rows):
                    for c in range(c0, c0 + gw, LANES):
                        sh = _shifted(dubuf, r0, rows, part * bw + c, LANES, boffs)
                        du0[r0:r0 + rows, part * bw + c:part * bw + c + LANES] = _taps(wcs[part], sh, boffs, c, LANES).astype(BF16)
                dpart = du0[:, part * bw + c0:part * bw + c0 + gw]
                dhp = lax.dot_general(dpart, wup_ref[part, :, gs], NT_DIMS, preferred_element_type=F32)
                dh = dhp if dh is None else dh + dhp
                dwup_ref[part, :, gs] += jnp.dot(ht, dpart, preferred_element_type=F32)
            dwdn_ref[gs, :] += lax.dot_general(act[:, gs], df, TN_DIMS, preferred_element_type=F32)
            c0 += gw
        dubuf[ts:ts + HALO_S, :] = dubuf[0:HALO_S, :]

        if second:
            dh = dh + dh0_ref[...]
            xv = x_ref[...]
            r = _rsqrt_mean_sq(xv)
            xh = xv * r
            dg_ref[...] += jnp.sum(dh * xh, axis=0, keepdims=True)
            dx_ref[...] = dxo_ref[...] + _rms_bwd(dh, xh, r, g_ref[...])
        else:
            dx_ref[...] = dh

        @pl.when(i == n - 1)
        def _():
            for k in range(K_S):
                ks = slice(SUBLANES * k, SUBLANES * (k + 1))
                dwcg_ref[k:k + 1, :] = jnp.sum(dwacc[ks, 0:bw], axis=0, keepdims=True)
                dwcv_ref[k:k + 1, :] = jnp.sum(dwacc[ks, bw:2 * bw], axis=0, keepdims=True)

    half2 = NCHIP // 2
    rev = lambda w, cb=0: pl.BlockSpec((ts, w), lambda i: (n - 1 - i, cb))
    halo = lambda cb: pl.BlockSpec((HALO_S, bw), lambda i: (jnp.maximum((n - 1 - i) * (ts // HALO_S) - 1, 0), cb))
    wcspec = lambda cb: pl.BlockSpec((K_S, bw), lambda i: (0, cb), pipeline_mode=pl.Buffered(1))
    in_specs = [rev(D), rev(D), rev(bw, p), rev(bw, half2 + p), halo(p), halo(half2 + p), wcspec(p), wcspec(half2 + p),
                pl.BlockSpec((half2, None, D, bw), lambda i: (0, p, 0, 0), pipeline_mode=pl.Buffered(1)),
                pl.BlockSpec((bw, D), lambda i: (p, 0), pipeline_mode=pl.Buffered(1))]
    args = [dxo, h, u0, u0, u0, u0, wc, wc, w_up.reshape(half2, half2, D, bw), w_down]
    vec = pl.BlockSpec((1, D), lambda i: (0, 0))
    small = pl.BlockSpec((K_S, bw), lambda i: (0, 0))
    out_specs = [rev(D)] + [vec] * second + [small, small,
                 pl.BlockSpec((half2, None, D, bw), lambda i: (0, p, 0, 0)), pl.BlockSpec((bw, D), lambda i: (p, 0))]
    out_shape = [jax.ShapeDtypeStruct((s, D), F32)] + [jax.ShapeDtypeStruct((1, D), F32)] * second + [
        jax.ShapeDtypeStruct((K_S, bw), F32), jax.ShapeDtypeStruct((K_S, bw), F32),
        jax.ShapeDtypeStruct((half2, half2, D, bw), F32), jax.ShapeDtypeStruct((FF, D), F32)]
    aliases = {}
    if second:
        dh0, _, _, dwup0, dwdn0 = first
        in_specs += [rev(D), rev(D), _const((1, D)), ANY, ANY]
        args += [dh0, x, g, dwup0, dwdn0]
        aliases = {13: 4, 14: 5}
    return _call(
        body, name=name, grid=(n,), comm=comm, args=tuple(args), in_specs=in_specs, out_specs=out_specs, out_shape=out_shape,
        aliases=aliases,
        scratch=[pltpu.VMEM((HALO_S + ts, 2 * bw), F32), pltpu.VMEM((ts + HALO_S, 2 * bw), F32), pltpu.VMEM((ts, bw), F32),
                 pltpu.VMEM((ts, 2 * bw), BF16), pltpu.VMEM((ts, bw), BF16), pltpu.VMEM((SUBLANES * K_S, 2 * bw), F32)])


def _nt_rms_bwd(dy, w, x, g, dres, name, comm=None):
    s = x.shape[0]
    ts = min(TS_MM, s)
    n = s // ts
    nw = dy.shape[1]
    bw = nw // NCHIP

    def body(dy_ref, w_ref, x_ref, g_ref, dres_ref, dx_ref, dg_ref):
        i = pl.program_id(0)

        @pl.when(i == 0)
        def _():
            dg_ref[...] = jnp.zeros((1, D), F32)

        dh = lax.dot_general(dy_ref[:, 0:bw], w_ref[0], NT_DIMS, preferred_element_type=F32)
        for j in range(1, NCHIP):
            dh = dh + lax.dot_general(dy_ref[:, j * bw:(j + 1) * bw], w_ref[j], NT_DIMS, preferred_element_type=F32)
        xv = x_ref[...]
        r = _rsqrt_mean_sq(xv)
        xh = xv * r
        dg_ref[...] += jnp.sum(dh * xh, axis=0, keepdims=True)
        dx_ref[...] = dres_ref[...] + _rms_bwd(dh, xh, r, g_ref[...])

    tile = lambda wd: pl.BlockSpec((ts, wd), lambda i: (i, 0))
    return _call(
        body, name=name, grid=(n,), comm=comm, args=(dy, w, x, g, dres),
        in_specs=[tile(nw), _const((NCHIP, D, bw)), tile(D), _const((1, D)), tile(D)],
        out_specs=[tile(D), pl.BlockSpec((1, D), lambda i: (0, 0))],
        out_shape=[jax.ShapeDtypeStruct((s, D), F32), jax.ShapeDtypeStruct((1, D), F32)])


def _wgrad(a, b, bn, out_shape, out_block, out_index, name, comm=None):
    s, m = a.shape
    nb = b.shape[1] // bn
    tk = min(TS_MM, s)
    nk = s // tk

    def body(a_ref, b_ref, o_ref):
        k = pl.program_id(1)

        @pl.when(k == 0)
        def _():
            o_ref[...] = jnp.zeros(o_ref.shape, F32)

        o_ref[...] += lax.dot_general(a_ref[...], b_ref[...].astype(BF16), TN_DIMS, preferred_element_type=F32)

    (out,), comm_out = _call(
        body, name=name, grid=(nb, nk), comm=comm, args=(a, b),
        in_specs=[pl.BlockSpec((tk, m), lambda jn, k: (k, 0)), pl.BlockSpec((tk, bn), lambda jn, k: (k, jn))],
        out_specs=[pl.BlockSpec(out_block, lambda jn, k: out_index(jn))], out_shape=[jax.ShapeDtypeStruct(out_shape, F32)])
    return out, comm_out


def _pool_bwd(dxo, x, g, w_pool, scale, comm=None):
    s = x.shape[0]
    ts = min(TS_POOL, s)
    n = s // ts
    ng = len(POOL_WINDOWS)
    rb = PG // NCHIP

    def body(dxo_ref, x_ref, halo_ref, g_ref, wp_ref, sc_ref, dx_ref, dwp_ref, dsc_ref, dg_ref,
             hbuf, pbuf, qbuf, dhbuf, wps, dwacc):
        i = pl.program_id(0)
        j = n - 1 - i

        @pl.when(i == 0)
        def _():
            qbuf[ts:ts + HALO_P, :] = jnp.zeros((HALO_P, D), F32)
            dwacc[...] = jnp.zeros(dwacc.shape, F32)
            dsc_ref[...] = jnp.zeros((1, D), F32)
            dg_ref[...] = jnp.zeros((1, D), F32)
            _assemble_wpool(wp_ref, wps)

        gv = g_ref[...]
        xl = halo_ref[...]
        hbuf[0:HALO_P, :] = jnp.where(j == 0, 0.0, xl * _rsqrt_mean_sq(xl) * gv)
        xv = x_ref[...]
        r = _rsqrt_mean_sq(xv)
        xh = xv * r
        hbuf[HALO_P:HALO_P + ts, :] = xh * gv
        _pool_windows(hbuf, pbuf, j * ts, ts)

        dy = dxo_ref[...]
        t1 = (j * ts + lax.broadcasted_iota(jnp.int32, (ts, 1), 0) + 1).astype(F32)
        for gi, w in enumerate(POOL_WINDOWS):
            cols = slice(gi * PG, (gi + 1) * PG)
            p = pbuf[:, cols]
            y = jnp.dot(p, wps[gi], preferred_element_type=F32)
            dsc_ref[:, cols] += jnp.sum(dy[:, cols] * y, axis=0, keepdims=True)
            dq = (dy[:, cols] * sc_ref[:, cols]).astype(BF16)
            dwacc[gi] += lax.dot_general(p, dq, TN_DIMS, preferred_element_type=F32)
            dp = lax.dot_general(dq, wps[gi], NT_DIMS, preferred_element_type=F32)
            qbuf[0:ts, cols] = dp / jnp.minimum(t1, float(w))

        def chunk(ci, carry):
            r0 = pl.multiple_of(ci * R_CHUNK, R_CHUNK)
            tc = (j * ts + r0 + lax.broadcasted_iota(jnp.int32, (R_CHUNK, 1), 0) + 1).astype(F32)
            for gi, w in enumerate(POOL_WINDOWS):
                cnt = jnp.minimum(tc, float(w))
                offs = list(range(w))
                for c0 in range(gi * PG, (gi + 1) * PG, LANES):
                    sh = _shifted(qbuf, r0, R_CHUNK, c0, LANES, offs)
                    tot = sh[0]
                    for o in offs[1:]:
                        tot = tot + sh[o]
                    dhbuf[pl.ds(r0, R_CHUNK), c0:c0 + LANES] = tot - sh[0] * cnt
            return carry

        lax.fori_loop(0, ts // R_CHUNK, chunk, 0)
        qbuf[ts:ts + HALO_P, :] = qbuf[0:HALO_P, :]
        dh = dhbuf[...]
        dg_ref[...] += jnp.sum(dh * xh, axis=0, keepdims=True)
        dx_ref[...] = dy + _rms_bwd(dh, xh, r, gv)

        @pl.when(i == n - 1)
        def _():
            for gi in range(ng):
                for jj in range(NCHIP):
                    dwp_ref[jj, gi] = dwacc[gi, jj * rb:(jj + 1) * rb, :]

    rev = pl.BlockSpec((ts, D), lambda i: (n - 1 - i, 0))
    halo = pl.BlockSpec((HALO_P, D), lambda i: (jnp.maximum((n - 1 - i) * (ts // HALO_P) - 1, 0), 0))
    vec = pl.BlockSpec((1, D), lambda i: (0, 0))
    return _call(
        body, name="pool_bwd", grid=(n,), comm=comm, args=(dxo, x, x, g, w_pool, scale),
        in_specs=[rev, rev, halo, _const((1, D)), _const((NCHIP, ng, rb, PG)), _const((1, D))],
        out_specs=[rev, pl.BlockSpec((NCHIP, ng, rb, PG), lambda i: (0, 0, 0, 0)), vec, vec],
        out_shape=[jax.ShapeDtypeStruct((s, D), F32), jax.ShapeDtypeStruct((NCHIP, ng, rb, PG), F32),
                   jax.ShapeDtypeStruct((1, D), F32), jax.ShapeDtypeStruct((1, D), F32)],
        scratch=[pltpu.VMEM((HALO_P + ts, D), F32), pltpu.VMEM((ts, D), BF16), pltpu.VMEM((ts + HALO_P, D), F32),
                 pltpu.VMEM((ts, D), F32), pltpu.VMEM((ng, PG, PG), BF16), pltpu.VMEM((ng, PG, PG), F32)])


def _mix0_bwd_a(dx1, z, ac, w_out, conv_a, ln_g, ln_b, conv_b, comm=None):
    s = dx1.shape[0]
    ts = min(TS_MIXB, s)
    n = s // ts
    rows = 32
    offs_a = [HALO_A - (K_A - 1) + k for k in range(K_A)]
    offs_s = [HALO_S - (K_S - 1) + k for k in range(K_S)]
    boffs_a = [K_A - 1 - k for k in range(K_A)]
    boffs_s = [K_S - 1 - k for k in range(K_S)]

    def body(dx1_ref, z_ref, zh_ref, ac_ref, wout_ref, ca_ref, lg_ref, lb_ref, cb_ref,
             dz_ref, dca_ref, dlg_ref, dlb_ref, dcb_ref,
             glu_buf, cv_buf, dac_buf, dbc_buf, db_buf, dca_acc, dcb_acc):
        i = pl.program_id(0)
        j = n - 1 - i

        @pl.when(i == 0)
        def _():
            dac_buf[ts:ts + HALO_A, :] = jnp.zeros((HALO_A, A), F32)
            dbc_buf[ts:ts + HALO_S, :] = jnp.zeros((HALO_S, A), F32)
            dca_acc[...] = jnp.zeros(dca_acc.shape, F32)
            dcb_acc[...] = jnp.zeros(dcb_acc.shape, F32)
            dlg_ref[...] = jnp.zeros((1, A), F32)
            dlb_ref[...] = jnp.zeros((1, A), F32)

        glu_h = zh_ref[:, 0:A] * _sigmoid(zh_ref[:, A:2 * A])
        glu_buf[0:HALO_A, :] = jnp.where(j == 0, 0.0, glu_h)
        hs = slice(HALO_A - HALO_S, HALO_A)
        cv_buf[0:HALO_S, :] = jnp.where(j == 0, 0.0, zh_ref[hs, 3 * A:4 * A] * zh_ref[hs, 4 * A:5 * A])
        glu_buf[HALO_A:HALO_A + ts, :] = z_ref[:, 0:A] * _sigmoid(z_ref[:, A:2 * A])
        cv_buf[HALO_S:HALO_S + ts, :] = z_ref[:, 3 * A:4 * A] * z_ref[:, 4 * A:5 * A]

        dcat = lax.dot_general(dx1_ref[...].astype(BF16), wout_ref[...], NT_DIMS, preferred_element_type=F32)
        db_buf[...] = dcat[:, A:2 * A]
        ac = ac_ref[...]
        xc = ac - jnp.mean(ac, axis=-1, keepdims=True)
        rstd = lax.rsqrt(jnp.mean(xc * xc, axis=-1, keepdims=True) + LN_EPS)
        xn = xc * rstd
        lg = lg_ref[...]
        ln = xn * lg + lb_ref[...]
        sl = _sigmoid(ln)
        dln = dcat[:, 0:A] * sl * (1.0 + ln * (1.0 - sl))
        dlg_ref[...] += jnp.sum(dln * xn, axis=0, keepdims=True)
        dlb_ref[...] += jnp.sum(dln, axis=0, keepdims=True)
        dxn = dln * lg
        dac_buf[0:ts, :] = rstd * (dxn - jnp.mean(dxn, axis=-1, keepdims=True)
                                   - xn * jnp.mean(dxn * xn, axis=-1, keepdims=True))

        def chunk(ci, carry):
            r0 = pl.multiple_of(ci * rows, rows)
            rs = pl.ds(r0, rows)
            for c0 in range(0, A, LANES):
                cs = slice(c0, c0 + LANES)
                shc = _shifted(cv_buf, r0, rows, c0, LANES, offs_s)
                bconv = _taps(cb_ref, shc, offs_s, c0, LANES)
                dbv = db_buf[rs, cs]
                dz_ref[rs, 2 * A + c0:2 * A + c0 + LANES] = (dbv * bconv).astype(BF16)
                dbconv = dbv * z_ref[rs, 2 * A + c0:2 * A + c0 + LANES]
                dbc_buf[rs, cs] = dbconv
                for k, o in enumerate(offs_s):
                    dcb_acc[SUBLANES * k:SUBLANES * (k + 1), cs] += _rowsum8(dbconv * shc[o])
                shg = _shifted(glu_buf, r0, rows, c0, LANES, offs_a)
                dacv = dac_buf[rs, cs]
                for k, o in enumerate(offs_a):
                    dca_acc[SUBLANES * k:SUBLANES * (k + 1), cs] += _rowsum8(dacv * shg[o])
            return carry

        lax.fori_loop(0, ts // rows, chunk, 0)

        def chunk2(ci, carry):
            r0 = pl.multiple_of(ci * rows, rows)
            rs = pl.ds(r0, rows)
            for c0 in range(0, A, LANES):
                col = lambda grp: slice(grp * A + c0, grp * A + c0 + LANES)
                dglu = _taps(ca_ref, _shifted(dac_buf, r0, rows, c0, LANES, boffs_a), boffs_a, c0, LANES)
                sg = _sigmoid(z_ref[rs, col(1)])
                dz_ref[rs, col(0)] = (dglu * sg).astype(BF16)
                dz_ref[rs, col(1)] = (dglu * z_ref[rs, col(0)] * sg * (1.0 - sg)).astype(BF16)
                dcv = _taps(cb_ref, _shifted(dbc_buf, r0, rows, c0, LANES, boffs_s), boffs_s, c0, LANES)
                dz_ref[rs, col(3)] = (dcv * z_ref[rs, col(4)]).astype(BF16)
                dz_ref[rs, col(4)] = (dcv * z_ref[rs, col(3)]).astype(BF16)
            return carry

        lax.fori_loop(0, ts // rows, chunk2, 0)
        dac_buf[ts:ts + HALO_A, :] = dac_buf[0:HALO_A, :]
        dbc_buf[ts:ts + HALO_S, :] = dbc_buf[0:HALO_S, :]

        @pl.when(i == n - 1)
        def _():
            _finish_tap_sums(dca_acc, dca_ref, K_A)
            _finish_tap_sums(dcb_acc, dcb_ref, K_S)

    rev = lambda w: pl.BlockSpec((ts, w), lambda i: (n - 1 - i, 0))
    halo = pl.BlockSpec((HALO_A, NZ), lambda i: (jnp.maximum((n - 1 - i) * (ts // HALO_A) - 1, 0), 0))
    full = lambda r, c: pl.BlockSpec((r, c), lambda i: (0, 0))
    return _call(
        body, name="mix0_bwd_a", grid=(n,), comm=comm, args=(dx1, z, z, ac, w_out, conv_a, ln_g, ln_b, conv_b),
        in_specs=[rev(D), rev(NZ), halo, rev(A), _const((2 * A, D)), _const((K_A, A)), _const((1, A)), _const((1, A)),
                  _const((K_S, A))],
        out_specs=[rev(NZ), full(K_A, A), full(1, A), full(1, A), full(K_S, A)],
        out_shape=[jax.ShapeDtypeStruct((s, NZ), BF16), jax.ShapeDtypeStruct((K_A, A), F32), jax.ShapeDtypeStruct((1, A), F32),
                   jax.ShapeDtypeStruct((1, A), F32), jax.ShapeDtypeStruct((K_S, A), F32)],
        scratch=[pltpu.VMEM((HALO_A + ts, A), F32), pltpu.VMEM((HALO_S + ts, A), F32), pltpu.VMEM((ts + HALO_A, A), F32),
                 pltpu.VMEM((ts + HALO_S, A), F32), pltpu.VMEM((ts, A), F32), pltpu.VMEM((SUBLANES * K_A, A), F32),
                 pltpu.VMEM((SUBLANES * K_S, A), F32)])


def _train_step(x, target, place, shard, rep, small_pack, small_shapes):
    bw_up, bw_in = FF2 // NCHIP, NZ // NCHIP
    five = lambda a, k: a.reshape(NCHIP, 2, *HALF[k])

    g_in, g_out, small_g = _run_comm(_gather_comm([shard["w_in"], shard["w_out"]], small_pack), "ag_first")
    whole = {}
    for k, part in zip(SMALL_SHARDED, _unpack(small_g, small_shapes, lead=(NCHIP,))):
        whole[k] = jnp.moveaxis(part, 0, 1).reshape(part.shape[1], NCHIP * part.shape[2])
    w_in, w_out = g_in.reshape(NCHIP, D, bw_in), g_out.reshape(2 * A, D)
    conv_ffn = whole["conv_ffn_w"].reshape(2, K_S, FF2)
    nffn = [rep["norm_ffn"][0:1], rep["norm_ffn"][1:2]]

    (h0, z, ac, cat, x1), (g_up0, g_dn0) = _mix0_fwd(
        x, rep["norm_mix_even"], w_in, whole["conv_a"], rep["ln_a_g"], rep["ln_a_b"], whole["conv_b"], w_out,
        comm=_gather_comm([shard["w_up0"], shard["w_down0"]]))
    w_up0, w_dn0 = g_up0.reshape(NCHIP, D, bw_up), g_dn0.reshape(FF, D)
    (hf0, u00, x2), (g_pool, g_up1, g_dn1) = _ffn_fwd(
        x1, nffn[0], w_up0, conv_ffn[0], w_dn0, "ffn0_fwd",
        comm=_gather_comm([shard["w_pool"], shard["w_up1"], shard["w_down1"]]))
    w_pool = g_pool.reshape(NCHIP, len(POOL_WINDOWS), PG // NCHIP, PG)
    w_up1, w_dn1 = g_up1.reshape(NCHIP, D, bw_up), g_dn1.reshape(FF, D)
    x3 = _pool_fwd(x2, whole["norm_mix_odd"], w_pool, whole["pool_scale"])
    (hf1, u01, x4), _ = _ffn_fwd(x3, nffn[1], w_up1, conv_ffn[1], w_dn1, "ffn1_fwd")
    dx4, g_nfin, loss_part = _final_loss(x4, rep["norm_final"], target)

    def ffn_bwd(dxo, hf, u0, layer, x_in, w_up, w_dn, comm):
        tag = "ffn%d_bwd_h" % layer
        first, brought = _ffn_bwd(dxo, hf, u0, conv_ffn[layer], w_up, w_dn, 0, tag + "0", comm=comm)
        (dx, g_nf, cg1, cv1, g_up, g_dn), _ = _ffn_bwd(dxo, hf, u0, conv_ffn[layer], w_up, w_dn, 1, tag + "1", first=first,
                                                       x=x_in, g=nffn[layer])
        g_wc = jnp.concatenate([first[1], cg1, first[2], cv1], axis=1)
        return dx, g_nf, g_wc, (five(g_up, "w_up"), five(g_dn, "w_down")), brought

    def pair_sums(keys, grads, landed, tag):
        return [_pair_sum(place, g, ld, "pair_sum_" + k + tag) for k, g, ld in zip(keys, grads, landed)]

    dx3, g_nf1, g_wc1, grads1, _ = ffn_bwd(dx4, hf1, u01, 1, x3, w_up1, w_dn1, None)
    (dx2, g_wpool, g_scale, g_nmo), landed1 = _pool_bwd(dx3, x2, whole["norm_mix_odd"], w_pool, whole["pool_scale"],
                                                        comm=_pair_comm(grads1))
    sums1 = pair_sums(("w_up", "w_down"), grads1, landed1, "1")
    dx1, g_nf0, g_wc0, grads0, parts1 = ffn_bwd(dx2, hf0, u00, 0, x1, w_up0, w_dn0, _chips_comm(sums1))
    grads0 = grads0 + (five(g_wpool, "w_pool"),)
    (dz, g_ca, g_lg, g_lb, g_cb), landed0 = _mix0_bwd_a(
        dx1, z, ac, w_out, whole["conv_a"], rep["ln_a_g"], rep["ln_a_b"], whole["conv_b"], comm=_pair_comm(grads0))
    sums0 = pair_sums(("w_up", "w_down", "w_pool"), grads0, landed0, "0")
    gr_out, _ = _wgrad(cat, dx1, D, (D, D), (D, D), lambda jn: (0, 0), "wgrad_out")
    gr_in, parts0 = _wgrad(h0, dz, bw_in, (NCHIP, D, bw_in), (None, D, bw_in), lambda jn: (jn, 0, 0), "wgrad_in",
                           comm=_chips_comm(sums0))
    gradsm = (five(gr_in, "w_in"), five(gr_out, "w_out"))
    (grad_x, g_nme), landedm = _nt_rms_bwd(dz, w_in, x, rep["norm_mix_even"], dx1, "mix0_bwd_b", comm=_pair_comm(gradsm))
    sumsm = pair_sums(("w_in", "w_out"), gradsm, landedm, "")

    small = {"norm_mix_even": g_nme, "conv_a": g_ca, "ln_a_g": g_lg, "ln_a_b": g_lb, "conv_b": g_cb, "norm_mix_odd": g_nmo,
             "pool_scale": g_scale, "norm_ffn": jnp.concatenate([g_nf0, g_nf1], axis=0),
             "conv_ffn_w": jnp.stack([g_wc0, g_wc1]), "norm_final": g_nfin}
    *partsm, small_all = _run_comm(_chips_comm(sumsm, _pack([small[k] for k in SMALL_ALL] + [loss_part])), "rs_last")

    tot = lambda k, g, ld, p, layer=0, nl=1, prev=None: _chip_sum(place, g, ld, p, layer, nl, "chip_sum_%s%d" % (k, layer), prev)
    t_in = tot("w_in", gradsm[0], landedm[0], partsm[0])
    t_out = tot("w_out", gradsm[1], landedm[1], partsm[1])
    t_pool = tot("w_pool", grads0[2], landed0[2], parts0[2])
    t_up = tot("w_up", grads1[0], landed1[0], parts1[0], 1, 2)
    t_up = tot("w_up", grads0[0], landed0[0], parts0[0], 0, 2, t_up)
    t_dn = tot("w_down", grads1[1], landed1[1], parts1[1], 1, 2)
    t_dn = tot("w_down", grads0[1], landed0[1], parts0[1], 0, 2, t_dn)
    swapped = _run_comm(_swap_comm([t_in, t_out, t_pool, t_up, t_dn]), "rs_swap")
    return grad_x, dict(zip(BIG, swapped)), _sum_devices(small_all), [small[k].shape for k in SMALL_ALL] + [(1, LANES)]


def _adamw_math(w, g, m, v):
    m = ADAM_B1 * m + (1.0 - ADAM_B1) * g
    v = ADAM_B2 * v + (1.0 - ADAM_B2) * (g * g)
    m_hat = m / (1.0 - ADAM_B1 ** ADAM_STEP)
    v_hat = v / (1.0 - ADAM_B2 ** ADAM_STEP)
    return -ADAM_LR * (m_hat / (jnp.sqrt(v_hat) + ADAM_EPS) + ADAM_WD * w), m, v


def _adamw_big(w, g, m, v, tr, name):
    nl, rows, cols = w.shape

    def body(w_ref, g_ref, m_ref, v_ref, d_ref, m2_ref, v2_ref):
        d_ref[...], m2_ref[...], v2_ref[...] = _adamw_math(w_ref[...], g_ref[...], m_ref[...], v_ref[...])

    spec = pl.BlockSpec((None, tr, cols), lambda l, r: (l, r, 0))
    return pl.pallas_call(
        body, name=name, grid=(nl, rows // tr), in_specs=[spec] * 4, out_specs=[spec] * 3,
        out_shape=[jax.ShapeDtypeStruct(w.shape, F32)] * 3, compiler_params=_cparams(2),
    )(w, g, m, v)


def _adamw_small(ws, gs, ms, vs):
    n = len(ws)

    def body(*refs):
        for p in range(n):
            w_ref, g_ref, m_ref, v_ref = (refs[q * n + p] for q in range(4))
            d_ref, m2_ref, v2_ref = (refs[(4 + q) * n + p] for q in range(3))
            d_ref[...], m2_ref[...], v2_ref[...] = _adamw_math(w_ref[...], g_ref[...], m_ref[...], v_ref[...])

    whole = lambda a: pl.BlockSpec(a.shape, lambda: (0,) * a.ndim)
    outs = pl.pallas_call(
        body, name="adamw_small", in_specs=[whole(a) for a in ws] * 4, out_specs=[whole(a) for a in ws] * 3,
        out_shape=[jax.ShapeDtypeStruct(a.shape, F32) for a in ws] * 3,
        compiler_params=pltpu.CompilerParams(vmem_limit_bytes=VMEM_LIMIT_BYTES),
    )(*ws, *gs, *ms, *vs)
    return outs[0:n], outs[n:2 * n], outs[2 * n:3 * n]


def _place():
    x, y, c = lax.axis_index("x"), lax.axis_index("y"), lax.axis_index("c")
    chips = [(x, 1 - y), (1 - x, y), (1 - x, 1 - y)]
    blocks = [2 * cx + cy for cx, cy in chips]
    return x, y, c, 2 * x + y, chips, blocks


def _gather_comm(shards, small=None):
    na = len(shards)
    nk = NCHIP - 1
    ns = 0 if small is None else 1

    def copies(ins, outs, sems):
        ici_send, ici_recv, fwd_send, fwd_recv, own_send, own_recv = sems[:6]
        x, y, c, j, chips, blocks = _place()
        sib = (x, y, 1 - c)

        def ici(a, k, arrival):
            dst = outs[a].at[blocks[k], c] if arrival else outs[a].at[j, c]
            return pltpu.make_async_remote_copy(
                src_ref=dst if arrival else ins[a].at[c], dst_ref=dst, send_sem=ici_send.at[a * nk + k],
                recv_sem=ici_recv.at[a * nk + k], device_id=(*chips[k], c), device_id_type=MESH)

        def fwd(a, k, half):
            ref = outs[a].at[blocks[k], half]
            return pltpu.make_async_remote_copy(
                src_ref=ref, dst_ref=ref, send_sem=fwd_send.at[a * nk + k], recv_sem=fwd_recv.at[a * nk + k],
                device_id=sib, device_id_type=MESH)

        own = [pltpu.make_async_remote_copy(src_ref=ins[a], dst_ref=outs[a].at[j], send_sem=own_send.at[a],
                                            recv_sem=own_recv.at[a], device_id=sib, device_id_type=MESH) for a in range(na)]
        small_copies = [pltpu.make_async_remote_copy(
            src_ref=ins[na], dst_ref=outs[na].at[j], send_sem=ici_send.at[na * nk + k], recv_sem=ici_recv.at[na * nk + k],
            device_id=(*chips[k], c), device_id_type=MESH) for k in range(nk * ns)]
        local = [pltpu.make_async_copy(ins[na], outs[na].at[j], sems[6])] if ns else []
        return ici, fwd, own, small_copies, local, c

    def start(ins, outs, sems):
        ici, _, own, small_copies, local, _ = copies(ins, outs, sems)
        for cp in local + own + [ici(a, k, False) for a in range(na) for k in range(nk)] + small_copies:
            cp.start()

    def finish(ins, outs, sems):
        ici, fwd, own, small_copies, local, c = copies(ins, outs, sems)
        for a in range(na):
            for k in range(nk):
                ici(a, k, True).wait_recv()
                fwd(a, k, c).start()
        for cp in small_copies:
            cp.wait()
        for a in range(na):
            for k in range(nk):
                ici(a, k, False).wait_send()
                fwd(a, k, c).wait_send()
                fwd(a, k, 1 - c).wait_recv()
        for cp in own + local:
            cp.wait()

    out_shapes = [jax.ShapeDtypeStruct((NCHIP,) + s.shape, s.dtype) for s in shards]
    sems = [pltpu.SemaphoreType.DMA((na * nk + nk * ns,)), pltpu.SemaphoreType.DMA((na * nk + nk * ns,)),
            pltpu.SemaphoreType.DMA((na * nk,)), pltpu.SemaphoreType.DMA((na * nk,)),
            pltpu.SemaphoreType.DMA((na,)), pltpu.SemaphoreType.DMA((na,))]
    if ns:
        out_shapes.append(jax.ShapeDtypeStruct((NCHIP,) + small.shape, small.dtype))
        sems.append(pltpu.SemaphoreType.DMA)
    return _Comm(list(shards) + [small] * ns, out_shapes, sems, start, finish)


def _simple_comm(inputs, out_shapes, make_copies, n_sems, aliases=None):
    def start(ins, outs, sems):
        for cp in make_copies(ins, outs, sems):
            cp.start()

    def finish(ins, outs, sems):
        for cp in make_copies(ins, outs, sems):
            cp.wait()

    return _Comm(inputs, out_shapes, [pltpu.SemaphoreType.DMA((n,)) for n in n_sems], start, finish, aliases)


def _pair_comm(grads):
    def make_copies(ins, outs, sems):
        x, y, c, _, _, _ = _place()
        return [pltpu.make_async_remote_copy(
            src_ref=ins[a].at[:, 1 - c], dst_ref=outs[a], send_sem=sems[0].at[a], recv_sem=sems[1].at[a],
            device_id=(x, y, 1 - c), device_id_type=MESH) for a in range(len(grads))]

    out_shapes = [jax.ShapeDtypeStruct(g.shape[:1] + g.shape[2:], F32) for g in grads]
    return _simple_comm(grads, out_shapes, make_copies, [len(grads)] * 2)


def _chips_comm(sums, small=None):
    na = len(sums)
    nk = NCHIP - 1

    def make_copies(ins, outs, sems):
        x, y, c, _, chips, blocks = _place()
        copies = [pltpu.make_async_remote_copy(
            src_ref=ins[a].at[blocks[k]], dst_ref=outs[a].at[k], send_sem=sems[0].at[a * nk + k],
            recv_sem=sems[1].at[a * nk + k], device_id=(*chips[k], c), device_id_type=MESH)
            for a in range(na) for k in range(nk)]
        if small is not None:
            me = 4 * x + 2 * y + c
            for r in range(1, NDEV):
                peer = (1 - x if r & 4 else x, 1 - y if r & 2 else y, 1 - c if r & 1 else c)
                copies.append(pltpu.make_async_remote_copy(
                    src_ref=ins[na], dst_ref=outs[na].at[me], send_sem=sems[2].at[r - 1], recv_sem=sems[3].at[r - 1],
                    device_id=peer, device_id_type=MESH))
            copies.append(pltpu.make_async_copy(ins[na], outs[na].at[me], sems[4].at[0]))
        return copies

    out_shapes = [jax.ShapeDtypeStruct((nk,) + g.shape[1:], BF16) for g in sums]
    if small is None:
        return _simple_comm(sums, out_shapes, make_copies, [na * nk] * 2)
    out_shapes.append(jax.ShapeDtypeStruct((NDEV,) + small.shape, F32))
    return _simple_comm(list(sums) + [small], out_shapes, make_copies, [na * nk] * 2 + [NDEV - 1] * 2 + [1])


def _swap_comm(totals):
    def make_copies(ins, outs, sems):
        x, y, c, _, _, _ = _place()
        return [pltpu.make_async_remote_copy(
            src_ref=outs[a].at[:, c], dst_ref=outs[a].at[:, c], send_sem=sems[0].at[a], recv_sem=sems[1].at[a],
            device_id=(x, y, 1 - c), device_id_type=MESH) for a in range(len(totals))]

    out_shapes = [jax.ShapeDtypeStruct(t.shape, F32) for t in totals]
    return _simple_comm(totals, out_shapes, make_copies, [len(totals)] * 2, aliases={a: a for a in range(len(totals))})


def _pair_sum(place, grad, landed, name):
    _, _, rows, cols = grad.shape

    def body(place_ref, g_ref, l_ref, o_ref):
        o_ref[...] = (g_ref[...] + l_ref[...]).astype(BF16)

    return pl.pallas_call(
        body, name=name,
        grid_spec=pltpu.PrefetchScalarGridSpec(
            num_scalar_prefetch=1, grid=(NCHIP,),
            in_specs=[pl.BlockSpec((None, None, rows, cols), lambda b, p: (b, p[1], 0, 0)),
                      pl.BlockSpec((None, rows, cols), lambda b, p: (b, 0, 0))],
            out_specs=pl.BlockSpec((None, rows, cols), lambda b, p: (b, 0, 0))),
        out_shape=jax.ShapeDtypeStruct((NCHIP, rows, cols), BF16), compiler_params=_cparams(1),
    )(place, grad, landed)


def _chip_sum(place, grad, landed, parts, layer, n_layers, name, prev=None):
    _, _, rows, cols = grad.shape

    def body(*refs):
        g_ref, l_ref, p_ref, o_ref = refs[1], refs[2], refs[3], refs[-1]
        tot = g_ref[...] + l_ref[...]
        for k in range(NCHIP - 1):
            tot = tot + p_ref[k].astype(F32)
        o_ref[...] = tot

    in_specs = [pl.BlockSpec((None, None, rows, cols), lambda i, p: (p[0], p[1], 0, 0)),
                pl.BlockSpec((None, rows, cols), lambda i, p: (p[0], 0, 0)),
                pl.BlockSpec((NCHIP - 1, rows, cols), lambda i, p: (0, 0, 0))]
    args = [place, grad, landed, parts]
    if prev is not None:
        in_specs.append(ANY)
        args.append(prev)
    return pl.pallas_call(
        body, name=name,
        grid_spec=pltpu.PrefetchScalarGridSpec(
            num_scalar_prefetch=1, grid=(1,), in_specs=in_specs,
            out_specs=pl.BlockSpec((None, None, rows, cols), lambda i, p: (layer, p[1], 0, 0))),
        out_shape=jax.ShapeDtypeStruct((n_layers, 2, rows, cols), F32),
        input_output_aliases={} if prev is None else {4: 0}, compiler_params=_cparams(1),
    )(*args)


def _sum_devices(parts):
    def body(p_ref, o_ref):
        tot = p_ref[0]
        for d in range(1, NDEV):
            tot = tot + p_ref[d]
        o_ref[...] = tot

    return pl.pallas_call(
        body, name="sum_small", in_specs=[pl.BlockSpec(parts.shape, lambda: (0, 0, 0))],
        out_specs=pl.BlockSpec(parts.shape[1:], lambda: (0, 0)), out_shape=jax.ShapeDtypeStruct(parts.shape[1:], F32),
    )(parts)


def _pack(parts):
    rows = []
    for p in parts:
        p = p.reshape(-1, LANES)
        rows.append(jnp.pad(p, ((0, -p.shape[0] % SUBLANES), (0, 0))))
    return jnp.concatenate(rows, axis=0)


def _unpack(buf, shapes, lead=()):
    out, r0 = [], 0
    nl = len(lead)
    for shp in shapes:
        nrow = 1
        for d in shp:
            nrow *= d
        nrow //= LANES
        out.append(buf[(slice(None),) * nl + (slice(r0, r0 + nrow),)].reshape(lead + tuple(shp)))
        r0 += nrow + (-nrow % SUBLANES)
    return out


WEIGHT_ORDER = ("norm_mix_even", "w_in", "conv_a", "ln_a_g", "ln_a_b", "conv_b", "w_out", "norm_mix_odd", "w_pool",
                "pool_scale", "norm_ffn", "w_up", "conv_ffn_w", "w_down", "norm_final")
BIG = ("w_in", "w_out", "w_pool", "w_up", "w_down")
HALF = {"w_in": (D // 2, NZ // NCHIP), "w_out": (2 * A // NCHIP // 2, D), "w_pool": (PG // 2, PG),
        "w_up": (D // 2, FF2 // NCHIP), "w_down": (FF // NCHIP // 2, D)}
SMALL_SHARDED = ("conv_a", "conv_b", "conv_ffn_w", "norm_mix_odd", "pool_scale")
SMALL_ALL = ("norm_mix_even", "conv_a", "ln_a_g", "ln_a_b", "conv_b", "norm_mix_odd", "pool_scale", "norm_ffn", "conv_ffn_w",
             "norm_final")


def kernel(x, norm_mix_even, w_in, conv_a, ln_a_g, ln_a_b, conv_b, w_out, norm_mix_odd, w_pool, pool_scale, norm_ffn, w_up, conv_ffn_w, w_down, norm_final, loss_target, m_norm_mix_even, m_w_in, m_conv_a, m_ln_a_g, m_ln_a_b, m_conv_b, m_w_out, m_norm_mix_odd, m_w_pool, m_pool_scale, m_norm_ffn, m_w_up, m_conv_ffn_w, m_w_down, m_norm_final, v_norm_mix_even, v_w_in, v_conv_a, v_ln_a_g, v_ln_a_b, v_conv_b, v_w_out, v_norm_mix_odd, v_w_pool, v_pool_scale, v_norm_ffn, v_w_up, v_conv_ffn_w, v_w_down, v_norm_final):
    w = dict(norm_mix_even=norm_mix_even, w_in=w_in, conv_a=conv_a, ln_a_g=ln_a_g, ln_a_b=ln_a_b, conv_b=conv_b, w_out=w_out,
             norm_mix_odd=norm_mix_odd, w_pool=w_pool, pool_scale=pool_scale, norm_ffn=norm_ffn, w_up=w_up,
             conv_ffn_w=conv_ffn_w, w_down=w_down, norm_final=norm_final)
    m = dict(norm_mix_even=m_norm_mix_even, w_in=m_w_in, conv_a=m_conv_a, ln_a_g=m_ln_a_g, ln_a_b=m_ln_a_b, conv_b=m_conv_b,
             w_out=m_w_out, norm_mix_odd=m_norm_mix_odd, w_pool=m_w_pool, pool_scale=m_pool_scale, norm_ffn=m_norm_ffn,
             w_up=m_w_up, conv_ffn_w=m_conv_ffn_w, w_down=m_w_down, norm_final=m_norm_final)
    v = dict(norm_mix_even=v_norm_mix_even, w_in=v_w_in, conv_a=v_conv_a, ln_a_g=v_ln_a_g, ln_a_b=v_ln_a_b, conv_b=v_conv_b,
             w_out=v_w_out, norm_mix_odd=v_norm_mix_odd, w_pool=v_w_pool, pool_scale=v_pool_scale, norm_ffn=v_norm_ffn,
             w_up=v_w_up, conv_ffn_w=v_conv_ffn_w, w_down=v_w_down, norm_final=v_norm_final)
    chip = 2 * lax.axis_index("x") + lax.axis_index("y")
    place = jnp.stack([chip, lax.axis_index("c")]).astype(jnp.int32)

    half = lambda a, name: a.astype(BF16).reshape((2,) + HALF[name])
    shard = {"w_in": half(w_in[0], "w_in"), "w_out": half(w_out[0], "w_out"), "w_pool": half(w_pool[0], "w_pool"),
             "w_up0": half(w_up[0], "w_up"), "w_up1": half(w_up[1], "w_up"),
             "w_down0": half(w_down[0], "w_down"), "w_down1": half(w_down[1], "w_down")}
    small_shapes = [w[k].shape[-2:] if w[k].ndim == 3 and k != "conv_ffn_w" else (w[k].size // w[k].shape[-1], w[k].shape[-1])
                    for k in SMALL_SHARDED]
    rep = dict(norm_mix_even=norm_mix_even, ln_a_g=ln_a_g, ln_a_b=ln_a_b, norm_ffn=norm_ffn, norm_final=norm_final.reshape(1, D))
    grad_x, swapped, small_packed, small_full_shapes = _train_step(
        x[0], loss_target[0], place, shard, rep, _pack([w[k] for k in SMALL_SHARDED]), small_shapes)
    small_sum = _unpack(small_packed, small_full_shapes)
    loss = small_sum[-1][0, 0]

    grad = {k: swapped[k].reshape(w[k].shape) for k in BIG}
    for k, gsum in zip(SMALL_ALL, small_sum):
        if k in SMALL_SHARDED:
            cols = w[k].shape[-1]
            gsum = lax.dynamic_slice_in_dim(gsum, chip * cols, cols, axis=gsum.ndim - 1)
        grad[k] = gsum.reshape(w[k].shape)

    delta, new_m, new_v = {}, {}, {}
    rows_per_step = {"w_in": 512, "w_out": 256, "w_pool": 256, "w_up": 256, "w_down": 352}
    for k in BIG:
        as3 = lambda a: a.reshape(a.shape[0], -1, a.shape[-1])
        d3, m3, v3 = _adamw_big(as3(w[k]), as3(grad[k]), as3(m[k]), as3(v[k]), rows_per_step[k], "adamw_" + k)
        delta[k], new_m[k], new_v[k] = (a.reshape(w[k].shape) for a in (d3, m3, v3))
    as2 = lambda a: a.reshape(-1, a.shape[-1])
    ds, ms, vs = _adamw_small(*[[as2(t[k]) for k in SMALL_ALL] for t in (w, grad, m, v)])
    for k, d2, m2, v2 in zip(SMALL_ALL, ds, ms, vs):
        delta[k], new_m[k], new_v[k] = (a.reshape(w[k].shape) for a in (d2, m2, v2))

    return (loss, grad_x[None], *[grad[k] for k in WEIGHT_ORDER], *[delta[k] for k in WEIGHT_ORDER],
            *[new_m[k] for k in WEIGHT_ORDER], *[new_v[k] for k in WEIGHT_ORDER])
```

```python
import functools

import jax
import jax.numpy as jnp
from jax import lax
from jax.experimental import pallas as pl
from jax.experimental.pallas import tpu as pltpu

F32, BF16 = jnp.float32, jnp.bfloat16

D = 1024
A = 512
NZ = 5 * A
FF = 2816
FF2 = 2 * FF
NCHIP = 4
NDEV = 8
K_A, K_S = 31, 3
POOL_WINDOWS = (2, 4, 8, 16)
PG = D // len(POOL_WINDOWS)
RMS_EPS, LN_EPS = 1e-6, 1e-5
ADAM_LR, ADAM_B1, ADAM_B2, ADAM_EPS, ADAM_WD, ADAM_STEP = 0.001, 0.9, 0.999, 1e-08, 0.01, 10

HALO_A, HALO_S, HALO_P = 32, 8, 16
SUBLANES = 8
LANES = 128
VMEM_LIMIT_BYTES = 56 * 1024 * 1024

TS_MIX = 512
TS_MIXB = 256
TS_FFN = 256
TS_POOL = 512
TS_MM = 512
R_CHUNK = 64

MESH = pl.DeviceIdType.MESH
ANY = pl.BlockSpec(memory_space=pl.ANY)
NT_DIMS = (((1,), (1,)), ((), ()))
TN_DIMS = (((0,), (0,)), ((), ()))


def _cparams(n_axes):
    return pltpu.CompilerParams(dimension_semantics=("arbitrary",) * n_axes, vmem_limit_bytes=VMEM_LIMIT_BYTES)


def _const(shape):
    nd = len(shape)
    return pl.BlockSpec(shape, lambda *_: (0,) * nd, pipeline_mode=pl.Buffered(1))


def _sigmoid(v):
    return 1.0 / (1.0 + jnp.exp(-v))


def _rsqrt_mean_sq(x):
    return lax.rsqrt(jnp.mean(x * x, axis=-1, keepdims=True) + RMS_EPS)


def _rms_bwd(dh, xh, r, g):
    dxh = dh * g
    return r * (dxh - xh * jnp.mean(dxh * xh, axis=-1, keepdims=True))


def _shifted(buf_ref, row0, rows, col0, width, offsets):
    lo = (min(offsets) // SUBLANES) * SUBLANES
    hi = -(-(max(offsets) + rows) // SUBLANES) * SUBLANES
    start = row0 + lo if isinstance(row0, int) else pl.multiple_of(row0 + lo, SUBLANES)
    win = buf_ref[pl.ds(start, hi - lo), col0:col0 + width]
    out = {}
    for res in sorted({(o - lo) % SUBLANES for o in offsets}):
        qs = {o: (o - lo) // SUBLANES for o in offsets if (o - lo) % SUBLANES == res}
        base = win[res:res + rows + SUBLANES * max(qs.values()), :]
        for o, q in qs.items():
            out[o] = base[SUBLANES * q:SUBLANES * q + rows, :]
    return out


def _rowsum8(v):
    acc = v[0:SUBLANES, :]
    for r in range(SUBLANES, v.shape[0], SUBLANES):
        acc = acc + v[r:r + SUBLANES, :]
    return acc


def _taps(w_ref, sh, offsets, col0, width):
    acc = None
    for k, o in enumerate(offsets):
        term = w_ref[k:k + 1, col0:col0 + width] * sh[o]
        acc = term if acc is None else acc + term
    return acc


def _finish_tap_sums(acc_ref, out_ref, n_taps):
    for k in range(n_taps):
        out_ref[k:k + 1, :] = jnp.sum(acc_ref[SUBLANES * k:SUBLANES * (k + 1), :], axis=0, keepdims=True)


class _Comm:
    def __init__(self, inputs, out_shapes, sems, start, finish, aliases=None):
        self.inputs, self.out_shapes, self.sems = list(inputs), list(out_shapes), list(sems)
        self.start, self.finish, self.aliases = start, finish, dict(aliases or {})


def _call(body, *, name, grid, in_specs, out_specs, out_shape, args, scratch=(), comm=None, aliases=None):
    n_in, n_out, n_scr, n_axes = len(in_specs), len(out_specs), len(scratch), len(grid)
    params = pltpu.CompilerParams(dimension_semantics=("arbitrary",) * n_axes, vmem_limit_bytes=VMEM_LIMIT_BYTES)
    aliases = dict(aliases or {})
    if comm is None:
        outs = pl.pallas_call(body, name=name, grid=grid, in_specs=list(in_specs), out_specs=list(out_specs),
                              out_shape=list(out_shape), scratch_shapes=list(scratch), input_output_aliases=aliases,
                              compiler_params=params)(*args)
        return list(outs), []
    ci, co = len(comm.inputs), len(comm.out_shapes)

    def wrapped(*refs):
        k_in, c_in = refs[:n_in], refs[n_in:n_in + ci]
        o0 = n_in + ci
        k_out, c_out = refs[o0:o0 + n_out], refs[o0 + n_out:o0 + n_out + co]
        s0 = o0 + n_out + co
        k_scr, c_sem = refs[s0:s0 + n_scr], refs[s0 + n_scr:]
        first = pl.program_id(0) == 0
        last = pl.program_id(0) == grid[0] - 1
        for ax in range(1, n_axes):
            first = jnp.logical_and(first, pl.program_id(ax) == 0)
            last = jnp.logical_and(last, pl.program_id(ax) == grid[ax] - 1)

        @pl.when(first)
        def _():
            comm.start(c_in, c_out, c_sem)

        body(*k_in, *k_out, *k_scr)

        @pl.when(last)
        def _():
            comm.finish(c_in, c_out, c_sem)

    outs = pl.pallas_call(
        wrapped, name=name, grid=grid, in_specs=list(in_specs) + [ANY] * ci, out_specs=list(out_specs) + [ANY] * co,
        out_shape=list(out_shape) + comm.out_shapes, scratch_shapes=list(scratch) + comm.sems,
        input_output_aliases={**aliases, **{n_in + i: n_out + o for i, o in comm.aliases.items()}}, compiler_params=params,
    )(*args, *comm.inputs)
    return list(outs[:n_out]), list(outs[n_out:])


def _run_comm(comm, name):
    ci, co = len(comm.inputs), len(comm.out_shapes)

    def body(*refs):
        c_in, c_out, c_sem = refs[:ci], refs[ci:ci + co], refs[ci + co:]
        comm.start(c_in, c_out, c_sem)
        comm.finish(c_in, c_out, c_sem)

    return list(pl.pallas_call(body, name=name, in_specs=[ANY] * ci, out_specs=[ANY] * co, out_shape=comm.out_shapes,
                               scratch_shapes=comm.sems, input_output_aliases=comm.aliases)(*comm.inputs))


def _mix0_fwd(x, g, w_in, conv_a, ln_g, ln_b, conv_b, w_out, comm=None):
    s = x.shape[0]
    ts = min(TS_MIX, s)
    n = s // ts
    bw = NZ // NCHIP
    offs_a = [HALO_A - (K_A - 1) + k for k in range(K_A)]
    offs_s = [HALO_S - (K_S - 1) + k for k in range(K_S)]

    def body(x_ref, g_ref, win_ref, ca_ref, lg_ref, lb_ref, cb_ref, wout_ref,
             h_ref, z_ref, ac_ref, cat_ref, x1_ref, glu_buf, cv_buf, bconv_buf):
        i = pl.program_id(0)

        @pl.when(i == 0)
        def _():
            glu_buf[0:HALO_A, :] = jnp.zeros((HALO_A, A), F32)
            cv_buf[0:HALO_S, :] = jnp.zeros((HALO_S, A), F32)

        xv = x_ref[...]
        h = (xv * _rsqrt_mean_sq(xv) * g_ref[...]).astype(BF16)
        h_ref[...] = h
        for j in range(NCHIP):
            z_ref[:, j * bw:(j + 1) * bw] = jnp.dot(h, win_ref[j], preferred_element_type=F32)
        glu_buf[HALO_A:HALO_A + ts, :] = z_ref[:, 0:A] * _sigmoid(z_ref[:, A:2 * A])
        cv_buf[HALO_S:HALO_S + ts, :] = z_ref[:, 3 * A:4 * A] * z_ref[:, 4 * A:5 * A]

        def chunk(ci, carry):
            r0 = pl.multiple_of(ci * R_CHUNK, R_CHUNK)
            for c0 in range(0, A, LANES):
                sh = _shifted(glu_buf, r0, R_CHUNK, c0, LANES, offs_a)
                ac_ref[pl.ds(r0, R_CHUNK), c0:c0 + LANES] = _taps(ca_ref, sh, offs_a, c0, LANES)
                sh = _shifted(cv_buf, r0, R_CHUNK, c0, LANES, offs_s)
                bconv_buf[pl.ds(r0, R_CHUNK), c0:c0 + LANES] = _taps(cb_ref, sh, offs_s, c0, LANES)
            return carry

        lax.fori_loop(0, ts // R_CHUNK, chunk, 0)
        glu_buf[0:HALO_A, :] = glu_buf[ts:ts + HALO_A, :]
        cv_buf[0:HALO_S, :] = cv_buf[ts:ts + HALO_S, :]

        ac = ac_ref[...]
        xc = ac - jnp.mean(ac, axis=-1, keepdims=True)
        xn = xc * lax.rsqrt(jnp.mean(xc * xc, axis=-1, keepdims=True) + LN_EPS)
        ln = xn * lg_ref[...] + lb_ref[...]
        cat_ref[:, 0:A] = (ln * _sigmoid(ln)).astype(BF16)
        cat_ref[:, A:2 * A] = (z_ref[:, 2 * A:3 * A] * bconv_buf[...]).astype(BF16)
        x1_ref[...] = xv + jnp.dot(cat_ref[...], wout_ref[...], preferred_element_type=F32)

    tile = lambda w: pl.BlockSpec((ts, w), lambda i: (i, 0))
    return _call(
        body, name="mix0_fwd", grid=(n,), comm=comm, args=(x, g, w_in, conv_a, ln_g, ln_b, conv_b, w_out),
        in_specs=[tile(D), _const((1, D)), _const((NCHIP, D, bw)), _const((K_A, A)), _const((1, A)), _const((1, A)),
                  _const((K_S, A)), _const((2 * A, D))],
        out_specs=[tile(D), tile(NZ), tile(A), tile(2 * A), tile(D)],
        out_shape=[jax.ShapeDtypeStruct((s, D), BF16), jax.ShapeDtypeStruct((s, NZ), F32), jax.ShapeDtypeStruct((s, A), F32),
                   jax.ShapeDtypeStruct((s, 2 * A), BF16), jax.ShapeDtypeStruct((s, D), F32)],
        scratch=[pltpu.VMEM((HALO_A + ts, A), F32), pltpu.VMEM((HALO_S + ts, A), F32), pltpu.VMEM((ts, A), F32)])


def _ffn_fwd(x, g, w_up, wc, w_down, name, comm=None):
    s = x.shape[0]
    ts = min(TS_FFN, s)
    n = s // ts
    bw = FF2 // NCHIP
    rows = 32
    offs = [HALO_S - (K_S - 1) + k for k in range(K_S)]

    def body(x_ref, g_ref, wup_ref, wc_ref, wdn_ref, h_ref, u0_ref, u_ref, act_ref, xo_ref, cbuf):
        i = pl.program_id(0)

        @pl.when(i == 0)
        def _():
            cbuf[0:HALO_S, :] = jnp.zeros((HALO_S, FF2), F32)

        xv = x_ref[...]
        h = (xv * _rsqrt_mean_sq(xv) * g_ref[...]).astype(BF16)
        h_ref[...] = h
        f = None
        for p in range(NCHIP // 2):
            for j in (p, NCHIP // 2 + p):
                zc = jnp.dot(h, wup_ref[j], preferred_element_type=F32)
                u0_ref[:, j * bw:(j + 1) * bw] = zc.astype(BF16)
                cbuf[HALO_S:HALO_S + ts, j * bw:(j + 1) * bw] = zc
            for r0 in range(0, ts, rows):
                for c0 in range(p * bw, (p + 1) * bw, LANES):
                    ug = _taps(wc_ref, _shifted(cbuf, r0, rows, c0, LANES, offs), offs, c0, LANES)
                    uv = _taps(wc_ref, _shifted(cbuf, r0, rows, FF + c0, LANES, offs), offs, FF + c0, LANES)
                    u_ref[r0:r0 + rows, c0:c0 + LANES] = ug
                    u_ref[r0:r0 + rows, FF + c0:FF + c0 + LANES] = uv
                    act_ref[r0:r0 + rows, c0:c0 + LANES] = (ug * _sigmoid(ug) * uv).astype(BF16)
            fp = jnp.dot(act_ref[:, p * bw:(p + 1) * bw], wdn_ref[p * bw:(p + 1) * bw, :], preferred_element_type=F32)
            f = fp if f is None else f + fp
        cbuf[0:HALO_S, :] = cbuf[ts:ts + HALO_S, :]
        xo_ref[...] = xv + f

    tile = lambda w: pl.BlockSpec((ts, w), lambda i: (i, 0))
    return _call(
        body, name=name, grid=(n,), comm=comm, args=(x, g, w_up, wc, w_down),
        in_specs=[tile(D), _const((1, D)), _const((NCHIP, D, bw)), _const((K_S, FF2)), _const((FF, D))],
        out_specs=[tile(D), tile(FF2), tile(FF2), tile(FF), tile(D)],
        out_shape=[jax.ShapeDtypeStruct((s, D), BF16), jax.ShapeDtypeStruct((s, FF2), BF16), jax.ShapeDtypeStruct((s, FF2), F32),
                   jax.ShapeDtypeStruct((s, FF), BF16), jax.ShapeDtypeStruct((s, D), F32)],
        scratch=[pltpu.VMEM((HALO_S + ts, FF2), F32)])


def _pool_windows(hbuf, pbuf, tile_row0, ts):
    def chunk(ci, carry):
        r0 = pl.multiple_of(ci * R_CHUNK, R_CHUNK)
        t1 = (tile_row0 + r0 + lax.broadcasted_iota(jnp.int32, (R_CHUNK, 1), 0) + 1).astype(F32)
        for gi, w in enumerate(POOL_WINDOWS):
            cnt = jnp.minimum(t1, float(w))
            offs = [HALO_P - jj for jj in range(w)]
            for c0 in range(gi * PG, (gi + 1) * PG, LANES):
                sh = _shifted(hbuf, r0, R_CHUNK, c0, LANES, offs)
                tot = sh[offs[0]]
                for o in offs[1:]:
                    tot = tot + sh[o]
                pbuf[pl.ds(r0, R_CHUNK), c0:c0 + LANES] = (tot / cnt - sh[HALO_P]).astype(BF16)
        return carry

    lax.fori_loop(0, ts // R_CHUNK, chunk, 0)


def _assemble_wpool(wp_ref, wps):
    rb = PG // NCHIP
    for gi in range(len(POOL_WINDOWS)):
        for j in range(NCHIP):
            wps[gi, j * rb:(j + 1) * rb, :] = wp_ref[j, gi]


def _pool_fwd(x, g, w_pool, scale):
    s = x.shape[0]
    ts = min(TS_POOL, s)
    n = s // ts
    ng = len(POOL_WINDOWS)

    def body(x_ref, g_ref, wp_ref, sc_ref, xo_ref, hbuf, pbuf, wps):
        i = pl.program_id(0)

        @pl.when(i == 0)
        def _():
            hbuf[0:HALO_P, :] = jnp.zeros((HALO_P, D), F32)
            _assemble_wpool(wp_ref, wps)

        xv = x_ref[...]
        hbuf[HALO_P:HALO_P + ts, :] = xv * _rsqrt_mean_sq(xv) * g_ref[...]
        _pool_windows(hbuf, pbuf, i * ts, ts)
        hbuf[0:HALO_P, :] = hbuf[ts:ts + HALO_P, :]
        for gi in range(ng):
            cols = slice(gi * PG, (gi + 1) * PG)
            y = jnp.dot(pbuf[:, cols], wps[gi], preferred_element_type=F32)
            xo_ref[:, cols] = xv[:, cols] + y * sc_ref[:, cols]

    tile = pl.BlockSpec((ts, D), lambda i: (i, 0))
    return pl.pallas_call(
        body, name="pool_fwd", grid=(n,),
        in_specs=[tile, _const((1, D)), _const((NCHIP, ng, PG // NCHIP, PG)), _const((1, D))],
        out_specs=tile, out_shape=jax.ShapeDtypeStruct((s, D), F32),
        scratch_shapes=[pltpu.VMEM((HALO_P + ts, D), F32), pltpu.VMEM((ts, D), BF16), pltpu.VMEM((ng, PG, PG), BF16)],
        compiler_params=_cparams(1),
    )(x, g, w_pool, scale)


def _final_loss(x, g, target):
    s = x.shape[0]
    ts = min(TS_MM, s)
    n = s // ts

    def body(x_ref, g_ref, t_ref, dx_ref, dg_ref, loss_ref):
        i = pl.program_id(0)

        @pl.when(i == 0)
        def _():
            dg_ref[...] = jnp.zeros((1, D), F32)
            loss_ref[...] = jnp.zeros((1, LANES), F32)

        xv = x_ref[...]
        r = _rsqrt_mean_sq(xv)
        xh = xv * r
        gv = g_ref[...]
        err = xh * gv - t_ref[...]
        sq = jnp.sum(jnp.sum(err * err, axis=1, keepdims=True), axis=0, keepdims=True)
        loss_ref[...] += sq * (0.5 / D)
        dy = err * (1.0 / D)
        dg_ref[...] += jnp.sum(dy * xh, axis=0, keepdims=True)
        dx_ref[...] = _rms_bwd(dy, xh, r, gv)

    tile = pl.BlockSpec((ts, D), lambda i: (i, 0))
    return pl.pallas_call(
        body, name="final_loss", grid=(n,),
        in_specs=[tile, _const((1, D)), tile],
        out_specs=[tile, pl.BlockSpec((1, D), lambda i: (0, 0)), pl.BlockSpec((1, LANES), lambda i: (0, 0))],
        out_shape=[jax.ShapeDtypeStruct((s, D), F32), jax.ShapeDtypeStruct((1, D), F32), jax.ShapeDtypeStruct((1, LANES), F32)],
        compiler_params=_cparams(1),
    )(x, g, target)


def _ffn_bwd_a(dxo, u, u0, wc, w_down, name, comm=None):
    s = dxo.shape[0]
    ts = min(TS_FFN, s)
    n = s // ts
    cw = FF2 // NCHIP
    rows = 32
    lw2 = 2 * LANES
    boffs = [K_S - 1 - k for k in range(K_S)]

    def body(dxo_ref, u_ref, u0_ref, wc_ref, wdn_ref, du0_ref, dwc_ref, dubuf, dact, dwacc):
        i = pl.program_id(0)

        @pl.when(i == 0)
        def _():
            dubuf[ts:ts + HALO_S, :] = jnp.zeros((HALO_S, FF2), F32)
            dwacc[...] = jnp.zeros(dwacc.shape, F32)

        df = dxo_ref[...].astype(BF16)
        for cg in range(0, FF, cw):
            dact[...] = lax.dot_general(df, wdn_ref[cg:cg + cw, :], NT_DIMS, preferred_element_type=F32)

            def chunk(ci, carry, cg=cg):
                rs = pl.ds(pl.multiple_of(ci * rows, rows), rows)
                for c in range(cg, cg + cw, LANES):
                    ug, uv = u_ref[rs, c:c + LANES], u_ref[rs, FF + c:FF + c + LANES]
                    sg = _sigmoid(ug)
                    da = dact[rs, c - cg:c - cg + LANES]
                    gs = ug * sg
                    dubuf[rs, c:c + LANES] = da * uv * (sg + gs * (1.0 - sg))
                    dubuf[rs, FF + c:FF + c + LANES] = da * gs
                return carry

            lax.fori_loop(0, ts // rows, chunk, 0)

        def chunk2(ci, carry):
            r0 = pl.multiple_of(ci * rows, rows)
            rs = pl.ds(r0, rows)
            for c in range(0, FF2, lw2):
                sh = _shifted(dubuf, r0, rows, c, lw2, boffs)
                du0_ref[rs, c:c + lw2] = _taps(wc_ref, sh, boffs, c, lw2).astype(BF16)
                u0v = u0_ref[rs, c:c + lw2].astype(F32)
                for k, o in enumerate(boffs):
                    dwacc[SUBLANES * k:SUBLANES * (k + 1), c:c + lw2] += _rowsum8(sh[o] * u0v)
            return carry

        lax.fori_loop(0, ts // rows, chunk2, 0)
        dubuf[ts:ts + HALO_S, :] = dubuf[0:HALO_S, :]

        @pl.when(i == n - 1)
        def _():
            _finish_tap_sums(dwacc, dwc_ref, K_S)

    rev = lambda w: pl.BlockSpec((ts, w), lambda i: (n - 1 - i, 0))
    return _call(
        body, name=name, grid=(n,), comm=comm, args=(dxo, u, u0, wc, w_down),
        in_specs=[rev(D), rev(FF2), rev(FF2), _const((K_S, FF2)), _const((FF, D))],
        out_specs=[rev(FF2), pl.BlockSpec((K_S, FF2), lambda i: (0, 0))],
        out_shape=[jax.ShapeDtypeStruct((s, FF2), BF16), jax.ShapeDtypeStruct((K_S, FF2), F32)],
        scratch=[pltpu.VMEM((ts + HALO_S, FF2), F32), pltpu.VMEM((ts, cw), F32), pltpu.VMEM((SUBLANES * K_S, FF2), F32)])


def _nt_rms_bwd(dy, w, x, g, dres, name, comm=None):
    s = x.shape[0]
    ts = min(TS_MM, s)
    n = s // ts
    nw = dy.shape[1]
    bw = nw // NCHIP

    def body(dy_ref, w_ref, x_ref, g_ref, dres_ref, dx_ref, dg_ref):
        i = pl.program_id(0)

        @pl.when(i == 0)
        def _():
            dg_ref[...] = jnp.zeros((1, D), F32)

        dh = lax.dot_general(dy_ref[:, 0:bw], w_ref[0], NT_DIMS, preferred_element_type=F32)
        for j in range(1, NCHIP):
            dh = dh + lax.dot_general(dy_ref[:, j * bw:(j + 1) * bw], w_ref[j], NT_DIMS, preferred_element_type=F32)
        xv = x_ref[...]
        r = _rsqrt_mean_sq(xv)
        xh = xv * r
        dg_ref[...] += jnp.sum(dh * xh, axis=0, keepdims=True)
        dx_ref[...] = dres_ref[...] + _rms_bwd(dh, xh, r, g_ref[...])

    tile = lambda wd: pl.BlockSpec((ts, wd), lambda i: (i, 0))
    return _call(
        body, name=name, grid=(n,), comm=comm, args=(dy, w, x, g, dres),
        in_specs=[tile(nw), _const((NCHIP, D, bw)), tile(D), _const((1, D)), tile(D)],
        out_specs=[tile(D), pl.BlockSpec((1, D), lambda i: (0, 0))],
        out_shape=[jax.ShapeDtypeStruct((s, D), F32), jax.ShapeDtypeStruct((1, D), F32)])


def _wgrad(a, b, bn, out_shape, out_block, out_index, name, comm=None):
    s, m = a.shape
    nb = b.shape[1] // bn
    tk = min(TS_MM, s)
    nk = s // tk

    def body(a_ref, b_ref, o_ref):
        k = pl.program_id(1)

        @pl.when(k == 0)
        def _():
            o_ref[...] = jnp.zeros(o_ref.shape, F32)

        o_ref[...] += lax.dot_general(a_ref[...], b_ref[...].astype(BF16), TN_DIMS, preferred_element_type=F32)

    (out,), comm_out = _call(
        body, name=name, grid=(nb, nk), comm=comm, args=(a, b),
        in_specs=[pl.BlockSpec((tk, m), lambda jn, k: (k, 0)), pl.BlockSpec((tk, bn), lambda jn, k: (k, jn))],
        out_specs=[pl.BlockSpec(out_block, lambda jn, k: out_index(jn))], out_shape=[jax.ShapeDtypeStruct(out_shape, F32)])
    return out, comm_out


def _pool_bwd(dxo, x, g, w_pool, scale, comm=None):
    s = x.shape[0]
    ts = min(TS_POOL, s)
    n = s // ts
    ng = len(POOL_WINDOWS)
    rb = PG // NCHIP

    def body(dxo_ref, x_ref, halo_ref, g_ref, wp_ref, sc_ref, dx_ref, dwp_ref, dsc_ref, dg_ref,
             hbuf, pbuf, qbuf, dhbuf, wps, dwacc):
        i = pl.program_id(0)
        j = n - 1 - i

        @pl.when(i == 0)
        def _():
            qbuf[ts:ts + HALO_P, :] = jnp.zeros((HALO_P, D), F32)
            dwacc[...] = jnp.zeros(dwacc.shape, F32)
            dsc_ref[...] = jnp.zeros((1, D), F32)
            dg_ref[...] = jnp.zeros((1, D), F32)
            _assemble_wpool(wp_ref, wps)

        gv = g_ref[...]
        xl = halo_ref[...]
        hbuf[0:HALO_P, :] = jnp.where(j == 0, 0.0, xl * _rsqrt_mean_sq(xl) * gv)
        xv = x_ref[...]
        r = _rsqrt_mean_sq(xv)
        xh = xv * r
        hbuf[HALO_P:HALO_P + ts, :] = xh * gv
        _pool_windows(hbuf, pbuf, j * ts, ts)

        dy = dxo_ref[...]
        t1 = (j * ts + lax.broadcasted_iota(jnp.int32, (ts, 1), 0) + 1).astype(F32)
        for gi, w in enumerate(POOL_WINDOWS):
            cols = slice(gi * PG, (gi + 1) * PG)
            p = pbuf[:, cols]
            y = jnp.dot(p, wps[gi], preferred_element_type=F32)
            dsc_ref[:, cols] += jnp.sum(dy[:, cols] * y, axis=0, keepdims=True)
            dq = (dy[:, cols] * sc_ref[:, cols]).astype(BF16)
            dwacc[gi] += lax.dot_general(p, dq, TN_DIMS, preferred_element_type=F32)
            dp = lax.dot_general(dq, wps[gi], NT_DIMS, preferred_element_type=F32)
            qbuf[0:ts, cols] = dp / jnp.minimum(t1, float(w))

        def chunk(ci, carry):
            r0 = pl.multiple_of(ci * R_CHUNK, R_CHUNK)
            tc = (j * ts + r0 + lax.broadcasted_iota(jnp.int32, (R_CHUNK, 1), 0) + 1).astype(F32)
            for gi, w in enumerate(POOL_WINDOWS):
                cnt = jnp.minimum(tc, float(w))
                offs = list(range(w))
                for c0 in range(gi * PG, (gi + 1) * PG, LANES):
                    sh = _shifted(qbuf, r0, R_CHUNK, c0, LANES, offs)
                    tot = sh[0]
                    for o in offs[1:]:
                        tot = tot + sh[o]
                    dhbuf[pl.ds(r0, R_CHUNK), c0:c0 + LANES] = tot - sh[0] * cnt
            return carry

        lax.fori_loop(0, ts // R_CHUNK, chunk, 0)
        qbuf[ts:ts + HALO_P, :] = qbuf[0:HALO_P, :]
        dh = dhbuf[...]
        dg_ref[...] += jnp.sum(dh * xh, axis=0, keepdims=True)
        dx_ref[...] = dy + _rms_bwd(dh, xh, r, gv)

        @pl.when(i == n - 1)
        def _():
            for gi in range(ng):
                for jj in range(NCHIP):
                    dwp_ref[jj, gi] = dwacc[gi, jj * rb:(jj + 1) * rb, :]

    rev = pl.BlockSpec((ts, D), lambda i: (n - 1 - i, 0))
    halo = pl.BlockSpec((HALO_P, D), lambda i: (jnp.maximum((n - 1 - i) * (ts // HALO_P) - 1, 0), 0))
    vec = pl.BlockSpec((1, D), lambda i: (0, 0))
    return _call(
        body, name="pool_bwd", grid=(n,), comm=comm, args=(dxo, x, x, g, w_pool, scale),
        in_specs=[rev, rev, halo, _const((1, D)), _const((NCHIP, ng, rb, PG)), _const((1, D))],
        out_specs=[rev, pl.BlockSpec((NCHIP, ng, rb, PG), lambda i: (0, 0, 0, 0)), vec, vec],
        out_shape=[jax.ShapeDtypeStruct((s, D), F32), jax.ShapeDtypeStruct((NCHIP, ng, rb, PG), F32),
                   jax.ShapeDtypeStruct((1, D), F32), jax.ShapeDtypeStruct((1, D), F32)],
        scratch=[pltpu.VMEM((HALO_P + ts, D), F32), pltpu.VMEM((ts, D), BF16), pltpu.VMEM((ts + HALO_P, D), F32),
                 pltpu.VMEM((ts, D), F32), pltpu.VMEM((ng, PG, PG), BF16), pltpu.VMEM((ng, PG, PG), F32)])


def _mix0_bwd_a(dx1, z, ac, w_out, conv_a, ln_g, ln_b, conv_b, comm=None):
    s = dx1.shape[0]
    ts = min(TS_MIXB, s)
    n = s // ts
    rows = 32
    offs_a = [HALO_A - (K_A - 1) + k for k in range(K_A)]
    offs_s = [HALO_S - (K_S - 1) + k for k in range(K_S)]
    boffs_a = [K_A - 1 - k for k in range(K_A)]
    boffs_s = [K_S - 1 - k for k in range(K_S)]

    def body(dx1_ref, z_ref, zh_ref, ac_ref, wout_ref, ca_ref, lg_ref, lb_ref, cb_ref,
             dz_ref, dca_ref, dlg_ref, dlb_ref, dcb_ref,
             glu_buf, cv_buf, dac_buf, dbc_buf, db_buf, dca_acc, dcb_acc):
        i = pl.program_id(0)
        j = n - 1 - i

        @pl.when(i == 0)
        def _():
            dac_buf[ts:ts + HALO_A, :] = jnp.zeros((HALO_A, A), F32)
            dbc_buf[ts:ts + HALO_S, :] = jnp.zeros((HALO_S, A), F32)
            dca_acc[...] = jnp.zeros(dca_acc.shape, F32)
            dcb_acc[...] = jnp.zeros(dcb_acc.shape, F32)
            dlg_ref[...] = jnp.zeros((1, A), F32)
            dlb_ref[...] = jnp.zeros((1, A), F32)

        glu_h = zh_ref[:, 0:A] * _sigmoid(zh_ref[:, A:2 * A])
        glu_buf[0:HALO_A, :] = jnp.where(j == 0, 0.0, glu_h)
        hs = slice(HALO_A - HALO_S, HALO_A)
        cv_buf[0:HALO_S, :] = jnp.where(j == 0, 0.0, zh_ref[hs, 3 * A:4 * A] * zh_ref[hs, 4 * A:5 * A])
        glu_buf[HALO_A:HALO_A + ts, :] = z_ref[:, 0:A] * _sigmoid(z_ref[:, A:2 * A])
        cv_buf[HALO_S:HALO_S + ts, :] = z_ref[:, 3 * A:4 * A] * z_ref[:, 4 * A:5 * A]

        dcat = lax.dot_general(dx1_ref[...].astype(BF16), wout_ref[...], NT_DIMS, preferred_element_type=F32)
        db_buf[...] = dcat[:, A:2 * A]
        ac = ac_ref[...]
        xc = ac - jnp.mean(ac, axis=-1, keepdims=True)
        rstd = lax.rsqrt(jnp.mean(xc * xc, axis=-1, keepdims=True) + LN_EPS)
        xn = xc * rstd
        lg = lg_ref[...]
        ln = xn * lg + lb_ref[...]
        sl = _sigmoid(ln)
        dln = dcat[:, 0:A] * sl * (1.0 + ln * (1.0 - sl))
        dlg_ref[...] += jnp.sum(dln * xn, axis=0, keepdims=True)
        dlb_ref[...] += jnp.sum(dln, axis=0, keepdims=True)
        dxn = dln * lg
        dac_buf[0:ts, :] = rstd * (dxn - jnp.mean(dxn, axis=-1, keepdims=True)
                                   - xn * jnp.mean(dxn * xn, axis=-1, keepdims=True))

        def chunk(ci, carry):
            r0 = pl.multiple_of(ci * rows, rows)
            rs = pl.ds(r0, rows)
            for c0 in range(0, A, LANES):
                cs = slice(c0, c0 + LANES)
                shc = _shifted(cv_buf, r0, rows, c0, LANES, offs_s)
                bconv = _taps(cb_ref, shc, offs_s, c0, LANES)
                dbv = db_buf[rs, cs]
                dz_ref[rs, 2 * A + c0:2 * A + c0 + LANES] = (dbv * bconv).astype(BF16)
                dbconv = dbv * z_ref[rs, 2 * A + c0:2 * A + c0 + LANES]
                dbc_buf[rs, cs] = dbconv
                for k, o in enumerate(offs_s):
                    dcb_acc[SUBLANES * k:SUBLANES * (k + 1), cs] += _rowsum8(dbconv * shc[o])
                shg = _shifted(glu_buf, r0, rows, c0, LANES, offs_a)
                dacv = dac_buf[rs, cs]
                for k, o in enumerate(offs_a):
                    dca_acc[SUBLANES * k:SUBLANES * (k + 1), cs] += _rowsum8(dacv * shg[o])
            return carry

        lax.fori_loop(0, ts // rows, chunk, 0)

        def chunk2(ci, carry):
            r0 = pl.multiple_of(ci * rows, rows)
            rs = pl.ds(r0, rows)
            for c0 in range(0, A, LANES):
                col = lambda grp: slice(grp * A + c0, grp * A + c0 + LANES)
                dglu = _taps(ca_ref, _shifted(dac_buf, r0, rows, c0, LANES, boffs_a), boffs_a, c0, LANES)
                sg = _sigmoid(z_ref[rs, col(1)])
                dz_ref[rs, col(0)] = (dglu * sg).astype(BF16)
                dz_ref[rs, col(1)] = (dglu * z_ref[rs, col(0)] * sg * (1.0 - sg)).astype(BF16)
                dcv = _taps(cb_ref, _shifted(dbc_buf, r0, rows, c0, LANES, boffs_s), boffs_s, c0, LANES)
                dz_ref[rs, col(3)] = (dcv * z_ref[rs, col(4)]).astype(BF16)
                dz_ref[rs, col(4)] = (dcv * z_ref[rs, col(3)]).astype(BF16)
            return carry

        lax.fori_loop(0, ts // rows, chunk2, 0)
        dac_buf[ts:ts + HALO_A, :] = dac_buf[0:HALO_A, :]
        dbc_buf[ts:ts + HALO_S, :] = dbc_buf[0:HALO_S, :]

        @pl.when(i == n - 1)
        def _():
            _finish_tap_sums(dca_acc, dca_ref, K_A)
            _finish_tap_sums(dcb_acc, dcb_ref, K_S)

    rev = lambda w: pl.BlockSpec((ts, w), lambda i: (n - 1 - i, 0))
    halo = pl.BlockSpec((HALO_A, NZ), lambda i: (jnp.maximum((n - 1 - i) * (ts // HALO_A) - 1, 0), 0))
    full = lambda r, c: pl.BlockSpec((r, c), lambda i: (0, 0))
    return _call(
        body, name="mix0_bwd_a", grid=(n,), comm=comm, args=(dx1, z, z, ac, w_out, conv_a, ln_g, ln_b, conv_b),
        in_specs=[rev(D), rev(NZ), halo, rev(A), _const((2 * A, D)), _const((K_A, A)), _const((1, A)), _const((1, A)),
                  _const((K_S, A))],
        out_specs=[rev(NZ), full(K_A, A), full(1, A), full(1, A), full(K_S, A)],
        out_shape=[jax.ShapeDtypeStruct((s, NZ), BF16), jax.ShapeDtypeStruct((K_A, A), F32), jax.ShapeDtypeStruct((1, A), F32),
                   jax.ShapeDtypeStruct((1, A), F32), jax.ShapeDtypeStruct((K_S, A), F32)],
        scratch=[pltpu.VMEM((HALO_A + ts, A), F32), pltpu.VMEM((HALO_S + ts, A), F32), pltpu.VMEM((ts + HALO_A, A), F32),
                 pltpu.VMEM((ts + HALO_S, A), F32), pltpu.VMEM((ts, A), F32), pltpu.VMEM((SUBLANES * K_A, A), F32),
                 pltpu.VMEM((SUBLANES * K_S, A), F32)])


def _train_step(x, target, place, shard, rep, small_pack, small_shapes):
    bw_up, bw_in = FF2 // NCHIP, NZ // NCHIP
    five = lambda a, k: a.reshape(NCHIP, 2, *HALF[k])

    g_in, g_out, small_g = _run_comm(_gather_comm([shard["w_in"], shard["w_out"]], small_pack), "ag_first")
    whole = {}
    for k, part in zip(SMALL_SHARDED, _unpack(small_g, small_shapes, lead=(NCHIP,))):
        whole[k] = jnp.moveaxis(part, 0, 1).reshape(part.shape[1], NCHIP * part.shape[2])
    w_in, w_out = g_in.reshape(NCHIP, D, bw_in), g_out.reshape(2 * A, D)
    conv_ffn = whole["conv_ffn_w"].reshape(2, K_S, FF2)
    nffn = [rep["norm_ffn"][0:1], rep["norm_ffn"][1:2]]

    (h0, z, ac, cat, x1), (g_up0, g_dn0) = _mix0_fwd(
        x, rep["norm_mix_even"], w_in, whole["conv_a"], rep["ln_a_g"], rep["ln_a_b"], whole["conv_b"], w_out,
        comm=_gather_comm([shard["w_up0"], shard["w_down0"]]))
    w_up0, w_dn0 = g_up0.reshape(NCHIP, D, bw_up), g_dn0.reshape(FF, D)
    (hf0, u00, u0, act0, x2), (g_pool, g_up1, g_dn1) = _ffn_fwd(
        x1, nffn[0], w_up0, conv_ffn[0], w_dn0, "ffn0_fwd",
        comm=_gather_comm([shard["w_pool"], shard["w_up1"], shard["w_down1"]]))
    w_pool = g_pool.reshape(NCHIP, len(POOL_WINDOWS), PG // NCHIP, PG)
    w_up1, w_dn1 = g_up1.reshape(NCHIP, D, bw_up), g_dn1.reshape(FF, D)
    x3 = _pool_fwd(x2, whole["norm_mix_odd"], w_pool, whole["pool_scale"])
    (hf1, u01, u1, act1, x4), _ = _ffn_fwd(x3, nffn[1], w_up1, conv_ffn[1], w_dn1, "ffn1_fwd")
    dx4, g_nfin, loss_part = _final_loss(x4, rep["norm_final"], target)

    def wgrads_ffn(hf, du0, act, dxo, tag):
        g_up, _ = _wgrad(hf, du0, bw_up, (NCHIP, D, bw_up), (None, D, bw_up), lambda jn: (jn, 0, 0), "wgrad_up" + tag)
        g_dn, _ = _wgrad(act, dxo, D // 2, (FF, D), (FF, D // 2), lambda jn: (0, jn), "wgrad_down" + tag)
        return five(g_up, "w_up"), five(g_dn, "w_down")

    def pair_sums(keys, grads, landed, tag):
        return [_pair_sum(place, g, ld, "pair_sum_" + k + tag) for k, g, ld in zip(keys, grads, landed)]

    (du01, g_wc1), _ = _ffn_bwd_a(dx4, u1, u01, conv_ffn[1], w_dn1, "ffn1_bwd_a")
    grads1 = wgrads_ffn(hf1, du01, act1, dx4, "1")
    (dx3, g_nf1), landed1 = _nt_rms_bwd(du01, w_up1, x3, nffn[1], dx4, "ffn1_bwd_b", comm=_pair_comm(grads1))
    sums1 = pair_sums(("w_up", "w_down"), grads1, landed1, "1")
    (dx2, g_wpool, g_scale, g_nmo), _ = _pool_bwd(dx3, x2, whole["norm_mix_odd"], w_pool, whole["pool_scale"])
    (du00, g_wc0), parts1 = _ffn_bwd_a(dx2, u0, u00, conv_ffn[0], w_dn0, "ffn0_bwd_a", comm=_chips_comm(sums1))
    grads0 = wgrads_ffn(hf0, du00, act0, dx2, "0") + (five(g_wpool, "w_pool"),)
    (dx1, g_nf0), landed0 = _nt_rms_bwd(du00, w_up0, x1, nffn[0], dx2, "ffn0_bwd_b", comm=_pair_comm(grads0))
    sums0 = pair_sums(("w_up", "w_down", "w_pool"), grads0, landed0, "0")
    (dz, g_ca, g_lg, g_lb, g_cb), parts0 = _mix0_bwd_a(
        dx1, z, ac, w_out, whole["conv_a"], rep["ln_a_g"], rep["ln_a_b"], whole["conv_b"], comm=_chips_comm(sums0))
    gr_out, _ = _wgrad(cat, dx1, D, (D, D), (D, D), lambda jn: (0, 0), "wgrad_out")
    gr_in, _ = _wgrad(h0, dz, bw_in, (NCHIP, D, bw_in), (None, D, bw_in), lambda jn: (jn, 0, 0), "wgrad_in")
    gradsm = (five(gr_in, "w_in"), five(gr_out, "w_out"))
    (grad_x, g_nme), landedm = _nt_rms_bwd(dz, w_in, x, rep["norm_mix_even"], dx1, "mix0_bwd_b", comm=_pair_comm(gradsm))
    sumsm = pair_sums(("w_in", "w_out"), gradsm, landedm, "")

    small = {"norm_mix_even": g_nme, "conv_a": g_ca, "ln_a_g": g_lg, "ln_a_b": g_lb, "conv_b": g_cb, "norm_mix_odd": g_nmo,
             "pool_scale": g_scale, "norm_ffn": jnp.concatenate([g_nf0, g_nf1], axis=0),
             "conv_ffn_w": jnp.stack([g_wc0, g_wc1]), "norm_final": g_nfin}
    *partsm, small_all = _run_comm(_chips_comm(sumsm, _pack([small[k] for k in SMALL_ALL] + [loss_part])), "rs_last")

    tot = lambda k, g, ld, p, layer=0, nl=1, prev=None: _chip_sum(place, g, ld, p, layer, nl, "chip_sum_%s%d" % (k, layer), prev)
    t_in = tot("w_in", gradsm[0], landedm[0], partsm[0])
    t_out = tot("w_out", gradsm[1], landedm[1], partsm[1])
    t_pool = tot("w_pool", grads0[2], landed0[2], parts0[2])
    t_up = tot("w_up", grads1[0], landed1[0], parts1[0], 1, 2)
    t_up = tot("w_up", grads0[0], landed0[0], parts0[0], 0, 2, t_up)
    t_dn = tot("w_down", grads1[1], landed1[1], parts1[1], 1, 2)
    t_dn = tot("w_down", grads0[1], landed0[1], parts0[1], 0, 2, t_dn)
    swapped = _run_comm(_swap_comm([t_in, t_out, t_pool, t_up, t_dn]), "rs_swap")
    return grad_x, dict(zip(BIG, swapped)), _sum_devices(small_all), [small[k].shape for k in SMALL_ALL] + [(1, LANES)]


def _adamw_math(w, g, m, v):
    m = ADAM_B1 * m + (1.0 - ADAM_B1) * g
    v = ADAM_B2 * v + (1.0 - ADAM_B2) * (g * g)
    m_hat = m / (1.0 - ADAM_B1 ** ADAM_STEP)
    v_hat = v / (1.0 - ADAM_B2 ** ADAM_STEP)
    return -ADAM_LR * (m_hat / (jnp.sqrt(v_hat) + ADAM_EPS) + ADAM_WD * w), m, v


def _adamw_big(w, g, m, v, tr, name):
    nl, rows, cols = w.shape

    def body(w_ref, g_ref, m_ref, v_ref, d_ref, m2_ref, v2_ref):
        d_ref[...], m2_ref[...], v2_ref[...] = _adamw_math(w_ref[...], g_ref[...], m_ref[...], v_ref[...])

    spec = pl.BlockSpec((None, tr, cols), lambda l, r: (l, r, 0))
    return pl.pallas_call(
        body, name=name, grid=(nl, rows // tr), in_specs=[spec] * 4, out_specs=[spec] * 3,
        out_shape=[jax.ShapeDtypeStruct(w.shape, F32)] * 3, compiler_params=_cparams(2),
    )(w, g, m, v)


def _adamw_small(ws, gs, ms, vs):
    n = len(ws)

    def body(*refs):
        for p in range(n):
            w_ref, g_ref, m_ref, v_ref = (refs[q * n + p] for q in range(4))
            d_ref, m2_ref, v2_ref = (refs[(4 + q) * n + p] for q in range(3))
            d_ref[...], m2_ref[...], v2_ref[...] = _adamw_math(w_ref[...], g_ref[...], m_ref[...], v_ref[...])

    whole = lambda a: pl.BlockSpec(a.shape, lambda: (0,) * a.ndim)
    outs = pl.pallas_call(
        body, name="adamw_small", in_specs=[whole(a) for a in ws] * 4, out_specs=[whole(a) for a in ws] * 3,
        out_shape=[jax.ShapeDtypeStruct(a.shape, F32) for a in ws] * 3,
        compiler_params=pltpu.CompilerParams(vmem_limit_bytes=VMEM_LIMIT_BYTES),
    )(*ws, *gs, *ms, *vs)
    return outs[0:n], outs[n:2 * n], outs[2 * n:3 * n]


def _place():
    x, y, c = lax.axis_index("x"), lax.axis_index("y"), lax.axis_index("c")
    chips = [(x, 1 - y), (1 - x, y), (1 - x, 1 - y)]
    blocks = [2 * cx + cy for cx, cy in chips]
    return x, y, c, 2 * x + y, chips, blocks


def _gather_comm(shards, small=None):
    na = len(shards)
    nk = NCHIP - 1
    ns = 0 if small is None else 1

    def copies(ins, outs, sems):
        ici_send, ici_recv, fwd_send, fwd_recv, own_send, own_recv = sems[:6]
        x, y, c, j, chips, blocks = _place()
        sib = (x, y, 1 - c)

        def ici(a, k, arrival):
            dst = outs[a].at[blocks[k], c] if arrival else outs[a].at[j, c]
            return pltpu.make_async_remote_copy(
                src_ref=dst if arrival else ins[a].at[c], dst_ref=dst, send_sem=ici_send.at[a * nk + k],
                recv_sem=ici_recv.at[a * nk + k], device_id=(*chips[k], c), device_id_type=MESH)

        def fwd(a, k, half):
            ref = outs[a].at[blocks[k], half]
            return pltpu.make_async_remote_copy(
                src_ref=ref, dst_ref=ref, send_sem=fwd_send.at[a * nk + k], recv_sem=fwd_recv.at[a * nk + k],
                device_id=sib, device_id_type=MESH)

        own = [pltpu.make_async_remote_copy(src_ref=ins[a], dst_ref=outs[a].at[j], send_sem=own_send.at[a],
                                            recv_sem=own_recv.at[a], device_id=sib, device_id_type=MESH) for a in range(na)]
        small_copies = [pltpu.make_async_remote_copy(
            src_ref=ins[na], dst_ref=outs[na].at[j], send_sem=ici_send.at[na * nk + k], recv_sem=ici_recv.at[na * nk + k],
            device_id=(*chips[k], c), device_id_type=MESH) for k in range(nk * ns)]
        local = [pltpu.make_async_copy(ins[na], outs[na].at[j], sems[6])] if ns else []
        return ici, fwd, own, small_copies, local, c

    def start(ins, outs, sems):
        ici, _, own, small_copies, local, _ = copies(ins, outs, sems)
        for cp in local + own + [ici(a, k, False) for a in range(na) for k in range(nk)] + small_copies:
            cp.start()

    def finish(ins, outs, sems):
        ici, fwd, own, small_copies, local, c = copies(ins, outs, sems)
        for a in range(na):
            for k in range(nk):
                ici(a, k, True).wait_recv()
                fwd(a, k, c).start()
        for cp in small_copies:
            cp.wait()
        for a in range(na):
            for k in range(nk):
                ici(a, k, False).wait_send()
                fwd(a, k, c).wait_send()
                fwd(a, k, 1 - c).wait_recv()
        for cp in own + local:
            cp.wait()

    out_shapes = [jax.ShapeDtypeStruct((NCHIP,) + s.shape, s.dtype) for s in shards]
    sems = [pltpu.SemaphoreType.DMA((na * nk + nk * ns,)), pltpu.SemaphoreType.DMA((na * nk + nk * ns,)),
            pltpu.SemaphoreType.DMA((na * nk,)), pltpu.SemaphoreType.DMA((na * nk,)),
            pltpu.SemaphoreType.DMA((na,)), pltpu.SemaphoreType.DMA((na,))]
    if ns:
        out_shapes.append(jax.ShapeDtypeStruct((NCHIP,) + small.shape, small.dtype))
        sems.append(pltpu.SemaphoreType.DMA)
    return _Comm(list(shards) + [small] * ns, out_shapes, sems, start, finish)


def _simple_comm(inputs, out_shapes, make_copies, n_sems, aliases=None):
    def start(ins, outs, sems):
        for cp in make_copies(ins, outs, sems):
            cp.start()

    def finish(ins, outs, sems):
        for cp in make_copies(ins, outs, sems):
            cp.wait()

    return _Comm(inputs, out_shapes, [pltpu.SemaphoreType.DMA((n,)) for n in n_sems], start, finish, aliases)


def _pair_comm(grads):
    def make_copies(ins, outs, sems):
        x, y, c, _, _, _ = _place()
        return [pltpu.make_async_remote_copy(
            src_ref=ins[a].at[:, 1 - c], dst_ref=outs[a], send_sem=sems[0].at[a], recv_sem=sems[1].at[a],
            device_id=(x, y, 1 - c), device_id_type=MESH) for a in range(len(grads))]

    out_shapes = [jax.ShapeDtypeStruct(g.shape[:1] + g.shape[2:], F32) for g in grads]
    return _simple_comm(grads, out_shapes, make_copies, [len(grads)] * 2)


def _chips_comm(sums, small=None):
    na = len(sums)
    nk = NCHIP - 1

    def make_copies(ins, outs, sems):
        x, y, c, _, chips, blocks = _place()
        copies = [pltpu.make_async_remote_copy(
            src_ref=ins[a].at[blocks[k]], dst_ref=outs[a].at[k], send_sem=sems[0].at[a * nk + k],
            recv_sem=sems[1].at[a * nk + k], device_id=(*chips[k], c), device_id_type=MESH)
            for a in range(na) for k in range(nk)]
        if small is not None:
            me = 4 * x + 2 * y + c
            for r in range(1, NDEV):
                peer = (1 - x if r & 4 else x, 1 - y if r & 2 else y, 1 - c if r & 1 else c)
                copies.append(pltpu.make_async_remote_copy(
                    src_ref=ins[na], dst_ref=outs[na].at[me], send_sem=sems[2].at[r - 1], recv_sem=sems[3].at[r - 1],
                    device_id=peer, device_id_type=MESH))
            copies.append(pltpu.make_async_copy(ins[na], outs[na].at[me], sems[4].at[0]))
        return copies

    out_shapes = [jax.ShapeDtypeStruct((nk,) + g.shape[1:], BF16) for g in sums]
    if small is None:
        return _simple_comm(sums, out_shapes, make_copies, [na * nk] * 2)
    out_shapes.append(jax.ShapeDtypeStruct((NDEV,) + small.shape, F32))
    return _simple_comm(list(sums) + [small], out_shapes, make_copies, [na * nk] * 2 + [NDEV - 1] * 2 + [1])


def _swap_comm(totals):
    def make_copies(ins, outs, sems):
        x, y, c, _, _, _ = _place()
        return [pltpu.make_async_remote_copy(
            src_ref=outs[a].at[:, c], dst_ref=outs[a].at[:, c], send_sem=sems[0].at[a], recv_sem=sems[1].at[a],
            device_id=(x, y, 1 - c), device_id_type=MESH) for a in range(len(totals))]

    out_shapes = [jax.ShapeDtypeStruct(t.shape, F32) for t in totals]
    return _simple_comm(totals, out_shapes, make_copies, [len(totals)] * 2, aliases={a: a for a in range(len(totals))})


def _pair_sum(place, grad, landed, name):
    _, _, rows, cols = grad.shape

    def body(place_ref, g_ref, l_ref, o_ref):
        o_ref[...] = (g_ref[...] + l_ref[...]).astype(BF16)

    return pl.pallas_call(
        body, name=name,
        grid_spec=pltpu.PrefetchScalarGridSpec(
            num_scalar_prefetch=1, grid=(NCHIP,),
            in_specs=[pl.BlockSpec((None, None, rows, cols), lambda b, p: (b, p[1], 0, 0)),
                      pl.BlockSpec((None, rows, cols), lambda b, p: (b, 0, 0))],
            out_specs=pl.BlockSpec((None, rows, cols), lambda b, p: (b, 0, 0))),
        out_shape=jax.ShapeDtypeStruct((NCHIP, rows, cols), BF16), compiler_params=_cparams(1),
    )(place, grad, landed)


def _chip_sum(place, grad, landed, parts, layer, n_layers, name, prev=None):
    _, _, rows, cols = grad.shape

    def body(*refs):
        g_ref, l_ref, p_ref, o_ref = refs[1], refs[2], refs[3], refs[-1]
        tot = g_ref[...] + l_ref[...]
        for k in range(NCHIP - 1):
            tot = tot + p_ref[k].astype(F32)
        o_ref[...] = tot

    in_specs = [pl.BlockSpec((None, None, rows, cols), lambda i, p: (p[0], p[1], 0, 0)),
                pl.BlockSpec((None, rows, cols), lambda i, p: (p[0], 0, 0)),
                pl.BlockSpec((NCHIP - 1, rows, cols), lambda i, p: (0, 0, 0))]
    args = [place, grad, landed, parts]
    if prev is not None:
        in_specs.append(ANY)
        args.append(prev)
    return pl.pallas_call(
        body, name=name,
        grid_spec=pltpu.PrefetchScalarGridSpec(
            num_scalar_prefetch=1, grid=(1,), in_specs=in_specs,
            out_specs=pl.BlockSpec((None, None, rows, cols), lambda i, p: (layer, p[1], 0, 0))),
        out_shape=jax.ShapeDtypeStruct((n_layers, 2, rows, cols), F32),
        input_output_aliases={} if prev is None else {4: 0}, compiler_params=_cparams(1),
    )(*args)


def _sum_devices(parts):
    def body(p_ref, o_ref):
        tot = p_ref[0]
        for d in range(1, NDEV):
            tot = tot + p_ref[d]
        o_ref[...] = tot

    return pl.pallas_call(
        body, name="sum_small", in_specs=[pl.BlockSpec(parts.shape, lambda: (0, 0, 0))],
        out_specs=pl.BlockSpec(parts.shape[1:], lambda: (0, 0)), out_shape=jax.ShapeDtypeStruct(parts.shape[1:], F32),
    )(parts)


def _pack(parts):
    rows = []
    for p in parts:
        p = p.reshape(-1, LANES)
        rows.append(jnp.pad(p, ((0, -p.shape[0] % SUBLANES), (0, 0))))
    return jnp.concatenate(rows, axis=0)


def _unpack(buf, shapes, lead=()):
    out, r0 = [], 0
    nl = len(lead)
    for shp in shapes:
        nrow = 1
        for d in shp:
            nrow *= d
        nrow //= LANES
        out.append(buf[(slice(None),) * nl + (slice(r0, r0 + nrow),)].reshape(lead + tuple(shp)))
        r0 += nrow + (-nrow % SUBLANES)
    return out


WEIGHT_ORDER = ("norm_mix_even", "w_in", "conv_a", "ln_a_g", "ln_a_b", "conv_b", "w_out", "norm_mix_odd", "w_pool",
                "pool_scale", "norm_ffn", "w_up", "conv_ffn_w", "w_down", "norm_final")
BIG = ("w_in", "w_out", "w_pool", "w_up", "w_down")
HALF = {"w_in": (D // 2, NZ // NCHIP), "w_out": (2 * A // NCHIP // 2, D), "w_pool": (PG // 2, PG),
        "w_up": (D // 2, FF2 // NCHIP), "w_down": (FF // NCHIP // 2, D)}
SMALL_SHARDED = ("conv_a", "conv_b", "conv_ffn_w", "norm_mix_odd", "pool_scale")
SMALL_ALL = ("norm_mix_even", "conv_a", "ln_a_g", "ln_a_b", "conv_b", "norm_mix_odd", "pool_scale", "norm_ffn", "conv_ffn_w",
             "norm_final")


def kernel(x, norm_mix_even, w_in, conv_a, ln_a_g, ln_a_b, conv_b, w_out, norm_mix_odd, w_pool, pool_scale, norm_ffn, w_up, conv_ffn_w, w_down, norm_final, loss_target, m_norm_mix_even, m_w_in, m_conv_a, m_ln_a_g, m_ln_a_b, m_conv_b, m_w_out, m_norm_mix_odd, m_w_pool, m_pool_scale, m_norm_ffn, m_w_up, m_conv_ffn_w, m_w_down, m_norm_final, v_norm_mix_even, v_w_in, v_conv_a, v_ln_a_g, v_ln_a_b, v_conv_b, v_w_out, v_norm_mix_odd, v_w_pool, v_pool_scale, v_norm_ffn, v_w_up, v_conv_ffn_w, v_w_down, v_norm_final):
    w = dict(norm_mix_even=norm_mix_even, w_in=w_in, conv_a=conv_a, ln_a_g=ln_a_g, ln_a_b=ln_a_b, conv_b=conv_b, w_out=w_out,
             norm_mix_odd=norm_mix_odd, w_pool=w_pool, pool_scale=pool_scale, norm_ffn=norm_ffn, w_up=w_up,
             conv_ffn_w=conv_ffn_w, w_down=w_down, norm_final=norm_final)
    m = dict(norm_mix_even=m_norm_mix_even, w_in=m_w_in, conv_a=m_conv_a, ln_a_g=m_ln_a_g, ln_a_b=m_ln_a_b, conv_b=m_conv_b,
             w_out=m_w_out, norm_mix_odd=m_norm_mix_odd, w_pool=m_w_pool, pool_scale=m_pool_scale, norm_ffn=m_norm_ffn,
             w_up=m_w_up, conv_ffn_w=m_conv_ffn_w, w_down=m_w_down, norm_final=m_norm_final)
    v = dict(norm_mix_even=v_norm_mix_even, w_in=v_w_in, conv_a=v_conv_a, ln_a_g=v_ln_a_g, ln_a_b=v_ln_a_b, conv_b=v_conv_b,
             w_out=v_w_out, norm_mix_odd=v_norm_mix_odd, w_pool=v_w_pool, pool_scale=v_pool_scale, norm_ffn=v_norm_ffn,
             w_up=v_w_up, conv_ffn_w=v_conv_ffn_w, w_down=v_w_down, norm_final=v_norm_final)
    chip = 2 * lax.axis_index("x") + lax.axis_index("y")
    place = jnp.stack([chip, lax.axis_index("c")]).astype(jnp.int32)

    half = lambda a, name: a.astype(BF16).reshape((2,) + HALF[name])
    shard = {"w_in": half(w_in[0], "w_in"), "w_out": half(w_out[0], "w_out"), "w_pool": half(w_pool[0], "w_pool"),
             "w_up0": half(w_up[0], "w_up"), "w_up1": half(w_up[1], "w_up"),
             "w_down0": half(w_down[0], "w_down"), "w_down1": half(w_down[1], "w_down")}
    small_shapes = [w[k].shape[-2:] if w[k].ndim == 3 and k != "conv_ffn_w" else (w[k].size // w[k].shape[-1], w[k].shape[-1])
                    for k in SMALL_SHARDED]
    rep = dict(norm_mix_even=norm_mix_even, ln_a_g=ln_a_g, ln_a_b=ln_a_b, norm_ffn=norm_ffn, norm_final=norm_final.reshape(1, D))
    grad_x, swapped, small_packed, small_full_shapes = _train_step(
        x[0], loss_target[0], place, shard, rep, _pack([w[k] for k in SMALL_SHARDED]), small_shapes)
    small_sum = _unpack(small_packed, small_full_shapes)
    loss = small_sum[-1][0, 0]

    grad = {k: swapped[k].reshape(w[k].shape) for k in BIG}
    for k, gsum in zip(SMALL_ALL, small_sum):
        if k in SMALL_SHARDED:
            cols = w[k].shape[-1]
            gsum = lax.dynamic_slice_in_dim(gsum, chip * cols, cols, axis=gsum.ndim - 1)
        grad[k] = gsum.reshape(w[k].shape)

    delta, new_m, new_v = {}, {}, {}
    rows_per_step = {"w_in": 512, "w_out": 256, "w_pool": 256, "w_up": 256, "w_down": 352}
    for k in BIG:
        as3 = lambda a: a.reshape(a.shape[0], -1, a.shape[-1])
        d3, m3, v3 = _adamw_big(as3(w[k]), as3(grad[k]), as3(m[k]), as3(v[k]), rows_per_step[k], "adamw_" + k)
        delta[k], new_m[k], new_v[k] = (a.reshape(w[k].shape) for a in (d3, m3, v3))
    as2 = lambda a: a.reshape(-1, a.shape[-1])
    ds, ms, vs = _adamw_small(*[[as2(t[k]) for k in SMALL_ALL] for t in (w, grad, m, v)])
    for k, d2, m2, v2 in zip(SMALL_ALL, ds, ms, vs):
        delta[k], new_m[k], new_v[k] = (a.reshape(w[k].shape) for a in (d2, m2, v2))

    return (loss, grad_x[None], *[grad[k] for k in WEIGHT_ORDER], *[delta[k] for k in WEIGHT_ORDER],
            *[new_m[k] for k in WEIGHT_ORDER], *[new_v[k] for k in WEIGHT_ORDER])
```

```python
import functools

import jax
import jax.numpy as jnp
from jax import lax
from jax.experimental import pallas as pl
from jax.experimental.pallas import tpu as pltpu

F32, BF16 = jnp.float32, jnp.bfloat16

D = 1024
A = 512
NZ = 5 * A
FF = 2816
FF2 = 2 * FF
NCHIP = 4
NDEV = 8
K_A, K_S = 31, 3
POOL_WINDOWS = (2, 4, 8, 16)
PG = D // len(POOL_WINDOWS)
RMS_EPS, LN_EPS = 1e-6, 1e-5
ADAM_LR, ADAM_B1, ADAM_B2, ADAM_EPS, ADAM_WD, ADAM_STEP = 0.001, 0.9, 0.999, 1e-08, 0.01, 10

HALO_A, HALO_S, HALO_P = 32, 8, 16
SUBLANES = 8
LANES = 128
VMEM_LIMIT_BYTES = 56 * 1024 * 1024

TS_MIX = 512
TS_MIXB = 256
TS_FFN = 256
TS_POOL = 512
TS_MM = 512
R_CHUNK = 64

MESH = pl.DeviceIdType.MESH
ANY = pl.BlockSpec(memory_space=pl.ANY)
NT_DIMS = (((1,), (1,)), ((), ()))
TN_DIMS = (((0,), (0,)), ((), ()))


def _cparams(n_axes):
    return pltpu.CompilerParams(dimension_semantics=("arbitrary",) * n_axes, vmem_limit_bytes=VMEM_LIMIT_BYTES)


def _const(shape):
    nd = len(shape)
    return pl.BlockSpec(shape, lambda *_: (0,) * nd, pipeline_mode=pl.Buffered(1))


def _sigmoid(v):
    return 1.0 / (1.0 + jnp.exp(-v))


def _rsqrt_mean_sq(x):
    return lax.rsqrt(jnp.mean(x * x, axis=-1, keepdims=True) + RMS_EPS)


def _rms_bwd(dh, xh, r, g):
    dxh = dh * g
    return r * (dxh - xh * jnp.mean(dxh * xh, axis=-1, keepdims=True))


def _shifted(buf_ref, row0, rows, col0, width, offsets):
    lo = (min(offsets) // SUBLANES) * SUBLANES
    hi = -(-(max(offsets) + rows) // SUBLANES) * SUBLANES
    start = row0 + lo if isinstance(row0, int) else pl.multiple_of(row0 + lo, SUBLANES)
    win = buf_ref[pl.ds(start, hi - lo), col0:col0 + width]
    out = {}
    for res in sorted({(o - lo) % SUBLANES for o in offsets}):
        qs = {o: (o - lo) // SUBLANES for o in offsets if (o - lo) % SUBLANES == res}
        base = win[res:res + rows + SUBLANES * max(qs.values()), :]
        for o, q in qs.items():
            out[o] = base[SUBLANES * q:SUBLANES * q + rows, :]
    return out


def _rowsum8(v):
    acc = v[0:SUBLANES, :]
    for r in range(SUBLANES, v.shape[0], SUBLANES):
        acc = acc + v[r:r + SUBLANES, :]
    return acc


def _taps(w_ref, sh, offsets, col0, width):
    acc = None
    for k, o in enumerate(offsets):
        term = w_ref[k:k + 1, col0:col0 + width] * sh[o]
        acc = term if acc is None else acc + term
    return acc


def _finish_tap_sums(acc_ref, out_ref, n_taps):
    for k in range(n_taps):
        out_ref[k:k + 1, :] = jnp.sum(acc_ref[SUBLANES * k:SUBLANES * (k + 1), :], axis=0, keepdims=True)


class _Comm:
    def __init__(self, inputs, out_shapes, sems, start, finish, aliases=None):
        self.inputs, self.out_shapes, self.sems = list(inputs), list(out_shapes), list(sems)
        self.start, self.finish, self.aliases = start, finish, dict(aliases or {})


def _call(body, *, name, grid, in_specs, out_specs, out_shape, args, scratch=(), comm=None, aliases=None):
    n_in, n_out, n_scr, n_axes = len(in_specs), len(out_specs), len(scratch), len(grid)
    params = pltpu.CompilerParams(dimension_semantics=("arbitrary",) * n_axes, vmem_limit_bytes=VMEM_LIMIT_BYTES)
    aliases = dict(aliases or {})
    if comm is None:
        outs = pl.pallas_call(body, name=name, grid=grid, in_specs=list(in_specs), out_specs=list(out_specs),
                              out_shape=list(out_shape), scratch_shapes=list(scratch), input_output_aliases=aliases,
                              compiler_params=params)(*args)
        return list(outs), []
    ci, co = len(comm.inputs), len(comm.out_shapes)

    def wrapped(*refs):
        k_in, c_in = refs[:n_in], refs[n_in:n_in + ci]
        o0 = n_in + ci
        k_out, c_out = refs[o0:o0 + n_out], refs[o0 + n_out:o0 + n_out + co]
        s0 = o0 + n_out + co
        k_scr, c_sem = refs[s0:s0 + n_scr], refs[s0 + n_scr:]
        first = pl.program_id(0) == 0
        last = pl.program_id(0) == grid[0] - 1
        for ax in range(1, n_axes):
            first = jnp.logical_and(first, pl.program_id(ax) == 0)
            last = jnp.logical_and(last, pl.program_id(ax) == grid[ax] - 1)

        @pl.when(first)
        def _():
            comm.start(c_in, c_out, c_sem)

        body(*k_in, *k_out, *k_scr)

        @pl.when(last)
        def _():
            comm.finish(c_in, c_out, c_sem)

    outs = pl.pallas_call(
        wrapped, name=name, grid=grid, in_specs=list(in_specs) + [ANY] * ci, out_specs=list(out_specs) + [ANY] * co,
        out_shape=list(out_shape) + comm.out_shapes, scratch_shapes=list(scratch) + comm.sems,
        input_output_aliases={**aliases, **{n_in + i: n_out + o for i, o in comm.aliases.items()}}, compiler_params=params,
    )(*args, *comm.inputs)
    return list(outs[:n_out]), list(outs[n_out:])


def _run_comm(comm, name):
    ci, co = len(comm.inputs), len(comm.out_shapes)

    def body(*refs):
        c_in, c_out, c_sem = refs[:ci], refs[ci:ci + co], refs[ci + co:]
        comm.start(c_in, c_out, c_sem)
        comm.finish(c_in, c_out, c_sem)

    return list(pl.pallas_call(body, name=name, in_specs=[ANY] * ci, out_specs=[ANY] * co, out_shape=comm.out_shapes,
                               scratch_shapes=comm.sems, input_output_aliases=comm.aliases)(*comm.inputs))


def _mix0_fwd(x, g, w_in, conv_a, ln_g, ln_b, conv_b, w_out, comm=None):
    s = x.shape[0]
    ts = min(TS_MIX, s)
    n = s // ts
    bw = NZ // NCHIP
    offs_a = [HALO_A - (K_A - 1) + k for k in range(K_A)]
    offs_s = [HALO_S - (K_S - 1) + k for k in range(K_S)]

    def body(x_ref, g_ref, win_ref, ca_ref, lg_ref, lb_ref, cb_ref, wout_ref,
             h_ref, z_ref, ac_ref, cat_ref, x1_ref, glu_buf, cv_buf, bconv_buf):
        i = pl.program_id(0)

        @pl.when(i == 0)
        def _():
            glu_buf[0:HALO_A, :] = jnp.zeros((HALO_A, A), F32)
            cv_buf[0:HALO_S, :] = jnp.zeros((HALO_S, A), F32)

        xv = x_ref[...]
        h = (xv * _rsqrt_mean_sq(xv) * g_ref[...]).astype(BF16)
        h_ref[...] = h
        for j in range(NCHIP):
            z_ref[:, j * bw:(j + 1) * bw] = jnp.dot(h, win_ref[j], preferred_element_type=F32)
        glu_buf[HALO_A:HALO_A + ts, :] = z_ref[:, 0:A] * _sigmoid(z_ref[:, A:2 * A])
        cv_buf[HALO_S:HALO_S + ts, :] = z_ref[:, 3 * A:4 * A] * z_ref[:, 4 * A:5 * A]

        def chunk(ci, carry):
            r0 = pl.multiple_of(ci * R_CHUNK, R_CHUNK)
            for c0 in range(0, A, LANES):
                sh = _shifted(glu_buf, r0, R_CHUNK, c0, LANES, offs_a)
                ac_ref[pl.ds(r0, R_CHUNK), c0:c0 + LANES] = _taps(ca_ref, sh, offs_a, c0, LANES)
                sh = _shifted(cv_buf, r0, R_CHUNK, c0, LANES, offs_s)
                bconv_buf[pl.ds(r0, R_CHUNK), c0:c0 + LANES] = _taps(cb_ref, sh, offs_s, c0, LANES)
            return carry

        lax.fori_loop(0, ts // R_CHUNK, chunk, 0)
        glu_buf[0:HALO_A, :] = glu_buf[ts:ts + HALO_A, :]
        cv_buf[0:HALO_S, :] = cv_buf[ts:ts + HALO_S, :]

        ac = ac_ref[...]
        xc = ac - jnp.mean(ac, axis=-1, keepdims=True)
        xn = xc * lax.rsqrt(jnp.mean(xc * xc, axis=-1, keepdims=True) + LN_EPS)
        ln = xn * lg_ref[...] + lb_ref[...]
        cat_ref[:, 0:A] = (ln * _sigmoid(ln)).astype(BF16)
        cat_ref[:, A:2 * A] = (z_ref[:, 2 * A:3 * A] * bconv_buf[...]).astype(BF16)
        x1_ref[...] = xv + jnp.dot(cat_ref[...], wout_ref[...], preferred_element_type=F32)

    tile = lambda w: pl.BlockSpec((ts, w), lambda i: (i, 0))
    return _call(
        body, name="mix0_fwd", grid=(n,), comm=comm, args=(x, g, w_in, conv_a, ln_g, ln_b, conv_b, w_out),
        in_specs=[tile(D), _const((1, D)), _const((NCHIP, D, bw)), _const((K_A, A)), _const((1, A)), _const((1, A)),
                  _const((K_S, A)), _const((2 * A, D))],
        out_specs=[tile(D), tile(NZ), tile(A), tile(2 * A), tile(D)],
        out_shape=[jax.ShapeDtypeStruct((s, D), BF16), jax.ShapeDtypeStruct((s, NZ), F32), jax.ShapeDtypeStruct((s, A), F32),
                   jax.ShapeDtypeStruct((s, 2 * A), BF16), jax.ShapeDtypeStruct((s, D), F32)],
        scratch=[pltpu.VMEM((HALO_A + ts, A), F32), pltpu.VMEM((HALO_S + ts, A), F32), pltpu.VMEM((ts, A), F32)])


def _ffn_fwd(x, g, w_up, wc, w_down, name, comm=None):
    s = x.shape[0]
    ts = min(TS_FFN, s)
    n = s // ts
    bw = FF2 // NCHIP
    rows = 32
    offs = [HALO_S - (K_S - 1) + k for k in range(K_S)]

    def body(x_ref, g_ref, wup_ref, wc_ref, wdn_ref, h_ref, u0_ref, u_ref, act_ref, xo_ref, cbuf):
        i = pl.program_id(0)

        @pl.when(i == 0)
        def _():
            cbuf[0:HALO_S, :] = jnp.zeros((HALO_S, FF2), F32)

        xv = x_ref[...]
        h = (xv * _rsqrt_mean_sq(xv) * g_ref[...]).astype(BF16)
        h_ref[...] = h
        f = None
        for p in range(NCHIP // 2):
            for j in (p, NCHIP // 2 + p):
                zc = jnp.dot(h, wup_ref[j], preferred_element_type=F32)
                u0_ref[:, j * bw:(j + 1) * bw] = zc.astype(BF16)
                cbuf[HALO_S:HALO_S + ts, j * bw:(j + 1) * bw] = zc
            for r0 in range(0, ts, rows):
                for c0 in range(p * bw, (p + 1) * bw, LANES):
                    ug = _taps(wc_ref, _shifted(cbuf, r0, rows, c0, LANES, offs), offs, c0, LANES)
                    uv = _taps(wc_ref, _shifted(cbuf, r0, rows, FF + c0, LANES, offs), offs, FF + c0, LANES)
                    u_ref[r0:r0 + rows, c0:c0 + LANES] = ug
                    u_ref[r0:r0 + rows, FF + c0:FF + c0 + LANES] = uv
                    act_ref[r0:r0 + rows, c0:c0 + LANES] = (ug * _sigmoid(ug) * uv).astype(BF16)
            fp = jnp.dot(act_ref[:, p * bw:(p + 1) * bw], wdn_ref[p * bw:(p + 1) * bw, :], preferred_element_type=F32)
            f = fp if f is None else f + fp
        cbuf[0:HALO_S, :] = cbuf[ts:ts + HALO_S, :]
        xo_ref[...] = xv + f

    tile = lambda w: pl.BlockSpec((ts, w), lambda i: (i, 0))
    return _call(
        body, name=name, grid=(n,), comm=comm, args=(x, g, w_up, wc, w_down),
        in_specs=[tile(D), _const((1, D)), _const((NCHIP, D, bw)), _const((K_S, FF2)), _const((FF, D))],
        out_specs=[tile(D), tile(FF2), tile(FF2), tile(FF), tile(D)],
        out_shape=[jax.ShapeDtypeStruct((s, D), BF16), jax.ShapeDtypeStruct((s, FF2), BF16), jax.ShapeDtypeStruct((s, FF2), F32),
                   jax.ShapeDtypeStruct((s, FF), BF16), jax.ShapeDtypeStruct((s, D), F32)],
        scratch=[pltpu.VMEM((HALO_S + ts, FF2), F32)])


def _pool_windows(hbuf, pbuf, tile_row0, ts):
    def chunk(ci, carry):
        r0 = pl.multiple_of(ci * R_CHUNK, R_CHUNK)
        t1 = (tile_row0 + r0 + lax.broadcasted_iota(jnp.int32, (R_CHUNK, 1), 0) + 1).astype(F32)
        for gi, w in enumerate(POOL_WINDOWS):
            cnt = jnp.minimum(t1, float(w))
            offs = [HALO_P - jj for jj in range(w)]
            for c0 in range(gi * PG, (gi + 1) * PG, LANES):
                sh = _shifted(hbuf, r0, R_CHUNK, c0, LANES, offs)
                tot = sh[offs[0]]
                for o in offs[1:]:
                    tot = tot + sh[o]
                pbuf[pl.ds(r0, R_CHUNK), c0:c0 + LANES] = (tot / cnt - sh[HALO_P]).astype(BF16)
        return carry

    lax.fori_loop(0, ts // R_CHUNK, chunk, 0)


def _assemble_wpool(wp_ref, wps):
    rb = PG // NCHIP
    for gi in range(len(POOL_WINDOWS)):
        for j in range(NCHIP):
            wps[gi, j * rb:(j + 1) * rb, :] = wp_ref[j, gi]


def _pool_fwd(x, g, w_pool, scale):
    s = x.shape[0]
    ts = min(TS_POOL, s)
    n = s // ts
    ng = len(POOL_WINDOWS)

    def body(x_ref, g_ref, wp_ref, sc_ref, xo_ref, hbuf, pbuf, wps):
        i = pl.program_id(0)

        @pl.when(i == 0)
        def _():
            hbuf[0:HALO_P, :] = jnp.zeros((HALO_P, D), F32)
            _assemble_wpool(wp_ref, wps)

        xv = x_ref[...]
        hbuf[HALO_P:HALO_P + ts, :] = xv * _rsqrt_mean_sq(xv) * g_ref[...]
        _pool_windows(hbuf, pbuf, i * ts, ts)
        hbuf[0:HALO_P, :] = hbuf[ts:ts + HALO_P, :]
        for gi in range(ng):
            cols = slice(gi * PG, (gi + 1) * PG)
            y = jnp.dot(pbuf[:, cols], wps[gi], preferred_element_type=F32)
            xo_ref[:, cols] = xv[:, cols] + y * sc_ref[:, cols]

    tile = pl.BlockSpec((ts, D), lambda i: (i, 0))
    return pl.pallas_call(
        body, name="pool_fwd", grid=(n,),
        in_specs=[tile, _const((1, D)), _const((NCHIP, ng, PG // NCHIP, PG)), _const((1, D))],
        out_specs=tile, out_shape=jax.ShapeDtypeStruct((s, D), F32),
        scratch_shapes=[pltpu.VMEM((HALO_P + ts, D), F32), pltpu.VMEM((ts, D), BF16), pltpu.VMEM((ng, PG, PG), BF16)],
        compiler_params=_cparams(1),
    )(x, g, w_pool, scale)


def _final_loss(x, g, target):
    s = x.shape[0]
    ts = min(TS_MM, s)
    n = s // ts

    def body(x_ref, g_ref, t_ref, dx_ref, dg_ref, loss_ref):
        i = pl.program_id(0)

        @pl.when(i == 0)
        def _():
            dg_ref[...] = jnp.zeros((1, D), F32)
            loss_ref[...] = jnp.zeros((1, LANES), F32)

        xv = x_ref[...]
        r = _rsqrt_mean_sq(xv)
        xh = xv * r
        gv = g_ref[...]
        err = xh * gv - t_ref[...]
        sq = jnp.sum(jnp.sum(err * err, axis=1, keepdims=True), axis=0, keepdims=True)
        loss_ref[...] += sq * (0.5 / D)
        dy = err * (1.0 / D)
        dg_ref[...] += jnp.sum(dy * xh, axis=0, keepdims=True)
        dx_ref[...] = _rms_bwd(dy, xh, r, gv)

    tile = pl.BlockSpec((ts, D), lambda i: (i, 0))
    return pl.pallas_call(
        body, name="final_loss", grid=(n,),
        in_specs=[tile, _const((1, D)), tile],
        out_specs=[tile, pl.BlockSpec((1, D), lambda i: (0, 0)), pl.BlockSpec((1, LANES), lambda i: (0, 0))],
        out_shape=[jax.ShapeDtypeStruct((s, D), F32), jax.ShapeDtypeStruct((1, D), F32), jax.ShapeDtypeStruct((1, LANES), F32)],
        compiler_params=_cparams(1),
    )(x, g, target)


def _ffn_bwd_a(dxo, u, u0, wc, w_down, name, comm=None):
    s = dxo.shape[0]
    ts = min(TS_FFN, s)
    n = s // ts
    cw = FF2 // NCHIP
    rows = 32
    lw2 = 2 * LANES
    boffs = [K_S - 1 - k for k in range(K_S)]

    def body(dxo_ref, u_ref, u0_ref, wc_ref, wdn_ref, du0_ref, dwc_ref, dubuf, dact, dwacc):
        i = pl.program_id(0)

        @pl.when(i == 0)
        def _():
            dubuf[ts:ts + HALO_S, :] = jnp.zeros((HALO_S, FF2), F32)
            dwacc[...] = jnp.zeros(dwacc.shape, F32)

        df = dxo_ref[...].astype(BF16)
        for cg in range(0, FF, cw):
            dact[...] = lax.dot_general(df, wdn_ref[cg:cg + cw, :], NT_DIMS, preferred_element_type=F32)

            def chunk(ci, carry, cg=cg):
                rs = pl.ds(pl.multiple_of(ci * rows, rows), rows)
                for c in range(cg, cg + cw, LANES):
                    ug, uv = u_ref[rs, c:c + LANES], u_ref[rs, FF + c:FF + c + LANES]
                    sg = _sigmoid(ug)
                    da = dact[rs, c - cg:c - cg + LANES]
                    gs = ug * sg
                    dubuf[rs, c:c + LANES] = da * uv * (sg + gs * (1.0 - sg))
                    dubuf[rs, FF + c:FF + c + LANES] = da * gs
                return carry

            lax.fori_loop(0, ts // rows, chunk, 0)

        def chunk2(ci, carry):
            r0 = pl.multiple_of(ci * rows, rows)
            rs = pl.ds(r0, rows)
            for c in range(0, FF2, lw2):
                sh = _shifted(dubuf, r0, rows, c, lw2, boffs)
                du0_ref[rs, c:c + lw2] = _taps(wc_ref, sh, boffs, c, lw2).astype(BF16)
                u0v = u0_ref[rs, c:c + lw2].astype(F32)
                for k, o in enumerate(boffs):
                    dwacc[SUBLANES * k:SUBLANES * (k + 1), c:c + lw2] += _rowsum8(sh[o] * u0v)
            return carry

        lax.fori_loop(0, ts // rows, chunk2, 0)
        dubuf[ts:ts + HALO_S, :] = dubuf[0:HALO_S, :]

        @pl.when(i == n - 1)
        def _():
            _finish_tap_sums(dwacc, dwc_ref, K_S)

    rev = lambda w: pl.BlockSpec((ts, w), lambda i: (n - 1 - i, 0))
    return _call(
        body, name=name, grid=(n,), comm=comm, args=(dxo, u, u0, wc, w_down),
        in_specs=[rev(D), rev(FF2), rev(FF2), _const((K_S, FF2)), _const((FF, D))],
        out_specs=[rev(FF2), pl.BlockSpec((K_S, FF2), lambda i: (0, 0))],
        out_shape=[jax.ShapeDtypeStruct((s, FF2), BF16), jax.ShapeDtypeStruct((K_S, FF2), F32)],
        scratch=[pltpu.VMEM((ts + HALO_S, FF2), F32), pltpu.VMEM((ts, cw), F32), pltpu.VMEM((SUBLANES * K_S, FF2), F32)])


def _nt_rms_bwd(dy, w, x, g, dres, name, comm=None):
    s = x.shape[0]
    ts = min(TS_MM, s)
    n = s // ts
    nw = dy.shape[1]
    bw = nw // NCHIP

    def body(dy_ref, w_ref, x_ref, g_ref, dres_ref, dx_ref, dg_ref):
        i = pl.program_id(0)

        @pl.when(i == 0)
        def _():
            dg_ref[...] = jnp.zeros((1, D), F32)

        dh = lax.dot_general(dy_ref[:, 0:bw], w_ref[0], NT_DIMS, preferred_element_type=F32)
        for j in range(1, NCHIP):
            dh = dh + lax.dot_general(dy_ref[:, j * bw:(j + 1) * bw], w_ref[j], NT_DIMS, preferred_element_type=F32)
        xv = x_ref[...]
        r = _rsqrt_mean_sq(xv)
        xh = xv * r
        dg_ref[...] += jnp.sum(dh * xh, axis=0, keepdims=True)
        dx_ref[...] = dres_ref[...] + _rms_bwd(dh, xh, r, g_ref[...])

    tile = lambda wd: pl.BlockSpec((ts, wd), lambda i: (i, 0))
    return _call(
        body, name=name, grid=(n,), comm=comm, args=(dy, w, x, g, dres),
        in_specs=[tile(nw), _const((NCHIP, D, bw)), tile(D), _const((1, D)), tile(D)],
        out_specs=[tile(D), pl.BlockSpec((1, D), lambda i: (0, 0))],
        out_shape=[jax.ShapeDtypeStruct((s, D), F32), jax.ShapeDtypeStruct((1, D), F32)])


def _wgrad(a, b, n_blocks, name, comm=None):
    s, m = a.shape
    bw = b.shape[1] // n_blocks
    tk = min(TS_MM, s)

    def body(a_ref, b_ref, o_ref):
        @pl.when(pl.program_id(0) == 0)
        def _():
            o_ref[...] = jnp.zeros(o_ref.shape, F32)

        for j in range(n_blocks):
            o_ref[j] += lax.dot_general(a_ref[...], b_ref[:, j * bw:(j + 1) * bw].astype(BF16), TN_DIMS,
                                        preferred_element_type=F32)

    (out,), comm_out = _call(
        body, name=name, grid=(s // tk,), comm=comm, args=(a, b),
        in_specs=[pl.BlockSpec((tk, m), lambda k: (k, 0)), pl.BlockSpec((tk, b.shape[1]), lambda k: (k, 0))],
        out_specs=[pl.BlockSpec((n_blocks, m, bw), lambda k: (0, 0, 0))],
        out_shape=[jax.ShapeDtypeStruct((n_blocks, m, bw), F32)])
    return out, comm_out


def _pool_bwd(dxo, x, g, w_pool, scale, comm=None):
    s = x.shape[0]
    ts = min(TS_POOL, s)
    n = s // ts
    ng = len(POOL_WINDOWS)
    rb = PG // NCHIP

    def body(dxo_ref, x_ref, halo_ref, g_ref, wp_ref, sc_ref, dx_ref, dwp_ref, dsc_ref, dg_ref,
             hbuf, pbuf, qbuf, dhbuf, wps, dwacc):
        i = pl.program_id(0)
        j = n - 1 - i

        @pl.when(i == 0)
        def _():
            qbuf[ts:ts + HALO_P, :] = jnp.zeros((HALO_P, D), F32)
            dwacc[...] = jnp.zeros(dwacc.shape, F32)
            dsc_ref[...] = jnp.zeros((1, D), F32)
            dg_ref[...] = jnp.zeros((1, D), F32)
            _assemble_wpool(wp_ref, wps)

        gv = g_ref[...]
        xl = halo_ref[...]
        hbuf[0:HALO_P, :] = jnp.where(j == 0, 0.0, xl * _rsqrt_mean_sq(xl) * gv)
        xv = x_ref[...]
        r = _rsqrt_mean_sq(xv)
        xh = xv * r
        hbuf[HALO_P:HALO_P + ts, :] = xh * gv
        _pool_windows(hbuf, pbuf, j * ts, ts)

        dy = dxo_ref[...]
        t1 = (j * ts + lax.broadcasted_iota(jnp.int32, (ts, 1), 0) + 1).astype(F32)
        for gi, w in enumerate(POOL_WINDOWS):
            cols = slice(gi * PG, (gi + 1) * PG)
            p = pbuf[:, cols]
            y = jnp.dot(p, wps[gi], preferred_element_type=F32)
            dsc_ref[:, cols] += jnp.sum(dy[:, cols] * y, axis=0, keepdims=True)
            dq = (dy[:, cols] * sc_ref[:, cols]).astype(BF16)
            dwacc[gi] += lax.dot_general(p, dq, TN_DIMS, preferred_element_type=F32)
            dp = lax.dot_general(dq, wps[gi], NT_DIMS, preferred_element_type=F32)
            qbuf[0:ts, cols] = dp / jnp.minimum(t1, float(w))

        def chunk(ci, carry):
            r0 = pl.multiple_of(ci * R_CHUNK, R_CHUNK)
            tc = (j * ts + r0 + lax.broadcasted_iota(jnp.int32, (R_CHUNK, 1), 0) + 1).astype(F32)
            for gi, w in enumerate(POOL_WINDOWS):
                cnt = jnp.minimum(tc, float(w))
                offs = list(range(w))
                for c0 in range(gi * PG, (gi + 1) * PG, LANES):
                    sh = _shifted(qbuf, r0, R_CHUNK, c0, LANES, offs)
                    tot = sh[0]
                    for o in offs[1:]:
                        tot = tot + sh[o]
                    dhbuf[pl.ds(r0, R_CHUNK), c0:c0 + LANES] = tot - sh[0] * cnt
            return carry

        lax.fori_loop(0, ts // R_CHUNK, chunk, 0)
        qbuf[ts:ts + HALO_P, :] = qbuf[0:HALO_P, :]
        dh = dhbuf[...]
        dg_ref[...] += jnp.sum(dh * xh, axis=0, keepdims=True)
        dx_ref[...] = dy + _rms_bwd(dh, xh, r, gv)

        @pl.when(i == n - 1)
        def _():
            for gi in range(ng):
                for jj in range(NCHIP):
                    dwp_ref[jj, gi] = dwacc[gi, jj * rb:(jj + 1) * rb, :]

    rev = pl.BlockSpec((ts, D), lambda i: (n - 1 - i, 0))
    halo = pl.BlockSpec((HALO_P, D), lambda i: (jnp.maximum((n - 1 - i) * (ts // HALO_P) - 1, 0), 0))
    vec = pl.BlockSpec((1, D), lambda i: (0, 0))
    return _call(
        body, name="pool_bwd", grid=(n,), comm=comm, args=(dxo, x, x, g, w_pool, scale),
        in_specs=[rev, rev, halo, _const((1, D)), _const((NCHIP, ng, rb, PG)), _const((1, D))],
        out_specs=[rev, pl.BlockSpec((NCHIP, ng, rb, PG), lambda i: (0, 0, 0, 0)), vec, vec],
        out_shape=[jax.ShapeDtypeStruct((s, D), F32), jax.ShapeDtypeStruct((NCHIP, ng, rb, PG), F32),
                   jax.ShapeDtypeStruct((1, D), F32), jax.ShapeDtypeStruct((1, D), F32)],
        scratch=[pltpu.VMEM((HALO_P + ts, D), F32), pltpu.VMEM((ts, D), BF16), pltpu.VMEM((ts + HALO_P, D), F32),
                 pltpu.VMEM((ts, D), F32), pltpu.VMEM((ng, PG, PG), BF16), pltpu.VMEM((ng, PG, PG), F32)])


def _mix0_bwd_a(dx1, z, ac, w_out, conv_a, ln_g, ln_b, conv_b, comm=None):
    s = dx1.shape[0]
    ts = min(TS_MIXB, s)
    n = s // ts
    rows = 32
    offs_a = [HALO_A - (K_A - 1) + k for k in range(K_A)]
    offs_s = [HALO_S - (K_S - 1) + k for k in range(K_S)]
    boffs_a = [K_A - 1 - k for k in range(K_A)]
    boffs_s = [K_S - 1 - k for k in range(K_S)]

    def body(dx1_ref, z_ref, zh_ref, ac_ref, wout_ref, ca_ref, lg_ref, lb_ref, cb_ref,
             dz_ref, dca_ref, dlg_ref, dlb_ref, dcb_ref,
             glu_buf, cv_buf, dac_buf, dbc_buf, db_buf, dca_acc, dcb_acc):
        i = pl.program_id(0)
        j = n - 1 - i

        @pl.when(i == 0)
        def _():
            dac_buf[ts:ts + HALO_A, :] = jnp.zeros((HALO_A, A), F32)
            dbc_buf[ts:ts + HALO_S, :] = jnp.zeros((HALO_S, A), F32)
            dca_acc[...] = jnp.zeros(dca_acc.shape, F32)
            dcb_acc[...] = jnp.zeros(dcb_acc.shape, F32)
            dlg_ref[...] = jnp.zeros((1, A), F32)
            dlb_ref[...] = jnp.zeros((1, A), F32)

        glu_h = zh_ref[:, 0:A] * _sigmoid(zh_ref[:, A:2 * A])
        glu_buf[0:HALO_A, :] = jnp.where(j == 0, 0.0, glu_h)
        hs = slice(HALO_A - HALO_S, HALO_A)
        cv_buf[0:HALO_S, :] = jnp.where(j == 0, 0.0, zh_ref[hs, 3 * A:4 * A] * zh_ref[hs, 4 * A:5 * A])
        glu_buf[HALO_A:HALO_A + ts, :] = z_ref[:, 0:A] * _sigmoid(z_ref[:, A:2 * A])
        cv_buf[HALO_S:HALO_S + ts, :] = z_ref[:, 3 * A:4 * A] * z_ref[:, 4 * A:5 * A]

        dcat = lax.dot_general(dx1_ref[...].astype(BF16), wout_ref[...], NT_DIMS, preferred_element_type=F32)
        db_buf[...] = dcat[:, A:2 * A]
        ac = ac_ref[...]
        xc = ac - jnp.mean(ac, axis=-1, keepdims=True)
        rstd = lax.rsqrt(jnp.mean(xc * xc, axis=-1, keepdims=True) + LN_EPS)
        xn = xc * rstd
        lg = lg_ref[...]
        ln = xn * lg + lb_ref[...]
        sl = _sigmoid(ln)
        dln = dcat[:, 0:A] * sl * (1.0 + ln * (1.0 - sl))
        dlg_ref[...] += jnp.sum(dln * xn, axis=0, keepdims=True)
        dlb_ref[...] += jnp.sum(dln, axis=0, keepdims=True)
        dxn = dln * lg
        dac_buf[0:ts, :] = rstd * (dxn - jnp.mean(dxn, axis=-1, keepdims=True)
                                   - xn * jnp.mean(dxn * xn, axis=-1, keepdims=True))

        def chunk(ci, carry):
            r0 = pl.multiple_of(ci * rows, rows)
            rs = pl.ds(r0, rows)
            for c0 in range(0, A, LANES):
                cs = slice(c0, c0 + LANES)
                shc = _shifted(cv_buf, r0, rows, c0, LANES, offs_s)
                bconv = _taps(cb_ref, shc, offs_s, c0, LANES)
                dbv = db_buf[rs, cs]
                dz_ref[rs, 2 * A + c0:2 * A + c0 + LANES] = (dbv * bconv).astype(BF16)
                dbconv = dbv * z_ref[rs, 2 * A + c0:2 * A + c0 + LANES]
                dbc_buf[rs, cs] = dbconv
                for k, o in enumerate(offs_s):
                    dcb_acc[SUBLANES * k:SUBLANES * (k + 1), cs] += _rowsum8(dbconv * shc[o])
                shg = _shifted(glu_buf, r0, rows, c0, LANES, offs_a)
                dacv = dac_buf[rs, cs]
                for k, o in enumerate(offs_a):
                    dca_acc[SUBLANES * k:SUBLANES * (k + 1), cs] += _rowsum8(dacv * shg[o])
            return carry

        lax.fori_loop(0, ts // rows, chunk, 0)

        def chunk2(ci, carry):
            r0 = pl.multiple_of(ci * rows, rows)
            rs = pl.ds(r0, rows)
            for c0 in range(0, A, LANES):
                col = lambda grp: slice(grp * A + c0, grp * A + c0 + LANES)
                dglu = _taps(ca_ref, _shifted(dac_buf, r0, rows, c0, LANES, boffs_a), boffs_a, c0, LANES)
                sg = _sigmoid(z_ref[rs, col(1)])
                dz_ref[rs, col(0)] = (dglu * sg).astype(BF16)
                dz_ref[rs, col(1)] = (dglu * z_ref[rs, col(0)] * sg * (1.0 - sg)).astype(BF16)
                dcv = _taps(cb_ref, _shifted(dbc_buf, r0, rows, c0, LANES, boffs_s), boffs_s, c0, LANES)
                dz_ref[rs, col(3)] = (dcv * z_ref[rs, col(4)]).astype(BF16)
                dz_ref[rs, col(4)] = (dcv * z_ref[rs, col(3)]).astype(BF16)
            return carry

        lax.fori_loop(0, ts // rows, chunk2, 0)
        dac_buf[ts:ts + HALO_A, :] = dac_buf[0:HALO_A, :]
        dbc_buf[ts:ts + HALO_S, :] = dbc_buf[0:HALO_S, :]

        @pl.when(i == n - 1)
        def _():
            _finish_tap_sums(dca_acc, dca_ref, K_A)
            _finish_tap_sums(dcb_acc, dcb_ref, K_S)

    rev = lambda w: pl.BlockSpec((ts, w), lambda i: (n - 1 - i, 0))
    halo = pl.BlockSpec((HALO_A, NZ), lambda i: (jnp.maximum((n - 1 - i) * (ts // HALO_A) - 1, 0), 0))
    full = lambda r, c: pl.BlockSpec((r, c), lambda i: (0, 0))
    return _call(
        body, name="mix0_bwd_a", grid=(n,), comm=comm, args=(dx1, z, z, ac, w_out, conv_a, ln_g, ln_b, conv_b),
        in_specs=[rev(D), rev(NZ), halo, rev(A), _const((2 * A, D)), _const((K_A, A)), _const((1, A)), _const((1, A)),
                  _const((K_S, A))],
        out_specs=[rev(NZ), full(K_A, A), full(1, A), full(1, A), full(K_S, A)],
        out_shape=[jax.ShapeDtypeStruct((s, NZ), BF16), jax.ShapeDtypeStruct((K_A, A), F32), jax.ShapeDtypeStruct((1, A), F32),
                   jax.ShapeDtypeStruct((1, A), F32), jax.ShapeDtypeStruct((K_S, A), F32)],
        scratch=[pltpu.VMEM((HALO_A + ts, A), F32), pltpu.VMEM((HALO_S + ts, A), F32), pltpu.VMEM((ts + HALO_A, A), F32),
                 pltpu.VMEM((ts + HALO_S, A), F32), pltpu.VMEM((ts, A), F32), pltpu.VMEM((SUBLANES * K_A, A), F32),
                 pltpu.VMEM((SUBLANES * K_S, A), F32)])


def _train_step(x, target, place, shard, rep, small_pack, small_shapes):
    bw_up, bw_in = FF2 // NCHIP, NZ // NCHIP
    five = lambda a, k: a.reshape(NCHIP, 2, *HALF[k])

    g_in, g_out, small_g = _run_comm(_gather_comm([shard["w_in"], shard["w_out"]], small_pack), "ag_first")
    whole = {}
    for k, part in zip(SMALL_SHARDED, _unpack(small_g, small_shapes, lead=(NCHIP,))):
        whole[k] = jnp.moveaxis(part, 0, 1).reshape(part.shape[1], NCHIP * part.shape[2])
    w_in, w_out = g_in.reshape(NCHIP, D, bw_in), g_out.reshape(2 * A, D)
    conv_ffn = whole["conv_ffn_w"].reshape(2, K_S, FF2)
    nffn = [rep["norm_ffn"][0:1], rep["norm_ffn"][1:2]]

    (h0, z, ac, cat, x1), (g_up0, g_dn0) = _mix0_fwd(
        x, rep["norm_mix_even"], w_in, whole["conv_a"], rep["ln_a_g"], rep["ln_a_b"], whole["conv_b"], w_out,
        comm=_gather_comm([shard["w_up0"], shard["w_down0"]]))
    w_up0, w_dn0 = g_up0.reshape(NCHIP, D, bw_up), g_dn0.reshape(FF, D)
    (hf0, u00, u0, act0, x2), (g_pool, g_up1, g_dn1) = _ffn_fwd(
        x1, nffn[0], w_up0, conv_ffn[0], w_dn0, "ffn0_fwd",
        comm=_gather_comm([shard["w_pool"], shard["w_up1"], shard["w_down1"]]))
    w_pool = g_pool.reshape(NCHIP, len(POOL_WINDOWS), PG // NCHIP, PG)
    w_up1, w_dn1 = g_up1.reshape(NCHIP, D, bw_up), g_dn1.reshape(FF, D)
    x3 = _pool_fwd(x2, whole["norm_mix_odd"], w_pool, whole["pool_scale"])
    (hf1, u01, u1, act1, x4), _ = _ffn_fwd(x3, nffn[1], w_up1, conv_ffn[1], w_dn1, "ffn1_fwd")
    dx4, g_nfin, loss_part = _final_loss(x4, rep["norm_final"], target)

    def wgrads_ffn(hf, du0, act, dxo, tag):
        g_up, _ = _wgrad(hf, du0, NCHIP, "wgrad_up" + tag)
        g_dn, _ = _wgrad(act, dxo, 1, "wgrad_down" + tag)
        return five(g_up, "w_up"), five(g_dn, "w_down")

    def pair_sums(keys, grads, landed, tag):
        return [_pair_sum(place, g, ld, "pair_sum_" + k + tag) for k, g, ld in zip(keys, grads, landed)]

    (du01, g_wc1), _ = _ffn_bwd_a(dx4, u1, u01, conv_ffn[1], w_dn1, "ffn1_bwd_a")
    grads1 = wgrads_ffn(hf1, du01, act1, dx4, "1")
    (dx3, g_nf1), landed1 = _nt_rms_bwd(du01, w_up1, x3, nffn[1], dx4, "ffn1_bwd_b", comm=_pair_comm(grads1))
    sums1 = pair_sums(("w_up", "w_down"), grads1, landed1, "1")
    (dx2, g_wpool, g_scale, g_nmo), _ = _pool_bwd(dx3, x2, whole["norm_mix_odd"], w_pool, whole["pool_scale"])
    (du00, g_wc0), parts1 = _ffn_bwd_a(dx2, u0, u00, conv_ffn[0], w_dn0, "ffn0_bwd_a", comm=_chips_comm(sums1))
    grads0 = wgrads_ffn(hf0, du00, act0, dx2, "0") + (five(g_wpool, "w_pool"),)
    (dx1, g_nf0), landed0 = _nt_rms_bwd(du00, w_up0, x1, nffn[0], dx2, "ffn0_bwd_b", comm=_pair_comm(grads0))
    sums0 = pair_sums(("w_up", "w_down", "w_pool"), grads0, landed0, "0")
    (dz, g_ca, g_lg, g_lb, g_cb), parts0 = _mix0_bwd_a(
        dx1, z, ac, w_out, whole["conv_a"], rep["ln_a_g"], rep["ln_a_b"], whole["conv_b"], comm=_chips_comm(sums0))
    gr_out, _ = _wgrad(cat, dx1, 1, "wgrad_out")
    gr_in, _ = _wgrad(h0, dz, NCHIP, "wgrad_in")
    gradsm = (five(gr_in, "w_in"), five(gr_out, "w_out"))
    (grad_x, g_nme), landedm = _nt_rms_bwd(dz, w_in, x, rep["norm_mix_even"], dx1, "mix0_bwd_b", comm=_pair_comm(gradsm))
    sumsm = pair_sums(("w_in", "w_out"), gradsm, landedm, "")

    small = {"norm_mix_even": g_nme, "conv_a": g_ca, "ln_a_g": g_lg, "ln_a_b": g_lb, "conv_b": g_cb, "norm_mix_odd": g_nmo,
             "pool_scale": g_scale, "norm_ffn": jnp.concatenate([g_nf0, g_nf1], axis=0),
             "conv_ffn_w": jnp.stack([g_wc0, g_wc1]), "norm_final": g_nfin}
    *partsm, small_all = _run_comm(_chips_comm(sumsm, _pack([small[k] for k in SMALL_ALL] + [loss_part])), "rs_last")

    tot = lambda k, g, ld, p, layer=0, nl=1, prev=None: _chip_sum(place, g, ld, p, layer, nl, "chip_sum_%s%d" % (k, layer), prev)
    t_in = tot("w_in", gradsm[0], landedm[0], partsm[0])
    t_out = tot("w_out", gradsm[1], landedm[1], partsm[1])
    t_pool = tot("w_pool", grads0[2], landed0[2], parts0[2])
    t_up = tot("w_up", grads1[0], landed1[0], parts1[0], 1, 2)
    t_up = tot("w_up", grads0[0], landed0[0], parts0[0], 0, 2, t_up)
    t_dn = tot("w_down", grads1[1], landed1[1], parts1[1], 1, 2)
    t_dn = tot("w_down", grads0[1], landed0[1], parts0[1], 0, 2, t_dn)
    swapped = _run_comm(_swap_comm([t_in, t_out, t_pool, t_up, t_dn]), "rs_swap")
    return grad_x, dict(zip(BIG, swapped)), _sum_devices(small_all), [small[k].shape for k in SMALL_ALL] + [(1, LANES)]


def _adamw_math(w, g, m, v):
    m = ADAM_B1 * m + (1.0 - ADAM_B1) * g
    v = ADAM_B2 * v + (1.0 - ADAM_B2) * (g * g)
    m_hat = m / (1.0 - ADAM_B1 ** ADAM_STEP)
    v_hat = v / (1.0 - ADAM_B2 ** ADAM_STEP)
    return -ADAM_LR * (m_hat / (jnp.sqrt(v_hat) + ADAM_EPS) + ADAM_WD * w), m, v


def _adamw_big(w, g, m, v, tr, name):
    nl, rows, cols = w.shape

    def body(w_ref, g_ref, m_ref, v_ref, g2_ref, d_ref, m2_ref, v2_ref):
        gv = g_ref[...]
        g2_ref[...] = gv
        d_ref[...], m2_ref[...], v2_ref[...] = _adamw_math(w_ref[...], gv, m_ref[...], v_ref[...])

    spec = pl.BlockSpec((None, tr, cols), lambda l, r: (l, r, 0))
    return pl.pallas_call(
        body, name=name, grid=(nl, rows // tr), in_specs=[spec] * 4, out_specs=[spec] * 4,
        out_shape=[jax.ShapeDtypeStruct(w.shape, F32)] * 4, compiler_params=_cparams(2),
    )(w, g, m, v)


def _adamw_small(ws, gs, ms, vs):
    n = len(ws)

    def body(*refs):
        for p in range(n):
            w_ref, g_ref, m_ref, v_ref = (refs[q * n + p] for q in range(4))
            d_ref, m2_ref, v2_ref = (refs[(4 + q) * n + p] for q in range(3))
            d_ref[...], m2_ref[...], v2_ref[...] = _adamw_math(w_ref[...], g_ref[...], m_ref[...], v_ref[...])

    whole = lambda a: pl.BlockSpec(a.shape, lambda: (0,) * a.ndim)
    outs = pl.pallas_call(
        body, name="adamw_small", in_specs=[whole(a) for a in ws] * 4, out_specs=[whole(a) for a in ws] * 3,
        out_shape=[jax.ShapeDtypeStruct(a.shape, F32) for a in ws] * 3,
        compiler_params=pltpu.CompilerParams(vmem_limit_bytes=VMEM_LIMIT_BYTES),
    )(*ws, *gs, *ms, *vs)
    return outs[0:n], outs[n:2 * n], outs[2 * n:3 * n]


def _place():
    x, y, c = lax.axis_index("x"), lax.axis_index("y"), lax.axis_index("c")
    chips = [(x, 1 - y), (1 - x, y), (1 - x, 1 - y)]
    blocks = [2 * cx + cy for cx, cy in chips]
    return x, y, c, 2 * x + y, chips, blocks


def _gather_comm(shards, small=None):
    na = len(shards)
    nk = NCHIP - 1
    ns = 0 if small is None else 1

    def copies(ins, outs, sems):
        ici_send, ici_recv, fwd_send, fwd_recv, own_send, own_recv = sems[:6]
        x, y, c, j, chips, blocks = _place()
        sib = (x, y, 1 - c)

        def ici(a, k, arrival):
            dst = outs[a].at[blocks[k], c] if arrival else outs[a].at[j, c]
            return pltpu.make_async_remote_copy(
                src_ref=dst if arrival else ins[a].at[c], dst_ref=dst, send_sem=ici_send.at[a * nk + k],
                recv_sem=ici_recv.at[a * nk + k], device_id=(*chips[k], c), device_id_type=MESH)

        def fwd(a, k, half):
            ref = outs[a].at[blocks[k], half]
            return pltpu.make_async_remote_copy(
                src_ref=ref, dst_ref=ref, send_sem=fwd_send.at[a * nk + k], recv_sem=fwd_recv.at[a * nk + k],
                device_id=sib, device_id_type=MESH)

        own = [pltpu.make_async_remote_copy(src_ref=ins[a], dst_ref=outs[a].at[j], send_sem=own_send.at[a],
                                            recv_sem=own_recv.at[a], device_id=sib, device_id_type=MESH) for a in range(na)]
        small_copies = [pltpu.make_async_remote_copy(
            src_ref=ins[na], dst_ref=outs[na].at[j], send_sem=ici_send.at[na * nk + k], recv_sem=ici_recv.at[na * nk + k],
            device_id=(*chips[k], c), device_id_type=MESH) for k in range(nk * ns)]
        local = [pltpu.make_async_copy(ins[na], outs[na].at[j], sems[6])] if ns else []
        return ici, fwd, own, small_copies, local, c

    def start(ins, outs, sems):
        ici, _, own, small_copies, local, _ = copies(ins, outs, sems)
        for cp in local + own + [ici(a, k, False) for a in range(na) for k in range(nk)] + small_copies:
            cp.start()

    def finish(ins, outs, sems):
        ici, fwd, own, small_copies, local, c = copies(ins, outs, sems)
        for a in range(na):
            for k in range(nk):
                ici(a, k, True).wait_recv()
                fwd(a, k, c).start()
        for cp in small_copies:
            cp.wait()
        for a in range(na):
            for k in range(nk):
                ici(a, k, False).wait_send()
                fwd(a, k, c).wait_send()
                fwd(a, k, 1 - c).wait_recv()
        for cp in own + local:
            cp.wait()

    out_shapes = [jax.ShapeDtypeStruct((NCHIP,) + s.shape, s.dtype) for s in shards]
    sems = [pltpu.SemaphoreType.DMA((na * nk + nk * ns,)), pltpu.SemaphoreType.DMA((na * nk + nk * ns,)),
            pltpu.SemaphoreType.DMA((na * nk,)), pltpu.SemaphoreType.DMA((na * nk,)),
            pltpu.SemaphoreType.DMA((na,)), pltpu.SemaphoreType.DMA((na,))]
    if ns:
        out_shapes.append(jax.ShapeDtypeStruct((NCHIP,) + small.shape, small.dtype))
        sems.append(pltpu.SemaphoreType.DMA)
    return _Comm(list(shards) + [small] * ns, out_shapes, sems, start, finish)


def _simple_comm(inputs, out_shapes, make_copies, n_sems, aliases=None):
    def start(ins, outs, sems):
        for cp in make_copies(ins, outs, sems):
            cp.start()

    def finish(ins, outs, sems):
        for cp in make_copies(ins, outs, sems):
            cp.wait()

    return _Comm(inputs, out_shapes, [pltpu.SemaphoreType.DMA((n,)) for n in n_sems], start, finish, aliases)


def _pair_comm(grads):
    def make_copies(ins, outs, sems):
        x, y, c, _, _, _ = _place()
        return [pltpu.make_async_remote_copy(
            src_ref=ins[a].at[:, 1 - c], dst_ref=outs[a], send_sem=sems[0].at[a], recv_sem=sems[1].at[a],
            device_id=(x, y, 1 - c), device_id_type=MESH) for a in range(len(grads))]

    out_shapes = [jax.ShapeDtypeStruct(g.shape[:1] + g.shape[2:], F32) for g in grads]
    return _simple_comm(grads, out_shapes, make_copies, [len(grads)] * 2)


def _chips_comm(sums, small=None):
    na = len(sums)
    nk = NCHIP - 1

    def make_copies(ins, outs, sems):
        x, y, c, _, chips, blocks = _place()
        copies = [pltpu.make_async_remote_copy(
            src_ref=ins[a].at[blocks[k]], dst_ref=outs[a].at[k], send_sem=sems[0].at[a * nk + k],
            recv_sem=sems[1].at[a * nk + k], device_id=(*chips[k], c), device_id_type=MESH)
            for a in range(na) for k in range(nk)]
        if small is not None:
            me = 4 * x + 2 * y + c
            for r in range(1, NDEV):
                peer = (1 - x if r & 4 else x, 1 - y if r & 2 else y, 1 - c if r & 1 else c)
                copies.append(pltpu.make_async_remote_copy(
                    src_ref=ins[na], dst_ref=outs[na].at[me], send_sem=sems[2].at[r - 1], recv_sem=sems[3].at[r - 1],
                    device_id=peer, device_id_type=MESH))
            copies.append(pltpu.make_async_copy(ins[na], outs[na].at[me], sems[4].at[0]))
        return copies

    out_shapes = [jax.ShapeDtypeStruct((nk,) + g.shape[1:], BF16) for g in sums]
    if small is None:
        return _simple_comm(sums, out_shapes, make_copies, [na * nk] * 2)
    out_shapes.append(jax.ShapeDtypeStruct((NDEV,) + small.shape, F32))
    return _simple_comm(list(sums) + [small], out_shapes, make_copies, [na * nk] * 2 + [NDEV - 1] * 2 + [1])


def _swap_comm(totals):
    def make_copies(ins, outs, sems):
        x, y, c, _, _, _ = _place()
        return [pltpu.make_async_remote_copy(
            src_ref=outs[a].at[:, c], dst_ref=outs[a].at[:, c], send_sem=sems[0].at[a], recv_sem=sems[1].at[a],
            device_id=(x, y, 1 - c), device_id_type=MESH) for a in range(len(totals))]

    out_shapes = [jax.ShapeDtypeStruct(t.shape, F32) for t in totals]
    return _simple_comm(totals, out_shapes, make_copies, [len(totals)] * 2, aliases={a: a for a in range(len(totals))})


def _pair_sum(place, grad, landed, name):
    _, _, rows, cols = grad.shape

    def body(place_ref, g_ref, l_ref, o_ref):
        o_ref[...] = (g_ref[...] + l_ref[...]).astype(BF16)

    return pl.pallas_call(
        body, name=name,
        grid_spec=pltpu.PrefetchScalarGridSpec(
            num_scalar_prefetch=1, grid=(NCHIP,),
            in_specs=[pl.BlockSpec((None, None, rows, cols), lambda b, p: (b, p[1], 0, 0)),
                      pl.BlockSpec((None, rows, cols), lambda b, p: (b, 0, 0))],
            out_specs=pl.BlockSpec((None, rows, cols), lambda b, p: (b, 0, 0))),
        out_shape=jax.ShapeDtypeStruct((NCHIP, rows, cols), BF16), compiler_params=_cparams(1),
    )(place, grad, landed)


def _chip_sum(place, grad, landed, parts, layer, n_layers, name, prev=None):
    _, _, rows, cols = grad.shape

    def body(*refs):
        g_ref, l_ref, p_ref, o_ref = refs[1], refs[2], refs[3], refs[-1]
        tot = g_ref[...] + l_ref[...]
        for k in range(NCHIP - 1):
            tot = tot + p_ref[k].astype(F32)
        o_ref[...] = tot

    in_specs = [pl.BlockSpec((None, None, rows, cols), lambda i, p: (p[0], p[1], 0, 0)),
                pl.BlockSpec((None, rows, cols), lambda i, p: (p[0], 0, 0)),
                pl.BlockSpec((NCHIP - 1, rows, cols), lambda i, p: (0, 0, 0))]
    args = [place, grad, landed, parts]
    if prev is not None:
        in_specs.append(ANY)
        args.append(prev)
    return pl.pallas_call(
        body, name=name,
        grid_spec=pltpu.PrefetchScalarGridSpec(
            num_scalar_prefetch=1, grid=(1,), in_specs=in_specs,
            out_specs=pl.BlockSpec((None, None, rows, cols), lambda i, p: (layer, p[1], 0, 0))),
        out_shape=jax.ShapeDtypeStruct((n_layers, 2, rows, cols), F32),
        input_output_aliases={} if prev is None else {4: 0}, compiler_params=_cparams(1),
    )(*args)


def _sum_devices(parts):
    def body(p_ref, o_ref):
        tot = p_ref[0]
        for d in range(1, NDEV):
            tot = tot + p_ref[d]
        o_ref[...] = tot

    return pl.pallas_call(
        body, name="sum_small", in_specs=[pl.BlockSpec(parts.shape, lambda: (0, 0, 0))],
        out_specs=pl.BlockSpec(parts.shape[1:], lambda: (0, 0)), out_shape=jax.ShapeDtypeStruct(parts.shape[1:], F32),
    )(parts)


def _pack(parts):
    rows = []
    for p in parts:
        p = p.reshape(-1, LANES)
        rows.append(jnp.pad(p, ((0, -p.shape[0] % SUBLANES), (0, 0))))
    return jnp.concatenate(rows, axis=0)


def _unpack(buf, shapes, lead=()):
    out, r0 = [], 0
    nl = len(lead)
    for shp in shapes:
        nrow = 1
        for d in shp:
            nrow *= d
        nrow //= LANES
        out.append(buf[(slice(None),) * nl + (slice(r0, r0 + nrow),)].reshape(lead + tuple(shp)))
        r0 += nrow + (-nrow % SUBLANES)
    return out


WEIGHT_ORDER = ("norm_mix_even", "w_in", "conv_a", "ln_a_g", "ln_a_b", "conv_b", "w_out", "norm_mix_odd", "w_pool",
                "pool_scale", "norm_ffn", "w_up", "conv_ffn_w", "w_down", "norm_final")
BIG = ("w_in", "w_out", "w_pool", "w_up", "w_down")
HALF = {"w_in": (D // 2, NZ // NCHIP), "w_out": (2 * A // NCHIP // 2, D), "w_pool": (PG // 2, PG),
        "w_up": (D // 2, FF2 // NCHIP), "w_down": (FF // NCHIP // 2, D)}
SMALL_SHARDED = ("conv_a", "conv_b", "conv_ffn_w", "norm_mix_odd", "pool_scale")
SMALL_ALL = ("norm_mix_even", "conv_a", "ln_a_g", "ln_a_b", "conv_b", "norm_mix_odd", "pool_scale", "norm_ffn", "conv_ffn_w",
             "norm_final")


def kernel(x, norm_mix_even, w_in, conv_a, ln_a_g, ln_a_b, conv_b, w_out, norm_mix_odd, w_pool, pool_scale, norm_ffn, w_up, conv_ffn_w, w_down, norm_final, loss_target, m_norm_mix_even, m_w_in, m_conv_a, m_ln_a_g, m_ln_a_b, m_conv_b, m_w_out, m_norm_mix_odd, m_w_pool, m_pool_scale, m_norm_ffn, m_w_up, m_conv_ffn_w, m_w_down, m_norm_final, v_norm_mix_even, v_w_in, v_conv_a, v_ln_a_g, v_ln_a_b, v_conv_b, v_w_out, v_norm_mix_odd, v_w_pool, v_pool_scale, v_norm_ffn, v_w_up, v_conv_ffn_w, v_w_down, v_norm_final):
    w = dict(norm_mix_even=norm_mix_even, w_in=w_in, conv_a=conv_a, ln_a_g=ln_a_g, ln_a_b=ln_a_b, conv_b=conv_b, w_out=w_out,
             norm_mix_odd=norm_mix_odd, w_pool=w_pool, pool_scale=pool_scale, norm_ffn=norm_ffn, w_up=w_up,
             conv_ffn_w=conv_ffn_w, w_down=w_down, norm_final=norm_final)
    m = dict(norm_mix_even=m_norm_mix_even, w_in=m_w_in, conv_a=m_conv_a, ln_a_g=m_ln_a_g, ln_a_b=m_ln_a_b, conv_b=m_conv_b,
             w_out=m_w_out, norm_mix_odd=m_norm_mix_odd, w_pool=m_w_pool, pool_scale=m_pool_scale, norm_ffn=m_norm_ffn,
             w_up=m_w_up, conv_ffn_w=m_conv_ffn_w, w_down=m_w_down, norm_final=m_norm_final)
    v = dict(norm_mix_even=v_norm_mix_even, w_in=v_w_in, conv_a=v_conv_a, ln_a_g=v_ln_a_g, ln_a_b=v_ln_a_b, conv_b=v_conv_b,
             w_out=v_w_out, norm_mix_odd=v_norm_mix_odd, w_pool=v_w_pool, pool_scale=v_pool_scale, norm_ffn=v_norm_ffn,
             w_up=v_w_up, conv_ffn_w=v_conv_ffn_w, w_down=v_w_down, norm_final=v_norm_final)
    chip = 2 * lax.axis_index("x") + lax.axis_index("y")
    place = jnp.stack([chip, lax.axis_index("c")]).astype(jnp.int32)

    half = lambda a, name: a.astype(BF16).reshape((2,) + HALF[name])
    shard = {"w_in": half(w_in[0], "w_in"), "w_out": half(w_out[0], "w_out"), "w_pool": half(w_pool[0], "w_pool"),
             "w_up0": half(w_up[0], "w_up"), "w_up1": half(w_up[1], "w_up"),
             "w_down0": half(w_down[0], "w_down"), "w_down1": half(w_down[1], "w_down")}
    small_shapes = [w[k].shape[-2:] if w[k].ndim == 3 and k != "conv_ffn_w" else (w[k].size // w[k].shape[-1], w[k].shape[-1])
                    for k in SMALL_SHARDED]
    rep = dict(norm_mix_even=norm_mix_even, ln_a_g=ln_a_g, ln_a_b=ln_a_b, norm_ffn=norm_ffn, norm_final=norm_final.reshape(1, D))
    grad_x, swapped, small_packed, small_full_shapes = _train_step(
        x[0], loss_target[0], place, shard, rep, _pack([w[k] for k in SMALL_SHARDED]), small_shapes)
    small_sum = _unpack(small_packed, small_full_shapes)
    loss = small_sum[-1][0, 0]

    grad = {k: swapped[k].reshape(w[k].shape) for k in BIG}
    for k, gsum in zip(SMALL_ALL, small_sum):
        if k in SMALL_SHARDED:
            cols = w[k].shape[-1]
            gsum = lax.dynamic_slice_in_dim(gsum, chip * cols, cols, axis=gsum.ndim - 1)
        grad[k] = gsum.reshape(w[k].shape)

    delta, new_m, new_v = {}, {}, {}
    rows_per_step = {"w_in": 512, "w_out": 256, "w_pool": 256, "w_up": 256, "w_down": 352}
    for k in BIG:
        as3 = lambda a: a.reshape(a.shape[0], -1, a.shape[-1])
        g3, d3, m3, v3 = _adamw_big(as3(w[k]), as3(grad[k]), as3(m[k]), as3(v[k]), rows_per_step[k], "adamw_" + k)
        grad[k], delta[k], new_m[k], new_v[k] = (a.reshape(w[k].shape) for a in (g3, d3, m3, v3))
    as2 = lambda a: a.reshape(-1, a.shape[-1])
    ds, ms, vs = _adamw_small(*[[as2(t[k]) for k in SMALL_ALL] for t in (w, grad, m, v)])
    for k, d2, m2, v2 in zip(SMALL_ALL, ds, ms, vs):
        delta[k], new_m[k], new_v[k] = (a.reshape(w[k].shape) for a in (d2, m2, v2))

    return (loss, grad_x[None], *[grad[k] for k in WEIGHT_ORDER], *[delta[k] for k in WEIGHT_ORDER],
            *[new_m[k] for k in WEIGHT_ORDER], *[new_v[k] for k in WEIGHT_ORDER])
```

```python
import functools

import jax
import jax.numpy as jnp
from jax import lax
from jax.experimental import pallas as pl
from jax.experimental.pallas import tpu as pltpu

F32, BF16 = jnp.float32, jnp.bfloat16

D = 1024
A = 512
NZ = 5 * A
FF = 2816
FF2 = 2 * FF
NCHIP = 4
NDEV = 8
K_A, K_S = 31, 3
POOL_WINDOWS = (2, 4, 8, 16)
PG = D // len(POOL_WINDOWS)
RMS_EPS, LN_EPS = 1e-6, 1e-5
ADAM_LR, ADAM_B1, ADAM_B2, ADAM_EPS, ADAM_WD, ADAM_STEP = 0.001, 0.9, 0.999, 1e-08, 0.01, 10

HALO_A, HALO_S, HALO_P = 32, 8, 16
SUBLANES = 8
LANES = 128
VMEM_LIMIT_BYTES = 56 * 1024 * 1024

TS_MIX = 512
TS_MIXB = 256
TS_FFN = 256
TS_POOL = 512
TS_MM = 512
R_CHUNK = 64

MESH = pl.DeviceIdType.MESH
ANY = pl.BlockSpec(memory_space=pl.ANY)
NT_DIMS = (((1,), (1,)), ((), ()))
TN_DIMS = (((0,), (0,)), ((), ()))


def _cparams(n_axes):
    return pltpu.CompilerParams(dimension_semantics=("arbitrary",) * n_axes, vmem_limit_bytes=VMEM_LIMIT_BYTES)


def _const(shape):
    nd = len(shape)
    return pl.BlockSpec(shape, lambda *_: (0,) * nd, pipeline_mode=pl.Buffered(1))


def _sigmoid(v):
    return 1.0 / (1.0 + jnp.exp(-v))


def _rsqrt_mean_sq(x):
    return lax.rsqrt(jnp.mean(x * x, axis=-1, keepdims=True) + RMS_EPS)


def _rms_bwd(dh, xh, r, g):
    dxh = dh * g
    return r * (dxh - xh * jnp.mean(dxh * xh, axis=-1, keepdims=True))


def _shifted(buf_ref, row0, rows, col0, width, offsets):
    lo = (min(offsets) // SUBLANES) * SUBLANES
    hi = -(-(max(offsets) + rows) // SUBLANES) * SUBLANES
    start = row0 + lo if isinstance(row0, int) else pl.multiple_of(row0 + lo, SUBLANES)
    win = buf_ref[pl.ds(start, hi - lo), col0:col0 + width]
    out = {}
    for res in sorted({(o - lo) % SUBLANES for o in offsets}):
        qs = {o: (o - lo) // SUBLANES for o in offsets if (o - lo) % SUBLANES == res}
        base = win[res:res + rows + SUBLANES * max(qs.values()), :]
        for o, q in qs.items():
            out[o] = base[SUBLANES * q:SUBLANES * q + rows, :]
    return out


def _rowsum8(v):
    acc = v[0:SUBLANES, :]
    for r in range(SUBLANES, v.shape[0], SUBLANES):
        acc = acc + v[r:r + SUBLANES, :]
    return acc


def _taps(w_ref, sh, offsets, col0, width):
    acc = None
    for k, o in enumerate(offsets):
        term = w_ref[k:k + 1, col0:col0 + width] * sh[o]
        acc = term if acc is None else acc + term
    return acc


def _finish_tap_sums(acc_ref, out_ref, n_taps):
    for k in range(n_taps):
        out_ref[k:k + 1, :] = jnp.sum(acc_ref[SUBLANES * k:SUBLANES * (k + 1), :], axis=0, keepdims=True)


class _Comm:
    def __init__(self, inputs, out_shapes, sems, start, finish, aliases=None):
        self.inputs, self.out_shapes, self.sems = list(inputs), list(out_shapes), list(sems)
        self.start, self.finish, self.aliases = start, finish, dict(aliases or {})


def _join_comm(a, b):
    ni, no, ns = len(a.inputs), len(a.out_shapes), len(a.sems)

    def start(ins, outs, sems):
        a.start(ins[:ni], outs[:no], sems[:ns])
        b.start(ins[ni:], outs[no:], sems[ns:])

    def finish(ins, outs, sems):
        a.finish(ins[:ni], outs[:no], sems[:ns])
        b.finish(ins[ni:], outs[no:], sems[ns:])

    aliases = {**a.aliases, **{ni + i: no + o for i, o in b.aliases.items()}}
    return _Comm(a.inputs + b.inputs, a.out_shapes + b.out_shapes, a.sems + b.sems, start, finish, aliases)


def _call(body, *, name, grid, in_specs, out_specs, out_shape, args, scratch=(), comm=None, aliases=None):
    n_in, n_out, n_scr, n_axes = len(in_specs), len(out_specs), len(scratch), len(grid)
    params = pltpu.CompilerParams(dimension_semantics=("arbitrary",) * n_axes, vmem_limit_bytes=VMEM_LIMIT_BYTES)
    aliases = dict(aliases or {})
    if comm is None:
        outs = pl.pallas_call(body, name=name, grid=grid, in_specs=list(in_specs), out_specs=list(out_specs),
                              out_shape=list(out_shape), scratch_shapes=list(scratch), input_output_aliases=aliases,
                              compiler_params=params)(*args)
        return list(outs), []
    ci, co = len(comm.inputs), len(comm.out_shapes)

    def wrapped(*refs):
        k_in, c_in = refs[:n_in], refs[n_in:n_in + ci]
        o0 = n_in + ci
        k_out, c_out = refs[o0:o0 + n_out], refs[o0 + n_out:o0 + n_out + co]
        s0 = o0 + n_out + co
        k_scr, c_sem = refs[s0:s0 + n_scr], refs[s0 + n_scr:]
        first = pl.program_id(0) == 0
        last = pl.program_id(0) == grid[0] - 1
        for ax in range(1, n_axes):
            first = jnp.logical_and(first, pl.program_id(ax) == 0)
            last = jnp.logical_and(last, pl.program_id(ax) == grid[ax] - 1)

        @pl.when(first)
        def _():
            comm.start(c_in, c_out, c_sem)

        body(*k_in, *k_out, *k_scr)

        @pl.when(last)
        def _():
            comm.finish(c_in, c_out, c_sem)

    outs = pl.pallas_call(
        wrapped, name=name, grid=grid, in_specs=list(in_specs) + [ANY] * ci, out_specs=list(out_specs) + [ANY] * co,
        out_shape=list(out_shape) + comm.out_shapes, scratch_shapes=list(scratch) + comm.sems,
        input_output_aliases={**aliases, **{n_in + i: n_out + o for i, o in comm.aliases.items()}}, compiler_params=params,
    )(*args, *comm.inputs)
    return list(outs[:n_out]), list(outs[n_out:])


def _run_comm(comm, name):
    ci, co = len(comm.inputs), len(comm.out_shapes)

    def body(*refs):
        c_in, c_out, c_sem = refs[:ci], refs[ci:ci + co], refs[ci + co:]
        comm.start(c_in, c_out, c_sem)
        comm.finish(c_in, c_out, c_sem)

    return list(pl.pallas_call(body, name=name, in_specs=[ANY] * ci, out_specs=[ANY] * co, out_shape=comm.out_shapes,
                               scratch_shapes=comm.sems, input_output_aliases=comm.aliases)(*comm.inputs))


def _mix0_fwd(x, g, w_in, conv_a, ln_g, ln_b, conv_b, w_out, comm=None):
    s = x.shape[0]
    ts = min(TS_MIX, s)
    n = s // ts
    bw = NZ // NCHIP
    offs_a = [HALO_A - (K_A - 1) + k for k in range(K_A)]
    offs_s = [HALO_S - (K_S - 1) + k for k in range(K_S)]

    def body(x_ref, g_ref, win_ref, ca_ref, lg_ref, lb_ref, cb_ref, wout_ref,
             h_ref, z_ref, ac_ref, cat_ref, x1_ref, glu_buf, cv_buf, bconv_buf):
        i = pl.program_id(0)

        @pl.when(i == 0)
        def _():
            glu_buf[0:HALO_A, :] = jnp.zeros((HALO_A, A), F32)
            cv_buf[0:HALO_S, :] = jnp.zeros((HALO_S, A), F32)

        xv = x_ref[...]
        h = (xv * _rsqrt_mean_sq(xv) * g_ref[...]).astype(BF16)
        h_ref[...] = h
        for j in range(NCHIP):
            z_ref[:, j * bw:(j + 1) * bw] = jnp.dot(h, win_ref[j], preferred_element_type=F32)
        glu_buf[HALO_A:HALO_A + ts, :] = z_ref[:, 0:A] * _sigmoid(z_ref[:, A:2 * A])
        cv_buf[HALO_S:HALO_S + ts, :] = z_ref[:, 3 * A:4 * A] * z_ref[:, 4 * A:5 * A]

        def chunk(ci, carry):
            r0 = pl.multiple_of(ci * R_CHUNK, R_CHUNK)
            for c0 in range(0, A, LANES):
                sh = _shifted(glu_buf, r0, R_CHUNK, c0, LANES, offs_a)
                ac_ref[pl.ds(r0, R_CHUNK), c0:c0 + LANES] = _taps(ca_ref, sh, offs_a, c0, LANES)
                sh = _shifted(cv_buf, r0, R_CHUNK, c0, LANES, offs_s)
                bconv_buf[pl.ds(r0, R_CHUNK), c0:c0 + LANES] = _taps(cb_ref, sh, offs_s, c0, LANES)
            return carry

        lax.fori_loop(0, ts // R_CHUNK, chunk, 0)
        glu_buf[0:HALO_A, :] = glu_buf[ts:ts + HALO_A, :]
        cv_buf[0:HALO_S, :] = cv_buf[ts:ts + HALO_S, :]

        ac = ac_ref[...]
        xc = ac - jnp.mean(ac, axis=-1, keepdims=True)
        xn = xc * lax.rsqrt(jnp.mean(xc * xc, axis=-1, keepdims=True) + LN_EPS)
        ln = xn * lg_ref[...] + lb_ref[...]
        cat_ref[:, 0:A] = (ln * _sigmoid(ln)).astype(BF16)
        cat_ref[:, A:2 * A] = (z_ref[:, 2 * A:3 * A] * bconv_buf[...]).astype(BF16)
        x1_ref[...] = xv + jnp.dot(cat_ref[...], wout_ref[...], preferred_element_type=F32)

    tile = lambda w: pl.BlockSpec((ts, w), lambda i: (i, 0))
    return _call(
        body, name="mix0_fwd", grid=(n,), comm=comm, args=(x, g, w_in, conv_a, ln_g, ln_b, conv_b, w_out),
        in_specs=[tile(D), _const((1, D)), _const((NCHIP, D, bw)), _const((K_A, A)), _const((1, A)), _const((1, A)),
                  _const((K_S, A)), _const((2 * A, D))],
        out_specs=[tile(D), tile(NZ), tile(A), tile(2 * A), tile(D)],
        out_shape=[jax.ShapeDtypeStruct((s, D), BF16), jax.ShapeDtypeStruct((s, NZ), F32), jax.ShapeDtypeStruct((s, A), F32),
                   jax.ShapeDtypeStruct((s, 2 * A), BF16), jax.ShapeDtypeStruct((s, D), F32)],
        scratch=[pltpu.VMEM((HALO_A + ts, A), F32), pltpu.VMEM((HALO_S + ts, A), F32), pltpu.VMEM((ts, A), F32)])


def _ffn_fwd(x, g, w_up, wc, w_down, name, comm=None):
    s = x.shape[0]
    ts = min(TS_FFN, s)
    n = s // ts
    bw = FF2 // NCHIP
    rows = 32
    offs = [HALO_S - (K_S - 1) + k for k in range(K_S)]

    def body(x_ref, g_ref, wup_ref, wc_ref, wdn_ref, h_ref, u0_ref, u_ref, act_ref, xo_ref, cbuf):
        i = pl.program_id(0)

        @pl.when(i == 0)
        def _():
            cbuf[0:HALO_S, :] = jnp.zeros((HALO_S, FF2), F32)

        xv = x_ref[...]
        h = (xv * _rsqrt_mean_sq(xv) * g_ref[...]).astype(BF16)
        h_ref[...] = h
        f = None
        for p in range(NCHIP // 2):
            for j in (p, NCHIP // 2 + p):
                zc = jnp.dot(h, wup_ref[j], preferred_element_type=F32)
                u0_ref[:, j * bw:(j + 1) * bw] = zc.astype(BF16)
                cbuf[HALO_S:HALO_S + ts, j * bw:(j + 1) * bw] = zc
            for r0 in range(0, ts, rows):
                for c0 in range(p * bw, (p + 1) * bw, LANES):
                    ug = _taps(wc_ref, _shifted(cbuf, r0, rows, c0, LANES, offs), offs, c0, LANES)
                    uv = _taps(wc_ref, _shifted(cbuf, r0, rows, FF + c0, LANES, offs), offs, FF + c0, LANES)
                    u_ref[r0:r0 + rows, c0:c0 + LANES] = ug
                    u_ref[r0:r0 + rows, FF + c0:FF + c0 + LANES] = uv
                    act_ref[r0:r0 + rows, c0:c0 + LANES] = (ug * _sigmoid(ug) * uv).astype(BF16)
            fp = jnp.dot(act_ref[:, p * bw:(p + 1) * bw], wdn_ref[p * bw:(p + 1) * bw, :], preferred_element_type=F32)
            f = fp if f is None else f + fp
        cbuf[0:HALO_S, :] = cbuf[ts:ts + HALO_S, :]
        xo_ref[...] = xv + f

    tile = lambda w: pl.BlockSpec((ts, w), lambda i: (i, 0))
    return _call(
        body, name=name, grid=(n,), comm=comm, args=(x, g, w_up, wc, w_down),
        in_specs=[tile(D), _const((1, D)), _const((NCHIP, D, bw)), _const((K_S, FF2)), _const((FF, D))],
        out_specs=[tile(D), tile(FF2), tile(FF2), tile(FF), tile(D)],
        out_shape=[jax.ShapeDtypeStruct((s, D), BF16), jax.ShapeDtypeStruct((s, FF2), BF16), jax.ShapeDtypeStruct((s, FF2), F32),
                   jax.ShapeDtypeStruct((s, FF), BF16), jax.ShapeDtypeStruct((s, D), F32)],
        scratch=[pltpu.VMEM((HALO_S + ts, FF2), F32)])


def _pool_windows(hbuf, pbuf, tile_row0, ts):
    def chunk(ci, carry):
        r0 = pl.multiple_of(ci * R_CHUNK, R_CHUNK)
        t1 = (tile_row0 + r0 + lax.broadcasted_iota(jnp.int32, (R_CHUNK, 1), 0) + 1).astype(F32)
        for gi, w in enumerate(POOL_WINDOWS):
            cnt = jnp.minimum(t1, float(w))
            offs = [HALO_P - jj for jj in range(w)]
            for c0 in range(gi * PG, (gi + 1) * PG, LANES):
                sh = _shifted(hbuf, r0, R_CHUNK, c0, LANES, offs)
                tot = sh[offs[0]]
                for o in offs[1:]:
                    tot = tot + sh[o]
                pbuf[pl.ds(r0, R_CHUNK), c0:c0 + LANES] = (tot / cnt - sh[HALO_P]).astype(BF16)
        return carry

    lax.fori_loop(0, ts // R_CHUNK, chunk, 0)


def _assemble_wpool(wp_ref, wps):
    rb = PG // NCHIP
    for gi in range(len(POOL_WINDOWS)):
        for j in range(NCHIP):
            wps[gi, j * rb:(j + 1) * rb, :] = wp_ref[j, gi]


def _pool_fwd(x, g, w_pool, scale):
    s = x.shape[0]
    ts = min(TS_POOL, s)
    n = s // ts
    ng = len(POOL_WINDOWS)

    def body(x_ref, g_ref, wp_ref, sc_ref, xo_ref, hbuf, pbuf, wps):
        i = pl.program_id(0)

        @pl.when(i == 0)
        def _():
            hbuf[0:HALO_P, :] = jnp.zeros((HALO_P, D), F32)
            _assemble_wpool(wp_ref, wps)

        xv = x_ref[...]
        hbuf[HALO_P:HALO_P + ts, :] = xv * _rsqrt_mean_sq(xv) * g_ref[...]
        _pool_windows(hbuf, pbuf, i * ts, ts)
        hbuf[0:HALO_P, :] = hbuf[ts:ts + HALO_P, :]
        for gi in range(ng):
            cols = slice(gi * PG, (gi + 1) * PG)
            y = jnp.dot(pbuf[:, cols], wps[gi], preferred_element_type=F32)
            xo_ref[:, cols] = xv[:, cols] + y * sc_ref[:, cols]

    tile = pl.BlockSpec((ts, D), lambda i: (i, 0))
    return pl.pallas_call(
        body, name="pool_fwd", grid=(n,),
        in_specs=[tile, _const((1, D)), _const((NCHIP, ng, PG // NCHIP, PG)), _const((1, D))],
        out_specs=tile, out_shape=jax.ShapeDtypeStruct((s, D), F32),
        scratch_shapes=[pltpu.VMEM((HALO_P + ts, D), F32), pltpu.VMEM((ts, D), BF16), pltpu.VMEM((ng, PG, PG), BF16)],
        compiler_params=_cparams(1),
    )(x, g, w_pool, scale)


def _final_loss(x, g, target):
    s = x.shape[0]
    ts = min(TS_MM, s)
    n = s // ts

    def body(x_ref, g_ref, t_ref, dx_ref, dg_ref, loss_ref):
        i = pl.program_id(0)

        @pl.when(i == 0)
        def _():
            dg_ref[...] = jnp.zeros((1, D), F32)
            loss_ref[...] = jnp.zeros((1, LANES), F32)

        xv = x_ref[...]
        r = _rsqrt_mean_sq(xv)
        xh = xv * r
        gv = g_ref[...]
        err = xh * gv - t_ref[...]
        sq = jnp.sum(jnp.sum(err * err, axis=1, keepdims=True), axis=0, keepdims=True)
        loss_ref[...] += sq * (0.5 / D)
        dy = err * (1.0 / D)
        dg_ref[...] += jnp.sum(dy * xh, axis=0, keepdims=True)
        dx_ref[...] = _rms_bwd(dy, xh, r, gv)

    tile = pl.BlockSpec((ts, D), lambda i: (i, 0))
    return pl.pallas_call(
        body, name="final_loss", grid=(n,),
        in_specs=[tile, _const((1, D)), tile],
        out_specs=[tile, pl.BlockSpec((1, D), lambda i: (0, 0)), pl.BlockSpec((1, LANES), lambda i: (0, 0))],
        out_shape=[jax.ShapeDtypeStruct((s, D), F32), jax.ShapeDtypeStruct((1, D), F32), jax.ShapeDtypeStruct((1, LANES), F32)],
        compiler_params=_cparams(1),
    )(x, g, target)


def _ffn_bwd_a(dxo, u, u0, wc, w_down, name, comm=None):
    s = dxo.shape[0]
    ts = min(TS_FFN, s)
    n = s // ts
    cw = FF2 // NCHIP
    rows = 32
    lw2 = 2 * LANES
    boffs = [K_S - 1 - k for k in range(K_S)]

    def body(dxo_ref, u_ref, u0_ref, wc_ref, wdn_ref, du0_ref, dwc_ref, dubuf, dact, dwacc):
        i = pl.program_id(0)

        @pl.when(i == 0)
        def _():
            dubuf[ts:ts + HALO_S, :] = jnp.zeros((HALO_S, FF2), F32)
            dwacc[...] = jnp.zeros(dwacc.shape, F32)

        df = dxo_ref[...].astype(BF16)
        for cg in range(0, FF, cw):
            dact[...] = lax.dot_general(df, wdn_ref[cg:cg + cw, :], NT_DIMS, preferred_element_type=F32)

            def chunk(ci, carry, cg=cg):
                rs = pl.ds(pl.multiple_of(ci * rows, rows), rows)
                for c in range(cg, cg + cw, LANES):
                    ug, uv = u_ref[rs, c:c + LANES], u_ref[rs, FF + c:FF + c + LANES]
                    sg = _sigmoid(ug)
                    da = dact[rs, c - cg:c - cg + LANES]
                    gs = ug * sg
                    dubuf[rs, c:c + LANES] = da * uv * (sg + gs * (1.0 - sg))
                    dubuf[rs, FF + c:FF + c + LANES] = da * gs
                return carry

            lax.fori_loop(0, ts // rows, chunk, 0)

        def chunk2(ci, carry):
            r0 = pl.multiple_of(ci * rows, rows)
            rs = pl.ds(r0, rows)
            for c in range(0, FF2, lw2):
                sh = _shifted(dubuf, r0, rows, c, lw2, boffs)
                du0_ref[rs, c:c + lw2] = _taps(wc_ref, sh, boffs, c, lw2).astype(BF16)
                u0v = u0_ref[rs, c:c + lw2].astype(F32)
                for k, o in enumerate(boffs):
                    dwacc[SUBLANES * k:SUBLANES * (k + 1), c:c + lw2] += _rowsum8(sh[o] * u0v)
            return carry

        lax.fori_loop(0, ts // rows, chunk2, 0)
        dubuf[ts:ts + HALO_S, :] = dubuf[0:HALO_S, :]

        @pl.when(i == n - 1)
        def _():
            _finish_tap_sums(dwacc, dwc_ref, K_S)

    rev = lambda w: pl.BlockSpec((ts, w), lambda i: (n - 1 - i, 0))
    return _call(
        body, name=name, grid=(n,), comm=comm, args=(dxo, u, u0, wc, w_down),
        in_specs=[rev(D), rev(FF2), rev(FF2), _const((K_S, FF2)), _const((FF, D))],
        out_specs=[rev(FF2), pl.BlockSpec((K_S, FF2), lambda i: (0, 0))],
        out_shape=[jax.ShapeDtypeStruct((s, FF2), BF16), jax.ShapeDtypeStruct((K_S, FF2), F32)],
        scratch=[pltpu.VMEM((ts + HALO_S, FF2), F32), pltpu.VMEM((ts, cw), F32), pltpu.VMEM((SUBLANES * K_S, FF2), F32)])


TS_FFNB = 256
N_SLICE = 4


def _ffn_bwd_half(dxo, u, u0, wc, w_up, w_down, p, name, first=None, x=None, g=None, comm=None):
    s = dxo.shape[0]
    ts = min(TS_FFNB, s)
    n = s // ts
    bw = FF2 // NCHIP
    rows = ts // N_SLICE
    dsl = D // N_SLICE
    boffs = [K_S - 1 - k for k in range(K_S)]
    second = first is not None
    n_in = 9 + (5 if second else 0)

    def body(*refs):
        dxo_ref, ug_ref, uv_ref, u0g_ref, u0v_ref, wcg_ref, wcv_ref, wup_ref, wdn_ref = refs[:9]
        if second:
            dh0_ref, x_ref, dxp_ref, g_ref = refs[9:13]
        outs = refs[n_in:]
        dx_ref = outs[0]
        dg_ref = outs[1] if second else None
        dwcg_ref, dwcv_ref, du0_ref = outs[1 + second:4 + second]
        dubuf, dact, du0_s, dh_s, dwacc = outs[4 + second:]
        u0_refs, wc_refs = (u0g_ref, u0v_ref), (wcg_ref, wcv_ref)
        i = pl.program_id(0)
        live = (i < n).astype(F32)

        @pl.when(i == 0)
        def _():
            dubuf[ts:ts + HALO_S, :] = jnp.zeros((HALO_S, 2 * bw), F32)
            dwacc[...] = jnp.zeros(dwacc.shape, F32)
            du0_s[...] = jnp.zeros(du0_s.shape, BF16)
            if second:
                dg_ref[...] = jnp.zeros((1, D), F32)

        dact[...] = lax.dot_general(dxo_ref[...].astype(BF16), wdn_ref[...], NT_DIMS, preferred_element_type=F32)

        def chunk(ci, carry):
            rs = pl.ds(pl.multiple_of(ci * rows, rows), rows)
            for c in range(0, bw, LANES):
                cs = slice(c, c + LANES)
                ug, uv = ug_ref[rs, cs], uv_ref[rs, cs]
                sg = _sigmoid(ug)
                gs = ug * sg
                da = dact[rs, cs]
                dubuf[rs, cs] = da * uv * (sg + gs * (1.0 - sg))
                dubuf[rs, bw + c:bw + c + LANES] = da * gs
            return carry

        lax.fori_loop(0, N_SLICE, chunk, 0)

        def chunk2(ci, carry):
            r0 = pl.multiple_of(ci * rows, rows)
            rs = pl.ds(r0, rows)
            for part in range(2):
                for c in range(0, bw, LANES):
                    cs = slice(c, c + LANES)
                    sh = _shifted(dubuf, r0, rows, part * bw + c, LANES, boffs)
                    du0_ref[part, rs, cs] = _taps(wc_refs[part], sh, boffs, c, LANES).astype(BF16)
                    u0v = u0_refs[part][rs, cs].astype(F32) * live
                    for k, o in enumerate(boffs):
                        dwacc[SUBLANES * k:SUBLANES * (k + 1), part * bw + c:part * bw + c + LANES] += _rowsum8(sh[o] * u0v)
            ds_ = pl.ds(pl.multiple_of(ci * dsl, dsl), dsl)
            dh_s[ci] = (lax.dot_general(du0_s[0], wup_ref[0, ds_, :], NT_DIMS, preferred_element_type=F32)
                        + lax.dot_general(du0_s[1], wup_ref[1, ds_, :], NT_DIMS, preferred_element_type=F32))
            return carry

        lax.fori_loop(0, N_SLICE, chunk2, 0)
        dubuf[ts:ts + HALO_S, :] = dubuf[0:HALO_S, :]
        du0_s[...] = du0_ref[...]

        @pl.when(i >= 1)
        def _():
            for q in range(N_SLICE):
                qs = slice(q * dsl, (q + 1) * dsl)
                if second:
                    dx_ref[:, qs] = dh_s[q] + dh0_ref[:, qs]
                else:
                    dx_ref[:, qs] = dh_s[q]
            if second:
                dh = dx_ref[...]
                xv = x_ref[...]
                r = _rsqrt_mean_sq(xv)
                xh = xv * r
                dg_ref[...] += jnp.sum(dh * xh, axis=0, keepdims=True)
                dx_ref[...] = dxp_ref[...] + _rms_bwd(dh, xh, r, g_ref[...])

        @pl.when(i == n)
        def _():
            for k in range(K_S):
                ks = slice(SUBLANES * k, SUBLANES * (k + 1))
                dwcg_ref[k:k + 1, :] = jnp.sum(dwacc[ks, 0:bw], axis=0, keepdims=True)
                dwcv_ref[k:k + 1, :] = jnp.sum(dwacc[ks, bw:2 * bw], axis=0, keepdims=True)

    half2 = NCHIP // 2
    cur_t = lambda i: jnp.maximum(n - 1 - i, 0)
    prv_t = lambda i: jnp.minimum(n - i, n - 1)
    tile = lambda w, t, cb=0: pl.BlockSpec((ts, w), lambda i: (t(i), cb))
    one = pl.Buffered(1)
    in_specs = [tile(D, cur_t), tile(bw, cur_t, p), tile(bw, cur_t, half2 + p), tile(bw, cur_t, p), tile(bw, cur_t, half2 + p),
                pl.BlockSpec((K_S, bw), lambda i: (0, p), pipeline_mode=one),
                pl.BlockSpec((K_S, bw), lambda i: (0, half2 + p), pipeline_mode=one),
                pl.BlockSpec((half2, None, D, bw), lambda i: (0, p, 0, 0), pipeline_mode=one),
                pl.BlockSpec((bw, D), lambda i: (p, 0), pipeline_mode=one)]
    args = [dxo, u, u, u0, u0, wc, wc, w_up.reshape(half2, half2, D, bw), w_down]
    vec = pl.BlockSpec((1, D), lambda i: (0, 0))
    small = pl.BlockSpec((K_S, bw), lambda i: (0, 0))
    out_specs = [tile(D, prv_t)] + [vec] * second + [small, small, pl.BlockSpec((2, ts, bw), lambda i: (0, n - 1 - i + (i // n) * (n + 1), p))]
    out_shape = [jax.ShapeDtypeStruct((s, D), F32)] + [jax.ShapeDtypeStruct((1, D), F32)] * second + [
        jax.ShapeDtypeStruct((K_S, bw), F32), jax.ShapeDtypeStruct((K_S, bw), F32), jax.ShapeDtypeStruct((2, s + ts, FF), BF16)]
    aliases = {}
    if second:
        dh0, _, _, du0_first = first
        in_specs += [tile(D, prv_t), tile(D, prv_t), tile(D, prv_t), _const((1, D)), ANY]
        args += [dh0, x, dxo, g, du0_first]
        aliases = {13: 4}
    return _call(
        body, name=name, grid=(n + 1,), comm=comm, args=tuple(args), in_specs=in_specs, out_specs=out_specs,
        out_shape=out_shape, aliases=aliases,
        scratch=[pltpu.VMEM((ts + HALO_S, 2 * bw), F32), pltpu.VMEM((ts, bw), F32), pltpu.VMEM((2, ts, bw), BF16),
                 pltpu.VMEM((N_SLICE, ts, dsl), F32), pltpu.VMEM((SUBLANES * K_S, 2 * bw), F32)])


def _nt_rms_bwd(dy, w, x, g, dres, name, comm=None):
    s = x.shape[0]
    ts = min(TS_MM, s)
    n = s // ts
    nw = dy.shape[1]
    bw = nw // NCHIP

    def body(dy_ref, w_ref, x_ref, g_ref, dres_ref, dx_ref, dg_ref):
        i = pl.program_id(0)

        @pl.when(i == 0)
        def _():
            dg_ref[...] = jnp.zeros((1, D), F32)

        dh = lax.dot_general(dy_ref[:, 0:bw], w_ref[0], NT_DIMS, preferred_element_type=F32)
        for j in range(1, NCHIP):
            dh = dh + lax.dot_general(dy_ref[:, j * bw:(j + 1) * bw], w_ref[j], NT_DIMS, preferred_element_type=F32)
        xv = x_ref[...]
        r = _rsqrt_mean_sq(xv)
        xh = xv * r
        dg_ref[...] += jnp.sum(dh * xh, axis=0, keepdims=True)
        dx_ref[...] = dres_ref[...] + _rms_bwd(dh, xh, r, g_ref[...])

    tile = lambda wd: pl.BlockSpec((ts, wd), lambda i: (i, 0))
    return _call(
        body, name=name, grid=(n,), comm=comm, args=(dy, w, x, g, dres),
        in_specs=[tile(nw), _const((NCHIP, D, bw)), tile(D), _const((1, D)), tile(D)],
        out_specs=[tile(D), pl.BlockSpec((1, D), lambda i: (0, 0))],
        out_shape=[jax.ShapeDtypeStruct((s, D), F32), jax.ShapeDtypeStruct((1, D), F32)])


def _wgrad(a, b, n_blocks, name, comm=None):
    s, m = a.shape
    bw = b.shape[1] // n_blocks
    tk = min(TS_MM, s)

    def body(a_ref, b_ref, o_ref):
        @pl.when(pl.program_id(0) == 0)
        def _():
            o_ref[...] = jnp.zeros(o_ref.shape, F32)

        for j in range(n_blocks):
            o_ref[j] += lax.dot_general(a_ref[...], b_ref[:, j * bw:(j + 1) * bw].astype(BF16), TN_DIMS,
                                        preferred_element_type=F32)

    (out,), comm_out = _call(
        body, name=name, grid=(s // tk,), comm=comm, args=(a, b),
        in_specs=[pl.BlockSpec((tk, m), lambda k: (k, 0)), pl.BlockSpec((tk, b.shape[1]), lambda k: (k, 0))],
        out_specs=[pl.BlockSpec((n_blocks, m, bw), lambda k: (0, 0, 0))],
        out_shape=[jax.ShapeDtypeStruct((n_blocks, m, bw), F32)])
    return out, comm_out


def _wgrad_up(h, du0, name, comm=None):
    s, m = h.shape
    bw = FF2 // NCHIP
    half2 = NCHIP // 2
    tk = min(TS_MM, s)

    def body(a_ref, b_ref, o_ref):
        @pl.when(pl.program_id(0) == 0)
        def _():
            o_ref[...] = jnp.zeros(o_ref.shape, F32)

        for j in range(NCHIP):
            o_ref[j] += lax.dot_general(a_ref[...], b_ref[j // half2, :, (j % half2) * bw:(j % half2 + 1) * bw], TN_DIMS,
                                        preferred_element_type=F32)

    (out,), comm_out = _call(
        body, name=name, grid=(s // tk,), comm=comm, args=(h, du0),
        in_specs=[pl.BlockSpec((tk, m), lambda k: (k, 0)), pl.BlockSpec((2, tk, FF), lambda k: (0, k, 0))],
        out_specs=[pl.BlockSpec((NCHIP, m, bw), lambda k: (0, 0, 0))], out_shape=[jax.ShapeDtypeStruct((NCHIP, m, bw), F32)])
    return out, comm_out


def _pool_bwd(dxo, x, g, w_pool, scale, comm=None):
    s = x.shape[0]
    ts = min(TS_POOL, s)
    n = s // ts
    ng = len(POOL_WINDOWS)
    rb = PG // NCHIP

    def body(dxo_ref, x_ref, halo_ref, g_ref, wp_ref, sc_ref, dx_ref, dwp_ref, dsc_ref, dg_ref,
             hbuf, pbuf, qbuf, dhbuf, wps, dwacc):
        i = pl.program_id(0)
        j = n - 1 - i

        @pl.when(i == 0)
        def _():
            qbuf[ts:ts + HALO_P, :] = jnp.zeros((HALO_P, D), F32)
            dwacc[...] = jnp.zeros(dwacc.shape, F32)
            dsc_ref[...] = jnp.zeros((1, D), F32)
            dg_ref[...] = jnp.zeros((1, D), F32)
            _assemble_wpool(wp_ref, wps)

        gv = g_ref[...]
        xl = halo_ref[...]
        hbuf[0:HALO_P, :] = jnp.where(j == 0, 0.0, xl * _rsqrt_mean_sq(xl) * gv)
        xv = x_ref[...]
        r = _rsqrt_mean_sq(xv)
        xh = xv * r
        hbuf[HALO_P:HALO_P + ts, :] = xh * gv
        _pool_windows(hbuf, pbuf, j * ts, ts)

        dy = dxo_ref[...]
        t1 = (j * ts + lax.broadcasted_iota(jnp.int32, (ts, 1), 0) + 1).astype(F32)
        for gi, w in enumerate(POOL_WINDOWS):
            cols = slice(gi * PG, (gi + 1) * PG)
            p = pbuf[:, cols]
            y = jnp.dot(p, wps[gi], preferred_element_type=F32)
            dsc_ref[:, cols] += jnp.sum(dy[:, cols] * y, axis=0, keepdims=True)
            dq = (dy[:, cols] * sc_ref[:, cols]).astype(BF16)
            dwacc[gi] += lax.dot_general(p, dq, TN_DIMS, preferred_element_type=F32)
            dp = lax.dot_general(dq, wps[gi], NT_DIMS, preferred_element_type=F32)
            qbuf[0:ts, cols] = dp / jnp.minimum(t1, float(w))

        def chunk(ci, carry):
            r0 = pl.multiple_of(ci * R_CHUNK, R_CHUNK)
            tc = (j * ts + r0 + lax.broadcasted_iota(jnp.int32, (R_CHUNK, 1), 0) + 1).astype(F32)
            for gi, w in enumerate(POOL_WINDOWS):
                cnt = jnp.minimum(tc, float(w))
                offs = list(range(w))
                for c0 in range(gi * PG, (gi + 1) * PG, LANES):
                    sh = _shifted(qbuf, r0, R_CHUNK, c0, LANES, offs)
                    tot = sh[0]
                    for o in offs[1:]:
                        tot = tot + sh[o]
                    dhbuf[pl.ds(r0, R_CHUNK), c0:c0 + LANES] = tot - sh[0] * cnt
            return carry

        lax.fori_loop(0, ts // R_CHUNK, chunk, 0)
        qbuf[ts:ts + HALO_P, :] = qbuf[0:HALO_P, :]
        dh = dhbuf[...]
        dg_ref[...] += jnp.sum(dh * xh, axis=0, keepdims=True)
        dx_ref[...] = dy + _rms_bwd(dh, xh, r, gv)

        @pl.when(i == n - 1)
        def _():
            for gi in range(ng):
                for jj in range(NCHIP):
                    dwp_ref[jj, gi] = dwacc[gi, jj * rb:(jj + 1) * rb, :]

    rev = pl.BlockSpec((ts, D), lambda i: (n - 1 - i, 0))
    halo = pl.BlockSpec((HALO_P, D), lambda i: (jnp.maximum((n - 1 - i) * (ts // HALO_P) - 1, 0), 0))
    vec = pl.BlockSpec((1, D), lambda i: (0, 0))
    return _call(
        body, name="pool_bwd", grid=(n,), comm=comm, args=(dxo, x, x, g, w_pool, scale),
        in_specs=[rev, rev, halo, _const((1, D)), _const((NCHIP, ng, rb, PG)), _const((1, D))],
        out_specs=[rev, pl.BlockSpec((NCHIP, ng, rb, PG), lambda i: (0, 0, 0, 0)), vec, vec],
        out_shape=[jax.ShapeDtypeStruct((s, D), F32), jax.ShapeDtypeStruct((NCHIP, ng, rb, PG), F32),
                   jax.ShapeDtypeStruct((1, D), F32), jax.ShapeDtypeStruct((1, D), F32)],
        scratch=[pltpu.VMEM((HALO_P + ts, D), F32), pltpu.VMEM((ts, D), BF16), pltpu.VMEM((ts + HALO_P, D), F32),
                 pltpu.VMEM((ts, D), F32), pltpu.VMEM((ng, PG, PG), BF16), pltpu.VMEM((ng, PG, PG), F32)])


def _mix0_bwd_a(dx1, z, ac, w_out, conv_a, ln_g, ln_b, conv_b, comm=None):
    s = dx1.shape[0]
    ts = min(TS_MIXB, s)
    n = s // ts
    rows = 32
    offs_a = [HALO_A - (K_A - 1) + k for k in range(K_A)]
    offs_s = [HALO_S - (K_S - 1) + k for k in range(K_S)]
    boffs_a = [K_A - 1 - k for k in range(K_A)]
    boffs_s = [K_S - 1 - k for k in range(K_S)]

    def body(dx1_ref, z_ref, zh_ref, ac_ref, wout_ref, ca_ref, lg_ref, lb_ref, cb_ref,
             dz_ref, dca_ref, dlg_ref, dlb_ref, dcb_ref,
             glu_buf, cv_buf, dac_buf, dbc_buf, db_buf, dca_acc, dcb_acc):
        i = pl.program_id(0)
        j = n - 1 - i

        @pl.when(i == 0)
        def _():
            dac_buf[ts:ts + HALO_A, :] = jnp.zeros((HALO_A, A), F32)
            dbc_buf[ts:ts + HALO_S, :] = jnp.zeros((HALO_S, A), F32)
            dca_acc[...] = jnp.zeros(dca_acc.shape, F32)
            dcb_acc[...] = jnp.zeros(dcb_acc.shape, F32)
            dlg_ref[...] = jnp.zeros((1, A), F32)
            dlb_ref[...] = jnp.zeros((1, A), F32)

        glu_h = zh_ref[:, 0:A] * _sigmoid(zh_ref[:, A:2 * A])
        glu_buf[0:HALO_A, :] = jnp.where(j == 0, 0.0, glu_h)
        hs = slice(HALO_A - HALO_S, HALO_A)
        cv_buf[0:HALO_S, :] = jnp.where(j == 0, 0.0, zh_ref[hs, 3 * A:4 * A] * zh_ref[hs, 4 * A:5 * A])
        glu_buf[HALO_A:HALO_A + ts, :] = z_ref[:, 0:A] * _sigmoid(z_ref[:, A:2 * A])
        cv_buf[HALO_S:HALO_S + ts, :] = z_ref[:, 3 * A:4 * A] * z_ref[:, 4 * A:5 * A]

        dcat = lax.dot_general(dx1_ref[...].astype(BF16), wout_ref[...], NT_DIMS, preferred_element_type=F32)
        db_buf[...] = dcat[:, A:2 * A]
        ac = ac_ref[...]
        xc = ac - jnp.mean(ac, axis=-1, keepdims=True)
        rstd = lax.rsqrt(jnp.mean(xc * xc, axis=-1, keepdims=True) + LN_EPS)
        xn = xc * rstd
        lg = lg_ref[...]
        ln = xn * lg + lb_ref[...]
        sl = _sigmoid(ln)
        dln = dcat[:, 0:A] * sl * (1.0 + ln * (1.0 - sl))
        dlg_ref[...] += jnp.sum(dln * xn, axis=0, keepdims=True)
        dlb_ref[...] += jnp.sum(dln, axis=0, keepdims=True)
        dxn = dln * lg
        dac_buf[0:ts, :] = rstd * (dxn - jnp.mean(dxn, axis=-1, keepdims=True)
                                   - xn * jnp.mean(dxn * xn, axis=-1, keepdims=True))

        def chunk(ci, carry):
            r0 = pl.multiple_of(ci * rows, rows)
            rs = pl.ds(r0, rows)
            for c0 in range(0, A, LANES):
                cs = slice(c0, c0 + LANES)
                shc = _shifted(cv_buf, r0, rows, c0, LANES, offs_s)
                bconv = _taps(cb_ref, shc, offs_s, c0, LANES)
                dbv = db_buf[rs, cs]
                dz_ref[rs, 2 * A + c0:2 * A + c0 + LANES] = (dbv * bconv).astype(BF16)
                dbconv = dbv * z_ref[rs, 2 * A + c0:2 * A + c0 + LANES]
                dbc_buf[rs, cs] = dbconv
                for k, o in enumerate(offs_s):
                    dcb_acc[SUBLANES * k:SUBLANES * (k + 1), cs] += _rowsum8(dbconv * shc[o])
                shg = _shifted(glu_buf, r0, rows, c0, LANES, offs_a)
                dacv = dac_buf[rs, cs]
                for k, o in enumerate(offs_a):
                    dca_acc[SUBLANES * k:SUBLANES * (k + 1), cs] += _rowsum8(dacv * shg[o])
            return carry

        lax.fori_loop(0, ts // rows, chunk, 0)

        def chunk2(ci, carry):
            r0 = pl.multiple_of(ci * rows, rows)
            rs = pl.ds(r0, rows)
            for c0 in range(0, A, LANES):
                col = lambda grp: slice(grp * A + c0, grp * A + c0 + LANES)
                dglu = _taps(ca_ref, _shifted(dac_buf, r0, rows, c0, LANES, boffs_a), boffs_a, c0, LANES)
                sg = _sigmoid(z_ref[rs, col(1)])
                dz_ref[rs, col(0)] = (dglu * sg).astype(BF16)
                dz_ref[rs, col(1)] = (dglu * z_ref[rs, col(0)] * sg * (1.0 - sg)).astype(BF16)
                dcv = _taps(cb_ref, _shifted(dbc_buf, r0, rows, c0, LANES, boffs_s), boffs_s, c0, LANES)
                dz_ref[rs, col(3)] = (dcv * z_ref[rs, col(4)]).astype(BF16)
                dz_ref[rs, col(4)] = (dcv * z_ref[rs, col(3)]).astype(BF16)
            return carry

        lax.fori_loop(0, ts // rows, chunk2, 0)
        dac_buf[ts:ts + HALO_A, :] = dac_buf[0:HALO_A, :]
        dbc_buf[ts:ts + HALO_S, :] = dbc_buf[0:HALO_S, :]

        @pl.when(i == n - 1)
        def _():
            _finish_tap_sums(dca_acc, dca_ref, K_A)
            _finish_tap_sums(dcb_acc, dcb_ref, K_S)

    rev = lambda w: pl.BlockSpec((ts, w), lambda i: (n - 1 - i, 0))
    halo = pl.BlockSpec((HALO_A, NZ), lambda i: (jnp.maximum((n - 1 - i) * (ts // HALO_A) - 1, 0), 0))
    full = lambda r, c: pl.BlockSpec((r, c), lambda i: (0, 0))
    return _call(
        body, name="mix0_bwd_a", grid=(n,), comm=comm, args=(dx1, z, z, ac, w_out, conv_a, ln_g, ln_b, conv_b),
        in_specs=[rev(D), rev(NZ), halo, rev(A), _const((2 * A, D)), _const((K_A, A)), _const((1, A)), _const((1, A)),
                  _const((K_S, A))],
        out_specs=[rev(NZ), full(K_A, A), full(1, A), full(1, A), full(K_S, A)],
        out_shape=[jax.ShapeDtypeStruct((s, NZ), BF16), jax.ShapeDtypeStruct((K_A, A), F32), jax.ShapeDtypeStruct((1, A), F32),
                   jax.ShapeDtypeStruct((1, A), F32), jax.ShapeDtypeStruct((K_S, A), F32)],
        scratch=[pltpu.VMEM((HALO_A + ts, A), F32), pltpu.VMEM((HALO_S + ts, A), F32), pltpu.VMEM((ts + HALO_A, A), F32),
                 pltpu.VMEM((ts + HALO_S, A), F32), pltpu.VMEM((ts, A), F32), pltpu.VMEM((SUBLANES * K_A, A), F32),
                 pltpu.VMEM((SUBLANES * K_S, A), F32)])


def _train_step(x, target, place, shard, rep, small_pack, small_shapes):
    bw_up, bw_in = FF2 // NCHIP, NZ // NCHIP
    five = lambda a, k: a.reshape(NCHIP, 2, *HALF[k])

    g_in, g_out, small_g = _run_comm(_gather_comm([shard["w_in"], shard["w_out"]], small_pack), "ag_first")
    whole = {}
    for k, part in zip(SMALL_SHARDED, _unpack(small_g, small_shapes, lead=(NCHIP,))):
        whole[k] = jnp.moveaxis(part, 0, 1).reshape(part.shape[1], NCHIP * part.shape[2])
    w_in, w_out = g_in.reshape(NCHIP, D, bw_in), g_out.reshape(2 * A, D)
    conv_ffn = whole["conv_ffn_w"].reshape(2, K_S, FF2)
    nffn = [rep["norm_ffn"][0:1], rep["norm_ffn"][1:2]]

    (h0, z, ac, cat, x1), (g_up0, g_dn0) = _mix0_fwd(
        x, rep["norm_mix_even"], w_in, whole["conv_a"], rep["ln_a_g"], rep["ln_a_b"], whole["conv_b"], w_out,
        comm=_gather_comm([shard["w_up0"], shard["w_down0"]]))
    w_up0, w_dn0 = g_up0.reshape(NCHIP, D, bw_up), g_dn0.reshape(FF, D)
    (hf0, u00, u0, act0, x2), (g_pool, g_up1, g_dn1) = _ffn_fwd(
        x1, nffn[0], w_up0, conv_ffn[0], w_dn0, "ffn0_fwd",
        comm=_gather_comm([shard["w_pool"], shard["w_up1"], shard["w_down1"]]))
    w_pool = g_pool.reshape(NCHIP, len(POOL_WINDOWS), PG // NCHIP, PG)
    w_up1, w_dn1 = g_up1.reshape(NCHIP, D, bw_up), g_dn1.reshape(FF, D)
    x3 = _pool_fwd(x2, whole["norm_mix_odd"], w_pool, whole["pool_scale"])
    (hf1, u01, u1, act1, x4), _ = _ffn_fwd(x3, nffn[1], w_up1, conv_ffn[1], w_dn1, "ffn1_fwd")
    dx4, g_nfin, loss_part = _final_loss(x4, rep["norm_final"], target)

    psum = lambda k, g, ld, tag: _pair_sum(place, g, ld, "pair_sum_" + k + tag)

    gr_dn1 = five(_wgrad(act1, dx4, 1, "wgrad_down1")[0], "w_down")
    first1, (ld_dn1,) = _ffn_bwd_half(dx4, u1, u01, conv_ffn[1], w_up1, w_dn1, 0, "ffn1_bwd_p0", comm=_pair_comm([gr_dn1]))
    s_dn1 = psum("w_down", gr_dn1, ld_dn1, "1")
    (dx3, g_nf1, cg1, cv1, du01), (p_dn1,) = _ffn_bwd_half(
        dx4, u1, u01, conv_ffn[1], w_up1, w_dn1, 1, "ffn1_bwd_p1", first=first1, x=x3, g=nffn[1], comm=_chips_comm([s_dn1]))
    g_wc1 = jnp.concatenate([first1[1], cg1, first1[2], cv1], axis=1)
    gr_up1 = five(_wgrad_up(hf1, du01, "wgrad_up1")[0], "w_up")
    (dx2, g_wpool, g_scale, g_nmo), (ld_up1,) = _pool_bwd(dx3, x2, whole["norm_mix_odd"], w_pool, whole["pool_scale"],
                                                          comm=_pair_comm([gr_up1]))
    s_up1 = psum("w_up", gr_up1, ld_up1, "1")
    gr_dn0, gr_pool = five(_wgrad(act0, dx2, 1, "wgrad_down0")[0], "w_down"), five(g_wpool, "w_pool")
    first0, (p_up1, ld_dn0, ld_pool) = _ffn_bwd_half(
        dx2, u0, u00, conv_ffn[0], w_up0, w_dn0, 0, "ffn0_bwd_p0",
        comm=_join_comm(_chips_comm([s_up1]), _pair_comm([gr_dn0, gr_pool])))
    s_dn0, s_pool = psum("w_down", gr_dn0, ld_dn0, "0"), psum("w_pool", gr_pool, ld_pool, "0")
    (dx1, g_nf0, cg0, cv0, du00), (p_dn0, p_pool) = _ffn_bwd_half(
        dx2, u0, u00, conv_ffn[0], w_up0, w_dn0, 1, "ffn0_bwd_p1", first=first0, x=x1, g=nffn[0],
        comm=_chips_comm([s_dn0, s_pool]))
    g_wc0 = jnp.concatenate([first0[1], cg0, first0[2], cv0], axis=1)
    gr_up0 = five(_wgrad_up(hf0, du00, "wgrad_up0")[0], "w_up")
    (dz, g_ca, g_lg, g_lb, g_cb), (ld_up0,) = _mix0_bwd_a(
        dx1, z, ac, w_out, whole["conv_a"], rep["ln_a_g"], rep["ln_a_b"], whole["conv_b"], comm=_pair_comm([gr_up0]))
    s_up0 = psum("w_up", gr_up0, ld_up0, "0")
    gr_out, _ = _wgrad(cat, dx1, 1, "wgrad_out")
    gr_in, (p_up0,) = _wgrad(h0, dz, NCHIP, "wgrad_in", comm=_chips_comm([s_up0]))
    gradsm = (five(gr_in, "w_in"), five(gr_out, "w_out"))
    (grad_x, g_nme), landedm = _nt_rms_bwd(dz, w_in, x, rep["norm_mix_even"], dx1, "mix0_bwd_b", comm=_pair_comm(gradsm))
    sumsm = [psum("w_in", gradsm[0], landedm[0], ""), psum("w_out", gradsm[1], landedm[1], "")]

    small = {"norm_mix_even": g_nme, "conv_a": g_ca, "ln_a_g": g_lg, "ln_a_b": g_lb, "conv_b": g_cb, "norm_mix_odd": g_nmo,
             "pool_scale": g_scale, "norm_ffn": jnp.concatenate([g_nf0, g_nf1], axis=0),
             "conv_ffn_w": jnp.stack([g_wc0, g_wc1]), "norm_final": g_nfin}
    *partsm, small_all = _run_comm(_chips_comm(sumsm, _pack([small[k] for k in SMALL_ALL] + [loss_part])), "rs_last")

    tot = lambda k, g, ld, p, layer=0, nl=1, prev=None: _chip_sum(place, g, ld, p, layer, nl, "chip_sum_%s%d" % (k, layer), prev)
    t_in = tot("w_in", gradsm[0], landedm[0], partsm[0])
    t_out = tot("w_out", gradsm[1], landedm[1], partsm[1])
    t_pool = tot("w_pool", gr_pool, ld_pool, p_pool)
    t_up = tot("w_up", gr_up1, ld_up1, p_up1, 1, 2)
    t_up = tot("w_up", gr_up0, ld_up0, p_up0, 0, 2, t_up)
    t_dn = tot("w_down", gr_dn1, ld_dn1, p_dn1, 1, 2)
    t_dn = tot("w_down", gr_dn0, ld_dn0, p_dn0, 0, 2, t_dn)
    swapped = _run_comm(_swap_comm([t_in, t_out, t_pool, t_up, t_dn]), "rs_swap")
    return grad_x, dict(zip(BIG, swapped)), _sum_devices(small_all), [small[k].shape for k in SMALL_ALL] + [(1, LANES)]


def _adamw_math(w, g, m, v):
    m = ADAM_B1 * m + (1.0 - ADAM_B1) * g
    v = ADAM_B2 * v + (1.0 - ADAM_B2) * (g * g)
    m_hat = m / (1.0 - ADAM_B1 ** ADAM_STEP)
    v_hat = v / (1.0 - ADAM_B2 ** ADAM_STEP)
    return -ADAM_LR * (m_hat / (jnp.sqrt(v_hat) + ADAM_EPS) + ADAM_WD * w), m, v


def _adamw_big(w, g, m, v, tr, name):
    nl, rows, cols = w.shape

    def body(w_ref, g_ref, m_ref, v_ref, g2_ref, d_ref, m2_ref, v2_ref):
        gv = g_ref[...]
        g2_ref[...] = gv
        d_ref[...], m2_ref[...], v2_ref[...] = _adamw_math(w_ref[...], gv, m_ref[...], v_ref[...])

    spec = pl.BlockSpec((None, tr, cols), lambda l, r: (l, r, 0))
    return pl.pallas_call(
        body, name=name, grid=(nl, rows // tr), in_specs=[spec] * 4, out_specs=[spec] * 4,
        out_shape=[jax.ShapeDtypeStruct(w.shape, F32)] * 4, compiler_params=_cparams(2),
    )(w, g, m, v)


def _adamw_small(ws, gs, ms, vs):
    n = len(ws)

    def body(*refs):
        for p in range(n):
            w_ref, g_ref, m_ref, v_ref = (refs[q * n + p] for q in range(4))
            d_ref, m2_ref, v2_ref = (refs[(4 + q) * n + p] for q in range(3))
            d_ref[...], m2_ref[...], v2_ref[...] = _adamw_math(w_ref[...], g_ref[...], m_ref[...], v_ref[...])

    whole = lambda a: pl.BlockSpec(a.shape, lambda: (0,) * a.ndim)
    outs = pl.pallas_call(
        body, name="adamw_small", in_specs=[whole(a) for a in ws] * 4, out_specs=[whole(a) for a in ws] * 3,
        out_shape=[jax.ShapeDtypeStruct(a.shape, F32) for a in ws] * 3,
        compiler_params=pltpu.CompilerParams(vmem_limit_bytes=VMEM_LIMIT_BYTES),
    )(*ws, *gs, *ms, *vs)
    return outs[0:n], outs[n:2 * n], outs[2 * n:3 * n]


def _place():
    x, y, c = lax.axis_index("x"), lax.axis_index("y"), lax.axis_index("c")
    chips = [(x, 1 - y), (1 - x, y), (1 - x, 1 - y)]
    blocks = [2 * cx + cy for cx, cy in chips]
    return x, y, c, 2 * x + y, chips, blocks


def _gather_comm(shards, small=None):
    na = len(shards)
    nk = NCHIP - 1
    ns = 0 if small is None else 1

    def copies(ins, outs, sems):
        ici_send, ici_recv, fwd_send, fwd_recv, own_send, own_recv = sems[:6]
        x, y, c, j, chips, blocks = _place()
        sib = (x, y, 1 - c)

        def ici(a, k, arrival):
            dst = outs[a].at[blocks[k], c] if arrival else outs[a].at[j, c]
            return pltpu.make_async_remote_copy(
                src_ref=dst if arrival else ins[a].at[c], dst_ref=dst, send_sem=ici_send.at[a * nk + k],
                recv_sem=ici_recv.at[a * nk + k], device_id=(*chips[k], c), device_id_type=MESH)

        def fwd(a, k, half):
            ref = outs[a].at[blocks[k], half]
            return pltpu.make_async_remote_copy(
                src_ref=ref, dst_ref=ref, send_sem=fwd_send.at[a * nk + k], recv_sem=fwd_recv.at[a * nk + k],
                device_id=sib, device_id_type=MESH)

        own = [pltpu.make_async_remote_copy(src_ref=ins[a], dst_ref=outs[a].at[j], send_sem=own_send.at[a],
                                            recv_sem=own_recv.at[a], device_id=sib, device_id_type=MESH) for a in range(na)]
        small_copies = [pltpu.make_async_remote_copy(
            src_ref=ins[na], dst_ref=outs[na].at[j], send_sem=ici_send.at[na * nk + k], recv_sem=ici_recv.at[na * nk + k],
            device_id=(*chips[k], c), device_id_type=MESH) for k in range(nk * ns)]
        local = [pltpu.make_async_copy(ins[na], outs[na].at[j], sems[6])] if ns else []
        return ici, fwd, own, small_copies, local, c

    def start(ins, outs, sems):
        ici, _, own, small_copies, local, _ = copies(ins, outs, sems)
        for cp in local + own + [ici(a, k, False) for a in range(na) for k in range(nk)] + small_copies:
            cp.start()

    def finish(ins, outs, sems):
        ici, fwd, own, small_copies, local, c = copies(ins, outs, sems)
        for a in range(na):
            for k in range(nk):
                ici(a, k, True).wait_recv()
                fwd(a, k, c).start()
        for cp in small_copies:
            cp.wait()
        for a in range(na):
            for k in range(nk):
                ici(a, k, False).wait_send()
                fwd(a, k, c).wait_send()
                fwd(a, k, 1 - c).wait_recv()
        for cp in own + local:
            cp.wait()

    out_shapes = [jax.ShapeDtypeStruct((NCHIP,) + s.shape, s.dtype) for s in shards]
    sems = [pltpu.SemaphoreType.DMA((na * nk + nk * ns,)), pltpu.SemaphoreType.DMA((na * nk + nk * ns,)),
            pltpu.SemaphoreType.DMA((na * nk,)), pltpu.SemaphoreType.DMA((na * nk,)),
            pltpu.SemaphoreType.DMA((na,)), pltpu.SemaphoreType.DMA((na,))]
    if ns:
        out_shapes.append(jax.ShapeDtypeStruct((NCHIP,) + small.shape, small.dtype))
        sems.append(pltpu.SemaphoreType.DMA)
    return _Comm(list(shards) + [small] * ns, out_shapes, sems, start, finish)


def _simple_comm(inputs, out_shapes, make_copies, n_sems, aliases=None):
    def start(ins, outs, sems):
        for cp in make_copies(ins, outs, sems):
            cp.start()

    def finish(ins, outs, sems):
        for cp in make_copies(ins, outs, sems):
            cp.wait()

    return _Comm(inputs, out_shapes, [pltpu.SemaphoreType.DMA((n,)) for n in n_sems], start, finish, aliases)


def _pair_comm(grads):
    def make_copies(ins, outs, sems):
        x, y, c, _, _, _ = _place()
        return [pltpu.make_async_remote_copy(
            src_ref=ins[a].at[:, 1 - c], dst_ref=outs[a], send_sem=sems[0].at[a], recv_sem=sems[1].at[a],
            device_id=(x, y, 1 - c), device_id_type=MESH) for a in range(len(grads))]

    out_shapes = [jax.ShapeDtypeStruct(g.shape[:1] + g.shape[2:], F32) for g in grads]
    return _simple_comm(grads, out_shapes, make_copies, [len(grads)] * 2)


def _chips_comm(sums, small=None):
    na = len(sums)
    nk = NCHIP - 1

    def make_copies(ins, outs, sems):
        x, y, c, _, chips, blocks = _place()
        copies = [pltpu.make_async_remote_copy(
            src_ref=ins[a].at[blocks[k]], dst_ref=outs[a].at[k], send_sem=sems[0].at[a * nk + k],
            recv_sem=sems[1].at[a * nk + k], device_id=(*chips[k], c), device_id_type=MESH)
            for a in range(na) for k in range(nk)]
        if small is not None:
            me = 4 * x + 2 * y + c
            for r in range(1, NDEV):
                peer = (1 - x if r & 4 else x, 1 - y if r & 2 else y, 1 - c if r & 1 else c)
                copies.append(pltpu.make_async_remote_copy(
                    src_ref=ins[na], dst_ref=outs[na].at[me], send_sem=sems[2].at[r - 1], recv_sem=sems[3].at[r - 1],
                    device_id=peer, device_id_type=MESH))
            copies.append(pltpu.make_async_copy(ins[na], outs[na].at[me], sems[4].at[0]))
        return copies

    out_shapes = [jax.ShapeDtypeStruct((nk,) + g.shape[1:], BF16) for g in sums]
    if small is None:
        return _simple_comm(sums, out_shapes, make_copies, [na * nk] * 2)
    out_shapes.append(jax.ShapeDtypeStruct((NDEV,) + small.shape, F32))
    return _simple_comm(list(sums) + [small], out_shapes, make_copies, [na * nk] * 2 + [NDEV - 1] * 2 + [1])


def _swap_comm(totals):
    def make_copies(ins, outs, sems):
        x, y, c, _, _, _ = _place()
        return [pltpu.make_async_remote_copy(
            src_ref=outs[a].at[:, c], dst_ref=outs[a].at[:, c], send_sem=sems[0].at[a], recv_sem=sems[1].at[a],
            device_id=(x, y, 1 - c), device_id_type=MESH) for a in range(len(totals))]

    out_shapes = [jax.ShapeDtypeStruct(t.shape, F32) for t in totals]
    return _simple_comm(totals, out_shapes, make_copies, [len(totals)] * 2, aliases={a: a for a in range(len(totals))})


def _pair_sum(place, grad, landed, name):
    _, _, rows, cols = grad.shape

    def body(place_ref, g_ref, l_ref, o_ref):
        o_ref[...] = (g_ref[...] + l_ref[...]).astype(BF16)

    return pl.pallas_call(
        body, name=name,
        grid_spec=pltpu.PrefetchScalarGridSpec(
            num_scalar_prefetch=1, grid=(NCHIP,),
            in_specs=[pl.BlockSpec((None, None, rows, cols), lambda b, p: (b, p[1], 0, 0)),
                      pl.BlockSpec((None, rows, cols), lambda b, p: (b, 0, 0))],
            out_specs=pl.BlockSpec((None, rows, cols), lambda b, p: (b, 0, 0))),
        out_shape=jax.ShapeDtypeStruct((NCHIP, rows, cols), BF16), compiler_params=_cparams(1),
    )(place, grad, landed)


def _chip_sum(place, grad, landed, parts, layer, n_layers, name, prev=None):
    _, _, rows, cols = grad.shape

    def body(*refs):
        g_ref, l_ref, p_ref, o_ref = refs[1], refs[2], refs[3], refs[-1]
        tot = g_ref[...] + l_ref[...]
        for k in range(NCHIP - 1):
            tot = tot + p_ref[k].astype(F32)
        o_ref[...] = tot

    in_specs = [pl.BlockSpec((None, None, rows, cols), lambda i, p: (p[0], p[1], 0, 0)),
                pl.BlockSpec((None, rows, cols), lambda i, p: (p[0], 0, 0)),
                pl.BlockSpec((NCHIP - 1, rows, cols), lambda i, p: (0, 0, 0))]
    args = [place, grad, landed, parts]
    if prev is not None:
        in_specs.append(ANY)
        args.append(prev)
    return pl.pallas_call(
        body, name=name,
        grid_spec=pltpu.PrefetchScalarGridSpec(
            num_scalar_prefetch=1, grid=(1,), in_specs=in_specs,
            out_specs=pl.BlockSpec((None, None, rows, cols), lambda i, p: (layer, p[1], 0, 0))),
        out_shape=jax.ShapeDtypeStruct((n_layers, 2, rows, cols), F32),
        input_output_aliases={} if prev is None else {4: 0}, compiler_params=_cparams(1),
    )(*args)


def _sum_devices(parts):
    def body(p_ref, o_ref):
        tot = p_ref[0]
        for d in range(1, NDEV):
            tot = tot + p_ref[d]
        o_ref[...] = tot

    return pl.pallas_call(
        body, name="sum_small", in_specs=[pl.BlockSpec(parts.shape, lambda: (0, 0, 0))],
        out_specs=pl.BlockSpec(parts.shape[1:], lambda: (0, 0)), out_shape=jax.ShapeDtypeStruct(parts.shape[1:], F32),
    )(parts)


def _pack(parts):
    rows = []
    for p in parts:
        p = p.reshape(-1, LANES)
        rows.append(jnp.pad(p, ((0, -p.shape[0] % SUBLANES), (0, 0))))
    return jnp.concatenate(rows, axis=0)


def _unpack(buf, shapes, lead=()):
    out, r0 = [], 0
    nl = len(lead)
    for shp in shapes:
        nrow = 1
        for d in shp:
            nrow *= d
        nrow //= LANES
        out.append(buf[(slice(None),) * nl + (slice(r0, r0 + nrow),)].reshape(lead + tuple(shp)))
        r0 += nrow + (-nrow % SUBLANES)
    return out


WEIGHT_ORDER = ("norm_mix_even", "w_in", "conv_a", "ln_a_g", "ln_a_b", "conv_b", "w_out", "norm_mix_odd", "w_pool",
                "pool_scale", "norm_ffn", "w_up", "conv_ffn_w", "w_down", "norm_final")
BIG = ("w_in", "w_out", "w_pool", "w_up", "w_down")
HALF = {"w_in": (D // 2, NZ // NCHIP), "w_out": (2 * A // NCHIP // 2, D), "w_pool": (PG // 2, PG),
        "w_up": (D // 2, FF2 // NCHIP), "w_down": (FF // NCHIP // 2, D)}
SMALL_SHARDED = ("conv_a", "conv_b", "conv_ffn_w", "norm_mix_odd", "pool_scale")
SMALL_ALL = ("norm_mix_even", "conv_a", "ln_a_g", "ln_a_b", "conv_b", "norm_mix_odd", "pool_scale", "norm_ffn", "conv_ffn_w",
             "norm_final")


def kernel(x, norm_mix_even, w_in, conv_a, ln_a_g, ln_a_b, conv_b, w_out, norm_mix_odd, w_pool, pool_scale, norm_ffn, w_up, conv_ffn_w, w_down, norm_final, loss_target, m_norm_mix_even, m_w_in, m_conv_a, m_ln_a_g, m_ln_a_b, m_conv_b, m_w_out, m_norm_mix_odd, m_w_pool, m_pool_scale, m_norm_ffn, m_w_up, m_conv_ffn_w, m_w_down, m_norm_final, v_norm_mix_even, v_w_in, v_conv_a, v_ln_a_g, v_ln_a_b, v_conv_b, v_w_out, v_norm_mix_odd, v_w_pool, v_pool_scale, v_norm_ffn, v_w_up, v_conv_ffn_w, v_w_down, v_norm_final):
    w = dict(norm_mix_even=norm_mix_even, w_in=w_in, conv_a=conv_a, ln_a_g=ln_a_g, ln_a_b=ln_a_b, conv_b=conv_b, w_out=w_out,
             norm_mix_odd=norm_mix_odd, w_pool=w_pool, pool_scale=pool_scale, norm_ffn=norm_ffn, w_up=w_up,
             conv_ffn_w=conv_ffn_w, w_down=w_down, norm_final=norm_final)
    m = dict(norm_mix_even=m_norm_mix_even, w_in=m_w_in, conv_a=m_conv_a, ln_a_g=m_ln_a_g, ln_a_b=m_ln_a_b, conv_b=m_conv_b,
             w_out=m_w_out, norm_mix_odd=m_norm_mix_odd, w_pool=m_w_pool, pool_scale=m_pool_scale, norm_ffn=m_norm_ffn,
             w_up=m_w_up, conv_ffn_w=m_conv_ffn_w, w_down=m_w_down, norm_final=m_norm_final)
    v = dict(norm_mix_even=v_norm_mix_even, w_in=v_w_in, conv_a=v_conv_a, ln_a_g=v_ln_a_g, ln_a_b=v_ln_a_b, conv_b=v_conv_b,
             w_out=v_w_out, norm_mix_odd=v_norm_mix_odd, w_pool=v_w_pool, pool_scale=v_pool_scale, norm_ffn=v_norm_ffn,
             w_up=v_w_up, conv_ffn_w=v_conv_ffn_w, w_down=v_w_down, norm_final=v_norm_final)
    chip = 2 * lax.axis_index("x") + lax.axis_index("y")
    place = jnp.stack([chip, lax.axis_index("c")]).astype(jnp.int32)

    half = lambda a, name: a.astype(BF16).reshape((2,) + HALF[name])
    shard = {"w_in": half(w_in[0], "w_in"), "w_out": half(w_out[0], "w_out"), "w_pool": half(w_pool[0], "w_pool"),
             "w_up0": half(w_up[0], "w_up"), "w_up1": half(w_up[1], "w_up"),
             "w_down0": half(w_down[0], "w_down"), "w_down1": half(w_down[1], "w_down")}
    small_shapes = [w[k].shape[-2:] if w[k].ndim == 3 and k != "conv_ffn_w" else (w[k].size // w[k].shape[-1], w[k].shape[-1])
                    for k in SMALL_SHARDED]
    rep = dict(norm_mix_even=norm_mix_even, ln_a_g=ln_a_g, ln_a_b=ln_a_b, norm_ffn=norm_ffn, norm_final=norm_final.reshape(1, D))
    grad_x, swapped, small_packed, small_full_shapes = _train_step(
        x[0], loss_target[0], place, shard, rep, _pack([w[k] for k in SMALL_SHARDED]), small_shapes)
    small_sum = _unpack(small_packed, small_full_shapes)
    loss = small_sum[-1][0, 0]

    grad = {k: swapped[k].reshape(w[k].shape) for k in BIG}
    for k, gsum in zip(SMALL_ALL, small_sum):
        if k in SMALL_SHARDED:
            cols = w[k].shape[-1]
            gsum = lax.dynamic_slice_in_dim(gsum, chip * cols, cols, axis=gsum.ndim - 1)
        grad[k] = gsum.reshape(w[k].shape)

    delta, new_m, new_v = {}, {}, {}
    rows_per_step = {"w_in": 512, "w_out": 256, "w_pool": 256, "w_up": 256, "w_down": 352}
    for k in BIG:
        as3 = lambda a: a.reshape(a.shape[0], -1, a.shape[-1])
        g3, d3, m3, v3 = _adamw_big(as3(w[k]), as3(grad[k]), as3(m[k]), as3(v[k]), rows_per_step[k], "adamw_" + k)
        grad[k], delta[k], new_m[k], new_v[k] = (a.reshape(w[k].shape) for a in (g3, d3, m3, v3))
    as2 = lambda a: a.reshape(-1, a.shape[-1])
    ds, ms, vs = _adamw_small(*[[as2(t[k]) for k in SMALL_ALL] for t in (w, grad, m, v)])
    for k, d2, m2, v2 in zip(SMALL_ALL, ds, ms, vs):
        delta[k], new_m[k], new_v[k] = (a.reshape(w[k].shape) for a in (d2, m2, v2))

    return (loss, grad_x[None], *[grad[k] for k in WEIGHT_ORDER], *[delta[k] for k in WEIGHT_ORDER],
            *[new_m[k] for k in WEIGHT_ORDER], *[new_v[k] for k in WEIGHT_ORDER])
```

```python
import functools

import jax
import jax.numpy as jnp
from jax import lax
from jax.experimental import pallas as pl
from jax.experimental.pallas import tpu as pltpu

F32, BF16 = jnp.float32, jnp.bfloat16

D = 1024
A = 512
NZ = 5 * A
FF = 2816
FF2 = 2 * FF
NCHIP = 4
NDEV = 8
K_A, K_S = 31, 3
POOL_WINDOWS = (2, 4, 8, 16)
PG = D // len(POOL_WINDOWS)
RMS_EPS, LN_EPS = 1e-6, 1e-5
ADAM_LR, ADAM_B1, ADAM_B2, ADAM_EPS, ADAM_WD, ADAM_STEP = 0.001, 0.9, 0.999, 1e-08, 0.01, 10

HALO_A, HALO_S, HALO_P = 32, 8, 16
SUBLANES = 8
LANES = 128
VMEM_LIMIT_BYTES = 56 * 1024 * 1024

TS_MIX = 512
TS_MIXB = 256
TS_FFN = 256
TS_POOL = 512
TS_MM = 512
R_CHUNK = 64

MESH = pl.DeviceIdType.MESH
ANY = pl.BlockSpec(memory_space=pl.ANY)
NT_DIMS = (((1,), (1,)), ((), ()))
TN_DIMS = (((0,), (0,)), ((), ()))


def _cparams(n_axes):
    return pltpu.CompilerParams(dimension_semantics=("arbitrary",) * n_axes, vmem_limit_bytes=VMEM_LIMIT_BYTES)


def _const(shape):
    nd = len(shape)
    return pl.BlockSpec(shape, lambda *_: (0,) * nd, pipeline_mode=pl.Buffered(1))


def _sigmoid(v):
    return 1.0 / (1.0 + jnp.exp(-v))


def _rsqrt_mean_sq(x):
    return lax.rsqrt(jnp.mean(x * x, axis=-1, keepdims=True) + RMS_EPS)


def _rms_bwd(dh, xh, r, g):
    dxh = dh * g
    return r * (dxh - xh * jnp.mean(dxh * xh, axis=-1, keepdims=True))


def _shifted(buf_ref, row0, rows, col0, width, offsets):
    lo = (min(offsets) // SUBLANES) * SUBLANES
    hi = -(-(max(offsets) + rows) // SUBLANES) * SUBLANES
    start = row0 + lo if isinstance(row0, int) else pl.multiple_of(row0 + lo, SUBLANES)
    win = buf_ref[pl.ds(start, hi - lo), col0:col0 + width]
    out = {}
    for res in sorted({(o - lo) % SUBLANES for o in offsets}):
        qs = {o: (o - lo) // SUBLANES for o in offsets if (o - lo) % SUBLANES == res}
        base = pltpu.roll(win, hi - lo - res, 0) if res else win
        for o, q in qs.items():
            out[o] = base[SUBLANES * q:SUBLANES * q + rows, :]
    return out


def _rowsum8(v):
    acc = v[0:SUBLANES, :]
    for r in range(SUBLANES, v.shape[0], SUBLANES):
        acc = acc + v[r:r + SUBLANES, :]
    return acc


def _taps(w_ref, sh, offsets, col0, width):
    acc = None
    for k, o in enumerate(offsets):
        term = w_ref[k:k + 1, col0:col0 + width] * sh[o]
        acc = term if acc is None else acc + term
    return acc


def _window_bases(buf_ref, row0, rows, col0, width, offsets):
    lo = (min(offsets) // SUBLANES) * SUBLANES
    hi = -(-(max(offsets) + rows) // SUBLANES) * SUBLANES
    start = row0 + lo if isinstance(row0, int) else pl.multiple_of(row0 + lo, SUBLANES)
    win = buf_ref[pl.ds(start, hi - lo), col0:col0 + width]
    for res in sorted({(o - lo) % SUBLANES for o in offsets}):
        taps = [(k, (o - lo) // SUBLANES * SUBLANES) for k, o in enumerate(offsets) if (o - lo) % SUBLANES == res]
        yield (pltpu.roll(win, hi - lo - res, 0) if res else win), taps


def _conv_acc(buf_ref, w_ref, row0, rows, col0, width, offsets):
    acc = None
    for base, taps in _window_bases(buf_ref, row0, rows, col0, width, offsets):
        for k, q in taps:
            term = w_ref[k:k + 1, col0:col0 + width] * base[q:q + rows, :]
            acc = term if acc is None else acc + term
    return acc


def _conv_corr(buf_ref, w_ref, other, acc_ref, row0, rows, col0, width, offsets):
    acc = None
    for base, taps in _window_bases(buf_ref, row0, rows, col0, width, offsets):
        for k, q in taps:
            sl = base[q:q + rows, :]
            term = w_ref[k:k + 1, col0:col0 + width] * sl
            acc = term if acc is None else acc + term
            acc_ref[SUBLANES * k:SUBLANES * (k + 1), col0:col0 + width] += _rowsum8(sl * other)
    return acc


def _finish_tap_sums(acc_ref, out_ref, n_taps):
    for k in range(n_taps):
        out_ref[k:k + 1, :] = jnp.sum(acc_ref[SUBLANES * k:SUBLANES * (k + 1), :], axis=0, keepdims=True)


class _Comm:
    def __init__(self, inputs, out_shapes, sems, start, finish, aliases=None):
        self.inputs, self.out_shapes, self.sems = list(inputs), list(out_shapes), list(sems)
        self.start, self.finish, self.aliases = start, finish, dict(aliases or {})


def _join_comm(a, b):
    ni, no, ns = len(a.inputs), len(a.out_shapes), len(a.sems)

    def start(ins, outs, sems):
        a.start(ins[:ni], outs[:no], sems[:ns])
        b.start(ins[ni:], outs[no:], sems[ns:])

    def finish(ins, outs, sems):
        a.finish(ins[:ni], outs[:no], sems[:ns])
        b.finish(ins[ni:], outs[no:], sems[ns:])

    aliases = {**a.aliases, **{ni + i: no + o for i, o in b.aliases.items()}}
    return _Comm(a.inputs + b.inputs, a.out_shapes + b.out_shapes, a.sems + b.sems, start, finish, aliases)


def _call(body, *, name, grid, in_specs, out_specs, out_shape, args, scratch=(), comm=None, aliases=None):
    n_in, n_out, n_scr, n_axes = len(in_specs), len(out_specs), len(scratch), len(grid)
    params = pltpu.CompilerParams(dimension_semantics=("arbitrary",) * n_axes, vmem_limit_bytes=VMEM_LIMIT_BYTES)
    aliases = dict(aliases or {})
    if comm is None:
        outs = pl.pallas_call(body, name=name, grid=grid, in_specs=list(in_specs), out_specs=list(out_specs),
                              out_shape=list(out_shape), scratch_shapes=list(scratch), input_output_aliases=aliases,
                              compiler_params=params)(*args)
        return list(outs), []
    ci, co = len(comm.inputs), len(comm.out_shapes)

    def wrapped(*refs):
        k_in, c_in = refs[:n_in], refs[n_in:n_in + ci]
        o0 = n_in + ci
        k_out, c_out = refs[o0:o0 + n_out], refs[o0 + n_out:o0 + n_out + co]
        s0 = o0 + n_out + co
        k_scr, c_sem = refs[s0:s0 + n_scr], refs[s0 + n_scr:]
        first = pl.program_id(0) == 0
        last = pl.program_id(0) == grid[0] - 1
        for ax in range(1, n_axes):
            first = jnp.logical_and(first, pl.program_id(ax) == 0)
            last = jnp.logical_and(last, pl.program_id(ax) == grid[ax] - 1)

        @pl.when(first)
        def _():
            comm.start(c_in, c_out, c_sem)

        body(*k_in, *k_out, *k_scr)

        @pl.when(last)
        def _():
            comm.finish(c_in, c_out, c_sem)

    outs = pl.pallas_call(
        wrapped, name=name, grid=grid, in_specs=list(in_specs) + [ANY] * ci, out_specs=list(out_specs) + [ANY] * co,
        out_shape=list(out_shape) + comm.out_shapes, scratch_shapes=list(scratch) + comm.sems,
        input_output_aliases={**aliases, **{n_in + i: n_out + o for i, o in comm.aliases.items()}}, compiler_params=params,
    )(*args, *comm.inputs)
    return list(outs[:n_out]), list(outs[n_out:])


def _run_comm(comm, name):
    ci, co = len(comm.inputs), len(comm.out_shapes)

    def body(*refs):
        c_in, c_out, c_sem = refs[:ci], refs[ci:ci + co], refs[ci + co:]
        comm.start(c_in, c_out, c_sem)
        comm.finish(c_in, c_out, c_sem)

    return list(pl.pallas_call(body, name=name, in_specs=[ANY] * ci, out_specs=[ANY] * co, out_shape=comm.out_shapes,
                               scratch_shapes=comm.sems, input_output_aliases=comm.aliases)(*comm.inputs))


def _mix0_fwd(x, g, w_in, conv_a, ln_g, ln_b, conv_b, w_out, comm=None):
    s = x.shape[0]
    ts = min(TS_MIX, s)
    n = s // ts
    bw = NZ // NCHIP
    offs_a = [HALO_A - (K_A - 1) + k for k in range(K_A)]
    offs_s = [HALO_S - (K_S - 1) + k for k in range(K_S)]

    def body(x_ref, g_ref, win_ref, ca_ref, lg_ref, lb_ref, cb_ref, wout_ref,
             h_ref, z_ref, ac_ref, bconv_buf, cat_ref, x1_ref, glu_buf, cv_buf):
        i = pl.program_id(0)

        @pl.when(i == 0)
        def _():
            glu_buf[0:HALO_A, :] = jnp.zeros((HALO_A, A), F32)
            cv_buf[0:HALO_S, :] = jnp.zeros((HALO_S, A), F32)

        xv = x_ref[...]
        h = (xv * _rsqrt_mean_sq(xv) * g_ref[...]).astype(BF16)
        h_ref[...] = h
        for j in range(NCHIP):
            z_ref[:, j * bw:(j + 1) * bw] = jnp.dot(h, win_ref[j], preferred_element_type=F32)
        glu_buf[HALO_A:HALO_A + ts, :] = z_ref[:, 0:A] * _sigmoid(z_ref[:, A:2 * A])
        cv_buf[HALO_S:HALO_S + ts, :] = z_ref[:, 3 * A:4 * A] * z_ref[:, 4 * A:5 * A]

        def chunk(ci, carry):
            r0 = pl.multiple_of(ci * R_CHUNK, R_CHUNK)
            for c0 in range(0, A, LANES):
                ac_ref[pl.ds(r0, R_CHUNK), c0:c0 + LANES] = _conv_acc(glu_buf, ca_ref, r0, R_CHUNK, c0, LANES, offs_a)
                bconv_buf[pl.ds(r0, R_CHUNK), c0:c0 + LANES] = _conv_acc(cv_buf, cb_ref, r0, R_CHUNK, c0, LANES, offs_s)
            return carry

        lax.fori_loop(0, ts // R_CHUNK, chunk, 0)
        glu_buf[0:HALO_A, :] = glu_buf[ts:ts + HALO_A, :]
        cv_buf[0:HALO_S, :] = cv_buf[ts:ts + HALO_S, :]

        ac = ac_ref[...]
        xc = ac - jnp.mean(ac, axis=-1, keepdims=True)
        xn = xc * lax.rsqrt(jnp.mean(xc * xc, axis=-1, keepdims=True) + LN_EPS)
        ln = xn * lg_ref[...] + lb_ref[...]
        cat_ref[:, 0:A] = (ln * _sigmoid(ln)).astype(BF16)
        cat_ref[:, A:2 * A] = (z_ref[:, 2 * A:3 * A] * bconv_buf[...]).astype(BF16)
        x1_ref[...] = xv + jnp.dot(cat_ref[...], wout_ref[...], preferred_element_type=F32)

    tile = lambda w: pl.BlockSpec((ts, w), lambda i: (i, 0))
    return _call(
        body, name="mix0_fwd", grid=(n,), comm=comm, args=(x, g, w_in, conv_a, ln_g, ln_b, conv_b, w_out),
        in_specs=[tile(D), _const((1, D)), _const((NCHIP, D, bw)), _const((K_A, A)), _const((1, A)), _const((1, A)),
                  _const((K_S, A)), _const((2 * A, D))],
        out_specs=[tile(D), tile(NZ), tile(A), tile(A), tile(2 * A), tile(D)],
        out_shape=[jax.ShapeDtypeStruct((s, D), BF16), jax.ShapeDtypeStruct((s, NZ), F32), jax.ShapeDtypeStruct((s, A), F32),
                   jax.ShapeDtypeStruct((s, A), F32), jax.ShapeDtypeStruct((s, 2 * A), BF16), jax.ShapeDtypeStruct((s, D), F32)],
        scratch=[pltpu.VMEM((HALO_A + ts, A), F32), pltpu.VMEM((HALO_S + ts, A), F32)])


def _ffn_fwd(x, g, w_up, wc, w_down, name, comm=None):
    s = x.shape[0]
    ts = min(TS_FFN, s)
    n = s // ts
    bw = FF2 // NCHIP
    rows = 32
    offs = [HALO_S - (K_S - 1) + k for k in range(K_S)]

    def body(x_ref, g_ref, wup_ref, wc_ref, wdn_ref, h_ref, u0_ref, u_ref, act_ref, xo_ref, cbuf):
        i = pl.program_id(0)

        @pl.when(i == 0)
        def _():
            cbuf[0:HALO_S, :] = jnp.zeros((HALO_S, FF2), F32)

        xv = x_ref[...]
        h = (xv * _rsqrt_mean_sq(xv) * g_ref[...]).astype(BF16)
        h_ref[...] = h
        f = None
        for p in range(NCHIP // 2):
            for j in (p, NCHIP // 2 + p):
                zc = jnp.dot(h, wup_ref[j], preferred_element_type=F32)
                u0_ref[:, j * bw:(j + 1) * bw] = zc.astype(BF16)
                cbuf[HALO_S:HALO_S + ts, j * bw:(j + 1) * bw] = zc
            for r0 in range(0, ts, rows):
                for c0 in range(p * bw, (p + 1) * bw, LANES):
                    ug = _taps(wc_ref, _shifted(cbuf, r0, rows, c0, LANES, offs), offs, c0, LANES)
                    uv = _taps(wc_ref, _shifted(cbuf, r0, rows, FF + c0, LANES, offs), offs, FF + c0, LANES)
                    u_ref[r0:r0 + rows, c0:c0 + LANES] = ug
                    u_ref[r0:r0 + rows, FF + c0:FF + c0 + LANES] = uv
                    act_ref[r0:r0 + rows, c0:c0 + LANES] = (ug * _sigmoid(ug) * uv).astype(BF16)
            fp = jnp.dot(act_ref[:, p * bw:(p + 1) * bw], wdn_ref[p * bw:(p + 1) * bw, :], preferred_element_type=F32)
            f = fp if f is None else f + fp
        cbuf[0:HALO_S, :] = cbuf[ts:ts + HALO_S, :]
        xo_ref[...] = xv + f

    tile = lambda w: pl.BlockSpec((ts, w), lambda i: (i, 0))
    return _call(
        body, name=name, grid=(n,), comm=comm, args=(x, g, w_up, wc, w_down),
        in_specs=[tile(D), _const((1, D)), _const((NCHIP, D, bw)), _const((K_S, FF2)), _const((FF, D))],
        out_specs=[tile(D), tile(FF2), tile(FF2), tile(FF), tile(D)],
        out_shape=[jax.ShapeDtypeStruct((s, D), BF16), jax.ShapeDtypeStruct((s, FF2), BF16), jax.ShapeDtypeStruct((s, FF2), F32),
                   jax.ShapeDtypeStruct((s, FF), BF16), jax.ShapeDtypeStruct((s, D), F32)],
        scratch=[pltpu.VMEM((HALO_S + ts, FF2), F32)])


def _pool_windows(hbuf, pbuf, tile_row0, ts):
    def chunk(ci, carry):
        r0 = pl.multiple_of(ci * R_CHUNK, R_CHUNK)
        t1 = (tile_row0 + r0 + lax.broadcasted_iota(jnp.int32, (R_CHUNK, 1), 0) + 1).astype(F32)
        for gi, w in enumerate(POOL_WINDOWS):
            cnt = jnp.minimum(t1, float(w))
            offs = [HALO_P - jj for jj in range(w)]
            for c0 in range(gi * PG, (gi + 1) * PG, LANES):
                sh = _shifted(hbuf, r0, R_CHUNK, c0, LANES, offs)
                tot = sh[offs[0]]
                for o in offs[1:]:
                    tot = tot + sh[o]
                pbuf[pl.ds(r0, R_CHUNK), c0:c0 + LANES] = (tot / cnt - sh[HALO_P]).astype(BF16)
        return carry

    lax.fori_loop(0, ts // R_CHUNK, chunk, 0)


def _assemble_wpool(wp_ref, wps):
    rb = PG // NCHIP
    for gi in range(len(POOL_WINDOWS)):
        for j in range(NCHIP):
            wps[gi, j * rb:(j + 1) * rb, :] = wp_ref[j, gi]


def _pool_fwd(x, g, w_pool, scale):
    s = x.shape[0]
    ts = min(TS_POOL, s)
    n = s // ts
    ng = len(POOL_WINDOWS)

    def body(x_ref, g_ref, wp_ref, sc_ref, xo_ref, hbuf, pbuf, wps):
        i = pl.program_id(0)

        @pl.when(i == 0)
        def _():
            hbuf[0:HALO_P, :] = jnp.zeros((HALO_P, D), F32)
            _assemble_wpool(wp_ref, wps)

        xv = x_ref[...]
        hbuf[HALO_P:HALO_P + ts, :] = xv * _rsqrt_mean_sq(xv) * g_ref[...]
        _pool_windows(hbuf, pbuf, i * ts, ts)
        hbuf[0:HALO_P, :] = hbuf[ts:ts + HALO_P, :]
        for gi in range(ng):
            cols = slice(gi * PG, (gi + 1) * PG)
            y = jnp.dot(pbuf[:, cols], wps[gi], preferred_element_type=F32)
            xo_ref[:, cols] = xv[:, cols] + y * sc_ref[:, cols]

    tile = pl.BlockSpec((ts, D), lambda i: (i, 0))
    return pl.pallas_call(
        body, name="pool_fwd", grid=(n,),
        in_specs=[tile, _const((1, D)), _const((NCHIP, ng, PG // NCHIP, PG)), _const((1, D))],
        out_specs=tile, out_shape=jax.ShapeDtypeStruct((s, D), F32),
        scratch_shapes=[pltpu.VMEM((HALO_P + ts, D), F32), pltpu.VMEM((ts, D), BF16), pltpu.VMEM((ng, PG, PG), BF16)],
        compiler_params=_cparams(1),
    )(x, g, w_pool, scale)


def _final_loss(x, g, target):
    s = x.shape[0]
    ts = min(TS_MM, s)
    n = s // ts

    def body(x_ref, g_ref, t_ref, dx_ref, dg_ref, loss_ref):
        i = pl.program_id(0)

        @pl.when(i == 0)
        def _():
            dg_ref[...] = jnp.zeros((1, D), F32)
            loss_ref[...] = jnp.zeros((1, LANES), F32)

        xv = x_ref[...]
        r = _rsqrt_mean_sq(xv)
        xh = xv * r
        gv = g_ref[...]
        err = xh * gv - t_ref[...]
        sq = jnp.sum(jnp.sum(err * err, axis=1, keepdims=True), axis=0, keepdims=True)
        loss_ref[...] += sq * (0.5 / D)
        dy = err * (1.0 / D)
        dg_ref[...] += jnp.sum(dy * xh, axis=0, keepdims=True)
        dx_ref[...] = _rms_bwd(dy, xh, r, gv)

    tile = pl.BlockSpec((ts, D), lambda i: (i, 0))
    return pl.pallas_call(
        body, name="final_loss", grid=(n,),
        in_specs=[tile, _const((1, D)), tile],
        out_specs=[tile, pl.BlockSpec((1, D), lambda i: (0, 0)), pl.BlockSpec((1, LANES), lambda i: (0, 0))],
        out_shape=[jax.ShapeDtypeStruct((s, D), F32), jax.ShapeDtypeStruct((1, D), F32), jax.ShapeDtypeStruct((1, LANES), F32)],
        compiler_params=_cparams(1),
    )(x, g, target)


def _ffn_bwd_a(dxo, u, u0, wc, w_down, name, comm=None):
    s = dxo.shape[0]
    ts = min(TS_FFN, s)
    n = s // ts
    cw = FF2 // NCHIP
    rows = 32
    lw2 = 2 * LANES
    boffs = [K_S - 1 - k for k in range(K_S)]

    def body(dxo_ref, u_ref, u0_ref, wc_ref, wdn_ref, du0_ref, dwc_ref, dubuf, dact, dwacc):
        i = pl.program_id(0)

        @pl.when(i == 0)
        def _():
            dubuf[ts:ts + HALO_S, :] = jnp.zeros((HALO_S, FF2), F32)
            dwacc[...] = jnp.zeros(dwacc.shape, F32)

        df = dxo_ref[...].astype(BF16)
        for cg in range(0, FF, cw):
            dact[...] = lax.dot_general(df, wdn_ref[cg:cg + cw, :], NT_DIMS, preferred_element_type=F32)

            def chunk(ci, carry, cg=cg):
                rs = pl.ds(pl.multiple_of(ci * rows, rows), rows)
                for c in range(cg, cg + cw, LANES):
                    ug, uv = u_ref[rs, c:c + LANES], u_ref[rs, FF + c:FF + c + LANES]
                    sg = _sigmoid(ug)
                    da = dact[rs, c - cg:c - cg + LANES]
                    gs = ug * sg
                    dubuf[rs, c:c + LANES] = da * uv * (sg + gs * (1.0 - sg))
                    dubuf[rs, FF + c:FF + c + LANES] = da * gs
                return carry

            lax.fori_loop(0, ts // rows, chunk, 0)

        def chunk2(ci, carry):
            r0 = pl.multiple_of(ci * rows, rows)
            rs = pl.ds(r0, rows)
            for c in range(0, FF2, lw2):
                sh = _shifted(dubuf, r0, rows, c, lw2, boffs)
                du0_ref[rs, c:c + lw2] = _taps(wc_ref, sh, boffs, c, lw2).astype(BF16)
                u0v = u0_ref[rs, c:c + lw2].astype(F32)
                for k, o in enumerate(boffs):
                    dwacc[SUBLANES * k:SUBLANES * (k + 1), c:c + lw2] += _rowsum8(sh[o] * u0v)
            return carry

        lax.fori_loop(0, ts // rows, chunk2, 0)
        dubuf[ts:ts + HALO_S, :] = dubuf[0:HALO_S, :]

        @pl.when(i == n - 1)
        def _():
            _finish_tap_sums(dwacc, dwc_ref, K_S)

    rev = lambda w: pl.BlockSpec((ts, w), lambda i: (n - 1 - i, 0))
    return _call(
        body, name=name, grid=(n,), comm=comm, args=(dxo, u, u0, wc, w_down),
        in_specs=[rev(D), rev(FF2), rev(FF2), _const((K_S, FF2)), _const((FF, D))],
        out_specs=[rev(FF2), pl.BlockSpec((K_S, FF2), lambda i: (0, 0))],
        out_shape=[jax.ShapeDtypeStruct((s, FF2), BF16), jax.ShapeDtypeStruct((K_S, FF2), F32)],
        scratch=[pltpu.VMEM((ts + HALO_S, FF2), F32), pltpu.VMEM((ts, cw), F32), pltpu.VMEM((SUBLANES * K_S, FF2), F32)])


TS_FFNB = 256
N_SLICE = 4


def _ffn_bwd_half(dxo, u, u0, wc, w_up, w_down, p, name, first=None, x=None, g=None, comm=None):
    s = dxo.shape[0]
    ts = min(TS_FFNB, s)
    n = s // ts
    bw = FF2 // NCHIP
    rows = ts // N_SLICE
    dsl = D // N_SLICE
    boffs = [K_S - 1 - k for k in range(K_S)]
    second = first is not None
    n_in = 9 + (5 if second else 0)

    def body(*refs):
        dxo_ref, ug_ref, uv_ref, u0g_ref, u0v_ref, wcg_ref, wcv_ref, wup_ref, wdn_ref = refs[:9]
        if second:
            dh0_ref, x_ref, dxp_ref, g_ref = refs[9:13]
        outs = refs[n_in:]
        dx_ref = outs[0]
        dg_ref = outs[1] if second else None
        dwcg_ref, dwcv_ref, du0_ref = outs[1 + second:4 + second]
        dubuf, dact, du0_s, dh_s, dwacc = outs[4 + second:]
        u0_refs, wc_refs = (u0g_ref, u0v_ref), (wcg_ref, wcv_ref)
        i = pl.program_id(0)
        live = (i < n).astype(F32)

        @pl.when(i == 0)
        def _():
            dubuf[ts:ts + HALO_S, :] = jnp.zeros((HALO_S, 2 * bw), F32)
            dwacc[...] = jnp.zeros(dwacc.shape, F32)
            du0_s[...] = jnp.zeros(du0_s.shape, BF16)
            if second:
                dg_ref[...] = jnp.zeros((1, D), F32)

        dact[...] = lax.dot_general(dxo_ref[...].astype(BF16), wdn_ref[...], NT_DIMS, preferred_element_type=F32)

        def chunk(ci, carry):
            rs = pl.ds(pl.multiple_of(ci * rows, rows), rows)
            for c in range(0, bw, LANES):
                cs = slice(c, c + LANES)
                ug, uv = ug_ref[rs, cs], uv_ref[rs, cs]
                sg = _sigmoid(ug)
                gs = ug * sg
                da = dact[rs, cs]
                dubuf[rs, cs] = da * uv * (sg + gs * (1.0 - sg))
                dubuf[rs, bw + c:bw + c + LANES] = da * gs
            return carry

        lax.fori_loop(0, N_SLICE, chunk, 0)

        def chunk2(ci, carry):
            r0 = pl.multiple_of(ci * rows, rows)
            rs = pl.ds(r0, rows)
            for part in range(2):
                for c in range(0, bw, LANES):
                    cs = slice(c, c + LANES)
                    sh = _shifted(dubuf, r0, rows, part * bw + c, LANES, boffs)
                    du0_ref[part, rs, cs] = _taps(wc_refs[part], sh, boffs, c, LANES).astype(BF16)
                    u0v = u0_refs[part][rs, cs].astype(F32) * live
                    for k, o in enumerate(boffs):
                        dwacc[SUBLANES * k:SUBLANES * (k + 1), part * bw + c:part * bw + c + LANES] += _rowsum8(sh[o] * u0v)
            ds_ = pl.ds(pl.multiple_of(ci * dsl, dsl), dsl)
            dh_s[ci] = (lax.dot_general(du0_s[0], wup_ref[0, ds_, :], NT_DIMS, preferred_element_type=F32)
                        + lax.dot_general(du0_s[1], wup_ref[1, ds_, :], NT_DIMS, preferred_element_type=F32))
            return carry

        lax.fori_loop(0, N_SLICE, chunk2, 0)
        dubuf[ts:ts + HALO_S, :] = dubuf[0:HALO_S, :]
        du0_s[...] = du0_ref[...]

        @pl.when(i >= 1)
        def _():
            for q in range(N_SLICE):
                qs = slice(q * dsl, (q + 1) * dsl)
                if second:
                    dx_ref[:, qs] = dh_s[q] + dh0_ref[:, qs]
                else:
                    dx_ref[:, qs] = dh_s[q]
            if second:
                dh = dx_ref[...]
                xv = x_ref[...]
                r = _rsqrt_mean_sq(xv)
                xh = xv * r
                dg_ref[...] += jnp.sum(dh * xh, axis=0, keepdims=True)
                dx_ref[...] = dxp_ref[...] + _rms_bwd(dh, xh, r, g_ref[...])

        @pl.when(i == n)
        def _():
            for k in range(K_S):
                ks = slice(SUBLANES * k, SUBLANES * (k + 1))
                dwcg_ref[k:k + 1, :] = jnp.sum(dwacc[ks, 0:bw], axis=0, keepdims=True)
                dwcv_ref[k:k + 1, :] = jnp.sum(dwacc[ks, bw:2 * bw], axis=0, keepdims=True)

    half2 = NCHIP // 2
    cur_t = lambda i: jnp.maximum(n - 1 - i, 0)
    prv_t = lambda i: jnp.minimum(n - i, n - 1)
    tile = lambda w, t, cb=0: pl.BlockSpec((ts, w), lambda i: (t(i), cb))
    one = pl.Buffered(1)
    in_specs = [tile(D, cur_t), tile(bw, cur_t, p), tile(bw, cur_t, half2 + p), tile(bw, cur_t, p), tile(bw, cur_t, half2 + p),
                pl.BlockSpec((K_S, bw), lambda i: (0, p), pipeline_mode=one),
                pl.BlockSpec((K_S, bw), lambda i: (0, half2 + p), pipeline_mode=one),
                pl.BlockSpec((half2, None, D, bw), lambda i: (0, p, 0, 0), pipeline_mode=one),
                pl.BlockSpec((bw, D), lambda i: (p, 0), pipeline_mode=one)]
    args = [dxo, u, u, u0, u0, wc, wc, w_up.reshape(half2, half2, D, bw), w_down]
    vec = pl.BlockSpec((1, D), lambda i: (0, 0))
    small = pl.BlockSpec((K_S, bw), lambda i: (0, 0))
    out_specs = [tile(D, prv_t)] + [vec] * second + [small, small, pl.BlockSpec((2, ts, bw), lambda i: (0, n - 1 - i + (i // n) * (n + 1), p))]
    out_shape = [jax.ShapeDtypeStruct((s, D), F32)] + [jax.ShapeDtypeStruct((1, D), F32)] * second + [
        jax.ShapeDtypeStruct((K_S, bw), F32), jax.ShapeDtypeStruct((K_S, bw), F32), jax.ShapeDtypeStruct((2, s + ts, FF), BF16)]
    aliases = {}
    if second:
        dh0, _, _, du0_first = first
        in_specs += [tile(D, prv_t), tile(D, prv_t), tile(D, prv_t), _const((1, D)), ANY]
        args += [dh0, x, dxo, g, du0_first]
        aliases = {13: 4}
    return _call(
        body, name=name, grid=(n + 1,), comm=comm, args=tuple(args), in_specs=in_specs, out_specs=out_specs,
        out_shape=out_shape, aliases=aliases,
        scratch=[pltpu.VMEM((ts + HALO_S, 2 * bw), F32), pltpu.VMEM((ts, bw), F32), pltpu.VMEM((2, ts, bw), BF16),
                 pltpu.VMEM((N_SLICE, ts, dsl), F32), pltpu.VMEM((SUBLANES * K_S, 2 * bw), F32)])


def _nt_rms_bwd(dy, w, x, g, dres, name, comm=None):
    s = x.shape[0]
    ts = min(TS_MM, s)
    n = s // ts
    nw = dy.shape[1]
    bw = nw // NCHIP

    def body(dy_ref, w_ref, x_ref, g_ref, dres_ref, dx_ref, dg_ref):
        i = pl.program_id(0)

        @pl.when(i == 0)
        def _():
            dg_ref[...] = jnp.zeros((1, D), F32)

        dh = lax.dot_general(dy_ref[:, 0:bw], w_ref[0], NT_DIMS, preferred_element_type=F32)
        for j in range(1, NCHIP):
            dh = dh + lax.dot_general(dy_ref[:, j * bw:(j + 1) * bw], w_ref[j], NT_DIMS, preferred_element_type=F32)
        xv = x_ref[...]
        r = _rsqrt_mean_sq(xv)
        xh = xv * r
        dg_ref[...] += jnp.sum(dh * xh, axis=0, keepdims=True)
        dx_ref[...] = dres_ref[...] + _rms_bwd(dh, xh, r, g_ref[...])

    tile = lambda wd: pl.BlockSpec((ts, wd), lambda i: (i, 0))
    return _call(
        body, name=name, grid=(n,), comm=comm, args=(dy, w, x, g, dres),
        in_specs=[tile(nw), _const((NCHIP, D, bw)), tile(D), _const((1, D)), tile(D)],
        out_specs=[tile(D), pl.BlockSpec((1, D), lambda i: (0, 0))],
        out_shape=[jax.ShapeDtypeStruct((s, D), F32), jax.ShapeDtypeStruct((1, D), F32)])


def _wgrad(a, b, n_blocks, name, comm=None):
    s, m = a.shape
    bw = b.shape[1] // n_blocks
    tk = min(TS_MM, s)

    def body(a_ref, b_ref, o_ref):
        @pl.when(pl.program_id(0) == 0)
        def _():
            o_ref[...] = jnp.zeros(o_ref.shape, F32)

        for j in range(n_blocks):
            o_ref[j] += lax.dot_general(a_ref[...], b_ref[:, j * bw:(j + 1) * bw].astype(BF16), TN_DIMS,
                                        preferred_element_type=F32)

    (out,), comm_out = _call(
        body, name=name, grid=(s // tk,), comm=comm, args=(a, b),
        in_specs=[pl.BlockSpec((tk, m), lambda k: (k, 0)), pl.BlockSpec((tk, b.shape[1]), lambda k: (k, 0))],
        out_specs=[pl.BlockSpec((n_blocks, m, bw), lambda k: (0, 0, 0))],
        out_shape=[jax.ShapeDtypeStruct((n_blocks, m, bw), F32)])
    return out, comm_out


def _wgrad_up(h, du0, name, comm=None):
    s, m = h.shape
    bw = FF2 // NCHIP
    half2 = NCHIP // 2
    tk = min(TS_MM, s)

    def body(a_ref, b_ref, o_ref):
        @pl.when(pl.program_id(0) == 0)
        def _():
            o_ref[...] = jnp.zeros(o_ref.shape, F32)

        for j in range(NCHIP):
            o_ref[j] += lax.dot_general(a_ref[...], b_ref[j // half2, :, (j % half2) * bw:(j % half2 + 1) * bw], TN_DIMS,
                                        preferred_element_type=F32)

    (out,), comm_out = _call(
        body, name=name, grid=(s // tk,), comm=comm, args=(h, du0),
        in_specs=[pl.BlockSpec((tk, m), lambda k: (k, 0)), pl.BlockSpec((2, tk, FF), lambda k: (0, k, 0))],
        out_specs=[pl.BlockSpec((NCHIP, m, bw), lambda k: (0, 0, 0))], out_shape=[jax.ShapeDtypeStruct((NCHIP, m, bw), F32)])
    return out, comm_out


def _pool_bwd(dxo, x, g, w_pool, scale, comm=None):
    s = x.shape[0]
    ts = min(TS_POOL, s)
    n = s // ts
    ng = len(POOL_WINDOWS)
    rb = PG // NCHIP

    def body(dxo_ref, x_ref, halo_ref, g_ref, wp_ref, sc_ref, dx_ref, dwp_ref, dsc_ref, dg_ref,
             hbuf, pbuf, qbuf, dhbuf, wps, dwacc):
        i = pl.program_id(0)
        j = n - 1 - i

        @pl.when(i == 0)
        def _():
            qbuf[ts:ts + HALO_P, :] = jnp.zeros((HALO_P, D), F32)
            dwacc[...] = jnp.zeros(dwacc.shape, F32)
            dsc_ref[...] = jnp.zeros((1, D), F32)
            dg_ref[...] = jnp.zeros((1, D), F32)
            _assemble_wpool(wp_ref, wps)

        gv = g_ref[...]
        xl = halo_ref[...]
        hbuf[0:HALO_P, :] = jnp.where(j == 0, 0.0, xl * _rsqrt_mean_sq(xl) * gv)
        xv = x_ref[...]
        r = _rsqrt_mean_sq(xv)
        xh = xv * r
        hbuf[HALO_P:HALO_P + ts, :] = xh * gv
        _pool_windows(hbuf, pbuf, j * ts, ts)

        dy = dxo_ref[...]
        t1 = (j * ts + lax.broadcasted_iota(jnp.int32, (ts, 1), 0) + 1).astype(F32)
        for gi, w in enumerate(POOL_WINDOWS):
            cols = slice(gi * PG, (gi + 1) * PG)
            p = pbuf[:, cols]
            y = jnp.dot(p, wps[gi], preferred_element_type=F32)
            dsc_ref[:, cols] += jnp.sum(dy[:, cols] * y, axis=0, keepdims=True)
            dq = (dy[:, cols] * sc_ref[:, cols]).astype(BF16)
            dwacc[gi] += lax.dot_general(p, dq, TN_DIMS, preferred_element_type=F32)
            dp = lax.dot_general(dq, wps[gi], NT_DIMS, preferred_element_type=F32)
            qbuf[0:ts, cols] = dp / jnp.minimum(t1, float(w))

        def chunk(ci, carry):
            r0 = pl.multiple_of(ci * R_CHUNK, R_CHUNK)
            tc = (j * ts + r0 + lax.broadcasted_iota(jnp.int32, (R_CHUNK, 1), 0) + 1).astype(F32)
            for gi, w in enumerate(POOL_WINDOWS):
                cnt = jnp.minimum(tc, float(w))
                offs = list(range(w))
                for c0 in range(gi * PG, (gi + 1) * PG, LANES):
                    sh = _shifted(qbuf, r0, R_CHUNK, c0, LANES, offs)
                    tot = sh[0]
                    for o in offs[1:]:
                        tot = tot + sh[o]
                    dhbuf[pl.ds(r0, R_CHUNK), c0:c0 + LANES] = tot - sh[0] * cnt
            return carry

        lax.fori_loop(0, ts // R_CHUNK, chunk, 0)
        qbuf[ts:ts + HALO_P, :] = qbuf[0:HALO_P, :]
        dh = dhbuf[...]
        dg_ref[...] += jnp.sum(dh * xh, axis=0, keepdims=True)
        dx_ref[...] = dy + _rms_bwd(dh, xh, r, gv)

        @pl.when(i == n - 1)
        def _():
            for gi in range(ng):
                for jj in range(NCHIP):
                    dwp_ref[jj, gi] = dwacc[gi, jj * rb:(jj + 1) * rb, :]

    rev = pl.BlockSpec((ts, D), lambda i: (n - 1 - i, 0))
    halo = pl.BlockSpec((HALO_P, D), lambda i: (jnp.maximum((n - 1 - i) * (ts // HALO_P) - 1, 0), 0))
    vec = pl.BlockSpec((1, D), lambda i: (0, 0))
    return _call(
        body, name="pool_bwd", grid=(n,), comm=comm, args=(dxo, x, x, g, w_pool, scale),
        in_specs=[rev, rev, halo, _const((1, D)), _const((NCHIP, ng, rb, PG)), _const((1, D))],
        out_specs=[rev, pl.BlockSpec((NCHIP, ng, rb, PG), lambda i: (0, 0, 0, 0)), vec, vec],
        out_shape=[jax.ShapeDtypeStruct((s, D), F32), jax.ShapeDtypeStruct((NCHIP, ng, rb, PG), F32),
                   jax.ShapeDtypeStruct((1, D), F32), jax.ShapeDtypeStruct((1, D), F32)],
        scratch=[pltpu.VMEM((HALO_P + ts, D), F32), pltpu.VMEM((ts, D), BF16), pltpu.VMEM((ts + HALO_P, D), F32),
                 pltpu.VMEM((ts, D), F32), pltpu.VMEM((ng, PG, PG), BF16), pltpu.VMEM((ng, PG, PG), F32)])


def _mix0_bwd_a(dx1, z, ac, bconv, w_out, conv_a, ln_g, ln_b, conv_b, comm=None):
    s = dx1.shape[0]
    ts = min(TS_MIXB, s)
    n = s // ts
    rows = 32
    boffs_a = [K_A - 1 - k for k in range(K_A)]
    boffs_s = [K_S - 1 - k for k in range(K_S)]

    def body(dx1_ref, z_ref, ac_ref, bconv_ref, wout_ref, ca_ref, lg_ref, lb_ref, cb_ref,
             dz_ref, dca_ref, dlg_ref, dlb_ref, dcb_ref, dac_buf, dbc_buf, dca_acc, dcb_acc):
        i = pl.program_id(0)

        @pl.when(i == 0)
        def _():
            dac_buf[ts:ts + HALO_A, :] = jnp.zeros((HALO_A, A), F32)
            dbc_buf[ts:ts + HALO_S, :] = jnp.zeros((HALO_S, A), F32)
            dca_acc[...] = jnp.zeros(dca_acc.shape, F32)
            dcb_acc[...] = jnp.zeros(dcb_acc.shape, F32)
            dlg_ref[...] = jnp.zeros((1, A), F32)
            dlb_ref[...] = jnp.zeros((1, A), F32)

        dcat = lax.dot_general(dx1_ref[...].astype(BF16), wout_ref[...], NT_DIMS, preferred_element_type=F32)
        db = dcat[:, A:2 * A]
        dz_ref[:, 2 * A:3 * A] = (db * bconv_ref[...]).astype(BF16)
        dbc_buf[0:ts, :] = db * z_ref[:, 2 * A:3 * A]
        ac = ac_ref[...]
        xc = ac - jnp.mean(ac, axis=-1, keepdims=True)
        rstd = lax.rsqrt(jnp.mean(xc * xc, axis=-1, keepdims=True) + LN_EPS)
        xn = xc * rstd
        lg = lg_ref[...]
        ln = xn * lg + lb_ref[...]
        sl = _sigmoid(ln)
        dln = dcat[:, 0:A] * sl * (1.0 + ln * (1.0 - sl))
        dlg_ref[...] += jnp.sum(dln * xn, axis=0, keepdims=True)
        dlb_ref[...] += jnp.sum(dln, axis=0, keepdims=True)
        dxn = dln * lg
        dac_buf[0:ts, :] = rstd * (dxn - jnp.mean(dxn, axis=-1, keepdims=True)
                                   - xn * jnp.mean(dxn * xn, axis=-1, keepdims=True))

        def chunk(ci, carry):
            r0 = pl.multiple_of(ci * rows, rows)
            rs = pl.ds(r0, rows)
            for c0 in range(0, A, LANES):
                col = lambda grp: slice(grp * A + c0, grp * A + c0 + LANES)
                a_val, sg = z_ref[rs, col(0)], _sigmoid(z_ref[rs, col(1)])
                dglu = _conv_corr(dac_buf, ca_ref, a_val * sg, dca_acc, r0, rows, c0, LANES, boffs_a)
                dz_ref[rs, col(0)] = (dglu * sg).astype(BF16)
                dz_ref[rs, col(1)] = (dglu * a_val * sg * (1.0 - sg)).astype(BF16)
                c_gate, bc_val = z_ref[rs, col(3)], z_ref[rs, col(4)]
                dcv = _conv_corr(dbc_buf, cb_ref, c_gate * bc_val, dcb_acc, r0, rows, c0, LANES, boffs_s)
                dz_ref[rs, col(3)] = (dcv * bc_val).astype(BF16)
                dz_ref[rs, col(4)] = (dcv * c_gate).astype(BF16)
            return carry

        lax.fori_loop(0, ts // rows, chunk, 0)
        dac_buf[ts:ts + HALO_A, :] = dac_buf[0:HALO_A, :]
        dbc_buf[ts:ts + HALO_S, :] = dbc_buf[0:HALO_S, :]

        @pl.when(i == n - 1)
        def _():
            _finish_tap_sums(dca_acc, dca_ref, K_A)
            _finish_tap_sums(dcb_acc, dcb_ref, K_S)

    rev = lambda w: pl.BlockSpec((ts, w), lambda i: (n - 1 - i, 0))
    full = lambda r, c: pl.BlockSpec((r, c), lambda i: (0, 0))
    return _call(
        body, name="mix0_bwd_a", grid=(n,), comm=comm, args=(dx1, z, ac, bconv, w_out, conv_a, ln_g, ln_b, conv_b),
        in_specs=[rev(D), rev(NZ), rev(A), rev(A), _const((2 * A, D)), _const((K_A, A)), _const((1, A)), _const((1, A)),
                  _const((K_S, A))],
        out_specs=[rev(NZ), full(K_A, A), full(1, A), full(1, A), full(K_S, A)],
        out_shape=[jax.ShapeDtypeStruct((s, NZ), BF16), jax.ShapeDtypeStruct((K_A, A), F32), jax.ShapeDtypeStruct((1, A), F32),
                   jax.ShapeDtypeStruct((1, A), F32), jax.ShapeDtypeStruct((K_S, A), F32)],
        scratch=[pltpu.VMEM((ts + HALO_A, A), F32), pltpu.VMEM((ts + HALO_S, A), F32), pltpu.VMEM((SUBLANES * K_A, A), F32),
                 pltpu.VMEM((SUBLANES * K_S, A), F32)])


def _train_step(x, target, place, shard, rep, small_pack, small_shapes):
    bw_up, bw_in = FF2 // NCHIP, NZ // NCHIP
    five = lambda a, k: a.reshape(NCHIP, 2, *HALF[k])

    g_in, g_out, small_g = _run_comm(_gather_comm([shard["w_in"], shard["w_out"]], small_pack), "ag_first")
    whole = {}
    for k, part in zip(SMALL_SHARDED, _unpack(small_g, small_shapes, lead=(NCHIP,))):
        whole[k] = jnp.moveaxis(part, 0, 1).reshape(part.shape[1], NCHIP * part.shape[2])
    w_in, w_out = g_in.reshape(NCHIP, D, bw_in), g_out.reshape(2 * A, D)
    conv_ffn = whole["conv_ffn_w"].reshape(2, K_S, FF2)
    nffn = [rep["norm_ffn"][0:1], rep["norm_ffn"][1:2]]

    (h0, z, ac, bconv, cat, x1), (g_up0, g_dn0) = _mix0_fwd(
        x, rep["norm_mix_even"], w_in, whole["conv_a"], rep["ln_a_g"], rep["ln_a_b"], whole["conv_b"], w_out,
        comm=_gather_comm([shard["w_up0"], shard["w_down0"]]))
    w_up0, w_dn0 = g_up0.reshape(NCHIP, D, bw_up), g_dn0.reshape(FF, D)
    (hf0, u00, u0, act0, x2), (g_pool, g_up1, g_dn1) = _ffn_fwd(
        x1, nffn[0], w_up0, conv_ffn[0], w_dn0, "ffn0_fwd",
        comm=_gather_comm([shard["w_pool"], shard["w_up1"], shard["w_down1"]]))
    w_pool = g_pool.reshape(NCHIP, len(POOL_WINDOWS), PG // NCHIP, PG)
    w_up1, w_dn1 = g_up1.reshape(NCHIP, D, bw_up), g_dn1.reshape(FF, D)
    x3 = _pool_fwd(x2, whole["norm_mix_odd"], w_pool, whole["pool_scale"])
    (hf1, u01, u1, act1, x4), _ = _ffn_fwd(x3, nffn[1], w_up1, conv_ffn[1], w_dn1, "ffn1_fwd")
    dx4, g_nfin, loss_part = _final_loss(x4, rep["norm_final"], target)

    def wgrads_ffn(hf, du0, act, dxo, tag):
        g_up, _ = _wgrad(hf, du0, NCHIP, "wgrad_up" + tag)
        g_dn, _ = _wgrad(act, dxo, 1, "wgrad_down" + tag)
        return five(g_up, "w_up"), five(g_dn, "w_down")

    def pair_sums(keys, grads, landed, tag):
        return [_pair_sum(place, g, ld, "pair_sum_" + k + tag) for k, g, ld in zip(keys, grads, landed)]

    (du01, g_wc1), _ = _ffn_bwd_a(dx4, u1, u01, conv_ffn[1], w_dn1, "ffn1_bwd_a")
    grads1 = wgrads_ffn(hf1, du01, act1, dx4, "1")
    (dx3, g_nf1), landed1 = _nt_rms_bwd(du01, w_up1, x3, nffn[1], dx4, "ffn1_bwd_b", comm=_pair_comm(grads1))
    sums1 = pair_sums(("w_up", "w_down"), grads1, landed1, "1")
    (dx2, g_wpool, g_scale, g_nmo), _ = _pool_bwd(dx3, x2, whole["norm_mix_odd"], w_pool, whole["pool_scale"])
    (du00, g_wc0), parts1 = _ffn_bwd_a(dx2, u0, u00, conv_ffn[0], w_dn0, "ffn0_bwd_a", comm=_chips_comm(sums1))
    grads0 = wgrads_ffn(hf0, du00, act0, dx2, "0") + (five(g_wpool, "w_pool"),)
    (dx1, g_nf0), landed0 = _nt_rms_bwd(du00, w_up0, x1, nffn[0], dx2, "ffn0_bwd_b", comm=_pair_comm(grads0))
    sums0 = pair_sums(("w_up", "w_down", "w_pool"), grads0, landed0, "0")
    (dz, g_ca, g_lg, g_lb, g_cb), parts0 = _mix0_bwd_a(
        dx1, z, ac, bconv, w_out, whole["conv_a"], rep["ln_a_g"], rep["ln_a_b"], whole["conv_b"], comm=_chips_comm(sums0))
    gr_out, _ = _wgrad(cat, dx1, 1, "wgrad_out")
    gr_in, _ = _wgrad(h0, dz, NCHIP, "wgrad_in")
    gradsm = (five(gr_in, "w_in"), five(gr_out, "w_out"))
    (grad_x, g_nme), landedm = _nt_rms_bwd(dz, w_in, x, rep["norm_mix_even"], dx1, "mix0_bwd_b", comm=_pair_comm(gradsm))
    sumsm = pair_sums(("w_in", "w_out"), gradsm, landedm, "")

    small = {"norm_mix_even": g_nme, "conv_a": g_ca, "ln_a_g": g_lg, "ln_a_b": g_lb, "conv_b": g_cb, "norm_mix_odd": g_nmo,
             "pool_scale": g_scale, "norm_ffn": jnp.concatenate([g_nf0, g_nf1], axis=0),
             "conv_ffn_w": jnp.stack([g_wc0, g_wc1]), "norm_final": g_nfin}
    *partsm, small_all = _run_comm(_chips_comm(sumsm, _pack([small[k] for k in SMALL_ALL] + [loss_part])), "rs_last")

    tot = lambda k, g, ld, p, layer=0, nl=1, prev=None: _chip_sum(place, g, ld, p, layer, nl, "chip_sum_%s%d" % (k, layer), prev)
    t_in = tot("w_in", gradsm[0], landedm[0], partsm[0])
    t_out = tot("w_out", gradsm[1], landedm[1], partsm[1])
    t_pool = tot("w_pool", grads0[2], landed0[2], parts0[2])
    t_up = tot("w_up", grads1[0], landed1[0], parts1[0], 1, 2)
    t_up = tot("w_up", grads0[0], landed0[0], parts0[0], 0, 2, t_up)
    t_dn = tot("w_down", grads1[1], landed1[1], parts1[1], 1, 2)
    t_dn = tot("w_down", grads0[1], landed0[1], parts0[1], 0, 2, t_dn)
    swapped = _run_comm(_swap_comm([t_in, t_out, t_pool, t_up, t_dn]), "rs_swap")
    return grad_x, dict(zip(BIG, swapped)), _sum_devices(small_all), [small[k].shape for k in SMALL_ALL] + [(1, LANES)]


def _adamw_math(w, g, m, v):
    m = ADAM_B1 * m + (1.0 - ADAM_B1) * g
    v = ADAM_B2 * v + (1.0 - ADAM_B2) * (g * g)
    m_hat = m / (1.0 - ADAM_B1 ** ADAM_STEP)
    v_hat = v / (1.0 - ADAM_B2 ** ADAM_STEP)
    return -ADAM_LR * (m_hat / (jnp.sqrt(v_hat) + ADAM_EPS) + ADAM_WD * w), m, v


def _adamw_big(w, g, m, v, tr, name):
    nl, rows, cols = w.shape

    def body(w_ref, g_ref, m_ref, v_ref, g2_ref, d_ref, m2_ref, v2_ref):
        gv = g_ref[...]
        g2_ref[...] = gv
        d_ref[...], m2_ref[...], v2_ref[...] = _adamw_math(w_ref[...], gv, m_ref[...], v_ref[...])

    spec = pl.BlockSpec((None, tr, cols), lambda l, r: (l, r, 0))
    return pl.pallas_call(
        body, name=name, grid=(nl, rows // tr), in_specs=[spec] * 4, out_specs=[spec] * 4,
        out_shape=[jax.ShapeDtypeStruct(w.shape, F32)] * 4, compiler_params=_cparams(2),
    )(w, g, m, v)


def _adamw_small(ws, gs, ms, vs):
    n = len(ws)

    def body(*refs):
        for p in range(n):
            w_ref, g_ref, m_ref, v_ref = (refs[q * n + p] for q in range(4))
            d_ref, m2_ref, v2_ref = (refs[(4 + q) * n + p] for q in range(3))
            d_ref[...], m2_ref[...], v2_ref[...] = _adamw_math(w_ref[...], g_ref[...], m_ref[...], v_ref[...])

    whole = lambda a: pl.BlockSpec(a.shape, lambda: (0,) * a.ndim)
    outs = pl.pallas_call(
        body, name="adamw_small", in_specs=[whole(a) for a in ws] * 4, out_specs=[whole(a) for a in ws] * 3,
        out_shape=[jax.ShapeDtypeStruct(a.shape, F32) for a in ws] * 3,
        compiler_params=pltpu.CompilerParams(vmem_limit_bytes=VMEM_LIMIT_BYTES),
    )(*ws, *gs, *ms, *vs)
    return outs[0:n], outs[n:2 * n], outs[2 * n:3 * n]


def _place():
    x, y, c = lax.axis_index("x"), lax.axis_index("y"), lax.axis_index("c")
    chips = [(x, 1 - y), (1 - x, y), (1 - x, 1 - y)]
    blocks = [2 * cx + cy for cx, cy in chips]
    return x, y, c, 2 * x + y, chips, blocks


def _gather_comm(shards, small=None):
    na = len(shards)
    nk = NCHIP - 1
    ns = 0 if small is None else 1

    def copies(ins, outs, sems):
        ici_send, ici_recv, fwd_send, fwd_recv, own_send, own_recv = sems[:6]
        x, y, c, j, chips, blocks = _place()
        sib = (x, y, 1 - c)

        def ici(a, k, arrival):
            dst = outs[a].at[blocks[k], c] if arrival else outs[a].at[j, c]
            return pltpu.make_async_remote_copy(
                src_ref=dst if arrival else ins[a].at[c], dst_ref=dst, send_sem=ici_send.at[a * nk + k],
                recv_sem=ici_recv.at[a * nk + k], device_id=(*chips[k], c), device_id_type=MESH)

        def fwd(a, k, half):
            ref = outs[a].at[blocks[k], half]
            return pltpu.make_async_remote_copy(
                src_ref=ref, dst_ref=ref, send_sem=fwd_send.at[a * nk + k], recv_sem=fwd_recv.at[a * nk + k],
                device_id=sib, device_id_type=MESH)

        own = [pltpu.make_async_remote_copy(src_ref=ins[a], dst_ref=outs[a].at[j], send_sem=own_send.at[a],
                                            recv_sem=own_recv.at[a], device_id=sib, device_id_type=MESH) for a in range(na)]
        small_copies = [pltpu.make_async_remote_copy(
            src_ref=ins[na], dst_ref=outs[na].at[j], send_sem=ici_send.at[na * nk + k], recv_sem=ici_recv.at[na * nk + k],
            device_id=(*chips[k], c), device_id_type=MESH) for k in range(nk * ns)]
        local = [pltpu.make_async_copy(ins[na], outs[na].at[j], sems[6])] if ns else []
        return ici, fwd, own, small_copies, local, c

    def start(ins, outs, sems):
        ici, _, own, small_copies, local, _ = copies(ins, outs, sems)
        for cp in local + own + [ici(a, k, False) for a in range(na) for k in range(nk)] + small_copies:
            cp.start()

    def finish(ins, outs, sems):
        ici, fwd, own, small_copies, local, c = copies(ins, outs, sems)
        for a in range(na):
            for k in range(nk):
                ici(a, k, True).wait_recv()
                fwd(a, k, c).start()
        for cp in small_copies:
            cp.wait()
        for a in range(na):
            for k in range(nk):
                ici(a, k, False).wait_send()
                fwd(a, k, c).wait_send()
                fwd(a, k, 1 - c).wait_recv()
        for cp in own + local:
            cp.wait()

    out_shapes = [jax.ShapeDtypeStruct((NCHIP,) + s.shape, s.dtype) for s in shards]
    sems = [pltpu.SemaphoreType.DMA((na * nk + nk * ns,)), pltpu.SemaphoreType.DMA((na * nk + nk * ns,)),
            pltpu.SemaphoreType.DMA((na * nk,)), pltpu.SemaphoreType.DMA((na * nk,)),
            pltpu.SemaphoreType.DMA((na,)), pltpu.SemaphoreType.DMA((na,))]
    if ns:
        out_shapes.append(jax.ShapeDtypeStruct((NCHIP,) + small.shape, small.dtype))
        sems.append(pltpu.SemaphoreType.DMA)
    return _Comm(list(shards) + [small] * ns, out_shapes, sems, start, finish)


def _simple_comm(inputs, out_shapes, make_copies, n_sems, aliases=None):
    def start(ins, outs, sems):
        for cp in make_copies(ins, outs, sems):
            cp.start()

    def finish(ins, outs, sems):
        for cp in make_copies(ins, outs, sems):
            cp.wait()

    return _Comm(inputs, out_shapes, [pltpu.SemaphoreType.DMA((n,)) for n in n_sems], start, finish, aliases)


def _pair_comm(grads):
    def make_copies(ins, outs, sems):
        x, y, c, _, _, _ = _place()
        return [pltpu.make_async_remote_copy(
            src_ref=ins[a].at[:, 1 - c], dst_ref=outs[a], send_sem=sems[0].at[a], recv_sem=sems[1].at[a],
            device_id=(x, y, 1 - c), device_id_type=MESH) for a in range(len(grads))]

    out_shapes = [jax.ShapeDtypeStruct(g.shape[:1] + g.shape[2:], F32) for g in grads]
    return _simple_comm(grads, out_shapes, make_copies, [len(grads)] * 2)


def _chips_comm(sums, small=None):
    na = len(sums)
    nk = NCHIP - 1

    def make_copies(ins, outs, sems):
        x, y, c, _, chips, blocks = _place()
        copies = [pltpu.make_async_remote_copy(
            src_ref=ins[a].at[blocks[k]], dst_ref=outs[a].at[k], send_sem=sems[0].at[a * nk + k],
            recv_sem=sems[1].at[a * nk + k], device_id=(*chips[k], c), device_id_type=MESH)
            for a in range(na) for k in range(nk)]
        if small is not None:
            me = 4 * x + 2 * y + c
            for r in range(1, NDEV):
                peer = (1 - x if r & 4 else x, 1 - y if r & 2 else y, 1 - c if r & 1 else c)
                copies.append(pltpu.make_async_remote_copy(
                    src_ref=ins[na], dst_ref=outs[na].at[me], send_sem=sems[2].at[r - 1], recv_sem=sems[3].at[r - 1],
                    device_id=peer, device_id_type=MESH))
            copies.append(pltpu.make_async_copy(ins[na], outs[na].at[me], sems[4].at[0]))
        return copies

    out_shapes = [jax.ShapeDtypeStruct((nk,) + g.shape[1:], BF16) for g in sums]
    if small is None:
        return _simple_comm(sums, out_shapes, make_copies, [na * nk] * 2)
    out_shapes.append(jax.ShapeDtypeStruct((NDEV,) + small.shape, F32))
    return _simple_comm(list(sums) + [small], out_shapes, make_copies, [na * nk] * 2 + [NDEV - 1] * 2 + [1])


def _swap_comm(totals):
    def make_copies(ins, outs, sems):
        x, y, c, _, _, _ = _place()
        return [pltpu.make_async_remote_copy(
            src_ref=outs[a].at[:, c], dst_ref=outs[a].at[:, c], send_sem=sems[0].at[a], recv_sem=sems[1].at[a],
            device_id=(x, y, 1 - c), device_id_type=MESH) for a in range(len(totals))]

    out_shapes = [jax.ShapeDtypeStruct(t.shape, F32) for t in totals]
    return _simple_comm(totals, out_shapes, make_copies, [len(totals)] * 2, aliases={a: a for a in range(len(totals))})


def _pair_sum(place, grad, landed, name):
    _, _, rows, cols = grad.shape

    def body(place_ref, g_ref, l_ref, o_ref):
        o_ref[...] = (g_ref[...] + l_ref[...]).astype(BF16)

    return pl.pallas_call(
        body, name=name,
        grid_spec=pltpu.PrefetchScalarGridSpec(
            num_scalar_prefetch=1, grid=(NCHIP,),
            in_specs=[pl.BlockSpec((None, None, rows, cols), lambda b, p: (b, p[1], 0, 0)),
                      pl.BlockSpec((None, rows, cols), lambda b, p: (b, 0, 0))],
            out_specs=pl.BlockSpec((None, rows, cols), lambda b, p: (b, 0, 0))),
        out_shape=jax.ShapeDtypeStruct((NCHIP, rows, cols), BF16), compiler_params=_cparams(1),
    )(place, grad, landed)


def _chip_sum(place, grad, landed, parts, layer, n_layers, name, prev=None):
    _, _, rows, cols = grad.shape

    def body(*refs):
        g_ref, l_ref, p_ref, o_ref = refs[1], refs[2], refs[3], refs[-1]
        tot = g_ref[...] + l_ref[...]
        for k in range(NCHIP - 1):
            tot = tot + p_ref[k].astype(F32)
        o_ref[...] = tot

    in_specs = [pl.BlockSpec((None, None, rows, cols), lambda i, p: (p[0], p[1], 0, 0)),
                pl.BlockSpec((None, rows, cols), lambda i, p: (p[0], 0, 0)),
                pl.BlockSpec((NCHIP - 1, rows, cols), lambda i, p: (0, 0, 0))]
    args = [place, grad, landed, parts]
    if prev is not None:
        in_specs.append(ANY)
        args.append(prev)
    return pl.pallas_call(
        body, name=name,
        grid_spec=pltpu.PrefetchScalarGridSpec(
            num_scalar_prefetch=1, grid=(1,), in_specs=in_specs,
            out_specs=pl.BlockSpec((None, None, rows, cols), lambda i, p: (layer, p[1], 0, 0))),
        out_shape=jax.ShapeDtypeStruct((n_layers, 2, rows, cols), F32),
        input_output_aliases={} if prev is None else {4: 0}, compiler_params=_cparams(1),
    )(*args)


def _sum_devices(parts):
    def body(p_ref, o_ref):
        tot = p_ref[0]
        for d in range(1, NDEV):
            tot = tot + p_ref[d]
        o_ref[...] = tot

    return pl.pallas_call(
        body, name="sum_small", in_specs=[pl.BlockSpec(parts.shape, lambda: (0, 0, 0))],
        out_specs=pl.BlockSpec(parts.shape[1:], lambda: (0, 0)), out_shape=jax.ShapeDtypeStruct(parts.shape[1:], F32),
    )(parts)


def _pack(parts):
    rows = []
    for p in parts:
        p = p.reshape(-1, LANES)
        rows.append(jnp.pad(p, ((0, -p.shape[0] % SUBLANES), (0, 0))))
    return jnp.concatenate(rows, axis=0)


def _unpack(buf, shapes, lead=()):
    out, r0 = [], 0
    nl = len(lead)
    for shp in shapes:
        nrow = 1
        for d in shp:
            nrow *= d
        nrow //= LANES
        out.append(buf[(slice(None),) * nl + (slice(r0, r0 + nrow),)].reshape(lead + tuple(shp)))
        r0 += nrow + (-nrow % SUBLANES)
    return out


WEIGHT_ORDER = ("norm_mix_even", "w_in", "conv_a", "ln_a_g", "ln_a_b", "conv_b", "w_out", "norm_mix_odd", "w_pool",
                "pool_scale", "norm_ffn", "w_up", "conv_ffn_w", "w_down", "norm_final")
BIG = ("w_in", "w_out", "w_pool", "w_up", "w_down")
HALF = {"w_in": (D // 2, NZ // NCHIP), "w_out": (2 * A // NCHIP // 2, D), "w_pool": (PG // 2, PG),
        "w_up": (D // 2, FF2 // NCHIP), "w_down": (FF // NCHIP // 2, D)}
SMALL_SHARDED = ("conv_a", "conv_b", "conv_ffn_w", "norm_mix_odd", "pool_scale")
SMALL_ALL = ("norm_mix_even", "conv_a", "ln_a_g", "ln_a_b", "conv_b", "norm_mix_odd", "pool_scale", "norm_ffn", "conv_ffn_w",
             "norm_final")


def kernel(x, norm_mix_even, w_in, conv_a, ln_a_g, ln_a_b, conv_b, w_out, norm_mix_odd, w_pool, pool_scale, norm_ffn, w_up, conv_ffn_w, w_down, norm_final, loss_target, m_norm_mix_even, m_w_in, m_conv_a, m_ln_a_g, m_ln_a_b, m_conv_b, m_w_out, m_norm_mix_odd, m_w_pool, m_pool_scale, m_norm_ffn, m_w_up, m_conv_ffn_w, m_w_down, m_norm_final, v_norm_mix_even, v_w_in, v_conv_a, v_ln_a_g, v_ln_a_b, v_conv_b, v_w_out, v_norm_mix_odd, v_w_pool, v_pool_scale, v_norm_ffn, v_w_up, v_conv_ffn_w, v_w_down, v_norm_final):
    w = dict(norm_mix_even=norm_mix_even, w_in=w_in, conv_a=conv_a, ln_a_g=ln_a_g, ln_a_b=ln_a_b, conv_b=conv_b, w_out=w_out,
             norm_mix_odd=norm_mix_odd, w_pool=w_pool, pool_scale=pool_scale, norm_ffn=norm_ffn, w_up=w_up,
             conv_ffn_w=conv_ffn_w, w_down=w_down, norm_final=norm_final)
    m = dict(norm_mix_even=m_norm_mix_even, w_in=m_w_in, conv_a=m_conv_a, ln_a_g=m_ln_a_g, ln_a_b=m_ln_a_b, conv_b=m_conv_b,
             w_out=m_w_out, norm_mix_odd=m_norm_mix_odd, w_pool=m_w_pool, pool_scale=m_pool_scale, norm_ffn=m_norm_ffn,
             w_up=m_w_up, conv_ffn_w=m_conv_ffn_w, w_down=m_w_down, norm_final=m_norm_final)
    v = dict(norm_mix_even=v_norm_mix_even, w_in=v_w_in, conv_a=v_conv_a, ln_a_g=v_ln_a_g, ln_a_b=v_ln_a_b, conv_b=v_conv_b,
             w_out=v_w_out, norm_mix_odd=v_norm_mix_odd, w_pool=v_w_pool, pool_scale=v_pool_scale, norm_ffn=v_norm_ffn,
             w_up=v_w_up, conv_ffn_w=v_conv_ffn_w, w_down=v_w_down, norm_final=v_norm_final)
    chip = 2 * lax.axis_index("x") + lax.axis_index("y")
    place = jnp.stack([chip, lax.axis_index("c")]).astype(jnp.int32)

    half = lambda a, name: a.astype(BF16).reshape((2,) + HALF[name])
    shard = {"w_in": half(w_in[0], "w_in"), "w_out": half(w_out[0], "w_out"), "w_pool": half(w_pool[0], "w_pool"),
             "w_up0": half(w_up[0], "w_up"), "w_up1": half(w_up[1], "w_up"),
             "w_down0": half(w_down[0], "w_down"), "w_down1": half(w_down[1], "w_down")}
    small_shapes = [w[k].shape[-2:] if w[k].ndim == 3 and k != "conv_ffn_w" else (w[k].size // w[k].shape[-1], w[k].shape[-1])
                    for k in SMALL_SHARDED]
    rep = dict(norm_mix_even=norm_mix_even, ln_a_g=ln_a_g, ln_a_b=ln_a_b, norm_ffn=norm_ffn, norm_final=norm_final.reshape(1, D))
    grad_x, swapped, small_packed, small_full_shapes = _train_step(
        x[0], loss_target[0], place, shard, rep, _pack([w[k] for k in SMALL_SHARDED]), small_shapes)
    small_sum = _unpack(small_packed, small_full_shapes)
    loss = small_sum[-1][0, 0]

    grad = {k: swapped[k].reshape(w[k].shape) for k in BIG}
    for k, gsum in zip(SMALL_ALL, small_sum):
        if k in SMALL_SHARDED:
            cols = w[k].shape[-1]
            gsum = lax.dynamic_slice_in_dim(gsum, chip * cols, cols, axis=gsum.ndim - 1)
        grad[k] = gsum.reshape(w[k].shape)

    delta, new_m, new_v = {}, {}, {}
    rows_per_step = {"w_in": 512, "w_out": 256, "w_pool": 256, "w_up": 256, "w_down": 352}
    for k in BIG:
        as3 = lambda a: a.reshape(a.shape[0], -1, a.shape[-1])
        g3, d3, m3, v3 = _adamw_big(as3(w[k]), as3(grad[k]), as3(m[k]), as3(v[k]), rows_per_step[k], "adamw_" + k)
        grad[k], delta[k], new_m[k], new_v[k] = (a.reshape(w[k].shape) for a in (g3, d3, m3, v3))
    as2 = lambda a: a.reshape(-1, a.shape[-1])
    ds, ms, vs = _adamw_small(*[[as2(t[k]) for k in SMALL_ALL] for t in (w, grad, m, v)])
    for k, d2, m2, v2 in zip(SMALL_ALL, ds, ms, vs):
        delta[k], new_m[k], new_v[k] = (a.reshape(w[k].shape) for a in (d2, m2, v2))

    return (loss, grad_x[None], *[grad[k] for k in WEIGHT_ORDER], *[delta[k] for k in WEIGHT_ORDER],
            *[new_m[k] for k in WEIGHT_ORDER], *[new_v[k] for k in WEIGHT_ORDER])
```

```python
import functools

import jax
import jax.numpy as jnp
from jax import lax
from jax.experimental import pallas as pl
from jax.experimental.pallas import tpu as pltpu

F32, BF16 = jnp.float32, jnp.bfloat16

D = 1024
A = 512
NZ = 5 * A
FF = 2816
FF2 = 2 * FF
NCHIP = 4
NDEV = 8
K_A, K_S = 31, 3
POOL_WINDOWS = (2, 4, 8, 16)
PG = D // len(POOL_WINDOWS)
RMS_EPS, LN_EPS = 1e-6, 1e-5
ADAM_LR, ADAM_B1, ADAM_B2, ADAM_EPS, ADAM_WD, ADAM_STEP = 0.001, 0.9, 0.999, 1e-08, 0.01, 10

HALO_A, HALO_S, HALO_P = 32, 8, 16
SUBLANES = 8
LANES = 128
VMEM_LIMIT_BYTES = 56 * 1024 * 1024

TS_MIX = 512
TS_MIXB = 256
TS_FFN = 256
TS_POOL = 512
TS_MM = 512
R_CHUNK = 64
MID_STEPS_BEFORE_END = 1

MESH = pl.DeviceIdType.MESH
ANY = pl.BlockSpec(memory_space=pl.ANY)
NT_DIMS = (((1,), (1,)), ((), ()))
TN_DIMS = (((0,), (0,)), ((), ()))


def _cparams(n_axes):
    return pltpu.CompilerParams(dimension_semantics=("arbitrary",) * n_axes, vmem_limit_bytes=VMEM_LIMIT_BYTES)


def _const(shape):
    nd = len(shape)
    return pl.BlockSpec(shape, lambda *_: (0,) * nd, pipeline_mode=pl.Buffered(1))


def _sigmoid(v):
    return 1.0 / (1.0 + jnp.exp(-v))


def _rsqrt_mean_sq(x):
    return lax.rsqrt(jnp.mean(x * x, axis=-1, keepdims=True) + RMS_EPS)


def _rms_bwd(dh, xh, r, g):
    dxh = dh * g
    return r * (dxh - xh * jnp.mean(dxh * xh, axis=-1, keepdims=True))


def _shifted(buf_ref, row0, rows, col0, width, offsets):
    lo = (min(offsets) // SUBLANES) * SUBLANES
    hi = -(-(max(offsets) + rows) // SUBLANES) * SUBLANES
    start = row0 + lo if isinstance(row0, int) else pl.multiple_of(row0 + lo, SUBLANES)
    win = buf_ref[pl.ds(start, hi - lo), col0:col0 + width]
    out = {}
    for res in sorted({(o - lo) % SUBLANES for o in offsets}):
        qs = {o: (o - lo) // SUBLANES for o in offsets if (o - lo) % SUBLANES == res}
        base = pltpu.roll(win, hi - lo - res, 0) if res else win
        for o, q in qs.items():
            out[o] = base[SUBLANES * q:SUBLANES * q + rows, :]
    return out


def _rowsum8(v):
    acc = v[0:SUBLANES, :]
    for r in range(SUBLANES, v.shape[0], SUBLANES):
        acc = acc + v[r:r + SUBLANES, :]
    return acc


def _taps(w_ref, sh, offsets, col0, width):
    acc = None
    for k, o in enumerate(offsets):
        term = w_ref[k:k + 1, col0:col0 + width] * sh[o]
        acc = term if acc is None else acc + term
    return acc


def _window_bases(buf_ref, row0, rows, col0, width, offsets):
    lo = (min(offsets) // SUBLANES) * SUBLANES
    hi = -(-(max(offsets) + rows) // SUBLANES) * SUBLANES
    start = row0 + lo if isinstance(row0, int) else pl.multiple_of(row0 + lo, SUBLANES)
    win = buf_ref[pl.ds(start, hi - lo), col0:col0 + width]
    for res in sorted({(o - lo) % SUBLANES for o in offsets}):
        taps = [(k, (o - lo) // SUBLANES * SUBLANES) for k, o in enumerate(offsets) if (o - lo) % SUBLANES == res]
        yield (pltpu.roll(win, hi - lo - res, 0) if res else win), taps


def _conv_acc(buf_ref, w_ref, row0, rows, col0, width, offsets):
    acc = None
    for base, taps in _window_bases(buf_ref, row0, rows, col0, width, offsets):
        for k, q in taps:
            term = w_ref[k:k + 1, col0:col0 + width] * base[q:q + rows, :]
            acc = term if acc is None else acc + term
    return acc


def _conv_corr(buf_ref, w_ref, other, acc_ref, row0, rows, col0, width, offsets):
    acc = None
    for base, taps in _window_bases(buf_ref, row0, rows, col0, width, offsets):
        for k, q in taps:
            sl = base[q:q + rows, :]
            term = w_ref[k:k + 1, col0:col0 + width] * sl
            acc = term if acc is None else acc + term
            acc_ref[SUBLANES * k:SUBLANES * (k + 1), col0:col0 + width] += _rowsum8(sl * other)
    return acc


def _finish_tap_sums(acc_ref, out_ref, n_taps):
    for k in range(n_taps):
        out_ref[k:k + 1, :] = jnp.sum(acc_ref[SUBLANES * k:SUBLANES * (k + 1), :], axis=0, keepdims=True)


class _Comm:
    def __init__(self, inputs, out_shapes, sems, start, finish, aliases=None, mid=None):
        self.inputs, self.out_shapes, self.sems = list(inputs), list(out_shapes), list(sems)
        self.start, self.finish, self.mid, self.aliases = start, finish, mid, dict(aliases or {})


def _join_comm(a, b):
    ni, no, ns = len(a.inputs), len(a.out_shapes), len(a.sems)

    def both(name):
        def run(ins, outs, sems):
            if getattr(a, name) is not None:
                getattr(a, name)(ins[:ni], outs[:no], sems[:ns])
            if getattr(b, name) is not None:
                getattr(b, name)(ins[ni:], outs[no:], sems[ns:])
        return run

    aliases = {**a.aliases, **{ni + i: no + o for i, o in b.aliases.items()}}
    mid = both("mid") if (a.mid is not None or b.mid is not None) else None
    return _Comm(a.inputs + b.inputs, a.out_shapes + b.out_shapes, a.sems + b.sems, both("start"), both("finish"), aliases, mid)


def _call(body, *, name, grid, in_specs, out_specs, out_shape, args, scratch=(), comm=None, aliases=None):
    n_in, n_out, n_scr, n_axes = len(in_specs), len(out_specs), len(scratch), len(grid)
    params = pltpu.CompilerParams(dimension_semantics=("arbitrary",) * n_axes, vmem_limit_bytes=VMEM_LIMIT_BYTES)
    aliases = dict(aliases or {})
    if comm is None:
        outs = pl.pallas_call(body, name=name, grid=grid, in_specs=list(in_specs), out_specs=list(out_specs),
                              out_shape=list(out_shape), scratch_shapes=list(scratch), input_output_aliases=aliases,
                              compiler_params=params)(*args)
        return list(outs), []
    ci, co = len(comm.inputs), len(comm.out_shapes)

    def wrapped(*refs):
        k_in, c_in = refs[:n_in], refs[n_in:n_in + ci]
        o0 = n_in + ci
        k_out, c_out = refs[o0:o0 + n_out], refs[o0 + n_out:o0 + n_out + co]
        s0 = o0 + n_out + co
        k_scr, c_sem = refs[s0:s0 + n_scr], refs[s0 + n_scr:]
        first = pl.program_id(0) == 0
        last = pl.program_id(0) == grid[0] - 1
        for ax in range(1, n_axes):
            first = jnp.logical_and(first, pl.program_id(ax) == 0)
            last = jnp.logical_and(last, pl.program_id(ax) == grid[ax] - 1)

        @pl.when(first)
        def _():
            comm.start(c_in, c_out, c_sem)

        if comm.mid is not None and n_axes == 1 and grid[0] > MID_STEPS_BEFORE_END:
            @pl.when(pl.program_id(0) == grid[0] - 1 - MID_STEPS_BEFORE_END)
            def _():
                comm.mid(c_in, c_out, c_sem)

        body(*k_in, *k_out, *k_scr)

        @pl.when(last)
        def _():
            if comm.mid is not None and not (n_axes == 1 and grid[0] > MID_STEPS_BEFORE_END):
                comm.mid(c_in, c_out, c_sem)
            comm.finish(c_in, c_out, c_sem)

    outs = pl.pallas_call(
        wrapped, name=name, grid=grid, in_specs=list(in_specs) + [ANY] * ci, out_specs=list(out_specs) + [ANY] * co,
        out_shape=list(out_shape) + comm.out_shapes, scratch_shapes=list(scratch) + comm.sems,
        input_output_aliases={**aliases, **{n_in + i: n_out + o for i, o in comm.aliases.items()}}, compiler_params=params,
    )(*args, *comm.inputs)
    return list(outs[:n_out]), list(outs[n_out:])


def _run_comm(comm, name):
    ci, co = len(comm.inputs), len(comm.out_shapes)

    def body(*refs):
        c_in, c_out, c_sem = refs[:ci], refs[ci:ci + co], refs[ci + co:]
        comm.start(c_in, c_out, c_sem)
        if comm.mid is not None:
            comm.mid(c_in, c_out, c_sem)
        comm.finish(c_in, c_out, c_sem)

    return list(pl.pallas_call(body, name=name, in_specs=[ANY] * ci, out_specs=[ANY] * co, out_shape=comm.out_shapes,
                               scratch_shapes=comm.sems, input_output_aliases=comm.aliases)(*comm.inputs))


def _mix0_fwd(x, g, w_in, conv_a, ln_g, ln_b, conv_b, w_out, comm=None):
    s = x.shape[0]
    ts = min(TS_MIX, s)
    n = s // ts
    bw = NZ // NCHIP
    offs_a = [HALO_A - (K_A - 1) + k for k in range(K_A)]
    offs_s = [HALO_S - (K_S - 1) + k for k in range(K_S)]

    def body(x_ref, g_ref, win_ref, ca_ref, lg_ref, lb_ref, cb_ref, wout_ref,
             h_ref, z_ref, ac_ref, bconv_buf, cat_ref, x1_ref, glu_buf, cv_buf):
        i = pl.program_id(0)

        @pl.when(i == 0)
        def _():
            glu_buf[0:HALO_A, :] = jnp.zeros((HALO_A, A), F32)
            cv_buf[0:HALO_S, :] = jnp.zeros((HALO_S, A), F32)

        xv = x_ref[...]
        h = (xv * _rsqrt_mean_sq(xv) * g_ref[...]).astype(BF16)
        h_ref[...] = h
        for j in range(NCHIP):
            z_ref[:, j * bw:(j + 1) * bw] = jnp.dot(h, win_ref[j], preferred_element_type=F32)
        glu_buf[HALO_A:HALO_A + ts, :] = z_ref[:, 0:A] * _sigmoid(z_ref[:, A:2 * A])
        cv_buf[HALO_S:HALO_S + ts, :] = z_ref[:, 3 * A:4 * A] * z_ref[:, 4 * A:5 * A]

        def chunk(ci, carry):
            r0 = pl.multiple_of(ci * R_CHUNK, R_CHUNK)
            for c0 in range(0, A, LANES):
                ac_ref[pl.ds(r0, R_CHUNK), c0:c0 + LANES] = _conv_acc(glu_buf, ca_ref, r0, R_CHUNK, c0, LANES, offs_a)
                bconv_buf[pl.ds(r0, R_CHUNK), c0:c0 + LANES] = _conv_acc(cv_buf, cb_ref, r0, R_CHUNK, c0, LANES, offs_s)
            return carry

        lax.fori_loop(0, ts // R_CHUNK, chunk, 0)
        glu_buf[0:HALO_A, :] = glu_buf[ts:ts + HALO_A, :]
        cv_buf[0:HALO_S, :] = cv_buf[ts:ts + HALO_S, :]

        ac = ac_ref[...]
        xc = ac - jnp.mean(ac, axis=-1, keepdims=True)
        xn = xc * lax.rsqrt(jnp.mean(xc * xc, axis=-1, keepdims=True) + LN_EPS)
        ln = xn * lg_ref[...] + lb_ref[...]
        cat_ref[:, 0:A] = (ln * _sigmoid(ln)).astype(BF16)
        cat_ref[:, A:2 * A] = (z_ref[:, 2 * A:3 * A] * bconv_buf[...]).astype(BF16)
        x1_ref[...] = xv + jnp.dot(cat_ref[...], wout_ref[...], preferred_element_type=F32)

    tile = lambda w: pl.BlockSpec((ts, w), lambda i: (i, 0))
    return _call(
        body, name="mix0_fwd", grid=(n,), comm=comm, args=(x, g, w_in, conv_a, ln_g, ln_b, conv_b, w_out),
        in_specs=[tile(D), _const((1, D)), _const((NCHIP, D, bw)), _const((K_A, A)), _const((1, A)), _const((1, A)),
                  _const((K_S, A)), _const((2 * A, D))],
        out_specs=[tile(D), tile(NZ), tile(A), tile(A), tile(2 * A), tile(D)],
        out_shape=[jax.ShapeDtypeStruct((s, D), BF16), jax.ShapeDtypeStruct((s, NZ), F32), jax.ShapeDtypeStruct((s, A), F32),
                   jax.ShapeDtypeStruct((s, A), F32), jax.ShapeDtypeStruct((s, 2 * A), BF16), jax.ShapeDtypeStruct((s, D), F32)],
        scratch=[pltpu.VMEM((HALO_A + ts, A), F32), pltpu.VMEM((HALO_S + ts, A), F32)])


def _ffn_fwd(x, g, w_up, wc, w_down, name, comm=None):
    s = x.shape[0]
    ts = min(TS_FFN, s)
    n = s // ts
    bw = FF2 // NCHIP
    rows = 32
    offs = [HALO_S - (K_S - 1) + k for k in range(K_S)]

    def body(x_ref, g_ref, wup_ref, wc_ref, wdn_ref, h_ref, u0_ref, u_ref, act_ref, xo_ref, cbuf):
        i = pl.program_id(0)

        @pl.when(i == 0)
        def _():
            cbuf[0:HALO_S, :] = jnp.zeros((HALO_S, FF2), F32)

        xv = x_ref[...]
        h = (xv * _rsqrt_mean_sq(xv) * g_ref[...]).astype(BF16)
        h_ref[...] = h
        f = None
        for p in range(NCHIP // 2):
            for j in (p, NCHIP // 2 + p):
                zc = jnp.dot(h, wup_ref[j], preferred_element_type=F32)
                u0_ref[:, j * bw:(j + 1) * bw] = zc.astype(BF16)
                cbuf[HALO_S:HALO_S + ts, j * bw:(j + 1) * bw] = zc
            for r0 in range(0, ts, rows):
                for c0 in range(p * bw, (p + 1) * bw, LANES):
                    ug = _taps(wc_ref, _shifted(cbuf, r0, rows, c0, LANES, offs), offs, c0, LANES)
                    uv = _taps(wc_ref, _shifted(cbuf, r0, rows, FF + c0, LANES, offs), offs, FF + c0, LANES)
                    u_ref[r0:r0 + rows, c0:c0 + LANES] = ug
                    u_ref[r0:r0 + rows, FF + c0:FF + c0 + LANES] = uv
                    act_ref[r0:r0 + rows, c0:c0 + LANES] = (ug * _sigmoid(ug) * uv).astype(BF16)
            fp = jnp.dot(act_ref[:, p * bw:(p + 1) * bw], wdn_ref[p * bw:(p + 1) * bw, :], preferred_element_type=F32)
            f = fp if f is None else f + fp
        cbuf[0:HALO_S, :] = cbuf[ts:ts + HALO_S, :]
        xo_ref[...] = xv + f

    tile = lambda w: pl.BlockSpec((ts, w), lambda i: (i, 0))
    return _call(
        body, name=name, grid=(n,), comm=comm, args=(x, g, w_up, wc, w_down),
        in_specs=[tile(D), _const((1, D)), _const((NCHIP, D, bw)), _const((K_S, FF2)), _const((FF, D))],
        out_specs=[tile(D), tile(FF2), tile(FF2), tile(FF), tile(D)],
        out_shape=[jax.ShapeDtypeStruct((s, D), BF16), jax.ShapeDtypeStruct((s, FF2), BF16), jax.ShapeDtypeStruct((s, FF2), F32),
                   jax.ShapeDtypeStruct((s, FF), BF16), jax.ShapeDtypeStruct((s, D), F32)],
        scratch=[pltpu.VMEM((HALO_S + ts, FF2), F32)])


def _pool_windows(hbuf, pbuf, tile_row0, ts):
    def chunk(ci, carry):
        r0 = pl.multiple_of(ci * R_CHUNK, R_CHUNK)
        t1 = (tile_row0 + r0 + lax.broadcasted_iota(jnp.int32, (R_CHUNK, 1), 0) + 1).astype(F32)
        for gi, w in enumerate(POOL_WINDOWS):
            cnt = jnp.minimum(t1, float(w))
            offs = [HALO_P - jj for jj in range(w)]
            for c0 in range(gi * PG, (gi + 1) * PG, LANES):
                sh = _shifted(hbuf, r0, R_CHUNK, c0, LANES, offs)
                tot = sh[offs[0]]
                for o in offs[1:]:
                    tot = tot + sh[o]
                pbuf[pl.ds(r0, R_CHUNK), c0:c0 + LANES] = (tot / cnt - sh[HALO_P]).astype(BF16)
        return carry

    lax.fori_loop(0, ts // R_CHUNK, chunk, 0)


def _assemble_wpool(wp_ref, wps):
    rb = PG // NCHIP
    for gi in range(len(POOL_WINDOWS)):
        for j in range(NCHIP):
            wps[gi, j * rb:(j + 1) * rb, :] = wp_ref[j, gi]


def _pool_fwd(x, g, w_pool, scale):
    s = x.shape[0]
    ts = min(TS_POOL, s)
    n = s // ts
    ng = len(POOL_WINDOWS)

    def body(x_ref, g_ref, wp_ref, sc_ref, xo_ref, hbuf, pbuf, wps):
        i = pl.program_id(0)

        @pl.when(i == 0)
        def _():
            hbuf[0:HALO_P, :] = jnp.zeros((HALO_P, D), F32)
            _assemble_wpool(wp_ref, wps)

        xv = x_ref[...]
        hbuf[HALO_P:HALO_P + ts, :] = xv * _rsqrt_mean_sq(xv) * g_ref[...]
        _pool_windows(hbuf, pbuf, i * ts, ts)
        hbuf[0:HALO_P, :] = hbuf[ts:ts + HALO_P, :]
        for gi in range(ng):
            cols = slice(gi * PG, (gi + 1) * PG)
            y = jnp.dot(pbuf[:, cols], wps[gi], preferred_element_type=F32)
            xo_ref[:, cols] = xv[:, cols] + y * sc_ref[:, cols]

    tile = pl.BlockSpec((ts, D), lambda i: (i, 0))
    return pl.pallas_call(
        body, name="pool_fwd", grid=(n,),
        in_specs=[tile, _const((1, D)), _const((NCHIP, ng, PG // NCHIP, PG)), _const((1, D))],
        out_specs=tile, out_shape=jax.ShapeDtypeStruct((s, D), F32),
        scratch_shapes=[pltpu.VMEM((HALO_P + ts, D), F32), pltpu.VMEM((ts, D), BF16), pltpu.VMEM((ng, PG, PG), BF16)],
        compiler_params=_cparams(1),
    )(x, g, w_pool, scale)


def _final_loss(x, g, target):
    s = x.shape[0]
    ts = min(TS_MM, s)
    n = s // ts

    def body(x_ref, g_ref, t_ref, dx_ref, dg_ref, loss_ref):
        i = pl.program_id(0)

        @pl.when(i == 0)
        def _():
            dg_ref[...] = jnp.zeros((1, D), F32)
            loss_ref[...] = jnp.zeros((1, LANES), F32)

        xv = x_ref[...]
        r = _rsqrt_mean_sq(xv)
        xh = xv * r
        gv = g_ref[...]
        err = xh * gv - t_ref[...]
        sq = jnp.sum(jnp.sum(err * err, axis=1, keepdims=True), axis=0, keepdims=True)
        loss_ref[...] += sq * (0.5 / D)
        dy = err * (1.0 / D)
        dg_ref[...] += jnp.sum(dy * xh, axis=0, keepdims=True)
        dx_ref[...] = _rms_bwd(dy, xh, r, gv)

    tile = pl.BlockSpec((ts, D), lambda i: (i, 0))
    return pl.pallas_call(
        body, name="final_loss", grid=(n,),
        in_specs=[tile, _const((1, D)), tile],
        out_specs=[tile, pl.BlockSpec((1, D), lambda i: (0, 0)), pl.BlockSpec((1, LANES), lambda i: (0, 0))],
        out_shape=[jax.ShapeDtypeStruct((s, D), F32), jax.ShapeDtypeStruct((1, D), F32), jax.ShapeDtypeStruct((1, LANES), F32)],
        compiler_params=_cparams(1),
    )(x, g, target)


def _ffn_bwd_a(dxo, u, u0, wc, w_down, name, comm=None):
    s = dxo.shape[0]
    ts = min(TS_FFN, s)
    n = s // ts
    cw = FF2 // NCHIP
    rows = 32
    lw2 = 2 * LANES
    boffs = [K_S - 1 - k for k in range(K_S)]

    def body(dxo_ref, u_ref, u0_ref, wc_ref, wdn_ref, du0_ref, dwc_ref, dubuf, dact, dwacc):
        i = pl.program_id(0)

        @pl.when(i == 0)
        def _():
            dubuf[ts:ts + HALO_S, :] = jnp.zeros((HALO_S, FF2), F32)
            dwacc[...] = jnp.zeros(dwacc.shape, F32)

        df = dxo_ref[...].astype(BF16)
        for cg in range(0, FF, cw):
            dact[...] = lax.dot_general(df, wdn_ref[cg:cg + cw, :], NT_DIMS, preferred_element_type=F32)

            def chunk(ci, carry, cg=cg):
                rs = pl.ds(pl.multiple_of(ci * rows, rows), rows)
                for c in range(cg, cg + cw, LANES):
                    ug, uv = u_ref[rs, c:c + LANES], u_ref[rs, FF + c:FF + c + LANES]
                    sg = _sigmoid(ug)
                    da = dact[rs, c - cg:c - cg + LANES]
                    gs = ug * sg
                    dubuf[rs, c:c + LANES] = da * uv * (sg + gs * (1.0 - sg))
                    dubuf[rs, FF + c:FF + c + LANES] = da * gs
                return carry

            lax.fori_loop(0, ts // rows, chunk, 0)

        def chunk2(ci, carry):
            r0 = pl.multiple_of(ci * rows, rows)
            rs = pl.ds(r0, rows)
            for c in range(0, FF2, lw2):
                sh = _shifted(dubuf, r0, rows, c, lw2, boffs)
                du0_ref[rs, c:c + lw2] = _taps(wc_ref, sh, boffs, c, lw2).astype(BF16)
                u0v = u0_ref[rs, c:c + lw2].astype(F32)
                for k, o in enumerate(boffs):
                    dwacc[SUBLANES * k:SUBLANES * (k + 1), c:c + lw2] += _rowsum8(sh[o] * u0v)
            return carry

        lax.fori_loop(0, ts // rows, chunk2, 0)
        dubuf[ts:ts + HALO_S, :] = dubuf[0:HALO_S, :]

        @pl.when(i == n - 1)
        def _():
            _finish_tap_sums(dwacc, dwc_ref, K_S)

    rev = lambda w: pl.BlockSpec((ts, w), lambda i: (n - 1 - i, 0))
    return _call(
        body, name=name, grid=(n,), comm=comm, args=(dxo, u, u0, wc, w_down),
        in_specs=[rev(D), rev(FF2), rev(FF2), _const((K_S, FF2)), _const((FF, D))],
        out_specs=[rev(FF2), pl.BlockSpec((K_S, FF2), lambda i: (0, 0))],
        out_shape=[jax.ShapeDtypeStruct((s, FF2), BF16), jax.ShapeDtypeStruct((K_S, FF2), F32)],
        scratch=[pltpu.VMEM((ts + HALO_S, FF2), F32), pltpu.VMEM((ts, cw), F32), pltpu.VMEM((SUBLANES * K_S, FF2), F32)])


def _nt_rms_bwd(dy, w, x, g, dres, name, comm=None):
    s = x.shape[0]
    ts = min(TS_MM, s)
    n = s // ts
    nw = dy.shape[1]
    bw = nw // NCHIP

    def body(dy_ref, w_ref, x_ref, g_ref, dres_ref, dx_ref, dg_ref):
        i = pl.program_id(0)

        @pl.when(i == 0)
        def _():
            dg_ref[...] = jnp.zeros((1, D), F32)

        dh = lax.dot_general(dy_ref[:, 0:bw], w_ref[0], NT_DIMS, preferred_element_type=F32)
        for j in range(1, NCHIP):
            dh = dh + lax.dot_general(dy_ref[:, j * bw:(j + 1) * bw], w_ref[j], NT_DIMS, preferred_element_type=F32)
        xv = x_ref[...]
        r = _rsqrt_mean_sq(xv)
        xh = xv * r
        dg_ref[...] += jnp.sum(dh * xh, axis=0, keepdims=True)
        dx_ref[...] = dres_ref[...] + _rms_bwd(dh, xh, r, g_ref[...])

    tile = lambda wd: pl.BlockSpec((ts, wd), lambda i: (i, 0))
    return _call(
        body, name=name, grid=(n,), comm=comm, args=(dy, w, x, g, dres),
        in_specs=[tile(nw), _const((NCHIP, D, bw)), tile(D), _const((1, D)), tile(D)],
        out_specs=[tile(D), pl.BlockSpec((1, D), lambda i: (0, 0))],
        out_shape=[jax.ShapeDtypeStruct((s, D), F32), jax.ShapeDtypeStruct((1, D), F32)])


def _wgrad(a, b, n_blocks, name, comm=None):
    s, m = a.shape
    bw = b.shape[1] // n_blocks
    tk = min(TS_MM, s)

    def body(a_ref, b_ref, o_ref):
        @pl.when(pl.program_id(0) == 0)
        def _():
            o_ref[...] = jnp.zeros(o_ref.shape, F32)

        for j in range(n_blocks):
            o_ref[j] += lax.dot_general(a_ref[...], b_ref[:, j * bw:(j + 1) * bw].astype(BF16), TN_DIMS,
                                        preferred_element_type=F32)

    (out,), comm_out = _call(
        body, name=name, grid=(s // tk,), comm=comm, args=(a, b),
        in_specs=[pl.BlockSpec((tk, m), lambda k: (k, 0)), pl.BlockSpec((tk, b.shape[1]), lambda k: (k, 0))],
        out_specs=[pl.BlockSpec((n_blocks, m, bw), lambda k: (0, 0, 0))],
        out_shape=[jax.ShapeDtypeStruct((n_blocks, m, bw), F32)])
    return out, comm_out


def _pool_bwd(dxo, x, g, w_pool, scale, comm=None):
    s = x.shape[0]
    ts = min(TS_POOL, s)
    n = s // ts
    ng = len(POOL_WINDOWS)
    rb = PG // NCHIP

    def body(dxo_ref, x_ref, halo_ref, g_ref, wp_ref, sc_ref, dx_ref, dwp_ref, dsc_ref, dg_ref,
             hbuf, pbuf, qbuf, dhbuf, wps, dwacc):
        i = pl.program_id(0)
        j = n - 1 - i

        @pl.when(i == 0)
        def _():
            qbuf[ts:ts + HALO_P, :] = jnp.zeros((HALO_P, D), F32)
            dwacc[...] = jnp.zeros(dwacc.shape, F32)
            dsc_ref[...] = jnp.zeros((1, D), F32)
            dg_ref[...] = jnp.zeros((1, D), F32)
            _assemble_wpool(wp_ref, wps)

        gv = g_ref[...]
        xl = halo_ref[...]
        hbuf[0:HALO_P, :] = jnp.where(j == 0, 0.0, xl * _rsqrt_mean_sq(xl) * gv)
        xv = x_ref[...]
        r = _rsqrt_mean_sq(xv)
        xh = xv * r
        hbuf[HALO_P:HALO_P + ts, :] = xh * gv
        _pool_windows(hbuf, pbuf, j * ts, ts)

        dy = dxo_ref[...]
        t1 = (j * ts + lax.broadcasted_iota(jnp.int32, (ts, 1), 0) + 1).astype(F32)
        for gi, w in enumerate(POOL_WINDOWS):
            cols = slice(gi * PG, (gi + 1) * PG)
            p = pbuf[:, cols]
            y = jnp.dot(p, wps[gi], preferred_element_type=F32)
            dsc_ref[:, cols] += jnp.sum(dy[:, cols] * y, axis=0, keepdims=True)
            dq = (dy[:, cols] * sc_ref[:, cols]).astype(BF16)
            dwacc[gi] += lax.dot_general(p, dq, TN_DIMS, preferred_element_type=F32)
            dp = lax.dot_general(dq, wps[gi], NT_DIMS, preferred_element_type=F32)
            qbuf[0:ts, cols] = dp / jnp.minimum(t1, float(w))

        def chunk(ci, carry):
            r0 = pl.multiple_of(ci * R_CHUNK, R_CHUNK)
            tc = (j * ts + r0 + lax.broadcasted_iota(jnp.int32, (R_CHUNK, 1), 0) + 1).astype(F32)
            for gi, w in enumerate(POOL_WINDOWS):
                cnt = jnp.minimum(tc, float(w))
                offs = list(range(w))
                for c0 in range(gi * PG, (gi + 1) * PG, LANES):
                    sh = _shifted(qbuf, r0, R_CHUNK, c0, LANES, offs)
                    tot = sh[0]
                    for o in offs[1:]:
                        tot = tot + sh[o]
                    dhbuf[pl.ds(r0, R_CHUNK), c0:c0 + LANES] = tot - sh[0] * cnt
            return carry

        lax.fori_loop(0, ts // R_CHUNK, chunk, 0)
        qbuf[ts:ts + HALO_P, :] = qbuf[0:HALO_P, :]
        dh = dhbuf[...]
        dg_ref[...] += jnp.sum(dh * xh, axis=0, keepdims=True)
        dx_ref[...] = dy + _rms_bwd(dh, xh, r, gv)

        @pl.when(i == n - 1)
        def _():
            for gi in range(ng):
                for jj in range(NCHIP):
                    dwp_ref[jj, gi] = dwacc[gi, jj * rb:(jj + 1) * rb, :]

    rev = pl.BlockSpec((ts, D), lambda i: (n - 1 - i, 0))
    halo = pl.BlockSpec((HALO_P, D), lambda i: (jnp.maximum((n - 1 - i) * (ts // HALO_P) - 1, 0), 0))
    vec = pl.BlockSpec((1, D), lambda i: (0, 0))
    return _call(
        body, name="pool_bwd", grid=(n,), comm=comm, args=(dxo, x, x, g, w_pool, scale),
        in_specs=[rev, rev, halo, _const((1, D)), _const((NCHIP, ng, rb, PG)), _const((1, D))],
        out_specs=[rev, pl.BlockSpec((NCHIP, ng, rb, PG), lambda i: (0, 0, 0, 0)), vec, vec],
        out_shape=[jax.ShapeDtypeStruct((s, D), F32), jax.ShapeDtypeStruct((NCHIP, ng, rb, PG), F32),
                   jax.ShapeDtypeStruct((1, D), F32), jax.ShapeDtypeStruct((1, D), F32)],
        scratch=[pltpu.VMEM((HALO_P + ts, D), F32), pltpu.VMEM((ts, D), BF16), pltpu.VMEM((ts + HALO_P, D), F32),
                 pltpu.VMEM((ts, D), F32), pltpu.VMEM((ng, PG, PG), BF16), pltpu.VMEM((ng, PG, PG), F32)])


def _mix0_bwd_a(dx1, z, ac, bconv, w_out, conv_a, ln_g, ln_b, conv_b, comm=None):
    s = dx1.shape[0]
    ts = min(TS_MIXB, s)
    n = s // ts
    rows = 32
    boffs_a = [K_A - 1 - k for k in range(K_A)]
    boffs_s = [K_S - 1 - k for k in range(K_S)]

    def body(dx1_ref, z_ref, ac_ref, bconv_ref, wout_ref, ca_ref, lg_ref, lb_ref, cb_ref,
             dz_ref, dca_ref, dlg_ref, dlb_ref, dcb_ref, dac_buf, dbc_buf, dca_acc, dcb_acc):
        i = pl.program_id(0)

        @pl.when(i == 0)
        def _():
            dac_buf[ts:ts + HALO_A, :] = jnp.zeros((HALO_A, A), F32)
            dbc_buf[ts:ts + HALO_S, :] = jnp.zeros((HALO_S, A), F32)
            dca_acc[...] = jnp.zeros(dca_acc.shape, F32)
            dcb_acc[...] = jnp.zeros(dcb_acc.shape, F32)
            dlg_ref[...] = jnp.zeros((1, A), F32)
            dlb_ref[...] = jnp.zeros((1, A), F32)

        dcat = lax.dot_general(dx1_ref[...].astype(BF16), wout_ref[...], NT_DIMS, preferred_element_type=F32)
        db = dcat[:, A:2 * A]
        dz_ref[:, 2 * A:3 * A] = (db * bconv_ref[...]).astype(BF16)
        dbc_buf[0:ts, :] = db * z_ref[:, 2 * A:3 * A]
        ac = ac_ref[...]
        xc = ac - jnp.mean(ac, axis=-1, keepdims=True)
        rstd = lax.rsqrt(jnp.mean(xc * xc, axis=-1, keepdims=True) + LN_EPS)
        xn = xc * rstd
        lg = lg_ref[...]
        ln = xn * lg + lb_ref[...]
        sl = _sigmoid(ln)
        dln = dcat[:, 0:A] * sl * (1.0 + ln * (1.0 - sl))
        dlg_ref[...] += jnp.sum(dln * xn, axis=0, keepdims=True)
        dlb_ref[...] += jnp.sum(dln, axis=0, keepdims=True)
        dxn = dln * lg
        dac_buf[0:ts, :] = rstd * (dxn - jnp.mean(dxn, axis=-1, keepdims=True)
                                   - xn * jnp.mean(dxn * xn, axis=-1, keepdims=True))

        def chunk(ci, carry):
            r0 = pl.multiple_of(ci * rows, rows)
            rs = pl.ds(r0, rows)
            for c0 in range(0, A, LANES):
                col = lambda grp: slice(grp * A + c0, grp * A + c0 + LANES)
                a_val, sg = z_ref[rs, col(0)], _sigmoid(z_ref[rs, col(1)])
                dglu = _conv_corr(dac_buf, ca_ref, a_val * sg, dca_acc, r0, rows, c0, LANES, boffs_a)
                dz_ref[rs, col(0)] = (dglu * sg).astype(BF16)
                dz_ref[rs, col(1)] = (dglu * a_val * sg * (1.0 - sg)).astype(BF16)
                c_gate, bc_val = z_ref[rs, col(3)], z_ref[rs, col(4)]
                dcv = _conv_corr(dbc_buf, cb_ref, c_gate * bc_val, dcb_acc, r0, rows, c0, LANES, boffs_s)
                dz_ref[rs, col(3)] = (dcv * bc_val).astype(BF16)
                dz_ref[rs, col(4)] = (dcv * c_gate).astype(BF16)
            return carry

        lax.fori_loop(0, ts // rows, chunk, 0)
        dac_buf[ts:ts + HALO_A, :] = dac_buf[0:HALO_A, :]
        dbc_buf[ts:ts + HALO_S, :] = dbc_buf[0:HALO_S, :]

        @pl.when(i == n - 1)
        def _():
            _finish_tap_sums(dca_acc, dca_ref, K_A)
            _finish_tap_sums(dcb_acc, dcb_ref, K_S)

    rev = lambda w: pl.BlockSpec((ts, w), lambda i: (n - 1 - i, 0))
    full = lambda r, c: pl.BlockSpec((r, c), lambda i: (0, 0))
    return _call(
        body, name="mix0_bwd_a", grid=(n,), comm=comm, args=(dx1, z, ac, bconv, w_out, conv_a, ln_g, ln_b, conv_b),
        in_specs=[rev(D), rev(NZ), rev(A), rev(A), _const((2 * A, D)), _const((K_A, A)), _const((1, A)), _const((1, A)),
                  _const((K_S, A))],
        out_specs=[rev(NZ), full(K_A, A), full(1, A), full(1, A), full(K_S, A)],
        out_shape=[jax.ShapeDtypeStruct((s, NZ), BF16), jax.ShapeDtypeStruct((K_A, A), F32), jax.ShapeDtypeStruct((1, A), F32),
                   jax.ShapeDtypeStruct((1, A), F32), jax.ShapeDtypeStruct((K_S, A), F32)],
        scratch=[pltpu.VMEM((ts + HALO_A, A), F32), pltpu.VMEM((ts + HALO_S, A), F32), pltpu.VMEM((SUBLANES * K_A, A), F32),
                 pltpu.VMEM((SUBLANES * K_S, A), F32)])


def _train_step(x, target, place, shard, rep, small_pack, small_shapes, adamw):
    bw_up, bw_in = FF2 // NCHIP, NZ // NCHIP
    five = lambda a, k: a.reshape(NCHIP, 2, *HALF[k])

    g_in, g_out, small_g = _run_comm(_gather_comm([shard["w_in"], shard["w_out"]], small_pack), "ag_first")
    whole = {}
    for k, part in zip(SMALL_SHARDED, _unpack(small_g, small_shapes, lead=(NCHIP,))):
        whole[k] = jnp.moveaxis(part, 0, 1).reshape(part.shape[1], NCHIP * part.shape[2])
    w_in, w_out = g_in.reshape(NCHIP, D, bw_in), g_out.reshape(2 * A, D)
    conv_ffn = whole["conv_ffn_w"].reshape(2, K_S, FF2)
    nffn = [rep["norm_ffn"][0:1], rep["norm_ffn"][1:2]]

    (h0, z, ac, bconv, cat, x1), (g_up0, g_dn0) = _mix0_fwd(
        x, rep["norm_mix_even"], w_in, whole["conv_a"], rep["ln_a_g"], rep["ln_a_b"], whole["conv_b"], w_out,
        comm=_gather_comm([shard["w_up0"], shard["w_down0"]]))
    w_up0, w_dn0 = g_up0.reshape(NCHIP, D, bw_up), g_dn0.reshape(FF, D)
    (hf0, u00, u0, act0, x2), (g_pool, g_up1, g_dn1) = _ffn_fwd(
        x1, nffn[0], w_up0, conv_ffn[0], w_dn0, "ffn0_fwd",
        comm=_gather_comm([shard["w_pool"], shard["w_up1"], shard["w_down1"]]))
    w_pool = g_pool.reshape(NCHIP, len(POOL_WINDOWS), PG // NCHIP, PG)
    w_up1, w_dn1 = g_up1.reshape(NCHIP, D, bw_up), g_dn1.reshape(FF, D)
    x3 = _pool_fwd(x2, whole["norm_mix_odd"], w_pool, whole["pool_scale"])
    (hf1, u01, u1, act1, x4), _ = _ffn_fwd(x3, nffn[1], w_up1, conv_ffn[1], w_dn1, "ffn1_fwd")
    dx4, g_nfin, loss_part = _final_loss(x4, rep["norm_final"], target)

    psum = lambda k, g, ld, tag="": _pair_sum(place, g, ld, "pair_sum_" + k + tag)
    tot = lambda k, g, ld, p, layer=0, nl=1, prev=None: _chip_sum(place, g, ld, p, layer, nl, "chip_sum_%s%d" % (k, layer), prev)
    pair, chips, join = _pair_comm, _chips_comm, _join_comm

    gr_dn1 = five(_wgrad(act1, dx4, 1, "wgrad_down1")[0], "w_down")
    (du01, g_wc1), _ = _ffn_bwd_a(dx4, u1, u01, conv_ffn[1], w_dn1, "ffn1_bwd_a")
    g_up, (ld_dn1,) = _wgrad(hf1, du01, NCHIP, "wgrad_up1", comm=pair([gr_dn1]))
    gr_up1 = five(g_up, "w_up")
    s_dn1 = psum("w_down", gr_dn1, ld_dn1, "1")
    (dx3, g_nf1), (p_dn1, ld_up1) = _nt_rms_bwd(du01, w_up1, x3, nffn[1], dx4, "ffn1_bwd_b",
                                                comm=join(chips([s_dn1]), pair([gr_up1])))
    s_up1 = psum("w_up", gr_up1, ld_up1, "1")
    (dx2, g_wpool, g_scale, g_nmo), _ = _pool_bwd(dx3, x2, whole["norm_mix_odd"], w_pool, whole["pool_scale"])
    gr_pool = five(g_wpool, "w_pool")
    (du00, g_wc0), (p_up1,) = _ffn_bwd_a(dx2, u0, u00, conv_ffn[0], w_dn0, "ffn0_bwd_a", comm=chips([s_up1]))
    gr_dn0 = five(_wgrad(act0, dx2, 1, "wgrad_down0")[0], "w_down")
    g_up, (ld_dn0, ld_pool) = _wgrad(hf0, du00, NCHIP, "wgrad_up0", comm=pair([gr_dn0, gr_pool]))
    gr_up0 = five(g_up, "w_up")
    s_dn0, s_pool = psum("w_down", gr_dn0, ld_dn0, "0"), psum("w_pool", gr_pool, ld_pool)
    (dx1, g_nf0), (p_dn0, p_pool, ld_up0) = _nt_rms_bwd(du00, w_up0, x1, nffn[0], dx2, "ffn0_bwd_b",
                                                        comm=join(chips([s_dn0, s_pool]), pair([gr_up0])))
    s_up0 = psum("w_up", gr_up0, ld_up0, "0")
    gr_out = five(_wgrad(cat, dx1, 1, "wgrad_out")[0], "w_out")
    (dz, g_ca, g_lg, g_lb, g_cb), (p_up0, ld_out) = _mix0_bwd_a(
        dx1, z, ac, bconv, w_out, whole["conv_a"], rep["ln_a_g"], rep["ln_a_b"], whole["conv_b"],
        comm=join(chips([s_up0]), pair([gr_out])))
    s_out = psum("w_out", gr_out, ld_out)
    t_pool = tot("w_pool", gr_pool, ld_pool, p_pool)
    t_up = tot("w_up", gr_up0, ld_up0, p_up0, 0, 2, tot("w_up", gr_up1, ld_up1, p_up1, 1, 2))
    t_dn = tot("w_down", gr_dn0, ld_dn0, p_dn0, 0, 2, tot("w_down", gr_dn1, ld_dn1, p_dn1, 1, 2))
    g_in, (p_out, t_pool, t_up, t_dn) = _wgrad(h0, dz, NCHIP, "wgrad_in",
                                                comm=join(chips([s_out]), _swap_comm([t_pool, t_up, t_dn])))
    gr_in = five(g_in, "w_in")
    (grad_x, g_nme), (ld_in,) = _nt_rms_bwd(dz, w_in, x, rep["norm_mix_even"], dx1, "mix0_bwd_b", comm=pair([gr_in]))
    s_in = psum("w_in", gr_in, ld_in)

    small = {"norm_mix_even": g_nme, "conv_a": g_ca, "ln_a_g": g_lg, "ln_a_b": g_lb, "conv_b": g_cb, "norm_mix_odd": g_nmo,
             "pool_scale": g_scale, "norm_ffn": jnp.concatenate([g_nf0, g_nf1], axis=0),
             "conv_ffn_w": jnp.stack([g_wc0, g_wc1]), "norm_final": g_nfin}
    done = {}
    done["w_up"], (p_in, small_all) = adamw("w_up", t_up, chips([s_in], _pack([small[k] for k in SMALL_ALL] + [loss_part])))
    done["w_down"], _ = adamw("w_down", t_dn, None)
    done["w_pool"], _ = adamw("w_pool", t_pool, None)
    t_in, t_out = _run_comm(_swap_comm([tot("w_in", gr_in, ld_in, p_in), tot("w_out", gr_out, ld_out, p_out)]), "rs_swap")
    done["w_in"], _ = adamw("w_in", t_in, None)
    done["w_out"], _ = adamw("w_out", t_out, None)
    return grad_x, done, _sum_devices(small_all), [small[k].shape for k in SMALL_ALL] + [(1, LANES)]


def _adamw_math(w, g, m, v):
    m = ADAM_B1 * m + (1.0 - ADAM_B1) * g
    v = ADAM_B2 * v + (1.0 - ADAM_B2) * (g * g)
    m_hat = m / (1.0 - ADAM_B1 ** ADAM_STEP)
    v_hat = v / (1.0 - ADAM_B2 ** ADAM_STEP)
    return -ADAM_LR * (m_hat / (jnp.sqrt(v_hat) + ADAM_EPS) + ADAM_WD * w), m, v


def _adamw_big(w, g, m, v, tr, name, comm=None):
    nl, rows, cols = w.shape

    def body(w_ref, g_ref, m_ref, v_ref, g2_ref, d_ref, m2_ref, v2_ref):
        gv = g_ref[...]
        g2_ref[...] = gv
        d_ref[...], m2_ref[...], v2_ref[...] = _adamw_math(w_ref[...], gv, m_ref[...], v_ref[...])

    spec = pl.BlockSpec((None, tr, cols), lambda l, r: (l, r, 0))
    return _call(body, name=name, grid=(nl, rows // tr), comm=comm, args=(w, g, m, v), in_specs=[spec] * 4,
                 out_specs=[spec] * 4, out_shape=[jax.ShapeDtypeStruct(w.shape, F32)] * 4)


def _adamw_small(ws, gs, ms, vs):
    n = len(ws)

    def body(*refs):
        for p in range(n):
            w_ref, g_ref, m_ref, v_ref = (refs[q * n + p] for q in range(4))
            d_ref, m2_ref, v2_ref = (refs[(4 + q) * n + p] for q in range(3))
            d_ref[...], m2_ref[...], v2_ref[...] = _adamw_math(w_ref[...], g_ref[...], m_ref[...], v_ref[...])

    whole = lambda a: pl.BlockSpec(a.shape, lambda: (0,) * a.ndim)
    outs = pl.pallas_call(
        body, name="adamw_small", in_specs=[whole(a) for a in ws] * 4, out_specs=[whole(a) for a in ws] * 3,
        out_shape=[jax.ShapeDtypeStruct(a.shape, F32) for a in ws] * 3,
        compiler_params=pltpu.CompilerParams(vmem_limit_bytes=VMEM_LIMIT_BYTES),
    )(*ws, *gs, *ms, *vs)
    return outs[0:n], outs[n:2 * n], outs[2 * n:3 * n]


def _place():
    x, y, c = lax.axis_index("x"), lax.axis_index("y"), lax.axis_index("c")
    chips = [(x, 1 - y), (1 - x, y), (1 - x, 1 - y)]
    blocks = [2 * cx + cy for cx, cy in chips]
    return x, y, c, 2 * x + y, chips, blocks


def _gather_comm(shards, small=None):
    na = len(shards)
    nk = NCHIP - 1
    ns = 0 if small is None else 1

    def copies(ins, outs, sems):
        ici_send, ici_recv, fwd_send, fwd_recv, own_send, own_recv = sems[:6]
        x, y, c, j, chips, blocks = _place()
        sib = (x, y, 1 - c)

        def ici(a, k, arrival):
            dst = outs[a].at[blocks[k], c] if arrival else outs[a].at[j, c]
            return pltpu.make_async_remote_copy(
                src_ref=dst if arrival else ins[a].at[c], dst_ref=dst, send_sem=ici_send.at[a * nk + k],
                recv_sem=ici_recv.at[a * nk + k], device_id=(*chips[k], c), device_id_type=MESH)

        def fwd(a, k, half):
            ref = outs[a].at[blocks[k], half]
            return pltpu.make_async_remote_copy(
                src_ref=ref, dst_ref=ref, send_sem=fwd_send.at[a * nk + k], recv_sem=fwd_recv.at[a * nk + k],
                device_id=sib, device_id_type=MESH)

        own = [pltpu.make_async_remote_copy(src_ref=ins[a], dst_ref=outs[a].at[j], send_sem=own_send.at[a],
                                            recv_sem=own_recv.at[a], device_id=sib, device_id_type=MESH) for a in range(na)]
        small_copies = [pltpu.make_async_remote_copy(
            src_ref=ins[na], dst_ref=outs[na].at[j], send_sem=ici_send.at[na * nk + k], recv_sem=ici_recv.at[na * nk + k],
            device_id=(*chips[k], c), device_id_type=MESH) for k in range(nk * ns)]
        local = [pltpu.make_async_copy(ins[na], outs[na].at[j], sems[6])] if ns else []
        return ici, fwd, own, small_copies, local, c

    def start(ins, outs, sems):
        ici, _, own, small_copies, local, _ = copies(ins, outs, sems)
        for cp in local + own + [ici(a, k, False) for a in range(na) for k in range(nk)] + small_copies:
            cp.start()

    def mid(ins, outs, sems):
        ici, fwd, _, _, _, c = copies(ins, outs, sems)
        for a in range(na):
            for k in range(nk):
                ici(a, k, True).wait_recv()
                fwd(a, k, c).start()

    def finish(ins, outs, sems):
        ici, fwd, own, small_copies, local, c = copies(ins, outs, sems)
        for cp in small_copies:
            cp.wait()
        for a in range(na):
            for k in range(nk):
                ici(a, k, False).wait_send()
                fwd(a, k, c).wait_send()
                fwd(a, k, 1 - c).wait_recv()
        for cp in own + local:
            cp.wait()

    out_shapes = [jax.ShapeDtypeStruct((NCHIP,) + s.shape, s.dtype) for s in shards]
    sems = [pltpu.SemaphoreType.DMA((na * nk + nk * ns,)), pltpu.SemaphoreType.DMA((na * nk + nk * ns,)),
            pltpu.SemaphoreType.DMA((na * nk,)), pltpu.SemaphoreType.DMA((na * nk,)),
            pltpu.SemaphoreType.DMA((na,)), pltpu.SemaphoreType.DMA((na,))]
    if ns:
        out_shapes.append(jax.ShapeDtypeStruct((NCHIP,) + small.shape, small.dtype))
        sems.append(pltpu.SemaphoreType.DMA)
    return _Comm(list(shards) + [small] * ns, out_shapes, sems, start, finish, mid=mid)


def _simple_comm(inputs, out_shapes, make_copies, n_sems, aliases=None):
    def start(ins, outs, sems):
        for cp in make_copies(ins, outs, sems):
            cp.start()

    def finish(ins, outs, sems):
        for cp in make_copies(ins, outs, sems):
            cp.wait()

    return _Comm(inputs, out_shapes, [pltpu.SemaphoreType.DMA((n,)) for n in n_sems], start, finish, aliases)


def _pair_comm(grads):
    def make_copies(ins, outs, sems):
        x, y, c, _, _, _ = _place()
        return [pltpu.make_async_remote_copy(
            src_ref=ins[a].at[:, 1 - c], dst_ref=outs[a], send_sem=sems[0].at[a], recv_sem=sems[1].at[a],
            device_id=(x, y, 1 - c), device_id_type=MESH) for a in range(len(grads))]

    out_shapes = [jax.ShapeDtypeStruct(g.shape[:1] + g.shape[2:], F32) for g in grads]
    return _simple_comm(grads, out_shapes, make_copies, [len(grads)] * 2)


def _chips_comm(sums, small=None):
    na = len(sums)
    nk = NCHIP - 1

    def make_copies(ins, outs, sems):
        x, y, c, _, chips, blocks = _place()
        copies = [pltpu.make_async_remote_copy(
            src_ref=ins[a].at[blocks[k]], dst_ref=outs[a].at[k], send_sem=sems[0].at[a * nk + k],
            recv_sem=sems[1].at[a * nk + k], device_id=(*chips[k], c), device_id_type=MESH)
            for a in range(na) for k in range(nk)]
        if small is not None:
            me = 4 * x + 2 * y + c
            for r in range(1, NDEV):
                peer = (1 - x if r & 4 else x, 1 - y if r & 2 else y, 1 - c if r & 1 else c)
                copies.append(pltpu.make_async_remote_copy(
                    src_ref=ins[na], dst_ref=outs[na].at[me], send_sem=sems[2].at[r - 1], recv_sem=sems[3].at[r - 1],
                    device_id=peer, device_id_type=MESH))
            copies.append(pltpu.make_async_copy(ins[na], outs[na].at[me], sems[4].at[0]))
        return copies

    out_shapes = [jax.ShapeDtypeStruct((nk,) + g.shape[1:], BF16) for g in sums]
    if small is None:
        return _simple_comm(sums, out_shapes, make_copies, [na * nk] * 2)
    out_shapes.append(jax.ShapeDtypeStruct((NDEV,) + small.shape, F32))
    return _simple_comm(list(sums) + [small], out_shapes, make_copies, [na * nk] * 2 + [NDEV - 1] * 2 + [1])


def _swap_comm(totals):
    def make_copies(ins, outs, sems):
        x, y, c, _, _, _ = _place()
        return [pltpu.make_async_remote_copy(
            src_ref=outs[a].at[:, c], dst_ref=outs[a].at[:, c], send_sem=sems[0].at[a], recv_sem=sems[1].at[a],
            device_id=(x, y, 1 - c), device_id_type=MESH) for a in range(len(totals))]

    out_shapes = [jax.ShapeDtypeStruct(t.shape, F32) for t in totals]
    return _simple_comm(totals, out_shapes, make_copies, [len(totals)] * 2, aliases={a: a for a in range(len(totals))})


def _pair_sum(place, grad, landed, name):
    _, _, rows, cols = grad.shape

    def body(place_ref, g_ref, l_ref, o_ref):
        o_ref[...] = (g_ref[...] + l_ref[...]).astype(BF16)

    return pl.pallas_call(
        body, name=name,
        grid_spec=pltpu.PrefetchScalarGridSpec(
            num_scalar_prefetch=1, grid=(NCHIP,),
            in_specs=[pl.BlockSpec((None, None, rows, cols), lambda b, p: (b, p[1], 0, 0)),
                      pl.BlockSpec((None, rows, cols), lambda b, p: (b, 0, 0))],
            out_specs=pl.BlockSpec((None, rows, cols), lambda b, p: (b, 0, 0))),
        out_shape=jax.ShapeDtypeStruct((NCHIP, rows, cols), BF16), compiler_params=_cparams(1),
    )(place, grad, landed)


def _chip_sum(place, grad, landed, parts, layer, n_layers, name, prev=None):
    _, _, rows, cols = grad.shape

    def body(*refs):
        g_ref, l_ref, p_ref, o_ref = refs[1], refs[2], refs[3], refs[-1]
        tot = g_ref[...] + l_ref[...]
        for k in range(NCHIP - 1):
            tot = tot + p_ref[k].astype(F32)
        o_ref[...] = tot

    in_specs = [pl.BlockSpec((None, None, rows, cols), lambda i, p: (p[0], p[1], 0, 0)),
                pl.BlockSpec((None, rows, cols), lambda i, p: (p[0], 0, 0)),
                pl.BlockSpec((NCHIP - 1, rows, cols), lambda i, p: (0, 0, 0))]
    args = [place, grad, landed, parts]
    if prev is not None:
        in_specs.append(ANY)
        args.append(prev)
    return pl.pallas_call(
        body, name=name,
        grid_spec=pltpu.PrefetchScalarGridSpec(
            num_scalar_prefetch=1, grid=(1,), in_specs=in_specs,
            out_specs=pl.BlockSpec((None, None, rows, cols), lambda i, p: (layer, p[1], 0, 0))),
        out_shape=jax.ShapeDtypeStruct((n_layers, 2, rows, cols), F32),
        input_output_aliases={} if prev is None else {4: 0}, compiler_params=_cparams(1),
    )(*args)


def _sum_devices(parts):
    def body(p_ref, o_ref):
        tot = p_ref[0]
        for d in range(1, NDEV):
            tot = tot + p_ref[d]
        o_ref[...] = tot

    return pl.pallas_call(
        body, name="sum_small", in_specs=[pl.BlockSpec(parts.shape, lambda: (0, 0, 0))],
        out_specs=pl.BlockSpec(parts.shape[1:], lambda: (0, 0)), out_shape=jax.ShapeDtypeStruct(parts.shape[1:], F32),
    )(parts)


def _pack(parts):
    rows = []
    for p in parts:
        p = p.reshape(-1, LANES)
        rows.append(jnp.pad(p, ((0, -p.shape[0] % SUBLANES), (0, 0))))
    return jnp.concatenate(rows, axis=0)


def _unpack(buf, shapes, lead=()):
    out, r0 = [], 0
    nl = len(lead)
    for shp in shapes:
        nrow = 1
        for d in shp:
            nrow *= d
        nrow //= LANES
        out.append(buf[(slice(None),) * nl + (slice(r0, r0 + nrow),)].reshape(lead + tuple(shp)))
        r0 += nrow + (-nrow % SUBLANES)
    return out


WEIGHT_ORDER = ("norm_mix_even", "w_in", "conv_a", "ln_a_g", "ln_a_b", "conv_b", "w_out", "norm_mix_odd", "w_pool",
                "pool_scale", "norm_ffn", "w_up", "conv_ffn_w", "w_down", "norm_final")
BIG = ("w_in", "w_out", "w_pool", "w_up", "w_down")
HALF = {"w_in": (D // 2, NZ // NCHIP), "w_out": (2 * A // NCHIP // 2, D), "w_pool": (PG // 2, PG),
        "w_up": (D // 2, FF2 // NCHIP), "w_down": (FF // NCHIP // 2, D)}
SMALL_SHARDED = ("conv_a", "conv_b", "conv_ffn_w", "norm_mix_odd", "pool_scale")
SMALL_ALL = ("norm_mix_even", "conv_a", "ln_a_g", "ln_a_b", "conv_b", "norm_mix_odd", "pool_scale", "norm_ffn", "conv_ffn_w",
             "norm_final")


def kernel(x, norm_mix_even, w_in, conv_a, ln_a_g, ln_a_b, conv_b, w_out, norm_mix_odd, w_pool, pool_scale, norm_ffn, w_up, conv_ffn_w, w_down, norm_final, loss_target, m_norm_mix_even, m_w_in, m_conv_a, m_ln_a_g, m_ln_a_b, m_conv_b, m_w_out, m_norm_mix_odd, m_w_pool, m_pool_scale, m_norm_ffn, m_w_up, m_conv_ffn_w, m_w_down, m_norm_final, v_norm_mix_even, v_w_in, v_conv_a, v_ln_a_g, v_ln_a_b, v_conv_b, v_w_out, v_norm_mix_odd, v_w_pool, v_pool_scale, v_norm_ffn, v_w_up, v_conv_ffn_w, v_w_down, v_norm_final):
    w = dict(norm_mix_even=norm_mix_even, w_in=w_in, conv_a=conv_a, ln_a_g=ln_a_g, ln_a_b=ln_a_b, conv_b=conv_b, w_out=w_out,
             norm_mix_odd=norm_mix_odd, w_pool=w_pool, pool_scale=pool_scale, norm_ffn=norm_ffn, w_up=w_up,
             conv_ffn_w=conv_ffn_w, w_down=w_down, norm_final=norm_final)
    m = dict(norm_mix_even=m_norm_mix_even, w_in=m_w_in, conv_a=m_conv_a, ln_a_g=m_ln_a_g, ln_a_b=m_ln_a_b, conv_b=m_conv_b,
             w_out=m_w_out, norm_mix_odd=m_norm_mix_odd, w_pool=m_w_pool, pool_scale=m_pool_scale, norm_ffn=m_norm_ffn,
             w_up=m_w_up, conv_ffn_w=m_conv_ffn_w, w_down=m_w_down, norm_final=m_norm_final)
    v = dict(norm_mix_even=v_norm_mix_even, w_in=v_w_in, conv_a=v_conv_a, ln_a_g=v_ln_a_g, ln_a_b=v_ln_a_b, conv_b=v_conv_b,
             w_out=v_w_out, norm_mix_odd=v_norm_mix_odd, w_pool=v_w_pool, pool_scale=v_pool_scale, norm_ffn=v_norm_ffn,
             w_up=v_w_up, conv_ffn_w=v_conv_ffn_w, w_down=v_w_down, norm_final=v_norm_final)
    chip = 2 * lax.axis_index("x") + lax.axis_index("y")
    place = jnp.stack([chip, lax.axis_index("c")]).astype(jnp.int32)

    half = lambda a, name: a.astype(BF16).reshape((2,) + HALF[name])
    shard = {"w_in": half(w_in[0], "w_in"), "w_out": half(w_out[0], "w_out"), "w_pool": half(w_pool[0], "w_pool"),
             "w_up0": half(w_up[0], "w_up"), "w_up1": half(w_up[1], "w_up"),
             "w_down0": half(w_down[0], "w_down"), "w_down1": half(w_down[1], "w_down")}
    small_shapes = [w[k].shape[-2:] if w[k].ndim == 3 and k != "conv_ffn_w" else (w[k].size // w[k].shape[-1], w[k].shape[-1])
                    for k in SMALL_SHARDED]
    rep = dict(norm_mix_even=norm_mix_even, ln_a_g=ln_a_g, ln_a_b=ln_a_b, norm_ffn=norm_ffn, norm_final=norm_final.reshape(1, D))
    rows_per_step = {"w_in": 512, "w_out": 256, "w_pool": 256, "w_up": 256, "w_down": 352}
    as3 = lambda a: a.reshape(a.shape[0], -1, a.shape[-1])

    def adamw(k, total, comm):
        outs, brought = _adamw_big(as3(w[k]), as3(total), as3(m[k]), as3(v[k]), rows_per_step[k], "adamw_" + k, comm)
        return [a.reshape(w[k].shape) for a in outs], brought

    grad_x, done, small_packed, small_full_shapes = _train_step(
        x[0], loss_target[0], place, shard, rep, _pack([w[k] for k in SMALL_SHARDED]), small_shapes, adamw)
    small_sum = _unpack(small_packed, small_full_shapes)
    loss = small_sum[-1][0, 0]

    grad, delta, new_m, new_v = ({k: done[k][q] for k in BIG} for q in range(4))
    for k, gsum in zip(SMALL_ALL, small_sum):
        if k in SMALL_SHARDED:
            cols = w[k].shape[-1]
            gsum = lax.dynamic_slice_in_dim(gsum, chip * cols, cols, axis=gsum.ndim - 1)
        grad[k] = gsum.reshape(w[k].shape)

    as2 = lambda a: a.reshape(-1, a.shape[-1])
    ds, ms, vs = _adamw_small(*[[as2(t[k]) for k in SMALL_ALL] for t in (w, grad, m, v)])
    for k, d2, m2, v2 in zip(SMALL_ALL, ds, ms, vs):
        delta[k], new_m[k], new_v[k] = (a.reshape(w[k].shape) for a in (d2, m2, v2))

    return (loss, grad_x[None], *[grad[k] for k in WEIGHT_ORDER], *[delta[k] for k in WEIGHT_ORDER],
            *[new_m[k] for k in WEIGHT_ORDER], *[new_v[k] for k in WEIGHT_ORDER])
```

```python
import functools

import jax
import jax.numpy as jnp
from jax import lax
from jax.experimental import pallas as pl
from jax.experimental.pallas import tpu as pltpu

F32, BF16 = jnp.float32, jnp.bfloat16

D = 1024
A = 512
NZ = 5 * A
FF = 2816
FF2 = 2 * FF
NCHIP = 4
NDEV = 8
K_A, K_S = 31, 3
POOL_WINDOWS = (2, 4, 8, 16)
PG = D // len(POOL_WINDOWS)
RMS_EPS, LN_EPS = 1e-6, 1e-5
ADAM_LR, ADAM_B1, ADAM_B2, ADAM_EPS, ADAM_WD, ADAM_STEP = 0.001, 0.9, 0.999, 1e-08, 0.01, 10

HALO_A, HALO_S, HALO_P = 32, 8, 16
SUBLANES = 8
LANES = 128
VMEM_LIMIT_BYTES = 56 * 1024 * 1024

TS_MIX = 512
TS_MIXB = 256
TS_FFN = 256
TS_POOL = 512
TS_MM = 512
R_CHUNK = 64
MID_STEPS_BEFORE_END = 1

MESH = pl.DeviceIdType.MESH
ANY = pl.BlockSpec(memory_space=pl.ANY)
NT_DIMS = (((1,), (1,)), ((), ()))
TN_DIMS = (((0,), (0,)), ((), ()))


def _cparams(n_axes):
    return pltpu.CompilerParams(dimension_semantics=("arbitrary",) * n_axes, vmem_limit_bytes=VMEM_LIMIT_BYTES)


def _const(shape):
    nd = len(shape)
    return pl.BlockSpec(shape, lambda *_: (0,) * nd, pipeline_mode=pl.Buffered(1))


def _sigmoid(v):
    return 1.0 / (1.0 + jnp.exp(-v))


def _rsqrt_mean_sq(x):
    return lax.rsqrt(jnp.mean(x * x, axis=-1, keepdims=True) + RMS_EPS)


def _rms_bwd(dh, xh, r, g):
    dxh = dh * g
    return r * (dxh - xh * jnp.mean(dxh * xh, axis=-1, keepdims=True))


def _shifted(buf_ref, row0, rows, col0, width, offsets):
    lo = (min(offsets) // SUBLANES) * SUBLANES
    hi = -(-(max(offsets) + rows) // SUBLANES) * SUBLANES
    start = row0 + lo if isinstance(row0, int) else pl.multiple_of(row0 + lo, SUBLANES)
    win = buf_ref[pl.ds(start, hi - lo), col0:col0 + width]
    out = {}
    for res in sorted({(o - lo) % SUBLANES for o in offsets}):
        qs = {o: (o - lo) // SUBLANES for o in offsets if (o - lo) % SUBLANES == res}
        base = pltpu.roll(win, hi - lo - res, 0) if res else win
        for o, q in qs.items():
            out[o] = base[SUBLANES * q:SUBLANES * q + rows, :]
    return out


def _rowsum8(v):
    acc = v[0:SUBLANES, :]
    for r in range(SUBLANES, v.shape[0], SUBLANES):
        acc = acc + v[r:r + SUBLANES, :]
    return acc


def _taps(w_ref, sh, offsets, col0, width):
    acc = None
    for k, o in enumerate(offsets):
        term = w_ref[k:k + 1, col0:col0 + width] * sh[o]
        acc = term if acc is None else acc + term
    return acc


def _window_bases(buf_ref, row0, rows, col0, width, offsets):
    lo = (min(offsets) // SUBLANES) * SUBLANES
    hi = -(-(max(offsets) + rows) // SUBLANES) * SUBLANES
    start = row0 + lo if isinstance(row0, int) else pl.multiple_of(row0 + lo, SUBLANES)
    win = buf_ref[pl.ds(start, hi - lo), col0:col0 + width]
    for res in sorted({(o - lo) % SUBLANES for o in offsets}):
        taps = [(k, (o - lo) // SUBLANES * SUBLANES) for k, o in enumerate(offsets) if (o - lo) % SUBLANES == res]
        yield (pltpu.roll(win, hi - lo - res, 0) if res else win), taps


def _conv_acc(buf_ref, w_ref, row0, rows, col0, width, offsets):
    acc = None
    for base, taps in _window_bases(buf_ref, row0, rows, col0, width, offsets):
        for k, q in taps:
            term = w_ref[k:k + 1, col0:col0 + width] * base[q:q + rows, :]
            acc = term if acc is None else acc + term
    return acc


def _conv_corr(buf_ref, w_ref, other, acc_ref, row0, rows, col0, width, offsets):
    acc = None
    for base, taps in _window_bases(buf_ref, row0, rows, col0, width, offsets):
        for k, q in taps:
            sl = base[q:q + rows, :]
            term = w_ref[k:k + 1, col0:col0 + width] * sl
            acc = term if acc is None else acc + term
            acc_ref[SUBLANES * k:SUBLANES * (k + 1), col0:col0 + width] += _rowsum8(sl * other)
    return acc


def _finish_tap_sums(acc_ref, out_ref, n_taps):
    for k in range(n_taps):
        out_ref[k:k + 1, :] = jnp.sum(acc_ref[SUBLANES * k:SUBLANES * (k + 1), :], axis=0, keepdims=True)


class _Comm:
    def __init__(self, inputs, out_shapes, sems, start, finish, aliases=None, mid=None):
        self.inputs, self.out_shapes, self.sems = list(inputs), list(out_shapes), list(sems)
        self.start, self.finish, self.mid, self.aliases = start, finish, mid, dict(aliases or {})


def _join_comm(a, b):
    ni, no, ns = len(a.inputs), len(a.out_shapes), len(a.sems)

    def both(name):
        def run(ins, outs, sems):
            if getattr(a, name) is not None:
                getattr(a, name)(ins[:ni], outs[:no], sems[:ns])
            if getattr(b, name) is not None:
                getattr(b, name)(ins[ni:], outs[no:], sems[ns:])
        return run

    aliases = {**a.aliases, **{ni + i: no + o for i, o in b.aliases.items()}}
    mid = both("mid") if (a.mid is not None or b.mid is not None) else None
    return _Comm(a.inputs + b.inputs, a.out_shapes + b.out_shapes, a.sems + b.sems, both("start"), both("finish"), aliases, mid)


def _call(body, *, name, grid, in_specs, out_specs, out_shape, args, scratch=(), comm=None, aliases=None):
    n_in, n_out, n_scr, n_axes = len(in_specs), len(out_specs), len(scratch), len(grid)
    params = pltpu.CompilerParams(dimension_semantics=("arbitrary",) * n_axes, vmem_limit_bytes=VMEM_LIMIT_BYTES)
    aliases = dict(aliases or {})
    if comm is None:
        outs = pl.pallas_call(body, name=name, grid=grid, in_specs=list(in_specs), out_specs=list(out_specs),
                              out_shape=list(out_shape), scratch_shapes=list(scratch), input_output_aliases=aliases,
                              compiler_params=params)(*args)
        return list(outs), []
    ci, co = len(comm.inputs), len(comm.out_shapes)

    def wrapped(*refs):
        k_in, c_in = refs[:n_in], refs[n_in:n_in + ci]
        o0 = n_in + ci
        k_out, c_out = refs[o0:o0 + n_out], refs[o0 + n_out:o0 + n_out + co]
        s0 = o0 + n_out + co
        k_scr, c_sem = refs[s0:s0 + n_scr], refs[s0 + n_scr:]
        first = pl.program_id(0) == 0
        last = pl.program_id(0) == grid[0] - 1
        for ax in range(1, n_axes):
            first = jnp.logical_and(first, pl.program_id(ax) == 0)
            last = jnp.logical_and(last, pl.program_id(ax) == grid[ax] - 1)

        @pl.when(first)
        def _():
            comm.start(c_in, c_out, c_sem)

        if comm.mid is not None and n_axes == 1 and grid[0] > MID_STEPS_BEFORE_END:
            @pl.when(pl.program_id(0) == grid[0] - 1 - MID_STEPS_BEFORE_END)
            def _():
                comm.mid(c_in, c_out, c_sem)

        body(*k_in, *k_out, *k_scr)

        @pl.when(last)
        def _():
            if comm.mid is not None and not (n_axes == 1 and grid[0] > MID_STEPS_BEFORE_END):
                comm.mid(c_in, c_out, c_sem)
            comm.finish(c_in, c_out, c_sem)

    outs = pl.pallas_call(
        wrapped, name=name, grid=grid, in_specs=list(in_specs) + [ANY] * ci, out_specs=list(out_specs) + [ANY] * co,
        out_shape=list(out_shape) + comm.out_shapes, scratch_shapes=list(scratch) + comm.sems,
        input_output_aliases={**aliases, **{n_in + i: n_out + o for i, o in comm.aliases.items()}}, compiler_params=params,
    )(*args, *comm.inputs)
    return list(outs[:n_out]), list(outs[n_out:])


def _run_comm(comm, name):
    ci, co = len(comm.inputs), len(comm.out_shapes)

    def body(*refs):
        c_in, c_out, c_sem = refs[:ci], refs[ci:ci + co], refs[ci + co:]
        comm.start(c_in, c_out, c_sem)
        if comm.mid is not None:
            comm.mid(c_in, c_out, c_sem)
        comm.finish(c_in, c_out, c_sem)

    return list(pl.pallas_call(body, name=name, in_specs=[ANY] * ci, out_specs=[ANY] * co, out_shape=comm.out_shapes,
                               scratch_shapes=comm.sems, input_output_aliases=comm.aliases)(*comm.inputs))


def _mix0_fwd(x, g, w_in, conv_a, ln_g, ln_b, conv_b, w_out, comm=None):
    s = x.shape[0]
    ts = min(TS_MIX, s)
    n = s // ts
    bw = NZ // NCHIP
    offs_a = [HALO_A - (K_A - 1) + k for k in range(K_A)]
    offs_s = [HALO_S - (K_S - 1) + k for k in range(K_S)]

    def body(x_ref, g_ref, win_ref, ca_ref, lg_ref, lb_ref, cb_ref, wout_ref,
             h_ref, z_ref, ac_ref, bconv_buf, cat_ref, x1_ref, glu_buf, cv_buf):
        i = pl.program_id(0)

        @pl.when(i == 0)
        def _():
            glu_buf[0:HALO_A, :] = jnp.zeros((HALO_A, A), F32)
            cv_buf[0:HALO_S, :] = jnp.zeros((HALO_S, A), F32)

        xv = x_ref[...]
        h = (xv * _rsqrt_mean_sq(xv) * g_ref[...]).astype(BF16)
        h_ref[...] = h
        for j in range(NCHIP):
            z_ref[:, j * bw:(j + 1) * bw] = jnp.dot(h, win_ref[j], preferred_element_type=F32)
        glu_buf[HALO_A:HALO_A + ts, :] = z_ref[:, 0:A] * _sigmoid(z_ref[:, A:2 * A])
        cv_buf[HALO_S:HALO_S + ts, :] = z_ref[:, 3 * A:4 * A] * z_ref[:, 4 * A:5 * A]

        def chunk(ci, carry):
            r0 = pl.multiple_of(ci * R_CHUNK, R_CHUNK)
            for c0 in range(0, A, LANES):
                ac_ref[pl.ds(r0, R_CHUNK), c0:c0 + LANES] = _conv_acc(glu_buf, ca_ref, r0, R_CHUNK, c0, LANES, offs_a)
                bconv_buf[pl.ds(r0, R_CHUNK), c0:c0 + LANES] = _conv_acc(cv_buf, cb_ref, r0, R_CHUNK, c0, LANES, offs_s)
            return carry

        lax.fori_loop(0, ts // R_CHUNK, chunk, 0)
        glu_buf[0:HALO_A, :] = glu_buf[ts:ts + HALO_A, :]
        cv_buf[0:HALO_S, :] = cv_buf[ts:ts + HALO_S, :]

        ac = ac_ref[...]
        xc = ac - jnp.mean(ac, axis=-1, keepdims=True)
        xn = xc * lax.rsqrt(jnp.mean(xc * xc, axis=-1, keepdims=True) + LN_EPS)
        ln = xn * lg_ref[...] + lb_ref[...]
        cat_ref[:, 0:A] = (ln * _sigmoid(ln)).astype(BF16)
        cat_ref[:, A:2 * A] = (z_ref[:, 2 * A:3 * A] * bconv_buf[...]).astype(BF16)
        x1_ref[...] = xv + jnp.dot(cat_ref[...], wout_ref[...], preferred_element_type=F32)

    tile = lambda w: pl.BlockSpec((ts, w), lambda i: (i, 0))
    return _call(
        body, name="mix0_fwd", grid=(n,), comm=comm, args=(x, g, w_in, conv_a, ln_g, ln_b, conv_b, w_out),
        in_specs=[tile(D), _const((1, D)), _const((NCHIP, D, bw)), _const((K_A, A)), _const((1, A)), _const((1, A)),
                  _const((K_S, A)), _const((2 * A, D))],
        out_specs=[tile(D), tile(NZ), tile(A), tile(A), tile(2 * A), tile(D)],
        out_shape=[jax.ShapeDtypeStruct((s, D), BF16), jax.ShapeDtypeStruct((s, NZ), F32), jax.ShapeDtypeStruct((s, A), F32),
                   jax.ShapeDtypeStruct((s, A), F32), jax.ShapeDtypeStruct((s, 2 * A), BF16), jax.ShapeDtypeStruct((s, D), F32)],
        scratch=[pltpu.VMEM((HALO_A + ts, A), F32), pltpu.VMEM((HALO_S + ts, A), F32)])


def _ffn_fwd(x, g, w_up, wc, w_down, name, comm=None):
    s = x.shape[0]
    ts = min(TS_FFN, s)
    n = s // ts
    bw = FF2 // NCHIP
    rows = 32
    offs = [HALO_S - (K_S - 1) + k for k in range(K_S)]

    def body(x_ref, g_ref, wup_ref, wc_ref, wdn_ref, h_ref, u0_ref, u_ref, act_ref, xo_ref, cbuf):
        i = pl.program_id(0)

        @pl.when(i == 0)
        def _():
            cbuf[0:HALO_S, :] = jnp.zeros((HALO_S, FF2), F32)

        xv = x_ref[...]
        h = (xv * _rsqrt_mean_sq(xv) * g_ref[...]).astype(BF16)
        h_ref[...] = h
        f = None
        for p in range(NCHIP // 2):
            for j in (p, NCHIP // 2 + p):
                zc = jnp.dot(h, wup_ref[j], preferred_element_type=F32)
                u0_ref[:, j * bw:(j + 1) * bw] = zc.astype(BF16)
                cbuf[HALO_S:HALO_S + ts, j * bw:(j + 1) * bw] = zc
            for r0 in range(0, ts, rows):
                for c0 in range(p * bw, (p + 1) * bw, LANES):
                    ug = _taps(wc_ref, _shifted(cbuf, r0, rows, c0, LANES, offs), offs, c0, LANES)
                    uv = _taps(wc_ref, _shifted(cbuf, r0, rows, FF + c0, LANES, offs), offs, FF + c0, LANES)
                    u_ref[r0:r0 + rows, c0:c0 + LANES] = ug
                    u_ref[r0:r0 + rows, FF + c0:FF + c0 + LANES] = uv
                    act_ref[r0:r0 + rows, c0:c0 + LANES] = (ug * _sigmoid(ug) * uv).astype(BF16)
            fp = jnp.dot(act_ref[:, p * bw:(p + 1) * bw], wdn_ref[p * bw:(p + 1) * bw, :], preferred_element_type=F32)
            f = fp if f is None else f + fp
        cbuf[0:HALO_S, :] = cbuf[ts:ts + HALO_S, :]
        xo_ref[...] = xv + f

    tile = lambda w: pl.BlockSpec((ts, w), lambda i: (i, 0))
    return _call(
        body, name=name, grid=(n,), comm=comm, args=(x, g, w_up, wc, w_down),
        in_specs=[tile(D), _const((1, D)), _const((NCHIP, D, bw)), _const((K_S, FF2)), _const((FF, D))],
        out_specs=[tile(D), tile(FF2), tile(FF2), tile(FF), tile(D)],
        out_shape=[jax.ShapeDtypeStruct((s, D), BF16), jax.ShapeDtypeStruct((s, FF2), BF16), jax.ShapeDtypeStruct((s, FF2), F32),
                   jax.ShapeDtypeStruct((s, FF), BF16), jax.ShapeDtypeStruct((s, D), F32)],
        scratch=[pltpu.VMEM((HALO_S + ts, FF2), F32)])


def _pool_windows(hbuf, pbuf, tile_row0, ts):
    def chunk(ci, carry):
        r0 = pl.multiple_of(ci * R_CHUNK, R_CHUNK)
        t1 = (tile_row0 + r0 + lax.broadcasted_iota(jnp.int32, (R_CHUNK, 1), 0) + 1).astype(F32)
        for gi, w in enumerate(POOL_WINDOWS):
            cnt = jnp.minimum(t1, float(w))
            offs = [HALO_P - jj for jj in range(w)]
            for c0 in range(gi * PG, (gi + 1) * PG, LANES):
                sh = _shifted(hbuf, r0, R_CHUNK, c0, LANES, offs)
                tot = sh[offs[0]]
                for o in offs[1:]:
                    tot = tot + sh[o]
                pbuf[pl.ds(r0, R_CHUNK), c0:c0 + LANES] = (tot / cnt - sh[HALO_P]).astype(BF16)
        return carry

    lax.fori_loop(0, ts // R_CHUNK, chunk, 0)


def _assemble_wpool(wp_ref, wps):
    rb = PG // NCHIP
    for gi in range(len(POOL_WINDOWS)):
        for j in range(NCHIP):
            wps[gi, j * rb:(j + 1) * rb, :] = wp_ref[j, gi]


def _pool_fwd(x, g, w_pool, scale):
    s = x.shape[0]
    ts = min(TS_POOL, s)
    n = s // ts
    ng = len(POOL_WINDOWS)

    def body(x_ref, g_ref, wp_ref, sc_ref, xo_ref, hbuf, pbuf, wps):
        i = pl.program_id(0)

        @pl.when(i == 0)
        def _():
            hbuf[0:HALO_P, :] = jnp.zeros((HALO_P, D), F32)
            _assemble_wpool(wp_ref, wps)

        xv = x_ref[...]
        hbuf[HALO_P:HALO_P + ts, :] = xv * _rsqrt_mean_sq(xv) * g_ref[...]
        _pool_windows(hbuf, pbuf, i * ts, ts)
        hbuf[0:HALO_P, :] = hbuf[ts:ts + HALO_P, :]
        for gi in range(ng):
            cols = slice(gi * PG, (gi + 1) * PG)
            y = jnp.dot(pbuf[:, cols], wps[gi], preferred_element_type=F32)
            xo_ref[:, cols] = xv[:, cols] + y * sc_ref[:, cols]

    tile = pl.BlockSpec((ts, D), lambda i: (i, 0))
    return pl.pallas_call(
        body, name="pool_fwd", grid=(n,),
        in_specs=[tile, _const((1, D)), _const((NCHIP, ng, PG // NCHIP, PG)), _const((1, D))],
        out_specs=tile, out_shape=jax.ShapeDtypeStruct((s, D), F32),
        scratch_shapes=[pltpu.VMEM((HALO_P + ts, D), F32), pltpu.VMEM((ts, D), BF16), pltpu.VMEM((ng, PG, PG), BF16)],
        compiler_params=_cparams(1),
    )(x, g, w_pool, scale)


def _final_loss(x, g, target):
    s = x.shape[0]
    ts = min(TS_MM, s)
    n = s // ts

    def body(x_ref, g_ref, t_ref, dx_ref, dg_ref, loss_ref):
        i = pl.program_id(0)

        @pl.when(i == 0)
        def _():
            dg_ref[...] = jnp.zeros((1, D), F32)
            loss_ref[...] = jnp.zeros((1, LANES), F32)

        xv = x_ref[...]
        r = _rsqrt_mean_sq(xv)
        xh = xv * r
        gv = g_ref[...]
        err = xh * gv - t_ref[...]
        sq = jnp.sum(jnp.sum(err * err, axis=1, keepdims=True), axis=0, keepdims=True)
        loss_ref[...] += sq * (0.5 / D)
        dy = err * (1.0 / D)
        dg_ref[...] += jnp.sum(dy * xh, axis=0, keepdims=True)
        dx_ref[...] = _rms_bwd(dy, xh, r, gv)

    tile = pl.BlockSpec((ts, D), lambda i: (i, 0))
    return pl.pallas_call(
        body, name="final_loss", grid=(n,),
        in_specs=[tile, _const((1, D)), tile],
        out_specs=[tile, pl.BlockSpec((1, D), lambda i: (0, 0)), pl.BlockSpec((1, LANES), lambda i: (0, 0))],
        out_shape=[jax.ShapeDtypeStruct((s, D), F32), jax.ShapeDtypeStruct((1, D), F32), jax.ShapeDtypeStruct((1, LANES), F32)],
        compiler_params=_cparams(1),
    )(x, g, target)


def _ffn_bwd_a(dxo, u, u0, wc, w_down, name, comm=None):
    s = dxo.shape[0]
    ts = min(TS_FFN, s)
    n = s // ts
    cw = FF2 // NCHIP
    rows = 32
    lw2 = 2 * LANES
    boffs = [K_S - 1 - k for k in range(K_S)]

    def body(dxo_ref, u_ref, u0_ref, wc_ref, wdn_ref, du0_ref, dwc_ref, dubuf, dact, dwacc):
        i = pl.program_id(0)

        @pl.when(i == 0)
        def _():
            dubuf[ts:ts + HALO_S, :] = jnp.zeros((HALO_S, FF2), F32)
            dwacc[...] = jnp.zeros(dwacc.shape, F32)

        df = dxo_ref[...].astype(BF16)
        for cg in range(0, FF, cw):
            dact[...] = lax.dot_general(df, wdn_ref[cg:cg + cw, :], NT_DIMS, preferred_element_type=F32)

            def chunk(ci, carry, cg=cg):
                rs = pl.ds(pl.multiple_of(ci * rows, rows), rows)
                for c in range(cg, cg + cw, LANES):
                    ug, uv = u_ref[rs, c:c + LANES], u_ref[rs, FF + c:FF + c + LANES]
                    sg = _sigmoid(ug)
                    da = dact[rs, c - cg:c - cg + LANES]
                    gs = ug * sg
                    dubuf[rs, c:c + LANES] = da * uv * (sg + gs * (1.0 - sg))
                    dubuf[rs, FF + c:FF + c + LANES] = da * gs
                return carry

            lax.fori_loop(0, ts // rows, chunk, 0)

        def chunk2(ci, carry):
            r0 = pl.multiple_of(ci * rows, rows)
            rs = pl.ds(r0, rows)
            for c in range(0, FF2, lw2):
                sh = _shifted(dubuf, r0, rows, c, lw2, boffs)
                du0_ref[rs, c:c + lw2] = _taps(wc_ref, sh, boffs, c, lw2).astype(BF16)
                u0v = u0_ref[rs, c:c + lw2].astype(F32)
                for k, o in enumerate(boffs):
                    dwacc[SUBLANES * k:SUBLANES * (k + 1), c:c + lw2] += _rowsum8(sh[o] * u0v)
            return carry

        lax.fori_loop(0, ts // rows, chunk2, 0)
        dubuf[ts:ts + HALO_S, :] = dubuf[0:HALO_S, :]

        @pl.when(i == n - 1)
        def _():
            _finish_tap_sums(dwacc, dwc_ref, K_S)

    rev = lambda w: pl.BlockSpec((ts, w), lambda i: (n - 1 - i, 0))
    return _call(
        body, name=name, grid=(n,), comm=comm, args=(dxo, u, u0, wc, w_down),
        in_specs=[rev(D), rev(FF2), rev(FF2), _const((K_S, FF2)), _const((FF, D))],
        out_specs=[rev(FF2), pl.BlockSpec((K_S, FF2), lambda i: (0, 0))],
        out_shape=[jax.ShapeDtypeStruct((s, FF2), BF16), jax.ShapeDtypeStruct((K_S, FF2), F32)],
        scratch=[pltpu.VMEM((ts + HALO_S, FF2), F32), pltpu.VMEM((ts, cw), F32), pltpu.VMEM((SUBLANES * K_S, FF2), F32)])


def _nt_rms_bwd(dy, w, x, g, dres, name, comm=None):
    s = x.shape[0]
    ts = min(TS_MM, s)
    n = s // ts
    nw = dy.shape[1]
    bw = nw // NCHIP

    def body(dy_ref, w_ref, x_ref, g_ref, dres_ref, dx_ref, dg_ref):
        i = pl.program_id(0)

        @pl.when(i == 0)
        def _():
            dg_ref[...] = jnp.zeros((1, D), F32)

        dh = lax.dot_general(dy_ref[:, 0:bw], w_ref[0], NT_DIMS, preferred_element_type=F32)
        for j in range(1, NCHIP):
            dh = dh + lax.dot_general(dy_ref[:, j * bw:(j + 1) * bw], w_ref[j], NT_DIMS, preferred_element_type=F32)
        xv = x_ref[...]
        r = _rsqrt_mean_sq(xv)
        xh = xv * r
        dg_ref[...] += jnp.sum(dh * xh, axis=0, keepdims=True)
        dx_ref[...] = dres_ref[...] + _rms_bwd(dh, xh, r, g_ref[...])

    tile = lambda wd: pl.BlockSpec((ts, wd), lambda i: (i, 0))
    return _call(
        body, name=name, grid=(n,), comm=comm, args=(dy, w, x, g, dres),
        in_specs=[tile(nw), _const((NCHIP, D, bw)), tile(D), _const((1, D)), tile(D)],
        out_specs=[tile(D), pl.BlockSpec((1, D), lambda i: (0, 0))],
        out_shape=[jax.ShapeDtypeStruct((s, D), F32), jax.ShapeDtypeStruct((1, D), F32)])


def _wgrad(a, b, n_blocks, name, comm=None):
    s, m = a.shape
    bw = b.shape[1] // n_blocks
    tk = min(TS_MM, s)

    def body(a_ref, b_ref, o_ref):
        @pl.when(pl.program_id(0) == 0)
        def _():
            o_ref[...] = jnp.zeros(o_ref.shape, F32)

        for j in range(n_blocks):
            o_ref[j] += lax.dot_general(a_ref[...], b_ref[:, j * bw:(j + 1) * bw].astype(BF16), TN_DIMS,
                                        preferred_element_type=F32)

    (out,), comm_out = _call(
        body, name=name, grid=(s // tk,), comm=comm, args=(a, b),
        in_specs=[pl.BlockSpec((tk, m), lambda k: (k, 0)), pl.BlockSpec((tk, b.shape[1]), lambda k: (k, 0))],
        out_specs=[pl.BlockSpec((n_blocks, m, bw), lambda k: (0, 0, 0))],
        out_shape=[jax.ShapeDtypeStruct((n_blocks, m, bw), F32)])
    return out, comm_out


def _pool_bwd(dxo, x, g, w_pool, scale, comm=None):
    s = x.shape[0]
    ts = min(TS_POOL, s)
    n = s // ts
    ng = len(POOL_WINDOWS)
    rb = PG // NCHIP

    def body(dxo_ref, x_ref, halo_ref, g_ref, wp_ref, sc_ref, dx_ref, dwp_ref, dsc_ref, dg_ref,
             hbuf, pbuf, qbuf, dhbuf, wps, dwacc):
        i = pl.program_id(0)
        j = n - 1 - i

        @pl.when(i == 0)
        def _():
            qbuf[ts:ts + HALO_P, :] = jnp.zeros((HALO_P, D), F32)
            dwacc[...] = jnp.zeros(dwacc.shape, F32)
            dsc_ref[...] = jnp.zeros((1, D), F32)
            dg_ref[...] = jnp.zeros((1, D), F32)
            _assemble_wpool(wp_ref, wps)

        gv = g_ref[...]
        xl = halo_ref[...]
        hbuf[0:HALO_P, :] = jnp.where(j == 0, 0.0, xl * _rsqrt_mean_sq(xl) * gv)
        xv = x_ref[...]
        r = _rsqrt_mean_sq(xv)
        xh = xv * r
        hbuf[HALO_P:HALO_P + ts, :] = xh * gv
        _pool_windows(hbuf, pbuf, j * ts, ts)

        dy = dxo_ref[...]
        t1 = (j * ts + lax.broadcasted_iota(jnp.int32, (ts, 1), 0) + 1).astype(F32)
        for gi, w in enumerate(POOL_WINDOWS):
            cols = slice(gi * PG, (gi + 1) * PG)
            p = pbuf[:, cols]
            y = jnp.dot(p, wps[gi], preferred_element_type=F32)
            dsc_ref[:, cols] += jnp.sum(dy[:, cols] * y, axis=0, keepdims=True)
            dq = (dy[:, cols] * sc_ref[:, cols]).astype(BF16)
            dwacc[gi] += lax.dot_general(p, dq, TN_DIMS, preferred_element_type=F32)
            dp = lax.dot_general(dq, wps[gi], NT_DIMS, preferred_element_type=F32)
            qbuf[0:ts, cols] = dp / jnp.minimum(t1, float(w))

        def chunk(ci, carry):
            r0 = pl.multiple_of(ci * R_CHUNK, R_CHUNK)
            tc = (j * ts + r0 + lax.broadcasted_iota(jnp.int32, (R_CHUNK, 1), 0) + 1).astype(F32)
            for gi, w in enumerate(POOL_WINDOWS):
                cnt = jnp.minimum(tc, float(w))
                offs = list(range(w))
                for c0 in range(gi * PG, (gi + 1) * PG, LANES):
                    sh = _shifted(qbuf, r0, R_CHUNK, c0, LANES, offs)
                    tot = sh[0]
                    for o in offs[1:]:
                        tot = tot + sh[o]
                    dhbuf[pl.ds(r0, R_CHUNK), c0:c0 + LANES] = tot - sh[0] * cnt
            return carry

        lax.fori_loop(0, ts // R_CHUNK, chunk, 0)
        qbuf[ts:ts + HALO_P, :] = qbuf[0:HALO_P, :]
        dh = dhbuf[...]
        dg_ref[...] += jnp.sum(dh * xh, axis=0, keepdims=True)
        dx_ref[...] = dy + _rms_bwd(dh, xh, r, gv)

        @pl.when(i == n - 1)
        def _():
            for gi in range(ng):
                for jj in range(NCHIP):
                    dwp_ref[jj, gi] = dwacc[gi, jj * rb:(jj + 1) * rb, :]

    rev = pl.BlockSpec((ts, D), lambda i: (n - 1 - i, 0))
    halo = pl.BlockSpec((HALO_P, D), lambda i: (jnp.maximum((n - 1 - i) * (ts // HALO_P) - 1, 0), 0))
    vec = pl.BlockSpec((1, D), lambda i: (0, 0))
    return _call(
        body, name="pool_bwd", grid=(n,), comm=comm, args=(dxo, x, x, g, w_pool, scale),
        in_specs=[rev, rev, halo, _const((1, D)), _const((NCHIP, ng, rb, PG)), _const((1, D))],
        out_specs=[rev, pl.BlockSpec((NCHIP, ng, rb, PG), lambda i: (0, 0, 0, 0)), vec, vec],
        out_shape=[jax.ShapeDtypeStruct((s, D), F32), jax.ShapeDtypeStruct((NCHIP, ng, rb, PG), F32),
                   jax.ShapeDtypeStruct((1, D), F32), jax.ShapeDtypeStruct((1, D), F32)],
        scratch=[pltpu.VMEM((HALO_P + ts, D), F32), pltpu.VMEM((ts, D), BF16), pltpu.VMEM((ts + HALO_P, D), F32),
                 pltpu.VMEM((ts, D), F32), pltpu.VMEM((ng, PG, PG), BF16), pltpu.VMEM((ng, PG, PG), F32)])


def _mix0_bwd_a(dx1, z, ac, bconv, w_out, conv_a, ln_g, ln_b, conv_b, comm=None):
    s = dx1.shape[0]
    ts = min(TS_MIXB, s)
    n = s // ts
    rows = 32
    boffs_a = [K_A - 1 - k for k in range(K_A)]
    boffs_s = [K_S - 1 - k for k in range(K_S)]

    def body(dx1_ref, z_ref, ac_ref, bconv_ref, wout_ref, ca_ref, lg_ref, lb_ref, cb_ref,
             dz_ref, dca_ref, dlg_ref, dlb_ref, dcb_ref, dac_buf, dbc_buf, dca_acc, dcb_acc):
        i = pl.program_id(0)

        @pl.when(i == 0)
        def _():
            dac_buf[ts:ts + HALO_A, :] = jnp.zeros((HALO_A, A), F32)
            dbc_buf[ts:ts + HALO_S, :] = jnp.zeros((HALO_S, A), F32)
            dca_acc[...] = jnp.zeros(dca_acc.shape, F32)
            dcb_acc[...] = jnp.zeros(dcb_acc.shape, F32)
            dlg_ref[...] = jnp.zeros((1, A), F32)
            dlb_ref[...] = jnp.zeros((1, A), F32)

        dcat = lax.dot_general(dx1_ref[...].astype(BF16), wout_ref[...], NT_DIMS, preferred_element_type=F32)
        db = dcat[:, A:2 * A]
        dz_ref[:, 2 * A:3 * A] = (db * bconv_ref[...]).astype(BF16)
        dbc_buf[0:ts, :] = db * z_ref[:, 2 * A:3 * A]
        ac = ac_ref[...]
        xc = ac - jnp.mean(ac, axis=-1, keepdims=True)
        rstd = lax.rsqrt(jnp.mean(xc * xc, axis=-1, keepdims=True) + LN_EPS)
        xn = xc * rstd
        lg = lg_ref[...]
        ln = xn * lg + lb_ref[...]
        sl = _sigmoid(ln)
        dln = dcat[:, 0:A] * sl * (1.0 + ln * (1.0 - sl))
        dlg_ref[...] += jnp.sum(dln * xn, axis=0, keepdims=True)
        dlb_ref[...] += jnp.sum(dln, axis=0, keepdims=True)
        dxn = dln * lg
        dac_buf[0:ts, :] = rstd * (dxn - jnp.mean(dxn, axis=-1, keepdims=True)
                                   - xn * jnp.mean(dxn * xn, axis=-1, keepdims=True))

        def chunk(ci, carry):
            r0 = pl.multiple_of(ci * rows, rows)
            rs = pl.ds(r0, rows)
            for c0 in range(0, A, LANES):
                col = lambda grp: slice(grp * A + c0, grp * A + c0 + LANES)
                a_val, sg = z_ref[rs, col(0)], _sigmoid(z_ref[rs, col(1)])
                dglu = _conv_corr(dac_buf, ca_ref, a_val * sg, dca_acc, r0, rows, c0, LANES, boffs_a)
                dz_ref[rs, col(0)] = (dglu * sg).astype(BF16)
                dz_ref[rs, col(1)] = (dglu * a_val * sg * (1.0 - sg)).astype(BF16)
                c_gate, bc_val = z_ref[rs, col(3)], z_ref[rs, col(4)]
                dcv = _conv_corr(dbc_buf, cb_ref, c_gate * bc_val, dcb_acc, r0, rows, c0, LANES, boffs_s)
                dz_ref[rs, col(3)] = (dcv * bc_val).astype(BF16)
                dz_ref[rs, col(4)] = (dcv * c_gate).astype(BF16)
            return carry

        lax.fori_loop(0, ts // rows, chunk, 0)
        dac_buf[ts:ts + HALO_A, :] = dac_buf[0:HALO_A, :]
        dbc_buf[ts:ts + HALO_S, :] = dbc_buf[0:HALO_S, :]

        @pl.when(i == n - 1)
        def _():
            _finish_tap_sums(dca_acc, dca_ref, K_A)
            _finish_tap_sums(dcb_acc, dcb_ref, K_S)

    rev = lambda w: pl.BlockSpec((ts, w), lambda i: (n - 1 - i, 0))
    full = lambda r, c: pl.BlockSpec((r, c), lambda i: (0, 0))
    return _call(
        body, name="mix0_bwd_a", grid=(n,), comm=comm, args=(dx1, z, ac, bconv, w_out, conv_a, ln_g, ln_b, conv_b),
        in_specs=[rev(D), rev(NZ), rev(A), rev(A), _const((2 * A, D)), _const((K_A, A)), _const((1, A)), _const((1, A)),
                  _const((K_S, A))],
        out_specs=[rev(NZ), full(K_A, A), full(1, A), full(1, A), full(K_S, A)],
        out_shape=[jax.ShapeDtypeStruct((s, NZ), BF16), jax.ShapeDtypeStruct((K_A, A), F32), jax.ShapeDtypeStruct((1, A), F32),
                   jax.ShapeDtypeStruct((1, A), F32), jax.ShapeDtypeStruct((K_S, A), F32)],
        scratch=[pltpu.VMEM((ts + HALO_A, A), F32), pltpu.VMEM((ts + HALO_S, A), F32), pltpu.VMEM((SUBLANES * K_A, A), F32),
                 pltpu.VMEM((SUBLANES * K_S, A), F32)])


def _train_step(x, target, place, shard, rep, small_pack, small_shapes, adamw):
    bw_up, bw_in = FF2 // NCHIP, NZ // NCHIP
    five = lambda a, k: a.reshape(NCHIP, 2, *HALF[k])

    g_in, g_out, small_g = _run_comm(_gather_comm([shard["w_in"], shard["w_out"]], small_pack), "ag_first")
    whole = {}
    for k, part in zip(SMALL_SHARDED, _unpack(small_g, small_shapes, lead=(NCHIP,))):
        whole[k] = jnp.moveaxis(part, 0, 1).reshape(part.shape[1], NCHIP * part.shape[2])
    w_in, w_out = g_in.reshape(NCHIP, D, bw_in), g_out.reshape(2 * A, D)
    conv_ffn = whole["conv_ffn_w"].reshape(2, K_S, FF2)
    nffn = [rep["norm_ffn"][0:1], rep["norm_ffn"][1:2]]

    (h0, z, ac, bconv, cat, x1), (g_up0, g_dn0) = _mix0_fwd(
        x, rep["norm_mix_even"], w_in, whole["conv_a"], rep["ln_a_g"], rep["ln_a_b"], whole["conv_b"], w_out,
        comm=_gather_comm([shard["w_up0"], shard["w_down0"]]))
    w_up0, w_dn0 = g_up0.reshape(NCHIP, D, bw_up), g_dn0.reshape(FF, D)
    (hf0, u00, u0, act0, x2), (g_pool, g_up1, g_dn1) = _ffn_fwd(
        x1, nffn[0], w_up0, conv_ffn[0], w_dn0, "ffn0_fwd",
        comm=_gather_comm([shard["w_pool"], shard["w_up1"], shard["w_down1"]], early_forward=True))
    w_pool = g_pool.reshape(NCHIP, len(POOL_WINDOWS), PG // NCHIP, PG)
    w_up1, w_dn1 = g_up1.reshape(NCHIP, D, bw_up), g_dn1.reshape(FF, D)
    x3 = _pool_fwd(x2, whole["norm_mix_odd"], w_pool, whole["pool_scale"])
    (hf1, u01, u1, act1, x4), _ = _ffn_fwd(x3, nffn[1], w_up1, conv_ffn[1], w_dn1, "ffn1_fwd")
    dx4, g_nfin, loss_part = _final_loss(x4, rep["norm_final"], target)

    psum = lambda k, g, ld, tag="": _pair_sum(place, g, ld, "pair_sum_" + k + tag)
    tot = lambda k, g, ld, p, layer=0, nl=1, prev=None: _chip_sum(place, g, ld, p, layer, nl, "chip_sum_%s%d" % (k, layer), prev)
    pair, chips, join = _pair_comm, _chips_comm, _join_comm

    gr_dn1 = five(_wgrad(act1, dx4, 1, "wgrad_down1")[0], "w_down")
    (du01, g_wc1), _ = _ffn_bwd_a(dx4, u1, u01, conv_ffn[1], w_dn1, "ffn1_bwd_a")
    g_up, (ld_dn1,) = _wgrad(hf1, du01, NCHIP, "wgrad_up1", comm=pair([gr_dn1]))
    gr_up1 = five(g_up, "w_up")
    s_dn1 = psum("w_down", gr_dn1, ld_dn1, "1")
    (dx3, g_nf1), (p_dn1, ld_up1) = _nt_rms_bwd(du01, w_up1, x3, nffn[1], dx4, "ffn1_bwd_b",
                                                comm=join(chips([s_dn1]), pair([gr_up1])))
    s_up1 = psum("w_up", gr_up1, ld_up1, "1")
    (dx2, g_wpool, g_scale, g_nmo), _ = _pool_bwd(dx3, x2, whole["norm_mix_odd"], w_pool, whole["pool_scale"])
    gr_pool = five(g_wpool, "w_pool")
    (du00, g_wc0), (p_up1,) = _ffn_bwd_a(dx2, u0, u00, conv_ffn[0], w_dn0, "ffn0_bwd_a", comm=chips([s_up1]))
    gr_dn0 = five(_wgrad(act0, dx2, 1, "wgrad_down0")[0], "w_down")
    g_up, (ld_dn0, ld_pool) = _wgrad(hf0, du00, NCHIP, "wgrad_up0", comm=pair([gr_dn0, gr_pool]))
    gr_up0 = five(g_up, "w_up")
    s_dn0, s_pool = psum("w_down", gr_dn0, ld_dn0, "0"), psum("w_pool", gr_pool, ld_pool)
    (dx1, g_nf0), (p_dn0, p_pool, ld_up0) = _nt_rms_bwd(du00, w_up0, x1, nffn[0], dx2, "ffn0_bwd_b",
                                                        comm=join(chips([s_dn0, s_pool]), pair([gr_up0])))
    s_up0 = psum("w_up", gr_up0, ld_up0, "0")
    gr_out = five(_wgrad(cat, dx1, 1, "wgrad_out")[0], "w_out")
    (dz, g_ca, g_lg, g_lb, g_cb), (p_up0, ld_out) = _mix0_bwd_a(
        dx1, z, ac, bconv, w_out, whole["conv_a"], rep["ln_a_g"], rep["ln_a_b"], whole["conv_b"],
        comm=join(chips([s_up0]), pair([gr_out])))
    s_out = psum("w_out", gr_out, ld_out)
    t_pool = tot("w_pool", gr_pool, ld_pool, p_pool)
    t_up = tot("w_up", gr_up0, ld_up0, p_up0, 0, 2, tot("w_up", gr_up1, ld_up1, p_up1, 1, 2))
    t_dn = tot("w_down", gr_dn0, ld_dn0, p_dn0, 0, 2, tot("w_down", gr_dn1, ld_dn1, p_dn1, 1, 2))
    small = {"norm_mix_odd": g_nmo, "pool_scale": g_scale, "norm_ffn": jnp.concatenate([g_nf0, g_nf1], axis=0),
             "conv_ffn_w": jnp.stack([g_wc0, g_wc1]), "norm_final": g_nfin, "loss": loss_part}
    g_in, (p_out, early_all, t_pool, t_up, t_dn) = _wgrad(
        h0, dz, NCHIP, "wgrad_in",
        comm=join(chips([s_out], _pack([small[k] for k in SMALL_EARLY])), _swap_comm([t_pool, t_up, t_dn])))
    gr_in = five(g_in, "w_in")
    (grad_x, g_nme), (ld_in,) = _nt_rms_bwd(dz, w_in, x, rep["norm_mix_even"], dx1, "mix0_bwd_b", comm=pair([gr_in]))
    s_in = psum("w_in", gr_in, ld_in)
    small.update({"norm_mix_even": g_nme, "conv_a": g_ca, "ln_a_g": g_lg, "ln_a_b": g_lb, "conv_b": g_cb})
    p_in, late_all = _run_comm(chips([s_in], _pack([small[k] for k in SMALL_LATE])), "rs_last")
    t_in, t_out = _run_comm(_swap_comm([tot("w_in", gr_in, ld_in, p_in), tot("w_out", gr_out, ld_out, p_out)]), "rs_swap")
    done = {k: adamw(k, t) for k, t in (("w_up", t_up), ("w_down", t_dn), ("w_pool", t_pool), ("w_in", t_in), ("w_out", t_out))}
    summed = dict(zip(SMALL_EARLY, _unpack(_sum_devices(early_all, "sum_small_early"), [small[k].shape for k in SMALL_EARLY])))
    summed.update(zip(SMALL_LATE, _unpack(_sum_devices(late_all, "sum_small_late"), [small[k].shape for k in SMALL_LATE])))
    return grad_x, done, summed


def _adamw_math(w, g, m, v):
    m = ADAM_B1 * m + (1.0 - ADAM_B1) * g
    v = ADAM_B2 * v + (1.0 - ADAM_B2) * (g * g)
    m_hat = m / (1.0 - ADAM_B1 ** ADAM_STEP)
    v_hat = v / (1.0 - ADAM_B2 ** ADAM_STEP)
    return -ADAM_LR * (m_hat / (jnp.sqrt(v_hat) + ADAM_EPS) + ADAM_WD * w), m, v


def _adamw_big(w, g, m, v, tr, name, comm=None):
    nl, rows, cols = w.shape

    def body(w_ref, g_ref, m_ref, v_ref, g2_ref, d_ref, m2_ref, v2_ref):
        gv = g_ref[...]
        g2_ref[...] = gv
        d_ref[...], m2_ref[...], v2_ref[...] = _adamw_math(w_ref[...], gv, m_ref[...], v_ref[...])

    spec = pl.BlockSpec((None, tr, cols), lambda l, r: (l, r, 0))
    return _call(body, name=name, grid=(nl, rows // tr), comm=comm, args=(w, g, m, v), in_specs=[spec] * 4,
                 out_specs=[spec] * 4, out_shape=[jax.ShapeDtypeStruct(w.shape, F32)] * 4)


def _adamw_small(ws, gs, ms, vs):
    n = len(ws)

    def body(*refs):
        for p in range(n):
            w_ref, g_ref, m_ref, v_ref = (refs[q * n + p] for q in range(4))
            d_ref, m2_ref, v2_ref = (refs[(4 + q) * n + p] for q in range(3))
            d_ref[...], m2_ref[...], v2_ref[...] = _adamw_math(w_ref[...], g_ref[...], m_ref[...], v_ref[...])

    whole = lambda a: pl.BlockSpec(a.shape, lambda: (0,) * a.ndim)
    outs = pl.pallas_call(
        body, name="adamw_small", in_specs=[whole(a) for a in ws] * 4, out_specs=[whole(a) for a in ws] * 3,
        out_shape=[jax.ShapeDtypeStruct(a.shape, F32) for a in ws] * 3,
        compiler_params=pltpu.CompilerParams(vmem_limit_bytes=VMEM_LIMIT_BYTES),
    )(*ws, *gs, *ms, *vs)
    return outs[0:n], outs[n:2 * n], outs[2 * n:3 * n]


def _place():
    x, y, c = lax.axis_index("x"), lax.axis_index("y"), lax.axis_index("c")
    chips = [(x, 1 - y), (1 - x, y), (1 - x, 1 - y)]
    blocks = [2 * cx + cy for cx, cy in chips]
    return x, y, c, 2 * x + y, chips, blocks


def _gather_comm(shards, small=None, early_forward=False):
    na = len(shards)
    nk = NCHIP - 1
    ns = 0 if small is None else 1

    def copies(ins, outs, sems):
        ici_send, ici_recv, fwd_send, fwd_recv, own_send, own_recv = sems[:6]
        x, y, c, j, chips, blocks = _place()
        sib = (x, y, 1 - c)

        def ici(a, k, arrival):
            dst = outs[a].at[blocks[k], c] if arrival else outs[a].at[j, c]
            return pltpu.make_async_remote_copy(
                src_ref=dst if arrival else ins[a].at[c], dst_ref=dst, send_sem=ici_send.at[a * nk + k],
                recv_sem=ici_recv.at[a * nk + k], device_id=(*chips[k], c), device_id_type=MESH)

        def fwd(a, k, half):
            ref = outs[a].at[blocks[k], half]
            return pltpu.make_async_remote_copy(
                src_ref=ref, dst_ref=ref, send_sem=fwd_send.at[a * nk + k], recv_sem=fwd_recv.at[a * nk + k],
                device_id=sib, device_id_type=MESH)

        own = [pltpu.make_async_remote_copy(src_ref=ins[a], dst_ref=outs[a].at[j], send_sem=own_send.at[a],
                                            recv_sem=own_recv.at[a], device_id=sib, device_id_type=MESH) for a in range(na)]
        small_copies = [pltpu.make_async_remote_copy(
            src_ref=ins[na], dst_ref=outs[na].at[j], send_sem=ici_send.at[na * nk + k], recv_sem=ici_recv.at[na * nk + k],
            device_id=(*chips[k], c), device_id_type=MESH) for k in range(nk * ns)]
        local = [pltpu.make_async_copy(ins[na], outs[na].at[j], sems[6])] if ns else []
        return ici, fwd, own, small_copies, local, c

    def start(ins, outs, sems):
        ici, _, own, small_copies, local, _ = copies(ins, outs, sems)
        for cp in local + own + [ici(a, k, False) for a in range(na) for k in range(nk)] + small_copies:
            cp.start()

    def mid(ins, outs, sems):
        ici, fwd, _, _, _, c = copies(ins, outs, sems)
        for a in range(na):
            for k in range(nk):
                ici(a, k, True).wait_recv()
                fwd(a, k, c).start()

    def finish(ins, outs, sems):
        ici, fwd, own, small_copies, local, c = copies(ins, outs, sems)
        for cp in small_copies:
            cp.wait()
        for a in range(na):
            for k in range(nk):
                ici(a, k, False).wait_send()
                fwd(a, k, c).wait_send()
                fwd(a, k, 1 - c).wait_recv()
        for cp in own + local:
            cp.wait()

    out_shapes = [jax.ShapeDtypeStruct((NCHIP,) + s.shape, s.dtype) for s in shards]
    sems = [pltpu.SemaphoreType.DMA((na * nk + nk * ns,)), pltpu.SemaphoreType.DMA((na * nk + nk * ns,)),
            pltpu.SemaphoreType.DMA((na * nk,)), pltpu.SemaphoreType.DMA((na * nk,)),
            pltpu.SemaphoreType.DMA((na,)), pltpu.SemaphoreType.DMA((na,))]
    if ns:
        out_shapes.append(jax.ShapeDtypeStruct((NCHIP,) + small.shape, small.dtype))
        sems.append(pltpu.SemaphoreType.DMA)
    if early_forward:
        return _Comm(list(shards) + [small] * ns, out_shapes, sems, start, finish, mid=mid)

    def forward_and_finish(ins, outs, sems):
        mid(ins, outs, sems)
        finish(ins, outs, sems)

    return _Comm(list(shards) + [small] * ns, out_shapes, sems, start, forward_and_finish)


def _simple_comm(inputs, out_shapes, make_copies, n_sems, aliases=None):
    def start(ins, outs, sems):
        for cp in make_copies(ins, outs, sems):
            cp.start()

    def finish(ins, outs, sems):
        for cp in make_copies(ins, outs, sems):
            cp.wait()

    return _Comm(inputs, out_shapes, [pltpu.SemaphoreType.DMA((n,)) for n in n_sems], start, finish, aliases)


def _pair_comm(grads):
    def make_copies(ins, outs, sems):
        x, y, c, _, _, _ = _place()
        return [pltpu.make_async_remote_copy(
            src_ref=ins[a].at[:, 1 - c], dst_ref=outs[a], send_sem=sems[0].at[a], recv_sem=sems[1].at[a],
            device_id=(x, y, 1 - c), device_id_type=MESH) for a in range(len(grads))]

    out_shapes = [jax.ShapeDtypeStruct(g.shape[:1] + g.shape[2:], F32) for g in grads]
    return _simple_comm(grads, out_shapes, make_copies, [len(grads)] * 2)


def _chips_comm(sums, small=None):
    na = len(sums)
    nk = NCHIP - 1

    def make_copies(ins, outs, sems):
        x, y, c, _, chips, blocks = _place()
        copies = [pltpu.make_async_remote_copy(
            src_ref=ins[a].at[blocks[k]], dst_ref=outs[a].at[k], send_sem=sems[0].at[a * nk + k],
            recv_sem=sems[1].at[a * nk + k], device_id=(*chips[k], c), device_id_type=MESH)
            for a in range(na) for k in range(nk)]
        if small is not None:
            me = 4 * x + 2 * y + c
            for r in range(1, NDEV):
                peer = (1 - x if r & 4 else x, 1 - y if r & 2 else y, 1 - c if r & 1 else c)
                copies.append(pltpu.make_async_remote_copy(
                    src_ref=ins[na], dst_ref=outs[na].at[me], send_sem=sems[2].at[r - 1], recv_sem=sems[3].at[r - 1],
                    device_id=peer, device_id_type=MESH))
            copies.append(pltpu.make_async_copy(ins[na], outs[na].at[me], sems[4].at[0]))
        return copies

    out_shapes = [jax.ShapeDtypeStruct((nk,) + g.shape[1:], BF16) for g in sums]
    if small is None:
        return _simple_comm(sums, out_shapes, make_copies, [na * nk] * 2)
    out_shapes.append(jax.ShapeDtypeStruct((NDEV,) + small.shape, F32))
    return _simple_comm(list(sums) + [small], out_shapes, make_copies, [na * nk] * 2 + [NDEV - 1] * 2 + [1])


def _swap_comm(totals):
    def make_copies(ins, outs, sems):
        x, y, c, _, _, _ = _place()
        return [pltpu.make_async_remote_copy(
            src_ref=outs[a].at[:, c], dst_ref=outs[a].at[:, c], send_sem=sems[0].at[a], recv_sem=sems[1].at[a],
            device_id=(x, y, 1 - c), device_id_type=MESH) for a in range(len(totals))]

    out_shapes = [jax.ShapeDtypeStruct(t.shape, F32) for t in totals]
    return _simple_comm(totals, out_shapes, make_copies, [len(totals)] * 2, aliases={a: a for a in range(len(totals))})


def _pair_sum(place, grad, landed, name):
    _, _, rows, cols = grad.shape

    def body(place_ref, g_ref, l_ref, o_ref):
        o_ref[...] = (g_ref[...] + l_ref[...]).astype(BF16)

    return pl.pallas_call(
        body, name=name,
        grid_spec=pltpu.PrefetchScalarGridSpec(
            num_scalar_prefetch=1, grid=(NCHIP,),
            in_specs=[pl.BlockSpec((None, None, rows, cols), lambda b, p: (b, p[1], 0, 0)),
                      pl.BlockSpec((None, rows, cols), lambda b, p: (b, 0, 0))],
            out_specs=pl.BlockSpec((None, rows, cols), lambda b, p: (b, 0, 0))),
        out_shape=jax.ShapeDtypeStruct((NCHIP, rows, cols), BF16), compiler_params=_cparams(1),
    )(place, grad, landed)


def _chip_sum(place, grad, landed, parts, layer, n_layers, name, prev=None):
    _, _, rows, cols = grad.shape

    def body(*refs):
        g_ref, l_ref, p_ref, o_ref = refs[1], refs[2], refs[3], refs[-1]
        tot = g_ref[...] + l_ref[...]
        for k in range(NCHIP - 1):
            tot = tot + p_ref[k].astype(F32)
        o_ref[...] = tot

    in_specs = [pl.BlockSpec((None, None, rows, cols), lambda i, p: (p[0], p[1], 0, 0)),
                pl.BlockSpec((None, rows, cols), lambda i, p: (p[0], 0, 0)),
                pl.BlockSpec((NCHIP - 1, rows, cols), lambda i, p: (0, 0, 0))]
    args = [place, grad, landed, parts]
    if prev is not None:
        in_specs.append(ANY)
        args.append(prev)
    return pl.pallas_call(
        body, name=name,
        grid_spec=pltpu.PrefetchScalarGridSpec(
            num_scalar_prefetch=1, grid=(1,), in_specs=in_specs,
            out_specs=pl.BlockSpec((None, None, rows, cols), lambda i, p: (layer, p[1], 0, 0))),
        out_shape=jax.ShapeDtypeStruct((n_layers, 2, rows, cols), F32),
        input_output_aliases={} if prev is None else {4: 0}, compiler_params=_cparams(1),
    )(*args)


def _sum_devices(parts, name):
    def body(p_ref, o_ref):
        tot = p_ref[0]
        for d in range(1, NDEV):
            tot = tot + p_ref[d]
        o_ref[...] = tot

    return pl.pallas_call(
        body, name=name, in_specs=[pl.BlockSpec(parts.shape, lambda: (0, 0, 0))],
        out_specs=pl.BlockSpec(parts.shape[1:], lambda: (0, 0)), out_shape=jax.ShapeDtypeStruct(parts.shape[1:], F32),
    )(parts)


def _pack(parts):
    rows = []
    for p in parts:
        p = p.reshape(-1, LANES)
        rows.append(jnp.pad(p, ((0, -p.shape[0] % SUBLANES), (0, 0))))
    return jnp.concatenate(rows, axis=0)


def _unpack(buf, shapes, lead=()):
    out, r0 = [], 0
    nl = len(lead)
    for shp in shapes:
        nrow = 1
        for d in shp:
            nrow *= d
        nrow //= LANES
        out.append(buf[(slice(None),) * nl + (slice(r0, r0 + nrow),)].reshape(lead + tuple(shp)))
        r0 += nrow + (-nrow % SUBLANES)
    return out


WEIGHT_ORDER = ("norm_mix_even", "w_in", "conv_a", "ln_a_g", "ln_a_b", "conv_b", "w_out", "norm_mix_odd", "w_pool",
                "pool_scale", "norm_ffn", "w_up", "conv_ffn_w", "w_down", "norm_final")
BIG = ("w_in", "w_out", "w_pool", "w_up", "w_down")
HALF = {"w_in": (D // 2, NZ // NCHIP), "w_out": (2 * A // NCHIP // 2, D), "w_pool": (PG // 2, PG),
        "w_up": (D // 2, FF2 // NCHIP), "w_down": (FF // NCHIP // 2, D)}
SMALL_SHARDED = ("conv_a", "conv_b", "conv_ffn_w", "norm_mix_odd", "pool_scale")
SMALL_ALL = ("norm_mix_even", "conv_a", "ln_a_g", "ln_a_b", "conv_b", "norm_mix_odd", "pool_scale", "norm_ffn", "conv_ffn_w",
             "norm_final")
SMALL_EARLY = ("norm_mix_odd", "pool_scale", "norm_ffn", "conv_ffn_w", "norm_final", "loss")
SMALL_LATE = ("norm_mix_even", "conv_a", "ln_a_g", "ln_a_b", "conv_b")


def kernel(x, norm_mix_even, w_in, conv_a, ln_a_g, ln_a_b, conv_b, w_out, norm_mix_odd, w_pool, pool_scale, norm_ffn, w_up, conv_ffn_w, w_down, norm_final, loss_target, m_norm_mix_even, m_w_in, m_conv_a, m_ln_a_g, m_ln_a_b, m_conv_b, m_w_out, m_norm_mix_odd, m_w_pool, m_pool_scale, m_norm_ffn, m_w_up, m_conv_ffn_w, m_w_down, m_norm_final, v_norm_mix_even, v_w_in, v_conv_a, v_ln_a_g, v_ln_a_b, v_conv_b, v_w_out, v_norm_mix_odd, v_w_pool, v_pool_scale, v_norm_ffn, v_w_up, v_conv_ffn_w, v_w_down, v_norm_final):
    w = dict(norm_mix_even=norm_mix_even, w_in=w_in, conv_a=conv_a, ln_a_g=ln_a_g, ln_a_b=ln_a_b, conv_b=conv_b, w_out=w_out,
             norm_mix_odd=norm_mix_odd, w_pool=w_pool, pool_scale=pool_scale, norm_ffn=norm_ffn, w_up=w_up,
             conv_ffn_w=conv_ffn_w, w_down=w_down, norm_final=norm_final)
    m = dict(norm_mix_even=m_norm_mix_even, w_in=m_w_in, conv_a=m_conv_a, ln_a_g=m_ln_a_g, ln_a_b=m_ln_a_b, conv_b=m_conv_b,
             w_out=m_w_out, norm_mix_odd=m_norm_mix_odd, w_pool=m_w_pool, pool_scale=m_pool_scale, norm_ffn=m_norm_ffn,
             w_up=m_w_up, conv_ffn_w=m_conv_ffn_w, w_down=m_w_down, norm_final=m_norm_final)
    v = dict(norm_mix_even=v_norm_mix_even, w_in=v_w_in, conv_a=v_conv_a, ln_a_g=v_ln_a_g, ln_a_b=v_ln_a_b, conv_b=v_conv_b,
             w_out=v_w_out, norm_mix_odd=v_norm_mix_odd, w_pool=v_w_pool, pool_scale=v_pool_scale, norm_ffn=v_norm_ffn,
             w_up=v_w_up, conv_ffn_w=v_conv_ffn_w, w_down=v_w_down, norm_final=v_norm_final)
    chip = 2 * lax.axis_index("x") + lax.axis_index("y")
    place = jnp.stack([chip, lax.axis_index("c")]).astype(jnp.int32)

    half = lambda a, name: a.astype(BF16).reshape((2,) + HALF[name])
    shard = {"w_in": half(w_in[0], "w_in"), "w_out": half(w_out[0], "w_out"), "w_pool": half(w_pool[0], "w_pool"),
             "w_up0": half(w_up[0], "w_up"), "w_up1": half(w_up[1], "w_up"),
             "w_down0": half(w_down[0], "w_down"), "w_down1": half(w_down[1], "w_down")}
    small_shapes = [w[k].shape[-2:] if w[k].ndim == 3 and k != "conv_ffn_w" else (w[k].size // w[k].shape[-1], w[k].shape[-1])
                    for k in SMALL_SHARDED]
    rep = dict(norm_mix_even=norm_mix_even, ln_a_g=ln_a_g, ln_a_b=ln_a_b, norm_ffn=norm_ffn, norm_final=norm_final.reshape(1, D))
    rows_per_step = {"w_in": 512, "w_out": 256, "w_pool": 256, "w_up": 256, "w_down": 352}
    as3 = lambda a: a.reshape(a.shape[0], -1, a.shape[-1])

    def adamw(k, total):
        outs, _ = _adamw_big(as3(w[k]), as3(total), as3(m[k]), as3(v[k]), rows_per_step[k], "adamw_" + k)
        return [a.reshape(w[k].shape) for a in outs]

    grad_x, done, summed = _train_step(
        x[0], loss_target[0], place, shard, rep, _pack([w[k] for k in SMALL_SHARDED]), small_shapes, adamw)
    loss = summed["loss"][0, 0]

    grad, delta, new_m, new_v = ({k: done[k][q] for k in BIG} for q in range(4))
    for k in SMALL_ALL:
        gsum = summed[k]
        if k in SMALL_SHARDED:
            cols = w[k].shape[-1]
            gsum = lax.dynamic_slice_in_dim(gsum, chip * cols, cols, axis=gsum.ndim - 1)
        grad[k] = gsum.reshape(w[k].shape)

    as2 = lambda a: a.reshape(-1, a.shape[-1])
    ds, ms, vs = _adamw_small(*[[as2(t[k]) for k in SMALL_ALL] for t in (w, grad, m, v)])
    for k, d2, m2, v2 in zip(SMALL_ALL, ds, ms, vs):
        delta[k], new_m[k], new_v[k] = (a.reshape(w[k].shape) for a in (d2, m2, v2))

    return (loss, grad_x[None], *[grad[k] for k in WEIGHT_ORDER], *[delta[k] for k in WEIGHT_ORDER],
            *[new_m[k] for k in WEIGHT_ORDER], *[new_v[k] for k in WEIGHT_ORDER])
```

```python
import functools

import jax
import jax.numpy as jnp
from jax import lax
from jax.experimental import pallas as pl
from jax.experimental.pallas import tpu as pltpu

F32, BF16 = jnp.float32, jnp.bfloat16

D = 1024
A = 512
NZ = 5 * A
FF = 2816
FF2 = 2 * FF
NCHIP = 4
NDEV = 8
K_A, K_S = 31, 3
POOL_WINDOWS = (2, 4, 8, 16)
PG = D // len(POOL_WINDOWS)
RMS_EPS, LN_EPS = 1e-6, 1e-5
ADAM_LR, ADAM_B1, ADAM_B2, ADAM_EPS, ADAM_WD, ADAM_STEP = 0.001, 0.9, 0.999, 1e-08, 0.01, 10

HALO_A, HALO_S, HALO_P = 32, 8, 16
SUBLANES = 8
LANES = 128
VMEM_LIMIT_BYTES = 56 * 1024 * 1024

TS_MIX = 512
TS_MIXB = 256
TS_FFN = 256
TS_POOL = 512
TS_MM = 512
R_CHUNK = 64
MID_STEPS_BEFORE_END = 1

MESH = pl.DeviceIdType.MESH
ANY = pl.BlockSpec(memory_space=pl.ANY)
NT_DIMS = (((1,), (1,)), ((), ()))
TN_DIMS = (((0,), (0,)), ((), ()))


def _cparams(n_axes):
    return pltpu.CompilerParams(dimension_semantics=("arbitrary",) * n_axes, vmem_limit_bytes=VMEM_LIMIT_BYTES)


def _const(shape):
    nd = len(shape)
    return pl.BlockSpec(shape, lambda *_: (0,) * nd, pipeline_mode=pl.Buffered(1))


def _sigmoid(v):
    return 1.0 / (1.0 + jnp.exp(-v))


def _rsqrt_mean_sq(x):
    return lax.rsqrt(jnp.mean(x * x, axis=-1, keepdims=True) + RMS_EPS)


def _rms_bwd(dh, xh, r, g):
    dxh = dh * g
    return r * (dxh - xh * jnp.mean(dxh * xh, axis=-1, keepdims=True))


def _shifted(buf_ref, row0, rows, col0, width, offsets):
    lo = (min(offsets) // SUBLANES) * SUBLANES
    hi = -(-(max(offsets) + rows) // SUBLANES) * SUBLANES
    start = row0 + lo if isinstance(row0, int) else pl.multiple_of(row0 + lo, SUBLANES)
    win = buf_ref[pl.ds(start, hi - lo), col0:col0 + width]
    out = {}
    for res in sorted({(o - lo) % SUBLANES for o in offsets}):
        qs = {o: (o - lo) // SUBLANES for o in offsets if (o - lo) % SUBLANES == res}
        base = pltpu.roll(win, hi - lo - res, 0) if res else win
        for o, q in qs.items():
            out[o] = base[SUBLANES * q:SUBLANES * q + rows, :]
    return out


def _rowsum8(v):
    acc = v[0:SUBLANES, :]
    for r in range(SUBLANES, v.shape[0], SUBLANES):
        acc = acc + v[r:r + SUBLANES, :]
    return acc


def _taps(w_ref, sh, offsets, col0, width):
    acc = None
    for k, o in enumerate(offsets):
        term = w_ref[k:k + 1, col0:col0 + width] * sh[o]
        acc = term if acc is None else acc + term
    return acc


def _window_bases(buf_ref, row0, rows, col0, width, offsets):
    lo = (min(offsets) // SUBLANES) * SUBLANES
    hi = -(-(max(offsets) + rows) // SUBLANES) * SUBLANES
    start = row0 + lo if isinstance(row0, int) else pl.multiple_of(row0 + lo, SUBLANES)
    win = buf_ref[pl.ds(start, hi - lo), col0:col0 + width]
    for res in sorted({(o - lo) % SUBLANES for o in offsets}):
        taps = [(k, (o - lo) // SUBLANES * SUBLANES) for k, o in enumerate(offsets) if (o - lo) % SUBLANES == res]
        yield (pltpu.roll(win, hi - lo - res, 0) if res else win), taps


def _conv_acc(buf_ref, w_ref, row0, rows, col0, width, offsets):
    acc = None
    for base, taps in _window_bases(buf_ref, row0, rows, col0, width, offsets):
        for k, q in taps:
            term = w_ref[k:k + 1, col0:col0 + width] * base[q:q + rows, :]
            acc = term if acc is None else acc + term
    return acc


def _conv_corr(buf_ref, w_ref, other, acc_ref, row0, rows, col0, width, offsets):
    acc = None
    for base, taps in _window_bases(buf_ref, row0, rows, col0, width, offsets):
        for k, q in taps:
            sl = base[q:q + rows, :]
            term = w_ref[k:k + 1, col0:col0 + width] * sl
            acc = term if acc is None else acc + term
            acc_ref[SUBLANES * k:SUBLANES * (k + 1), col0:col0 + width] += _rowsum8(sl * other)
    return acc


def _finish_tap_sums(acc_ref, out_ref, n_taps):
    for k in range(n_taps):
        out_ref[k:k + 1, :] = jnp.sum(acc_ref[SUBLANES * k:SUBLANES * (k + 1), :], axis=0, keepdims=True)


class _Comm:
    def __init__(self, inputs, out_shapes, sems, start, finish, aliases=None, mid=None):
        self.inputs, self.out_shapes, self.sems = list(inputs), list(out_shapes), list(sems)
        self.start, self.finish, self.mid, self.aliases = start, finish, mid, dict(aliases or {})


def _join_comm(a, b):
    ni, no, ns = len(a.inputs), len(a.out_shapes), len(a.sems)

    def both(name):
        def run(ins, outs, sems):
            if getattr(a, name) is not None:
                getattr(a, name)(ins[:ni], outs[:no], sems[:ns])
            if getattr(b, name) is not None:
                getattr(b, name)(ins[ni:], outs[no:], sems[ns:])
        return run

    aliases = {**a.aliases, **{ni + i: no + o for i, o in b.aliases.items()}}
    mid = both("mid") if (a.mid is not None or b.mid is not None) else None
    return _Comm(a.inputs + b.inputs, a.out_shapes + b.out_shapes, a.sems + b.sems, both("start"), both("finish"), aliases, mid)


def _call(body, *, name, grid, in_specs, out_specs, out_shape, args, scratch=(), comm=None, aliases=None):
    n_in, n_out, n_scr, n_axes = len(in_specs), len(out_specs), len(scratch), len(grid)
    params = pltpu.CompilerParams(dimension_semantics=("arbitrary",) * n_axes, vmem_limit_bytes=VMEM_LIMIT_BYTES)
    aliases = dict(aliases or {})
    if comm is None:
        outs = pl.pallas_call(body, name=name, grid=grid, in_specs=list(in_specs), out_specs=list(out_specs),
                              out_shape=list(out_shape), scratch_shapes=list(scratch), input_output_aliases=aliases,
                              compiler_params=params)(*args)
        return list(outs), []
    ci, co = len(comm.inputs), len(comm.out_shapes)

    def wrapped(*refs):
        k_in, c_in = refs[:n_in], refs[n_in:n_in + ci]
        o0 = n_in + ci
        k_out, c_out = refs[o0:o0 + n_out], refs[o0 + n_out:o0 + n_out + co]
        s0 = o0 + n_out + co
        k_scr, c_sem = refs[s0:s0 + n_scr], refs[s0 + n_scr:]
        first = pl.program_id(0) == 0
        last = pl.program_id(0) == grid[0] - 1
        for ax in range(1, n_axes):
            first = jnp.logical_and(first, pl.program_id(ax) == 0)
            last = jnp.logical_and(last, pl.program_id(ax) == grid[ax] - 1)

        @pl.when(first)
        def _():
            comm.start(c_in, c_out, c_sem)

        if comm.mid is not None and n_axes == 1 and grid[0] > MID_STEPS_BEFORE_END:
            @pl.when(pl.program_id(0) == grid[0] - 1 - MID_STEPS_BEFORE_END)
            def _():
                comm.mid(c_in, c_out, c_sem)

        body(*k_in, *k_out, *k_scr)

        @pl.when(last)
        def _():
            if comm.mid is not None and not (n_axes == 1 and grid[0] > MID_STEPS_BEFORE_END):
                comm.mid(c_in, c_out, c_sem)
            comm.finish(c_in, c_out, c_sem)

    outs = pl.pallas_call(
        wrapped, name=name, grid=grid, in_specs=list(in_specs) + [ANY] * ci, out_specs=list(out_specs) + [ANY] * co,
        out_shape=list(out_shape) + comm.out_shapes, scratch_shapes=list(scratch) + comm.sems,
        input_output_aliases={**aliases, **{n_in + i: n_out + o for i, o in comm.aliases.items()}}, compiler_params=params,
    )(*args, *comm.inputs)
    return list(outs[:n_out]), list(outs[n_out:])


def _run_comm(comm, name):
    ci, co = len(comm.inputs), len(comm.out_shapes)

    def body(*refs):
        c_in, c_out, c_sem = refs[:ci], refs[ci:ci + co], refs[ci + co:]
        comm.start(c_in, c_out, c_sem)
        if comm.mid is not None:
            comm.mid(c_in, c_out, c_sem)
        comm.finish(c_in, c_out, c_sem)

    return list(pl.pallas_call(body, name=name, in_specs=[ANY] * ci, out_specs=[ANY] * co, out_shape=comm.out_shapes,
                               scratch_shapes=comm.sems, input_output_aliases=comm.aliases)(*comm.inputs))


def _mix0_fwd(x, g, w_in, conv_a, ln_g, ln_b, conv_b, w_out, comm=None):
    s = x.shape[0]
    ts = min(TS_MIX, s)
    n = s // ts
    bw = NZ // NCHIP
    offs_a = [HALO_A - (K_A - 1) + k for k in range(K_A)]
    offs_s = [HALO_S - (K_S - 1) + k for k in range(K_S)]

    def body(x_ref, g_ref, win_ref, ca_ref, lg_ref, lb_ref, cb_ref, wout_ref,
             h_ref, z_ref, ac_ref, bconv_buf, cat_ref, x1_ref, glu_buf, cv_buf):
        i = pl.program_id(0)

        @pl.when(i == 0)
        def _():
            glu_buf[0:HALO_A, :] = jnp.zeros((HALO_A, A), F32)
            cv_buf[0:HALO_S, :] = jnp.zeros((HALO_S, A), F32)

        xv = x_ref[...]
        h = (xv * _rsqrt_mean_sq(xv) * g_ref[...]).astype(BF16)
        h_ref[...] = h
        for j in range(NCHIP):
            z_ref[:, j * bw:(j + 1) * bw] = jnp.dot(h, win_ref[j], preferred_element_type=F32)
        glu_buf[HALO_A:HALO_A + ts, :] = z_ref[:, 0:A] * _sigmoid(z_ref[:, A:2 * A])
        cv_buf[HALO_S:HALO_S + ts, :] = z_ref[:, 3 * A:4 * A] * z_ref[:, 4 * A:5 * A]

        def chunk(ci, carry):
            r0 = pl.multiple_of(ci * R_CHUNK, R_CHUNK)
            for c0 in range(0, A, LANES):
                ac_ref[pl.ds(r0, R_CHUNK), c0:c0 + LANES] = _conv_acc(glu_buf, ca_ref, r0, R_CHUNK, c0, LANES, offs_a)
                bconv_buf[pl.ds(r0, R_CHUNK), c0:c0 + LANES] = _conv_acc(cv_buf, cb_ref, r0, R_CHUNK, c0, LANES, offs_s)
            return carry

        lax.fori_loop(0, ts // R_CHUNK, chunk, 0)
        glu_buf[0:HALO_A, :] = glu_buf[ts:ts + HALO_A, :]
        cv_buf[0:HALO_S, :] = cv_buf[ts:ts + HALO_S, :]

        ac = ac_ref[...]
        xc = ac - jnp.mean(ac, axis=-1, keepdims=True)
        xn = xc * lax.rsqrt(jnp.mean(xc * xc, axis=-1, keepdims=True) + LN_EPS)
        ln = xn * lg_ref[...] + lb_ref[...]
        cat_ref[:, 0:A] = (ln * _sigmoid(ln)).astype(BF16)
        cat_ref[:, A:2 * A] = (z_ref[:, 2 * A:3 * A] * bconv_buf[...]).astype(BF16)
        x1_ref[...] = xv + jnp.dot(cat_ref[...], wout_ref[...], preferred_element_type=F32)

    tile = lambda w: pl.BlockSpec((ts, w), lambda i: (i, 0))
    return _call(
        body, name="mix0_fwd", grid=(n,), comm=comm, args=(x, g, w_in, conv_a, ln_g, ln_b, conv_b, w_out),
        in_specs=[tile(D), _const((1, D)), _const((NCHIP, D, bw)), _const((K_A, A)), _const((1, A)), _const((1, A)),
                  _const((K_S, A)), _const((2 * A, D))],
        out_specs=[tile(D), tile(NZ), tile(A), tile(A), tile(2 * A), tile(D)],
        out_shape=[jax.ShapeDtypeStruct((s, D), BF16), jax.ShapeDtypeStruct((s, NZ), F32), jax.ShapeDtypeStruct((s, A), F32),
                   jax.ShapeDtypeStruct((s, A), F32), jax.ShapeDtypeStruct((s, 2 * A), BF16), jax.ShapeDtypeStruct((s, D), F32)],
        scratch=[pltpu.VMEM((HALO_A + ts, A), F32), pltpu.VMEM((HALO_S + ts, A), F32)])


def _ffn_fwd(x, g, w_up, wc, w_down, name, comm=None):
    s = x.shape[0]
    ts = min(TS_FFN, s)
    n = s // ts
    bw = FF2 // NCHIP
    rows = 32
    offs = [HALO_S - (K_S - 1) + k for k in range(K_S)]

    def body(x_ref, g_ref, wup_ref, wc_ref, wdn_ref, h_ref, u0_ref, u_ref, act_ref, xo_ref, cbuf):
        i = pl.program_id(0)

        @pl.when(i == 0)
        def _():
            cbuf[0:HALO_S, :] = jnp.zeros((HALO_S, FF2), F32)

        xv = x_ref[...]
        h = (xv * _rsqrt_mean_sq(xv) * g_ref[...]).astype(BF16)
        h_ref[...] = h
        f = None
        for p in range(NCHIP // 2):
            for j in (p, NCHIP // 2 + p):
                zc = jnp.dot(h, wup_ref[j], preferred_element_type=F32)
                u0_ref[:, j * bw:(j + 1) * bw] = zc.astype(BF16)
                cbuf[HALO_S:HALO_S + ts, j * bw:(j + 1) * bw] = zc
            for r0 in range(0, ts, rows):
                for c0 in range(p * bw, (p + 1) * bw, LANES):
                    ug = _taps(wc_ref, _shifted(cbuf, r0, rows, c0, LANES, offs), offs, c0, LANES)
                    uv = _taps(wc_ref, _shifted(cbuf, r0, rows, FF + c0, LANES, offs), offs, FF + c0, LANES)
                    u_ref[r0:r0 + rows, c0:c0 + LANES] = ug
                    u_ref[r0:r0 + rows, FF + c0:FF + c0 + LANES] = uv
                    act_ref[r0:r0 + rows, c0:c0 + LANES] = (ug * _sigmoid(ug) * uv).astype(BF16)
            fp = jnp.dot(act_ref[:, p * bw:(p + 1) * bw], wdn_ref[p * bw:(p + 1) * bw, :], preferred_element_type=F32)
            f = fp if f is None else f + fp
        cbuf[0:HALO_S, :] = cbuf[ts:ts + HALO_S, :]
        xo_ref[...] = xv + f

    tile = lambda w: pl.BlockSpec((ts, w), lambda i: (i, 0))
    return _call(
        body, name=name, grid=(n,), comm=comm, args=(x, g, w_up, wc, w_down),
        in_specs=[tile(D), _const((1, D)), _const((NCHIP, D, bw)), _const((K_S, FF2)), _const((FF, D))],
        out_specs=[tile(D), tile(FF2), tile(FF2), tile(FF), tile(D)],
        out_shape=[jax.ShapeDtypeStruct((s, D), BF16), jax.ShapeDtypeStruct((s, FF2), BF16), jax.ShapeDtypeStruct((s, FF2), F32),
                   jax.ShapeDtypeStruct((s, FF), BF16), jax.ShapeDtypeStruct((s, D), F32)],
        scratch=[pltpu.VMEM((HALO_S + ts, FF2), F32)])


def _pool_windows(hbuf, pbuf, tile_row0, ts):
    def chunk(ci, carry):
        r0 = pl.multiple_of(ci * R_CHUNK, R_CHUNK)
        t1 = (tile_row0 + r0 + lax.broadcasted_iota(jnp.int32, (R_CHUNK, 1), 0) + 1).astype(F32)
        for gi, w in enumerate(POOL_WINDOWS):
            cnt = jnp.minimum(t1, float(w))
            offs = [HALO_P - jj for jj in range(w)]
            for c0 in range(gi * PG, (gi + 1) * PG, LANES):
                sh = _shifted(hbuf, r0, R_CHUNK, c0, LANES, offs)
                tot = sh[offs[0]]
                for o in offs[1:]:
                    tot = tot + sh[o]
                pbuf[pl.ds(r0, R_CHUNK), c0:c0 + LANES] = (tot / cnt - sh[HALO_P]).astype(BF16)
        return carry

    lax.fori_loop(0, ts // R_CHUNK, chunk, 0)


def _assemble_wpool(wp_ref, wps):
    rb = PG // NCHIP
    for gi in range(len(POOL_WINDOWS)):
        for j in range(NCHIP):
            wps[gi, j * rb:(j + 1) * rb, :] = wp_ref[j, gi]


def _pool_fwd(x, g, w_pool, scale):
    s = x.shape[0]
    ts = min(TS_POOL, s)
    n = s // ts
    ng = len(POOL_WINDOWS)

    def body(x_ref, g_ref, wp_ref, sc_ref, xo_ref, hbuf, pbuf, wps):
        i = pl.program_id(0)

        @pl.when(i == 0)
        def _():
            hbuf[0:HALO_P, :] = jnp.zeros((HALO_P, D), F32)
            _assemble_wpool(wp_ref, wps)

        xv = x_ref[...]
        hbuf[HALO_P:HALO_P + ts, :] = xv * _rsqrt_mean_sq(xv) * g_ref[...]
        _pool_windows(hbuf, pbuf, i * ts, ts)
        hbuf[0:HALO_P, :] = hbuf[ts:ts + HALO_P, :]
        for gi in range(ng):
            cols = slice(gi * PG, (gi + 1) * PG)
            y = jnp.dot(pbuf[:, cols], wps[gi], preferred_element_type=F32)
            xo_ref[:, cols] = xv[:, cols] + y * sc_ref[:, cols]

    tile = pl.BlockSpec((ts, D), lambda i: (i, 0))
    return pl.pallas_call(
        body, name="pool_fwd", grid=(n,),
        in_specs=[tile, _const((1, D)), _const((NCHIP, ng, PG // NCHIP, PG)), _const((1, D))],
        out_specs=tile, out_shape=jax.ShapeDtypeStruct((s, D), F32),
        scratch_shapes=[pltpu.VMEM((HALO_P + ts, D), F32), pltpu.VMEM((ts, D), BF16), pltpu.VMEM((ng, PG, PG), BF16)],
        compiler_params=_cparams(1),
    )(x, g, w_pool, scale)


def _final_loss(x, g, target):
    s = x.shape[0]
    ts = min(TS_MM, s)
    n = s // ts

    def body(x_ref, g_ref, t_ref, dx_ref, dg_ref, loss_ref):
        i = pl.program_id(0)

        @pl.when(i == 0)
        def _():
            dg_ref[...] = jnp.zeros((1, D), F32)
            loss_ref[...] = jnp.zeros((1, LANES), F32)

        xv = x_ref[...]
        r = _rsqrt_mean_sq(xv)
        xh = xv * r
        gv = g_ref[...]
        err = xh * gv - t_ref[...]
        sq = jnp.sum(jnp.sum(err * err, axis=1, keepdims=True), axis=0, keepdims=True)
        loss_ref[...] += sq * (0.5 / D)
        dy = err * (1.0 / D)
        dg_ref[...] += jnp.sum(dy * xh, axis=0, keepdims=True)
        dx_ref[...] = _rms_bwd(dy, xh, r, gv)

    tile = pl.BlockSpec((ts, D), lambda i: (i, 0))
    return pl.pallas_call(
        body, name="final_loss", grid=(n,),
        in_specs=[tile, _const((1, D)), tile],
        out_specs=[tile, pl.BlockSpec((1, D), lambda i: (0, 0)), pl.BlockSpec((1, LANES), lambda i: (0, 0))],
        out_shape=[jax.ShapeDtypeStruct((s, D), F32), jax.ShapeDtypeStruct((1, D), F32), jax.ShapeDtypeStruct((1, LANES), F32)],
        compiler_params=_cparams(1),
    )(x, g, target)


def _ffn_bwd_a(dxo, u, u0, wc, w_down, name, comm=None):
    s = dxo.shape[0]
    ts = min(TS_FFN, s)
    n = s // ts
    cw = FF2 // NCHIP
    rows = 32
    lw2 = 2 * LANES
    boffs = [K_S - 1 - k for k in range(K_S)]

    def body(dxo_ref, u_ref, u0_ref, wc_ref, wdn_ref, du0_ref, dwc_ref, dubuf, dact, dwacc):
        i = pl.program_id(0)

        @pl.when(i == 0)
        def _():
            dubuf[ts:ts + HALO_S, :] = jnp.zeros((HALO_S, FF2), F32)
            dwacc[...] = jnp.zeros(dwacc.shape, F32)

        df = dxo_ref[...].astype(BF16)
        for cg in range(0, FF, cw):
            dact[...] = lax.dot_general(df, wdn_ref[cg:cg + cw, :], NT_DIMS, preferred_element_type=F32)

            def chunk(ci, carry, cg=cg):
                rs = pl.ds(pl.multiple_of(ci * rows, rows), rows)
                for c in range(cg, cg + cw, LANES):
                    ug, uv = u_ref[rs, c:c + LANES], u_ref[rs, FF + c:FF + c + LANES]
                    sg = _sigmoid(ug)
                    da = dact[rs, c - cg:c - cg + LANES]
                    gs = ug * sg
                    dubuf[rs, c:c + LANES] = da * uv * (sg + gs * (1.0 - sg))
                    dubuf[rs, FF + c:FF + c + LANES] = da * gs
                return carry

            lax.fori_loop(0, ts // rows, chunk, 0)

        def chunk2(ci, carry):
            r0 = pl.multiple_of(ci * rows, rows)
            rs = pl.ds(r0, rows)
            for c in range(0, FF2, lw2):
                sh = _shifted(dubuf, r0, rows, c, lw2, boffs)
                du0_ref[rs, c:c + lw2] = _taps(wc_ref, sh, boffs, c, lw2).astype(BF16)
                u0v = u0_ref[rs, c:c + lw2].astype(F32)
                for k, o in enumerate(boffs):
                    dwacc[SUBLANES * k:SUBLANES * (k + 1), c:c + lw2] += _rowsum8(sh[o] * u0v)
            return carry

        lax.fori_loop(0, ts // rows, chunk2, 0)
        dubuf[ts:ts + HALO_S, :] = dubuf[0:HALO_S, :]

        @pl.when(i == n - 1)
        def _():
            _finish_tap_sums(dwacc, dwc_ref, K_S)

    rev = lambda w: pl.BlockSpec((ts, w), lambda i: (n - 1 - i, 0))
    return _call(
        body, name=name, grid=(n,), comm=comm, args=(dxo, u, u0, wc, w_down),
        in_specs=[rev(D), rev(FF2), rev(FF2), _const((K_S, FF2)), _const((FF, D))],
        out_specs=[rev(FF2), pl.BlockSpec((K_S, FF2), lambda i: (0, 0))],
        out_shape=[jax.ShapeDtypeStruct((s, FF2), BF16), jax.ShapeDtypeStruct((K_S, FF2), F32)],
        scratch=[pltpu.VMEM((ts + HALO_S, FF2), F32), pltpu.VMEM((ts, cw), F32), pltpu.VMEM((SUBLANES * K_S, FF2), F32)])


def _nt_rms_bwd(dy, w, x, g, dres, name, comm=None, tiles=None, dx_so_far=None):
    s = x.shape[0]
    ts = min(TS_MM, s)
    first, n = (0, s // ts) if tiles is None else tiles
    nw = dy.shape[1]
    bw = nw // NCHIP

    def body(dy_ref, w_ref, x_ref, g_ref, dres_ref, dx_ref, dg_ref):
        i = pl.program_id(0)

        @pl.when(i == 0)
        def _():
            dg_ref[...] = jnp.zeros((1, D), F32)

        dh = lax.dot_general(dy_ref[:, 0:bw], w_ref[0], NT_DIMS, preferred_element_type=F32)
        for j in range(1, NCHIP):
            dh = dh + lax.dot_general(dy_ref[:, j * bw:(j + 1) * bw], w_ref[j], NT_DIMS, preferred_element_type=F32)
        xv = x_ref[...]
        r = _rsqrt_mean_sq(xv)
        xh = xv * r
        dg_ref[...] += jnp.sum(dh * xh, axis=0, keepdims=True)
        dx_ref[...] = dres_ref[...] + _rms_bwd(dh, xh, r, g_ref[...])

    def body_with_alias(dy_ref, w_ref, x_ref, g_ref, dres_ref, _, dx_ref, dg_ref):
        body(dy_ref, w_ref, x_ref, g_ref, dres_ref, dx_ref, dg_ref)

    tile = lambda wd: pl.BlockSpec((ts, wd), lambda i: (first + i, 0))
    in_specs = [tile(nw), _const((NCHIP, D, bw)), tile(D), _const((1, D)), tile(D)]
    more = dx_so_far is not None
    return _call(
        body_with_alias if more else body, name=name, grid=(n,), comm=comm,
        args=(dy, w, x, g, dres) + ((dx_so_far,) if more else ()), in_specs=in_specs + [ANY] * more,
        out_specs=[tile(D), pl.BlockSpec((1, D), lambda i: (0, 0))], aliases={5: 0} if more else None,
        out_shape=[jax.ShapeDtypeStruct((s, D), F32), jax.ShapeDtypeStruct((1, D), F32)])


def _wgrad(a, b, n_blocks, name, comm=None):
    s, m = a.shape
    bw = b.shape[1] // n_blocks
    tk = min(TS_MM, s)

    def body(a_ref, b_ref, o_ref):
        @pl.when(pl.program_id(0) == 0)
        def _():
            o_ref[...] = jnp.zeros(o_ref.shape, F32)

        for j in range(n_blocks):
            o_ref[j] += lax.dot_general(a_ref[...], b_ref[:, j * bw:(j + 1) * bw].astype(BF16), TN_DIMS,
                                        preferred_element_type=F32)

    (out,), comm_out = _call(
        body, name=name, grid=(s // tk,), comm=comm, args=(a, b),
        in_specs=[pl.BlockSpec((tk, m), lambda k: (k, 0)), pl.BlockSpec((tk, b.shape[1]), lambda k: (k, 0))],
        out_specs=[pl.BlockSpec((n_blocks, m, bw), lambda k: (0, 0, 0))],
        out_shape=[jax.ShapeDtypeStruct((n_blocks, m, bw), F32)])
    return out, comm_out


def _pool_bwd(dxo, x, g, w_pool, scale, comm=None):
    s = x.shape[0]
    ts = min(TS_POOL, s)
    n = s // ts
    ng = len(POOL_WINDOWS)
    rb = PG // NCHIP

    def body(dxo_ref, x_ref, halo_ref, g_ref, wp_ref, sc_ref, dx_ref, dwp_ref, dsc_ref, dg_ref,
             hbuf, pbuf, qbuf, dhbuf, wps, dwacc):
        i = pl.program_id(0)
        j = n - 1 - i

        @pl.when(i == 0)
        def _():
            qbuf[ts:ts + HALO_P, :] = jnp.zeros((HALO_P, D), F32)
            dwacc[...] = jnp.zeros(dwacc.shape, F32)
            dsc_ref[...] = jnp.zeros((1, D), F32)
            dg_ref[...] = jnp.zeros((1, D), F32)
            _assemble_wpool(wp_ref, wps)

        gv = g_ref[...]
        xl = halo_ref[...]
        hbuf[0:HALO_P, :] = jnp.where(j == 0, 0.0, xl * _rsqrt_mean_sq(xl) * gv)
        xv = x_ref[...]
        r = _rsqrt_mean_sq(xv)
        xh = xv * r
        hbuf[HALO_P:HALO_P + ts, :] = xh * gv
        _pool_windows(hbuf, pbuf, j * ts, ts)

        dy = dxo_ref[...]
        t1 = (j * ts + lax.broadcasted_iota(jnp.int32, (ts, 1), 0) + 1).astype(F32)
        for gi, w in enumerate(POOL_WINDOWS):
            cols = slice(gi * PG, (gi + 1) * PG)
            p = pbuf[:, cols]
            y = jnp.dot(p, wps[gi], preferred_element_type=F32)
            dsc_ref[:, cols] += jnp.sum(dy[:, cols] * y, axis=0, keepdims=True)
            dq = (dy[:, cols] * sc_ref[:, cols]).astype(BF16)
            dwacc[gi] += lax.dot_general(p, dq, TN_DIMS, preferred_element_type=F32)
            dp = lax.dot_general(dq, wps[gi], NT_DIMS, preferred_element_type=F32)
            qbuf[0:ts, cols] = dp / jnp.minimum(t1, float(w))

        def chunk(ci, carry):
            r0 = pl.multiple_of(ci * R_CHUNK, R_CHUNK)
            tc = (j * ts + r0 + lax.broadcasted_iota(jnp.int32, (R_CHUNK, 1), 0) + 1).astype(F32)
            for gi, w in enumerate(POOL_WINDOWS):
                cnt = jnp.minimum(tc, float(w))
                offs = list(range(w))
                for c0 in range(gi * PG, (gi + 1) * PG, LANES):
                    sh = _shifted(qbuf, r0, R_CHUNK, c0, LANES, offs)
                    tot = sh[0]
                    for o in offs[1:]:
                        tot = tot + sh[o]
                    dhbuf[pl.ds(r0, R_CHUNK), c0:c0 + LANES] = tot - sh[0] * cnt
            return carry

        lax.fori_loop(0, ts // R_CHUNK, chunk, 0)
        qbuf[ts:ts + HALO_P, :] = qbuf[0:HALO_P, :]
        dh = dhbuf[...]
        dg_ref[...] += jnp.sum(dh * xh, axis=0, keepdims=True)
        dx_ref[...] = dy + _rms_bwd(dh, xh, r, gv)

        @pl.when(i == n - 1)
        def _():
            for gi in range(ng):
                for jj in range(NCHIP):
                    dwp_ref[jj, gi] = dwacc[gi, jj * rb:(jj + 1) * rb, :]

    rev = pl.BlockSpec((ts, D), lambda i: (n - 1 - i, 0))
    halo = pl.BlockSpec((HALO_P, D), lambda i: (jnp.maximum((n - 1 - i) * (ts // HALO_P) - 1, 0), 0))
    vec = pl.BlockSpec((1, D), lambda i: (0, 0))
    return _call(
        body, name="pool_bwd", grid=(n,), comm=comm, args=(dxo, x, x, g, w_pool, scale),
        in_specs=[rev, rev, halo, _const((1, D)), _const((NCHIP, ng, rb, PG)), _const((1, D))],
        out_specs=[rev, pl.BlockSpec((NCHIP, ng, rb, PG), lambda i: (0, 0, 0, 0)), vec, vec],
        out_shape=[jax.ShapeDtypeStruct((s, D), F32), jax.ShapeDtypeStruct((NCHIP, ng, rb, PG), F32),
                   jax.ShapeDtypeStruct((1, D), F32), jax.ShapeDtypeStruct((1, D), F32)],
        scratch=[pltpu.VMEM((HALO_P + ts, D), F32), pltpu.VMEM((ts, D), BF16), pltpu.VMEM((ts + HALO_P, D), F32),
                 pltpu.VMEM((ts, D), F32), pltpu.VMEM((ng, PG, PG), BF16), pltpu.VMEM((ng, PG, PG), F32)])


def _mix0_bwd_a(dx1, z, ac, bconv, w_out, conv_a, ln_g, ln_b, conv_b, comm=None):
    s = dx1.shape[0]
    ts = min(TS_MIXB, s)
    n = s // ts
    rows = 32
    boffs_a = [K_A - 1 - k for k in range(K_A)]
    boffs_s = [K_S - 1 - k for k in range(K_S)]

    def body(dx1_ref, z_ref, ac_ref, bconv_ref, wout_ref, ca_ref, lg_ref, lb_ref, cb_ref,
             dz_ref, dca_ref, dlg_ref, dlb_ref, dcb_ref, dac_buf, dbc_buf, dca_acc, dcb_acc):
        i = pl.program_id(0)

        @pl.when(i == 0)
        def _():
            dac_buf[ts:ts + HALO_A, :] = jnp.zeros((HALO_A, A), F32)
            dbc_buf[ts:ts + HALO_S, :] = jnp.zeros((HALO_S, A), F32)
            dca_acc[...] = jnp.zeros(dca_acc.shape, F32)
            dcb_acc[...] = jnp.zeros(dcb_acc.shape, F32)
            dlg_ref[...] = jnp.zeros((1, A), F32)
            dlb_ref[...] = jnp.zeros((1, A), F32)

        dcat = lax.dot_general(dx1_ref[...].astype(BF16), wout_ref[...], NT_DIMS, preferred_element_type=F32)
        db = dcat[:, A:2 * A]
        dz_ref[:, 2 * A:3 * A] = (db * bconv_ref[...]).astype(BF16)
        dbc_buf[0:ts, :] = db * z_ref[:, 2 * A:3 * A]
        ac = ac_ref[...]
        xc = ac - jnp.mean(ac, axis=-1, keepdims=True)
        rstd = lax.rsqrt(jnp.mean(xc * xc, axis=-1, keepdims=True) + LN_EPS)
        xn = xc * rstd
        lg = lg_ref[...]
        ln = xn * lg + lb_ref[...]
        sl = _sigmoid(ln)
        dln = dcat[:, 0:A] * sl * (1.0 + ln * (1.0 - sl))
        dlg_ref[...] += jnp.sum(dln * xn, axis=0, keepdims=True)
        dlb_ref[...] += jnp.sum(dln, axis=0, keepdims=True)
        dxn = dln * lg
        dac_buf[0:ts, :] = rstd * (dxn - jnp.mean(dxn, axis=-1, keepdims=True)
                                   - xn * jnp.mean(dxn * xn, axis=-1, keepdims=True))

        def chunk(ci, carry):
            r0 = pl.multiple_of(ci * rows, rows)
            rs = pl.ds(r0, rows)
            for c0 in range(0, A, LANES):
                col = lambda grp: slice(grp * A + c0, grp * A + c0 + LANES)
                a_val, sg = z_ref[rs, col(0)], _sigmoid(z_ref[rs, col(1)])
                dglu = _conv_corr(dac_buf, ca_ref, a_val * sg, dca_acc, r0, rows, c0, LANES, boffs_a)
                dz_ref[rs, col(0)] = (dglu * sg).astype(BF16)
                dz_ref[rs, col(1)] = (dglu * a_val * sg * (1.0 - sg)).astype(BF16)
                c_gate, bc_val = z_ref[rs, col(3)], z_ref[rs, col(4)]
                dcv = _conv_corr(dbc_buf, cb_ref, c_gate * bc_val, dcb_acc, r0, rows, c0, LANES, boffs_s)
                dz_ref[rs, col(3)] = (dcv * bc_val).astype(BF16)
                dz_ref[rs, col(4)] = (dcv * c_gate).astype(BF16)
            return carry

        lax.fori_loop(0, ts // rows, chunk, 0)
        dac_buf[ts:ts + HALO_A, :] = dac_buf[0:HALO_A, :]
        dbc_buf[ts:ts + HALO_S, :] = dbc_buf[0:HALO_S, :]

        @pl.when(i == n - 1)
        def _():
            _finish_tap_sums(dca_acc, dca_ref, K_A)
            _finish_tap_sums(dcb_acc, dcb_ref, K_S)

    rev = lambda w: pl.BlockSpec((ts, w), lambda i: (n - 1 - i, 0))
    full = lambda r, c: pl.BlockSpec((r, c), lambda i: (0, 0))
    return _call(
        body, name="mix0_bwd_a", grid=(n,), comm=comm, args=(dx1, z, ac, bconv, w_out, conv_a, ln_g, ln_b, conv_b),
        in_specs=[rev(D), rev(NZ), rev(A), rev(A), _const((2 * A, D)), _const((K_A, A)), _const((1, A)), _const((1, A)),
                  _const((K_S, A))],
        out_specs=[rev(NZ), full(K_A, A), full(1, A), full(1, A), full(K_S, A)],
        out_shape=[jax.ShapeDtypeStruct((s, NZ), BF16), jax.ShapeDtypeStruct((K_A, A), F32), jax.ShapeDtypeStruct((1, A), F32),
                   jax.ShapeDtypeStruct((1, A), F32), jax.ShapeDtypeStruct((K_S, A), F32)],
        scratch=[pltpu.VMEM((ts + HALO_A, A), F32), pltpu.VMEM((ts + HALO_S, A), F32), pltpu.VMEM((SUBLANES * K_A, A), F32),
                 pltpu.VMEM((SUBLANES * K_S, A), F32)])


def _train_step(x, target, place, shard, rep, small_pack, small_shapes, adamw):
    bw_up, bw_in = FF2 // NCHIP, NZ // NCHIP
    five = lambda a, k: a.reshape(NCHIP, 2, *HALF[k])

    g_in, g_out, small_g = _run_comm(_gather_comm([shard["w_in"], shard["w_out"]], small_pack), "ag_first")
    whole = {}
    for k, part in zip(SMALL_SHARDED, _unpack(small_g, small_shapes, lead=(NCHIP,))):
        whole[k] = jnp.moveaxis(part, 0, 1).reshape(part.shape[1], NCHIP * part.shape[2])
    w_in, w_out = g_in.reshape(NCHIP, D, bw_in), g_out.reshape(2 * A, D)
    conv_ffn = whole["conv_ffn_w"].reshape(2, K_S, FF2)
    nffn = [rep["norm_ffn"][0:1], rep["norm_ffn"][1:2]]

    (h0, z, ac, bconv, cat, x1), (g_up0, g_dn0) = _mix0_fwd(
        x, rep["norm_mix_even"], w_in, whole["conv_a"], rep["ln_a_g"], rep["ln_a_b"], whole["conv_b"], w_out,
        comm=_gather_comm([shard["w_up0"], shard["w_down0"]]))
    w_up0, w_dn0 = g_up0.reshape(NCHIP, D, bw_up), g_dn0.reshape(FF, D)
    (hf0, u00, u0, act0, x2), (g_pool, g_up1, g_dn1) = _ffn_fwd(
        x1, nffn[0], w_up0, conv_ffn[0], w_dn0, "ffn0_fwd",
        comm=_gather_comm([shard["w_pool"], shard["w_up1"], shard["w_down1"]], early_forward=True))
    w_pool = g_pool.reshape(NCHIP, len(POOL_WINDOWS), PG // NCHIP, PG)
    w_up1, w_dn1 = g_up1.reshape(NCHIP, D, bw_up), g_dn1.reshape(FF, D)
    x3 = _pool_fwd(x2, whole["norm_mix_odd"], w_pool, whole["pool_scale"])
    (hf1, u01, u1, act1, x4), _ = _ffn_fwd(x3, nffn[1], w_up1, conv_ffn[1], w_dn1, "ffn1_fwd")
    dx4, g_nfin, loss_part = _final_loss(x4, rep["norm_final"], target)

    psum = lambda k, g, ld, tag="": _pair_sum(place, g, ld, "pair_sum_" + k + tag)
    tot = lambda k, g, ld, p, layer=0, nl=1, prev=None: _chip_sum(place, g, ld, p, layer, nl, "chip_sum_%s%d" % (k, layer), prev)
    pair, chips, join = _pair_comm, _chips_comm, _join_comm

    gr_dn1 = five(_wgrad(act1, dx4, 1, "wgrad_down1")[0], "w_down")
    (du01, g_wc1), _ = _ffn_bwd_a(dx4, u1, u01, conv_ffn[1], w_dn1, "ffn1_bwd_a")
    g_up, (ld_dn1,) = _wgrad(hf1, du01, NCHIP, "wgrad_up1", comm=pair([gr_dn1]))
    gr_up1 = five(g_up, "w_up")
    s_dn1 = psum("w_down", gr_dn1, ld_dn1, "1")
    (dx3, g_nf1), (p_dn1, ld_up1) = _nt_rms_bwd(du01, w_up1, x3, nffn[1], dx4, "ffn1_bwd_b",
                                                comm=join(chips([s_dn1]), pair([gr_up1])))
    s_up1 = psum("w_up", gr_up1, ld_up1, "1")
    (dx2, g_wpool, g_scale, g_nmo), _ = _pool_bwd(dx3, x2, whole["norm_mix_odd"], w_pool, whole["pool_scale"])
    gr_pool = five(g_wpool, "w_pool")
    (du00, g_wc0), (p_up1,) = _ffn_bwd_a(dx2, u0, u00, conv_ffn[0], w_dn0, "ffn0_bwd_a", comm=chips([s_up1]))
    gr_dn0 = five(_wgrad(act0, dx2, 1, "wgrad_down0")[0], "w_down")
    g_up, (ld_dn0, ld_pool) = _wgrad(hf0, du00, NCHIP, "wgrad_up0", comm=pair([gr_dn0, gr_pool]))
    gr_up0 = five(g_up, "w_up")
    s_dn0, s_pool = psum("w_down", gr_dn0, ld_dn0, "0"), psum("w_pool", gr_pool, ld_pool)
    (dx1, g_nf0), (p_dn0, p_pool, ld_up0) = _nt_rms_bwd(du00, w_up0, x1, nffn[0], dx2, "ffn0_bwd_b",
                                                        comm=join(chips([s_dn0, s_pool]), pair([gr_up0])))
    s_up0 = psum("w_up", gr_up0, ld_up0, "0")
    gr_out = five(_wgrad(cat, dx1, 1, "wgrad_out")[0], "w_out")
    (dz, g_ca, g_lg, g_lb, g_cb), (p_up0, ld_out) = _mix0_bwd_a(
        dx1, z, ac, bconv, w_out, whole["conv_a"], rep["ln_a_g"], rep["ln_a_b"], whole["conv_b"],
        comm=join(chips([s_up0]), pair([gr_out])))
    s_out = psum("w_out", gr_out, ld_out)
    t_pool = tot("w_pool", gr_pool, ld_pool, p_pool)
    t_up = tot("w_up", gr_up0, ld_up0, p_up0, 0, 2, tot("w_up", gr_up1, ld_up1, p_up1, 1, 2))
    t_dn = tot("w_down", gr_dn0, ld_dn0, p_dn0, 0, 2, tot("w_down", gr_dn1, ld_dn1, p_dn1, 1, 2))
    small = {"norm_mix_odd": g_nmo, "pool_scale": g_scale, "norm_ffn": jnp.concatenate([g_nf0, g_nf1], axis=0),
             "conv_ffn_w": jnp.stack([g_wc0, g_wc1]), "norm_final": g_nfin, "loss": loss_part,
             "conv_a": g_ca, "ln_a_g": g_lg, "ln_a_b": g_lb, "conv_b": g_cb}
    g_in, (p_out, early_all, t_pool, t_up, t_dn) = _wgrad(
        h0, dz, NCHIP, "wgrad_in",
        comm=join(chips([s_out], _pack([small[k] for k in SMALL_EARLY])), _swap_comm([t_pool, t_up, t_dn])))
    gr_in = five(g_in, "w_in")
    n_mm = x.shape[0] // min(TS_MM, x.shape[0])
    (dx_a, g_nme_a), (ld_in,) = _nt_rms_bwd(dz, w_in, x, rep["norm_mix_even"], dx1, "mix0_bwd_b0", comm=pair([gr_in]),
                                            tiles=(0, n_mm // 2))
    s_in = psum("w_in", gr_in, ld_in)
    (grad_x, g_nme_b), (p_in,) = _nt_rms_bwd(dz, w_in, x, rep["norm_mix_even"], dx1, "mix0_bwd_b1", comm=chips([s_in]),
                                             tiles=(n_mm // 2, n_mm - n_mm // 2), dx_so_far=dx_a)
    small["norm_mix_even"] = g_nme_a + g_nme_b
    (late_all,) = _run_comm(chips([], _pack([small[k] for k in SMALL_LATE])), "rs_last")
    t_in, t_out = _run_comm(_swap_comm([tot("w_in", gr_in, ld_in, p_in), tot("w_out", gr_out, ld_out, p_out)]), "rs_swap")
    done = {k: adamw(k, t) for k, t in (("w_up", t_up), ("w_down", t_dn), ("w_pool", t_pool), ("w_in", t_in), ("w_out", t_out))}
    summed = dict(zip(SMALL_EARLY, _unpack(_sum_devices(early_all, "sum_small_early"), [small[k].shape for k in SMALL_EARLY])))
    summed.update(zip(SMALL_LATE, _unpack(_sum_devices(late_all, "sum_small_late"), [small[k].shape for k in SMALL_LATE])))
    return grad_x, done, summed


def _adamw_math(w, g, m, v):
    m = ADAM_B1 * m + (1.0 - ADAM_B1) * g
    v = ADAM_B2 * v + (1.0 - ADAM_B2) * (g * g)
    m_hat = m / (1.0 - ADAM_B1 ** ADAM_STEP)
    v_hat = v / (1.0 - ADAM_B2 ** ADAM_STEP)
    return -ADAM_LR * (m_hat / (jnp.sqrt(v_hat) + ADAM_EPS) + ADAM_WD * w), m, v


def _adamw_big(w, g, m, v, tr, name, comm=None):
    nl, rows, cols = w.shape

    def body(w_ref, g_ref, m_ref, v_ref, g2_ref, d_ref, m2_ref, v2_ref):
        gv = g_ref[...]
        g2_ref[...] = gv
        d_ref[...], m2_ref[...], v2_ref[...] = _adamw_math(w_ref[...], gv, m_ref[...], v_ref[...])

    spec = pl.BlockSpec((None, tr, cols), lambda l, r: (l, r, 0))
    return _call(body, name=name, grid=(nl, rows // tr), comm=comm, args=(w, g, m, v), in_specs=[spec] * 4,
                 out_specs=[spec] * 4, out_shape=[jax.ShapeDtypeStruct(w.shape, F32)] * 4)


def _adamw_small(ws, gs, ms, vs):
    n = len(ws)

    def body(*refs):
        for p in range(n):
            w_ref, g_ref, m_ref, v_ref = (refs[q * n + p] for q in range(4))
            d_ref, m2_ref, v2_ref = (refs[(4 + q) * n + p] for q in range(3))
            d_ref[...], m2_ref[...], v2_ref[...] = _adamw_math(w_ref[...], g_ref[...], m_ref[...], v_ref[...])

    whole = lambda a: pl.BlockSpec(a.shape, lambda: (0,) * a.ndim)
    outs = pl.pallas_call(
        body, name="adamw_small", in_specs=[whole(a) for a in ws] * 4, out_specs=[whole(a) for a in ws] * 3,
        out_shape=[jax.ShapeDtypeStruct(a.shape, F32) for a in ws] * 3,
        compiler_params=pltpu.CompilerParams(vmem_limit_bytes=VMEM_LIMIT_BYTES),
    )(*ws, *gs, *ms, *vs)
    return outs[0:n], outs[n:2 * n], outs[2 * n:3 * n]


def _place():
    x, y, c = lax.axis_index("x"), lax.axis_index("y"), lax.axis_index("c")
    chips = [(x, 1 - y), (1 - x, y), (1 - x, 1 - y)]
    blocks = [2 * cx + cy for cx, cy in chips]
    return x, y, c, 2 * x + y, chips, blocks


def _gather_comm(shards, small=None, early_forward=False):
    na = len(shards)
    nk = NCHIP - 1
    ns = 0 if small is None else 1

    def copies(ins, outs, sems):
        ici_send, ici_recv, fwd_send, fwd_recv, own_send, own_recv = sems[:6]
        x, y, c, j, chips, blocks = _place()
        sib = (x, y, 1 - c)

        def ici(a, k, arrival):
            dst = outs[a].at[blocks[k], c] if arrival else outs[a].at[j, c]
            return pltpu.make_async_remote_copy(
                src_ref=dst if arrival else ins[a].at[c], dst_ref=dst, send_sem=ici_send.at[a * nk + k],
                recv_sem=ici_recv.at[a * nk + k], device_id=(*chips[k], c), device_id_type=MESH)

        def fwd(a, k, half):
            ref = outs[a].at[blocks[k], half]
            return pltpu.make_async_remote_copy(
                src_ref=ref, dst_ref=ref, send_sem=fwd_send.at[a * nk + k], recv_sem=fwd_recv.at[a * nk + k],
                device_id=sib, device_id_type=MESH)

        own = [pltpu.make_async_remote_copy(src_ref=ins[a], dst_ref=outs[a].at[j], send_sem=own_send.at[a],
                                            recv_sem=own_recv.at[a], device_id=sib, device_id_type=MESH) for a in range(na)]
        small_copies = [pltpu.make_async_remote_copy(
            src_ref=ins[na], dst_ref=outs[na].at[j], send_sem=ici_send.at[na * nk + k], recv_sem=ici_recv.at[na * nk + k],
            device_id=(*chips[k], c), device_id_type=MESH) for k in range(nk * ns)]
        local = [pltpu.make_async_copy(ins[na], outs[na].at[j], sems[6])] if ns else []
        return ici, fwd, own, small_copies, local, c

    def start(ins, outs, sems):
        ici, _, own, small_copies, local, _ = copies(ins, outs, sems)
        for cp in local + own + [ici(a, k, False) for a in range(na) for k in range(nk)] + small_copies:
            cp.start()

    def mid(ins, outs, sems):
        ici, fwd, _, _, _, c = copies(ins, outs, sems)
        for a in range(na):
            for k in range(nk):
                ici(a, k, True).wait_recv()
                fwd(a, k, c).start()

    def finish(ins, outs, sems):
        ici, fwd, own, small_copies, local, c = copies(ins, outs, sems)
        for cp in small_copies:
            cp.wait()
        for a in range(na):
            for k in range(nk):
                ici(a, k, False).wait_send()
                fwd(a, k, c).wait_send()
                fwd(a, k, 1 - c).wait_recv()
        for cp in own + local:
            cp.wait()

    out_shapes = [jax.ShapeDtypeStruct((NCHIP,) + s.shape, s.dtype) for s in shards]
    sems = [pltpu.SemaphoreType.DMA((na * nk + nk * ns,)), pltpu.SemaphoreType.DMA((na * nk + nk * ns,)),
            pltpu.SemaphoreType.DMA((na * nk,)), pltpu.SemaphoreType.DMA((na * nk,)),
            pltpu.SemaphoreType.DMA((na,)), pltpu.SemaphoreType.DMA((na,))]
    if ns:
        out_shapes.append(jax.ShapeDtypeStruct((NCHIP,) + small.shape, small.dtype))
        sems.append(pltpu.SemaphoreType.DMA)
    if early_forward:
        return _Comm(list(shards) + [small] * ns, out_shapes, sems, start, finish, mid=mid)

    def forward_and_finish(ins, outs, sems):
        mid(ins, outs, sems)
        finish(ins, outs, sems)

    return _Comm(list(shards) + [small] * ns, out_shapes, sems, start, forward_and_finish)


def _simple_comm(inputs, out_shapes, make_copies, n_sems, aliases=None):
    def start(ins, outs, sems):
        for cp in make_copies(ins, outs, sems):
            cp.start()

    def finish(ins, outs, sems):
        for cp in make_copies(ins, outs, sems):
            cp.wait()

    return _Comm(inputs, out_shapes, [pltpu.SemaphoreType.DMA((n,)) for n in n_sems], start, finish, aliases)


def _pair_comm(grads):
    def make_copies(ins, outs, sems):
        x, y, c, _, _, _ = _place()
        return [pltpu.make_async_remote_copy(
            src_ref=ins[a].at[:, 1 - c], dst_ref=outs[a], send_sem=sems[0].at[a], recv_sem=sems[1].at[a],
            device_id=(x, y, 1 - c), device_id_type=MESH) for a in range(len(grads))]

    out_shapes = [jax.ShapeDtypeStruct(g.shape[:1] + g.shape[2:], F32) for g in grads]
    return _simple_comm(grads, out_shapes, make_copies, [len(grads)] * 2)


def _chips_comm(sums, small=None):
    na = len(sums)
    nk = NCHIP - 1

    def make_copies(ins, outs, sems):
        x, y, c, _, chips, blocks = _place()
        copies = [pltpu.make_async_remote_copy(
            src_ref=ins[a].at[blocks[k]], dst_ref=outs[a].at[k], send_sem=sems[0].at[a * nk + k],
            recv_sem=sems[1].at[a * nk + k], device_id=(*chips[k], c), device_id_type=MESH)
            for a in range(na) for k in range(nk)]
        if small is not None:
            me = 4 * x + 2 * y + c
            for r in range(1, NDEV):
                peer = (1 - x if r & 4 else x, 1 - y if r & 2 else y, 1 - c if r & 1 else c)
                copies.append(pltpu.make_async_remote_copy(
                    src_ref=ins[na], dst_ref=outs[na].at[me], send_sem=sems[2].at[r - 1], recv_sem=sems[3].at[r - 1],
                    device_id=peer, device_id_type=MESH))
            copies.append(pltpu.make_async_copy(ins[na], outs[na].at[me], sems[4].at[0]))
        return copies

    out_shapes = [jax.ShapeDtypeStruct((nk,) + g.shape[1:], BF16) for g in sums]
    chip_sems = [max(na * nk, 1)] * 2
    if small is None:
        return _simple_comm(sums, out_shapes, make_copies, chip_sems)
    out_shapes.append(jax.ShapeDtypeStruct((NDEV,) + small.shape, F32))
    return _simple_comm(list(sums) + [small], out_shapes, make_copies, chip_sems + [NDEV - 1] * 2 + [1])


def _swap_comm(totals):
    def make_copies(ins, outs, sems):
        x, y, c, _, _, _ = _place()
        return [pltpu.make_async_remote_copy(
            src_ref=outs[a].at[:, c], dst_ref=outs[a].at[:, c], send_sem=sems[0].at[a], recv_sem=sems[1].at[a],
            device_id=(x, y, 1 - c), device_id_type=MESH) for a in range(len(totals))]

    out_shapes = [jax.ShapeDtypeStruct(t.shape, F32) for t in totals]
    return _simple_comm(totals, out_shapes, make_copies, [len(totals)] * 2, aliases={a: a for a in range(len(totals))})


def _pair_sum(place, grad, landed, name):
    _, _, rows, cols = grad.shape

    def body(place_ref, g_ref, l_ref, o_ref):
        o_ref[...] = (g_ref[...] + l_ref[...]).astype(BF16)

    return pl.pallas_call(
        body, name=name,
        grid_spec=pltpu.PrefetchScalarGridSpec(
            num_scalar_prefetch=1, grid=(NCHIP,),
            in_specs=[pl.BlockSpec((None, None, rows, cols), lambda b, p: (b, p[1], 0, 0)),
                      pl.BlockSpec((None, rows, cols), lambda b, p: (b, 0, 0))],
            out_specs=pl.BlockSpec((None, rows, cols), lambda b, p: (b, 0, 0))),
        out_shape=jax.ShapeDtypeStruct((NCHIP, rows, cols), BF16), compiler_params=_cparams(1),
    )(place, grad, landed)


def _chip_sum(place, grad, landed, parts, layer, n_layers, name, prev=None):
    _, _, rows, cols = grad.shape

    def body(*refs):
        g_ref, l_ref, p_ref, o_ref = refs[1], refs[2], refs[3], refs[-1]
        tot = g_ref[...] + l_ref[...]
        for k in range(NCHIP - 1):
            tot = tot + p_ref[k].astype(F32)
        o_ref[...] = tot

    in_specs = [pl.BlockSpec((None, None, rows, cols), lambda i, p: (p[0], p[1], 0, 0)),
                pl.BlockSpec((None, rows, cols), lambda i, p: (p[0], 0, 0)),
                pl.BlockSpec((NCHIP - 1, rows, cols), lambda i, p: (0, 0, 0))]
    args = [place, grad, landed, parts]
    if prev is not None:
        in_specs.append(ANY)
        args.append(prev)
    return pl.pallas_call(
        body, name=name,
        grid_spec=pltpu.PrefetchScalarGridSpec(
            num_scalar_prefetch=1, grid=(1,), in_specs=in_specs,
            out_specs=pl.BlockSpec((None, None, rows, cols), lambda i, p: (layer, p[1], 0, 0))),
        out_shape=jax.ShapeDtypeStruct((n_layers, 2, rows, cols), F32),
        input_output_aliases={} if prev is None else {4: 0}, compiler_params=_cparams(1),
    )(*args)


def _sum_devices(parts, name):
    def body(p_ref, o_ref):
        tot = p_ref[0]
        for d in range(1, NDEV):
            tot = tot + p_ref[d]
        o_ref[...] = tot

    return pl.pallas_call(
        body, name=name, in_specs=[pl.BlockSpec(parts.shape, lambda: (0, 0, 0))],
        out_specs=pl.BlockSpec(parts.shape[1:], lambda: (0, 0)), out_shape=jax.ShapeDtypeStruct(parts.shape[1:], F32),
    )(parts)


def _pack(parts):
    rows = []
    for p in parts:
        p = p.reshape(-1, LANES)
        rows.append(jnp.pad(p, ((0, -p.shape[0] % SUBLANES), (0, 0))))
    return jnp.concatenate(rows, axis=0)


def _unpack(buf, shapes, lead=()):
    out, r0 = [], 0
    nl = len(lead)
    for shp in shapes:
        nrow = 1
        for d in shp:
            nrow *= d
        nrow //= LANES
        out.append(buf[(slice(None),) * nl + (slice(r0, r0 + nrow),)].reshape(lead + tuple(shp)))
        r0 += nrow + (-nrow % SUBLANES)
    return out


WEIGHT_ORDER = ("norm_mix_even", "w_in", "conv_a", "ln_a_g", "ln_a_b", "conv_b", "w_out", "norm_mix_odd", "w_pool",
                "pool_scale", "norm_ffn", "w_up", "conv_ffn_w", "w_down", "norm_final")
BIG = ("w_in", "w_out", "w_pool", "w_up", "w_down")
HALF = {"w_in": (D // 2, NZ // NCHIP), "w_out": (2 * A // NCHIP // 2, D), "w_pool": (PG // 2, PG),
        "w_up": (D // 2, FF2 // NCHIP), "w_down": (FF // NCHIP // 2, D)}
SMALL_SHARDED = ("conv_a", "conv_b", "conv_ffn_w", "norm_mix_odd", "pool_scale")
SMALL_ALL = ("norm_mix_even", "conv_a", "ln_a_g", "ln_a_b", "conv_b", "norm_mix_odd", "pool_scale", "norm_ffn", "conv_ffn_w",
             "norm_final")
SMALL_EARLY = ("norm_mix_odd", "pool_scale", "norm_ffn", "conv_ffn_w", "norm_final", "loss", "conv_a", "ln_a_g", "ln_a_b", "conv_b")
SMALL_LATE = ("norm_mix_even",)


def kernel(x, norm_mix_even, w_in, conv_a, ln_a_g, ln_a_b, conv_b, w_out, norm_mix_odd, w_pool, pool_scale, norm_ffn, w_up, conv_ffn_w, w_down, norm_final, loss_target, m_norm_mix_even, m_w_in, m_conv_a, m_ln_a_g, m_ln_a_b, m_conv_b, m_w_out, m_norm_mix_odd, m_w_pool, m_pool_scale, m_norm_ffn, m_w_up, m_conv_ffn_w, m_w_down, m_norm_final, v_norm_mix_even, v_w_in, v_conv_a, v_ln_a_g, v_ln_a_b, v_conv_b, v_w_out, v_norm_mix_odd, v_w_pool, v_pool_scale, v_norm_ffn, v_w_up, v_conv_ffn_w, v_w_down, v_norm_final):
    w = dict(norm_mix_even=norm_mix_even, w_in=w_in, conv_a=conv_a, ln_a_g=ln_a_g, ln_a_b=ln_a_b, conv_b=conv_b, w_out=w_out,
             norm_mix_odd=norm_mix_odd, w_pool=w_pool, pool_scale=pool_scale, norm_ffn=norm_ffn, w_up=w_up,
             conv_ffn_w=conv_ffn_w, w_down=w_down, norm_final=norm_final)
    m = dict(norm_mix_even=m_norm_mix_even, w_in=m_w_in, conv_a=m_conv_a, ln_a_g=m_ln_a_g, ln_a_b=m_ln_a_b, conv_b=m_conv_b,
             w_out=m_w_out, norm_mix_odd=m_norm_mix_odd, w_pool=m_w_pool, pool_scale=m_pool_scale, norm_ffn=m_norm_ffn,
             w_up=m_w_up, conv_ffn_w=m_conv_ffn_w, w_down=m_w_down, norm_final=m_norm_final)
    v = dict(norm_mix_even=v_norm_mix_even, w_in=v_w_in, conv_a=v_conv_a, ln_a_g=v_ln_a_g, ln_a_b=v_ln_a_b, conv_b=v_conv_b,
             w_out=v_w_out, norm_mix_odd=v_norm_mix_odd, w_pool=v_w_pool, pool_scale=v_pool_scale, norm_ffn=v_norm_ffn,
             w_up=v_w_up, conv_ffn_w=v_conv_ffn_w, w_down=v_w_down, norm_final=v_norm_final)
    chip = 2 * lax.axis_index("x") + lax.axis_index("y")
    place = jnp.stack([chip, lax.axis_index("c")]).astype(jnp.int32)

    half = lambda a, name: a.astype(BF16).reshape((2,) + HALF[name])
    shard = {"w_in": half(w_in[0], "w_in"), "w_out": half(w_out[0], "w_out"), "w_pool": half(w_pool[0], "w_pool"),
             "w_up0": half(w_up[0], "w_up"), "w_up1": half(w_up[1], "w_up"),
             "w_down0": half(w_down[0], "w_down"), "w_down1": half(w_down[1], "w_down")}
    small_shapes = [w[k].shape[-2:] if w[k].ndim == 3 and k != "conv_ffn_w" else (w[k].size // w[k].shape[-1], w[k].shape[-1])
                    for k in SMALL_SHARDED]
    rep = dict(norm_mix_even=norm_mix_even, ln_a_g=ln_a_g, ln_a_b=ln_a_b, norm_ffn=norm_ffn, norm_final=norm_final.reshape(1, D))
    rows_per_step = {"w_in": 512, "w_out": 256, "w_pool": 256, "w_up": 256, "w_down": 352}
    as3 = lambda a: a.reshape(a.shape[0], -1, a.shape[-1])

    def adamw(k, total):
        outs, _ = _adamw_big(as3(w[k]), as3(total), as3(m[k]), as3(v[k]), rows_per_step[k], "adamw_" + k)
        return [a.reshape(w[k].shape) for a in outs]

    grad_x, done, summed = _train_step(
        x[0], loss_target[0], place, shard, rep, _pack([w[k] for k in SMALL_SHARDED]), small_shapes, adamw)
    loss = summed["loss"][0, 0]

    grad, delta, new_m, new_v = ({k: done[k][q] for k in BIG} for q in range(4))
    for k in SMALL_ALL:
        gsum = summed[k]
        if k in SMALL_SHARDED:
            cols = w[k].shape[-1]
            gsum = lax.dynamic_slice_in_dim(gsum, chip * cols, cols, axis=gsum.ndim - 1)
        grad[k] = gsum.reshape(w[k].shape)

    as2 = lambda a: a.reshape(1, -1) if a.ndim == 1 else a
    ds, ms, vs = _adamw_small(*[[as2(t[k]) for k in SMALL_ALL] for t in (w, grad, m, v)])
    for k, d2, m2, v2 in zip(SMALL_ALL, ds, ms, vs):
        delta[k], new_m[k], new_v[k] = (a.reshape(w[k].shape) for a in (d2, m2, v2))

    return (loss, grad_x[None], *[grad[k] for k in WEIGHT_ORDER], *[delta[k] for k in WEIGHT_ORDER],
            *[new_m[k] for k in WEIGHT_ORDER], *[new_v[k] for k in WEIGHT_ORDER])
```

```python
import functools

import jax
import jax.numpy as jnp
from jax import lax
from jax.experimental import pallas as pl
from jax.experimental.pallas import tpu as pltpu

F32, BF16 = jnp.float32, jnp.bfloat16

D = 1024
A = 512
NZ = 5 * A
FF = 2816
FF2 = 2 * FF
NCHIP = 4
NDEV = 8
K_A, K_S = 31, 3
POOL_WINDOWS = (2, 4, 8, 16)
PG = D // len(POOL_WINDOWS)
RMS_EPS, LN_EPS = 1e-6, 1e-5
ADAM_LR, ADAM_B1, ADAM_B2, ADAM_EPS, ADAM_WD, ADAM_STEP = 0.001, 0.9, 0.999, 1e-08, 0.01, 10

HALO_A, HALO_S, HALO_P = 32, 8, 16
SUBLANES = 8
LANES = 128
VMEM_LIMIT_BYTES = 56 * 1024 * 1024

TS_MIX = 512
TS_MIXB = 256
TS_FFN = 256
TS_POOL = 512
TS_MM = 512
R_CHUNK = 64
MID_STEPS_BEFORE_END = 1

MESH = pl.DeviceIdType.MESH
ANY = pl.BlockSpec(memory_space=pl.ANY)
NT_DIMS = (((1,), (1,)), ((), ()))
TN_DIMS = (((0,), (0,)), ((), ()))


def _cparams(n_axes):
    return pltpu.CompilerParams(dimension_semantics=("arbitrary",) * n_axes, vmem_limit_bytes=VMEM_LIMIT_BYTES)


def _const(shape):
    nd = len(shape)
    return pl.BlockSpec(shape, lambda *_: (0,) * nd, pipeline_mode=pl.Buffered(1))


def _sigmoid(v):
    return 1.0 / (1.0 + jnp.exp(-v))


def _rsqrt_mean_sq(x):
    return lax.rsqrt(jnp.mean(x * x, axis=-1, keepdims=True) + RMS_EPS)


def _rms_bwd(dh, xh, r, g):
    dxh = dh * g
    return r * (dxh - xh * jnp.mean(dxh * xh, axis=-1, keepdims=True))


def _shifted(buf_ref, row0, rows, col0, width, offsets):
    lo = (min(offsets) // SUBLANES) * SUBLANES
    hi = -(-(max(offsets) + rows) // SUBLANES) * SUBLANES
    start = row0 + lo if isinstance(row0, int) else pl.multiple_of(row0 + lo, SUBLANES)
    win = buf_ref[pl.ds(start, hi - lo), col0:col0 + width]
    out = {}
    for res in sorted({(o - lo) % SUBLANES for o in offsets}):
        qs = {o: (o - lo) // SUBLANES for o in offsets if (o - lo) % SUBLANES == res}
        base = pltpu.roll(win, hi - lo - res, 0) if res else win
        for o, q in qs.items():
            out[o] = base[SUBLANES * q:SUBLANES * q + rows, :]
    return out


def _rowsum8(v):
    acc = v[0:SUBLANES, :]
    for r in range(SUBLANES, v.shape[0], SUBLANES):
        acc = acc + v[r:r + SUBLANES, :]
    return acc


def _taps(w_ref, sh, offsets, col0, width):
    acc = None
    for k, o in enumerate(offsets):
        term = w_ref[k:k + 1, col0:col0 + width] * sh[o]
        acc = term if acc is None else acc + term
    return acc


def _window_bases(buf_ref, row0, rows, col0, width, offsets):
    lo = (min(offsets) // SUBLANES) * SUBLANES
    hi = -(-(max(offsets) + rows) // SUBLANES) * SUBLANES
    start = row0 + lo if isinstance(row0, int) else pl.multiple_of(row0 + lo, SUBLANES)
    win = buf_ref[pl.ds(start, hi - lo), col0:col0 + width]
    for res in sorted({(o - lo) % SUBLANES for o in offsets}):
        taps = [(k, (o - lo) // SUBLANES * SUBLANES) for k, o in enumerate(offsets) if (o - lo) % SUBLANES == res]
        yield (pltpu.roll(win, hi - lo - res, 0) if res else win), taps


def _conv_acc(buf_ref, w_ref, row0, rows, col0, width, offsets):
    acc = None
    for base, taps in _window_bases(buf_ref, row0, rows, col0, width, offsets):
        for k, q in taps:
            term = w_ref[k:k + 1, col0:col0 + width] * base[q:q + rows, :]
            acc = term if acc is None else acc + term
    return acc


def _conv_corr(buf_ref, w_ref, other, acc_ref, row0, rows, col0, width, offsets):
    acc = None
    for base, taps in _window_bases(buf_ref, row0, rows, col0, width, offsets):
        for k, q in taps:
            sl = base[q:q + rows, :]
            term = w_ref[k:k + 1, col0:col0 + width] * sl
            acc = term if acc is None else acc + term
            acc_ref[SUBLANES * k:SUBLANES * (k + 1), col0:col0 + width] += _rowsum8(sl * other)
    return acc


def _finish_tap_sums(acc_ref, out_ref, n_taps):
    for k in range(n_taps):
        out_ref[k:k + 1, :] = jnp.sum(acc_ref[SUBLANES * k:SUBLANES * (k + 1), :], axis=0, keepdims=True)


class _Comm:
    def __init__(self, inputs, out_shapes, sems, start, finish, aliases=None, mid=None):
        self.inputs, self.out_shapes, self.sems = list(inputs), list(out_shapes), list(sems)
        self.start, self.finish, self.mid, self.aliases = start, finish, mid, dict(aliases or {})


def _join_comm(a, b):
    ni, no, ns = len(a.inputs), len(a.out_shapes), len(a.sems)

    def both(name):
        def run(ins, outs, sems):
            if getattr(a, name) is not None:
                getattr(a, name)(ins[:ni], outs[:no], sems[:ns])
            if getattr(b, name) is not None:
                getattr(b, name)(ins[ni:], outs[no:], sems[ns:])
        return run

    aliases = {**a.aliases, **{ni + i: no + o for i, o in b.aliases.items()}}
    mid = both("mid") if (a.mid is not None or b.mid is not None) else None
    return _Comm(a.inputs + b.inputs, a.out_shapes + b.out_shapes, a.sems + b.sems, both("start"), both("finish"), aliases, mid)


def _call(body, *, name, grid, in_specs, out_specs, out_shape, args, scratch=(), comm=None, aliases=None):
    n_in, n_out, n_scr, n_axes = len(in_specs), len(out_specs), len(scratch), len(grid)
    params = pltpu.CompilerParams(dimension_semantics=("arbitrary",) * n_axes, vmem_limit_bytes=VMEM_LIMIT_BYTES)
    aliases = dict(aliases or {})
    if comm is None:
        outs = pl.pallas_call(body, name=name, grid=grid, in_specs=list(in_specs), out_specs=list(out_specs),
                              out_shape=list(out_shape), scratch_shapes=list(scratch), input_output_aliases=aliases,
                              compiler_params=params)(*args)
        return list(outs), []
    ci, co = len(comm.inputs), len(comm.out_shapes)

    def wrapped(*refs):
        k_in, c_in = refs[:n_in], refs[n_in:n_in + ci]
        o0 = n_in + ci
        k_out, c_out = refs[o0:o0 + n_out], refs[o0 + n_out:o0 + n_out + co]
        s0 = o0 + n_out + co
        k_scr, c_sem = refs[s0:s0 + n_scr], refs[s0 + n_scr:]
        first = pl.program_id(0) == 0
        last = pl.program_id(0) == grid[0] - 1
        for ax in range(1, n_axes):
            first = jnp.logical_and(first, pl.program_id(ax) == 0)
            last = jnp.logical_and(last, pl.program_id(ax) == grid[ax] - 1)

        @pl.when(first)
        def _():
            comm.start(c_in, c_out, c_sem)

        if comm.mid is not None and n_axes == 1 and grid[0] > MID_STEPS_BEFORE_END:
            @pl.when(pl.program_id(0) == grid[0] - 1 - MID_STEPS_BEFORE_END)
            def _():
                comm.mid(c_in, c_out, c_sem)

        body(*k_in, *k_out, *k_scr)

        @pl.when(last)
        def _():
            if comm.mid is not None and not (n_axes == 1 and grid[0] > MID_STEPS_BEFORE_END):
                comm.mid(c_in, c_out, c_sem)
            comm.finish(c_in, c_out, c_sem)

    outs = pl.pallas_call(
        wrapped, name=name, grid=grid, in_specs=list(in_specs) + [ANY] * ci, out_specs=list(out_specs) + [ANY] * co,
        out_shape=list(out_shape) + comm.out_shapes, scratch_shapes=list(scratch) + comm.sems,
        input_output_aliases={**aliases, **{n_in + i: n_out + o for i, o in comm.aliases.items()}}, compiler_params=params,
    )(*args, *comm.inputs)
    return list(outs[:n_out]), list(outs[n_out:])


def _run_comm(comm, name):
    ci, co = len(comm.inputs), len(comm.out_shapes)

    def body(*refs):
        c_in, c_out, c_sem = refs[:ci], refs[ci:ci + co], refs[ci + co:]
        comm.start(c_in, c_out, c_sem)
        if comm.mid is not None:
            comm.mid(c_in, c_out, c_sem)
        comm.finish(c_in, c_out, c_sem)

    return list(pl.pallas_call(body, name=name, in_specs=[ANY] * ci, out_specs=[ANY] * co, out_shape=comm.out_shapes,
                               scratch_shapes=comm.sems, input_output_aliases=comm.aliases)(*comm.inputs))


def _mix0_fwd(x, g, w_in, conv_a, ln_g, ln_b, conv_b, w_out, comm=None):
    s = x.shape[0]
    ts = min(TS_MIX, s)
    n = s // ts
    bw = NZ // NCHIP
    offs_a = [HALO_A - (K_A - 1) + k for k in range(K_A)]
    offs_s = [HALO_S - (K_S - 1) + k for k in range(K_S)]

    def body(x_ref, g_ref, win_ref, ca_ref, lg_ref, lb_ref, cb_ref, wout_ref,
             h_ref, z_ref, ac_ref, bconv_buf, cat_ref, x1_ref, glu_buf, cv_buf):
        i = pl.program_id(0)

        @pl.when(i == 0)
        def _():
            glu_buf[0:HALO_A, :] = jnp.zeros((HALO_A, A), F32)
            cv_buf[0:HALO_S, :] = jnp.zeros((HALO_S, A), F32)

        xv = x_ref[...]
        h = (xv * _rsqrt_mean_sq(xv) * g_ref[...]).astype(BF16)
        h_ref[...] = h
        for j in range(NCHIP):
            z_ref[:, j * bw:(j + 1) * bw] = jnp.dot(h, win_ref[j], preferred_element_type=F32)
        glu_buf[HALO_A:HALO_A + ts, :] = z_ref[:, 0:A] * _sigmoid(z_ref[:, A:2 * A])
        cv_buf[HALO_S:HALO_S + ts, :] = z_ref[:, 3 * A:4 * A] * z_ref[:, 4 * A:5 * A]

        def chunk(ci, carry):
            r0 = pl.multiple_of(ci * R_CHUNK, R_CHUNK)
            for c0 in range(0, A, LANES):
                ac_ref[pl.ds(r0, R_CHUNK), c0:c0 + LANES] = _conv_acc(glu_buf, ca_ref, r0, R_CHUNK, c0, LANES, offs_a)
                bconv_buf[pl.ds(r0, R_CHUNK), c0:c0 + LANES] = _conv_acc(cv_buf, cb_ref, r0, R_CHUNK, c0, LANES, offs_s)
            return carry

        lax.fori_loop(0, ts // R_CHUNK, chunk, 0)
        glu_buf[0:HALO_A, :] = glu_buf[ts:ts + HALO_A, :]
        cv_buf[0:HALO_S, :] = cv_buf[ts:ts + HALO_S, :]

        ac = ac_ref[...]
        xc = ac - jnp.mean(ac, axis=-1, keepdims=True)
        xn = xc * lax.rsqrt(jnp.mean(xc * xc, axis=-1, keepdims=True) + LN_EPS)
        ln = xn * lg_ref[...] + lb_ref[...]
        cat_ref[:, 0:A] = (ln * _sigmoid(ln)).astype(BF16)
        cat_ref[:, A:2 * A] = (z_ref[:, 2 * A:3 * A] * bconv_buf[...]).astype(BF16)
        x1_ref[...] = xv + jnp.dot(cat_ref[...], wout_ref[...], preferred_element_type=F32)

    tile = lambda w: pl.BlockSpec((ts, w), lambda i: (i, 0))
    return _call(
        body, name="mix0_fwd", grid=(n,), comm=comm, args=(x, g, w_in, conv_a, ln_g, ln_b, conv_b, w_out),
        in_specs=[tile(D), _const((1, D)), _const((NCHIP, D, bw)), _const((K_A, A)), _const((1, A)), _const((1, A)),
                  _const((K_S, A)), _const((2 * A, D))],
        out_specs=[tile(D), tile(NZ), tile(A), tile(A), tile(2 * A), tile(D)],
        out_shape=[jax.ShapeDtypeStruct((s, D), BF16), jax.ShapeDtypeStruct((s, NZ), F32), jax.ShapeDtypeStruct((s, A), F32),
                   jax.ShapeDtypeStruct((s, A), F32), jax.ShapeDtypeStruct((s, 2 * A), BF16), jax.ShapeDtypeStruct((s, D), F32)],
        scratch=[pltpu.VMEM((HALO_A + ts, A), F32), pltpu.VMEM((HALO_S + ts, A), F32)])


def _ffn_fwd(x, g, w_up, wc, w_down, name, comm=None):
    s = x.shape[0]
    ts = min(TS_FFN, s)
    n = s // ts
    bw = FF2 // NCHIP
    rows = 32
    offs = [HALO_S - (K_S - 1) + k for k in range(K_S)]

    def body(x_ref, g_ref, wup_ref, wc_ref, wdn_ref, h_ref, u0_ref, u_ref, act_ref, xo_ref, cbuf):
        i = pl.program_id(0)

        @pl.when(i == 0)
        def _():
            cbuf[0:HALO_S, :] = jnp.zeros((HALO_S, FF2), F32)

        xv = x_ref[...]
        h = (xv * _rsqrt_mean_sq(xv) * g_ref[...]).astype(BF16)
        h_ref[...] = h
        f = None
        for p in range(NCHIP // 2):
            for j in (p, NCHIP // 2 + p):
                zc = jnp.dot(h, wup_ref[j], preferred_element_type=F32)
                u0_ref[:, j * bw:(j + 1) * bw] = zc.astype(BF16)
                cbuf[HALO_S:HALO_S + ts, j * bw:(j + 1) * bw] = zc
            for r0 in range(0, ts, rows):
                for c0 in range(p * bw, (p + 1) * bw, LANES):
                    ug = _taps(wc_ref, _shifted(cbuf, r0, rows, c0, LANES, offs), offs, c0, LANES)
                    uv = _taps(wc_ref, _shifted(cbuf, r0, rows, FF + c0, LANES, offs), offs, FF + c0, LANES)
                    u_ref[r0:r0 + rows, c0:c0 + LANES] = ug
                    u_ref[r0:r0 + rows, FF + c0:FF + c0 + LANES] = uv
                    act_ref[r0:r0 + rows, c0:c0 + LANES] = (ug * _sigmoid(ug) * uv).astype(BF16)
            fp = jnp.dot(act_ref[:, p * bw:(p + 1) * bw], wdn_ref[p * bw:(p + 1) * bw, :], preferred_element_type=F32)
            f = fp if f is None else f + fp
        cbuf[0:HALO_S, :] = cbuf[ts:ts + HALO_S, :]
        xo_ref[...] = xv + f

    tile = lambda w: pl.BlockSpec((ts, w), lambda i: (i, 0))
    return _call(
        body, name=name, grid=(n,), comm=comm, args=(x, g, w_up, wc, w_down),
        in_specs=[tile(D), _const((1, D)), _const((NCHIP, D, bw)), _const((K_S, FF2)), _const((FF, D))],
        out_specs=[tile(D), tile(FF2), tile(FF2), tile(FF), tile(D)],
        out_shape=[jax.ShapeDtypeStruct((s, D), BF16), jax.ShapeDtypeStruct((s, FF2), BF16), jax.ShapeDtypeStruct((s, FF2), F32),
                   jax.ShapeDtypeStruct((s, FF), BF16), jax.ShapeDtypeStruct((s, D), F32)],
        scratch=[pltpu.VMEM((HALO_S + ts, FF2), F32)])


def _pool_windows(hbuf, pbuf, tile_row0, ts):
    def chunk(ci, carry):
        r0 = pl.multiple_of(ci * R_CHUNK, R_CHUNK)
        t1 = (tile_row0 + r0 + lax.broadcasted_iota(jnp.int32, (R_CHUNK, 1), 0) + 1).astype(F32)
        for gi, w in enumerate(POOL_WINDOWS):
            cnt = jnp.minimum(t1, float(w))
            offs = [HALO_P - jj for jj in range(w)]
            for c0 in range(gi * PG, (gi + 1) * PG, LANES):
                sh = _shifted(hbuf, r0, R_CHUNK, c0, LANES, offs)
                tot = sh[offs[0]]
                for o in offs[1:]:
                    tot = tot + sh[o]
                pbuf[pl.ds(r0, R_CHUNK), c0:c0 + LANES] = (tot / cnt - sh[HALO_P]).astype(BF16)
        return carry

    lax.fori_loop(0, ts // R_CHUNK, chunk, 0)


def _assemble_wpool(wp_ref, wps):
    rb = PG // NCHIP
    for gi in range(len(POOL_WINDOWS)):
        for j in range(NCHIP):
            wps[gi, j * rb:(j + 1) * rb, :] = wp_ref[j, gi]


def _pool_fwd(x, g, w_pool, scale):
    s = x.shape[0]
    ts = min(TS_POOL, s)
    n = s // ts
    ng = len(POOL_WINDOWS)

    def body(x_ref, g_ref, wp_ref, sc_ref, xo_ref, hbuf, pbuf, wps):
        i = pl.program_id(0)

        @pl.when(i == 0)
        def _():
            hbuf[0:HALO_P, :] = jnp.zeros((HALO_P, D), F32)
            _assemble_wpool(wp_ref, wps)

        xv = x_ref[...]
        hbuf[HALO_P:HALO_P + ts, :] = xv * _rsqrt_mean_sq(xv) * g_ref[...]
        _pool_windows(hbuf, pbuf, i * ts, ts)
        hbuf[0:HALO_P, :] = hbuf[ts:ts + HALO_P, :]
        for gi in range(ng):
            cols = slice(gi * PG, (gi + 1) * PG)
            y = jnp.dot(pbuf[:, cols], wps[gi], preferred_element_type=F32)
            xo_ref[:, cols] = xv[:, cols] + y * sc_ref[:, cols]

    tile = pl.BlockSpec((ts, D), lambda i: (i, 0))
    return pl.pallas_call(
        body, name="pool_fwd", grid=(n,),
        in_specs=[tile, _const((1, D)), _const((NCHIP, ng, PG // NCHIP, PG)), _const((1, D))],
        out_specs=tile, out_shape=jax.ShapeDtypeStruct((s, D), F32),
        scratch_shapes=[pltpu.VMEM((HALO_P + ts, D), F32), pltpu.VMEM((ts, D), BF16), pltpu.VMEM((ng, PG, PG), BF16)],
        compiler_params=_cparams(1),
    )(x, g, w_pool, scale)


def _final_loss(x, g, target):
    s = x.shape[0]
    ts = min(TS_MM, s)
    n = s // ts

    def body(x_ref, g_ref, t_ref, dx_ref, dg_ref, loss_ref):
        i = pl.program_id(0)

        @pl.when(i == 0)
        def _():
            dg_ref[...] = jnp.zeros((1, D), F32)
            loss_ref[...] = jnp.zeros((1, LANES), F32)

        xv = x_ref[...]
        r = _rsqrt_mean_sq(xv)
        xh = xv * r
        gv = g_ref[...]
        err = xh * gv - t_ref[...]
        sq = jnp.sum(jnp.sum(err * err, axis=1, keepdims=True), axis=0, keepdims=True)
        loss_ref[...] += sq * (0.5 / D)
        dy = err * (1.0 / D)
        dg_ref[...] += jnp.sum(dy * xh, axis=0, keepdims=True)
        dx_ref[...] = _rms_bwd(dy, xh, r, gv)

    tile = pl.BlockSpec((ts, D), lambda i: (i, 0))
    return pl.pallas_call(
        body, name="final_loss", grid=(n,),
        in_specs=[tile, _const((1, D)), tile],
        out_specs=[tile, pl.BlockSpec((1, D), lambda i: (0, 0)), pl.BlockSpec((1, LANES), lambda i: (0, 0))],
        out_shape=[jax.ShapeDtypeStruct((s, D), F32), jax.ShapeDtypeStruct((1, D), F32), jax.ShapeDtypeStruct((1, LANES), F32)],
        compiler_params=_cparams(1),
    )(x, g, target)


def _ffn_bwd_a(dxo, u, u0, wc, w_down, name, comm=None):
    s = dxo.shape[0]
    ts = min(TS_FFN, s)
    n = s // ts
    cw = FF2 // NCHIP
    rows = 32
    lw2 = 2 * LANES
    boffs = [K_S - 1 - k for k in range(K_S)]

    def body(dxo_ref, u_ref, u0_ref, wc_ref, wdn_ref, du0_ref, dwc_ref, dubuf, dact, dwacc):
        i = pl.program_id(0)

        @pl.when(i == 0)
        def _():
            dubuf[ts:ts + HALO_S, :] = jnp.zeros((HALO_S, FF2), F32)
            dwacc[...] = jnp.zeros(dwacc.shape, F32)

        df = dxo_ref[...].astype(BF16)
        for cg in range(0, FF, cw):
            dact[...] = lax.dot_general(df, wdn_ref[cg:cg + cw, :], NT_DIMS, preferred_element_type=F32)

            def chunk(ci, carry, cg=cg):
                rs = pl.ds(pl.multiple_of(ci * rows, rows), rows)
                for c in range(cg, cg + cw, LANES):
                    ug, uv = u_ref[rs, c:c + LANES], u_ref[rs, FF + c:FF + c + LANES]
                    sg = _sigmoid(ug)
                    da = dact[rs, c - cg:c - cg + LANES]
                    gs = ug * sg
                    dubuf[rs, c:c + LANES] = da * uv * (sg + gs * (1.0 - sg))
                    dubuf[rs, FF + c:FF + c + LANES] = da * gs
                return carry

            lax.fori_loop(0, ts // rows, chunk, 0)

        def chunk2(ci, carry):
            r0 = pl.multiple_of(ci * rows, rows)
            rs = pl.ds(r0, rows)
            for c in range(0, FF2, lw2):
                sh = _shifted(dubuf, r0, rows, c, lw2, boffs)
                du0_ref[rs, c:c + lw2] = _taps(wc_ref, sh, boffs, c, lw2).astype(BF16)
                u0v = u0_ref[rs, c:c + lw2].astype(F32)
                for k, o in enumerate(boffs):
                    dwacc[SUBLANES * k:SUBLANES * (k + 1), c:c + lw2] += _rowsum8(sh[o] * u0v)
            return carry

        lax.fori_loop(0, ts // rows, chunk2, 0)
        dubuf[ts:ts + HALO_S, :] = dubuf[0:HALO_S, :]

        @pl.when(i == n - 1)
        def _():
            _finish_tap_sums(dwacc, dwc_ref, K_S)

    rev = lambda w: pl.BlockSpec((ts, w), lambda i: (n - 1 - i, 0))
    return _call(
        body, name=name, grid=(n,), comm=comm, args=(dxo, u, u0, wc, w_down),
        in_specs=[rev(D), rev(FF2), rev(FF2), _const((K_S, FF2)), _const((FF, D))],
        out_specs=[rev(FF2), pl.BlockSpec((K_S, FF2), lambda i: (0, 0))],
        out_shape=[jax.ShapeDtypeStruct((s, FF2), BF16), jax.ShapeDtypeStruct((K_S, FF2), F32)],
        scratch=[pltpu.VMEM((ts + HALO_S, FF2), F32), pltpu.VMEM((ts, cw), F32), pltpu.VMEM((SUBLANES * K_S, FF2), F32)])


def _nt_rms_bwd(dy, w, x, g, dres, name, comm=None, tiles=None, dx_so_far=None):
    s = x.shape[0]
    ts = min(TS_MM, s)
    first, n = (0, s // ts) if tiles is None else tiles
    nw = dy.shape[1]
    bw = nw // NCHIP

    def body(dy_ref, w_ref, x_ref, g_ref, dres_ref, dx_ref, dg_ref):
        i = pl.program_id(0)

        @pl.when(i == 0)
        def _():
            dg_ref[...] = jnp.zeros((1, D), F32)

        dh = lax.dot_general(dy_ref[:, 0:bw], w_ref[0], NT_DIMS, preferred_element_type=F32)
        for j in range(1, NCHIP):
            dh = dh + lax.dot_general(dy_ref[:, j * bw:(j + 1) * bw], w_ref[j], NT_DIMS, preferred_element_type=F32)
        xv = x_ref[...]
        r = _rsqrt_mean_sq(xv)
        xh = xv * r
        dg_ref[...] += jnp.sum(dh * xh, axis=0, keepdims=True)
        dx_ref[...] = dres_ref[...] + _rms_bwd(dh, xh, r, g_ref[...])

    def body_with_alias(dy_ref, w_ref, x_ref, g_ref, dres_ref, _, dx_ref, dg_ref):
        body(dy_ref, w_ref, x_ref, g_ref, dres_ref, dx_ref, dg_ref)

    tile = lambda wd: pl.BlockSpec((ts, wd), lambda i: (first + i, 0))
    in_specs = [tile(nw), _const((NCHIP, D, bw)), tile(D), _const((1, D)), tile(D)]
    more = dx_so_far is not None
    return _call(
        body_with_alias if more else body, name=name, grid=(n,), comm=comm,
        args=(dy, w, x, g, dres) + ((dx_so_far,) if more else ()), in_specs=in_specs + [ANY] * more,
        out_specs=[tile(D), pl.BlockSpec((1, D), lambda i: (0, 0))], aliases={5: 0} if more else None,
        out_shape=[jax.ShapeDtypeStruct((s, D), F32), jax.ShapeDtypeStruct((1, D), F32)])


def _wgrad(a, b, n_blocks, name, comm=None):
    s, m = a.shape
    bw = b.shape[1] // n_blocks
    tk = min(TS_MM, s)

    def body(a_ref, b_ref, o_ref):
        @pl.when(pl.program_id(0) == 0)
        def _():
            o_ref[...] = jnp.zeros(o_ref.shape, F32)

        for j in range(n_blocks):
            o_ref[j] += lax.dot_general(a_ref[...], b_ref[:, j * bw:(j + 1) * bw].astype(BF16), TN_DIMS,
                                        preferred_element_type=F32)

    (out,), comm_out = _call(
        body, name=name, grid=(s // tk,), comm=comm, args=(a, b),
        in_specs=[pl.BlockSpec((tk, m), lambda k: (k, 0)), pl.BlockSpec((tk, b.shape[1]), lambda k: (k, 0))],
        out_specs=[pl.BlockSpec((n_blocks, m, bw), lambda k: (0, 0, 0))],
        out_shape=[jax.ShapeDtypeStruct((n_blocks, m, bw), F32)])
    return out, comm_out


def _pool_bwd(dxo, x, g, w_pool, scale, comm=None):
    s = x.shape[0]
    ts = min(TS_POOL, s)
    n = s // ts
    ng = len(POOL_WINDOWS)
    rb = PG // NCHIP

    def body(dxo_ref, x_ref, halo_ref, g_ref, wp_ref, sc_ref, dx_ref, dwp_ref, dsc_ref, dg_ref,
             hbuf, pbuf, qbuf, dhbuf, wps, dwacc):
        i = pl.program_id(0)
        j = n - 1 - i

        @pl.when(i == 0)
        def _():
            qbuf[ts:ts + HALO_P, :] = jnp.zeros((HALO_P, D), F32)
            dwacc[...] = jnp.zeros(dwacc.shape, F32)
            dsc_ref[...] = jnp.zeros((1, D), F32)
            dg_ref[...] = jnp.zeros((1, D), F32)
            _assemble_wpool(wp_ref, wps)

        gv = g_ref[...]
        xl = halo_ref[...]
        hbuf[0:HALO_P, :] = jnp.where(j == 0, 0.0, xl * _rsqrt_mean_sq(xl) * gv)
        xv = x_ref[...]
        r = _rsqrt_mean_sq(xv)
        xh = xv * r
        hbuf[HALO_P:HALO_P + ts, :] = xh * gv
        _pool_windows(hbuf, pbuf, j * ts, ts)

        dy = dxo_ref[...]
        t1 = (j * ts + lax.broadcasted_iota(jnp.int32, (ts, 1), 0) + 1).astype(F32)
        for gi, w in enumerate(POOL_WINDOWS):
            cols = slice(gi * PG, (gi + 1) * PG)
            p = pbuf[:, cols]
            y = jnp.dot(p, wps[gi], preferred_element_type=F32)
            dsc_ref[:, cols] += jnp.sum(dy[:, cols] * y, axis=0, keepdims=True)
            dq = (dy[:, cols] * sc_ref[:, cols]).astype(BF16)
            dwacc[gi] += lax.dot_general(p, dq, TN_DIMS, preferred_element_type=F32)
            dp = lax.dot_general(dq, wps[gi], NT_DIMS, preferred_element_type=F32)
            qbuf[0:ts, cols] = dp / jnp.minimum(t1, float(w))

        def chunk(ci, carry):
            r0 = pl.multiple_of(ci * R_CHUNK, R_CHUNK)
            tc = (j * ts + r0 + lax.broadcasted_iota(jnp.int32, (R_CHUNK, 1), 0) + 1).astype(F32)
            for gi, w in enumerate(POOL_WINDOWS):
                cnt = jnp.minimum(tc, float(w))
                offs = list(range(w))
                for c0 in range(gi * PG, (gi + 1) * PG, LANES):
                    sh = _shifted(qbuf, r0, R_CHUNK, c0, LANES, offs)
                    tot = sh[0]
                    for o in offs[1:]:
                        tot = tot + sh[o]
                    dhbuf[pl.ds(r0, R_CHUNK), c0:c0 + LANES] = tot - sh[0] * cnt
            return carry

        lax.fori_loop(0, ts // R_CHUNK, chunk, 0)
        qbuf[ts:ts + HALO_P, :] = qbuf[0:HALO_P, :]
        dh = dhbuf[...]
        dg_ref[...] += jnp.sum(dh * xh, axis=0, keepdims=True)
        dx_ref[...] = dy + _rms_bwd(dh, xh, r, gv)

        @pl.when(i == n - 1)
        def _():
            for gi in range(ng):
                for jj in range(NCHIP):
                    dwp_ref[jj, gi] = dwacc[gi, jj * rb:(jj + 1) * rb, :]

    rev = pl.BlockSpec((ts, D), lambda i: (n - 1 - i, 0))
    halo = pl.BlockSpec((HALO_P, D), lambda i: (jnp.maximum((n - 1 - i) * (ts // HALO_P) - 1, 0), 0))
    vec = pl.BlockSpec((1, D), lambda i: (0, 0))
    return _call(
        body, name="pool_bwd", grid=(n,), comm=comm, args=(dxo, x, x, g, w_pool, scale),
        in_specs=[rev, rev, halo, _const((1, D)), _const((NCHIP, ng, rb, PG)), _const((1, D))],
        out_specs=[rev, pl.BlockSpec((NCHIP, ng, rb, PG), lambda i: (0, 0, 0, 0)), vec, vec],
        out_shape=[jax.ShapeDtypeStruct((s, D), F32), jax.ShapeDtypeStruct((NCHIP, ng, rb, PG), F32),
                   jax.ShapeDtypeStruct((1, D), F32), jax.ShapeDtypeStruct((1, D), F32)],
        scratch=[pltpu.VMEM((HALO_P + ts, D), F32), pltpu.VMEM((ts, D), BF16), pltpu.VMEM((ts + HALO_P, D), F32),
                 pltpu.VMEM((ts, D), F32), pltpu.VMEM((ng, PG, PG), BF16), pltpu.VMEM((ng, PG, PG), F32)])


def _mix0_bwd_a(dx1, z, ac, bconv, w_out, conv_a, ln_g, ln_b, conv_b, comm=None):
    s = dx1.shape[0]
    ts = min(TS_MIXB, s)
    n = s // ts
    rows = 32
    boffs_a = [K_A - 1 - k for k in range(K_A)]
    boffs_s = [K_S - 1 - k for k in range(K_S)]

    def body(dx1_ref, z_ref, ac_ref, bconv_ref, wout_ref, ca_ref, lg_ref, lb_ref, cb_ref,
             dz_ref, dca_ref, dlg_ref, dlb_ref, dcb_ref, dac_buf, dbc_buf, dca_acc, dcb_acc):
        i = pl.program_id(0)

        @pl.when(i == 0)
        def _():
            dac_buf[ts:ts + HALO_A, :] = jnp.zeros((HALO_A, A), F32)
            dbc_buf[ts:ts + HALO_S, :] = jnp.zeros((HALO_S, A), F32)
            dca_acc[...] = jnp.zeros(dca_acc.shape, F32)
            dcb_acc[...] = jnp.zeros(dcb_acc.shape, F32)
            dlg_ref[...] = jnp.zeros((1, A), F32)
            dlb_ref[...] = jnp.zeros((1, A), F32)

        dcat = lax.dot_general(dx1_ref[...].astype(BF16), wout_ref[...], NT_DIMS, preferred_element_type=F32)
        db = dcat[:, A:2 * A]
        dz_ref[:, 2 * A:3 * A] = (db * bconv_ref[...]).astype(BF16)
        dbc_buf[0:ts, :] = db * z_ref[:, 2 * A:3 * A]
        ac = ac_ref[...]
        xc = ac - jnp.mean(ac, axis=-1, keepdims=True)
        rstd = lax.rsqrt(jnp.mean(xc * xc, axis=-1, keepdims=True) + LN_EPS)
        xn = xc * rstd
        lg = lg_ref[...]
        ln = xn * lg + lb_ref[...]
        sl = _sigmoid(ln)
        dln = dcat[:, 0:A] * sl * (1.0 + ln * (1.0 - sl))
        dlg_ref[...] += jnp.sum(dln * xn, axis=0, keepdims=True)
        dlb_ref[...] += jnp.sum(dln, axis=0, keepdims=True)
        dxn = dln * lg
        dac_buf[0:ts, :] = rstd * (dxn - jnp.mean(dxn, axis=-1, keepdims=True)
                                   - xn * jnp.mean(dxn * xn, axis=-1, keepdims=True))

        def chunk(ci, carry):
            r0 = pl.multiple_of(ci * rows, rows)
            rs = pl.ds(r0, rows)
            for c0 in range(0, A, LANES):
                col = lambda grp: slice(grp * A + c0, grp * A + c0 + LANES)
                a_val, sg = z_ref[rs, col(0)], _sigmoid(z_ref[rs, col(1)])
                dglu = _conv_corr(dac_buf, ca_ref, a_val * sg, dca_acc, r0, rows, c0, LANES, boffs_a)
                dz_ref[rs, col(0)] = (dglu * sg).astype(BF16)
                dz_ref[rs, col(1)] = (dglu * a_val * sg * (1.0 - sg)).astype(BF16)
                c_gate, bc_val = z_ref[rs, col(3)], z_ref[rs, col(4)]
                dcv = _conv_corr(dbc_buf, cb_ref, c_gate * bc_val, dcb_acc, r0, rows, c0, LANES, boffs_s)
                dz_ref[rs, col(3)] = (dcv * bc_val).astype(BF16)
                dz_ref[rs, col(4)] = (dcv * c_gate).astype(BF16)
            return carry

        lax.fori_loop(0, ts // rows, chunk, 0)
        dac_buf[ts:ts + HALO_A, :] = dac_buf[0:HALO_A, :]
        dbc_buf[ts:ts + HALO_S, :] = dbc_buf[0:HALO_S, :]

        @pl.when(i == n - 1)
        def _():
            _finish_tap_sums(dca_acc, dca_ref, K_A)
            _finish_tap_sums(dcb_acc, dcb_ref, K_S)

    rev = lambda w: pl.BlockSpec((ts, w), lambda i: (n - 1 - i, 0))
    full = lambda r, c: pl.BlockSpec((r, c), lambda i: (0, 0))
    return _call(
        body, name="mix0_bwd_a", grid=(n,), comm=comm, args=(dx1, z, ac, bconv, w_out, conv_a, ln_g, ln_b, conv_b),
        in_specs=[rev(D), rev(NZ), rev(A), rev(A), _const((2 * A, D)), _const((K_A, A)), _const((1, A)), _const((1, A)),
                  _const((K_S, A))],
        out_specs=[rev(NZ), full(K_A, A), full(1, A), full(1, A), full(K_S, A)],
        out_shape=[jax.ShapeDtypeStruct((s, NZ), BF16), jax.ShapeDtypeStruct((K_A, A), F32), jax.ShapeDtypeStruct((1, A), F32),
                   jax.ShapeDtypeStruct((1, A), F32), jax.ShapeDtypeStruct((K_S, A), F32)],
        scratch=[pltpu.VMEM((ts + HALO_A, A), F32), pltpu.VMEM((ts + HALO_S, A), F32), pltpu.VMEM((SUBLANES * K_A, A), F32),
                 pltpu.VMEM((SUBLANES * K_S, A), F32)])


def _cast_later_weights(w_up, w_down, w_pool, comm):
    nl = w_up.shape[0]

    def body(up_ref, dn_ref, pool_ref, *outs):
        up_o, dn_o, pool_o = outs[:nl], outs[nl:2 * nl], outs[2 * nl]
        for layer in range(nl):
            @pl.when(pl.program_id(0) == layer)
            def _(layer=layer):
                up_o[layer][...] = up_ref[...].astype(BF16)
                dn_o[layer][...] = dn_ref[...].astype(BF16)
                if layer == 0:
                    pool_o[...] = pool_ref[...].astype(BF16)

    per_layer = lambda a: pl.BlockSpec((None,) + a.shape[1:], lambda l: (l,) + (0,) * (a.ndim - 1))
    whole = lambda shape: pl.BlockSpec(shape, lambda l: (0,) * len(shape))
    out_shapes = [w_up.shape[1:]] * nl + [w_down.shape[1:]] * nl + [w_pool.shape]
    return _call(body, name="cast_later_weights", grid=(nl,), comm=comm, args=(w_up, w_down, w_pool),
                 in_specs=[per_layer(w_up), per_layer(w_down), whole(w_pool.shape)],
                 out_specs=[whole(shape) for shape in out_shapes],
                 out_shape=[jax.ShapeDtypeStruct(shape, BF16) for shape in out_shapes])


def _train_step(x, target, place, shard, later, rep, small_pack, small_shapes, adamw):
    bw_up, bw_in = FF2 // NCHIP, NZ // NCHIP
    five = lambda a, k: a.reshape(NCHIP, 2, *HALF[k])

    half = lambda a, name: a.reshape((2,) + HALF[name])
    (up0, up1, dn0, dn1, pool_bf), (g_in, g_out, small_g) = _cast_later_weights(
        *later, comm=_gather_comm([shard["w_in"], shard["w_out"]], small_pack))
    shard = dict(shard, w_pool=half(pool_bf, "w_pool"), w_up0=half(up0, "w_up"), w_up1=half(up1, "w_up"),
                 w_down0=half(dn0, "w_down"), w_down1=half(dn1, "w_down"))
    whole = {}
    for k, part in zip(SMALL_SHARDED, _unpack(small_g, small_shapes, lead=(NCHIP,))):
        whole[k] = jnp.moveaxis(part, 0, 1).reshape(part.shape[1], NCHIP * part.shape[2])
    w_in, w_out = g_in.reshape(NCHIP, D, bw_in), g_out.reshape(2 * A, D)
    conv_ffn = whole["conv_ffn_w"].reshape(2, K_S, FF2)
    nffn = [rep["norm_ffn"][0:1], rep["norm_ffn"][1:2]]

    (h0, z, ac, bconv, cat, x1), (g_up0, g_dn0) = _mix0_fwd(
        x, rep["norm_mix_even"], w_in, whole["conv_a"], rep["ln_a_g"], rep["ln_a_b"], whole["conv_b"], w_out,
        comm=_gather_comm([shard["w_up0"], shard["w_down0"]]))
    w_up0, w_dn0 = g_up0.reshape(NCHIP, D, bw_up), g_dn0.reshape(FF, D)
    (hf0, u00, u0, act0, x2), (g_pool, g_up1, g_dn1) = _ffn_fwd(
        x1, nffn[0], w_up0, conv_ffn[0], w_dn0, "ffn0_fwd",
        comm=_gather_comm([shard["w_pool"], shard["w_up1"], shard["w_down1"]], early_forward=True))
    w_pool = g_pool.reshape(NCHIP, len(POOL_WINDOWS), PG // NCHIP, PG)
    w_up1, w_dn1 = g_up1.reshape(NCHIP, D, bw_up), g_dn1.reshape(FF, D)
    x3 = _pool_fwd(x2, whole["norm_mix_odd"], w_pool, whole["pool_scale"])
    (hf1, u01, u1, act1, x4), _ = _ffn_fwd(x3, nffn[1], w_up1, conv_ffn[1], w_dn1, "ffn1_fwd")
    dx4, g_nfin, loss_part = _final_loss(x4, rep["norm_final"], target)

    psum = lambda k, g, ld, tag="": _pair_sum(place, g, ld, "pair_sum_" + k + tag)
    tot = lambda k, g, ld, p, layer=0, nl=1, prev=None: _chip_sum(place, g, ld, p, layer, nl, "chip_sum_%s%d" % (k, layer), prev)
    pair, chips, join = _pair_comm, _chips_comm, _join_comm

    gr_dn1 = five(_wgrad(act1, dx4, 1, "wgrad_down1")[0], "w_down")
    (du01, g_wc1), _ = _ffn_bwd_a(dx4, u1, u01, conv_ffn[1], w_dn1, "ffn1_bwd_a")
    g_up, (ld_dn1,) = _wgrad(hf1, du01, NCHIP, "wgrad_up1", comm=pair([gr_dn1]))
    gr_up1 = five(g_up, "w_up")
    s_dn1 = psum("w_down", gr_dn1, ld_dn1, "1")
    (dx3, g_nf1), (p_dn1, ld_up1) = _nt_rms_bwd(du01, w_up1, x3, nffn[1], dx4, "ffn1_bwd_b",
                                                comm=join(chips([s_dn1]), pair([gr_up1])))
    s_up1 = psum("w_up", gr_up1, ld_up1, "1")
    (dx2, g_wpool, g_scale, g_nmo), _ = _pool_bwd(dx3, x2, whole["norm_mix_odd"], w_pool, whole["pool_scale"])
    gr_pool = five(g_wpool, "w_pool")
    (du00, g_wc0), (p_up1,) = _ffn_bwd_a(dx2, u0, u00, conv_ffn[0], w_dn0, "ffn0_bwd_a", comm=chips([s_up1]))
    gr_dn0 = five(_wgrad(act0, dx2, 1, "wgrad_down0")[0], "w_down")
    g_up, (ld_dn0, ld_pool) = _wgrad(hf0, du00, NCHIP, "wgrad_up0", comm=pair([gr_dn0, gr_pool]))
    gr_up0 = five(g_up, "w_up")
    s_dn0, s_pool = psum("w_down", gr_dn0, ld_dn0, "0"), psum("w_pool", gr_pool, ld_pool)
    (dx1, g_nf0), (p_dn0, p_pool, ld_up0) = _nt_rms_bwd(du00, w_up0, x1, nffn[0], dx2, "ffn0_bwd_b",
                                                        comm=join(chips([s_dn0, s_pool]), pair([gr_up0])))
    s_up0 = psum("w_up", gr_up0, ld_up0, "0")
    gr_out = five(_wgrad(cat, dx1, 1, "wgrad_out")[0], "w_out")
    (dz, g_ca, g_lg, g_lb, g_cb), (p_up0, ld_out) = _mix0_bwd_a(
        dx1, z, ac, bconv, w_out, whole["conv_a"], rep["ln_a_g"], rep["ln_a_b"], whole["conv_b"],
        comm=join(chips([s_up0]), pair([gr_out])))
    s_out = psum("w_out", gr_out, ld_out)
    t_pool = tot("w_pool", gr_pool, ld_pool, p_pool)
    t_up = tot("w_up", gr_up0, ld_up0, p_up0, 0, 2, tot("w_up", gr_up1, ld_up1, p_up1, 1, 2))
    t_dn = tot("w_down", gr_dn0, ld_dn0, p_dn0, 0, 2, tot("w_down", gr_dn1, ld_dn1, p_dn1, 1, 2))
    small = {"norm_mix_odd": g_nmo, "pool_scale": g_scale, "norm_ffn": jnp.concatenate([g_nf0, g_nf1], axis=0),
             "conv_ffn_w": jnp.stack([g_wc0, g_wc1]), "norm_final": g_nfin, "loss": loss_part,
             "conv_a": g_ca, "ln_a_g": g_lg, "ln_a_b": g_lb, "conv_b": g_cb}
    g_in, (p_out, early_all, t_pool, t_up, t_dn) = _wgrad(
        h0, dz, NCHIP, "wgrad_in",
        comm=join(chips([s_out], _pack([small[k] for k in SMALL_EARLY])), _swap_comm([t_pool, t_up, t_dn])))
    gr_in = five(g_in, "w_in")
    n_mm = x.shape[0] // min(TS_MM, x.shape[0])
    (dx_a, g_nme_a), (ld_in,) = _nt_rms_bwd(dz, w_in, x, rep["norm_mix_even"], dx1, "mix0_bwd_b0", comm=pair([gr_in]),
                                            tiles=(0, n_mm // 2))
    s_in = psum("w_in", gr_in, ld_in)
    (grad_x, g_nme_b), (p_in,) = _nt_rms_bwd(dz, w_in, x, rep["norm_mix_even"], dx1, "mix0_bwd_b1", comm=chips([s_in]),
                                             tiles=(n_mm // 2, n_mm - n_mm // 2), dx_so_far=dx_a)
    small["norm_mix_even"] = g_nme_a + g_nme_b
    (late_all,) = _run_comm(chips([], _pack([small[k] for k in SMALL_LATE])), "rs_last")
    t_in, t_out = _run_comm(_swap_comm([tot("w_in", gr_in, ld_in, p_in), tot("w_out", gr_out, ld_out, p_out)]), "rs_swap")
    done = {k: adamw(k, t) for k, t in (("w_up", t_up), ("w_down", t_dn), ("w_pool", t_pool), ("w_in", t_in), ("w_out", t_out))}
    summed = dict(zip(SMALL_EARLY, _unpack(_sum_devices(early_all, "sum_small_early"), [small[k].shape for k in SMALL_EARLY])))
    summed.update(zip(SMALL_LATE, _unpack(_sum_devices(late_all, "sum_small_late"), [small[k].shape for k in SMALL_LATE])))
    return grad_x, done, summed


def _adamw_math(w, g, m, v):
    m = ADAM_B1 * m + (1.0 - ADAM_B1) * g
    v = ADAM_B2 * v + (1.0 - ADAM_B2) * (g * g)
    m_hat = m / (1.0 - ADAM_B1 ** ADAM_STEP)
    v_hat = v / (1.0 - ADAM_B2 ** ADAM_STEP)
    return -ADAM_LR * (m_hat / (jnp.sqrt(v_hat) + ADAM_EPS) + ADAM_WD * w), m, v


def _adamw_big(w, g, m, v, tr, name, comm=None):
    nl, rows, cols = w.shape

    def body(w_ref, g_ref, m_ref, v_ref, g2_ref, d_ref, m2_ref, v2_ref):
        gv = g_ref[...]
        g2_ref[...] = gv
        d_ref[...], m2_ref[...], v2_ref[...] = _adamw_math(w_ref[...], gv, m_ref[...], v_ref[...])

    spec = pl.BlockSpec((None, tr, cols), lambda l, r: (l, r, 0))
    return _call(body, name=name, grid=(nl, rows // tr), comm=comm, args=(w, g, m, v), in_specs=[spec] * 4,
                 out_specs=[spec] * 4, out_shape=[jax.ShapeDtypeStruct(w.shape, F32)] * 4)


def _adamw_small(ws, gs, ms, vs):
    n = len(ws)

    def body(*refs):
        for p in range(n):
            w_ref, g_ref, m_ref, v_ref = (refs[q * n + p] for q in range(4))
            d_ref, m2_ref, v2_ref = (refs[(4 + q) * n + p] for q in range(3))
            d_ref[...], m2_ref[...], v2_ref[...] = _adamw_math(w_ref[...], g_ref[...], m_ref[...], v_ref[...])

    whole = lambda a: pl.BlockSpec(a.shape, lambda: (0,) * a.ndim)
    outs = pl.pallas_call(
        body, name="adamw_small", in_specs=[whole(a) for a in ws] * 4, out_specs=[whole(a) for a in ws] * 3,
        out_shape=[jax.ShapeDtypeStruct(a.shape, F32) for a in ws] * 3,
        compiler_params=pltpu.CompilerParams(vmem_limit_bytes=VMEM_LIMIT_BYTES),
    )(*ws, *gs, *ms, *vs)
    return outs[0:n], outs[n:2 * n], outs[2 * n:3 * n]


def _place():
    x, y, c = lax.axis_index("x"), lax.axis_index("y"), lax.axis_index("c")
    chips = [(x, 1 - y), (1 - x, y), (1 - x, 1 - y)]
    blocks = [2 * cx + cy for cx, cy in chips]
    return x, y, c, 2 * x + y, chips, blocks


def _gather_comm(shards, small=None, early_forward=False):
    na = len(shards)
    nk = NCHIP - 1
    ns = 0 if small is None else 1

    def copies(ins, outs, sems):
        ici_send, ici_recv, fwd_send, fwd_recv, own_send, own_recv = sems[:6]
        x, y, c, j, chips, blocks = _place()
        sib = (x, y, 1 - c)

        def ici(a, k, arrival):
            dst = outs[a].at[blocks[k], c] if arrival else outs[a].at[j, c]
            return pltpu.make_async_remote_copy(
                src_ref=dst if arrival else ins[a].at[c], dst_ref=dst, send_sem=ici_send.at[a * nk + k],
                recv_sem=ici_recv.at[a * nk + k], device_id=(*chips[k], c), device_id_type=MESH)

        def fwd(a, k, half):
            ref = outs[a].at[blocks[k], half]
            return pltpu.make_async_remote_copy(
                src_ref=ref, dst_ref=ref, send_sem=fwd_send.at[a * nk + k], recv_sem=fwd_recv.at[a * nk + k],
                device_id=sib, device_id_type=MESH)

        own = [pltpu.make_async_remote_copy(src_ref=ins[a], dst_ref=outs[a].at[j], send_sem=own_send.at[a],
                                            recv_sem=own_recv.at[a], device_id=sib, device_id_type=MESH) for a in range(na)]
        small_copies = [pltpu.make_async_remote_copy(
            src_ref=ins[na], dst_ref=outs[na].at[j], send_sem=ici_send.at[na * nk + k], recv_sem=ici_recv.at[na * nk + k],
            device_id=(*chips[k], c), device_id_type=MESH) for k in range(nk * ns)]
        local = [pltpu.make_async_copy(ins[na], outs[na].at[j], sems[6])] if ns else []
        return ici, fwd, own, small_copies, local, c

    def start(ins, outs, sems):
        ici, _, own, small_copies, local, _ = copies(ins, outs, sems)
        for cp in local + own + [ici(a, k, False) for a in range(na) for k in range(nk)] + small_copies:
            cp.start()

    def mid(ins, outs, sems):
        ici, fwd, _, _, _, c = copies(ins, outs, sems)
        for a in range(na):
            for k in range(nk):
                ici(a, k, True).wait_recv()
                fwd(a, k, c).start()

    def finish(ins, outs, sems):
        ici, fwd, own, small_copies, local, c = copies(ins, outs, sems)
        for cp in small_copies:
            cp.wait()
        for a in range(na):
            for k in range(nk):
                ici(a, k, False).wait_send()
                fwd(a, k, c).wait_send()
                fwd(a, k, 1 - c).wait_recv()
        for cp in own + local:
            cp.wait()

    out_shapes = [jax.ShapeDtypeStruct((NCHIP,) + s.shape, s.dtype) for s in shards]
    sems = [pltpu.SemaphoreType.DMA((na * nk + nk * ns,)), pltpu.SemaphoreType.DMA((na * nk + nk * ns,)),
            pltpu.SemaphoreType.DMA((na * nk,)), pltpu.SemaphoreType.DMA((na * nk,)),
            pltpu.SemaphoreType.DMA((na,)), pltpu.SemaphoreType.DMA((na,))]
    if ns:
        out_shapes.append(jax.ShapeDtypeStruct((NCHIP,) + small.shape, small.dtype))
        sems.append(pltpu.SemaphoreType.DMA)
    if early_forward:
        return _Comm(list(shards) + [small] * ns, out_shapes, sems, start, finish, mid=mid)

    def forward_and_finish(ins, outs, sems):
        mid(ins, outs, sems)
        finish(ins, outs, sems)

    return _Comm(list(shards) + [small] * ns, out_shapes, sems, start, forward_and_finish)


def _simple_comm(inputs, out_shapes, make_copies, n_sems, aliases=None):
    def start(ins, outs, sems):
        for cp in make_copies(ins, outs, sems):
            cp.start()

    def finish(ins, outs, sems):
        for cp in make_copies(ins, outs, sems):
            cp.wait()

    return _Comm(inputs, out_shapes, [pltpu.SemaphoreType.DMA((n,)) for n in n_sems], start, finish, aliases)


def _pair_comm(grads):
    def make_copies(ins, outs, sems):
        x, y, c, _, _, _ = _place()
        return [pltpu.make_async_remote_copy(
            src_ref=ins[a].at[:, 1 - c], dst_ref=outs[a], send_sem=sems[0].at[a], recv_sem=sems[1].at[a],
            device_id=(x, y, 1 - c), device_id_type=MESH) for a in range(len(grads))]

    out_shapes = [jax.ShapeDtypeStruct(g.shape[:1] + g.shape[2:], F32) for g in grads]
    return _simple_comm(grads, out_shapes, make_copies, [len(grads)] * 2)


def _chips_comm(sums, small=None):
    na = len(sums)
    nk = NCHIP - 1

    def make_copies(ins, outs, sems):
        x, y, c, _, chips, blocks = _place()
        copies = [pltpu.make_async_remote_copy(
            src_ref=ins[a].at[blocks[k]], dst_ref=outs[a].at[k], send_sem=sems[0].at[a * nk + k],
            recv_sem=sems[1].at[a * nk + k], device_id=(*chips[k], c), device_id_type=MESH)
            for a in range(na) for k in range(nk)]
        if small is not None:
            me = 4 * x + 2 * y + c
            for r in range(1, NDEV):
                peer = (1 - x if r & 4 else x, 1 - y if r & 2 else y, 1 - c if r & 1 else c)
                copies.append(pltpu.make_async_remote_copy(
                    src_ref=ins[na], dst_ref=outs[na].at[me], send_sem=sems[2].at[r - 1], recv_sem=sems[3].at[r - 1],
                    device_id=peer, device_id_type=MESH))
            copies.append(pltpu.make_async_copy(ins[na], outs[na].at[me], sems[4].at[0]))
        return copies

    out_shapes = [jax.ShapeDtypeStruct((nk,) + g.shape[1:], BF16) for g in sums]
    chip_sems = [max(na * nk, 1)] * 2
    if small is None:
        return _simple_comm(sums, out_shapes, make_copies, chip_sems)
    out_shapes.append(jax.ShapeDtypeStruct((NDEV,) + small.shape, F32))
    return _simple_comm(list(sums) + [small], out_shapes, make_copies, chip_sems + [NDEV - 1] * 2 + [1])


def _swap_comm(totals):
    def make_copies(ins, outs, sems):
        x, y, c, _, _, _ = _place()
        return [pltpu.make_async_remote_copy(
            src_ref=outs[a].at[:, c], dst_ref=outs[a].at[:, c], send_sem=sems[0].at[a], recv_sem=sems[1].at[a],
            device_id=(x, y, 1 - c), device_id_type=MESH) for a in range(len(totals))]

    out_shapes = [jax.ShapeDtypeStruct(t.shape, F32) for t in totals]
    return _simple_comm(totals, out_shapes, make_copies, [len(totals)] * 2, aliases={a: a for a in range(len(totals))})


def _pair_sum(place, grad, landed, name):
    _, _, rows, cols = grad.shape

    def body(place_ref, g_ref, l_ref, o_ref):
        o_ref[...] = (g_ref[...] + l_ref[...]).astype(BF16)

    return pl.pallas_call(
        body, name=name,
        grid_spec=pltpu.PrefetchScalarGridSpec(
            num_scalar_prefetch=1, grid=(NCHIP,),
            in_specs=[pl.BlockSpec((None, None, rows, cols), lambda b, p: (b, p[1], 0, 0)),
                      pl.BlockSpec((None, rows, cols), lambda b, p: (b, 0, 0))],
            out_specs=pl.BlockSpec((None, rows, cols), lambda b, p: (b, 0, 0))),
        out_shape=jax.ShapeDtypeStruct((NCHIP, rows, cols), BF16), compiler_params=_cparams(1),
    )(place, grad, landed)


def _chip_sum(place, grad, landed, parts, layer, n_layers, name, prev=None):
    _, _, rows, cols = grad.shape

    def body(*refs):
        g_ref, l_ref, p_ref, o_ref = refs[1], refs[2], refs[3], refs[-1]
        tot = g_ref[...] + l_ref[...]
        for k in range(NCHIP - 1):
            tot = tot + p_ref[k].astype(F32)
        o_ref[...] = tot

    in_specs = [pl.BlockSpec((None, None, rows, cols), lambda i, p: (p[0], p[1], 0, 0)),
                pl.BlockSpec((None, rows, cols), lambda i, p: (p[0], 0, 0)),
                pl.BlockSpec((NCHIP - 1, rows, cols), lambda i, p: (0, 0, 0))]
    args = [place, grad, landed, parts]
    if prev is not None:
        in_specs.append(ANY)
        args.append(prev)
    return pl.pallas_call(
        body, name=name,
        grid_spec=pltpu.PrefetchScalarGridSpec(
            num_scalar_prefetch=1, grid=(1,), in_specs=in_specs,
            out_specs=pl.BlockSpec((None, None, rows, cols), lambda i, p: (layer, p[1], 0, 0))),
        out_shape=jax.ShapeDtypeStruct((n_layers, 2, rows, cols), F32),
        input_output_aliases={} if prev is None else {4: 0}, compiler_params=_cparams(1),
    )(*args)


def _sum_devices(parts, name):
    def body(p_ref, o_ref):
        tot = p_ref[0]
        for d in range(1, NDEV):
            tot = tot + p_ref[d]
        o_ref[...] = tot

    return pl.pallas_call(
        body, name=name, in_specs=[pl.BlockSpec(parts.shape, lambda: (0, 0, 0))],
        out_specs=pl.BlockSpec(parts.shape[1:], lambda: (0, 0)), out_shape=jax.ShapeDtypeStruct(parts.shape[1:], F32),
    )(parts)


def _pack(parts):
    rows = []
    for p in parts:
        p = p.reshape(-1, LANES)
        rows.append(jnp.pad(p, ((0, -p.shape[0] % SUBLANES), (0, 0))))
    return jnp.concatenate(rows, axis=0)


def _unpack(buf, shapes, lead=()):
    out, r0 = [], 0
    nl = len(lead)
    for shp in shapes:
        nrow = 1
        for d in shp:
            nrow *= d
        nrow //= LANES
        out.append(buf[(slice(None),) * nl + (slice(r0, r0 + nrow),)].reshape(lead + tuple(shp)))
        r0 += nrow + (-nrow % SUBLANES)
    return out


WEIGHT_ORDER = ("norm_mix_even", "w_in", "conv_a", "ln_a_g", "ln_a_b", "conv_b", "w_out", "norm_mix_odd", "w_pool",
                "pool_scale", "norm_ffn", "w_up", "conv_ffn_w", "w_down", "norm_final")
BIG = ("w_in", "w_out", "w_pool", "w_up", "w_down")
HALF = {"w_in": (D // 2, NZ // NCHIP), "w_out": (2 * A // NCHIP // 2, D), "w_pool": (PG // 2, PG),
        "w_up": (D // 2, FF2 // NCHIP), "w_down": (FF // NCHIP // 2, D)}
SMALL_SHARDED = ("conv_a", "conv_b", "conv_ffn_w", "norm_mix_odd", "pool_scale")
SMALL_ALL = ("norm_mix_even", "conv_a", "ln_a_g", "ln_a_b", "conv_b", "norm_mix_odd", "pool_scale", "norm_ffn", "conv_ffn_w",
             "norm_final")
SMALL_EARLY = ("norm_mix_odd", "pool_scale", "norm_ffn", "conv_ffn_w", "norm_final", "loss", "conv_a", "ln_a_g", "ln_a_b", "conv_b")
SMALL_LATE = ("norm_mix_even",)


def kernel(x, norm_mix_even, w_in, conv_a, ln_a_g, ln_a_b, conv_b, w_out, norm_mix_odd, w_pool, pool_scale, norm_ffn, w_up, conv_ffn_w, w_down, norm_final, loss_target, m_norm_mix_even, m_w_in, m_conv_a, m_ln_a_g, m_ln_a_b, m_conv_b, m_w_out, m_norm_mix_odd, m_w_pool, m_pool_scale, m_norm_ffn, m_w_up, m_conv_ffn_w, m_w_down, m_norm_final, v_norm_mix_even, v_w_in, v_conv_a, v_ln_a_g, v_ln_a_b, v_conv_b, v_w_out, v_norm_mix_odd, v_w_pool, v_pool_scale, v_norm_ffn, v_w_up, v_conv_ffn_w, v_w_down, v_norm_final):
    w = dict(norm_mix_even=norm_mix_even, w_in=w_in, conv_a=conv_a, ln_a_g=ln_a_g, ln_a_b=ln_a_b, conv_b=conv_b, w_out=w_out,
             norm_mix_odd=norm_mix_odd, w_pool=w_pool, pool_scale=pool_scale, norm_ffn=norm_ffn, w_up=w_up,
             conv_ffn_w=conv_ffn_w, w_down=w_down, norm_final=norm_final)
    m = dict(norm_mix_even=m_norm_mix_even, w_in=m_w_in, conv_a=m_conv_a, ln_a_g=m_ln_a_g, ln_a_b=m_ln_a_b, conv_b=m_conv_b,
             w_out=m_w_out, norm_mix_odd=m_norm_mix_odd, w_pool=m_w_pool, pool_scale=m_pool_scale, norm_ffn=m_norm_ffn,
             w_up=m_w_up, conv_ffn_w=m_conv_ffn_w, w_down=m_w_down, norm_final=m_norm_final)
    v = dict(norm_mix_even=v_norm_mix_even, w_in=v_w_in, conv_a=v_conv_a, ln_a_g=v_ln_a_g, ln_a_b=v_ln_a_b, conv_b=v_conv_b,
             w_out=v_w_out, norm_mix_odd=v_norm_mix_odd, w_pool=v_w_pool, pool_scale=v_pool_scale, norm_ffn=v_norm_ffn,
             w_up=v_w_up, conv_ffn_w=v_conv_ffn_w, w_down=v_w_down, norm_final=v_norm_final)
    chip = 2 * lax.axis_index("x") + lax.axis_index("y")
    place = jnp.stack([chip, lax.axis_index("c")]).astype(jnp.int32)

    half = lambda a, name: a.astype(BF16).reshape((2,) + HALF[name])
    shard = {"w_in": half(w_in[0], "w_in"), "w_out": half(w_out[0], "w_out")}
    small_shapes = [w[k].shape[-2:] if w[k].ndim == 3 and k != "conv_ffn_w" else (w[k].size // w[k].shape[-1], w[k].shape[-1])
                    for k in SMALL_SHARDED]
    rep = dict(norm_mix_even=norm_mix_even, ln_a_g=ln_a_g, ln_a_b=ln_a_b, norm_ffn=norm_ffn, norm_final=norm_final.reshape(1, D))
    rows_per_step = {"w_in": 512, "w_out": 256, "w_pool": 256, "w_up": 256, "w_down": 352}
    as3 = lambda a: a.reshape(a.shape[0], -1, a.shape[-1])

    def adamw(k, total):
        outs, _ = _adamw_big(as3(w[k]), as3(total), as3(m[k]), as3(v[k]), rows_per_step[k], "adamw_" + k)
        return [a.reshape(w[k].shape) for a in outs]

    grad_x, done, summed = _train_step(
        x[0], loss_target[0], place, shard, (w_up, w_down, w_pool[0]), rep, _pack([w[k] for k in SMALL_SHARDED]), small_shapes,
        adamw)
    loss = summed["loss"][0, 0]

    grad, delta, new_m, new_v = ({k: done[k][q] for k in BIG} for q in range(4))
    for k in SMALL_ALL:
        gsum = summed[k]
        if k in SMALL_SHARDED:
            cols = w[k].shape[-1]
            gsum = lax.dynamic_slice_in_dim(gsum, chip * cols, cols, axis=gsum.ndim - 1)
        grad[k] = gsum.reshape(w[k].shape)

    as2 = lambda a: a.reshape(1, -1) if a.ndim == 1 else a
    ds, ms, vs = _adamw_small(*[[as2(t[k]) for k in SMALL_ALL] for t in (w, grad, m, v)])
    for k, d2, m2, v2 in zip(SMALL_ALL, ds, ms, vs):
        delta[k], new_m[k], new_v[k] = (a.reshape(w[k].shape) for a in (d2, m2, v2))

    return (loss, grad_x[None], *[grad[k] for k in WEIGHT_ORDER], *[delta[k] for k in WEIGHT_ORDER],
            *[new_m[k] for k in WEIGHT_ORDER], *[new_v[k] for k in WEIGHT_ORDER])
```

```python
import functools

import jax
import jax.numpy as jnp
from jax import lax
from jax.experimental import pallas as pl
from jax.experimental.pallas import tpu as pltpu

F32, BF16 = jnp.float32, jnp.bfloat16

D = 1024
A = 512
NZ = 5 * A
FF = 2816
FF2 = 2 * FF
NCHIP = 4
NDEV = 8
K_A, K_S = 31, 3
POOL_WINDOWS = (2, 4, 8, 16)
PG = D // len(POOL_WINDOWS)
RMS_EPS, LN_EPS = 1e-6, 1e-5
ADAM_LR, ADAM_B1, ADAM_B2, ADAM_EPS, ADAM_WD, ADAM_STEP = 0.001, 0.9, 0.999, 1e-08, 0.01, 10

HALO_A, HALO_S, HALO_P = 32, 8, 16
SUBLANES = 8
LANES = 128
VMEM_LIMIT_BYTES = 56 * 1024 * 1024

TS_MIX = 512
TS_MIXB = 256
TS_FFN = 256
TS_POOL = 512
TS_MM = 512
R_CHUNK = 64
GATHER_MID_STEPS = {"mix0_fwd": 2, "ffn0_fwd": 7}

MESH = pl.DeviceIdType.MESH
ANY = pl.BlockSpec(memory_space=pl.ANY)
NT_DIMS = (((1,), (1,)), ((), ()))
TN_DIMS = (((0,), (0,)), ((), ()))


def _cparams(n_axes):
    return pltpu.CompilerParams(dimension_semantics=("arbitrary",) * n_axes, vmem_limit_bytes=VMEM_LIMIT_BYTES)


def _const(shape):
    nd = len(shape)
    return pl.BlockSpec(shape, lambda *_: (0,) * nd, pipeline_mode=pl.Buffered(1))


def _sigmoid(v):
    return 1.0 / (1.0 + jnp.exp(-v))


def _rsqrt_mean_sq(x):
    return lax.rsqrt(jnp.mean(x * x, axis=-1, keepdims=True) + RMS_EPS)


def _rms_bwd(dh, xh, r, g):
    dxh = dh * g
    return r * (dxh - xh * jnp.mean(dxh * xh, axis=-1, keepdims=True))


def _shifted(buf_ref, row0, rows, col0, width, offsets):
    lo = (min(offsets) // SUBLANES) * SUBLANES
    hi = -(-(max(offsets) + rows) // SUBLANES) * SUBLANES
    start = row0 + lo if isinstance(row0, int) else pl.multiple_of(row0 + lo, SUBLANES)
    win = buf_ref[pl.ds(start, hi - lo), col0:col0 + width]
    out = {}
    for res in sorted({(o - lo) % SUBLANES for o in offsets}):
        qs = {o: (o - lo) // SUBLANES for o in offsets if (o - lo) % SUBLANES == res}
        base = pltpu.roll(win, hi - lo - res, 0) if res else win
        for o, q in qs.items():
            out[o] = base[SUBLANES * q:SUBLANES * q + rows, :]
    return out


def _rowsum8(v):
    acc = v[0:SUBLANES, :]
    for r in range(SUBLANES, v.shape[0], SUBLANES):
        acc = acc + v[r:r + SUBLANES, :]
    return acc


def _taps(w_ref, sh, offsets, col0, width):
    acc = None
    for k, o in enumerate(offsets):
        term = w_ref[k:k + 1, col0:col0 + width] * sh[o]
        acc = term if acc is None else acc + term
    return acc


def _window_bases(buf_ref, row0, rows, col0, width, offsets):
    lo = (min(offsets) // SUBLANES) * SUBLANES
    hi = -(-(max(offsets) + rows) // SUBLANES) * SUBLANES
    start = row0 + lo if isinstance(row0, int) else pl.multiple_of(row0 + lo, SUBLANES)
    win = buf_ref[pl.ds(start, hi - lo), col0:col0 + width]
    for res in sorted({(o - lo) % SUBLANES for o in offsets}):
        taps = [(k, (o - lo) // SUBLANES * SUBLANES) for k, o in enumerate(offsets) if (o - lo) % SUBLANES == res]
        yield (pltpu.roll(win, hi - lo - res, 0) if res else win), taps


def _conv_acc(buf_ref, w_ref, row0, rows, col0, width, offsets):
    acc = None
    for base, taps in _window_bases(buf_ref, row0, rows, col0, width, offsets):
        for k, q in taps:
            term = w_ref[k:k + 1, col0:col0 + width] * base[q:q + rows, :]
            acc = term if acc is None else acc + term
    return acc


def _conv_corr(buf_ref, w_ref, other, acc_ref, row0, rows, col0, width, offsets):
    acc = None
    for base, taps in _window_bases(buf_ref, row0, rows, col0, width, offsets):
        for k, q in taps:
            sl = base[q:q + rows, :]
            term = w_ref[k:k + 1, col0:col0 + width] * sl
            acc = term if acc is None else acc + term
            acc_ref[SUBLANES * k:SUBLANES * (k + 1), col0:col0 + width] += _rowsum8(sl * other)
    return acc


def _finish_tap_sums(acc_ref, out_ref, n_taps):
    for k in range(n_taps):
        out_ref[k:k + 1, :] = jnp.sum(acc_ref[SUBLANES * k:SUBLANES * (k + 1), :], axis=0, keepdims=True)


class _Comm:
    def __init__(self, inputs, out_shapes, sems, start, finish, aliases=None, mid=None, mid_steps=1):
        self.inputs, self.out_shapes, self.sems = list(inputs), list(out_shapes), list(sems)
        self.start, self.finish, self.mid, self.aliases = start, finish, mid, dict(aliases or {})
        self.mid_steps = mid_steps


def _join_comm(a, b):
    ni, no, ns = len(a.inputs), len(a.out_shapes), len(a.sems)

    def both(name):
        def run(ins, outs, sems):
            if getattr(a, name) is not None:
                getattr(a, name)(ins[:ni], outs[:no], sems[:ns])
            if getattr(b, name) is not None:
                getattr(b, name)(ins[ni:], outs[no:], sems[ns:])
        return run

    aliases = {**a.aliases, **{ni + i: no + o for i, o in b.aliases.items()}}
    mid = both("mid") if (a.mid is not None or b.mid is not None) else None
    return _Comm(a.inputs + b.inputs, a.out_shapes + b.out_shapes, a.sems + b.sems, both("start"), both("finish"), aliases, mid)


def _call(body, *, name, grid, in_specs, out_specs, out_shape, args, scratch=(), comm=None, aliases=None):
    n_in, n_out, n_scr, n_axes = len(in_specs), len(out_specs), len(scratch), len(grid)
    params = pltpu.CompilerParams(dimension_semantics=("arbitrary",) * n_axes, vmem_limit_bytes=VMEM_LIMIT_BYTES)
    aliases = dict(aliases or {})
    if comm is None:
        outs = pl.pallas_call(body, name=name, grid=grid, in_specs=list(in_specs), out_specs=list(out_specs),
                              out_shape=list(out_shape), scratch_shapes=list(scratch), input_output_aliases=aliases,
                              compiler_params=params)(*args)
        return list(outs), []
    ci, co = len(comm.inputs), len(comm.out_shapes)

    def wrapped(*refs):
        k_in, c_in = refs[:n_in], refs[n_in:n_in + ci]
        o0 = n_in + ci
        k_out, c_out = refs[o0:o0 + n_out], refs[o0 + n_out:o0 + n_out + co]
        s0 = o0 + n_out + co
        k_scr, c_sem = refs[s0:s0 + n_scr], refs[s0 + n_scr:]
        first = pl.program_id(0) == 0
        last = pl.program_id(0) == grid[0] - 1
        for ax in range(1, n_axes):
            first = jnp.logical_and(first, pl.program_id(ax) == 0)
            last = jnp.logical_and(last, pl.program_id(ax) == grid[ax] - 1)

        @pl.when(first)
        def _():
            comm.start(c_in, c_out, c_sem)

        mid_early = comm.mid is not None and n_axes == 1 and grid[0] > comm.mid_steps
        if mid_early:
            @pl.when(pl.program_id(0) == grid[0] - 1 - comm.mid_steps)
            def _():
                comm.mid(c_in, c_out, c_sem)

        body(*k_in, *k_out, *k_scr)

        @pl.when(last)
        def _():
            if comm.mid is not None and not mid_early:
                comm.mid(c_in, c_out, c_sem)
            comm.finish(c_in, c_out, c_sem)

    outs = pl.pallas_call(
        wrapped, name=name, grid=grid, in_specs=list(in_specs) + [ANY] * ci, out_specs=list(out_specs) + [ANY] * co,
        out_shape=list(out_shape) + comm.out_shapes, scratch_shapes=list(scratch) + comm.sems,
        input_output_aliases={**aliases, **{n_in + i: n_out + o for i, o in comm.aliases.items()}}, compiler_params=params,
    )(*args, *comm.inputs)
    return list(outs[:n_out]), list(outs[n_out:])


def _run_comm(comm, name):
    ci, co = len(comm.inputs), len(comm.out_shapes)

    def body(*refs):
        c_in, c_out, c_sem = refs[:ci], refs[ci:ci + co], refs[ci + co:]
        comm.start(c_in, c_out, c_sem)
        if comm.mid is not None:
            comm.mid(c_in, c_out, c_sem)
        comm.finish(c_in, c_out, c_sem)

    return list(pl.pallas_call(body, name=name, in_specs=[ANY] * ci, out_specs=[ANY] * co, out_shape=comm.out_shapes,
                               scratch_shapes=comm.sems, input_output_aliases=comm.aliases)(*comm.inputs))


def _mix0_fwd(x, g, w_in, conv_a, ln_g, ln_b, conv_b, w_out, comm=None):
    s = x.shape[0]
    ts = min(TS_MIX, s)
    n = s // ts
    bw = NZ // NCHIP
    offs_a = [HALO_A - (K_A - 1) + k for k in range(K_A)]
    offs_s = [HALO_S - (K_S - 1) + k for k in range(K_S)]

    def body(x_ref, g_ref, win_ref, ca_ref, lg_ref, lb_ref, cb_ref, wout_ref,
             h_ref, z_ref, ac_ref, bconv_buf, cat_ref, x1_ref, glu_buf, cv_buf):
        i = pl.program_id(0)

        @pl.when(i == 0)
        def _():
            glu_buf[0:HALO_A, :] = jnp.zeros((HALO_A, A), F32)
            cv_buf[0:HALO_S, :] = jnp.zeros((HALO_S, A), F32)

        xv = x_ref[...]
        h = (xv * _rsqrt_mean_sq(xv) * g_ref[...]).astype(BF16)
        h_ref[...] = h
        for j in range(NCHIP):
            z_ref[:, j * bw:(j + 1) * bw] = jnp.dot(h, win_ref[j], preferred_element_type=F32)
        glu_buf[HALO_A:HALO_A + ts, :] = z_ref[:, 0:A] * _sigmoid(z_ref[:, A:2 * A])
        cv_buf[HALO_S:HALO_S + ts, :] = z_ref[:, 3 * A:4 * A] * z_ref[:, 4 * A:5 * A]

        def chunk(ci, carry):
            r0 = pl.multiple_of(ci * R_CHUNK, R_CHUNK)
            for c0 in range(0, A, LANES):
                ac_ref[pl.ds(r0, R_CHUNK), c0:c0 + LANES] = _conv_acc(glu_buf, ca_ref, r0, R_CHUNK, c0, LANES, offs_a)
                bconv_buf[pl.ds(r0, R_CHUNK), c0:c0 + LANES] = _conv_acc(cv_buf, cb_ref, r0, R_CHUNK, c0, LANES, offs_s)
            return carry

        lax.fori_loop(0, ts // R_CHUNK, chunk, 0)
        glu_buf[0:HALO_A, :] = glu_buf[ts:ts + HALO_A, :]
        cv_buf[0:HALO_S, :] = cv_buf[ts:ts + HALO_S, :]

        ac = ac_ref[...]
        xc = ac - jnp.mean(ac, axis=-1, keepdims=True)
        xn = xc * lax.rsqrt(jnp.mean(xc * xc, axis=-1, keepdims=True) + LN_EPS)
        ln = xn * lg_ref[...] + lb_ref[...]
        cat_ref[:, 0:A] = (ln * _sigmoid(ln)).astype(BF16)
        cat_ref[:, A:2 * A] = (z_ref[:, 2 * A:3 * A] * bconv_buf[...]).astype(BF16)
        x1_ref[...] = xv + jnp.dot(cat_ref[...], wout_ref[...], preferred_element_type=F32)

    tile = lambda w: pl.BlockSpec((ts, w), lambda i: (i, 0))
    return _call(
        body, name="mix0_fwd", grid=(n,), comm=comm, args=(x, g, w_in, conv_a, ln_g, ln_b, conv_b, w_out),
        in_specs=[tile(D), _const((1, D)), _const((NCHIP, D, bw)), _const((K_A, A)), _const((1, A)), _const((1, A)),
                  _const((K_S, A)), _const((2 * A, D))],
        out_specs=[tile(D), tile(NZ), tile(A), tile(A), tile(2 * A), tile(D)],
        out_shape=[jax.ShapeDtypeStruct((s, D), BF16), jax.ShapeDtypeStruct((s, NZ), F32), jax.ShapeDtypeStruct((s, A), F32),
                   jax.ShapeDtypeStruct((s, A), F32), jax.ShapeDtypeStruct((s, 2 * A), BF16), jax.ShapeDtypeStruct((s, D), F32)],
        scratch=[pltpu.VMEM((HALO_A + ts, A), F32), pltpu.VMEM((HALO_S + ts, A), F32)])


def _ffn_fwd(x, g, w_up, wc, w_down, name, comm=None):
    s = x.shape[0]
    ts = min(TS_FFN, s)
    n = s // ts
    bw = FF2 // NCHIP
    rows = 32
    offs = [HALO_S - (K_S - 1) + k for k in range(K_S)]

    def body(x_ref, g_ref, wup_ref, wc_ref, wdn_ref, h_ref, u0_ref, u_ref, act_ref, xo_ref, cbuf):
        i = pl.program_id(0)

        @pl.when(i == 0)
        def _():
            cbuf[0:HALO_S, :] = jnp.zeros((HALO_S, FF2), F32)

        xv = x_ref[...]
        h = (xv * _rsqrt_mean_sq(xv) * g_ref[...]).astype(BF16)
        h_ref[...] = h
        f = None
        for p in range(NCHIP // 2):
            for j in (p, NCHIP // 2 + p):
                zc = jnp.dot(h, wup_ref[j], preferred_element_type=F32)
                u0_ref[:, j * bw:(j + 1) * bw] = zc.astype(BF16)
                cbuf[HALO_S:HALO_S + ts, j * bw:(j + 1) * bw] = zc
            for r0 in range(0, ts, rows):
                for c0 in range(p * bw, (p + 1) * bw, LANES):
                    ug = _taps(wc_ref, _shifted(cbuf, r0, rows, c0, LANES, offs), offs, c0, LANES)
                    uv = _taps(wc_ref, _shifted(cbuf, r0, rows, FF + c0, LANES, offs), offs, FF + c0, LANES)
                    u_ref[r0:r0 + rows, c0:c0 + LANES] = ug
                    u_ref[r0:r0 + rows, FF + c0:FF + c0 + LANES] = uv
                    act_ref[r0:r0 + rows, c0:c0 + LANES] = (ug * _sigmoid(ug) * uv).astype(BF16)
            fp = jnp.dot(act_ref[:, p * bw:(p + 1) * bw], wdn_ref[p * bw:(p + 1) * bw, :], preferred_element_type=F32)
            f = fp if f is None else f + fp
        cbuf[0:HALO_S, :] = cbuf[ts:ts + HALO_S, :]
        xo_ref[...] = xv + f

    tile = lambda w: pl.BlockSpec((ts, w), lambda i: (i, 0))
    return _call(
        body, name=name, grid=(n,), comm=comm, args=(x, g, w_up, wc, w_down),
        in_specs=[tile(D), _const((1, D)), _const((NCHIP, D, bw)), _const((K_S, FF2)), _const((FF, D))],
        out_specs=[tile(D), tile(FF2), tile(FF2), tile(FF), tile(D)],
        out_shape=[jax.ShapeDtypeStruct((s, D), BF16), jax.ShapeDtypeStruct((s, FF2), BF16), jax.ShapeDtypeStruct((s, FF2), F32),
                   jax.ShapeDtypeStruct((s, FF), BF16), jax.ShapeDtypeStruct((s, D), F32)],
        scratch=[pltpu.VMEM((HALO_S + ts, FF2), F32)])


def _pool_windows(hbuf, pbuf, tile_row0, ts):
    def chunk(ci, carry):
        r0 = pl.multiple_of(ci * R_CHUNK, R_CHUNK)
        t1 = (tile_row0 + r0 + lax.broadcasted_iota(jnp.int32, (R_CHUNK, 1), 0) + 1).astype(F32)
        for gi, w in enumerate(POOL_WINDOWS):
            cnt = jnp.minimum(t1, float(w))
            offs = [HALO_P - jj for jj in range(w)]
            for c0 in range(gi * PG, (gi + 1) * PG, LANES):
                sh = _shifted(hbuf, r0, R_CHUNK, c0, LANES, offs)
                tot = sh[offs[0]]
                for o in offs[1:]:
                    tot = tot + sh[o]
                pbuf[pl.ds(r0, R_CHUNK), c0:c0 + LANES] = (tot / cnt - sh[HALO_P]).astype(BF16)
        return carry

    lax.fori_loop(0, ts // R_CHUNK, chunk, 0)


def _assemble_wpool(wp_ref, wps):
    rb = PG // NCHIP
    for gi in range(len(POOL_WINDOWS)):
        for j in range(NCHIP):
            wps[gi, j * rb:(j + 1) * rb, :] = wp_ref[j, gi]


def _pool_fwd(x, g, w_pool, scale):
    s = x.shape[0]
    ts = min(TS_POOL, s)
    n = s // ts
    ng = len(POOL_WINDOWS)

    def body(x_ref, g_ref, wp_ref, sc_ref, xo_ref, hbuf, pbuf, wps):
        i = pl.program_id(0)

        @pl.when(i == 0)
        def _():
            hbuf[0:HALO_P, :] = jnp.zeros((HALO_P, D), F32)
            _assemble_wpool(wp_ref, wps)

        xv = x_ref[...]
        hbuf[HALO_P:HALO_P + ts, :] = xv * _rsqrt_mean_sq(xv) * g_ref[...]
        _pool_windows(hbuf, pbuf, i * ts, ts)
        hbuf[0:HALO_P, :] = hbuf[ts:ts + HALO_P, :]
        for gi in range(ng):
            cols = slice(gi * PG, (gi + 1) * PG)
            y = jnp.dot(pbuf[:, cols], wps[gi], preferred_element_type=F32)
            xo_ref[:, cols] = xv[:, cols] + y * sc_ref[:, cols]

    tile = pl.BlockSpec((ts, D), lambda i: (i, 0))
    return pl.pallas_call(
        body, name="pool_fwd", grid=(n,),
        in_specs=[tile, _const((1, D)), _const((NCHIP, ng, PG // NCHIP, PG)), _const((1, D))],
        out_specs=tile, out_shape=jax.ShapeDtypeStruct((s, D), F32),
        scratch_shapes=[pltpu.VMEM((HALO_P + ts, D), F32), pltpu.VMEM((ts, D), BF16), pltpu.VMEM((ng, PG, PG), BF16)],
        compiler_params=_cparams(1),
    )(x, g, w_pool, scale)


def _final_loss(x, g, target):
    s = x.shape[0]
    ts = min(TS_MM, s)
    n = s // ts

    def body(x_ref, g_ref, t_ref, dx_ref, dg_ref, loss_ref):
        i = pl.program_id(0)

        @pl.when(i == 0)
        def _():
            dg_ref[...] = jnp.zeros((1, D), F32)
            loss_ref[...] = jnp.zeros((1, LANES), F32)

        xv = x_ref[...]
        r = _rsqrt_mean_sq(xv)
        xh = xv * r
        gv = g_ref[...]
        err = xh * gv - t_ref[...]
        sq = jnp.sum(jnp.sum(err * err, axis=1, keepdims=True), axis=0, keepdims=True)
        loss_ref[...] += sq * (0.5 / D)
        dy = err * (1.0 / D)
        dg_ref[...] += jnp.sum(dy * xh, axis=0, keepdims=True)
        dx_ref[...] = _rms_bwd(dy, xh, r, gv)

    tile = pl.BlockSpec((ts, D), lambda i: (i, 0))
    return pl.pallas_call(
        body, name="final_loss", grid=(n,),
        in_specs=[tile, _const((1, D)), tile],
        out_specs=[tile, pl.BlockSpec((1, D), lambda i: (0, 0)), pl.BlockSpec((1, LANES), lambda i: (0, 0))],
        out_shape=[jax.ShapeDtypeStruct((s, D), F32), jax.ShapeDtypeStruct((1, D), F32), jax.ShapeDtypeStruct((1, LANES), F32)],
        compiler_params=_cparams(1),
    )(x, g, target)


def _ffn_bwd_a(dxo, u, u0, wc, w_down, name, comm=None):
    s = dxo.shape[0]
    ts = min(TS_FFN, s)
    n = s // ts
    cw = FF2 // NCHIP
    rows = 32
    lw2 = 2 * LANES
    boffs = [K_S - 1 - k for k in range(K_S)]

    def body(dxo_ref, u_ref, u0_ref, wc_ref, wdn_ref, du0_ref, dwc_ref, dubuf, dact, dwacc):
        i = pl.program_id(0)

        @pl.when(i == 0)
        def _():
            dubuf[ts:ts + HALO_S, :] = jnp.zeros((HALO_S, FF2), F32)
            dwacc[...] = jnp.zeros(dwacc.shape, F32)

        df = dxo_ref[...].astype(BF16)
        for cg in range(0, FF, cw):
            dact[...] = lax.dot_general(df, wdn_ref[cg:cg + cw, :], NT_DIMS, preferred_element_type=F32)

            def chunk(ci, carry, cg=cg):
                rs = pl.ds(pl.multiple_of(ci * rows, rows), rows)
                for c in range(cg, cg + cw, LANES):
                    ug, uv = u_ref[rs, c:c + LANES], u_ref[rs, FF + c:FF + c + LANES]
                    sg = _sigmoid(ug)
                    da = dact[rs, c - cg:c - cg + LANES]
                    gs = ug * sg
                    dubuf[rs, c:c + LANES] = da * uv * (sg + gs * (1.0 - sg))
                    dubuf[rs, FF + c:FF + c + LANES] = da * gs
                return carry

            lax.fori_loop(0, ts // rows, chunk, 0)

        def chunk2(ci, carry):
            r0 = pl.multiple_of(ci * rows, rows)
            rs = pl.ds(r0, rows)
            for c in range(0, FF2, lw2):
                sh = _shifted(dubuf, r0, rows, c, lw2, boffs)
                du0_ref[rs, c:c + lw2] = _taps(wc_ref, sh, boffs, c, lw2).astype(BF16)
                u0v = u0_ref[rs, c:c + lw2].astype(F32)
                for k, o in enumerate(boffs):
                    dwacc[SUBLANES * k:SUBLANES * (k + 1), c:c + lw2] += _rowsum8(sh[o] * u0v)
            return carry

        lax.fori_loop(0, ts // rows, chunk2, 0)
        dubuf[ts:ts + HALO_S, :] = dubuf[0:HALO_S, :]

        @pl.when(i == n - 1)
        def _():
            _finish_tap_sums(dwacc, dwc_ref, K_S)

    rev = lambda w: pl.BlockSpec((ts, w), lambda i: (n - 1 - i, 0))
    return _call(
        body, name=name, grid=(n,), comm=comm, args=(dxo, u, u0, wc, w_down),
        in_specs=[rev(D), rev(FF2), rev(FF2), _const((K_S, FF2)), _const((FF, D))],
        out_specs=[rev(FF2), pl.BlockSpec((K_S, FF2), lambda i: (0, 0))],
        out_shape=[jax.ShapeDtypeStruct((s, FF2), BF16), jax.ShapeDtypeStruct((K_S, FF2), F32)],
        scratch=[pltpu.VMEM((ts + HALO_S, FF2), F32), pltpu.VMEM((ts, cw), F32), pltpu.VMEM((SUBLANES * K_S, FF2), F32)])


def _nt_rms_bwd(dy, w, x, g, dres, name, comm=None, tiles=None, dx_so_far=None):
    s = x.shape[0]
    ts = min(TS_MM, s)
    first, n = (0, s // ts) if tiles is None else tiles
    nw = dy.shape[1]
    bw = nw // NCHIP

    def body(dy_ref, w_ref, x_ref, g_ref, dres_ref, dx_ref, dg_ref):
        i = pl.program_id(0)

        @pl.when(i == 0)
        def _():
            dg_ref[...] = jnp.zeros((1, D), F32)

        dh = lax.dot_general(dy_ref[:, 0:bw], w_ref[0], NT_DIMS, preferred_element_type=F32)
        for j in range(1, NCHIP):
            dh = dh + lax.dot_general(dy_ref[:, j * bw:(j + 1) * bw], w_ref[j], NT_DIMS, preferred_element_type=F32)
        xv = x_ref[...]
        r = _rsqrt_mean_sq(xv)
        xh = xv * r
        dg_ref[...] += jnp.sum(dh * xh, axis=0, keepdims=True)
        dx_ref[...] = dres_ref[...] + _rms_bwd(dh, xh, r, g_ref[...])

    def body_with_alias(dy_ref, w_ref, x_ref, g_ref, dres_ref, _, dx_ref, dg_ref):
        body(dy_ref, w_ref, x_ref, g_ref, dres_ref, dx_ref, dg_ref)

    tile = lambda wd: pl.BlockSpec((ts, wd), lambda i: (first + i, 0))
    in_specs = [tile(nw), _const((NCHIP, D, bw)), tile(D), _const((1, D)), tile(D)]
    more = dx_so_far is not None
    return _call(
        body_with_alias if more else body, name=name, grid=(n,), comm=comm,
        args=(dy, w, x, g, dres) + ((dx_so_far,) if more else ()), in_specs=in_specs + [ANY] * more,
        out_specs=[tile(D), pl.BlockSpec((1, D), lambda i: (0, 0))], aliases={5: 0} if more else None,
        out_shape=[jax.ShapeDtypeStruct((s, D), F32), jax.ShapeDtypeStruct((1, D), F32)])


def _wgrad(a, b, n_blocks, name, comm=None):
    s, m = a.shape
    bw = b.shape[1] // n_blocks
    tk = min(TS_MM, s)

    def body(a_ref, b_ref, o_ref):
        @pl.when(pl.program_id(0) == 0)
        def _():
            o_ref[...] = jnp.zeros(o_ref.shape, F32)

        for j in range(n_blocks):
            o_ref[j] += lax.dot_general(a_ref[...], b_ref[:, j * bw:(j + 1) * bw].astype(BF16), TN_DIMS,
                                        preferred_element_type=F32)

    (out,), comm_out = _call(
        body, name=name, grid=(s // tk,), comm=comm, args=(a, b),
        in_specs=[pl.BlockSpec((tk, m), lambda k: (k, 0)), pl.BlockSpec((tk, b.shape[1]), lambda k: (k, 0))],
        out_specs=[pl.BlockSpec((n_blocks, m, bw), lambda k: (0, 0, 0))],
        out_shape=[jax.ShapeDtypeStruct((n_blocks, m, bw), F32)])
    return out, comm_out


def _pool_bwd(dxo, x, g, w_pool, scale, comm=None):
    s = x.shape[0]
    ts = min(TS_POOL, s)
    n = s // ts
    ng = len(POOL_WINDOWS)
    rb = PG // NCHIP

    def body(dxo_ref, x_ref, halo_ref, g_ref, wp_ref, sc_ref, dx_ref, dwp_ref, dsc_ref, dg_ref,
             hbuf, pbuf, qbuf, dhbuf, wps, dwacc):
        i = pl.program_id(0)
        j = n - 1 - i

        @pl.when(i == 0)
        def _():
            qbuf[ts:ts + HALO_P, :] = jnp.zeros((HALO_P, D), F32)
            dwacc[...] = jnp.zeros(dwacc.shape, F32)
            dsc_ref[...] = jnp.zeros((1, D), F32)
            dg_ref[...] = jnp.zeros((1, D), F32)
            _assemble_wpool(wp_ref, wps)

        gv = g_ref[...]
        xl = halo_ref[...]
        hbuf[0:HALO_P, :] = jnp.where(j == 0, 0.0, xl * _rsqrt_mean_sq(xl) * gv)
        xv = x_ref[...]
        r = _rsqrt_mean_sq(xv)
        xh = xv * r
        hbuf[HALO_P:HALO_P + ts, :] = xh * gv
        _pool_windows(hbuf, pbuf, j * ts, ts)

        dy = dxo_ref[...]
        t1 = (j * ts + lax.broadcasted_iota(jnp.int32, (ts, 1), 0) + 1).astype(F32)
        for gi, w in enumerate(POOL_WINDOWS):
            cols = slice(gi * PG, (gi + 1) * PG)
            p = pbuf[:, cols]
            y = jnp.dot(p, wps[gi], preferred_element_type=F32)
            dsc_ref[:, cols] += jnp.sum(dy[:, cols] * y, axis=0, keepdims=True)
            dq = (dy[:, cols] * sc_ref[:, cols]).astype(BF16)
            dwacc[gi] += lax.dot_general(p, dq, TN_DIMS, preferred_element_type=F32)
            dp = lax.dot_general(dq, wps[gi], NT_DIMS, preferred_element_type=F32)
            qbuf[0:ts, cols] = dp / jnp.minimum(t1, float(w))

        def chunk(ci, carry):
            r0 = pl.multiple_of(ci * R_CHUNK, R_CHUNK)
            tc = (j * ts + r0 + lax.broadcasted_iota(jnp.int32, (R_CHUNK, 1), 0) + 1).astype(F32)
            for gi, w in enumerate(POOL_WINDOWS):
                cnt = jnp.minimum(tc, float(w))
                offs = list(range(w))
                for c0 in range(gi * PG, (gi + 1) * PG, LANES):
                    sh = _shifted(qbuf, r0, R_CHUNK, c0, LANES, offs)
                    tot = sh[0]
                    for o in offs[1:]:
                        tot = tot + sh[o]
                    dhbuf[pl.ds(r0, R_CHUNK), c0:c0 + LANES] = tot - sh[0] * cnt
            return carry

        lax.fori_loop(0, ts // R_CHUNK, chunk, 0)
        qbuf[ts:ts + HALO_P, :] = qbuf[0:HALO_P, :]
        dh = dhbuf[...]
        dg_ref[...] += jnp.sum(dh * xh, axis=0, keepdims=True)
        dx_ref[...] = dy + _rms_bwd(dh, xh, r, gv)

        @pl.when(i == n - 1)
        def _():
            for gi in range(ng):
                for jj in range(NCHIP):
                    dwp_ref[jj, gi] = dwacc[gi, jj * rb:(jj + 1) * rb, :]

    rev = pl.BlockSpec((ts, D), lambda i: (n - 1 - i, 0))
    halo = pl.BlockSpec((HALO_P, D), lambda i: (jnp.maximum((n - 1 - i) * (ts // HALO_P) - 1, 0), 0))
    vec = pl.BlockSpec((1, D), lambda i: (0, 0))
    return _call(
        body, name="pool_bwd", grid=(n,), comm=comm, args=(dxo, x, x, g, w_pool, scale),
        in_specs=[rev, rev, halo, _const((1, D)), _const((NCHIP, ng, rb, PG)), _const((1, D))],
        out_specs=[rev, pl.BlockSpec((NCHIP, ng, rb, PG), lambda i: (0, 0, 0, 0)), vec, vec],
        out_shape=[jax.ShapeDtypeStruct((s, D), F32), jax.ShapeDtypeStruct((NCHIP, ng, rb, PG), F32),
                   jax.ShapeDtypeStruct((1, D), F32), jax.ShapeDtypeStruct((1, D), F32)],
        scratch=[pltpu.VMEM((HALO_P + ts, D), F32), pltpu.VMEM((ts, D), BF16), pltpu.VMEM((ts + HALO_P, D), F32),
                 pltpu.VMEM((ts, D), F32), pltpu.VMEM((ng, PG, PG), BF16), pltpu.VMEM((ng, PG, PG), F32)])


def _mix0_bwd_a(dx1, z, ac, bconv, w_out, conv_a, ln_g, ln_b, conv_b, comm=None):
    s = dx1.shape[0]
    ts = min(TS_MIXB, s)
    n = s // ts
    rows = 32
    boffs_a = [K_A - 1 - k for k in range(K_A)]
    boffs_s = [K_S - 1 - k for k in range(K_S)]

    def body(dx1_ref, z_ref, ac_ref, bconv_ref, wout_ref, ca_ref, lg_ref, lb_ref, cb_ref,
             dz_ref, dca_ref, dlg_ref, dlb_ref, dcb_ref, dac_buf, dbc_buf, dca_acc, dcb_acc):
        i = pl.program_id(0)

        @pl.when(i == 0)
        def _():
            dac_buf[ts:ts + HALO_A, :] = jnp.zeros((HALO_A, A), F32)
            dbc_buf[ts:ts + HALO_S, :] = jnp.zeros((HALO_S, A), F32)
            dca_acc[...] = jnp.zeros(dca_acc.shape, F32)
            dcb_acc[...] = jnp.zeros(dcb_acc.shape, F32)
            dlg_ref[...] = jnp.zeros((1, A), F32)
            dlb_ref[...] = jnp.zeros((1, A), F32)

        dcat = lax.dot_general(dx1_ref[...].astype(BF16), wout_ref[...], NT_DIMS, preferred_element_type=F32)
        db = dcat[:, A:2 * A]
        dz_ref[:, 2 * A:3 * A] = (db * bconv_ref[...]).astype(BF16)
        dbc_buf[0:ts, :] = db * z_ref[:, 2 * A:3 * A]
        ac = ac_ref[...]
        xc = ac - jnp.mean(ac, axis=-1, keepdims=True)
        rstd = lax.rsqrt(jnp.mean(xc * xc, axis=-1, keepdims=True) + LN_EPS)
        xn = xc * rstd
        lg = lg_ref[...]
        ln = xn * lg + lb_ref[...]
        sl = _sigmoid(ln)
        dln = dcat[:, 0:A] * sl * (1.0 + ln * (1.0 - sl))
        dlg_ref[...] += jnp.sum(dln * xn, axis=0, keepdims=True)
        dlb_ref[...] += jnp.sum(dln, axis=0, keepdims=True)
        dxn = dln * lg
        dac_buf[0:ts, :] = rstd * (dxn - jnp.mean(dxn, axis=-1, keepdims=True)
                                   - xn * jnp.mean(dxn * xn, axis=-1, keepdims=True))

        def chunk(ci, carry):
            r0 = pl.multiple_of(ci * rows, rows)
            rs = pl.ds(r0, rows)
            for c0 in range(0, A, LANES):
                col = lambda grp: slice(grp * A + c0, grp * A + c0 + LANES)
                a_val, sg = z_ref[rs, col(0)], _sigmoid(z_ref[rs, col(1)])
                dglu = _conv_corr(dac_buf, ca_ref, a_val * sg, dca_acc, r0, rows, c0, LANES, boffs_a)
                dz_ref[rs, col(0)] = (dglu * sg).astype(BF16)
                dz_ref[rs, col(1)] = (dglu * a_val * sg * (1.0 - sg)).astype(BF16)
                c_gate, bc_val = z_ref[rs, col(3)], z_ref[rs, col(4)]
                dcv = _conv_corr(dbc_buf, cb_ref, c_gate * bc_val, dcb_acc, r0, rows, c0, LANES, boffs_s)
                dz_ref[rs, col(3)] = (dcv * bc_val).astype(BF16)
                dz_ref[rs, col(4)] = (dcv * c_gate).astype(BF16)
            return carry

        lax.fori_loop(0, ts // rows, chunk, 0)
        dac_buf[ts:ts + HALO_A, :] = dac_buf[0:HALO_A, :]
        dbc_buf[ts:ts + HALO_S, :] = dbc_buf[0:HALO_S, :]

        @pl.when(i == n - 1)
        def _():
            _finish_tap_sums(dca_acc, dca_ref, K_A)
            _finish_tap_sums(dcb_acc, dcb_ref, K_S)

    rev = lambda w: pl.BlockSpec((ts, w), lambda i: (n - 1 - i, 0))
    full = lambda r, c: pl.BlockSpec((r, c), lambda i: (0, 0))
    return _call(
        body, name="mix0_bwd_a", grid=(n,), comm=comm, args=(dx1, z, ac, bconv, w_out, conv_a, ln_g, ln_b, conv_b),
        in_specs=[rev(D), rev(NZ), rev(A), rev(A), _const((2 * A, D)), _const((K_A, A)), _const((1, A)), _const((1, A)),
                  _const((K_S, A))],
        out_specs=[rev(NZ), full(K_A, A), full(1, A), full(1, A), full(K_S, A)],
        out_shape=[jax.ShapeDtypeStruct((s, NZ), BF16), jax.ShapeDtypeStruct((K_A, A), F32), jax.ShapeDtypeStruct((1, A), F32),
                   jax.ShapeDtypeStruct((1, A), F32), jax.ShapeDtypeStruct((K_S, A), F32)],
        scratch=[pltpu.VMEM((ts + HALO_A, A), F32), pltpu.VMEM((ts + HALO_S, A), F32), pltpu.VMEM((SUBLANES * K_A, A), F32),
                 pltpu.VMEM((SUBLANES * K_S, A), F32)])


def _cast_later_weights(w_up, w_down, w_pool, comm):
    nl = w_up.shape[0]

    def body(up_ref, dn_ref, pool_ref, *outs):
        up_o, dn_o, pool_o = outs[:nl], outs[nl:2 * nl], outs[2 * nl]
        for layer in range(nl):
            @pl.when(pl.program_id(0) == layer)
            def _(layer=layer):
                up_o[layer][...] = up_ref[...].astype(BF16)
                dn_o[layer][...] = dn_ref[...].astype(BF16)
                if layer == 0:
                    pool_o[...] = pool_ref[...].astype(BF16)

    per_layer = lambda a: pl.BlockSpec((None,) + a.shape[1:], lambda l: (l,) + (0,) * (a.ndim - 1))
    whole = lambda shape: pl.BlockSpec(shape, lambda l: (0,) * len(shape))
    out_shapes = [w_up.shape[1:]] * nl + [w_down.shape[1:]] * nl + [w_pool.shape]
    return _call(body, name="cast_later_weights", grid=(nl,), comm=comm, args=(w_up, w_down, w_pool),
                 in_specs=[per_layer(w_up), per_layer(w_down), whole(w_pool.shape)],
                 out_specs=[whole(shape) for shape in out_shapes],
                 out_shape=[jax.ShapeDtypeStruct(shape, BF16) for shape in out_shapes])


def _train_step(x, target, place, shard, later, rep, small_pack, small_shapes, adamw):
    bw_up, bw_in = FF2 // NCHIP, NZ // NCHIP
    five = lambda a, k: a.reshape(NCHIP, 2, *HALF[k])

    half = lambda a, name: a.reshape((2,) + HALF[name])
    (up0, up1, dn0, dn1, pool_bf), (g_in, g_out, small_g) = _cast_later_weights(
        *later, comm=_gather_comm([shard["w_in"], shard["w_out"]], small_pack))
    shard = dict(shard, w_pool=half(pool_bf, "w_pool"), w_up0=half(up0, "w_up"), w_up1=half(up1, "w_up"),
                 w_down0=half(dn0, "w_down"), w_down1=half(dn1, "w_down"))
    whole = {}
    for k, part in zip(SMALL_SHARDED, _unpack(small_g, small_shapes, lead=(NCHIP,))):
        whole[k] = jnp.moveaxis(part, 0, 1).reshape(part.shape[1], NCHIP * part.shape[2])
    w_in, w_out = g_in.reshape(NCHIP, D, bw_in), g_out.reshape(2 * A, D)
    conv_ffn = whole["conv_ffn_w"].reshape(2, K_S, FF2)
    nffn = [rep["norm_ffn"][0:1], rep["norm_ffn"][1:2]]

    (h0, z, ac, bconv, cat, x1), (g_up0, g_dn0) = _mix0_fwd(
        x, rep["norm_mix_even"], w_in, whole["conv_a"], rep["ln_a_g"], rep["ln_a_b"], whole["conv_b"], w_out,
        comm=_gather_comm([shard["w_up0"], shard["w_down0"]], mid_steps=GATHER_MID_STEPS["mix0_fwd"]))
    w_up0, w_dn0 = g_up0.reshape(NCHIP, D, bw_up), g_dn0.reshape(FF, D)
    (hf0, u00, u0, act0, x2), (g_pool, g_up1, g_dn1) = _ffn_fwd(
        x1, nffn[0], w_up0, conv_ffn[0], w_dn0, "ffn0_fwd",
        comm=_gather_comm([shard["w_pool"], shard["w_up1"], shard["w_down1"]], mid_steps=GATHER_MID_STEPS["ffn0_fwd"]))
    w_pool = g_pool.reshape(NCHIP, len(POOL_WINDOWS), PG // NCHIP, PG)
    w_up1, w_dn1 = g_up1.reshape(NCHIP, D, bw_up), g_dn1.reshape(FF, D)
    x3 = _pool_fwd(x2, whole["norm_mix_odd"], w_pool, whole["pool_scale"])
    (hf1, u01, u1, act1, x4), _ = _ffn_fwd(x3, nffn[1], w_up1, conv_ffn[1], w_dn1, "ffn1_fwd")
    dx4, g_nfin, loss_part = _final_loss(x4, rep["norm_final"], target)

    psum = lambda k, g, ld, tag="": _pair_sum(place, g, ld, "pair_sum_" + k + tag)
    tot = lambda k, g, ld, p, layer=0, nl=1, prev=None: _chip_sum(place, g, ld, p, layer, nl, "chip_sum_%s%d" % (k, layer), prev)
    pair, chips, join = _pair_comm, _chips_comm, _join_comm

    gr_dn1 = five(_wgrad(act1, dx4, 1, "wgrad_down1")[0], "w_down")
    (du01, g_wc1), _ = _ffn_bwd_a(dx4, u1, u01, conv_ffn[1], w_dn1, "ffn1_bwd_a")
    g_up, (ld_dn1,) = _wgrad(hf1, du01, NCHIP, "wgrad_up1", comm=pair([gr_dn1]))
    gr_up1 = five(g_up, "w_up")
    s_dn1 = psum("w_down", gr_dn1, ld_dn1, "1")
    (dx3, g_nf1), (p_dn1, ld_up1) = _nt_rms_bwd(du01, w_up1, x3, nffn[1], dx4, "ffn1_bwd_b",
                                                comm=join(chips([s_dn1]), pair([gr_up1])))
    s_up1 = psum("w_up", gr_up1, ld_up1, "1")
    (dx2, g_wpool, g_scale, g_nmo), _ = _pool_bwd(dx3, x2, whole["norm_mix_odd"], w_pool, whole["pool_scale"])
    gr_pool = five(g_wpool, "w_pool")
    (du00, g_wc0), (p_up1,) = _ffn_bwd_a(dx2, u0, u00, conv_ffn[0], w_dn0, "ffn0_bwd_a", comm=chips([s_up1]))
    gr_dn0 = five(_wgrad(act0, dx2, 1, "wgrad_down0")[0], "w_down")
    g_up, (ld_dn0, ld_pool) = _wgrad(hf0, du00, NCHIP, "wgrad_up0", comm=pair([gr_dn0, gr_pool]))
    gr_up0 = five(g_up, "w_up")
    s_dn0, s_pool = psum("w_down", gr_dn0, ld_dn0, "0"), psum("w_pool", gr_pool, ld_pool)
    (dx1, g_nf0), (p_dn0, p_pool, ld_up0) = _nt_rms_bwd(du00, w_up0, x1, nffn[0], dx2, "ffn0_bwd_b",
                                                        comm=join(chips([s_dn0, s_pool]), pair([gr_up0])))
    s_up0 = psum("w_up", gr_up0, ld_up0, "0")
    gr_out = five(_wgrad(cat, dx1, 1, "wgrad_out")[0], "w_out")
    (dz, g_ca, g_lg, g_lb, g_cb), (p_up0, ld_out) = _mix0_bwd_a(
        dx1, z, ac, bconv, w_out, whole["conv_a"], rep["ln_a_g"], rep["ln_a_b"], whole["conv_b"],
        comm=join(chips([s_up0]), pair([gr_out])))
    s_out = psum("w_out", gr_out, ld_out)
    t_pool = tot("w_pool", gr_pool, ld_pool, p_pool)
    t_up = tot("w_up", gr_up0, ld_up0, p_up0, 0, 2, tot("w_up", gr_up1, ld_up1, p_up1, 1, 2))
    t_dn = tot("w_down", gr_dn0, ld_dn0, p_dn0, 0, 2, tot("w_down", gr_dn1, ld_dn1, p_dn1, 1, 2))
    small = {"norm_mix_odd": g_nmo, "pool_scale": g_scale, "norm_ffn": jnp.concatenate([g_nf0, g_nf1], axis=0),
             "conv_ffn_w": jnp.stack([g_wc0, g_wc1]), "norm_final": g_nfin, "loss": loss_part,
             "conv_a": g_ca, "ln_a_g": g_lg, "ln_a_b": g_lb, "conv_b": g_cb}
    g_in, (p_out, early_all, t_pool, t_up, t_dn) = _wgrad(
        h0, dz, NCHIP, "wgrad_in",
        comm=join(chips([s_out], _pack([small[k] for k in SMALL_EARLY])), _swap_comm([t_pool, t_up, t_dn])))
    gr_in = five(g_in, "w_in")
    n_mm = x.shape[0] // min(TS_MM, x.shape[0])
    (dx_a, g_nme_a), (ld_in,) = _nt_rms_bwd(dz, w_in, x, rep["norm_mix_even"], dx1, "mix0_bwd_b0", comm=pair([gr_in]),
                                            tiles=(0, n_mm // 2))
    s_in = psum("w_in", gr_in, ld_in)
    (grad_x, g_nme_b), (p_in,) = _nt_rms_bwd(dz, w_in, x, rep["norm_mix_even"], dx1, "mix0_bwd_b1", comm=chips([s_in]),
                                             tiles=(n_mm // 2, n_mm - n_mm // 2), dx_so_far=dx_a)
    small["norm_mix_even"] = g_nme_a + g_nme_b
    (late_all,) = _run_comm(chips([], _pack([small[k] for k in SMALL_LATE])), "rs_last")
    t_in, t_out = _run_comm(_swap_comm([tot("w_in", gr_in, ld_in, p_in), tot("w_out", gr_out, ld_out, p_out)]), "rs_swap")
    done = {k: adamw(k, t) for k, t in (("w_up", t_up), ("w_down", t_dn), ("w_pool", t_pool), ("w_in", t_in), ("w_out", t_out))}
    summed = dict(zip(SMALL_EARLY, _unpack(_sum_devices(early_all, "sum_small_early"), [small[k].shape for k in SMALL_EARLY])))
    summed.update(zip(SMALL_LATE, _unpack(_sum_devices(late_all, "sum_small_late"), [small[k].shape for k in SMALL_LATE])))
    return grad_x, done, summed


def _adamw_math(w, g, m, v):
    m = ADAM_B1 * m + (1.0 - ADAM_B1) * g
    v = ADAM_B2 * v + (1.0 - ADAM_B2) * (g * g)
    m_hat = m / (1.0 - ADAM_B1 ** ADAM_STEP)
    v_hat = v / (1.0 - ADAM_B2 ** ADAM_STEP)
    return -ADAM_LR * (m_hat / (jnp.sqrt(v_hat) + ADAM_EPS) + ADAM_WD * w), m, v


def _adamw_big(w, g, m, v, tr, name, comm=None):
    nl, rows, cols = w.shape

    def body(w_ref, g_ref, m_ref, v_ref, g2_ref, d_ref, m2_ref, v2_ref):
        gv = g_ref[...]
        g2_ref[...] = gv
        d_ref[...], m2_ref[...], v2_ref[...] = _adamw_math(w_ref[...], gv, m_ref[...], v_ref[...])

    spec = pl.BlockSpec((None, tr, cols), lambda l, r: (l, r, 0))
    return _call(body, name=name, grid=(nl, rows // tr), comm=comm, args=(w, g, m, v), in_specs=[spec] * 4,
                 out_specs=[spec] * 4, out_shape=[jax.ShapeDtypeStruct(w.shape, F32)] * 4)


def _adamw_small(ws, gs, ms, vs):
    n = len(ws)

    def body(*refs):
        for p in range(n):
            w_ref, g_ref, m_ref, v_ref = (refs[q * n + p] for q in range(4))
            d_ref, m2_ref, v2_ref = (refs[(4 + q) * n + p] for q in range(3))
            d_ref[...], m2_ref[...], v2_ref[...] = _adamw_math(w_ref[...], g_ref[...], m_ref[...], v_ref[...])

    whole = lambda a: pl.BlockSpec(a.shape, lambda: (0,) * a.ndim)
    outs = pl.pallas_call(
        body, name="adamw_small", in_specs=[whole(a) for a in ws] * 4, out_specs=[whole(a) for a in ws] * 3,
        out_shape=[jax.ShapeDtypeStruct(a.shape, F32) for a in ws] * 3,
        compiler_params=pltpu.CompilerParams(vmem_limit_bytes=VMEM_LIMIT_BYTES),
    )(*ws, *gs, *ms, *vs)
    return outs[0:n], outs[n:2 * n], outs[2 * n:3 * n]


def _place():
    x, y, c = lax.axis_index("x"), lax.axis_index("y"), lax.axis_index("c")
    chips = [(x, 1 - y), (1 - x, y), (1 - x, 1 - y)]
    blocks = [2 * cx + cy for cx, cy in chips]
    return x, y, c, 2 * x + y, chips, blocks


def _gather_comm(shards, small=None, mid_steps=None):
    na = len(shards)
    ns = 0 if small is None else 1

    def copies(ins, outs, sems):
        h1_send, h1_recv, h2_send, h2_recv, f1_send, f1_recv, f2_send, f2_recv, own_send, own_recv = sems[:10]
        x, y, c, j, chips, blocks = _place()
        sib = (x, y, 1 - c)
        piece = lambda a, p: pl.ds(p * (shards[a].shape[1] // 2), shards[a].shape[1] // 2)

        def remote(src, dst, send, recv, q, to):
            return pltpu.make_async_remote_copy(src_ref=src, dst_ref=dst, send_sem=send.at[q], recv_sem=recv.at[q],
                                                device_id=to, device_id_type=MESH)

        def hop1(a, k, arrival):
            dst = outs[a].at[blocks[k], c] if arrival else outs[a].at[j, c]
            return remote(dst if arrival else ins[a].at[c], dst, h1_send, h1_recv, 2 * a + k, (*chips[k], c))

        def hop2(a, p, arrival):
            ref = outs[a].at[blocks[2] if arrival else blocks[p], c, piece(a, p)]
            return remote(ref, ref, h2_send, h2_recv, 2 * a + p, (*chips[1 - p], c))

        def fwd1(a, k, half):
            ref = outs[a].at[blocks[k], half]
            return remote(ref, ref, f1_send, f1_recv, 2 * a + k, sib)

        def fwd2(a, p, half):
            ref = outs[a].at[blocks[2], half, piece(a, p)]
            return remote(ref, ref, f2_send, f2_recv, 2 * a + p, sib)

        own = [remote(ins[a], outs[a].at[j], own_send, own_recv, a, sib) for a in range(na)]
        small_copies = [remote(ins[na], outs[na].at[j], sems[10], sems[11], k, (*chips[k], c)) for k in range(3 * ns)]
        local = [pltpu.make_async_copy(ins[na], outs[na].at[j], sems[12])] if ns else []
        return hop1, hop2, fwd1, fwd2, own, small_copies, local, c

    pairs = [(a, k) for a in range(na) for k in range(2)]

    def start(ins, outs, sems):
        hop1, _, _, _, own, small_copies, local, _ = copies(ins, outs, sems)
        for cp in local + own + [hop1(a, k, False) for a, k in pairs] + small_copies:
            cp.start()

    def mid(ins, outs, sems):
        hop1, hop2, fwd1, _, _, _, _, c = copies(ins, outs, sems)
        for a, k in pairs:
            hop1(a, k, True).wait_recv()
            hop2(a, k, False).start()
            fwd1(a, k, c).start()

    def finish(ins, outs, sems):
        hop1, hop2, fwd1, fwd2, own, small_copies, local, c = copies(ins, outs, sems)
        for a, p in pairs:
            hop2(a, p, True).wait_recv()
            fwd2(a, p, c).start()
        for cp in small_copies:
            cp.wait()
        for a, k in pairs:
            hop1(a, k, False).wait_send()
            hop2(a, k, False).wait_send()
            fwd1(a, k, c).wait_send()
            fwd1(a, k, 1 - c).wait_recv()
            fwd2(a, k, c).wait_send()
            fwd2(a, k, 1 - c).wait_recv()
        for cp in own + local:
            cp.wait()

    out_shapes = [jax.ShapeDtypeStruct((NCHIP,) + s.shape, s.dtype) for s in shards]
    sems = [pltpu.SemaphoreType.DMA((2 * na,))] * 8 + [pltpu.SemaphoreType.DMA((na,))] * 2
    if ns:
        out_shapes.append(jax.ShapeDtypeStruct((NCHIP,) + small.shape, small.dtype))
        sems += [pltpu.SemaphoreType.DMA((3,)), pltpu.SemaphoreType.DMA((3,)), pltpu.SemaphoreType.DMA]
    if mid_steps is not None:
        return _Comm(list(shards) + [small] * ns, out_shapes, sems, start, finish, mid=mid, mid_steps=mid_steps)

    def forward_and_finish(ins, outs, sems):
        mid(ins, outs, sems)
        finish(ins, outs, sems)

    return _Comm(list(shards) + [small] * ns, out_shapes, sems, start, forward_and_finish)


def _simple_comm(inputs, out_shapes, make_copies, n_sems, aliases=None):
    def start(ins, outs, sems):
        for cp in make_copies(ins, outs, sems):
            cp.start()

    def finish(ins, outs, sems):
        for cp in make_copies(ins, outs, sems):
            cp.wait()

    return _Comm(inputs, out_shapes, [pltpu.SemaphoreType.DMA((n,)) for n in n_sems], start, finish, aliases)


def _pair_comm(grads):
    def make_copies(ins, outs, sems):
        x, y, c, _, _, _ = _place()
        return [pltpu.make_async_remote_copy(
            src_ref=ins[a].at[:, 1 - c], dst_ref=outs[a], send_sem=sems[0].at[a], recv_sem=sems[1].at[a],
            device_id=(x, y, 1 - c), device_id_type=MESH) for a in range(len(grads))]

    out_shapes = [jax.ShapeDtypeStruct(g.shape[:1] + g.shape[2:], F32) for g in grads]
    return _simple_comm(grads, out_shapes, make_copies, [len(grads)] * 2)


def _chips_comm(sums, small=None):
    na = len(sums)
    nk = NCHIP - 1

    def make_copies(ins, outs, sems):
        x, y, c, _, chips, blocks = _place()
        copies = [pltpu.make_async_remote_copy(
            src_ref=ins[a].at[blocks[k]], dst_ref=outs[a].at[k], send_sem=sems[0].at[a * nk + k],
            recv_sem=sems[1].at[a * nk + k], device_id=(*chips[k], c), device_id_type=MESH)
            for a in range(na) for k in range(nk)]
        if small is not None:
            me = 4 * x + 2 * y + c
            for r in range(1, NDEV):
                peer = (1 - x if r & 4 else x, 1 - y if r & 2 else y, 1 - c if r & 1 else c)
                copies.append(pltpu.make_async_remote_copy(
                    src_ref=ins[na], dst_ref=outs[na].at[me], send_sem=sems[2].at[r - 1], recv_sem=sems[3].at[r - 1],
                    device_id=peer, device_id_type=MESH))
            copies.append(pltpu.make_async_copy(ins[na], outs[na].at[me], sems[4].at[0]))
        return copies

    out_shapes = [jax.ShapeDtypeStruct((nk,) + g.shape[1:], BF16) for g in sums]
    chip_sems = [max(na * nk, 1)] * 2
    if small is None:
        return _simple_comm(sums, out_shapes, make_copies, chip_sems)
    out_shapes.append(jax.ShapeDtypeStruct((NDEV,) + small.shape, F32))
    return _simple_comm(list(sums) + [small], out_shapes, make_copies, chip_sems + [NDEV - 1] * 2 + [1])


def _swap_comm(totals):
    def make_copies(ins, outs, sems):
        x, y, c, _, _, _ = _place()
        return [pltpu.make_async_remote_copy(
            src_ref=outs[a].at[:, c], dst_ref=outs[a].at[:, c], send_sem=sems[0].at[a], recv_sem=sems[1].at[a],
            device_id=(x, y, 1 - c), device_id_type=MESH) for a in range(len(totals))]

    out_shapes = [jax.ShapeDtypeStruct(t.shape, F32) for t in totals]
    return _simple_comm(totals, out_shapes, make_copies, [len(totals)] * 2, aliases={a: a for a in range(len(totals))})


def _pair_sum(place, grad, landed, name):
    _, _, rows, cols = grad.shape

    def body(place_ref, g_ref, l_ref, o_ref):
        o_ref[...] = (g_ref[...] + l_ref[...]).astype(BF16)

    return pl.pallas_call(
        body, name=name,
        grid_spec=pltpu.PrefetchScalarGridSpec(
            num_scalar_prefetch=1, grid=(NCHIP,),
            in_specs=[pl.BlockSpec((None, None, rows, cols), lambda b, p: (b, p[1], 0, 0)),
                      pl.BlockSpec((None, rows, cols), lambda b, p: (b, 0, 0))],
            out_specs=pl.BlockSpec((None, rows, cols), lambda b, p: (b, 0, 0))),
        out_shape=jax.ShapeDtypeStruct((NCHIP, rows, cols), BF16), compiler_params=_cparams(1),
    )(place, grad, landed)


def _chip_sum(place, grad, landed, parts, layer, n_layers, name, prev=None):
    _, _, rows, cols = grad.shape

    def body(*refs):
        g_ref, l_ref, p_ref, o_ref = refs[1], refs[2], refs[3], refs[-1]
        tot = g_ref[...] + l_ref[...]
        for k in range(NCHIP - 1):
            tot = tot + p_ref[k].astype(F32)
        o_ref[...] = tot

    in_specs = [pl.BlockSpec((None, None, rows, cols), lambda i, p: (p[0], p[1], 0, 0)),
                pl.BlockSpec((None, rows, cols), lambda i, p: (p[0], 0, 0)),
                pl.BlockSpec((NCHIP - 1, rows, cols), lambda i, p: (0, 0, 0))]
    args = [place, grad, landed, parts]
    if prev is not None:
        in_specs.append(ANY)
        args.append(prev)
    return pl.pallas_call(
        body, name=name,
        grid_spec=pltpu.PrefetchScalarGridSpec(
            num_scalar_prefetch=1, grid=(1,), in_specs=in_specs,
            out_specs=pl.BlockSpec((None, None, rows, cols), lambda i, p: (layer, p[1], 0, 0))),
        out_shape=jax.ShapeDtypeStruct((n_layers, 2, rows, cols), F32),
        input_output_aliases={} if prev is None else {4: 0}, compiler_params=_cparams(1),
    )(*args)


def _sum_devices(parts, name):
    def body(p_ref, o_ref):
        tot = p_ref[0]
        for d in range(1, NDEV):
            tot = tot + p_ref[d]
        o_ref[...] = tot

    return pl.pallas_call(
        body, name=name, in_specs=[pl.BlockSpec(parts.shape, lambda: (0, 0, 0))],
        out_specs=pl.BlockSpec(parts.shape[1:], lambda: (0, 0)), out_shape=jax.ShapeDtypeStruct(parts.shape[1:], F32),
    )(parts)


def _pack(parts):
    rows = []
    for p in parts:
        p = p.reshape(-1, LANES)
        rows.append(jnp.pad(p, ((0, -p.shape[0] % SUBLANES), (0, 0))))
    return jnp.concatenate(rows, axis=0)


def _unpack(buf, shapes, lead=()):
    out, r0 = [], 0
    nl = len(lead)
    for shp in shapes:
        nrow = 1
        for d in shp:
            nrow *= d
        nrow //= LANES
        out.append(buf[(slice(None),) * nl + (slice(r0, r0 + nrow),)].reshape(lead + tuple(shp)))
        r0 += nrow + (-nrow % SUBLANES)
    return out


WEIGHT_ORDER = ("norm_mix_even", "w_in", "conv_a", "ln_a_g", "ln_a_b", "conv_b", "w_out", "norm_mix_odd", "w_pool",
                "pool_scale", "norm_ffn", "w_up", "conv_ffn_w", "w_down", "norm_final")
BIG = ("w_in", "w_out", "w_pool", "w_up", "w_down")
HALF = {"w_in": (D // 2, NZ // NCHIP), "w_out": (2 * A // NCHIP // 2, D), "w_pool": (PG // 2, PG),
        "w_up": (D // 2, FF2 // NCHIP), "w_down": (FF // NCHIP // 2, D)}
SMALL_SHARDED = ("conv_a", "conv_b", "conv_ffn_w", "norm_mix_odd", "pool_scale")
SMALL_ALL = ("norm_mix_even", "conv_a", "ln_a_g", "ln_a_b", "conv_b", "norm_mix_odd", "pool_scale", "norm_ffn", "conv_ffn_w",
             "norm_final")
SMALL_EARLY = ("norm_mix_odd", "pool_scale", "norm_ffn", "conv_ffn_w", "norm_final", "loss", "conv_a", "ln_a_g", "ln_a_b", "conv_b")
SMALL_LATE = ("norm_mix_even",)


def kernel(x, norm_mix_even, w_in, conv_a, ln_a_g, ln_a_b, conv_b, w_out, norm_mix_odd, w_pool, pool_scale, norm_ffn, w_up, conv_ffn_w, w_down, norm_final, loss_target, m_norm_mix_even, m_w_in, m_conv_a, m_ln_a_g, m_ln_a_b, m_conv_b, m_w_out, m_norm_mix_odd, m_w_pool, m_pool_scale, m_norm_ffn, m_w_up, m_conv_ffn_w, m_w_down, m_norm_final, v_norm_mix_even, v_w_in, v_conv_a, v_ln_a_g, v_ln_a_b, v_conv_b, v_w_out, v_norm_mix_odd, v_w_pool, v_pool_scale, v_norm_ffn, v_w_up, v_conv_ffn_w, v_w_down, v_norm_final):
    w = dict(norm_mix_even=norm_mix_even, w_in=w_in, conv_a=conv_a, ln_a_g=ln_a_g, ln_a_b=ln_a_b, conv_b=conv_b, w_out=w_out,
             norm_mix_odd=norm_mix_odd, w_pool=w_pool, pool_scale=pool_scale, norm_ffn=norm_ffn, w_up=w_up,
             conv_ffn_w=conv_ffn_w, w_down=w_down, norm_final=norm_final)
    m = dict(norm_mix_even=m_norm_mix_even, w_in=m_w_in, conv_a=m_conv_a, ln_a_g=m_ln_a_g, ln_a_b=m_ln_a_b, conv_b=m_conv_b,
             w_out=m_w_out, norm_mix_odd=m_norm_mix_odd, w_pool=m_w_pool, pool_scale=m_pool_scale, norm_ffn=m_norm_ffn,
             w_up=m_w_up, conv_ffn_w=m_conv_ffn_w, w_down=m_w_down, norm_final=m_norm_final)
    v = dict(norm_mix_even=v_norm_mix_even, w_in=v_w_in, conv_a=v_conv_a, ln_a_g=v_ln_a_g, ln_a_b=v_ln_a_b, conv_b=v_conv_b,
             w_out=v_w_out, norm_mix_odd=v_norm_mix_odd, w_pool=v_w_pool, pool_scale=v_pool_scale, norm_ffn=v_norm_ffn,
             w_up=v_w_up, conv_ffn_w=v_conv_ffn_w, w_down=v_w_down, norm_final=v_norm_final)
    chip = 2 * lax.axis_index("x") + lax.axis_index("y")
    place = jnp.stack([chip, lax.axis_index("c")]).astype(jnp.int32)

    half = lambda a, name: a.astype(BF16).reshape((2,) + HALF[name])
    shard = {"w_in": half(w_in[0], "w_in"), "w_out": half(w_out[0], "w_out")}
    small_shapes = [w[k].shape[-2:] if w[k].ndim == 3 and k != "conv_ffn_w" else (w[k].size // w[k].shape[-1], w[k].shape[-1])
                    for k in SMALL_SHARDED]
    rep = dict(norm_mix_even=norm_mix_even, ln_a_g=ln_a_g, ln_a_b=ln_a_b, norm_ffn=norm_ffn, norm_final=norm_final.reshape(1, D))
    rows_per_step = {"w_in": 512, "w_out": 256, "w_pool": 256, "w_up": 256, "w_down": 352}
    as3 = lambda a: a.reshape(a.shape[0], -1, a.shape[-1])

    def adamw(k, total):
        outs, _ = _adamw_big(as3(w[k]), as3(total), as3(m[k]), as3(v[k]), rows_per_step[k], "adamw_" + k)
        return [a.reshape(w[k].shape) for a in outs]

    grad_x, done, summed = _train_step(
        x[0], loss_target[0], place, shard, (w_up, w_down, w_pool[0]), rep, _pack([w[k] for k in SMALL_SHARDED]), small_shapes,
        adamw)
    loss = summed["loss"][0, 0]

    grad, delta, new_m, new_v = ({k: done[k][q] for k in BIG} for q in range(4))
    for k in SMALL_ALL:
        gsum = summed[k]
        if k in SMALL_SHARDED:
            cols = w[k].shape[-1]
            gsum = lax.dynamic_slice_in_dim(gsum, chip * cols, cols, axis=gsum.ndim - 1)
        grad[k] = gsum.reshape(w[k].shape)

    as2 = lambda a: a.reshape(1, -1) if a.ndim == 1 else a
    ds, ms, vs = _adamw_small(*[[as2(t[k]) for k in SMALL_ALL] for t in (w, grad, m, v)])
    for k, d2, m2, v2 in zip(SMALL_ALL, ds, ms, vs):
        delta[k], new_m[k], new_v[k] = (a.reshape(w[k].shape) for a in (d2, m2, v2))

    return (loss, grad_x[None], *[grad[k] for k in WEIGHT_ORDER], *[delta[k] for k in WEIGHT_ORDER],
            *[new_m[k] for k in WEIGHT_ORDER], *[new_v[k] for k in WEIGHT_ORDER])
```

```python
import functools

import jax
import jax.numpy as jnp
from jax import lax
from jax.experimental import pallas as pl
from jax.experimental.pallas import tpu as pltpu

F32, BF16 = jnp.float32, jnp.bfloat16

D = 1024
A = 512
NZ = 5 * A
FF = 2816
FF2 = 2 * FF
NCHIP = 4
NDEV = 8
K_A, K_S = 31, 3
POOL_WINDOWS = (2, 4, 8, 16)
PG = D // len(POOL_WINDOWS)
RMS_EPS, LN_EPS = 1e-6, 1e-5
ADAM_LR, ADAM_B1, ADAM_B2, ADAM_EPS, ADAM_WD, ADAM_STEP = 0.001, 0.9, 0.999, 1e-08, 0.01, 10

HALO_A, HALO_S, HALO_P = 32, 8, 16
SUBLANES = 8
LANES = 128
VMEM_LIMIT_BYTES = 56 * 1024 * 1024

TS_MIX = 512
TS_MIXB = 256
TS_FFN = 256
TS_POOL = 512
TS_MM = 512
R_CHUNK = 64
GATHER_MID_STEPS = {"mix0_fwd": 2, "ffn0_fwd": 5}

MESH = pl.DeviceIdType.MESH
ANY = pl.BlockSpec(memory_space=pl.ANY)
NT_DIMS = (((1,), (1,)), ((), ()))
TN_DIMS = (((0,), (0,)), ((), ()))


def _cparams(n_axes):
    return pltpu.CompilerParams(dimension_semantics=("arbitrary",) * n_axes, vmem_limit_bytes=VMEM_LIMIT_BYTES)


def _const(shape):
    nd = len(shape)
    return pl.BlockSpec(shape, lambda *_: (0,) * nd, pipeline_mode=pl.Buffered(1))


def _sigmoid(v):
    return 1.0 / (1.0 + jnp.exp(-v))


def _rsqrt_mean_sq(x):
    return lax.rsqrt(jnp.mean(x * x, axis=-1, keepdims=True) + RMS_EPS)


def _rms_bwd(dh, xh, r, g):
    dxh = dh * g
    return r * (dxh - xh * jnp.mean(dxh * xh, axis=-1, keepdims=True))


def _shifted(buf_ref, row0, rows, col0, width, offsets):
    lo = (min(offsets) // SUBLANES) * SUBLANES
    hi = -(-(max(offsets) + rows) // SUBLANES) * SUBLANES
    start = row0 + lo if isinstance(row0, int) else pl.multiple_of(row0 + lo, SUBLANES)
    win = buf_ref[pl.ds(start, hi - lo), col0:col0 + width]
    out = {}
    for res in sorted({(o - lo) % SUBLANES for o in offsets}):
        qs = {o: (o - lo) // SUBLANES for o in offsets if (o - lo) % SUBLANES == res}
        base = pltpu.roll(win, hi - lo - res, 0) if res else win
        for o, q in qs.items():
            out[o] = base[SUBLANES * q:SUBLANES * q + rows, :]
    return out


def _rowsum8(v):
    acc = v[0:SUBLANES, :]
    for r in range(SUBLANES, v.shape[0], SUBLANES):
        acc = acc + v[r:r + SUBLANES, :]
    return acc


def _taps(w_ref, sh, offsets, col0, width):
    acc = None
    for k, o in enumerate(offsets):
        term = w_ref[k:k + 1, col0:col0 + width] * sh[o]
        acc = term if acc is None else acc + term
    return acc


def _window_bases(buf_ref, row0, rows, col0, width, offsets):
    lo = (min(offsets) // SUBLANES) * SUBLANES
    hi = -(-(max(offsets) + rows) // SUBLANES) * SUBLANES
    start = row0 + lo if isinstance(row0, int) else pl.multiple_of(row0 + lo, SUBLANES)
    win = buf_ref[pl.ds(start, hi - lo), col0:col0 + width]
    for res in sorted({(o - lo) % SUBLANES for o in offsets}):
        taps = [(k, (o - lo) // SUBLANES * SUBLANES) for k, o in enumerate(offsets) if (o - lo) % SUBLANES == res]
        yield (pltpu.roll(win, hi - lo - res, 0) if res else win), taps


def _conv_acc(buf_ref, w_ref, row0, rows, col0, width, offsets):
    acc = None
    for base, taps in _window_bases(buf_ref, row0, rows, col0, width, offsets):
        for k, q in taps:
            term = w_ref[k:k + 1, col0:col0 + width] * base[q:q + rows, :]
            acc = term if acc is None else acc + term
    return acc


def _conv_corr(buf_ref, w_ref, other, acc_ref, row0, rows, col0, width, offsets):
    acc = None
    for base, taps in _window_bases(buf_ref, row0, rows, col0, width, offsets):
        for k, q in taps:
            sl = base[q:q + rows, :]
            term = w_ref[k:k + 1, col0:col0 + width] * sl
            acc = term if acc is None else acc + term
            acc_ref[SUBLANES * k:SUBLANES * (k + 1), col0:col0 + width] += _rowsum8(sl * other)
    return acc


def _finish_tap_sums(acc_ref, out_ref, n_taps):
    for k in range(n_taps):
        out_ref[k:k + 1, :] = jnp.sum(acc_ref[SUBLANES * k:SUBLANES * (k + 1), :], axis=0, keepdims=True)


class _Comm:
    def __init__(self, inputs, out_shapes, sems, start, finish, aliases=None, mid=None, mid_steps=1):
        self.inputs, self.out_shapes, self.sems = list(inputs), list(out_shapes), list(sems)
        self.start, self.finish, self.mid, self.aliases = start, finish, mid, dict(aliases or {})
        self.mid_steps = mid_steps


def _join_comm(a, b):
    ni, no, ns = len(a.inputs), len(a.out_shapes), len(a.sems)

    def both(name):
        def run(ins, outs, sems):
            if getattr(a, name) is not None:
                getattr(a, name)(ins[:ni], outs[:no], sems[:ns])
            if getattr(b, name) is not None:
                getattr(b, name)(ins[ni:], outs[no:], sems[ns:])
        return run

    aliases = {**a.aliases, **{ni + i: no + o for i, o in b.aliases.items()}}
    mid = both("mid") if (a.mid is not None or b.mid is not None) else None
    return _Comm(a.inputs + b.inputs, a.out_shapes + b.out_shapes, a.sems + b.sems, both("start"), both("finish"), aliases, mid)


def _call(body, *, name, grid, in_specs, out_specs, out_shape, args, scratch=(), comm=None, aliases=None):
    n_in, n_out, n_scr, n_axes = len(in_specs), len(out_specs), len(scratch), len(grid)
    params = pltpu.CompilerParams(dimension_semantics=("arbitrary",) * n_axes, vmem_limit_bytes=VMEM_LIMIT_BYTES)
    aliases = dict(aliases or {})
    if comm is None:
        outs = pl.pallas_call(body, name=name, grid=grid, in_specs=list(in_specs), out_specs=list(out_specs),
                              out_shape=list(out_shape), scratch_shapes=list(scratch), input_output_aliases=aliases,
                              compiler_params=params)(*args)
        return list(outs), []
    ci, co = len(comm.inputs), len(comm.out_shapes)

    def wrapped(*refs):
        k_in, c_in = refs[:n_in], refs[n_in:n_in + ci]
        o0 = n_in + ci
        k_out, c_out = refs[o0:o0 + n_out], refs[o0 + n_out:o0 + n_out + co]
        s0 = o0 + n_out + co
        k_scr, c_sem = refs[s0:s0 + n_scr], refs[s0 + n_scr:]
        first = pl.program_id(0) == 0
        last = pl.program_id(0) == grid[0] - 1
        for ax in range(1, n_axes):
            first = jnp.logical_and(first, pl.program_id(ax) == 0)
            last = jnp.logical_and(last, pl.program_id(ax) == grid[ax] - 1)

        @pl.when(first)
        def _():
            comm.start(c_in, c_out, c_sem)

        mid_early = comm.mid is not None and n_axes == 1 and grid[0] > comm.mid_steps
        if mid_early:
            @pl.when(pl.program_id(0) == grid[0] - 1 - comm.mid_steps)
            def _():
                comm.mid(c_in, c_out, c_sem)

        body(*k_in, *k_out, *k_scr)

        @pl.when(last)
        def _():
            if comm.mid is not None and not mid_early:
                comm.mid(c_in, c_out, c_sem)
            comm.finish(c_in, c_out, c_sem)

    outs = pl.pallas_call(
        wrapped, name=name, grid=grid, in_specs=list(in_specs) + [ANY] * ci, out_specs=list(out_specs) + [ANY] * co,
        out_shape=list(out_shape) + comm.out_shapes, scratch_shapes=list(scratch) + comm.sems,
        input_output_aliases={**aliases, **{n_in + i: n_out + o for i, o in comm.aliases.items()}}, compiler_params=params,
    )(*args, *comm.inputs)
    return list(outs[:n_out]), list(outs[n_out:])


def _run_comm(comm, name):
    ci, co = len(comm.inputs), len(comm.out_shapes)

    def body(*refs):
        c_in, c_out, c_sem = refs[:ci], refs[ci:ci + co], refs[ci + co:]
        comm.start(c_in, c_out, c_sem)
        if comm.mid is not None:
            comm.mid(c_in, c_out, c_sem)
        comm.finish(c_in, c_out, c_sem)

    return list(pl.pallas_call(body, name=name, in_specs=[ANY] * ci, out_specs=[ANY] * co, out_shape=comm.out_shapes,
                               scratch_shapes=comm.sems, input_output_aliases=comm.aliases)(*comm.inputs))


def _mix0_fwd(x, g, w_in, conv_a, ln_g, ln_b, conv_b, w_out, comm=None):
    s = x.shape[0]
    ts = min(TS_MIX, s)
    n = s // ts
    bw = NZ // NCHIP
    offs_a = [HALO_A - (K_A - 1) + k for k in range(K_A)]
    offs_s = [HALO_S - (K_S - 1) + k for k in range(K_S)]

    def body(x_ref, g_ref, win_ref, ca_ref, lg_ref, lb_ref, cb_ref, wout_ref,
             h_ref, z_ref, ac_ref, bconv_buf, cat_ref, x1_ref, glu_buf, cv_buf):
        i = pl.program_id(0)

        @pl.when(i == 0)
        def _():
            glu_buf[0:HALO_A, :] = jnp.zeros((HALO_A, A), F32)
            cv_buf[0:HALO_S, :] = jnp.zeros((HALO_S, A), F32)

        xv = x_ref[...]
        h = (xv * _rsqrt_mean_sq(xv) * g_ref[...]).astype(BF16)
        h_ref[...] = h
        for j in range(NCHIP):
            z_ref[:, j * bw:(j + 1) * bw] = jnp.dot(h, win_ref[j], preferred_element_type=F32)
        glu_buf[HALO_A:HALO_A + ts, :] = z_ref[:, 0:A] * _sigmoid(z_ref[:, A:2 * A])
        cv_buf[HALO_S:HALO_S + ts, :] = z_ref[:, 3 * A:4 * A] * z_ref[:, 4 * A:5 * A]

        def chunk(ci, carry):
            r0 = pl.multiple_of(ci * R_CHUNK, R_CHUNK)
            for c0 in range(0, A, LANES):
                ac_ref[pl.ds(r0, R_CHUNK), c0:c0 + LANES] = _conv_acc(glu_buf, ca_ref, r0, R_CHUNK, c0, LANES, offs_a)
                bconv_buf[pl.ds(r0, R_CHUNK), c0:c0 + LANES] = _conv_acc(cv_buf, cb_ref, r0, R_CHUNK, c0, LANES, offs_s)
            return carry

        lax.fori_loop(0, ts // R_CHUNK, chunk, 0)
        glu_buf[0:HALO_A, :] = glu_buf[ts:ts + HALO_A, :]
        cv_buf[0:HALO_S, :] = cv_buf[ts:ts + HALO_S, :]

        ac = ac_ref[...]
        xc = ac - jnp.mean(ac, axis=-1, keepdims=True)
        xn = xc * lax.rsqrt(jnp.mean(xc * xc, axis=-1, keepdims=True) + LN_EPS)
        ln = xn * lg_ref[...] + lb_ref[...]
        cat_ref[:, 0:A] = (ln * _sigmoid(ln)).astype(BF16)
        cat_ref[:, A:2 * A] = (z_ref[:, 2 * A:3 * A] * bconv_buf[...]).astype(BF16)
        x1_ref[...] = xv + jnp.dot(cat_ref[...], wout_ref[...], preferred_element_type=F32)

    tile = lambda w: pl.BlockSpec((ts, w), lambda i: (i, 0))
    return _call(
        body, name="mix0_fwd", grid=(n,), comm=comm, args=(x, g, w_in, conv_a, ln_g, ln_b, conv_b, w_out),
        in_specs=[tile(D), _const((1, D)), _const((NCHIP, D, bw)), _const((K_A, A)), _const((1, A)), _const((1, A)),
                  _const((K_S, A)), _const((2 * A, D))],
        out_specs=[tile(D), tile(NZ), tile(A), tile(A), tile(2 * A), tile(D)],
        out_shape=[jax.ShapeDtypeStruct((s, D), BF16), jax.ShapeDtypeStruct((s, NZ), F32), jax.ShapeDtypeStruct((s, A), F32),
                   jax.ShapeDtypeStruct((s, A), F32), jax.ShapeDtypeStruct((s, 2 * A), BF16), jax.ShapeDtypeStruct((s, D), F32)],
        scratch=[pltpu.VMEM((HALO_A + ts, A), F32), pltpu.VMEM((HALO_S + ts, A), F32)])


def _ffn_fwd(x, g, w_up, wc, w_down, name, comm=None, head=None):
    s = x.shape[0]
    ts = min(TS_FFN, s)
    n = s // ts
    bw = FF2 // NCHIP
    rows = 32
    offs = [HALO_S - (K_S - 1) + k for k in range(K_S)]
    n_in = 5 + (2 if head else 0)

    def body(*refs):
        x_ref, g_ref, wup_ref, wc_ref, wdn_ref = refs[:5]
        h_ref, u0_ref, u_ref, act_ref, xo_ref = refs[n_in:n_in + 5]
        cbuf = refs[-1]
        i = pl.program_id(0)

        @pl.when(i == 0)
        def _():
            cbuf[0:HALO_S, :] = jnp.zeros((HALO_S, FF2), F32)
            if head:
                refs[n_in + 5][...] = jnp.zeros((1, D), F32)
                refs[n_in + 6][...] = jnp.zeros((1, LANES), F32)

        xv = x_ref[...]
        h = (xv * _rsqrt_mean_sq(xv) * g_ref[...]).astype(BF16)
        h_ref[...] = h
        f = None
        for p in range(NCHIP // 2):
            for j in (p, NCHIP // 2 + p):
                zc = jnp.dot(h, wup_ref[j], preferred_element_type=F32)
                u0_ref[:, j * bw:(j + 1) * bw] = zc.astype(BF16)
                cbuf[HALO_S:HALO_S + ts, j * bw:(j + 1) * bw] = zc
            for r0 in range(0, ts, rows):
                for c0 in range(p * bw, (p + 1) * bw, LANES):
                    ug = _taps(wc_ref, _shifted(cbuf, r0, rows, c0, LANES, offs), offs, c0, LANES)
                    uv = _taps(wc_ref, _shifted(cbuf, r0, rows, FF + c0, LANES, offs), offs, FF + c0, LANES)
                    u_ref[r0:r0 + rows, c0:c0 + LANES] = ug
                    u_ref[r0:r0 + rows, FF + c0:FF + c0 + LANES] = uv
                    act_ref[r0:r0 + rows, c0:c0 + LANES] = (ug * _sigmoid(ug) * uv).astype(BF16)
            fp = jnp.dot(act_ref[:, p * bw:(p + 1) * bw], wdn_ref[p * bw:(p + 1) * bw, :], preferred_element_type=F32)
            f = fp if f is None else f + fp
        cbuf[0:HALO_S, :] = cbuf[ts:ts + HALO_S, :]
        if not head:
            xo_ref[...] = xv + f
        else:
            gf_ref, t_ref, dgf_ref, loss_ref = refs[5], refs[6], refs[n_in + 5], refs[n_in + 6]
            xo = xv + f
            r = _rsqrt_mean_sq(xo)
            xh = xo * r
            gv = gf_ref[...]
            err = xh * gv - t_ref[...]
            loss_ref[...] += jnp.sum(jnp.sum(err * err, axis=1, keepdims=True), axis=0, keepdims=True) * (0.5 / D)
            dy = err * (1.0 / D)
            dgf_ref[...] += jnp.sum(dy * xh, axis=0, keepdims=True)
            xo_ref[...] = _rms_bwd(dy, xh, r, gv)

    tile = lambda w: pl.BlockSpec((ts, w), lambda i: (i, 0))
    one_row = lambda w: pl.BlockSpec((1, w), lambda i: (0, 0))
    return _call(
        body, name=name, grid=(n,), comm=comm, args=(x, g, w_up, wc, w_down) + (tuple(head) if head else ()),
        in_specs=[tile(D), _const((1, D)), _const((NCHIP, D, bw)), _const((K_S, FF2)), _const((FF, D))]
        + ([_const((1, D)), tile(D)] if head else []),
        out_specs=[tile(D), tile(FF2), tile(FF2), tile(FF), tile(D)] + ([one_row(D), one_row(LANES)] if head else []),
        out_shape=[jax.ShapeDtypeStruct((s, D), BF16), jax.ShapeDtypeStruct((s, FF2), BF16), jax.ShapeDtypeStruct((s, FF2), F32),
                   jax.ShapeDtypeStruct((s, FF), BF16), jax.ShapeDtypeStruct((s, D), F32)]
        + ([jax.ShapeDtypeStruct((1, D), F32), jax.ShapeDtypeStruct((1, LANES), F32)] if head else []),
        scratch=[pltpu.VMEM((HALO_S + ts, FF2), F32)])


def _pool_windows(hbuf, pbuf, tile_row0, ts):
    def chunk(ci, carry):
        r0 = pl.multiple_of(ci * R_CHUNK, R_CHUNK)
        t1 = (tile_row0 + r0 + lax.broadcasted_iota(jnp.int32, (R_CHUNK, 1), 0) + 1).astype(F32)
        for gi, w in enumerate(POOL_WINDOWS):
            cnt = jnp.minimum(t1, float(w))
            offs = [HALO_P - jj for jj in range(w)]
            for c0 in range(gi * PG, (gi + 1) * PG, LANES):
                sh = _shifted(hbuf, r0, R_CHUNK, c0, LANES, offs)
                tot = sh[offs[0]]
                for o in offs[1:]:
                    tot = tot + sh[o]
                pbuf[pl.ds(r0, R_CHUNK), c0:c0 + LANES] = (tot / cnt - sh[HALO_P]).astype(BF16)
        return carry

    lax.fori_loop(0, ts // R_CHUNK, chunk, 0)


def _assemble_wpool(wp_ref, wps):
    rb = PG // NCHIP
    for gi in range(len(POOL_WINDOWS)):
        for j in range(NCHIP):
            wps[gi, j * rb:(j + 1) * rb, :] = wp_ref[j, gi]


def _pool_fwd(x, g, w_pool, scale):
    s = x.shape[0]
    ts = min(TS_POOL, s)
    n = s // ts
    ng = len(POOL_WINDOWS)

    def body(x_ref, g_ref, wp_ref, sc_ref, xo_ref, hbuf, pbuf, wps):
        i = pl.program_id(0)

        @pl.when(i == 0)
        def _():
            hbuf[0:HALO_P, :] = jnp.zeros((HALO_P, D), F32)
            _assemble_wpool(wp_ref, wps)

        xv = x_ref[...]
        hbuf[HALO_P:HALO_P + ts, :] = xv * _rsqrt_mean_sq(xv) * g_ref[...]
        _pool_windows(hbuf, pbuf, i * ts, ts)
        hbuf[0:HALO_P, :] = hbuf[ts:ts + HALO_P, :]
        for gi in range(ng):
            cols = slice(gi * PG, (gi + 1) * PG)
            y = jnp.dot(pbuf[:, cols], wps[gi], preferred_element_type=F32)
            xo_ref[:, cols] = xv[:, cols] + y * sc_ref[:, cols]

    tile = pl.BlockSpec((ts, D), lambda i: (i, 0))
    return pl.pallas_call(
        body, name="pool_fwd", grid=(n,),
        in_specs=[tile, _const((1, D)), _const((NCHIP, ng, PG // NCHIP, PG)), _const((1, D))],
        out_specs=tile, out_shape=jax.ShapeDtypeStruct((s, D), F32),
        scratch_shapes=[pltpu.VMEM((HALO_P + ts, D), F32), pltpu.VMEM((ts, D), BF16), pltpu.VMEM((ng, PG, PG), BF16)],
        compiler_params=_cparams(1),
    )(x, g, w_pool, scale)


def _ffn_bwd_a(dxo, u, u0, wc, w_down, name, comm=None):
    s = dxo.shape[0]
    ts = min(TS_FFN, s)
    n = s // ts
    cw = FF2 // NCHIP
    rows = 32
    lw2 = 2 * LANES
    boffs = [K_S - 1 - k for k in range(K_S)]

    def body(dxo_ref, u_ref, u0_ref, wc_ref, wdn_ref, du0_ref, dwc_ref, dubuf, dact, dwacc):
        i = pl.program_id(0)

        @pl.when(i == 0)
        def _():
            dubuf[ts:ts + HALO_S, :] = jnp.zeros((HALO_S, FF2), F32)
            dwacc[...] = jnp.zeros(dwacc.shape, F32)

        df = dxo_ref[...].astype(BF16)
        for cg in range(0, FF, cw):
            dact[...] = lax.dot_general(df, wdn_ref[cg:cg + cw, :], NT_DIMS, preferred_element_type=F32)

            def chunk(ci, carry, cg=cg):
                rs = pl.ds(pl.multiple_of(ci * rows, rows), rows)
                for c in range(cg, cg + cw, LANES):
                    ug, uv = u_ref[rs, c:c + LANES], u_ref[rs, FF + c:FF + c + LANES]
                    sg = _sigmoid(ug)
                    da = dact[rs, c - cg:c - cg + LANES]
                    gs = ug * sg
                    dubuf[rs, c:c + LANES] = da * uv * (sg + gs * (1.0 - sg))
                    dubuf[rs, FF + c:FF + c + LANES] = da * gs
                return carry

            lax.fori_loop(0, ts // rows, chunk, 0)

        def chunk2(ci, carry):
            r0 = pl.multiple_of(ci * rows, rows)
            rs = pl.ds(r0, rows)
            for c in range(0, FF2, lw2):
                sh = _shifted(dubuf, r0, rows, c, lw2, boffs)
                du0_ref[rs, c:c + lw2] = _taps(wc_ref, sh, boffs, c, lw2).astype(BF16)
                u0v = u0_ref[rs, c:c + lw2].astype(F32)
                for k, o in enumerate(boffs):
                    dwacc[SUBLANES * k:SUBLANES * (k + 1), c:c + lw2] += _rowsum8(sh[o] * u0v)
            return carry

        lax.fori_loop(0, ts // rows, chunk2, 0)
        dubuf[ts:ts + HALO_S, :] = dubuf[0:HALO_S, :]

        @pl.when(i == n - 1)
        def _():
            _finish_tap_sums(dwacc, dwc_ref, K_S)

    rev = lambda w: pl.BlockSpec((ts, w), lambda i: (n - 1 - i, 0))
    return _call(
        body, name=name, grid=(n,), comm=comm, args=(dxo, u, u0, wc, w_down),
        in_specs=[rev(D), rev(FF2), rev(FF2), _const((K_S, FF2)), _const((FF, D))],
        out_specs=[rev(FF2), pl.BlockSpec((K_S, FF2), lambda i: (0, 0))],
        out_shape=[jax.ShapeDtypeStruct((s, FF2), BF16), jax.ShapeDtypeStruct((K_S, FF2), F32)],
        scratch=[pltpu.VMEM((ts + HALO_S, FF2), F32), pltpu.VMEM((ts, cw), F32), pltpu.VMEM((SUBLANES * K_S, FF2), F32)])


def _nt_rms_bwd(dy, w, x, g, dres, name, comm=None, tiles=None, dx_so_far=None):
    s = x.shape[0]
    ts = min(TS_MM, s)
    first, n = (0, s // ts) if tiles is None else tiles
    nw = dy.shape[1]
    bw = nw // NCHIP

    def body(dy_ref, w_ref, x_ref, g_ref, dres_ref, dx_ref, dg_ref):
        i = pl.program_id(0)

        @pl.when(i == 0)
        def _():
            dg_ref[...] = jnp.zeros((1, D), F32)

        dh = lax.dot_general(dy_ref[:, 0:bw], w_ref[0], NT_DIMS, preferred_element_type=F32)
        for j in range(1, NCHIP):
            dh = dh + lax.dot_general(dy_ref[:, j * bw:(j + 1) * bw], w_ref[j], NT_DIMS, preferred_element_type=F32)
        xv = x_ref[...]
        r = _rsqrt_mean_sq(xv)
        xh = xv * r
        dg_ref[...] += jnp.sum(dh * xh, axis=0, keepdims=True)
        dx_ref[...] = dres_ref[...] + _rms_bwd(dh, xh, r, g_ref[...])

    def body_with_alias(dy_ref, w_ref, x_ref, g_ref, dres_ref, _, dx_ref, dg_ref):
        body(dy_ref, w_ref, x_ref, g_ref, dres_ref, dx_ref, dg_ref)

    tile = lambda wd: pl.BlockSpec((ts, wd), lambda i: (first + i, 0))
    in_specs = [tile(nw), _const((NCHIP, D, bw)), tile(D), _const((1, D)), tile(D)]
    more = dx_so_far is not None
    return _call(
        body_with_alias if more else body, name=name, grid=(n,), comm=comm,
        args=(dy, w, x, g, dres) + ((dx_so_far,) if more else ()), in_specs=in_specs + [ANY] * more,
        out_specs=[tile(D), pl.BlockSpec((1, D), lambda i: (0, 0))], aliases={5: 0} if more else None,
        out_shape=[jax.ShapeDtypeStruct((s, D), F32), jax.ShapeDtypeStruct((1, D), F32)])


def _wgrad(a, b, n_blocks, name, comm=None):
    s, m = a.shape
    bw = b.shape[1] // n_blocks
    tk = min(TS_MM, s)

    def body(a_ref, b_ref, o_ref):
        @pl.when(pl.program_id(0) == 0)
        def _():
            o_ref[...] = jnp.zeros(o_ref.shape, F32)

        for j in range(n_blocks):
            o_ref[j] += lax.dot_general(a_ref[...], b_ref[:, j * bw:(j + 1) * bw].astype(BF16), TN_DIMS,
                                        preferred_element_type=F32)

    (out,), comm_out = _call(
        body, name=name, grid=(s // tk,), comm=comm, args=(a, b),
        in_specs=[pl.BlockSpec((tk, m), lambda k: (k, 0)), pl.BlockSpec((tk, b.shape[1]), lambda k: (k, 0))],
        out_specs=[pl.BlockSpec((n_blocks, m, bw), lambda k: (0, 0, 0))],
        out_shape=[jax.ShapeDtypeStruct((n_blocks, m, bw), F32)])
    return out, comm_out


def _pool_bwd(dxo, x, g, w_pool, scale, comm=None):
    s = x.shape[0]
    ts = min(TS_POOL, s)
    n = s // ts
    ng = len(POOL_WINDOWS)
    rb = PG // NCHIP

    def body(dxo_ref, x_ref, halo_ref, g_ref, wp_ref, sc_ref, dx_ref, dwp_ref, dsc_ref, dg_ref,
             hbuf, pbuf, qbuf, dhbuf, wps, dwacc):
        i = pl.program_id(0)
        j = n - 1 - i

        @pl.when(i == 0)
        def _():
            qbuf[ts:ts + HALO_P, :] = jnp.zeros((HALO_P, D), F32)
            dwacc[...] = jnp.zeros(dwacc.shape, F32)
            dsc_ref[...] = jnp.zeros((1, D), F32)
            dg_ref[...] = jnp.zeros((1, D), F32)
            _assemble_wpool(wp_ref, wps)

        gv = g_ref[...]
        xl = halo_ref[...]
        hbuf[0:HALO_P, :] = jnp.where(j == 0, 0.0, xl * _rsqrt_mean_sq(xl) * gv)
        xv = x_ref[...]
        r = _rsqrt_mean_sq(xv)
        xh = xv * r
        hbuf[HALO_P:HALO_P + ts, :] = xh * gv
        _pool_windows(hbuf, pbuf, j * ts, ts)

        dy = dxo_ref[...]
        t1 = (j * ts + lax.broadcasted_iota(jnp.int32, (ts, 1), 0) + 1).astype(F32)
        for gi, w in enumerate(POOL_WINDOWS):
            cols = slice(gi * PG, (gi + 1) * PG)
            p = pbuf[:, cols]
            y = jnp.dot(p, wps[gi], preferred_element_type=F32)
            dsc_ref[:, cols] += jnp.sum(dy[:, cols] * y, axis=0, keepdims=True)
            dq = (dy[:, cols] * sc_ref[:, cols]).astype(BF16)
            dwacc[gi] += lax.dot_general(p, dq, TN_DIMS, preferred_element_type=F32)
            dp = lax.dot_general(dq, wps[gi], NT_DIMS, preferred_element_type=F32)
            qbuf[0:ts, cols] = dp / jnp.minimum(t1, float(w))

        def chunk(ci, carry):
            r0 = pl.multiple_of(ci * R_CHUNK, R_CHUNK)
            tc = (j * ts + r0 + lax.broadcasted_iota(jnp.int32, (R_CHUNK, 1), 0) + 1).astype(F32)
            for gi, w in enumerate(POOL_WINDOWS):
                cnt = jnp.minimum(tc, float(w))
                offs = list(range(w))
                for c0 in range(gi * PG, (gi + 1) * PG, LANES):
                    sh = _shifted(qbuf, r0, R_CHUNK, c0, LANES, offs)
                    tot = sh[0]
                    for o in offs[1:]:
                        tot = tot + sh[o]
                    dhbuf[pl.ds(r0, R_CHUNK), c0:c0 + LANES] = tot - sh[0] * cnt
            return carry

        lax.fori_loop(0, ts // R_CHUNK, chunk, 0)
        qbuf[ts:ts + HALO_P, :] = qbuf[0:HALO_P, :]
        dh = dhbuf[...]
        dg_ref[...] += jnp.sum(dh * xh, axis=0, keepdims=True)
        dx_ref[...] = dy + _rms_bwd(dh, xh, r, gv)

        @pl.when(i == n - 1)
        def _():
            for gi in range(ng):
                for jj in range(NCHIP):
                    dwp_ref[jj, gi] = dwacc[gi, jj * rb:(jj + 1) * rb, :]

    rev = pl.BlockSpec((ts, D), lambda i: (n - 1 - i, 0))
    halo = pl.BlockSpec((HALO_P, D), lambda i: (jnp.maximum((n - 1 - i) * (ts // HALO_P) - 1, 0), 0))
    vec = pl.BlockSpec((1, D), lambda i: (0, 0))
    return _call(
        body, name="pool_bwd", grid=(n,), comm=comm, args=(dxo, x, x, g, w_pool, scale),
        in_specs=[rev, rev, halo, _const((1, D)), _const((NCHIP, ng, rb, PG)), _const((1, D))],
        out_specs=[rev, pl.BlockSpec((NCHIP, ng, rb, PG), lambda i: (0, 0, 0, 0)), vec, vec],
        out_shape=[jax.ShapeDtypeStruct((s, D), F32), jax.ShapeDtypeStruct((NCHIP, ng, rb, PG), F32),
                   jax.ShapeDtypeStruct((1, D), F32), jax.ShapeDtypeStruct((1, D), F32)],
        scratch=[pltpu.VMEM((HALO_P + ts, D), F32), pltpu.VMEM((ts, D), BF16), pltpu.VMEM((ts + HALO_P, D), F32),
                 pltpu.VMEM((ts, D), F32), pltpu.VMEM((ng, PG, PG), BF16), pltpu.VMEM((ng, PG, PG), F32)])


def _mix0_bwd_a(dx1, z, ac, bconv, w_out, conv_a, ln_g, ln_b, conv_b, comm=None):
    s = dx1.shape[0]
    ts = min(TS_MIXB, s)
    n = s // ts
    rows = 32
    boffs_a = [K_A - 1 - k for k in range(K_A)]
    boffs_s = [K_S - 1 - k for k in range(K_S)]

    def body(dx1_ref, z_ref, ac_ref, bconv_ref, wout_ref, ca_ref, lg_ref, lb_ref, cb_ref,
             dz_ref, dca_ref, dlg_ref, dlb_ref, dcb_ref, dac_buf, dbc_buf, dca_acc, dcb_acc):
        i = pl.program_id(0)

        @pl.when(i == 0)
        def _():
            dac_buf[ts:ts + HALO_A, :] = jnp.zeros((HALO_A, A), F32)
            dbc_buf[ts:ts + HALO_S, :] = jnp.zeros((HALO_S, A), F32)
            dca_acc[...] = jnp.zeros(dca_acc.shape, F32)
            dcb_acc[...] = jnp.zeros(dcb_acc.shape, F32)
            dlg_ref[...] = jnp.zeros((1, A), F32)
            dlb_ref[...] = jnp.zeros((1, A), F32)

        dcat = lax.dot_general(dx1_ref[...].astype(BF16), wout_ref[...], NT_DIMS, preferred_element_type=F32)
        db = dcat[:, A:2 * A]
        dz_ref[:, 2 * A:3 * A] = (db * bconv_ref[...]).astype(BF16)
        dbc_buf[0:ts, :] = db * z_ref[:, 2 * A:3 * A]
        ac = ac_ref[...]
        xc = ac - jnp.mean(ac, axis=-1, keepdims=True)
        rstd = lax.rsqrt(jnp.mean(xc * xc, axis=-1, keepdims=True) + LN_EPS)
        xn = xc * rstd
        lg = lg_ref[...]
        ln = xn * lg + lb_ref[...]
        sl = _sigmoid(ln)
        dln = dcat[:, 0:A] * sl * (1.0 + ln * (1.0 - sl))
        dlg_ref[...] += jnp.sum(dln * xn, axis=0, keepdims=True)
        dlb_ref[...] += jnp.sum(dln, axis=0, keepdims=True)
        dxn = dln * lg
        dac_buf[0:ts, :] = rstd * (dxn - jnp.mean(dxn, axis=-1, keepdims=True)
                                   - xn * jnp.mean(dxn * xn, axis=-1, keepdims=True))

        def chunk(ci, carry):
            r0 = pl.multiple_of(ci * rows, rows)
            rs = pl.ds(r0, rows)
            for c0 in range(0, A, LANES):
                col = lambda grp: slice(grp * A + c0, grp * A + c0 + LANES)
                a_val, sg = z_ref[rs, col(0)], _sigmoid(z_ref[rs, col(1)])
                dglu = _conv_corr(dac_buf, ca_ref, a_val * sg, dca_acc, r0, rows, c0, LANES, boffs_a)
                dz_ref[rs, col(0)] = (dglu * sg).astype(BF16)
                dz_ref[rs, col(1)] = (dglu * a_val * sg * (1.0 - sg)).astype(BF16)
                c_gate, bc_val = z_ref[rs, col(3)], z_ref[rs, col(4)]
                dcv = _conv_corr(dbc_buf, cb_ref, c_gate * bc_val, dcb_acc, r0, rows, c0, LANES, boffs_s)
                dz_ref[rs, col(3)] = (dcv * bc_val).astype(BF16)
                dz_ref[rs, col(4)] = (dcv * c_gate).astype(BF16)
            return carry

        lax.fori_loop(0, ts // rows, chunk, 0)
        dac_buf[ts:ts + HALO_A, :] = dac_buf[0:HALO_A, :]
        dbc_buf[ts:ts + HALO_S, :] = dbc_buf[0:HALO_S, :]

        @pl.when(i == n - 1)
        def _():
            _finish_tap_sums(dca_acc, dca_ref, K_A)
            _finish_tap_sums(dcb_acc, dcb_ref, K_S)

    rev = lambda w: pl.BlockSpec((ts, w), lambda i: (n - 1 - i, 0))
    full = lambda r, c: pl.BlockSpec((r, c), lambda i: (0, 0))
    return _call(
        body, name="mix0_bwd_a", grid=(n,), comm=comm, args=(dx1, z, ac, bconv, w_out, conv_a, ln_g, ln_b, conv_b),
        in_specs=[rev(D), rev(NZ), rev(A), rev(A), _const((2 * A, D)), _const((K_A, A)), _const((1, A)), _const((1, A)),
                  _const((K_S, A))],
        out_specs=[rev(NZ), full(K_A, A), full(1, A), full(1, A), full(K_S, A)],
        out_shape=[jax.ShapeDtypeStruct((s, NZ), BF16), jax.ShapeDtypeStruct((K_A, A), F32), jax.ShapeDtypeStruct((1, A), F32),
                   jax.ShapeDtypeStruct((1, A), F32), jax.ShapeDtypeStruct((K_S, A), F32)],
        scratch=[pltpu.VMEM((ts + HALO_A, A), F32), pltpu.VMEM((ts + HALO_S, A), F32), pltpu.VMEM((SUBLANES * K_A, A), F32),
                 pltpu.VMEM((SUBLANES * K_S, A), F32)])


def _cast_later_weights(w_up, w_down, w_pool, comm):
    nl = w_up.shape[0]

    def body(up_ref, dn_ref, pool_ref, *outs):
        up_o, dn_o, pool_o = outs[:nl], outs[nl:2 * nl], outs[2 * nl]
        for layer in range(nl):
            @pl.when(pl.program_id(0) == layer)
            def _(layer=layer):
                up_o[layer][...] = up_ref[...].astype(BF16)
                dn_o[layer][...] = dn_ref[...].astype(BF16)
                if layer == 0:
                    pool_o[...] = pool_ref[...].astype(BF16)

    per_layer = lambda a: pl.BlockSpec((None,) + a.shape[1:], lambda l: (l,) + (0,) * (a.ndim - 1))
    whole = lambda shape: pl.BlockSpec(shape, lambda l: (0,) * len(shape))
    out_shapes = [w_up.shape[1:]] * nl + [w_down.shape[1:]] * nl + [w_pool.shape]
    return _call(body, name="cast_later_weights", grid=(nl,), comm=comm, args=(w_up, w_down, w_pool),
                 in_specs=[per_layer(w_up), per_layer(w_down), whole(w_pool.shape)],
                 out_specs=[whole(shape) for shape in out_shapes],
                 out_shape=[jax.ShapeDtypeStruct(shape, BF16) for shape in out_shapes])


def _train_step(x, target, place, shard, later, rep, small_pack, small_shapes, adamw):
    bw_up, bw_in = FF2 // NCHIP, NZ // NCHIP
    five = lambda a, k: a.reshape(NCHIP, 2, *HALF[k])

    half = lambda a, name: a.reshape((2,) + HALF[name])
    (up0, up1, dn0, dn1, pool_bf), (g_in, g_out, small_g) = _cast_later_weights(
        *later, comm=_gather_comm([shard["w_in"], shard["w_out"]], small_pack))
    shard = dict(shard, w_pool=half(pool_bf, "w_pool"), w_up0=half(up0, "w_up"), w_up1=half(up1, "w_up"),
                 w_down0=half(dn0, "w_down"), w_down1=half(dn1, "w_down"))
    whole = {}
    for k, part in zip(SMALL_SHARDED, _unpack(small_g, small_shapes, lead=(NCHIP,))):
        whole[k] = jnp.moveaxis(part, 0, 1).reshape(part.shape[1], NCHIP * part.shape[2])
    w_in, w_out = g_in.reshape(NCHIP, D, bw_in), g_out.reshape(2 * A, D)
    conv_ffn = whole["conv_ffn_w"].reshape(2, K_S, FF2)
    nffn = [rep["norm_ffn"][0:1], rep["norm_ffn"][1:2]]

    (h0, z, ac, bconv, cat, x1), (g_up0, g_dn0) = _mix0_fwd(
        x, rep["norm_mix_even"], w_in, whole["conv_a"], rep["ln_a_g"], rep["ln_a_b"], whole["conv_b"], w_out,
        comm=_gather_comm([shard["w_up0"], shard["w_down0"]], mid_steps=GATHER_MID_STEPS["mix0_fwd"]))
    w_up0, w_dn0 = g_up0.reshape(NCHIP, D, bw_up), g_dn0.reshape(FF, D)
    (hf0, u00, u0, act0, x2), (g_pool, g_up1, g_dn1) = _ffn_fwd(
        x1, nffn[0], w_up0, conv_ffn[0], w_dn0, "ffn0_fwd",
        comm=_gather_comm([shard["w_pool"], shard["w_up1"], shard["w_down1"]], mid_steps=GATHER_MID_STEPS["ffn0_fwd"]))
    w_pool = g_pool.reshape(NCHIP, len(POOL_WINDOWS), PG // NCHIP, PG)
    w_up1, w_dn1 = g_up1.reshape(NCHIP, D, bw_up), g_dn1.reshape(FF, D)
    x3 = _pool_fwd(x2, whole["norm_mix_odd"], w_pool, whole["pool_scale"])
    (hf1, u01, u1, act1, dx4, g_nfin, loss_part), _ = _ffn_fwd(x3, nffn[1], w_up1, conv_ffn[1], w_dn1, "ffn1_fwd",
                                                                head=(rep["norm_final"], target))

    psum = lambda k, g, ld, tag="": _pair_sum(place, g, ld, "pair_sum_" + k + tag)
    tot = lambda k, g, ld, p, layer=0, nl=1, prev=None: _chip_sum(place, g, ld, p, layer, nl, "chip_sum_%s%d" % (k, layer), prev)
    pair, chips, join = _pair_comm, _chips_comm, _join_comm

    gr_dn1 = five(_wgrad(act1, dx4, 1, "wgrad_down1")[0], "w_down")
    (du01, g_wc1), _ = _ffn_bwd_a(dx4, u1, u01, conv_ffn[1], w_dn1, "ffn1_bwd_a")
    g_up, (ld_dn1,) = _wgrad(hf1, du01, NCHIP, "wgrad_up1", comm=pair([gr_dn1]))
    gr_up1 = five(g_up, "w_up")
    s_dn1 = psum("w_down", gr_dn1, ld_dn1, "1")
    (dx3, g_nf1), (p_dn1, ld_up1) = _nt_rms_bwd(du01, w_up1, x3, nffn[1], dx4, "ffn1_bwd_b",
                                                comm=join(chips([s_dn1]), pair([gr_up1])))
    s_up1 = psum("w_up", gr_up1, ld_up1, "1")
    (dx2, g_wpool, g_scale, g_nmo), _ = _pool_bwd(dx3, x2, whole["norm_mix_odd"], w_pool, whole["pool_scale"])
    gr_pool = five(g_wpool, "w_pool")
    (du00, g_wc0), (p_up1,) = _ffn_bwd_a(dx2, u0, u00, conv_ffn[0], w_dn0, "ffn0_bwd_a", comm=chips([s_up1]))
    gr_dn0 = five(_wgrad(act0, dx2, 1, "wgrad_down0")[0], "w_down")
    g_up, (ld_dn0, ld_pool) = _wgrad(hf0, du00, NCHIP, "wgrad_up0", comm=pair([gr_dn0, gr_pool]))
    gr_up0 = five(g_up, "w_up")
    s_dn0, s_pool = psum("w_down", gr_dn0, ld_dn0, "0"), psum("w_pool", gr_pool, ld_pool)
    (dx1, g_nf0), (p_dn0, p_pool, ld_up0) = _nt_rms_bwd(du00, w_up0, x1, nffn[0], dx2, "ffn0_bwd_b",
                                                        comm=join(chips([s_dn0, s_pool]), pair([gr_up0])))
    s_up0 = psum("w_up", gr_up0, ld_up0, "0")
    gr_out = five(_wgrad(cat, dx1, 1, "wgrad_out")[0], "w_out")
    (dz, g_ca, g_lg, g_lb, g_cb), (p_up0, ld_out) = _mix0_bwd_a(
        dx1, z, ac, bconv, w_out, whole["conv_a"], rep["ln_a_g"], rep["ln_a_b"], whole["conv_b"],
        comm=join(chips([s_up0]), pair([gr_out])))
    s_out = psum("w_out", gr_out, ld_out)
    t_pool = tot("w_pool", gr_pool, ld_pool, p_pool)
    t_up = tot("w_up", gr_up0, ld_up0, p_up0, 0, 2, tot("w_up", gr_up1, ld_up1, p_up1, 1, 2))
    t_dn = tot("w_down", gr_dn0, ld_dn0, p_dn0, 0, 2, tot("w_down", gr_dn1, ld_dn1, p_dn1, 1, 2))
    small = {"norm_mix_odd": g_nmo, "pool_scale": g_scale, "norm_ffn": jnp.concatenate([g_nf0, g_nf1], axis=0),
             "conv_ffn_w": jnp.stack([g_wc0, g_wc1]), "norm_final": g_nfin, "loss": loss_part,
             "conv_a": g_ca, "ln_a_g": g_lg, "ln_a_b": g_lb, "conv_b": g_cb}
    g_in, (p_out, early_all, t_pool, t_up, t_dn) = _wgrad(
        h0, dz, NCHIP, "wgrad_in",
        comm=join(chips([s_out], _pack([small[k] for k in SMALL_EARLY])), _swap_comm([t_pool, t_up, t_dn])))
    gr_in = five(g_in, "w_in")
    n_mm = x.shape[0] // min(TS_MM, x.shape[0])
    (dx_a, g_nme_a), (ld_in,) = _nt_rms_bwd(dz, w_in, x, rep["norm_mix_even"], dx1, "mix0_bwd_b0", comm=pair([gr_in]),
                                            tiles=(0, n_mm // 2))
    s_in = psum("w_in", gr_in, ld_in)
    (grad_x, g_nme_b), (p_in,) = _nt_rms_bwd(dz, w_in, x, rep["norm_mix_even"], dx1, "mix0_bwd_b1", comm=chips([s_in]),
                                             tiles=(n_mm // 2, n_mm - n_mm // 2), dx_so_far=dx_a)
    small["norm_mix_even"] = g_nme_a + g_nme_b
    (late_all,) = _run_comm(chips([], _pack([small[k] for k in SMALL_LATE])), "rs_last")
    t_in, t_out = _run_comm(_swap_comm([tot("w_in", gr_in, ld_in, p_in), tot("w_out", gr_out, ld_out, p_out)]), "rs_swap")
    done = {k: adamw(k, t) for k, t in (("w_up", t_up), ("w_down", t_dn), ("w_pool", t_pool), ("w_in", t_in), ("w_out", t_out))}
    summed = dict(zip(SMALL_EARLY, _unpack(_sum_devices(early_all, "sum_small_early"), [small[k].shape for k in SMALL_EARLY])))
    summed.update(zip(SMALL_LATE, _unpack(_sum_devices(late_all, "sum_small_late"), [small[k].shape for k in SMALL_LATE])))
    return grad_x, done, summed


def _adamw_math(w, g, m, v):
    m = ADAM_B1 * m + (1.0 - ADAM_B1) * g
    v = ADAM_B2 * v + (1.0 - ADAM_B2) * (g * g)
    m_hat = m / (1.0 - ADAM_B1 ** ADAM_STEP)
    v_hat = v / (1.0 - ADAM_B2 ** ADAM_STEP)
    return -ADAM_LR * (m_hat / (jnp.sqrt(v_hat) + ADAM_EPS) + ADAM_WD * w), m, v


def _adamw_big(w, g, m, v, tr, name, comm=None):
    nl, rows, cols = w.shape

    def body(w_ref, g_ref, m_ref, v_ref, g2_ref, d_ref, m2_ref, v2_ref):
        gv = g_ref[...]
        g2_ref[...] = gv
        d_ref[...], m2_ref[...], v2_ref[...] = _adamw_math(w_ref[...], gv, m_ref[...], v_ref[...])

    spec = pl.BlockSpec((None, tr, cols), lambda l, r: (l, r, 0))
    return _call(body, name=name, grid=(nl, rows // tr), comm=comm, args=(w, g, m, v), in_specs=[spec] * 4,
                 out_specs=[spec] * 4, out_shape=[jax.ShapeDtypeStruct(w.shape, F32)] * 4)


def _adamw_small(ws, gs, ms, vs):
    n = len(ws)

    def body(*refs):
        for p in range(n):
            w_ref, g_ref, m_ref, v_ref = (refs[q * n + p] for q in range(4))
            d_ref, m2_ref, v2_ref = (refs[(4 + q) * n + p] for q in range(3))
            d_ref[...], m2_ref[...], v2_ref[...] = _adamw_math(w_ref[...], g_ref[...], m_ref[...], v_ref[...])

    whole = lambda a: pl.BlockSpec(a.shape, lambda: (0,) * a.ndim)
    outs = pl.pallas_call(
        body, name="adamw_small", in_specs=[whole(a) for a in ws] * 4, out_specs=[whole(a) for a in ws] * 3,
        out_shape=[jax.ShapeDtypeStruct(a.shape, F32) for a in ws] * 3,
        compiler_params=pltpu.CompilerParams(vmem_limit_bytes=VMEM_LIMIT_BYTES),
    )(*ws, *gs, *ms, *vs)
    return outs[0:n], outs[n:2 * n], outs[2 * n:3 * n]


def _place():
    x, y, c = lax.axis_index("x"), lax.axis_index("y"), lax.axis_index("c")
    chips = [(x, 1 - y), (1 - x, y), (1 - x, 1 - y)]
    blocks = [2 * cx + cy for cx, cy in chips]
    return x, y, c, 2 * x + y, chips, blocks


def _gather_comm(shards, small=None, mid_steps=None):
    na = len(shards)
    ns = 0 if small is None else 1

    def copies(ins, outs, sems):
        h1_send, h1_recv, h2_send, h2_recv, f1_send, f1_recv, f2_send, f2_recv, own_send, own_recv = sems[:10]
        x, y, c, j, chips, blocks = _place()
        sib = (x, y, 1 - c)
        piece = lambda a, p: pl.ds(p * (shards[a].shape[1] // 2), shards[a].shape[1] // 2)

        def remote(src, dst, send, recv, q, to):
            return pltpu.make_async_remote_copy(src_ref=src, dst_ref=dst, send_sem=send.at[q], recv_sem=recv.at[q],
                                                device_id=to, device_id_type=MESH)

        def hop1(a, k, arrival):
            dst = outs[a].at[blocks[k], c] if arrival else outs[a].at[j, c]
            return remote(dst if arrival else ins[a].at[c], dst, h1_send, h1_recv, 2 * a + k, (*chips[k], c))

        def hop2(a, p, arrival):
            ref = outs[a].at[blocks[2] if arrival else blocks[p], c, piece(a, p)]
            return remote(ref, ref, h2_send, h2_recv, 2 * a + p, (*chips[1 - p], c))

        def fwd1(a, k, half):
            ref = outs[a].at[blocks[k], half]
            return remote(ref, ref, f1_send, f1_recv, 2 * a + k, sib)

        def fwd2(a, p, half):
            ref = outs[a].at[blocks[2], half, piece(a, p)]
            return remote(ref, ref, f2_send, f2_recv, 2 * a + p, sib)

        own = [remote(ins[a], outs[a].at[j], own_send, own_recv, a, sib) for a in range(na)]
        small_copies = [remote(ins[na], outs[na].at[j], sems[10], sems[11], k, (*chips[k], c)) for k in range(3 * ns)]
        local = [pltpu.make_async_copy(ins[na], outs[na].at[j], sems[12])] if ns else []
        return hop1, hop2, fwd1, fwd2, own, small_copies, local, c

    pairs = [(a, k) for a in range(na) for k in range(2)]

    def start(ins, outs, sems):
        hop1, _, _, _, own, small_copies, local, _ = copies(ins, outs, sems)
        for cp in local + own + [hop1(a, k, False) for a, k in pairs] + small_copies:
            cp.start()

    def mid(ins, outs, sems):
        hop1, hop2, fwd1, _, _, _, _, c = copies(ins, outs, sems)
        for a, k in pairs:
            hop1(a, k, True).wait_recv()
            hop2(a, k, False).start()
            fwd1(a, k, c).start()

    def finish(ins, outs, sems):
        hop1, hop2, fwd1, fwd2, own, small_copies, local, c = copies(ins, outs, sems)
        for a, p in pairs:
            hop2(a, p, True).wait_recv()
            fwd2(a, p, c).start()
        for cp in small_copies:
            cp.wait()
        for a, k in pairs:
            hop1(a, k, False).wait_send()
            hop2(a, k, False).wait_send()
            fwd1(a, k, c).wait_send()
            fwd1(a, k, 1 - c).wait_recv()
            fwd2(a, k, c).wait_send()
            fwd2(a, k, 1 - c).wait_recv()
        for cp in own + local:
            cp.wait()

    out_shapes = [jax.ShapeDtypeStruct((NCHIP,) + s.shape, s.dtype) for s in shards]
    sems = [pltpu.SemaphoreType.DMA((2 * na,))] * 8 + [pltpu.SemaphoreType.DMA((na,))] * 2
    if ns:
        out_shapes.append(jax.ShapeDtypeStruct((NCHIP,) + small.shape, small.dtype))
        sems += [pltpu.SemaphoreType.DMA((3,)), pltpu.SemaphoreType.DMA((3,)), pltpu.SemaphoreType.DMA]
    if mid_steps is not None:
        return _Comm(list(shards) + [small] * ns, out_shapes, sems, start, finish, mid=mid, mid_steps=mid_steps)

    def forward_and_finish(ins, outs, sems):
        mid(ins, outs, sems)
        finish(ins, outs, sems)

    return _Comm(list(shards) + [small] * ns, out_shapes, sems, start, forward_and_finish)


def _simple_comm(inputs, out_shapes, make_copies, n_sems, aliases=None):
    def start(ins, outs, sems):
        for cp in make_copies(ins, outs, sems):
            cp.start()

    def finish(ins, outs, sems):
        for cp in make_copies(ins, outs, sems):
            cp.wait()

    return _Comm(inputs, out_shapes, [pltpu.SemaphoreType.DMA((n,)) for n in n_sems], start, finish, aliases)


def _pair_comm(grads):
    def make_copies(ins, outs, sems):
        x, y, c, _, _, _ = _place()
        return [pltpu.make_async_remote_copy(
            src_ref=ins[a].at[:, 1 - c], dst_ref=outs[a], send_sem=sems[0].at[a], recv_sem=sems[1].at[a],
            device_id=(x, y, 1 - c), device_id_type=MESH) for a in range(len(grads))]

    out_shapes = [jax.ShapeDtypeStruct(g.shape[:1] + g.shape[2:], F32) for g in grads]
    return _simple_comm(grads, out_shapes, make_copies, [len(grads)] * 2)


def _chips_comm(sums, small=None):
    na = len(sums)
    nk = NCHIP - 1

    def make_copies(ins, outs, sems):
        x, y, c, _, chips, blocks = _place()
        copies = [pltpu.make_async_remote_copy(
            src_ref=ins[a].at[blocks[k]], dst_ref=outs[a].at[k], send_sem=sems[0].at[a * nk + k],
            recv_sem=sems[1].at[a * nk + k], device_id=(*chips[k], c), device_id_type=MESH)
            for a in range(na) for k in range(nk)]
        if small is not None:
            me = 4 * x + 2 * y + c
            for r in range(1, NDEV):
                peer = (1 - x if r & 4 else x, 1 - y if r & 2 else y, 1 - c if r & 1 else c)
                copies.append(pltpu.make_async_remote_copy(
                    src_ref=ins[na], dst_ref=outs[na].at[me], send_sem=sems[2].at[r - 1], recv_sem=sems[3].at[r - 1],
                    device_id=peer, device_id_type=MESH))
            copies.append(pltpu.make_async_copy(ins[na], outs[na].at[me], sems[4].at[0]))
        return copies

    out_shapes = [jax.ShapeDtypeStruct((nk,) + g.shape[1:], BF16) for g in sums]
    chip_sems = [max(na * nk, 1)] * 2
    if small is None:
        return _simple_comm(sums, out_shapes, make_copies, chip_sems)
    out_shapes.append(jax.ShapeDtypeStruct((NDEV,) + small.shape, F32))
    return _simple_comm(list(sums) + [small], out_shapes, make_copies, chip_sems + [NDEV - 1] * 2 + [1])


def _swap_comm(totals):
    def make_copies(ins, outs, sems):
        x, y, c, _, _, _ = _place()
        return [pltpu.make_async_remote_copy(
            src_ref=outs[a].at[:, c], dst_ref=outs[a].at[:, c], send_sem=sems[0].at[a], recv_sem=sems[1].at[a],
            device_id=(x, y, 1 - c), device_id_type=MESH) for a in range(len(totals))]

    out_shapes = [jax.ShapeDtypeStruct(t.shape, F32) for t in totals]
    return _simple_comm(totals, out_shapes, make_copies, [len(totals)] * 2, aliases={a: a for a in range(len(totals))})


def _pair_sum(place, grad, landed, name):
    _, _, rows, cols = grad.shape

    def body(place_ref, g_ref, l_ref, o_ref):
        o_ref[...] = (g_ref[...] + l_ref[...]).astype(BF16)

    return pl.pallas_call(
        body, name=name,
        grid_spec=pltpu.PrefetchScalarGridSpec(
            num_scalar_prefetch=1, grid=(NCHIP,),
            in_specs=[pl.BlockSpec((None, None, rows, cols), lambda b, p: (b, p[1], 0, 0)),
                      pl.BlockSpec((None, rows, cols), lambda b, p: (b, 0, 0))],
            out_specs=pl.BlockSpec((None, rows, cols), lambda b, p: (b, 0, 0))),
        out_shape=jax.ShapeDtypeStruct((NCHIP, rows, cols), BF16), compiler_params=_cparams(1),
    )(place, grad, landed)


def _chip_sum(place, grad, landed, parts, layer, n_layers, name, prev=None):
    _, _, rows, cols = grad.shape

    def body(*refs):
        g_ref, l_ref, p_ref, o_ref = refs[1], refs[2], refs[3], refs[-1]
        tot = g_ref[...] + l_ref[...]
        for k in range(NCHIP - 1):
            tot = tot + p_ref[k].astype(F32)
        o_ref[...] = tot

    in_specs = [pl.BlockSpec((None, None, rows, cols), lambda i, p: (p[0], p[1], 0, 0)),
                pl.BlockSpec((None, rows, cols), lambda i, p: (p[0], 0, 0)),
                pl.BlockSpec((NCHIP - 1, rows, cols), lambda i, p: (0, 0, 0))]
    args = [place, grad, landed, parts]
    if prev is not None:
        in_specs.append(ANY)
        args.append(prev)
    return pl.pallas_call(
        body, name=name,
        grid_spec=pltpu.PrefetchScalarGridSpec(
            num_scalar_prefetch=1, grid=(1,), in_specs=in_specs,
            out_specs=pl.BlockSpec((None, None, rows, cols), lambda i, p: (layer, p[1], 0, 0))),
        out_shape=jax.ShapeDtypeStruct((n_layers, 2, rows, cols), F32),
        input_output_aliases={} if prev is None else {4: 0}, compiler_params=_cparams(1),
    )(*args)


def _sum_devices(parts, name):
    def body(p_ref, o_ref):
        tot = p_ref[0]
        for d in range(1, NDEV):
            tot = tot + p_ref[d]
        o_ref[...] = tot

    return pl.pallas_call(
        body, name=name, in_specs=[pl.BlockSpec(parts.shape, lambda: (0, 0, 0))],
        out_specs=pl.BlockSpec(parts.shape[1:], lambda: (0, 0)), out_shape=jax.ShapeDtypeStruct(parts.shape[1:], F32),
    )(parts)


def _pack(parts):
    rows = []
    for p in parts:
        p = p.reshape(-1, LANES)
        rows.append(jnp.pad(p, ((0, -p.shape[0] % SUBLANES), (0, 0))))
    return jnp.concatenate(rows, axis=0)


def _unpack(buf, shapes, lead=()):
    out, r0 = [], 0
    nl = len(lead)
    for shp in shapes:
        nrow = 1
        for d in shp:
            nrow *= d
        nrow //= LANES
        out.append(buf[(slice(None),) * nl + (slice(r0, r0 + nrow),)].reshape(lead + tuple(shp)))
        r0 += nrow + (-nrow % SUBLANES)
    return out


WEIGHT_ORDER = ("norm_mix_even", "w_in", "conv_a", "ln_a_g", "ln_a_b", "conv_b", "w_out", "norm_mix_odd", "w_pool",
                "pool_scale", "norm_ffn", "w_up", "conv_ffn_w", "w_down", "norm_final")
BIG = ("w_in", "w_out", "w_pool", "w_up", "w_down")
HALF = {"w_in": (D // 2, NZ // NCHIP), "w_out": (2 * A // NCHIP // 2, D), "w_pool": (PG // 2, PG),
        "w_up": (D // 2, FF2 // NCHIP), "w_down": (FF // NCHIP // 2, D)}
SMALL_SHARDED = ("conv_a", "conv_b", "conv_ffn_w", "norm_mix_odd", "pool_scale")
SMALL_ALL = ("norm_mix_even", "conv_a", "ln_a_g", "ln_a_b", "conv_b", "norm_mix_odd", "pool_scale", "norm_ffn", "conv_ffn_w",
             "norm_final")
SMALL_EARLY = ("norm_mix_odd", "pool_scale", "norm_ffn", "conv_ffn_w", "norm_final", "loss", "conv_a", "ln_a_g", "ln_a_b", "conv_b")
SMALL_LATE = ("norm_mix_even",)


def kernel(x, norm_mix_even, w_in, conv_a, ln_a_g, ln_a_b, conv_b, w_out, norm_mix_odd, w_pool, pool_scale, norm_ffn, w_up, conv_ffn_w, w_down, norm_final, loss_target, m_norm_mix_even, m_w_in, m_conv_a, m_ln_a_g, m_ln_a_b, m_conv_b, m_w_out, m_norm_mix_odd, m_w_pool, m_pool_scale, m_norm_ffn, m_w_up, m_conv_ffn_w, m_w_down, m_norm_final, v_norm_mix_even, v_w_in, v_conv_a, v_ln_a_g, v_ln_a_b, v_conv_b, v_w_out, v_norm_mix_odd, v_w_pool, v_pool_scale, v_norm_ffn, v_w_up, v_conv_ffn_w, v_w_down, v_norm_final):
    w = dict(norm_mix_even=norm_mix_even, w_in=w_in, conv_a=conv_a, ln_a_g=ln_a_g, ln_a_b=ln_a_b, conv_b=conv_b, w_out=w_out,
             norm_mix_odd=norm_mix_odd, w_pool=w_pool, pool_scale=pool_scale, norm_ffn=norm_ffn, w_up=w_up,
             conv_ffn_w=conv_ffn_w, w_down=w_down, norm_final=norm_final)
    m = dict(norm_mix_even=m_norm_mix_even, w_in=m_w_in, conv_a=m_conv_a, ln_a_g=m_ln_a_g, ln_a_b=m_ln_a_b, conv_b=m_conv_b,
             w_out=m_w_out, norm_mix_odd=m_norm_mix_odd, w_pool=m_w_pool, pool_scale=m_pool_scale, norm_ffn=m_norm_ffn,
             w_up=m_w_up, conv_ffn_w=m_conv_ffn_w, w_down=m_w_down, norm_final=m_norm_final)
    v = dict(norm_mix_even=v_norm_mix_even, w_in=v_w_in, conv_a=v_conv_a, ln_a_g=v_ln_a_g, ln_a_b=v_ln_a_b, conv_b=v_conv_b,
             w_out=v_w_out, norm_mix_odd=v_norm_mix_odd, w_pool=v_w_pool, pool_scale=v_pool_scale, norm_ffn=v_norm_ffn,
             w_up=v_w_up, conv_ffn_w=v_conv_ffn_w, w_down=v_w_down, norm_final=v_norm_final)
    chip = 2 * lax.axis_index("x") + lax.axis_index("y")
    place = jnp.stack([chip, lax.axis_index("c")]).astype(jnp.int32)

    half = lambda a, name: a.astype(BF16).reshape((2,) + HALF[name])
    shard = {"w_in": half(w_in[0], "w_in"), "w_out": half(w_out[0], "w_out")}
    small_shapes = [w[k].shape[-2:] if w[k].ndim == 3 and k != "conv_ffn_w" else (w[k].size // w[k].shape[-1], w[k].shape[-1])
                    for k in SMALL_SHARDED]
    rep = dict(norm_mix_even=norm_mix_even, ln_a_g=ln_a_g, ln_a_b=ln_a_b, norm_ffn=norm_ffn, norm_final=norm_final.reshape(1, D))
    rows_per_step = {"w_in": 512, "w_out": 256, "w_pool": 256, "w_up": 256, "w_down": 352}
    as3 = lambda a: a.reshape(a.shape[0], -1, a.shape[-1])

    def adamw(k, total):
        outs, _ = _adamw_big(as3(w[k]), as3(total), as3(m[k]), as3(v[k]), rows_per_step[k], "adamw_" + k)
        return [a.reshape(w[k].shape) for a in outs]

    grad_x, done, summed = _train_step(
        x[0], loss_target[0], place, shard, (w_up, w_down, w_pool[0]), rep, _pack([w[k] for k in SMALL_SHARDED]), small_shapes,
        adamw)
    loss = summed["loss"][0, 0]

    grad, delta, new_m, new_v = ({k: done[k][q] for k in BIG} for q in range(4))
    for k in SMALL_ALL:
        gsum = summed[k]
        if k in SMALL_SHARDED:
            cols = w[k].shape[-1]
            gsum = lax.dynamic_slice_in_dim(gsum, chip * cols, cols, axis=gsum.ndim - 1)
        grad[k] = gsum.reshape(w[k].shape)

    as2 = lambda a: a.reshape(1, -1) if a.ndim == 1 else a
    ds, ms, vs = _adamw_small(*[[as2(t[k]) for k in SMALL_ALL] for t in (w, grad, m, v)])
    for k, d2, m2, v2 in zip(SMALL_ALL, ds, ms, vs):
        delta[k], new_m[k], new_v[k] = (a.reshape(w[k].shape) for a in (d2, m2, v2))

    return (loss, grad_x[None], *[grad[k] for k in WEIGHT_ORDER], *[delta[k] for k in WEIGHT_ORDER],
            *[new_m[k] for k in WEIGHT_ORDER], *[new_v[k] for k in WEIGHT_ORDER])
```

```python
import functools

import jax
import jax.numpy as jnp
from jax import lax
from jax.experimental import pallas as pl
from jax.experimental.pallas import tpu as pltpu

F32, BF16 = jnp.float32, jnp.bfloat16

D = 1024
A = 512
NZ = 5 * A
FF = 2816
FF2 = 2 * FF
NCHIP = 4
NDEV = 8
K_A, K_S = 31, 3
POOL_WINDOWS = (2, 4, 8, 16)
PG = D // len(POOL_WINDOWS)
RMS_EPS, LN_EPS = 1e-6, 1e-5
ADAM_LR, ADAM_B1, ADAM_B2, ADAM_EPS, ADAM_WD, ADAM_STEP = 0.001, 0.9, 0.999, 1e-08, 0.01, 10

HALO_A, HALO_S, HALO_P = 32, 8, 16
SUBLANES = 8
LANES = 128
VMEM_LIMIT_BYTES = 56 * 1024 * 1024

TS_MIX = 512
TS_MIXB = 256
TS_FFN = 256
TS_POOL = 512
TS_MM = 512
R_CHUNK = 64
GATHER_MID_STEPS = {"mix0_fwd": 2, "ffn0_fwd": 5}

MESH = pl.DeviceIdType.MESH
ANY = pl.BlockSpec(memory_space=pl.ANY)
NT_DIMS = (((1,), (1,)), ((), ()))
TN_DIMS = (((0,), (0,)), ((), ()))


def _cparams(n_axes):
    return pltpu.CompilerParams(dimension_semantics=("arbitrary",) * n_axes, vmem_limit_bytes=VMEM_LIMIT_BYTES)


def _const(shape):
    nd = len(shape)
    return pl.BlockSpec(shape, lambda *_: (0,) * nd, pipeline_mode=pl.Buffered(1))


def _sigmoid(v):
    return 1.0 / (1.0 + jnp.exp(-v))


def _rsqrt_mean_sq(x):
    return lax.rsqrt(jnp.mean(x * x, axis=-1, keepdims=True) + RMS_EPS)


def _rms_bwd(dh, xh, r, g):
    dxh = dh * g
    return r * (dxh - xh * jnp.mean(dxh * xh, axis=-1, keepdims=True))


def _shifted(buf_ref, row0, rows, col0, width, offsets):
    lo = (min(offsets) // SUBLANES) * SUBLANES
    hi = -(-(max(offsets) + rows) // SUBLANES) * SUBLANES
    start = row0 + lo if isinstance(row0, int) else pl.multiple_of(row0 + lo, SUBLANES)
    win = buf_ref[pl.ds(start, hi - lo), col0:col0 + width]
    out = {}
    for res in sorted({(o - lo) % SUBLANES for o in offsets}):
        qs = {o: (o - lo) // SUBLANES for o in offsets if (o - lo) % SUBLANES == res}
        base = pltpu.roll(win, hi - lo - res, 0) if res else win
        for o, q in qs.items():
            out[o] = base[SUBLANES * q:SUBLANES * q + rows, :]
    return out


def _rowsum8(v):
    acc = v[0:SUBLANES, :]
    for r in range(SUBLANES, v.shape[0], SUBLANES):
        acc = acc + v[r:r + SUBLANES, :]
    return acc


def _taps(w_ref, sh, offsets, col0, width):
    acc = None
    for k, o in enumerate(offsets):
        term = w_ref[k:k + 1, col0:col0 + width] * sh[o]
        acc = term if acc is None else acc + term
    return acc


def _window_bases(buf_ref, row0, rows, col0, width, offsets):
    lo = (min(offsets) // SUBLANES) * SUBLANES
    hi = -(-(max(offsets) + rows) // SUBLANES) * SUBLANES
    start = row0 + lo if isinstance(row0, int) else pl.multiple_of(row0 + lo, SUBLANES)
    win = buf_ref[pl.ds(start, hi - lo), col0:col0 + width]
    for res in sorted({(o - lo) % SUBLANES for o in offsets}):
        taps = [(k, (o - lo) // SUBLANES * SUBLANES) for k, o in enumerate(offsets) if (o - lo) % SUBLANES == res]
        yield (pltpu.roll(win, hi - lo - res, 0) if res else win), taps


def _conv_acc(buf_ref, w_ref, row0, rows, col0, width, offsets):
    acc = None
    for base, taps in _window_bases(buf_ref, row0, rows, col0, width, offsets):
        for k, q in taps:
            term = w_ref[k:k + 1, col0:col0 + width] * base[q:q + rows, :]
            acc = term if acc is None else acc + term
    return acc


def _conv_corr(buf_ref, w_ref, other, acc_ref, row0, rows, col0, width, offsets):
    acc = None
    for base, taps in _window_bases(buf_ref, row0, rows, col0, width, offsets):
        for k, q in taps:
            sl = base[q:q + rows, :]
            term = w_ref[k:k + 1, col0:col0 + width] * sl
            acc = term if acc is None else acc + term
            acc_ref[SUBLANES * k:SUBLANES * (k + 1), col0:col0 + width] += _rowsum8(sl * other)
    return acc


def _finish_tap_sums(acc_ref, out_ref, n_taps):
    for k in range(n_taps):
        out_ref[k:k + 1, :] = jnp.sum(acc_ref[SUBLANES * k:SUBLANES * (k + 1), :], axis=0, keepdims=True)


class _Comm:
    def __init__(self, inputs, out_shapes, sems, start, finish, aliases=None, mid=None, mid_steps=1):
        self.inputs, self.out_shapes, self.sems = list(inputs), list(out_shapes), list(sems)
        self.start, self.finish, self.mid, self.aliases = start, finish, mid, dict(aliases or {})
        self.mid_steps = mid_steps


def _join_comm(a, b):
    ni, no, ns = len(a.inputs), len(a.out_shapes), len(a.sems)

    def both(name):
        def run(ins, outs, sems):
            if getattr(a, name) is not None:
                getattr(a, name)(ins[:ni], outs[:no], sems[:ns])
            if getattr(b, name) is not None:
                getattr(b, name)(ins[ni:], outs[no:], sems[ns:])
        return run

    aliases = {**a.aliases, **{ni + i: no + o for i, o in b.aliases.items()}}
    mid = both("mid") if (a.mid is not None or b.mid is not None) else None
    return _Comm(a.inputs + b.inputs, a.out_shapes + b.out_shapes, a.sems + b.sems, both("start"), both("finish"), aliases, mid)


def _call(body, *, name, grid, in_specs, out_specs, out_shape, args, scratch=(), comm=None, aliases=None):
    n_in, n_out, n_scr, n_axes = len(in_specs), len(out_specs), len(scratch), len(grid)
    params = pltpu.CompilerParams(dimension_semantics=("arbitrary",) * n_axes, vmem_limit_bytes=VMEM_LIMIT_BYTES)
    aliases = dict(aliases or {})
    if comm is None:
        outs = pl.pallas_call(body, name=name, grid=grid, in_specs=list(in_specs), out_specs=list(out_specs),
                              out_shape=list(out_shape), scratch_shapes=list(scratch), input_output_aliases=aliases,
                              compiler_params=params)(*args)
        return list(outs), []
    ci, co = len(comm.inputs), len(comm.out_shapes)

    def wrapped(*refs):
        k_in, c_in = refs[:n_in], refs[n_in:n_in + ci]
        o0 = n_in + ci
        k_out, c_out = refs[o0:o0 + n_out], refs[o0 + n_out:o0 + n_out + co]
        s0 = o0 + n_out + co
        k_scr, c_sem = refs[s0:s0 + n_scr], refs[s0 + n_scr:]
        first = pl.program_id(0) == 0
        last = pl.program_id(0) == grid[0] - 1
        for ax in range(1, n_axes):
            first = jnp.logical_and(first, pl.program_id(ax) == 0)
            last = jnp.logical_and(last, pl.program_id(ax) == grid[ax] - 1)

        @pl.when(first)
        def _():
            comm.start(c_in, c_out, c_sem)

        mid_early = comm.mid is not None and n_axes == 1 and grid[0] > comm.mid_steps
        if mid_early:
            @pl.when(pl.program_id(0) == grid[0] - 1 - comm.mid_steps)
            def _():
                comm.mid(c_in, c_out, c_sem)

        body(*k_in, *k_out, *k_scr)

        @pl.when(last)
        def _():
            if comm.mid is not None and not mid_early:
                comm.mid(c_in, c_out, c_sem)
            comm.finish(c_in, c_out, c_sem)

    outs = pl.pallas_call(
        wrapped, name=name, grid=grid, in_specs=list(in_specs) + [ANY] * ci, out_specs=list(out_specs) + [ANY] * co,
        out_shape=list(out_shape) + comm.out_shapes, scratch_shapes=list(scratch) + comm.sems,
        input_output_aliases={**aliases, **{n_in + i: n_out + o for i, o in comm.aliases.items()}}, compiler_params=params,
    )(*args, *comm.inputs)
    return list(outs[:n_out]), list(outs[n_out:])


def _run_comm(comm, name):
    ci, co = len(comm.inputs), len(comm.out_shapes)

    def body(*refs):
        c_in, c_out, c_sem = refs[:ci], refs[ci:ci + co], refs[ci + co:]
        comm.start(c_in, c_out, c_sem)
        if comm.mid is not None:
            comm.mid(c_in, c_out, c_sem)
        comm.finish(c_in, c_out, c_sem)

    return list(pl.pallas_call(body, name=name, in_specs=[ANY] * ci, out_specs=[ANY] * co, out_shape=comm.out_shapes,
                               scratch_shapes=comm.sems, input_output_aliases=comm.aliases)(*comm.inputs))


def _mix0_fwd(x, g, w_in, conv_a, ln_g, ln_b, conv_b, w_out, comm=None):
    s = x.shape[0]
    ts = min(TS_MIX, s)
    n = s // ts
    bw = NZ // NCHIP
    offs_a = [HALO_A - (K_A - 1) + k for k in range(K_A)]
    offs_s = [HALO_S - (K_S - 1) + k for k in range(K_S)]

    def body(x_ref, g_ref, win_ref, ca_ref, lg_ref, lb_ref, cb_ref, wout_ref,
             h_ref, z_ref, ac_ref, bconv_buf, cat_ref, x1_ref, glu_buf, cv_buf):
        i = pl.program_id(0)

        @pl.when(i == 0)
        def _():
            glu_buf[0:HALO_A, :] = jnp.zeros((HALO_A, A), F32)
            cv_buf[0:HALO_S, :] = jnp.zeros((HALO_S, A), F32)

        xv = x_ref[...]
        h = (xv * _rsqrt_mean_sq(xv) * g_ref[...]).astype(BF16)
        h_ref[...] = h
        for j in range(NCHIP):
            z_ref[:, j * bw:(j + 1) * bw] = jnp.dot(h, win_ref[j], preferred_element_type=F32)
        glu_buf[HALO_A:HALO_A + ts, :] = z_ref[:, 0:A] * _sigmoid(z_ref[:, A:2 * A])
        cv_buf[HALO_S:HALO_S + ts, :] = z_ref[:, 3 * A:4 * A] * z_ref[:, 4 * A:5 * A]

        def chunk(ci, carry):
            r0 = pl.multiple_of(ci * R_CHUNK, R_CHUNK)
            for c0 in range(0, A, LANES):
                ac_ref[pl.ds(r0, R_CHUNK), c0:c0 + LANES] = _conv_acc(glu_buf, ca_ref, r0, R_CHUNK, c0, LANES, offs_a)
                bconv_buf[pl.ds(r0, R_CHUNK), c0:c0 + LANES] = _conv_acc(cv_buf, cb_ref, r0, R_CHUNK, c0, LANES, offs_s)
            return carry

        lax.fori_loop(0, ts // R_CHUNK, chunk, 0)
        glu_buf[0:HALO_A, :] = glu_buf[ts:ts + HALO_A, :]
        cv_buf[0:HALO_S, :] = cv_buf[ts:ts + HALO_S, :]

        ac = ac_ref[...]
        xc = ac - jnp.mean(ac, axis=-1, keepdims=True)
        xn = xc * lax.rsqrt(jnp.mean(xc * xc, axis=-1, keepdims=True) + LN_EPS)
        ln = xn * lg_ref[...] + lb_ref[...]
        cat_ref[:, 0:A] = (ln * _sigmoid(ln)).astype(BF16)
        cat_ref[:, A:2 * A] = (z_ref[:, 2 * A:3 * A] * bconv_buf[...]).astype(BF16)
        x1_ref[...] = xv + jnp.dot(cat_ref[...], wout_ref[...], preferred_element_type=F32)

    tile = lambda w: pl.BlockSpec((ts, w), lambda i: (i, 0))
    return _call(
        body, name="mix0_fwd", grid=(n,), comm=comm, args=(x, g, w_in, conv_a, ln_g, ln_b, conv_b, w_out),
        in_specs=[tile(D), _const((1, D)), _const((NCHIP, D, bw)), _const((K_A, A)), _const((1, A)), _const((1, A)),
                  _const((K_S, A)), _const((2 * A, D))],
        out_specs=[tile(D), tile(NZ), tile(A), tile(A), tile(2 * A), tile(D)],
        out_shape=[jax.ShapeDtypeStruct((s, D), BF16), jax.ShapeDtypeStruct((s, NZ), F32), jax.ShapeDtypeStruct((s, A), F32),
                   jax.ShapeDtypeStruct((s, A), F32), jax.ShapeDtypeStruct((s, 2 * A), BF16), jax.ShapeDtypeStruct((s, D), F32)],
        scratch=[pltpu.VMEM((HALO_A + ts, A), F32), pltpu.VMEM((HALO_S + ts, A), F32)])


def _ffn_fwd(x, g, w_up, wc, w_down, name, comm=None, head=None):
    s = x.shape[0]
    ts = min(TS_FFN, s)
    n = s // ts
    bw = FF2 // NCHIP
    rows = 32
    offs = [HALO_S - (K_S - 1) + k for k in range(K_S)]
    n_in = 5 + (2 if head else 0)

    def body(*refs):
        x_ref, g_ref, wup_ref, wc_ref, wdn_ref = refs[:5]
        h_ref, u0_ref, u_ref, act_ref, xo_ref = refs[n_in:n_in + 5]
        cbuf = refs[-1]
        i = pl.program_id(0)

        @pl.when(i == 0)
        def _():
            cbuf[0:HALO_S, :] = jnp.zeros((HALO_S, FF2), F32)
            if head:
                refs[n_in + 5][...] = jnp.zeros((1, D), F32)
                refs[n_in + 6][...] = jnp.zeros((1, LANES), F32)

        xv = x_ref[...]
        h = (xv * _rsqrt_mean_sq(xv) * g_ref[...]).astype(BF16)
        h_ref[...] = h
        f = None
        for p in range(NCHIP // 2):
            for j in (p, NCHIP // 2 + p):
                zc = jnp.dot(h, wup_ref[j], preferred_element_type=F32)
                u0_ref[:, j * bw:(j + 1) * bw] = zc.astype(BF16)
                cbuf[HALO_S:HALO_S + ts, j * bw:(j + 1) * bw] = zc
            for r0 in range(0, ts, rows):
                for c0 in range(p * bw, (p + 1) * bw, LANES):
                    ug = _taps(wc_ref, _shifted(cbuf, r0, rows, c0, LANES, offs), offs, c0, LANES)
                    uv = _taps(wc_ref, _shifted(cbuf, r0, rows, FF + c0, LANES, offs), offs, FF + c0, LANES)
                    u_ref[r0:r0 + rows, c0:c0 + LANES] = ug.astype(BF16)
                    u_ref[r0:r0 + rows, FF + c0:FF + c0 + LANES] = uv.astype(BF16)
                    act_ref[r0:r0 + rows, c0:c0 + LANES] = (ug * _sigmoid(ug) * uv).astype(BF16)
            fp = jnp.dot(act_ref[:, p * bw:(p + 1) * bw], wdn_ref[p * bw:(p + 1) * bw, :], preferred_element_type=F32)
            f = fp if f is None else f + fp
        cbuf[0:HALO_S, :] = cbuf[ts:ts + HALO_S, :]
        if not head:
            xo_ref[...] = xv + f
        else:
            gf_ref, t_ref, dgf_ref, loss_ref = refs[5], refs[6], refs[n_in + 5], refs[n_in + 6]
            xo = xv + f
            r = _rsqrt_mean_sq(xo)
            xh = xo * r
            gv = gf_ref[...]
            err = xh * gv - t_ref[...]
            loss_ref[...] += jnp.sum(jnp.sum(err * err, axis=1, keepdims=True), axis=0, keepdims=True) * (0.5 / D)
            dy = err * (1.0 / D)
            dgf_ref[...] += jnp.sum(dy * xh, axis=0, keepdims=True)
            xo_ref[...] = _rms_bwd(dy, xh, r, gv)

    tile = lambda w: pl.BlockSpec((ts, w), lambda i: (i, 0))
    one_row = lambda w: pl.BlockSpec((1, w), lambda i: (0, 0))
    return _call(
        body, name=name, grid=(n,), comm=comm, args=(x, g, w_up, wc, w_down) + (tuple(head) if head else ()),
        in_specs=[tile(D), _const((1, D)), _const((NCHIP, D, bw)), _const((K_S, FF2)), _const((FF, D))]
        + ([_const((1, D)), tile(D)] if head else []),
        out_specs=[tile(D), tile(FF2), tile(FF2), tile(FF), tile(D)] + ([one_row(D), one_row(LANES)] if head else []),
        out_shape=[jax.ShapeDtypeStruct((s, D), BF16), jax.ShapeDtypeStruct((s, FF2), BF16), jax.ShapeDtypeStruct((s, FF2), BF16),
                   jax.ShapeDtypeStruct((s, FF), BF16), jax.ShapeDtypeStruct((s, D), F32)]
        + ([jax.ShapeDtypeStruct((1, D), F32), jax.ShapeDtypeStruct((1, LANES), F32)] if head else []),
        scratch=[pltpu.VMEM((HALO_S + ts, FF2), F32)])


def _pool_windows(hbuf, pbuf, tile_row0, ts):
    def chunk(ci, carry):
        r0 = pl.multiple_of(ci * R_CHUNK, R_CHUNK)
        t1 = (tile_row0 + r0 + lax.broadcasted_iota(jnp.int32, (R_CHUNK, 1), 0) + 1).astype(F32)
        for gi, w in enumerate(POOL_WINDOWS):
            cnt = jnp.minimum(t1, float(w))
            offs = [HALO_P - jj for jj in range(w)]
            for c0 in range(gi * PG, (gi + 1) * PG, LANES):
                sh = _shifted(hbuf, r0, R_CHUNK, c0, LANES, offs)
                tot = sh[offs[0]]
                for o in offs[1:]:
                    tot = tot + sh[o]
                pbuf[pl.ds(r0, R_CHUNK), c0:c0 + LANES] = (tot / cnt - sh[HALO_P]).astype(BF16)
        return carry

    lax.fori_loop(0, ts // R_CHUNK, chunk, 0)


def _assemble_wpool(wp_ref, wps):
    rb = PG // NCHIP
    for gi in range(len(POOL_WINDOWS)):
        for j in range(NCHIP):
            wps[gi, j * rb:(j + 1) * rb, :] = wp_ref[j, gi]


def _pool_fwd(x, g, w_pool, scale):
    s = x.shape[0]
    ts = min(TS_POOL, s)
    n = s // ts
    ng = len(POOL_WINDOWS)

    def body(x_ref, g_ref, wp_ref, sc_ref, xo_ref, hbuf, pbuf, wps):
        i = pl.program_id(0)

        @pl.when(i == 0)
        def _():
            hbuf[0:HALO_P, :] = jnp.zeros((HALO_P, D), F32)
            _assemble_wpool(wp_ref, wps)

        xv = x_ref[...]
        hbuf[HALO_P:HALO_P + ts, :] = xv * _rsqrt_mean_sq(xv) * g_ref[...]
        _pool_windows(hbuf, pbuf, i * ts, ts)
        hbuf[0:HALO_P, :] = hbuf[ts:ts + HALO_P, :]
        for gi in range(ng):
            cols = slice(gi * PG, (gi + 1) * PG)
            y = jnp.dot(pbuf[:, cols], wps[gi], preferred_element_type=F32)
            xo_ref[:, cols] = xv[:, cols] + y * sc_ref[:, cols]

    tile = pl.BlockSpec((ts, D), lambda i: (i, 0))
    return pl.pallas_call(
        body, name="pool_fwd", grid=(n,),
        in_specs=[tile, _const((1, D)), _const((NCHIP, ng, PG // NCHIP, PG)), _const((1, D))],
        out_specs=tile, out_shape=jax.ShapeDtypeStruct((s, D), F32),
        scratch_shapes=[pltpu.VMEM((HALO_P + ts, D), F32), pltpu.VMEM((ts, D), BF16), pltpu.VMEM((ng, PG, PG), BF16)],
        compiler_params=_cparams(1),
    )(x, g, w_pool, scale)


def _ffn_bwd_a(dxo, u, u0, wc, w_down, name, comm=None):
    s = dxo.shape[0]
    ts = min(TS_FFN, s)
    n = s // ts
    cw = FF2 // NCHIP
    rows = 32
    lw2 = 2 * LANES
    boffs = [K_S - 1 - k for k in range(K_S)]

    def body(dxo_ref, u_ref, u0_ref, wc_ref, wdn_ref, du0_ref, dwc_ref, dubuf, dact, dwacc):
        i = pl.program_id(0)

        @pl.when(i == 0)
        def _():
            dubuf[ts:ts + HALO_S, :] = jnp.zeros((HALO_S, FF2), F32)
            dwacc[...] = jnp.zeros(dwacc.shape, F32)

        df = dxo_ref[...].astype(BF16)
        for cg in range(0, FF, cw):
            dact[...] = lax.dot_general(df, wdn_ref[cg:cg + cw, :], NT_DIMS, preferred_element_type=F32)

            def chunk(ci, carry, cg=cg):
                rs = pl.ds(pl.multiple_of(ci * rows, rows), rows)
                for c in range(cg, cg + cw, LANES):
                    ug, uv = u_ref[rs, c:c + LANES].astype(F32), u_ref[rs, FF + c:FF + c + LANES].astype(F32)
                    sg = _sigmoid(ug)
                    da = dact[rs, c - cg:c - cg + LANES]
                    gs = ug * sg
                    dubuf[rs, c:c + LANES] = da * uv * (sg + gs * (1.0 - sg))
                    dubuf[rs, FF + c:FF + c + LANES] = da * gs
                return carry

            lax.fori_loop(0, ts // rows, chunk, 0)

        def chunk2(ci, carry):
            r0 = pl.multiple_of(ci * rows, rows)
            rs = pl.ds(r0, rows)
            for c in range(0, FF2, lw2):
                sh = _shifted(dubuf, r0, rows, c, lw2, boffs)
                du0_ref[rs, c:c + lw2] = _taps(wc_ref, sh, boffs, c, lw2).astype(BF16)
                u0v = u0_ref[rs, c:c + lw2].astype(F32)
                for k, o in enumerate(boffs):
                    dwacc[SUBLANES * k:SUBLANES * (k + 1), c:c + lw2] += _rowsum8(sh[o] * u0v)
            return carry

        lax.fori_loop(0, ts // rows, chunk2, 0)
        dubuf[ts:ts + HALO_S, :] = dubuf[0:HALO_S, :]

        @pl.when(i == n - 1)
        def _():
            _finish_tap_sums(dwacc, dwc_ref, K_S)

    rev = lambda w: pl.BlockSpec((ts, w), lambda i: (n - 1 - i, 0))
    return _call(
        body, name=name, grid=(n,), comm=comm, args=(dxo, u, u0, wc, w_down),
        in_specs=[rev(D), rev(FF2), rev(FF2), _const((K_S, FF2)), _const((FF, D))],
        out_specs=[rev(FF2), pl.BlockSpec((K_S, FF2), lambda i: (0, 0))],
        out_shape=[jax.ShapeDtypeStruct((s, FF2), BF16), jax.ShapeDtypeStruct((K_S, FF2), F32)],
        scratch=[pltpu.VMEM((ts + HALO_S, FF2), F32), pltpu.VMEM((ts, cw), F32), pltpu.VMEM((SUBLANES * K_S, FF2), F32)])


def _nt_rms_bwd(dy, w, x, g, dres, name, comm=None, tiles=None, dx_so_far=None):
    s = x.shape[0]
    ts = min(TS_MM, s)
    first, n = (0, s // ts) if tiles is None else tiles
    nw = dy.shape[1]
    bw = nw // NCHIP

    def body(dy_ref, w_ref, x_ref, g_ref, dres_ref, dx_ref, dg_ref):
        i = pl.program_id(0)

        @pl.when(i == 0)
        def _():
            dg_ref[...] = jnp.zeros((1, D), F32)

        dh = lax.dot_general(dy_ref[:, 0:bw], w_ref[0], NT_DIMS, preferred_element_type=F32)
        for j in range(1, NCHIP):
            dh = dh + lax.dot_general(dy_ref[:, j * bw:(j + 1) * bw], w_ref[j], NT_DIMS, preferred_element_type=F32)
        xv = x_ref[...]
        r = _rsqrt_mean_sq(xv)
        xh = xv * r
        dg_ref[...] += jnp.sum(dh * xh, axis=0, keepdims=True)
        dx_ref[...] = dres_ref[...] + _rms_bwd(dh, xh, r, g_ref[...])

    def body_with_alias(dy_ref, w_ref, x_ref, g_ref, dres_ref, _, dx_ref, dg_ref):
        body(dy_ref, w_ref, x_ref, g_ref, dres_ref, dx_ref, dg_ref)

    tile = lambda wd: pl.BlockSpec((ts, wd), lambda i: (first + i, 0))
    in_specs = [tile(nw), _const((NCHIP, D, bw)), tile(D), _const((1, D)), tile(D)]
    more = dx_so_far is not None
    return _call(
        body_with_alias if more else body, name=name, grid=(n,), comm=comm,
        args=(dy, w, x, g, dres) + ((dx_so_far,) if more else ()), in_specs=in_specs + [ANY] * more,
        out_specs=[tile(D), pl.BlockSpec((1, D), lambda i: (0, 0))], aliases={5: 0} if more else None,
        out_shape=[jax.ShapeDtypeStruct((s, D), F32), jax.ShapeDtypeStruct((1, D), F32)])


def _wgrad(a, b, n_blocks, name, comm=None):
    s, m = a.shape
    bw = b.shape[1] // n_blocks
    tk = min(TS_MM, s)

    def body(a_ref, b_ref, o_ref):
        @pl.when(pl.program_id(0) == 0)
        def _():
            o_ref[...] = jnp.zeros(o_ref.shape, F32)

        for j in range(n_blocks):
            o_ref[j] += lax.dot_general(a_ref[...], b_ref[:, j * bw:(j + 1) * bw].astype(BF16), TN_DIMS,
                                        preferred_element_type=F32)

    (out,), comm_out = _call(
        body, name=name, grid=(s // tk,), comm=comm, args=(a, b),
        in_specs=[pl.BlockSpec((tk, m), lambda k: (k, 0)), pl.BlockSpec((tk, b.shape[1]), lambda k: (k, 0))],
        out_specs=[pl.BlockSpec((n_blocks, m, bw), lambda k: (0, 0, 0))],
        out_shape=[jax.ShapeDtypeStruct((n_blocks, m, bw), F32)])
    return out, comm_out


def _pool_bwd(dxo, x, g, w_pool, scale, comm=None):
    s = x.shape[0]
    ts = min(TS_POOL, s)
    n = s // ts
    ng = len(POOL_WINDOWS)
    rb = PG // NCHIP

    def body(dxo_ref, x_ref, halo_ref, g_ref, wp_ref, sc_ref, dx_ref, dwp_ref, dsc_ref, dg_ref,
             hbuf, pbuf, qbuf, dhbuf, wps, dwacc):
        i = pl.program_id(0)
        j = n - 1 - i

        @pl.when(i == 0)
        def _():
            qbuf[ts:ts + HALO_P, :] = jnp.zeros((HALO_P, D), F32)
            dwacc[...] = jnp.zeros(dwacc.shape, F32)
            dsc_ref[...] = jnp.zeros((1, D), F32)
            dg_ref[...] = jnp.zeros((1, D), F32)
            _assemble_wpool(wp_ref, wps)

        gv = g_ref[...]
        xl = halo_ref[...]
        hbuf[0:HALO_P, :] = jnp.where(j == 0, 0.0, xl * _rsqrt_mean_sq(xl) * gv)
        xv = x_ref[...]
        r = _rsqrt_mean_sq(xv)
        xh = xv * r
        hbuf[HALO_P:HALO_P + ts, :] = xh * gv
        _pool_windows(hbuf, pbuf, j * ts, ts)

        dy = dxo_ref[...]
        t1 = (j * ts + lax.broadcasted_iota(jnp.int32, (ts, 1), 0) + 1).astype(F32)
        for gi, w in enumerate(POOL_WINDOWS):
            cols = slice(gi * PG, (gi + 1) * PG)
            p = pbuf[:, cols]
            y = jnp.dot(p, wps[gi], preferred_element_type=F32)
            dsc_ref[:, cols] += jnp.sum(dy[:, cols] * y, axis=0, keepdims=True)
            dq = (dy[:, cols] * sc_ref[:, cols]).astype(BF16)
            dwacc[gi] += lax.dot_general(p, dq, TN_DIMS, preferred_element_type=F32)
            dp = lax.dot_general(dq, wps[gi], NT_DIMS, preferred_element_type=F32)
            qbuf[0:ts, cols] = dp / jnp.minimum(t1, float(w))

        def chunk(ci, carry):
            r0 = pl.multiple_of(ci * R_CHUNK, R_CHUNK)
            tc = (j * ts + r0 + lax.broadcasted_iota(jnp.int32, (R_CHUNK, 1), 0) + 1).astype(F32)
            for gi, w in enumerate(POOL_WINDOWS):
                cnt = jnp.minimum(tc, float(w))
                offs = list(range(w))
                for c0 in range(gi * PG, (gi + 1) * PG, LANES):
                    sh = _shifted(qbuf, r0, R_CHUNK, c0, LANES, offs)
                    tot = sh[0]
                    for o in offs[1:]:
                        tot = tot + sh[o]
                    dhbuf[pl.ds(r0, R_CHUNK), c0:c0 + LANES] = tot - sh[0] * cnt
            return carry

        lax.fori_loop(0, ts // R_CHUNK, chunk, 0)
        qbuf[ts:ts + HALO_P, :] = qbuf[0:HALO_P, :]
        dh = dhbuf[...]
        dg_ref[...] += jnp.sum(dh * xh, axis=0, keepdims=True)
        dx_ref[...] = dy + _rms_bwd(dh, xh, r, gv)

        @pl.when(i == n - 1)
        def _():
            for gi in range(ng):
                for jj in range(NCHIP):
                    dwp_ref[jj, gi] = dwacc[gi, jj * rb:(jj + 1) * rb, :]

    rev = pl.BlockSpec((ts, D), lambda i: (n - 1 - i, 0))
    halo = pl.BlockSpec((HALO_P, D), lambda i: (jnp.maximum((n - 1 - i) * (ts // HALO_P) - 1, 0), 0))
    vec = pl.BlockSpec((1, D), lambda i: (0, 0))
    return _call(
        body, name="pool_bwd", grid=(n,), comm=comm, args=(dxo, x, x, g, w_pool, scale),
        in_specs=[rev, rev, halo, _const((1, D)), _const((NCHIP, ng, rb, PG)), _const((1, D))],
        out_specs=[rev, pl.BlockSpec((NCHIP, ng, rb, PG), lambda i: (0, 0, 0, 0)), vec, vec],
        out_shape=[jax.ShapeDtypeStruct((s, D), F32), jax.ShapeDtypeStruct((NCHIP, ng, rb, PG), F32),
                   jax.ShapeDtypeStruct((1, D), F32), jax.ShapeDtypeStruct((1, D), F32)],
        scratch=[pltpu.VMEM((HALO_P + ts, D), F32), pltpu.VMEM((ts, D), BF16), pltpu.VMEM((ts + HALO_P, D), F32),
                 pltpu.VMEM((ts, D), F32), pltpu.VMEM((ng, PG, PG), BF16), pltpu.VMEM((ng, PG, PG), F32)])


def _mix0_bwd_a(dx1, z, ac, bconv, w_out, conv_a, ln_g, ln_b, conv_b, comm=None):
    s = dx1.shape[0]
    ts = min(TS_MIXB, s)
    n = s // ts
    rows = 32
    boffs_a = [K_A - 1 - k for k in range(K_A)]
    boffs_s = [K_S - 1 - k for k in range(K_S)]

    def body(dx1_ref, z_ref, ac_ref, bconv_ref, wout_ref, ca_ref, lg_ref, lb_ref, cb_ref,
             dz_ref, dca_ref, dlg_ref, dlb_ref, dcb_ref, dac_buf, dbc_buf, dca_acc, dcb_acc):
        i = pl.program_id(0)

        @pl.when(i == 0)
        def _():
            dac_buf[ts:ts + HALO_A, :] = jnp.zeros((HALO_A, A), F32)
            dbc_buf[ts:ts + HALO_S, :] = jnp.zeros((HALO_S, A), F32)
            dca_acc[...] = jnp.zeros(dca_acc.shape, F32)
            dcb_acc[...] = jnp.zeros(dcb_acc.shape, F32)
            dlg_ref[...] = jnp.zeros((1, A), F32)
            dlb_ref[...] = jnp.zeros((1, A), F32)

        dcat = lax.dot_general(dx1_ref[...].astype(BF16), wout_ref[...], NT_DIMS, preferred_element_type=F32)
        db = dcat[:, A:2 * A]
        dz_ref[:, 2 * A:3 * A] = (db * bconv_ref[...]).astype(BF16)
        dbc_buf[0:ts, :] = db * z_ref[:, 2 * A:3 * A]
        ac = ac_ref[...]
        xc = ac - jnp.mean(ac, axis=-1, keepdims=True)
        rstd = lax.rsqrt(jnp.mean(xc * xc, axis=-1, keepdims=True) + LN_EPS)
        xn = xc * rstd
        lg = lg_ref[...]
        ln = xn * lg + lb_ref[...]
        sl = _sigmoid(ln)
        dln = dcat[:, 0:A] * sl * (1.0 + ln * (1.0 - sl))
        dlg_ref[...] += jnp.sum(dln * xn, axis=0, keepdims=True)
        dlb_ref[...] += jnp.sum(dln, axis=0, keepdims=True)
        dxn = dln * lg
        dac_buf[0:ts, :] = rstd * (dxn - jnp.mean(dxn, axis=-1, keepdims=True)
                                   - xn * jnp.mean(dxn * xn, axis=-1, keepdims=True))

        def chunk(ci, carry):
            r0 = pl.multiple_of(ci * rows, rows)
            rs = pl.ds(r0, rows)
            for c0 in range(0, A, LANES):
                col = lambda grp: slice(grp * A + c0, grp * A + c0 + LANES)
                a_val, sg = z_ref[rs, col(0)], _sigmoid(z_ref[rs, col(1)])
                dglu = _conv_corr(dac_buf, ca_ref, a_val * sg, dca_acc, r0, rows, c0, LANES, boffs_a)
                dz_ref[rs, col(0)] = (dglu * sg).astype(BF16)
                dz_ref[rs, col(1)] = (dglu * a_val * sg * (1.0 - sg)).astype(BF16)
                c_gate, bc_val = z_ref[rs, col(3)], z_ref[rs, col(4)]
                dcv = _conv_corr(dbc_buf, cb_ref, c_gate * bc_val, dcb_acc, r0, rows, c0, LANES, boffs_s)
                dz_ref[rs, col(3)] = (dcv * bc_val).astype(BF16)
                dz_ref[rs, col(4)] = (dcv * c_gate).astype(BF16)
            return carry

        lax.fori_loop(0, ts // rows, chunk, 0)
        dac_buf[ts:ts + HALO_A, :] = dac_buf[0:HALO_A, :]
        dbc_buf[ts:ts + HALO_S, :] = dbc_buf[0:HALO_S, :]

        @pl.when(i == n - 1)
        def _():
            _finish_tap_sums(dca_acc, dca_ref, K_A)
            _finish_tap_sums(dcb_acc, dcb_ref, K_S)

    rev = lambda w: pl.BlockSpec((ts, w), lambda i: (n - 1 - i, 0))
    full = lambda r, c: pl.BlockSpec((r, c), lambda i: (0, 0))
    return _call(
        body, name="mix0_bwd_a", grid=(n,), comm=comm, args=(dx1, z, ac, bconv, w_out, conv_a, ln_g, ln_b, conv_b),
        in_specs=[rev(D), rev(NZ), rev(A), rev(A), _const((2 * A, D)), _const((K_A, A)), _const((1, A)), _const((1, A)),
                  _const((K_S, A))],
        out_specs=[rev(NZ), full(K_A, A), full(1, A), full(1, A), full(K_S, A)],
        out_shape=[jax.ShapeDtypeStruct((s, NZ), BF16), jax.ShapeDtypeStruct((K_A, A), F32), jax.ShapeDtypeStruct((1, A), F32),
                   jax.ShapeDtypeStruct((1, A), F32), jax.ShapeDtypeStruct((K_S, A), F32)],
        scratch=[pltpu.VMEM((ts + HALO_A, A), F32), pltpu.VMEM((ts + HALO_S, A), F32), pltpu.VMEM((SUBLANES * K_A, A), F32),
                 pltpu.VMEM((SUBLANES * K_S, A), F32)])


def _cast_later_weights(w_up, w_down, w_pool, comm):
    nl = w_up.shape[0]

    def body(up_ref, dn_ref, pool_ref, *outs):
        up_o, dn_o, pool_o = outs[:nl], outs[nl:2 * nl], outs[2 * nl]
        for layer in range(nl):
            @pl.when(pl.program_id(0) == layer)
            def _(layer=layer):
                up_o[layer][...] = up_ref[...].astype(BF16)
                dn_o[layer][...] = dn_ref[...].astype(BF16)
                if layer == 0:
                    pool_o[...] = pool_ref[...].astype(BF16)

    per_layer = lambda a: pl.BlockSpec((None,) + a.shape[1:], lambda l: (l,) + (0,) * (a.ndim - 1))
    whole = lambda shape: pl.BlockSpec(shape, lambda l: (0,) * len(shape))
    out_shapes = [w_up.shape[1:]] * nl + [w_down.shape[1:]] * nl + [w_pool.shape]
    return _call(body, name="cast_later_weights", grid=(nl,), comm=comm, args=(w_up, w_down, w_pool),
                 in_specs=[per_layer(w_up), per_layer(w_down), whole(w_pool.shape)],
                 out_specs=[whole(shape) for shape in out_shapes],
                 out_shape=[jax.ShapeDtypeStruct(shape, BF16) for shape in out_shapes])


def _train_step(x, target, place, shard, later, rep, small_pack, small_shapes, adamw):
    bw_up, bw_in = FF2 // NCHIP, NZ // NCHIP
    five = lambda a, k: a.reshape(NCHIP, 2, *HALF[k])

    half = lambda a, name: a.reshape((2,) + HALF[name])
    (up0, up1, dn0, dn1, pool_bf), (g_in, g_out, small_g) = _cast_later_weights(
        *later, comm=_gather_comm([shard["w_in"], shard["w_out"]], small_pack))
    shard = dict(shard, w_pool=half(pool_bf, "w_pool"), w_up0=half(up0, "w_up"), w_up1=half(up1, "w_up"),
                 w_down0=half(dn0, "w_down"), w_down1=half(dn1, "w_down"))
    whole = {}
    for k, part in zip(SMALL_SHARDED, _unpack(small_g, small_shapes, lead=(NCHIP,))):
        whole[k] = jnp.moveaxis(part, 0, 1).reshape(part.shape[1], NCHIP * part.shape[2])
    w_in, w_out = g_in.reshape(NCHIP, D, bw_in), g_out.reshape(2 * A, D)
    conv_ffn = whole["conv_ffn_w"].reshape(2, K_S, FF2)
    nffn = [rep["norm_ffn"][0:1], rep["norm_ffn"][1:2]]

    (h0, z, ac, bconv, cat, x1), (g_up0, g_dn0) = _mix0_fwd(
        x, rep["norm_mix_even"], w_in, whole["conv_a"], rep["ln_a_g"], rep["ln_a_b"], whole["conv_b"], w_out,
        comm=_gather_comm([shard["w_up0"], shard["w_down0"]], mid_steps=GATHER_MID_STEPS["mix0_fwd"]))
    w_up0, w_dn0 = g_up0.reshape(NCHIP, D, bw_up), g_dn0.reshape(FF, D)
    (hf0, u00, u0, act0, x2), (g_pool, g_up1, g_dn1) = _ffn_fwd(
        x1, nffn[0], w_up0, conv_ffn[0], w_dn0, "ffn0_fwd",
        comm=_gather_comm([shard["w_pool"], shard["w_up1"], shard["w_down1"]], mid_steps=GATHER_MID_STEPS["ffn0_fwd"]))
    w_pool = g_pool.reshape(NCHIP, len(POOL_WINDOWS), PG // NCHIP, PG)
    w_up1, w_dn1 = g_up1.reshape(NCHIP, D, bw_up), g_dn1.reshape(FF, D)
    x3 = _pool_fwd(x2, whole["norm_mix_odd"], w_pool, whole["pool_scale"])
    (hf1, u01, u1, act1, dx4, g_nfin, loss_part), _ = _ffn_fwd(x3, nffn[1], w_up1, conv_ffn[1], w_dn1, "ffn1_fwd",
                                                                head=(rep["norm_final"], target))

    psum = lambda k, g, ld, tag="": _pair_sum(place, g, ld, "pair_sum_" + k + tag)
    tot = lambda k, g, ld, p, layer=0, nl=1, prev=None: _chip_sum(place, g, ld, p, layer, nl, "chip_sum_%s%d" % (k, layer), prev)
    pair, chips, join = _pair_comm, _chips_comm, _join_comm

    gr_dn1 = five(_wgrad(act1, dx4, 1, "wgrad_down1")[0], "w_down")
    (du01, g_wc1), _ = _ffn_bwd_a(dx4, u1, u01, conv_ffn[1], w_dn1, "ffn1_bwd_a")
    g_up, (ld_dn1,) = _wgrad(hf1, du01, NCHIP, "wgrad_up1", comm=pair([gr_dn1]))
    gr_up1 = five(g_up, "w_up")
    s_dn1 = psum("w_down", gr_dn1, ld_dn1, "1")
    (dx3, g_nf1), (p_dn1, ld_up1) = _nt_rms_bwd(du01, w_up1, x3, nffn[1], dx4, "ffn1_bwd_b",
                                                comm=join(chips([s_dn1]), pair([gr_up1])))
    s_up1 = psum("w_up", gr_up1, ld_up1, "1")
    (dx2, g_wpool, g_scale, g_nmo), _ = _pool_bwd(dx3, x2, whole["norm_mix_odd"], w_pool, whole["pool_scale"])
    gr_pool = five(g_wpool, "w_pool")
    (du00, g_wc0), (p_up1,) = _ffn_bwd_a(dx2, u0, u00, conv_ffn[0], w_dn0, "ffn0_bwd_a", comm=chips([s_up1]))
    gr_dn0 = five(_wgrad(act0, dx2, 1, "wgrad_down0")[0], "w_down")
    g_up, (ld_dn0, ld_pool) = _wgrad(hf0, du00, NCHIP, "wgrad_up0", comm=pair([gr_dn0, gr_pool]))
    gr_up0 = five(g_up, "w_up")
    s_dn0, s_pool = psum("w_down", gr_dn0, ld_dn0, "0"), psum("w_pool", gr_pool, ld_pool)
    (dx1, g_nf0), (p_dn0, p_pool, ld_up0) = _nt_rms_bwd(du00, w_up0, x1, nffn[0], dx2, "ffn0_bwd_b",
                                                        comm=join(chips([s_dn0, s_pool]), pair([gr_up0])))
    s_up0 = psum("w_up", gr_up0, ld_up0, "0")
    gr_out = five(_wgrad(cat, dx1, 1, "wgrad_out")[0], "w_out")
    (dz, g_ca, g_lg, g_lb, g_cb), (p_up0, ld_out) = _mix0_bwd_a(
        dx1, z, ac, bconv, w_out, whole["conv_a"], rep["ln_a_g"], rep["ln_a_b"], whole["conv_b"],
        comm=join(chips([s_up0]), pair([gr_out])))
    s_out = psum("w_out", gr_out, ld_out)
    t_pool = tot("w_pool", gr_pool, ld_pool, p_pool)
    t_up = tot("w_up", gr_up0, ld_up0, p_up0, 0, 2, tot("w_up", gr_up1, ld_up1, p_up1, 1, 2))
    t_dn = tot("w_down", gr_dn0, ld_dn0, p_dn0, 0, 2, tot("w_down", gr_dn1, ld_dn1, p_dn1, 1, 2))
    small = {"norm_mix_odd": g_nmo, "pool_scale": g_scale, "norm_ffn": jnp.concatenate([g_nf0, g_nf1], axis=0),
             "conv_ffn_w": jnp.stack([g_wc0, g_wc1]), "norm_final": g_nfin, "loss": loss_part,
             "conv_a": g_ca, "ln_a_g": g_lg, "ln_a_b": g_lb, "conv_b": g_cb}
    g_in, (p_out, early_all, t_pool, t_up, t_dn) = _wgrad(
        h0, dz, NCHIP, "wgrad_in",
        comm=join(chips([s_out], _pack([small[k] for k in SMALL_EARLY])), _swap_comm([t_pool, t_up, t_dn])))
    gr_in = five(g_in, "w_in")
    n_mm = x.shape[0] // min(TS_MM, x.shape[0])
    (dx_a, g_nme_a), (ld_in,) = _nt_rms_bwd(dz, w_in, x, rep["norm_mix_even"], dx1, "mix0_bwd_b0", comm=pair([gr_in]),
                                            tiles=(0, n_mm // 2))
    s_in = psum("w_in", gr_in, ld_in)
    (grad_x, g_nme_b), (p_in,) = _nt_rms_bwd(dz, w_in, x, rep["norm_mix_even"], dx1, "mix0_bwd_b1", comm=chips([s_in]),
                                             tiles=(n_mm // 2, n_mm - n_mm // 2), dx_so_far=dx_a)
    small["norm_mix_even"] = g_nme_a + g_nme_b
    (late_all,) = _run_comm(chips([], _pack([small[k] for k in SMALL_LATE])), "rs_last")
    t_in, t_out = _run_comm(_swap_comm([tot("w_in", gr_in, ld_in, p_in), tot("w_out", gr_out, ld_out, p_out)]), "rs_swap")
    done = {k: adamw(k, t) for k, t in (("w_up", t_up), ("w_down", t_dn), ("w_pool", t_pool), ("w_in", t_in), ("w_out", t_out))}
    summed = dict(zip(SMALL_EARLY, _unpack(_sum_devices(early_all, "sum_small_early"), [small[k].shape for k in SMALL_EARLY])))
    summed.update(zip(SMALL_LATE, _unpack(_sum_devices(late_all, "sum_small_late"), [small[k].shape for k in SMALL_LATE])))
    return grad_x, done, summed


def _adamw_math(w, g, m, v):
    m = ADAM_B1 * m + (1.0 - ADAM_B1) * g
    v = ADAM_B2 * v + (1.0 - ADAM_B2) * (g * g)
    m_hat = m / (1.0 - ADAM_B1 ** ADAM_STEP)
    v_hat = v / (1.0 - ADAM_B2 ** ADAM_STEP)
    return -ADAM_LR * (m_hat / (jnp.sqrt(v_hat) + ADAM_EPS) + ADAM_WD * w), m, v


def _adamw_big(w, g, m, v, tr, name, comm=None):
    nl, rows, cols = w.shape

    def body(w_ref, g_ref, m_ref, v_ref, g2_ref, d_ref, m2_ref, v2_ref):
        gv = g_ref[...]
        g2_ref[...] = gv
        d_ref[...], m2_ref[...], v2_ref[...] = _adamw_math(w_ref[...], gv, m_ref[...], v_ref[...])

    spec = pl.BlockSpec((None, tr, cols), lambda l, r: (l, r, 0))
    return _call(body, name=name, grid=(nl, rows // tr), comm=comm, args=(w, g, m, v), in_specs=[spec] * 4,
                 out_specs=[spec] * 4, out_shape=[jax.ShapeDtypeStruct(w.shape, F32)] * 4)


def _adamw_small(ws, gs, ms, vs):
    n = len(ws)

    def body(*refs):
        for p in range(n):
            w_ref, g_ref, m_ref, v_ref = (refs[q * n + p] for q in range(4))
            d_ref, m2_ref, v2_ref = (refs[(4 + q) * n + p] for q in range(3))
            d_ref[...], m2_ref[...], v2_ref[...] = _adamw_math(w_ref[...], g_ref[...], m_ref[...], v_ref[...])

    whole = lambda a: pl.BlockSpec(a.shape, lambda: (0,) * a.ndim)
    outs = pl.pallas_call(
        body, name="adamw_small", in_specs=[whole(a) for a in ws] * 4, out_specs=[whole(a) for a in ws] * 3,
        out_shape=[jax.ShapeDtypeStruct(a.shape, F32) for a in ws] * 3,
        compiler_params=pltpu.CompilerParams(vmem_limit_bytes=VMEM_LIMIT_BYTES),
    )(*ws, *gs, *ms, *vs)
    return outs[0:n], outs[n:2 * n], outs[2 * n:3 * n]


def _place():
    x, y, c = lax.axis_index("x"), lax.axis_index("y"), lax.axis_index("c")
    chips = [(x, 1 - y), (1 - x, y), (1 - x, 1 - y)]
    blocks = [2 * cx + cy for cx, cy in chips]
    return x, y, c, 2 * x + y, chips, blocks


def _gather_comm(shards, small=None, mid_steps=None):
    na = len(shards)
    ns = 0 if small is None else 1

    def copies(ins, outs, sems):
        h1_send, h1_recv, h2_send, h2_recv, f1_send, f1_recv, f2_send, f2_recv, own_send, own_recv = sems[:10]
        x, y, c, j, chips, blocks = _place()
        sib = (x, y, 1 - c)
        piece = lambda a, p: pl.ds(p * (shards[a].shape[1] // 2), shards[a].shape[1] // 2)

        def remote(src, dst, send, recv, q, to):
            return pltpu.make_async_remote_copy(src_ref=src, dst_ref=dst, send_sem=send.at[q], recv_sem=recv.at[q],
                                                device_id=to, device_id_type=MESH)

        def hop1(a, k, arrival):
            dst = outs[a].at[blocks[k], c] if arrival else outs[a].at[j, c]
            return remote(dst if arrival else ins[a].at[c], dst, h1_send, h1_recv, 2 * a + k, (*chips[k], c))

        def hop2(a, p, arrival):
            ref = outs[a].at[blocks[2] if arrival else blocks[p], c, piece(a, p)]
            return remote(ref, ref, h2_send, h2_recv, 2 * a + p, (*chips[1 - p], c))

        def fwd1(a, k, half):
            ref = outs[a].at[blocks[k], half]
            return remote(ref, ref, f1_send, f1_recv, 2 * a + k, sib)

        def fwd2(a, p, half):
            ref = outs[a].at[blocks[2], half, piece(a, p)]
            return remote(ref, ref, f2_send, f2_recv, 2 * a + p, sib)

        own = [remote(ins[a], outs[a].at[j], own_send, own_recv, a, sib) for a in range(na)]
        small_copies = [remote(ins[na], outs[na].at[j], sems[10], sems[11], k, (*chips[k], c)) for k in range(3 * ns)]
        local = [pltpu.make_async_copy(ins[na], outs[na].at[j], sems[12])] if ns else []
        return hop1, hop2, fwd1, fwd2, own, small_copies, local, c

    pairs = [(a, k) for a in range(na) for k in range(2)]

    def start(ins, outs, sems):
        hop1, _, _, _, own, small_copies, local, _ = copies(ins, outs, sems)
        for cp in local + own + [hop1(a, k, False) for a, k in pairs] + small_copies:
            cp.start()

    def mid(ins, outs, sems):
        hop1, hop2, fwd1, _, _, _, _, c = copies(ins, outs, sems)
        for a, k in pairs:
            hop1(a, k, True).wait_recv()
            hop2(a, k, False).start()
            fwd1(a, k, c).start()

    def finish(ins, outs, sems):
        hop1, hop2, fwd1, fwd2, own, small_copies, local, c = copies(ins, outs, sems)
        for a, p in pairs:
            hop2(a, p, True).wait_recv()
            fwd2(a, p, c).start()
        for cp in small_copies:
            cp.wait()
        for a, k in pairs:
            hop1(a, k, False).wait_send()
            hop2(a, k, False).wait_send()
            fwd1(a, k, c).wait_send()
            fwd1(a, k, 1 - c).wait_recv()
            fwd2(a, k, c).wait_send()
            fwd2(a, k, 1 - c).wait_recv()
        for cp in own + local:
            cp.wait()

    out_shapes = [jax.ShapeDtypeStruct((NCHIP,) + s.shape, s.dtype) for s in shards]
    sems = [pltpu.SemaphoreType.DMA((2 * na,))] * 8 + [pltpu.SemaphoreType.DMA((na,))] * 2
    if ns:
        out_shapes.append(jax.ShapeDtypeStruct((NCHIP,) + small.shape, small.dtype))
        sems += [pltpu.SemaphoreType.DMA((3,)), pltpu.SemaphoreType.DMA((3,)), pltpu.SemaphoreType.DMA]
    if mid_steps is not None:
        return _Comm(list(shards) + [small] * ns, out_shapes, sems, start, finish, mid=mid, mid_steps=mid_steps)

    def forward_and_finish(ins, outs, sems):
        mid(ins, outs, sems)
        finish(ins, outs, sems)

    return _Comm(list(shards) + [small] * ns, out_shapes, sems, start, forward_and_finish)


def _simple_comm(inputs, out_shapes, make_copies, n_sems, aliases=None):
    def start(ins, outs, sems):
        for cp in make_copies(ins, outs, sems):
            cp.start()

    def finish(ins, outs, sems):
        for cp in make_copies(ins, outs, sems):
            cp.wait()

    return _Comm(inputs, out_shapes, [pltpu.SemaphoreType.DMA((n,)) for n in n_sems], start, finish, aliases)


def _pair_comm(grads):
    def make_copies(ins, outs, sems):
        x, y, c, _, _, _ = _place()
        return [pltpu.make_async_remote_copy(
            src_ref=ins[a].at[:, 1 - c], dst_ref=outs[a], send_sem=sems[0].at[a], recv_sem=sems[1].at[a],
            device_id=(x, y, 1 - c), device_id_type=MESH) for a in range(len(grads))]

    out_shapes = [jax.ShapeDtypeStruct(g.shape[:1] + g.shape[2:], F32) for g in grads]
    return _simple_comm(grads, out_shapes, make_copies, [len(grads)] * 2)


def _chips_comm(sums, small=None):
    na = len(sums)
    nk = NCHIP - 1

    def make_copies(ins, outs, sems):
        x, y, c, _, chips, blocks = _place()
        copies = [pltpu.make_async_remote_copy(
            src_ref=ins[a].at[blocks[k]], dst_ref=outs[a].at[k], send_sem=sems[0].at[a * nk + k],
            recv_sem=sems[1].at[a * nk + k], device_id=(*chips[k], c), device_id_type=MESH)
            for a in range(na) for k in range(nk)]
        if small is not None:
            me = 4 * x + 2 * y + c
            for r in range(1, NDEV):
                peer = (1 - x if r & 4 else x, 1 - y if r & 2 else y, 1 - c if r & 1 else c)
                copies.append(pltpu.make_async_remote_copy(
                    src_ref=ins[na], dst_ref=outs[na].at[me], send_sem=sems[2].at[r - 1], recv_sem=sems[3].at[r - 1],
                    device_id=peer, device_id_type=MESH))
            copies.append(pltpu.make_async_copy(ins[na], outs[na].at[me], sems[4].at[0]))
        return copies

    out_shapes = [jax.ShapeDtypeStruct((nk,) + g.shape[1:], BF16) for g in sums]
    chip_sems = [max(na * nk, 1)] * 2
    if small is None:
        return _simple_comm(sums, out_shapes, make_copies, chip_sems)
    out_shapes.append(jax.ShapeDtypeStruct((NDEV,) + small.shape, F32))
    return _simple_comm(list(sums) + [small], out_shapes, make_copies, chip_sems + [NDEV - 1] * 2 + [1])


def _swap_comm(totals):
    def make_copies(ins, outs, sems):
        x, y, c, _, _, _ = _place()
        return [pltpu.make_async_remote_copy(
            src_ref=outs[a].at[:, c], dst_ref=outs[a].at[:, c], send_sem=sems[0].at[a], recv_sem=sems[1].at[a],
            device_id=(x, y, 1 - c), device_id_type=MESH) for a in range(len(totals))]

    out_shapes = [jax.ShapeDtypeStruct(t.shape, F32) for t in totals]
    return _simple_comm(totals, out_shapes, make_copies, [len(totals)] * 2, aliases={a: a for a in range(len(totals))})


def _pair_sum(place, grad, landed, name):
    _, _, rows, cols = grad.shape

    def body(place_ref, g_ref, l_ref, o_ref):
        o_ref[...] = (g_ref[...] + l_ref[...]).astype(BF16)

    return pl.pallas_call(
        body, name=name,
        grid_spec=pltpu.PrefetchScalarGridSpec(
            num_scalar_prefetch=1, grid=(NCHIP,),
            in_specs=[pl.BlockSpec((None, None, rows, cols), lambda b, p: (b, p[1], 0, 0)),
                      pl.BlockSpec((None, rows, cols), lambda b, p: (b, 0, 0))],
            out_specs=pl.BlockSpec((None, rows, cols), lambda b, p: (b, 0, 0))),
        out_shape=jax.ShapeDtypeStruct((NCHIP, rows, cols), BF16), compiler_params=_cparams(1),
    )(place, grad, landed)


def _chip_sum(place, grad, landed, parts, layer, n_layers, name, prev=None):
    _, _, rows, cols = grad.shape

    def body(*refs):
        g_ref, l_ref, p_ref, o_ref = refs[1], refs[2], refs[3], refs[-1]
        tot = g_ref[...] + l_ref[...]
        for k in range(NCHIP - 1):
            tot = tot + p_ref[k].astype(F32)
        o_ref[...] = tot

    in_specs = [pl.BlockSpec((None, None, rows, cols), lambda i, p: (p[0], p[1], 0, 0)),
                pl.BlockSpec((None, rows, cols), lambda i, p: (p[0], 0, 0)),
                pl.BlockSpec((NCHIP - 1, rows, cols), lambda i, p: (0, 0, 0))]
    args = [place, grad, landed, parts]
    if prev is not None:
        in_specs.append(ANY)
        args.append(prev)
    return pl.pallas_call(
        body, name=name,
        grid_spec=pltpu.PrefetchScalarGridSpec(
            num_scalar_prefetch=1, grid=(1,), in_specs=in_specs,
            out_specs=pl.BlockSpec((None, None, rows, cols), lambda i, p: (layer, p[1], 0, 0))),
        out_shape=jax.ShapeDtypeStruct((n_layers, 2, rows, cols), F32),
        input_output_aliases={} if prev is None else {4: 0}, compiler_params=_cparams(1),
    )(*args)


def _sum_devices(parts, name):
    def body(p_ref, o_ref):
        tot = p_ref[0]
        for d in range(1, NDEV):
            tot = tot + p_ref[d]
        o_ref[...] = tot

    return pl.pallas_call(
        body, name=name, in_specs=[pl.BlockSpec(parts.shape, lambda: (0, 0, 0))],
        out_specs=pl.BlockSpec(parts.shape[1:], lambda: (0, 0)), out_shape=jax.ShapeDtypeStruct(parts.shape[1:], F32),
    )(parts)


def _pack(parts):
    rows = []
    for p in parts:
        p = p.reshape(-1, LANES)
        rows.append(jnp.pad(p, ((0, -p.shape[0] % SUBLANES), (0, 0))))
    return jnp.concatenate(rows, axis=0)


def _unpack(buf, shapes, lead=()):
    out, r0 = [], 0
    nl = len(lead)
    for shp in shapes:
        nrow = 1
        for d in shp:
            nrow *= d
        nrow //= LANES
        out.append(buf[(slice(None),) * nl + (slice(r0, r0 + nrow),)].reshape(lead + tuple(shp)))
        r0 += nrow + (-nrow % SUBLANES)
    return out


WEIGHT_ORDER = ("norm_mix_even", "w_in", "conv_a", "ln_a_g", "ln_a_b", "conv_b", "w_out", "norm_mix_odd", "w_pool",
                "pool_scale", "norm_ffn", "w_up", "conv_ffn_w", "w_down", "norm_final")
BIG = ("w_in", "w_out", "w_pool", "w_up", "w_down")
HALF = {"w_in": (D // 2, NZ // NCHIP), "w_out": (2 * A // NCHIP // 2, D), "w_pool": (PG // 2, PG),
        "w_up": (D // 2, FF2 // NCHIP), "w_down": (FF // NCHIP // 2, D)}
SMALL_SHARDED = ("conv_a", "conv_b", "conv_ffn_w", "norm_mix_odd", "pool_scale")
SMALL_ALL = ("norm_mix_even", "conv_a", "ln_a_g", "ln_a_b", "conv_b", "norm_mix_odd", "pool_scale", "norm_ffn", "conv_ffn_w",
             "norm_final")
SMALL_EARLY = ("norm_mix_odd", "pool_scale", "norm_ffn", "conv_ffn_w", "norm_final", "loss", "conv_a", "ln_a_g", "ln_a_b", "conv_b")
SMALL_LATE = ("norm_mix_even",)


def kernel(x, norm_mix_even, w_in, conv_a, ln_a_g, ln_a_b, conv_b, w_out, norm_mix_odd, w_pool, pool_scale, norm_ffn, w_up, conv_ffn_w, w_down, norm_final, loss_target, m_norm_mix_even, m_w_in, m_conv_a, m_ln_a_g, m_ln_a_b, m_conv_b, m_w_out, m_norm_mix_odd, m_w_pool, m_pool_scale, m_norm_ffn, m_w_up, m_conv_ffn_w, m_w_down, m_norm_final, v_norm_mix_even, v_w_in, v_conv_a, v_ln_a_g, v_ln_a_b, v_conv_b, v_w_out, v_norm_mix_odd, v_w_pool, v_pool_scale, v_norm_ffn, v_w_up, v_conv_ffn_w, v_w_down, v_norm_final):
    w = dict(norm_mix_even=norm_mix_even, w_in=w_in, conv_a=conv_a, ln_a_g=ln_a_g, ln_a_b=ln_a_b, conv_b=conv_b, w_out=w_out,
             norm_mix_odd=norm_mix_odd, w_pool=w_pool, pool_scale=pool_scale, norm_ffn=norm_ffn, w_up=w_up,
             conv_ffn_w=conv_ffn_w, w_down=w_down, norm_final=norm_final)
    m = dict(norm_mix_even=m_norm_mix_even, w_in=m_w_in, conv_a=m_conv_a, ln_a_g=m_ln_a_g, ln_a_b=m_ln_a_b, conv_b=m_conv_b,
             w_out=m_w_out, norm_mix_odd=m_norm_mix_odd, w_pool=m_w_pool, pool_scale=m_pool_scale, norm_ffn=m_norm_ffn,
             w_up=m_w_up, conv_ffn_w=m_conv_ffn_w, w_down=m_w_down, norm_final=m_norm_final)
    v = dict(norm_mix_even=v_norm_mix_even, w_in=v_w_in, conv_a=v_conv_a, ln_a_g=v_ln_a_g, ln_a_b=v_ln_a_b, conv_b=v_conv_b,
             w_out=v_w_out, norm_mix_odd=v_norm_mix_odd, w_pool=v_w_pool, pool_scale=v_pool_scale, norm_ffn=v_norm_ffn,
             w_up=v_w_up, conv_ffn_w=v_conv_ffn_w, w_down=v_w_down, norm_final=v_norm_final)
    chip = 2 * lax.axis_index("x") + lax.axis_index("y")
    place = jnp.stack([chip, lax.axis_index("c")]).astype(jnp.int32)

    half = lambda a, name: a.astype(BF16).reshape((2,) + HALF[name])
    shard = {"w_in": half(w_in[0], "w_in"), "w_out": half(w_out[0], "w_out")}
    small_shapes = [w[k].shape[-2:] if w[k].ndim == 3 and k != "conv_ffn_w" else (w[k].size // w[k].shape[-1], w[k].shape[-1])
                    for k in SMALL_SHARDED]
    rep = dict(norm_mix_even=norm_mix_even, ln_a_g=ln_a_g, ln_a_b=ln_a_b, norm_ffn=norm_ffn, norm_final=norm_final.reshape(1, D))
    rows_per_step = {"w_in": 512, "w_out": 256, "w_pool": 256, "w_up": 256, "w_down": 352}
    as3 = lambda a: a.reshape(a.shape[0], -1, a.shape[-1])

    def adamw(k, total):
        outs, _ = _adamw_big(as3(w[k]), as3(total), as3(m[k]), as3(v[k]), rows_per_step[k], "adamw_" + k)
        return [a.reshape(w[k].shape) for a in outs]

    grad_x, done, summed = _train_step(
        x[0], loss_target[0], place, shard, (w_up, w_down, w_pool[0]), rep, _pack([w[k] for k in SMALL_SHARDED]), small_shapes,
        adamw)
    loss = summed["loss"][0, 0]

    grad, delta, new_m, new_v = ({k: done[k][q] for k in BIG} for q in range(4))
    for k in SMALL_ALL:
        gsum = summed[k]
        if k in SMALL_SHARDED:
            cols = w[k].shape[-1]
            gsum = lax.dynamic_slice_in_dim(gsum, chip * cols, cols, axis=gsum.ndim - 1)
        grad[k] = gsum.reshape(w[k].shape)

    as2 = lambda a: a.reshape(1, -1) if a.ndim == 1 else a
    ds, ms, vs = _adamw_small(*[[as2(t[k]) for k in SMALL_ALL] for t in (w, grad, m, v)])
    for k, d2, m2, v2 in zip(SMALL_ALL, ds, ms, vs):
        delta[k], new_m[k], new_v[k] = (a.reshape(w[k].shape) for a in (d2, m2, v2))

    return (loss, grad_x[None], *[grad[k] for k in WEIGHT_ORDER], *[delta[k] for k in WEIGHT_ORDER],
            *[new_m[k] for k in WEIGHT_ORDER], *[new_v[k] for k in WEIGHT_ORDER])
```

```python
import functools

import jax
import jax.numpy as jnp
from jax import lax
from jax.experimental import pallas as pl
from jax.experimental.pallas import tpu as pltpu

F32, BF16 = jnp.float32, jnp.bfloat16

D = 1024
A = 512
NZ = 5 * A
FF = 2816
FF2 = 2 * FF
NCHIP = 4
NDEV = 8
K_A, K_S = 31, 3
POOL_WINDOWS = (2, 4, 8, 16)
PG = D // len(POOL_WINDOWS)
RMS_EPS, LN_EPS = 1e-6, 1e-5
ADAM_LR, ADAM_B1, ADAM_B2, ADAM_EPS, ADAM_WD, ADAM_STEP = 0.001, 0.9, 0.999, 1e-08, 0.01, 10

HALO_A, HALO_S, HALO_P = 32, 8, 16
SUBLANES = 8
LANES = 128
VMEM_LIMIT_BYTES = 56 * 1024 * 1024

TS_MIX = 512
TS_MIXB = 256
TS_FFN = 256
TS_POOL = 512
TS_MM = 512
TS_WGRAD = 1024
R_CHUNK = 64
GATHER_MID_STEPS = {"mix0_fwd": 2, "ffn0_fwd": 5}

MESH = pl.DeviceIdType.MESH
ANY = pl.BlockSpec(memory_space=pl.ANY)
NT_DIMS = (((1,), (1,)), ((), ()))
TN_DIMS = (((0,), (0,)), ((), ()))


def _cparams(n_axes):
    return pltpu.CompilerParams(dimension_semantics=("arbitrary",) * n_axes, vmem_limit_bytes=VMEM_LIMIT_BYTES)


def _const(shape):
    nd = len(shape)
    return pl.BlockSpec(shape, lambda *_: (0,) * nd, pipeline_mode=pl.Buffered(1))


def _sigmoid(v):
    return 1.0 / (1.0 + jnp.exp(-v))


def _rsqrt_mean_sq(x):
    return lax.rsqrt(jnp.mean(x * x, axis=-1, keepdims=True) + RMS_EPS)


def _rms_bwd(dh, xh, r, g):
    dxh = dh * g
    return r * (dxh - xh * jnp.mean(dxh * xh, axis=-1, keepdims=True))


def _shifted(buf_ref, row0, rows, col0, width, offsets):
    lo = (min(offsets) // SUBLANES) * SUBLANES
    hi = -(-(max(offsets) + rows) // SUBLANES) * SUBLANES
    start = row0 + lo if isinstance(row0, int) else pl.multiple_of(row0 + lo, SUBLANES)
    win = buf_ref[pl.ds(start, hi - lo), col0:col0 + width]
    out = {}
    for res in sorted({(o - lo) % SUBLANES for o in offsets}):
        qs = {o: (o - lo) // SUBLANES for o in offsets if (o - lo) % SUBLANES == res}
        base = pltpu.roll(win, hi - lo - res, 0) if res else win
        for o, q in qs.items():
            out[o] = base[SUBLANES * q:SUBLANES * q + rows, :]
    return out


def _rowsum8(v):
    acc = v[0:SUBLANES, :]
    for r in range(SUBLANES, v.shape[0], SUBLANES):
        acc = acc + v[r:r + SUBLANES, :]
    return acc


def _taps(w_ref, sh, offsets, col0, width):
    acc = None
    for k, o in enumerate(offsets):
        term = w_ref[k:k + 1, col0:col0 + width] * sh[o]
        acc = term if acc is None else acc + term
    return acc


def _window_bases(buf_ref, row0, rows, col0, width, offsets):
    lo = (min(offsets) // SUBLANES) * SUBLANES
    hi = -(-(max(offsets) + rows) // SUBLANES) * SUBLANES
    start = row0 + lo if isinstance(row0, int) else pl.multiple_of(row0 + lo, SUBLANES)
    win = buf_ref[pl.ds(start, hi - lo), col0:col0 + width]
    for res in sorted({(o - lo) % SUBLANES for o in offsets}):
        taps = [(k, (o - lo) // SUBLANES * SUBLANES) for k, o in enumerate(offsets) if (o - lo) % SUBLANES == res]
        yield (pltpu.roll(win, hi - lo - res, 0) if res else win), taps


def _conv_acc(buf_ref, w_ref, row0, rows, col0, width, offsets):
    acc = None
    for base, taps in _window_bases(buf_ref, row0, rows, col0, width, offsets):
        for k, q in taps:
            term = w_ref[k:k + 1, col0:col0 + width] * base[q:q + rows, :]
            acc = term if acc is None else acc + term
    return acc


def _conv_corr(buf_ref, w_ref, other, acc_ref, row0, rows, col0, width, offsets):
    acc = None
    for base, taps in _window_bases(buf_ref, row0, rows, col0, width, offsets):
        for k, q in taps:
            sl = base[q:q + rows, :]
            term = w_ref[k:k + 1, col0:col0 + width] * sl
            acc = term if acc is None else acc + term
            acc_ref[SUBLANES * k:SUBLANES * (k + 1), col0:col0 + width] += _rowsum8(sl * other)
    return acc


def _finish_tap_sums(acc_ref, out_ref, n_taps):
    for k in range(n_taps):
        out_ref[k:k + 1, :] = jnp.sum(acc_ref[SUBLANES * k:SUBLANES * (k + 1), :], axis=0, keepdims=True)


class _Comm:
    def __init__(self, inputs, out_shapes, sems, start, finish, aliases=None, mid=None, mid_steps=1):
        self.inputs, self.out_shapes, self.sems = list(inputs), list(out_shapes), list(sems)
        self.start, self.finish, self.mid, self.aliases = start, finish, mid, dict(aliases or {})
        self.mid_steps = mid_steps


def _join_comm(a, b):
    ni, no, ns = len(a.inputs), len(a.out_shapes), len(a.sems)

    def both(name):
        def run(ins, outs, sems):
            if getattr(a, name) is not None:
                getattr(a, name)(ins[:ni], outs[:no], sems[:ns])
            if getattr(b, name) is not None:
                getattr(b, name)(ins[ni:], outs[no:], sems[ns:])
        return run

    aliases = {**a.aliases, **{ni + i: no + o for i, o in b.aliases.items()}}
    mid = both("mid") if (a.mid is not None or b.mid is not None) else None
    return _Comm(a.inputs + b.inputs, a.out_shapes + b.out_shapes, a.sems + b.sems, both("start"), both("finish"), aliases, mid)


def _call(body, *, name, grid, in_specs, out_specs, out_shape, args, scratch=(), comm=None, aliases=None):
    n_in, n_out, n_scr, n_axes = len(in_specs), len(out_specs), len(scratch), len(grid)
    params = pltpu.CompilerParams(dimension_semantics=("arbitrary",) * n_axes, vmem_limit_bytes=VMEM_LIMIT_BYTES)
    aliases = dict(aliases or {})
    if comm is None:
        outs = pl.pallas_call(body, name=name, grid=grid, in_specs=list(in_specs), out_specs=list(out_specs),
                              out_shape=list(out_shape), scratch_shapes=list(scratch), input_output_aliases=aliases,
                              compiler_params=params)(*args)
        return list(outs), []
    ci, co = len(comm.inputs), len(comm.out_shapes)

    def wrapped(*refs):
        k_in, c_in = refs[:n_in], refs[n_in:n_in + ci]
        o0 = n_in + ci
        k_out, c_out = refs[o0:o0 + n_out], refs[o0 + n_out:o0 + n_out + co]
        s0 = o0 + n_out + co
        k_scr, c_sem = refs[s0:s0 + n_scr], refs[s0 + n_scr:]
        first = pl.program_id(0) == 0
        last = pl.program_id(0) == grid[0] - 1
        for ax in range(1, n_axes):
            first = jnp.logical_and(first, pl.program_id(ax) == 0)
            last = jnp.logical_and(last, pl.program_id(ax) == grid[ax] - 1)

        @pl.when(first)
        def _():
            comm.start(c_in, c_out, c_sem)

        mid_early = comm.mid is not None and n_axes == 1 and grid[0] > comm.mid_steps
        if mid_early:
            @pl.when(pl.program_id(0) == grid[0] - 1 - comm.mid_steps)
            def _():
                comm.mid(c_in, c_out, c_sem)

        body(*k_in, *k_out, *k_scr)

        @pl.when(last)
        def _():
            if comm.mid is not None and not mid_early:
                comm.mid(c_in, c_out, c_sem)
            comm.finish(c_in, c_out, c_sem)

    outs = pl.pallas_call(
        wrapped, name=name, grid=grid, in_specs=list(in_specs) + [ANY] * ci, out_specs=list(out_specs) + [ANY] * co,
        out_shape=list(out_shape) + comm.out_shapes, scratch_shapes=list(scratch) + comm.sems,
        input_output_aliases={**aliases, **{n_in + i: n_out + o for i, o in comm.aliases.items()}}, compiler_params=params,
    )(*args, *comm.inputs)
    return list(outs[:n_out]), list(outs[n_out:])


def _run_comm(comm, name):
    ci, co = len(comm.inputs), len(comm.out_shapes)

    def body(*refs):
        c_in, c_out, c_sem = refs[:ci], refs[ci:ci + co], refs[ci + co:]
        comm.start(c_in, c_out, c_sem)
        if comm.mid is not None:
            comm.mid(c_in, c_out, c_sem)
        comm.finish(c_in, c_out, c_sem)

    return list(pl.pallas_call(body, name=name, in_specs=[ANY] * ci, out_specs=[ANY] * co, out_shape=comm.out_shapes,
                               scratch_shapes=comm.sems, input_output_aliases=comm.aliases)(*comm.inputs))


def _mix0_fwd(x, g, w_in, conv_a, ln_g, ln_b, conv_b, w_out, comm=None):
    s = x.shape[0]
    ts = min(TS_MIX, s)
    n = s // ts
    bw = NZ // NCHIP
    offs_a = [HALO_A - (K_A - 1) + k for k in range(K_A)]
    offs_s = [HALO_S - (K_S - 1) + k for k in range(K_S)]

    def body(x_ref, g_ref, win_ref, ca_ref, lg_ref, lb_ref, cb_ref, wout_ref,
             h_ref, z_ref, ac_ref, bconv_buf, cat_ref, x1_ref, glu_buf, cv_buf):
        i = pl.program_id(0)

        @pl.when(i == 0)
        def _():
            glu_buf[0:HALO_A, :] = jnp.zeros((HALO_A, A), F32)
            cv_buf[0:HALO_S, :] = jnp.zeros((HALO_S, A), F32)

        xv = x_ref[...]
        h = (xv * _rsqrt_mean_sq(xv) * g_ref[...]).astype(BF16)
        h_ref[...] = h
        for j in range(NCHIP):
            z_ref[:, j * bw:(j + 1) * bw] = jnp.dot(h, win_ref[j], preferred_element_type=F32)
        glu_buf[HALO_A:HALO_A + ts, :] = z_ref[:, 0:A] * _sigmoid(z_ref[:, A:2 * A])
        cv_buf[HALO_S:HALO_S + ts, :] = z_ref[:, 3 * A:4 * A] * z_ref[:, 4 * A:5 * A]

        def chunk(ci, carry):
            r0 = pl.multiple_of(ci * R_CHUNK, R_CHUNK)
            for c0 in range(0, A, LANES):
                ac_ref[pl.ds(r0, R_CHUNK), c0:c0 + LANES] = _conv_acc(glu_buf, ca_ref, r0, R_CHUNK, c0, LANES, offs_a)
                bconv_buf[pl.ds(r0, R_CHUNK), c0:c0 + LANES] = _conv_acc(cv_buf, cb_ref, r0, R_CHUNK, c0, LANES, offs_s)
            return carry

        lax.fori_loop(0, ts // R_CHUNK, chunk, 0)
        glu_buf[0:HALO_A, :] = glu_buf[ts:ts + HALO_A, :]
        cv_buf[0:HALO_S, :] = cv_buf[ts:ts + HALO_S, :]

        ac = ac_ref[...]
        xc = ac - jnp.mean(ac, axis=-1, keepdims=True)
        xn = xc * lax.rsqrt(jnp.mean(xc * xc, axis=-1, keepdims=True) + LN_EPS)
        ln = xn * lg_ref[...] + lb_ref[...]
        cat_ref[:, 0:A] = (ln * _sigmoid(ln)).astype(BF16)
        cat_ref[:, A:2 * A] = (z_ref[:, 2 * A:3 * A] * bconv_buf[...]).astype(BF16)
        x1_ref[...] = xv + jnp.dot(cat_ref[...], wout_ref[...], preferred_element_type=F32)

    tile = lambda w: pl.BlockSpec((ts, w), lambda i: (i, 0))
    return _call(
        body, name="mix0_fwd", grid=(n,), comm=comm, args=(x, g, w_in, conv_a, ln_g, ln_b, conv_b, w_out),
        in_specs=[tile(D), _const((1, D)), _const((NCHIP, D, bw)), _const((K_A, A)), _const((1, A)), _const((1, A)),
                  _const((K_S, A)), _const((2 * A, D))],
        out_specs=[tile(D), tile(NZ), tile(A), tile(A), tile(2 * A), tile(D)],
        out_shape=[jax.ShapeDtypeStruct((s, D), BF16), jax.ShapeDtypeStruct((s, NZ), F32), jax.ShapeDtypeStruct((s, A), F32),
                   jax.ShapeDtypeStruct((s, A), F32), jax.ShapeDtypeStruct((s, 2 * A), BF16), jax.ShapeDtypeStruct((s, D), F32)],
        scratch=[pltpu.VMEM((HALO_A + ts, A), F32), pltpu.VMEM((HALO_S + ts, A), F32)])


def _ffn_fwd(x, g, w_up, wc, w_down, name, comm=None, head=None):
    s = x.shape[0]
    ts = min(TS_FFN, s)
    n = s // ts
    bw = FF2 // NCHIP
    rows = 32
    offs = [HALO_S - (K_S - 1) + k for k in range(K_S)]
    n_in = 5 + (2 if head else 0)

    def body(*refs):
        x_ref, g_ref, wup_ref, wc_ref, wdn_ref = refs[:5]
        h_ref, u0_ref, u_ref, act_ref, xo_ref = refs[n_in:n_in + 5]
        cbuf = refs[-1]
        i = pl.program_id(0)

        @pl.when(i == 0)
        def _():
            cbuf[0:HALO_S, :] = jnp.zeros((HALO_S, FF2), F32)
            if head:
                refs[n_in + 5][...] = jnp.zeros((1, D), F32)
                refs[n_in + 6][...] = jnp.zeros((1, LANES), F32)

        xv = x_ref[...]
        h = (xv * _rsqrt_mean_sq(xv) * g_ref[...]).astype(BF16)
        h_ref[...] = h
        f = None
        for p in range(NCHIP // 2):
            for j in (p, NCHIP // 2 + p):
                zc = jnp.dot(h, wup_ref[j], preferred_element_type=F32)
                u0_ref[:, j * bw:(j + 1) * bw] = zc.astype(BF16)
                cbuf[HALO_S:HALO_S + ts, j * bw:(j + 1) * bw] = zc
            for r0 in range(0, ts, rows):
                for c0 in range(p * bw, (p + 1) * bw, LANES):
                    ug = _taps(wc_ref, _shifted(cbuf, r0, rows, c0, LANES, offs), offs, c0, LANES)
                    uv = _taps(wc_ref, _shifted(cbuf, r0, rows, FF + c0, LANES, offs), offs, FF + c0, LANES)
                    u_ref[r0:r0 + rows, c0:c0 + LANES] = ug
                    u_ref[r0:r0 + rows, FF + c0:FF + c0 + LANES] = uv
                    act_ref[r0:r0 + rows, c0:c0 + LANES] = (ug * _sigmoid(ug) * uv).astype(BF16)
            fp = jnp.dot(act_ref[:, p * bw:(p + 1) * bw], wdn_ref[p * bw:(p + 1) * bw, :], preferred_element_type=F32)
            f = fp if f is None else f + fp
        cbuf[0:HALO_S, :] = cbuf[ts:ts + HALO_S, :]
        if not head:
            xo_ref[...] = xv + f
        else:
            gf_ref, t_ref, dgf_ref, loss_ref = refs[5], refs[6], refs[n_in + 5], refs[n_in + 6]
            xo = xv + f
            r = _rsqrt_mean_sq(xo)
            xh = xo * r
            gv = gf_ref[...]
            err = xh * gv - t_ref[...]
            loss_ref[...] += jnp.sum(jnp.sum(err * err, axis=1, keepdims=True), axis=0, keepdims=True) * (0.5 / D)
            dy = err * (1.0 / D)
            dgf_ref[...] += jnp.sum(dy * xh, axis=0, keepdims=True)
            xo_ref[...] = _rms_bwd(dy, xh, r, gv)

    tile = lambda w: pl.BlockSpec((ts, w), lambda i: (i, 0))
    one_row = lambda w: pl.BlockSpec((1, w), lambda i: (0, 0))
    return _call(
        body, name=name, grid=(n,), comm=comm, args=(x, g, w_up, wc, w_down) + (tuple(head) if head else ()),
        in_specs=[tile(D), _const((1, D)), _const((NCHIP, D, bw)), _const((K_S, FF2)), _const((FF, D))]
        + ([_const((1, D)), tile(D)] if head else []),
        out_specs=[tile(D), tile(FF2), tile(FF2), tile(FF), tile(D)] + ([one_row(D), one_row(LANES)] if head else []),
        out_shape=[jax.ShapeDtypeStruct((s, D), BF16), jax.ShapeDtypeStruct((s, FF2), BF16), jax.ShapeDtypeStruct((s, FF2), F32),
                   jax.ShapeDtypeStruct((s, FF), BF16), jax.ShapeDtypeStruct((s, D), F32)]
        + ([jax.ShapeDtypeStruct((1, D), F32), jax.ShapeDtypeStruct((1, LANES), F32)] if head else []),
        scratch=[pltpu.VMEM((HALO_S + ts, FF2), F32)])


def _pool_windows(hbuf, pbuf, tile_row0, ts):
    def chunk(ci, carry):
        r0 = pl.multiple_of(ci * R_CHUNK, R_CHUNK)
        t1 = (tile_row0 + r0 + lax.broadcasted_iota(jnp.int32, (R_CHUNK, 1), 0) + 1).astype(F32)
        for gi, w in enumerate(POOL_WINDOWS):
            cnt = jnp.minimum(t1, float(w))
            offs = [HALO_P - jj for jj in range(w)]
            for c0 in range(gi * PG, (gi + 1) * PG, LANES):
                sh = _shifted(hbuf, r0, R_CHUNK, c0, LANES, offs)
                tot = sh[offs[0]]
                for o in offs[1:]:
                    tot = tot + sh[o]
                pbuf[pl.ds(r0, R_CHUNK), c0:c0 + LANES] = (tot / cnt - sh[HALO_P]).astype(BF16)
        return carry

    lax.fori_loop(0, ts // R_CHUNK, chunk, 0)


def _assemble_wpool(wp_ref, wps):
    rb = PG // NCHIP
    for gi in range(len(POOL_WINDOWS)):
        for j in range(NCHIP):
            wps[gi, j * rb:(j + 1) * rb, :] = wp_ref[j, gi]


def _pool_fwd(x, g, w_pool, scale):
    s = x.shape[0]
    ts = min(TS_POOL, s)
    n = s // ts
    ng = len(POOL_WINDOWS)

    def body(x_ref, g_ref, wp_ref, sc_ref, xo_ref, hbuf, pbuf, wps):
        i = pl.program_id(0)

        @pl.when(i == 0)
        def _():
            hbuf[0:HALO_P, :] = jnp.zeros((HALO_P, D), F32)
            _assemble_wpool(wp_ref, wps)

        xv = x_ref[...]
        hbuf[HALO_P:HALO_P + ts, :] = xv * _rsqrt_mean_sq(xv) * g_ref[...]
        _pool_windows(hbuf, pbuf, i * ts, ts)
        hbuf[0:HALO_P, :] = hbuf[ts:ts + HALO_P, :]
        for gi in range(ng):
            cols = slice(gi * PG, (gi + 1) * PG)
            y = jnp.dot(pbuf[:, cols], wps[gi], preferred_element_type=F32)
            xo_ref[:, cols] = xv[:, cols] + y * sc_ref[:, cols]

    tile = pl.BlockSpec((ts, D), lambda i: (i, 0))
    return pl.pallas_call(
        body, name="pool_fwd", grid=(n,),
        in_specs=[tile, _const((1, D)), _const((NCHIP, ng, PG // NCHIP, PG)), _const((1, D))],
        out_specs=tile, out_shape=jax.ShapeDtypeStruct((s, D), F32),
        scratch_shapes=[pltpu.VMEM((HALO_P + ts, D), F32), pltpu.VMEM((ts, D), BF16), pltpu.VMEM((ng, PG, PG), BF16)],
        compiler_params=_cparams(1),
    )(x, g, w_pool, scale)


def _ffn_bwd_a(dxo, u, u0, wc, w_down, name, comm=None):
    s = dxo.shape[0]
    ts = min(TS_FFN, s)
    n = s // ts
    cw = FF2 // NCHIP
    rows = 32
    lw2 = 2 * LANES
    boffs = [K_S - 1 - k for k in range(K_S)]

    def body(dxo_ref, u_ref, u0_ref, wc_ref, wdn_ref, du0_ref, dwc_ref, dubuf, dact, dwacc):
        i = pl.program_id(0)

        @pl.when(i == 0)
        def _():
            dubuf[ts:ts + HALO_S, :] = jnp.zeros((HALO_S, FF2), F32)
            dwacc[...] = jnp.zeros(dwacc.shape, F32)

        df = dxo_ref[...].astype(BF16)
        for cg in range(0, FF, cw):
            dact[...] = lax.dot_general(df, wdn_ref[cg:cg + cw, :], NT_DIMS, preferred_element_type=F32)

            def chunk(ci, carry, cg=cg):
                rs = pl.ds(pl.multiple_of(ci * rows, rows), rows)
                for c in range(cg, cg + cw, LANES):
                    ug, uv = u_ref[rs, c:c + LANES], u_ref[rs, FF + c:FF + c + LANES]
                    sg = _sigmoid(ug)
                    da = dact[rs, c - cg:c - cg + LANES]
                    gs = ug * sg
                    dubuf[rs, c:c + LANES] = da * uv * (sg + gs * (1.0 - sg))
                    dubuf[rs, FF + c:FF + c + LANES] = da * gs
                return carry

            lax.fori_loop(0, ts // rows, chunk, 0)

        def chunk2(ci, carry):
            r0 = pl.multiple_of(ci * rows, rows)
            rs = pl.ds(r0, rows)
            for c in range(0, FF2, lw2):
                sh = _shifted(dubuf, r0, rows, c, lw2, boffs)
                du0_ref[rs, c:c + lw2] = _taps(wc_ref, sh, boffs, c, lw2).astype(BF16)
                u0v = u0_ref[rs, c:c + lw2].astype(F32)
                for k, o in enumerate(boffs):
                    dwacc[SUBLANES * k:SUBLANES * (k + 1), c:c + lw2] += _rowsum8(sh[o] * u0v)
            return carry

        lax.fori_loop(0, ts // rows, chunk2, 0)
        dubuf[ts:ts + HALO_S, :] = dubuf[0:HALO_S, :]

        @pl.when(i == n - 1)
        def _():
            _finish_tap_sums(dwacc, dwc_ref, K_S)

    rev = lambda w: pl.BlockSpec((ts, w), lambda i: (n - 1 - i, 0))
    return _call(
        body, name=name, grid=(n,), comm=comm, args=(dxo, u, u0, wc, w_down),
        in_specs=[rev(D), rev(FF2), rev(FF2), _const((K_S, FF2)), _const((FF, D))],
        out_specs=[rev(FF2), pl.BlockSpec((K_S, FF2), lambda i: (0, 0))],
        out_shape=[jax.ShapeDtypeStruct((s, FF2), BF16), jax.ShapeDtypeStruct((K_S, FF2), F32)],
        scratch=[pltpu.VMEM((ts + HALO_S, FF2), F32), pltpu.VMEM((ts, cw), F32), pltpu.VMEM((SUBLANES * K_S, FF2), F32)])


def _nt_rms_bwd(dy, w, x, g, dres, name, comm=None, tiles=None, dx_so_far=None):
    s = x.shape[0]
    ts = min(TS_MM, s)
    first, n = (0, s // ts) if tiles is None else tiles
    nw = dy.shape[1]
    bw = nw // NCHIP

    def body(dy_ref, w_ref, x_ref, g_ref, dres_ref, dx_ref, dg_ref):
        i = pl.program_id(0)

        @pl.when(i == 0)
        def _():
            dg_ref[...] = jnp.zeros((1, D), F32)

        dh = lax.dot_general(dy_ref[:, 0:bw], w_ref[0], NT_DIMS, preferred_element_type=F32)
        for j in range(1, NCHIP):
            dh = dh + lax.dot_general(dy_ref[:, j * bw:(j + 1) * bw], w_ref[j], NT_DIMS, preferred_element_type=F32)
        xv = x_ref[...]
        r = _rsqrt_mean_sq(xv)
        xh = xv * r
        dg_ref[...] += jnp.sum(dh * xh, axis=0, keepdims=True)
        dx_ref[...] = dres_ref[...] + _rms_bwd(dh, xh, r, g_ref[...])

    def body_with_alias(dy_ref, w_ref, x_ref, g_ref, dres_ref, _, dx_ref, dg_ref):
        body(dy_ref, w_ref, x_ref, g_ref, dres_ref, dx_ref, dg_ref)

    tile = lambda wd: pl.BlockSpec((ts, wd), lambda i: (first + i, 0))
    in_specs = [tile(nw), _const((NCHIP, D, bw)), tile(D), _const((1, D)), tile(D)]
    more = dx_so_far is not None
    return _call(
        body_with_alias if more else body, name=name, grid=(n,), comm=comm,
        args=(dy, w, x, g, dres) + ((dx_so_far,) if more else ()), in_specs=in_specs + [ANY] * more,
        out_specs=[tile(D), pl.BlockSpec((1, D), lambda i: (0, 0))], aliases={5: 0} if more else None,
        out_shape=[jax.ShapeDtypeStruct((s, D), F32), jax.ShapeDtypeStruct((1, D), F32)])


def _wgrad(a, b, n_blocks, name, comm=None):
    s, m = a.shape
    bw = b.shape[1] // n_blocks
    tk = min(TS_WGRAD, s)

    def body(a_ref, b_ref, o_ref):
        @pl.when(pl.program_id(0) == 0)
        def _():
            o_ref[...] = jnp.zeros(o_ref.shape, F32)

        for j in range(n_blocks):
            o_ref[j] += lax.dot_general(a_ref[...], b_ref[:, j * bw:(j + 1) * bw].astype(BF16), TN_DIMS,
                                        preferred_element_type=F32)

    (out,), comm_out = _call(
        body, name=name, grid=(s // tk,), comm=comm, args=(a, b),
        in_specs=[pl.BlockSpec((tk, m), lambda k: (k, 0)), pl.BlockSpec((tk, b.shape[1]), lambda k: (k, 0))],
        out_specs=[pl.BlockSpec((n_blocks, m, bw), lambda k: (0, 0, 0))],
        out_shape=[jax.ShapeDtypeStruct((n_blocks, m, bw), F32)])
    return out, comm_out


def _pool_bwd(dxo, x, g, w_pool, scale, comm=None):
    s = x.shape[0]
    ts = min(TS_POOL, s)
    n = s // ts
    ng = len(POOL_WINDOWS)
    rb = PG // NCHIP

    def body(dxo_ref, x_ref, halo_ref, g_ref, wp_ref, sc_ref, dx_ref, dwp_ref, dsc_ref, dg_ref,
             hbuf, pbuf, qbuf, dhbuf, wps, dwacc):
        i = pl.program_id(0)
        j = n - 1 - i

        @pl.when(i == 0)
        def _():
            qbuf[ts:ts + HALO_P, :] = jnp.zeros((HALO_P, D), F32)
            dwacc[...] = jnp.zeros(dwacc.shape, F32)
            dsc_ref[...] = jnp.zeros((1, D), F32)
            dg_ref[...] = jnp.zeros((1, D), F32)
            _assemble_wpool(wp_ref, wps)

        gv = g_ref[...]
        xl = halo_ref[...]
        hbuf[0:HALO_P, :] = jnp.where(j == 0, 0.0, xl * _rsqrt_mean_sq(xl) * gv)
        xv = x_ref[...]
        r = _rsqrt_mean_sq(xv)
        xh = xv * r
        hbuf[HALO_P:HALO_P + ts, :] = xh * gv
        _pool_windows(hbuf, pbuf, j * ts, ts)

        dy = dxo_ref[...]
        t1 = (j * ts + lax.broadcasted_iota(jnp.int32, (ts, 1), 0) + 1).astype(F32)
        for gi, w in enumerate(POOL_WINDOWS):
            cols = slice(gi * PG, (gi + 1) * PG)
            p = pbuf[:, cols]
            y = jnp.dot(p, wps[gi], preferred_element_type=F32)
            dsc_ref[:, cols] += jnp.sum(dy[:, cols] * y, axis=0, keepdims=True)
            dq = (dy[:, cols] * sc_ref[:, cols]).astype(BF16)
            dwacc[gi] += lax.dot_general(p, dq, TN_DIMS, preferred_element_type=F32)
            dp = lax.dot_general(dq, wps[gi], NT_DIMS, preferred_element_type=F32)
            qbuf[0:ts, cols] = dp / jnp.minimum(t1, float(w))

        def chunk(ci, carry):
            r0 = pl.multiple_of(ci * R_CHUNK, R_CHUNK)
            tc = (j * ts + r0 + lax.broadcasted_iota(jnp.int32, (R_CHUNK, 1), 0) + 1).astype(F32)
            for gi, w in enumerate(POOL_WINDOWS):
                cnt = jnp.minimum(tc, float(w))
                offs = list(range(w))
                for c0 in range(gi * PG, (gi + 1) * PG, LANES):
                    sh = _shifted(qbuf, r0, R_CHUNK, c0, LANES, offs)
                    tot = sh[0]
                    for o in offs[1:]:
                        tot = tot + sh[o]
                    dhbuf[pl.ds(r0, R_CHUNK), c0:c0 + LANES] = tot - sh[0] * cnt
            return carry

        lax.fori_loop(0, ts // R_CHUNK, chunk, 0)
        qbuf[ts:ts + HALO_P, :] = qbuf[0:HALO_P, :]
        dh = dhbuf[...]
        dg_ref[...] += jnp.sum(dh * xh, axis=0, keepdims=True)
        dx_ref[...] = dy + _rms_bwd(dh, xh, r, gv)

        @pl.when(i == n - 1)
        def _():
            for gi in range(ng):
                for jj in range(NCHIP):
                    dwp_ref[jj, gi] = dwacc[gi, jj * rb:(jj + 1) * rb, :]

    rev = pl.BlockSpec((ts, D), lambda i: (n - 1 - i, 0))
    halo = pl.BlockSpec((HALO_P, D), lambda i: (jnp.maximum((n - 1 - i) * (ts // HALO_P) - 1, 0), 0))
    vec = pl.BlockSpec((1, D), lambda i: (0, 0))
    return _call(
        body, name="pool_bwd", grid=(n,), comm=comm, args=(dxo, x, x, g, w_pool, scale),
        in_specs=[rev, rev, halo, _const((1, D)), _const((NCHIP, ng, rb, PG)), _const((1, D))],
        out_specs=[rev, pl.BlockSpec((NCHIP, ng, rb, PG), lambda i: (0, 0, 0, 0)), vec, vec],
        out_shape=[jax.ShapeDtypeStruct((s, D), F32), jax.ShapeDtypeStruct((NCHIP, ng, rb, PG), F32),
                   jax.ShapeDtypeStruct((1, D), F32), jax.ShapeDtypeStruct((1, D), F32)],
        scratch=[pltpu.VMEM((HALO_P + ts, D), F32), pltpu.VMEM((ts, D), BF16), pltpu.VMEM((ts + HALO_P, D), F32),
                 pltpu.VMEM((ts, D), F32), pltpu.VMEM((ng, PG, PG), BF16), pltpu.VMEM((ng, PG, PG), F32)])


def _mix0_bwd_a(dx1, z, ac, bconv, w_out, conv_a, ln_g, ln_b, conv_b, comm=None):
    s = dx1.shape[0]
    ts = min(TS_MIXB, s)
    n = s // ts
    rows = 32
    boffs_a = [K_A - 1 - k for k in range(K_A)]
    boffs_s = [K_S - 1 - k for k in range(K_S)]

    def body(dx1_ref, z_ref, ac_ref, bconv_ref, wout_ref, ca_ref, lg_ref, lb_ref, cb_ref,
             dz_ref, dca_ref, dlg_ref, dlb_ref, dcb_ref, dac_buf, dbc_buf, dca_acc, dcb_acc):
        i = pl.program_id(0)

        @pl.when(i == 0)
        def _():
            dac_buf[ts:ts + HALO_A, :] = jnp.zeros((HALO_A, A), F32)
            dbc_buf[ts:ts + HALO_S, :] = jnp.zeros((HALO_S, A), F32)
            dca_acc[...] = jnp.zeros(dca_acc.shape, F32)
            dcb_acc[...] = jnp.zeros(dcb_acc.shape, F32)
            dlg_ref[...] = jnp.zeros((1, A), F32)
            dlb_ref[...] = jnp.zeros((1, A), F32)

        dcat = lax.dot_general(dx1_ref[...].astype(BF16), wout_ref[...], NT_DIMS, preferred_element_type=F32)
        db = dcat[:, A:2 * A]
        dz_ref[:, 2 * A:3 * A] = (db * bconv_ref[...]).astype(BF16)
        dbc_buf[0:ts, :] = db * z_ref[:, 2 * A:3 * A]
        ac = ac_ref[...]
        xc = ac - jnp.mean(ac, axis=-1, keepdims=True)
        rstd = lax.rsqrt(jnp.mean(xc * xc, axis=-1, keepdims=True) + LN_EPS)
        xn = xc * rstd
        lg = lg_ref[...]
        ln = xn * lg + lb_ref[...]
        sl = _sigmoid(ln)
        dln = dcat[:, 0:A] * sl * (1.0 + ln * (1.0 - sl))
        dlg_ref[...] += jnp.sum(dln * xn, axis=0, keepdims=True)
        dlb_ref[...] += jnp.sum(dln, axis=0, keepdims=True)
        dxn = dln * lg
        dac_buf[0:ts, :] = rstd * (dxn - jnp.mean(dxn, axis=-1, keepdims=True)
                                   - xn * jnp.mean(dxn * xn, axis=-1, keepdims=True))

        def chunk(ci, carry):
            r0 = pl.multiple_of(ci * rows, rows)
            rs = pl.ds(r0, rows)
            for c0 in range(0, A, LANES):
                col = lambda grp: slice(grp * A + c0, grp * A + c0 + LANES)
                a_val, sg = z_ref[rs, col(0)], _sigmoid(z_ref[rs, col(1)])
                dglu = _conv_corr(dac_buf, ca_ref, a_val * sg, dca_acc, r0, rows, c0, LANES, boffs_a)
                dz_ref[rs, col(0)] = (dglu * sg).astype(BF16)
                dz_ref[rs, col(1)] = (dglu * a_val * sg * (1.0 - sg)).astype(BF16)
                c_gate, bc_val = z_ref[rs, col(3)], z_ref[rs, col(4)]
                dcv = _conv_corr(dbc_buf, cb_ref, c_gate * bc_val, dcb_acc, r0, rows, c0, LANES, boffs_s)
                dz_ref[rs, col(3)] = (dcv * bc_val).astype(BF16)
                dz_ref[rs, col(4)] = (dcv * c_gate).astype(BF16)
            return carry

        lax.fori_loop(0, ts // rows, chunk, 0)
        dac_buf[ts:ts + HALO_A, :] = dac_buf[0:HALO_A, :]
        dbc_buf[ts:ts + HALO_S, :] = dbc_buf[0:HALO_S, :]

        @pl.when(i == n - 1)
        def _():
            _finish_tap_sums(dca_acc, dca_ref, K_A)
            _finish_tap_sums(dcb_acc, dcb_ref, K_S)

    rev = lambda w: pl.BlockSpec((ts, w), lambda i: (n - 1 - i, 0))
    full = lambda r, c: pl.BlockSpec((r, c), lambda i: (0, 0))
    return _call(
        body, name="mix0_bwd_a", grid=(n,), comm=comm, args=(dx1, z, ac, bconv, w_out, conv_a, ln_g, ln_b, conv_b),
        in_specs=[rev(D), rev(NZ), rev(A), rev(A), _const((2 * A, D)), _const((K_A, A)), _const((1, A)), _const((1, A)),
                  _const((K_S, A))],
        out_specs=[rev(NZ), full(K_A, A), full(1, A), full(1, A), full(K_S, A)],
        out_shape=[jax.ShapeDtypeStruct((s, NZ), BF16), jax.ShapeDtypeStruct((K_A, A), F32), jax.ShapeDtypeStruct((1, A), F32),
                   jax.ShapeDtypeStruct((1, A), F32), jax.ShapeDtypeStruct((K_S, A), F32)],
        scratch=[pltpu.VMEM((ts + HALO_A, A), F32), pltpu.VMEM((ts + HALO_S, A), F32), pltpu.VMEM((SUBLANES * K_A, A), F32),
                 pltpu.VMEM((SUBLANES * K_S, A), F32)])


def _cast_later_weights(w_up, w_down, w_pool, comm):
    nl = w_up.shape[0]

    def body(up_ref, dn_ref, pool_ref, *outs):
        up_o, dn_o, pool_o = outs[:nl], outs[nl:2 * nl], outs[2 * nl]
        for layer in range(nl):
            @pl.when(pl.program_id(0) == layer)
            def _(layer=layer):
                up_o[layer][...] = up_ref[...].astype(BF16)
                dn_o[layer][...] = dn_ref[...].astype(BF16)
                if layer == 0:
                    pool_o[...] = pool_ref[...].astype(BF16)

    per_layer = lambda a: pl.BlockSpec((None,) + a.shape[1:], lambda l: (l,) + (0,) * (a.ndim - 1))
    whole = lambda shape: pl.BlockSpec(shape, lambda l: (0,) * len(shape))
    out_shapes = [w_up.shape[1:]] * nl + [w_down.shape[1:]] * nl + [w_pool.shape]
    return _call(body, name="cast_later_weights", grid=(nl,), comm=comm, args=(w_up, w_down, w_pool),
                 in_specs=[per_layer(w_up), per_layer(w_down), whole(w_pool.shape)],
                 out_specs=[whole(shape) for shape in out_shapes],
                 out_shape=[jax.ShapeDtypeStruct(shape, BF16) for shape in out_shapes])


def _train_step(x, target, place, shard, later, rep, small_pack, small_shapes, adamw):
    bw_up, bw_in = FF2 // NCHIP, NZ // NCHIP
    five = lambda a, k: a.reshape(NCHIP, 2, *HALF[k])

    half = lambda a, name: a.reshape((2,) + HALF[name])
    (up0, up1, dn0, dn1, pool_bf), (g_in, g_out, small_g) = _cast_later_weights(
        *later, comm=_gather_comm([shard["w_in"], shard["w_out"]], small_pack))
    shard = dict(shard, w_pool=half(pool_bf, "w_pool"), w_up0=half(up0, "w_up"), w_up1=half(up1, "w_up"),
                 w_down0=half(dn0, "w_down"), w_down1=half(dn1, "w_down"))
    whole = {}
    for k, part in zip(SMALL_SHARDED, _unpack(small_g, small_shapes, lead=(NCHIP,))):
        whole[k] = jnp.moveaxis(part, 0, 1).reshape(part.shape[1], NCHIP * part.shape[2])
    w_in, w_out = g_in.reshape(NCHIP, D, bw_in), g_out.reshape(2 * A, D)
    conv_ffn = whole["conv_ffn_w"].reshape(2, K_S, FF2)
    nffn = [rep["norm_ffn"][0:1], rep["norm_ffn"][1:2]]

    (h0, z, ac, bconv, cat, x1), (g_up0, g_dn0) = _mix0_fwd(
        x, rep["norm_mix_even"], w_in, whole["conv_a"], rep["ln_a_g"], rep["ln_a_b"], whole["conv_b"], w_out,
        comm=_gather_comm([shard["w_up0"], shard["w_down0"]], mid_steps=GATHER_MID_STEPS["mix0_fwd"]))
    w_up0, w_dn0 = g_up0.reshape(NCHIP, D, bw_up), g_dn0.reshape(FF, D)
    (hf0, u00, u0, act0, x2), (g_pool, g_up1, g_dn1) = _ffn_fwd(
        x1, nffn[0], w_up0, conv_ffn[0], w_dn0, "ffn0_fwd",
        comm=_gather_comm([shard["w_pool"], shard["w_up1"], shard["w_down1"]], mid_steps=GATHER_MID_STEPS["ffn0_fwd"]))
    w_pool = g_pool.reshape(NCHIP, len(POOL_WINDOWS), PG // NCHIP, PG)
    w_up1, w_dn1 = g_up1.reshape(NCHIP, D, bw_up), g_dn1.reshape(FF, D)
    x3 = _pool_fwd(x2, whole["norm_mix_odd"], w_pool, whole["pool_scale"])
    (hf1, u01, u1, act1, dx4, g_nfin, loss_part), _ = _ffn_fwd(x3, nffn[1], w_up1, conv_ffn[1], w_dn1, "ffn1_fwd",
                                                                head=(rep["norm_final"], target))

    psum = lambda k, g, ld, tag="": _pair_sum(place, g, ld, "pair_sum_" + k + tag)
    tot = lambda k, g, ld, p, layer=0, nl=1, prev=None: _chip_sum(place, g, ld, p, layer, nl, "chip_sum_%s%d" % (k, layer), prev)
    pair, chips, join = _pair_comm, _chips_comm, _join_comm

    gr_dn1 = five(_wgrad(act1, dx4, 1, "wgrad_down1")[0], "w_down")
    (du01, g_wc1), _ = _ffn_bwd_a(dx4, u1, u01, conv_ffn[1], w_dn1, "ffn1_bwd_a")
    g_up, (ld_dn1,) = _wgrad(hf1, du01, NCHIP, "wgrad_up1", comm=pair([gr_dn1]))
    gr_up1 = five(g_up, "w_up")
    s_dn1 = psum("w_down", gr_dn1, ld_dn1, "1")
    (dx3, g_nf1), (p_dn1, ld_up1) = _nt_rms_bwd(du01, w_up1, x3, nffn[1], dx4, "ffn1_bwd_b",
                                                comm=join(chips([s_dn1]), pair([gr_up1])))
    s_up1 = psum("w_up", gr_up1, ld_up1, "1")
    (dx2, g_wpool, g_scale, g_nmo), _ = _pool_bwd(dx3, x2, whole["norm_mix_odd"], w_pool, whole["pool_scale"])
    gr_pool = five(g_wpool, "w_pool")
    (du00, g_wc0), (p_up1,) = _ffn_bwd_a(dx2, u0, u00, conv_ffn[0], w_dn0, "ffn0_bwd_a", comm=chips([s_up1]))
    gr_dn0 = five(_wgrad(act0, dx2, 1, "wgrad_down0")[0], "w_down")
    g_up, (ld_dn0, ld_pool) = _wgrad(hf0, du00, NCHIP, "wgrad_up0", comm=pair([gr_dn0, gr_pool]))
    gr_up0 = five(g_up, "w_up")
    s_dn0, s_pool = psum("w_down", gr_dn0, ld_dn0, "0"), psum("w_pool", gr_pool, ld_pool)
    (dx1, g_nf0), (p_dn0, p_pool, ld_up0) = _nt_rms_bwd(du00, w_up0, x1, nffn[0], dx2, "ffn0_bwd_b",
                                                        comm=join(chips([s_dn0, s_pool]), pair([gr_up0])))
    s_up0 = psum("w_up", gr_up0, ld_up0, "0")
    gr_out = five(_wgrad(cat, dx1, 1, "wgrad_out")[0], "w_out")
    (dz, g_ca, g_lg, g_lb, g_cb), (p_up0, ld_out) = _mix0_bwd_a(
        dx1, z, ac, bconv, w_out, whole["conv_a"], rep["ln_a_g"], rep["ln_a_b"], whole["conv_b"],
        comm=join(chips([s_up0]), pair([gr_out])))
    s_out = psum("w_out", gr_out, ld_out)
    t_pool = tot("w_pool", gr_pool, ld_pool, p_pool)
    t_up = tot("w_up", gr_up0, ld_up0, p_up0, 0, 2, tot("w_up", gr_up1, ld_up1, p_up1, 1, 2))
    t_dn = tot("w_down", gr_dn0, ld_dn0, p_dn0, 0, 2, tot("w_down", gr_dn1, ld_dn1, p_dn1, 1, 2))
    small = {"norm_mix_odd": g_nmo, "pool_scale": g_scale, "norm_ffn": jnp.concatenate([g_nf0, g_nf1], axis=0),
             "conv_ffn_w": jnp.stack([g_wc0, g_wc1]), "norm_final": g_nfin, "loss": loss_part,
             "conv_a": g_ca, "ln_a_g": g_lg, "ln_a_b": g_lb, "conv_b": g_cb}
    g_in, (p_out, early_all, t_pool, t_up, t_dn) = _wgrad(
        h0, dz, NCHIP, "wgrad_in",
        comm=join(chips([s_out], _pack([small[k] for k in SMALL_EARLY])), _swap_comm([t_pool, t_up, t_dn])))
    gr_in = five(g_in, "w_in")
    n_mm = x.shape[0] // min(TS_MM, x.shape[0])
    (dx_a, g_nme_a), (ld_in,) = _nt_rms_bwd(dz, w_in, x, rep["norm_mix_even"], dx1, "mix0_bwd_b0", comm=pair([gr_in]),
                                            tiles=(0, n_mm // 2))
    s_in = psum("w_in", gr_in, ld_in)
    (grad_x, g_nme_b), (p_in,) = _nt_rms_bwd(dz, w_in, x, rep["norm_mix_even"], dx1, "mix0_bwd_b1", comm=chips([s_in]),
                                             tiles=(n_mm // 2, n_mm - n_mm // 2), dx_so_far=dx_a)
    small["norm_mix_even"] = g_nme_a + g_nme_b
    (late_all,) = _run_comm(chips([], _pack([small[k] for k in SMALL_LATE])), "rs_last")
    t_in, t_out = _run_comm(_swap_comm([tot("w_in", gr_in, ld_in, p_in), tot("w_out", gr_out, ld_out, p_out)]), "rs_swap")
    done = {k: adamw(k, t) for k, t in (("w_up", t_up), ("w_down", t_dn), ("w_pool", t_pool), ("w_in", t_in), ("w_out", t_out))}
    summed = dict(zip(SMALL_EARLY, _unpack(_sum_devices(early_all, "sum_small_early"), [small[k].shape for k in SMALL_EARLY])))
    summed.update(zip(SMALL_LATE, _unpack(_sum_devices(late_all, "sum_small_late"), [small[k].shape for k in SMALL_LATE])))
    return grad_x, done, summed


def _adamw_math(w, g, m, v):
    m = ADAM_B1 * m + (1.0 - ADAM_B1) * g
    v = ADAM_B2 * v + (1.0 - ADAM_B2) * (g * g)
    m_hat = m / (1.0 - ADAM_B1 ** ADAM_STEP)
    v_hat = v / (1.0 - ADAM_B2 ** ADAM_STEP)
    return -ADAM_LR * (m_hat / (jnp.sqrt(v_hat) + ADAM_EPS) + ADAM_WD * w), m, v


def _adamw_big(w, g, m, v, tr, name, comm=None):
    nl, rows, cols = w.shape

    def body(w_ref, g_ref, m_ref, v_ref, g2_ref, d_ref, m2_ref, v2_ref):
        gv = g_ref[...]
        g2_ref[...] = gv
        d_ref[...], m2_ref[...], v2_ref[...] = _adamw_math(w_ref[...], gv, m_ref[...], v_ref[...])

    spec = pl.BlockSpec((None, tr, cols), lambda l, r: (l, r, 0))
    return _call(body, name=name, grid=(nl, rows // tr), comm=comm, args=(w, g, m, v), in_specs=[spec] * 4,
                 out_specs=[spec] * 4, out_shape=[jax.ShapeDtypeStruct(w.shape, F32)] * 4)


def _adamw_small(ws, gs, ms, vs):
    n = len(ws)

    def body(*refs):
        for p in range(n):
            w_ref, g_ref, m_ref, v_ref = (refs[q * n + p] for q in range(4))
            d_ref, m2_ref, v2_ref = (refs[(4 + q) * n + p] for q in range(3))
            d_ref[...], m2_ref[...], v2_ref[...] = _adamw_math(w_ref[...], g_ref[...], m_ref[...], v_ref[...])

    whole = lambda a: pl.BlockSpec(a.shape, lambda: (0,) * a.ndim)
    outs = pl.pallas_call(
        body, name="adamw_small", in_specs=[whole(a) for a in ws] * 4, out_specs=[whole(a) for a in ws] * 3,
        out_shape=[jax.ShapeDtypeStruct(a.shape, F32) for a in ws] * 3,
        compiler_params=pltpu.CompilerParams(vmem_limit_bytes=VMEM_LIMIT_BYTES),
    )(*ws, *gs, *ms, *vs)
    return outs[0:n], outs[n:2 * n], outs[2 * n:3 * n]


def _place():
    x, y, c = lax.axis_index("x"), lax.axis_index("y"), lax.axis_index("c")
    chips = [(x, 1 - y), (1 - x, y), (1 - x, 1 - y)]
    blocks = [2 * cx + cy for cx, cy in chips]
    return x, y, c, 2 * x + y, chips, blocks


def _gather_comm(shards, small=None, mid_steps=None):
    na = len(shards)
    ns = 0 if small is None else 1

    def copies(ins, outs, sems):
        h1_send, h1_recv, h2_send, h2_recv, f1_send, f1_recv, f2_send, f2_recv, own_send, own_recv = sems[:10]
        x, y, c, j, chips, blocks = _place()
        sib = (x, y, 1 - c)
        piece = lambda a, p: pl.ds(p * (shards[a].shape[1] // 2), shards[a].shape[1] // 2)

        def remote(src, dst, send, recv, q, to):
            return pltpu.make_async_remote_copy(src_ref=src, dst_ref=dst, send_sem=send.at[q], recv_sem=recv.at[q],
                                                device_id=to, device_id_type=MESH)

        def hop1(a, k, arrival):
            dst = outs[a].at[blocks[k], c] if arrival else outs[a].at[j, c]
            return remote(dst if arrival else ins[a].at[c], dst, h1_send, h1_recv, 2 * a + k, (*chips[k], c))

        def hop2(a, p, arrival):
            ref = outs[a].at[blocks[2] if arrival else blocks[p], c, piece(a, p)]
            return remote(ref, ref, h2_send, h2_recv, 2 * a + p, (*chips[1 - p], c))

        def fwd1(a, k, half):
            ref = outs[a].at[blocks[k], half]
            return remote(ref, ref, f1_send, f1_recv, 2 * a + k, sib)

        def fwd2(a, p, half):
            ref = outs[a].at[blocks[2], half, piece(a, p)]
            return remote(ref, ref, f2_send, f2_recv, 2 * a + p, sib)

        own = [remote(ins[a], outs[a].at[j], own_send, own_recv, a, sib) for a in range(na)]
        small_copies = [remote(ins[na], outs[na].at[j], sems[10], sems[11], k, (*chips[k], c)) for k in range(3 * ns)]
        local = [pltpu.make_async_copy(ins[na], outs[na].at[j], sems[12])] if ns else []
        return hop1, hop2, fwd1, fwd2, own, small_copies, local, c

    pairs = [(a, k) for a in range(na) for k in range(2)]

    def start(ins, outs, sems):
        hop1, _, _, _, own, small_copies, local, _ = copies(ins, outs, sems)
        for cp in local + own + [hop1(a, k, False) for a, k in pairs] + small_copies:
            cp.start()

    def mid(ins, outs, sems):
        hop1, hop2, fwd1, _, _, _, _, c = copies(ins, outs, sems)
        for a, k in pairs:
            hop1(a, k, True).wait_recv()
            hop2(a, k, False).start()
            fwd1(a, k, c).start()

    def finish(ins, outs, sems):
        hop1, hop2, fwd1, fwd2, own, small_copies, local, c = copies(ins, outs, sems)
        for a, p in pairs:
            hop2(a, p, True).wait_recv()
            fwd2(a, p, c).start()
        for cp in small_copies:
            cp.wait()
        for a, k in pairs:
            hop1(a, k, False).wait_send()
            hop2(a, k, False).wait_send()
            fwd1(a, k, c).wait_send()
            fwd1(a, k, 1 - c).wait_recv()
            fwd2(a, k, c).wait_send()
            fwd2(a, k, 1 - c).wait_recv()
        for cp in own + local:
            cp.wait()

    out_shapes = [jax.ShapeDtypeStruct((NCHIP,) + s.shape, s.dtype) for s in shards]
    sems = [pltpu.SemaphoreType.DMA((2 * na,))] * 8 + [pltpu.SemaphoreType.DMA((na,))] * 2
    if ns:
        out_shapes.append(jax.ShapeDtypeStruct((NCHIP,) + small.shape, small.dtype))
        sems += [pltpu.SemaphoreType.DMA((3,)), pltpu.SemaphoreType.DMA((3,)), pltpu.SemaphoreType.DMA]
    if mid_steps is not None:
        return _Comm(list(shards) + [small] * ns, out_shapes, sems, start, finish, mid=mid, mid_steps=mid_steps)

    def forward_and_finish(ins, outs, sems):
        mid(ins, outs, sems)
        finish(ins, outs, sems)

    return _Comm(list(shards) + [small] * ns, out_shapes, sems, start, forward_and_finish)


def _simple_comm(inputs, out_shapes, make_copies, n_sems, aliases=None):
    def start(ins, outs, sems):
        for cp in make_copies(ins, outs, sems):
            cp.start()

    def finish(ins, outs, sems):
        for cp in make_copies(ins, outs, sems):
            cp.wait()

    return _Comm(inputs, out_shapes, [pltpu.SemaphoreType.DMA((n,)) for n in n_sems], start, finish, aliases)


def _pair_comm(grads):
    def make_copies(ins, outs, sems):
        x, y, c, _, _, _ = _place()
        return [pltpu.make_async_remote_copy(
            src_ref=ins[a].at[:, 1 - c], dst_ref=outs[a], send_sem=sems[0].at[a], recv_sem=sems[1].at[a],
            device_id=(x, y, 1 - c), device_id_type=MESH) for a in range(len(grads))]

    out_shapes = [jax.ShapeDtypeStruct(g.shape[:1] + g.shape[2:], F32) for g in grads]
    return _simple_comm(grads, out_shapes, make_copies, [len(grads)] * 2)


def _chips_comm(sums, small=None):
    na = len(sums)
    nk = NCHIP - 1

    def make_copies(ins, outs, sems):
        x, y, c, _, chips, _ = _place()
        copies = [pltpu.make_async_remote_copy(
            src_ref=ins[a].at[k], dst_ref=outs[a].at[k], send_sem=sems[0].at[a * nk + k],
            recv_sem=sems[1].at[a * nk + k], device_id=(*chips[k], c), device_id_type=MESH)
            for a in range(na) for k in range(nk)]
        if small is not None:
            me = 4 * x + 2 * y + c
            for r in range(1, NDEV):
                peer = (1 - x if r & 4 else x, 1 - y if r & 2 else y, 1 - c if r & 1 else c)
                copies.append(pltpu.make_async_remote_copy(
                    src_ref=ins[na], dst_ref=outs[na].at[me], send_sem=sems[2].at[r - 1], recv_sem=sems[3].at[r - 1],
                    device_id=peer, device_id_type=MESH))
            copies.append(pltpu.make_async_copy(ins[na], outs[na].at[me], sems[4].at[0]))
        return copies

    out_shapes = [jax.ShapeDtypeStruct(g.shape, BF16) for g in sums]
    chip_sems = [max(na * nk, 1)] * 2
    if small is None:
        return _simple_comm(sums, out_shapes, make_copies, chip_sems)
    out_shapes.append(jax.ShapeDtypeStruct((NDEV,) + small.shape, F32))
    return _simple_comm(list(sums) + [small], out_shapes, make_copies, chip_sems + [NDEV - 1] * 2 + [1])


def _swap_comm(totals):
    def make_copies(ins, outs, sems):
        x, y, c, _, _, _ = _place()
        return [pltpu.make_async_remote_copy(
            src_ref=outs[a].at[:, c], dst_ref=outs[a].at[:, c], send_sem=sems[0].at[a], recv_sem=sems[1].at[a],
            device_id=(x, y, 1 - c), device_id_type=MESH) for a in range(len(totals))]

    out_shapes = [jax.ShapeDtypeStruct(t.shape, F32) for t in totals]
    return _simple_comm(totals, out_shapes, make_copies, [len(totals)] * 2, aliases={a: a for a in range(len(totals))})


def _pair_sum(place, grad, landed, name):
    _, _, rows, cols = grad.shape

    def body(place_ref, g_ref, l_ref, o_ref):
        o_ref[...] = (g_ref[...] + l_ref[...]).astype(BF16)

    other = lambda k, p: jnp.bitwise_xor(p[0], k + 1)
    return pl.pallas_call(
        body, name=name,
        grid_spec=pltpu.PrefetchScalarGridSpec(
            num_scalar_prefetch=1, grid=(NCHIP - 1,),
            in_specs=[pl.BlockSpec((None, None, rows, cols), lambda k, p: (other(k, p), p[1], 0, 0)),
                      pl.BlockSpec((None, rows, cols), lambda k, p: (other(k, p), 0, 0))],
            out_specs=pl.BlockSpec((None, rows, cols), lambda k, p: (k, 0, 0))),
        out_shape=jax.ShapeDtypeStruct((NCHIP - 1, rows, cols), BF16), compiler_params=_cparams(1),
    )(place, grad, landed)


def _chip_sum(place, grad, landed, parts, layer, n_layers, name, prev=None):
    _, _, rows, cols = grad.shape

    def body(*refs):
        g_ref, l_ref, p_ref, o_ref = refs[1], refs[2], refs[3], refs[-1]
        tot = g_ref[...] + l_ref[...]
        for k in range(NCHIP - 1):
            tot = tot + p_ref[k].astype(F32)
        o_ref[...] = tot

    in_specs = [pl.BlockSpec((None, None, rows, cols), lambda i, p: (p[0], p[1], 0, 0)),
                pl.BlockSpec((None, rows, cols), lambda i, p: (p[0], 0, 0)),
                pl.BlockSpec((NCHIP - 1, rows, cols), lambda i, p: (0, 0, 0))]
    args = [place, grad, landed, parts]
    if prev is not None:
        in_specs.append(ANY)
        args.append(prev)
    return pl.pallas_call(
        body, name=name,
        grid_spec=pltpu.PrefetchScalarGridSpec(
            num_scalar_prefetch=1, grid=(1,), in_specs=in_specs,
            out_specs=pl.BlockSpec((None, None, rows, cols), lambda i, p: (layer, p[1], 0, 0))),
        out_shape=jax.ShapeDtypeStruct((n_layers, 2, rows, cols), F32),
        input_output_aliases={} if prev is None else {4: 0}, compiler_params=_cparams(1),
    )(*args)


def _sum_devices(parts, name):
    def body(p_ref, o_ref):
        tot = p_ref[0]
        for d in range(1, NDEV):
            tot = tot + p_ref[d]
        o_ref[...] = tot

    return pl.pallas_call(
        body, name=name, in_specs=[pl.BlockSpec(parts.shape, lambda: (0, 0, 0))],
        out_specs=pl.BlockSpec(parts.shape[1:], lambda: (0, 0)), out_shape=jax.ShapeDtypeStruct(parts.shape[1:], F32),
    )(parts)


def _pack(parts):
    rows = []
    for p in parts:
        p = p.reshape(-1, LANES)
        rows.append(jnp.pad(p, ((0, -p.shape[0] % SUBLANES), (0, 0))))
    return jnp.concatenate(rows, axis=0)


def _unpack(buf, shapes, lead=()):
    out, r0 = [], 0
    nl = len(lead)
    for shp in shapes:
        nrow = 1
        for d in shp:
            nrow *= d
        nrow //= LANES
        out.append(buf[(slice(None),) * nl + (slice(r0, r0 + nrow),)].reshape(lead + tuple(shp)))
        r0 += nrow + (-nrow % SUBLANES)
    return out


WEIGHT_ORDER = ("norm_mix_even", "w_in", "conv_a", "ln_a_g", "ln_a_b", "conv_b", "w_out", "norm_mix_odd", "w_pool",
                "pool_scale", "norm_ffn", "w_up", "conv_ffn_w", "w_down", "norm_final")
BIG = ("w_in", "w_out", "w_pool", "w_up", "w_down")
HALF = {"w_in": (D // 2, NZ // NCHIP), "w_out": (2 * A // NCHIP // 2, D), "w_pool": (PG // 2, PG),
        "w_up": (D // 2, FF2 // NCHIP), "w_down": (FF // NCHIP // 2, D)}
SMALL_SHARDED = ("conv_a", "conv_b", "conv_ffn_w", "norm_mix_odd", "pool_scale")
SMALL_ALL = ("norm_mix_even", "conv_a", "ln_a_g", "ln_a_b", "conv_b", "norm_mix_odd", "pool_scale", "norm_ffn", "conv_ffn_w",
             "norm_final")
SMALL_EARLY = ("norm_mix_odd", "pool_scale", "norm_ffn", "conv_ffn_w", "norm_final", "loss", "conv_a", "ln_a_g", "ln_a_b", "conv_b")
SMALL_LATE = ("norm_mix_even",)


def kernel(x, norm_mix_even, w_in, conv_a, ln_a_g, ln_a_b, conv_b, w_out, norm_mix_odd, w_pool, pool_scale, norm_ffn, w_up, conv_ffn_w, w_down, norm_final, loss_target, m_norm_mix_even, m_w_in, m_conv_a, m_ln_a_g, m_ln_a_b, m_conv_b, m_w_out, m_norm_mix_odd, m_w_pool, m_pool_scale, m_norm_ffn, m_w_up, m_conv_ffn_w, m_w_down, m_norm_final, v_norm_mix_even, v_w_in, v_conv_a, v_ln_a_g, v_ln_a_b, v_conv_b, v_w_out, v_norm_mix_odd, v_w_pool, v_pool_scale, v_norm_ffn, v_w_up, v_conv_ffn_w, v_w_down, v_norm_final):
    w = dict(norm_mix_even=norm_mix_even, w_in=w_in, conv_a=conv_a, ln_a_g=ln_a_g, ln_a_b=ln_a_b, conv_b=conv_b, w_out=w_out,
             norm_mix_odd=norm_mix_odd, w_pool=w_pool, pool_scale=pool_scale, norm_ffn=norm_ffn, w_up=w_up,
             conv_ffn_w=conv_ffn_w, w_down=w_down, norm_final=norm_final)
    m = dict(norm_mix_even=m_norm_mix_even, w_in=m_w_in, conv_a=m_conv_a, ln_a_g=m_ln_a_g, ln_a_b=m_ln_a_b, conv_b=m_conv_b,
             w_out=m_w_out, norm_mix_odd=m_norm_mix_odd, w_pool=m_w_pool, pool_scale=m_pool_scale, norm_ffn=m_norm_ffn,
             w_up=m_w_up, conv_ffn_w=m_conv_ffn_w, w_down=m_w_down, norm_final=m_norm_final)
    v = dict(norm_mix_even=v_norm_mix_even, w_in=v_w_in, conv_a=v_conv_a, ln_a_g=v_ln_a_g, ln_a_b=v_ln_a_b, conv_b=v_conv_b,
             w_out=v_w_out, norm_mix_odd=v_norm_mix_odd, w_pool=v_w_pool, pool_scale=v_pool_scale, norm_ffn=v_norm_ffn,
             w_up=v_w_up, conv_ffn_w=v_conv_ffn_w, w_down=v_w_down, norm_final=v_norm_final)
    chip = 2 * lax.axis_index("x") + lax.axis_index("y")
    place = jnp.stack([chip, lax.axis_index("c")]).astype(jnp.int32)

    half = lambda a, name: a.astype(BF16).reshape((2,) + HALF[name])
    shard = {"w_in": half(w_in[0], "w_in"), "w_out": half(w_out[0], "w_out")}
    small_shapes = [w[k].shape[-2:] if w[k].ndim == 3 and k != "conv_ffn_w" else (w[k].size // w[k].shape[-1], w[k].shape[-1])
                    for k in SMALL_SHARDED]
    rep = dict(norm_mix_even=norm_mix_even, ln_a_g=ln_a_g, ln_a_b=ln_a_b, norm_ffn=norm_ffn, norm_final=norm_final.reshape(1, D))
    rows_per_step = {"w_in": 512, "w_out": 256, "w_pool": 256, "w_up": 256, "w_down": 352}
    as3 = lambda a: a.reshape(a.shape[0], -1, a.shape[-1])

    def adamw(k, total):
        outs, _ = _adamw_big(as3(w[k]), as3(total), as3(m[k]), as3(v[k]), rows_per_step[k], "adamw_" + k)
        return [a.reshape(w[k].shape) for a in outs]

    grad_x, done, summed = _train_step(
        x[0], loss_target[0], place, shard, (w_up, w_down, w_pool[0]), rep, _pack([w[k] for k in SMALL_SHARDED]), small_shapes,
        adamw)
    loss = summed["loss"][0, 0]

    grad, delta, new_m, new_v = ({k: done[k][q] for k in BIG} for q in range(4))
    for k in SMALL_ALL:
        gsum = summed[k]
        if k in SMALL_SHARDED:
            cols = w[k].shape[-1]
            gsum = lax.dynamic_slice_in_dim(gsum, chip * cols, cols, axis=gsum.ndim - 1)
        grad[k] = gsum.reshape(w[k].shape)

    as2 = lambda a: a.reshape(1, -1) if a.ndim == 1 else a
    ds, ms, vs = _adamw_small(*[[as2(t[k]) for k in SMALL_ALL] for t in (w, grad, m, v)])
    for k, d2, m2, v2 in zip(SMALL_ALL, ds, ms, vs):
        delta[k], new_m[k], new_v[k] = (a.reshape(w[k].shape) for a in (d2, m2, v2))

    return (loss, grad_x[None], *[grad[k] for k in WEIGHT_ORDER], *[delta[k] for k in WEIGHT_ORDER],
            *[new_m[k] for k in WEIGHT_ORDER], *[new_v[k] for k in WEIGHT_ORDER])
```

```python
import jax
import jax.numpy as jnp
from jax import lax
from jax.experimental import pallas as pl
from jax.experimental.pallas import tpu as pltpu

F32, BF16 = jnp.float32, jnp.bfloat16

D = 1024
A = 512
NZ = 5 * A
FF = 2816
FF2 = 2 * FF
NCHIP = 4
NDEV = 8
K_A, K_S = 31, 3
POOL_WINDOWS = (2, 4, 8, 16)
PG = D // len(POOL_WINDOWS)
RMS_EPS, LN_EPS = 1e-6, 1e-5
ADAM_LR, ADAM_B1, ADAM_B2, ADAM_EPS, ADAM_WD, ADAM_STEP = 0.001, 0.9, 0.999, 1e-08, 0.01, 10

HALO_A, HALO_S, HALO_P = 32, 8, 16
SUBLANES = 8
LANES = 128
VMEM_LIMIT_BYTES = 56 * 1024 * 1024

TS_MIX = 512
TS_MIXB = 512
TS_FFN = 256
TS_POOL = 512
TS_MM = 512
TS_WGRAD = 1024
ADAMW_ROWS = {"w_in": 512, "w_out": 256, "w_pool": 256, "w_up": 256, "w_down": 352}
R_CHUNK = 64
GATHER_MID_STEPS = {"mix0_fwd": 2, "ffn0_fwd": 5}

MESH = pl.DeviceIdType.MESH
ANY = pl.BlockSpec(memory_space=pl.ANY)
NT_DIMS = (((1,), (1,)), ((), ()))
TN_DIMS = (((0,), (0,)), ((), ()))


def _cparams(n_axes):
    return pltpu.CompilerParams(dimension_semantics=("arbitrary",) * n_axes, vmem_limit_bytes=VMEM_LIMIT_BYTES)


def _const(shape):
    nd = len(shape)
    return pl.BlockSpec(shape, lambda *_: (0,) * nd, pipeline_mode=pl.Buffered(1))


def _sigmoid(v):
    return 1.0 / (1.0 + jnp.exp(-v))


def _rsqrt_mean_sq(x):
    return lax.rsqrt(jnp.mean(x * x, axis=-1, keepdims=True) + RMS_EPS)


def _rms_bwd(dh, xh, r, g):
    dxh = dh * g
    return r * (dxh - xh * jnp.mean(dxh * xh, axis=-1, keepdims=True))


def _shifted(buf_ref, row0, rows, col0, width, offsets):
    lo = (min(offsets) // SUBLANES) * SUBLANES
    hi = -(-(max(offsets) + rows) // SUBLANES) * SUBLANES
    start = row0 + lo if isinstance(row0, int) else pl.multiple_of(row0 + lo, SUBLANES)
    win = buf_ref[pl.ds(start, hi - lo), col0:col0 + width]
    out = {}
    for res in sorted({(o - lo) % SUBLANES for o in offsets}):
        qs = {o: (o - lo) // SUBLANES for o in offsets if (o - lo) % SUBLANES == res}
        base = pltpu.roll(win, hi - lo - res, 0) if res else win
        for o, q in qs.items():
            out[o] = base[SUBLANES * q:SUBLANES * q + rows, :]
    return out


def _rowsum8(v):
    acc = v[0:SUBLANES, :]
    for r in range(SUBLANES, v.shape[0], SUBLANES):
        acc = acc + v[r:r + SUBLANES, :]
    return acc


def _taps(w_ref, sh, offsets, col0, width):
    acc = None
    for k, o in enumerate(offsets):
        term = w_ref[k:k + 1, col0:col0 + width] * sh[o]
        acc = term if acc is None else acc + term
    return acc


def _window_bases(buf_ref, row0, rows, col0, width, offsets):
    lo = (min(offsets) // SUBLANES) * SUBLANES
    hi = -(-(max(offsets) + rows) // SUBLANES) * SUBLANES
    start = row0 + lo if isinstance(row0, int) else pl.multiple_of(row0 + lo, SUBLANES)
    win = buf_ref[pl.ds(start, hi - lo), col0:col0 + width]
    for res in sorted({(o - lo) % SUBLANES for o in offsets}):
        taps = [(k, (o - lo) // SUBLANES * SUBLANES) for k, o in enumerate(offsets) if (o - lo) % SUBLANES == res]
        yield (pltpu.roll(win, hi - lo - res, 0) if res else win), taps


def _conv_acc(buf_ref, w_ref, row0, rows, col0, width, offsets):
    acc = None
    for base, taps in _window_bases(buf_ref, row0, rows, col0, width, offsets):
        for k, q in taps:
            term = w_ref[k:k + 1, col0:col0 + width] * base[q:q + rows, :]
            acc = term if acc is None else acc + term
    return acc


def _conv_corr(buf_ref, w_ref, other, acc_ref, row0, rows, col0, width, offsets):
    acc = None
    for base, taps in _window_bases(buf_ref, row0, rows, col0, width, offsets):
        for k, q in taps:
            sl = base[q:q + rows, :]
            term = w_ref[k:k + 1, col0:col0 + width] * sl
            acc = term if acc is None else acc + term
            acc_ref[SUBLANES * k:SUBLANES * (k + 1), col0:col0 + width] += _rowsum8(sl * other)
    return acc


def _finish_tap_sums(acc_ref, out_ref, n_taps):
    for k in range(n_taps):
        out_ref[k:k + 1, :] = jnp.sum(acc_ref[SUBLANES * k:SUBLANES * (k + 1), :], axis=0, keepdims=True)


class _Comm:
    def __init__(self, inputs, out_shapes, sems, start, finish, aliases=None, mid=None, mid_steps=1):
        self.inputs, self.out_shapes, self.sems = list(inputs), list(out_shapes), list(sems)
        self.start, self.finish, self.mid, self.aliases = start, finish, mid, dict(aliases or {})
        self.mid_steps = mid_steps


def _join_comm(a, b):
    ni, no, ns = len(a.inputs), len(a.out_shapes), len(a.sems)

    def both(name):
        def run(ins, outs, sems):
            if getattr(a, name) is not None:
                getattr(a, name)(ins[:ni], outs[:no], sems[:ns])
            if getattr(b, name) is not None:
                getattr(b, name)(ins[ni:], outs[no:], sems[ns:])
        return run

    aliases = {**a.aliases, **{ni + i: no + o for i, o in b.aliases.items()}}
    mid = both("mid") if (a.mid is not None or b.mid is not None) else None
    return _Comm(a.inputs + b.inputs, a.out_shapes + b.out_shapes, a.sems + b.sems, both("start"), both("finish"), aliases, mid)


def _call(body, *, name, grid, in_specs, out_specs, out_shape, args, scratch=(), comm=None, aliases=None):
    n_in, n_out, n_scr, n_axes = len(in_specs), len(out_specs), len(scratch), len(grid)
    params = pltpu.CompilerParams(dimension_semantics=("arbitrary",) * n_axes, vmem_limit_bytes=VMEM_LIMIT_BYTES)
    aliases = dict(aliases or {})
    if comm is None:
        outs = pl.pallas_call(body, name=name, grid=grid, in_specs=list(in_specs), out_specs=list(out_specs),
                              out_shape=list(out_shape), scratch_shapes=list(scratch), input_output_aliases=aliases,
                              compiler_params=params)(*args)
        return list(outs), []
    ci, co = len(comm.inputs), len(comm.out_shapes)

    def wrapped(*refs):
        k_in, c_in = refs[:n_in], refs[n_in:n_in + ci]
        o0 = n_in + ci
        k_out, c_out = refs[o0:o0 + n_out], refs[o0 + n_out:o0 + n_out + co]
        s0 = o0 + n_out + co
        k_scr, c_sem = refs[s0:s0 + n_scr], refs[s0 + n_scr:]
        first = pl.program_id(0) == 0
        last = pl.program_id(0) == grid[0] - 1
        for ax in range(1, n_axes):
            first = jnp.logical_and(first, pl.program_id(ax) == 0)
            last = jnp.logical_and(last, pl.program_id(ax) == grid[ax] - 1)

        @pl.when(first)
        def _():
            comm.start(c_in, c_out, c_sem)

        mid_early = comm.mid is not None and n_axes == 1 and grid[0] > comm.mid_steps
        if mid_early:
            @pl.when(pl.program_id(0) == grid[0] - 1 - comm.mid_steps)
            def _():
                comm.mid(c_in, c_out, c_sem)

        body(*k_in, *k_out, *k_scr)

        @pl.when(last)
        def _():
            if comm.mid is not None and not mid_early:
                comm.mid(c_in, c_out, c_sem)
            comm.finish(c_in, c_out, c_sem)

    outs = pl.pallas_call(
        wrapped, name=name, grid=grid, in_specs=list(in_specs) + [ANY] * ci, out_specs=list(out_specs) + [ANY] * co,
        out_shape=list(out_shape) + comm.out_shapes, scratch_shapes=list(scratch) + comm.sems,
        input_output_aliases={**aliases, **{n_in + i: n_out + o for i, o in comm.aliases.items()}}, compiler_params=params,
    )(*args, *comm.inputs)
    return list(outs[:n_out]), list(outs[n_out:])


def _run_comm(comm, name):
    ci, co = len(comm.inputs), len(comm.out_shapes)

    def body(*refs):
        c_in, c_out, c_sem = refs[:ci], refs[ci:ci + co], refs[ci + co:]
        comm.start(c_in, c_out, c_sem)
        if comm.mid is not None:
            comm.mid(c_in, c_out, c_sem)
        comm.finish(c_in, c_out, c_sem)

    return list(pl.pallas_call(body, name=name, in_specs=[ANY] * ci, out_specs=[ANY] * co, out_shape=comm.out_shapes,
                               scratch_shapes=comm.sems, input_output_aliases=comm.aliases)(*comm.inputs))


def _mix0_fwd(x, g, w_in, conv_a, ln_g, ln_b, conv_b, w_out, comm=None):
    s = x.shape[0]
    ts = min(TS_MIX, s)
    n = s // ts
    bw = NZ // NCHIP
    offs_a = [HALO_A - (K_A - 1) + k for k in range(K_A)]
    offs_s = [HALO_S - (K_S - 1) + k for k in range(K_S)]

    def body(x_ref, g_ref, win_ref, ca_ref, lg_ref, lb_ref, cb_ref, wout_ref,
             h_ref, z_ref, ac_ref, bconv_buf, cat_ref, x1_ref, glu_buf, cv_buf):
        i = pl.program_id(0)

        @pl.when(i == 0)
        def _():
            glu_buf[0:HALO_A, :] = jnp.zeros((HALO_A, A), F32)
            cv_buf[0:HALO_S, :] = jnp.zeros((HALO_S, A), F32)

        xv = x_ref[...]
        h = (xv * _rsqrt_mean_sq(xv) * g_ref[...]).astype(BF16)
        h_ref[...] = h
        for j in range(NCHIP):
            z_ref[:, j * bw:(j + 1) * bw] = jnp.dot(h, win_ref[j], preferred_element_type=F32)
        glu_buf[HALO_A:HALO_A + ts, :] = z_ref[:, 0:A] * _sigmoid(z_ref[:, A:2 * A])
        cv_buf[HALO_S:HALO_S + ts, :] = z_ref[:, 3 * A:4 * A] * z_ref[:, 4 * A:5 * A]

        def chunk(ci, carry):
            r0 = pl.multiple_of(ci * R_CHUNK, R_CHUNK)
            for c0 in range(0, A, LANES):
                ac_ref[pl.ds(r0, R_CHUNK), c0:c0 + LANES] = _conv_acc(glu_buf, ca_ref, r0, R_CHUNK, c0, LANES, offs_a)
                bconv_buf[pl.ds(r0, R_CHUNK), c0:c0 + LANES] = _conv_acc(cv_buf, cb_ref, r0, R_CHUNK, c0, LANES, offs_s)
            return carry

        lax.fori_loop(0, ts // R_CHUNK, chunk, 0)
        glu_buf[0:HALO_A, :] = glu_buf[ts:ts + HALO_A, :]
        cv_buf[0:HALO_S, :] = cv_buf[ts:ts + HALO_S, :]

        ac = ac_ref[...]
        xc = ac - jnp.mean(ac, axis=-1, keepdims=True)
        xn = xc * lax.rsqrt(jnp.mean(xc * xc, axis=-1, keepdims=True) + LN_EPS)
        ln = xn * lg_ref[...] + lb_ref[...]
        cat_ref[:, 0:A] = (ln * _sigmoid(ln)).astype(BF16)
        cat_ref[:, A:2 * A] = (z_ref[:, 2 * A:3 * A] * bconv_buf[...]).astype(BF16)
        x1_ref[...] = xv + jnp.dot(cat_ref[...], wout_ref[...], preferred_element_type=F32)

    tile = lambda w: pl.BlockSpec((ts, w), lambda i: (i, 0))
    return _call(
        body, name="mix0_fwd", grid=(n,), comm=comm, args=(x, g, w_in, conv_a, ln_g, ln_b, conv_b, w_out),
        in_specs=[tile(D), _const((1, D)), _const((NCHIP, D, bw)), _const((K_A, A)), _const((1, A)), _const((1, A)),
                  _const((K_S, A)), _const((2 * A, D))],
        out_specs=[tile(D), tile(NZ), tile(A), tile(A), tile(2 * A), tile(D)],
        out_shape=[jax.ShapeDtypeStruct((s, D), BF16), jax.ShapeDtypeStruct((s, NZ), F32), jax.ShapeDtypeStruct((s, A), F32),
                   jax.ShapeDtypeStruct((s, A), F32), jax.ShapeDtypeStruct((s, 2 * A), BF16), jax.ShapeDtypeStruct((s, D), F32)],
        scratch=[pltpu.VMEM((HALO_A + ts, A), F32), pltpu.VMEM((HALO_S + ts, A), F32)])


def _ffn_fwd(x, g, w_up, wc, w_down, name, comm=None, head=None):
    s = x.shape[0]
    ts = min(TS_FFN, s)
    n = s // ts
    bw = FF2 // NCHIP
    rows = 32
    offs = [HALO_S - (K_S - 1) + k for k in range(K_S)]
    n_in = 5 + (2 if head else 0)

    def body(*refs):
        x_ref, g_ref, wup_ref, wc_ref, wdn_ref = refs[:5]
        h_ref, u0_ref, u_ref, act_ref, xo_ref = refs[n_in:n_in + 5]
        cbuf = refs[-1]
        i = pl.program_id(0)

        @pl.when(i == 0)
        def _():
            cbuf[0:HALO_S, :] = jnp.zeros((HALO_S, FF2), F32)
            if head:
                refs[n_in + 5][...] = jnp.zeros((1, D), F32)
                refs[n_in + 6][...] = jnp.zeros((1, LANES), F32)

        xv = x_ref[...]
        h = (xv * _rsqrt_mean_sq(xv) * g_ref[...]).astype(BF16)
        h_ref[...] = h
        f = None
        for p in range(NCHIP // 2):
            for j in (p, NCHIP // 2 + p):
                zc = jnp.dot(h, wup_ref[j], preferred_element_type=F32)
                u0_ref[:, j * bw:(j + 1) * bw] = zc.astype(BF16)
                cbuf[HALO_S:HALO_S + ts, j * bw:(j + 1) * bw] = zc
            for r0 in range(0, ts, rows):
                for c0 in range(p * bw, (p + 1) * bw, LANES):
                    ug = _taps(wc_ref, _shifted(cbuf, r0, rows, c0, LANES, offs), offs, c0, LANES)
                    uv = _taps(wc_ref, _shifted(cbuf, r0, rows, FF + c0, LANES, offs), offs, FF + c0, LANES)
                    u_ref[r0:r0 + rows, c0:c0 + LANES] = ug
                    u_ref[r0:r0 + rows, FF + c0:FF + c0 + LANES] = uv
                    act_ref[r0:r0 + rows, c0:c0 + LANES] = (ug * _sigmoid(ug) * uv).astype(BF16)
            fp = jnp.dot(act_ref[:, p * bw:(p + 1) * bw], wdn_ref[p * bw:(p + 1) * bw, :], preferred_element_type=F32)
            f = fp if f is None else f + fp
        cbuf[0:HALO_S, :] = cbuf[ts:ts + HALO_S, :]
        if not head:
            xo_ref[...] = xv + f
        else:
            gf_ref, t_ref, dgf_ref, loss_ref = refs[5], refs[6], refs[n_in + 5], refs[n_in + 6]
            xo = xv + f
            r = _rsqrt_mean_sq(xo)
            xh = xo * r
            gv = gf_ref[...]
            err = xh * gv - t_ref[...]
            loss_ref[...] += jnp.sum(jnp.sum(err * err, axis=1, keepdims=True), axis=0, keepdims=True) * (0.5 / D)
            dy = err * (1.0 / D)
            dgf_ref[...] += jnp.sum(dy * xh, axis=0, keepdims=True)
            xo_ref[...] = _rms_bwd(dy, xh, r, gv)

    tile = lambda w: pl.BlockSpec((ts, w), lambda i: (i, 0))
    one_row = lambda w: pl.BlockSpec((1, w), lambda i: (0, 0))
    return _call(
        body, name=name, grid=(n,), comm=comm, args=(x, g, w_up, wc, w_down) + (tuple(head) if head else ()),
        in_specs=[tile(D), _const((1, D)), _const((NCHIP, D, bw)), _const((K_S, FF2)), _const((FF, D))]
        + ([_const((1, D)), tile(D)] if head else []),
        out_specs=[tile(D), tile(FF2), tile(FF2), tile(FF), tile(D)] + ([one_row(D), one_row(LANES)] if head else []),
        out_shape=[jax.ShapeDtypeStruct((s, D), BF16), jax.ShapeDtypeStruct((s, FF2), BF16), jax.ShapeDtypeStruct((s, FF2), F32),
                   jax.ShapeDtypeStruct((s, FF), BF16), jax.ShapeDtypeStruct((s, D), F32)]
        + ([jax.ShapeDtypeStruct((1, D), F32), jax.ShapeDtypeStruct((1, LANES), F32)] if head else []),
        scratch=[pltpu.VMEM((HALO_S + ts, FF2), F32)])


def _pool_windows(hbuf, pbuf, tile_row0, ts):
    def chunk(ci, carry):
        r0 = pl.multiple_of(ci * R_CHUNK, R_CHUNK)
        t1 = (tile_row0 + r0 + lax.broadcasted_iota(jnp.int32, (R_CHUNK, 1), 0) + 1).astype(F32)
        for gi, w in enumerate(POOL_WINDOWS):
            cnt = jnp.minimum(t1, float(w))
            offs = [HALO_P - jj for jj in range(w)]
            for c0 in range(gi * PG, (gi + 1) * PG, LANES):
                sh = _shifted(hbuf, r0, R_CHUNK, c0, LANES, offs)
                tot = sh[offs[0]]
                for o in offs[1:]:
                    tot = tot + sh[o]
                pbuf[pl.ds(r0, R_CHUNK), c0:c0 + LANES] = (tot / cnt - sh[HALO_P]).astype(BF16)
        return carry

    lax.fori_loop(0, ts // R_CHUNK, chunk, 0)


def _assemble_wpool(wp_ref, wps):
    rb = PG // NCHIP
    for gi in range(len(POOL_WINDOWS)):
        for j in range(NCHIP):
            wps[gi, j * rb:(j + 1) * rb, :] = wp_ref[j, gi]


def _pool_fwd(x, g, w_pool, scale):
    s = x.shape[0]
    ts = min(TS_POOL, s)
    n = s // ts
    ng = len(POOL_WINDOWS)

    def body(x_ref, g_ref, wp_ref, sc_ref, xo_ref, hbuf, pbuf, wps):
        i = pl.program_id(0)

        @pl.when(i == 0)
        def _():
            hbuf[0:HALO_P, :] = jnp.zeros((HALO_P, D), F32)
            _assemble_wpool(wp_ref, wps)

        xv = x_ref[...]
        hbuf[HALO_P:HALO_P + ts, :] = xv * _rsqrt_mean_sq(xv) * g_ref[...]
        _pool_windows(hbuf, pbuf, i * ts, ts)
        hbuf[0:HALO_P, :] = hbuf[ts:ts + HALO_P, :]
        for gi in range(ng):
            cols = slice(gi * PG, (gi + 1) * PG)
            y = jnp.dot(pbuf[:, cols], wps[gi], preferred_element_type=F32)
            xo_ref[:, cols] = xv[:, cols] + y * sc_ref[:, cols]

    tile = pl.BlockSpec((ts, D), lambda i: (i, 0))
    return pl.pallas_call(
        body, name="pool_fwd", grid=(n,),
        in_specs=[tile, _const((1, D)), _const((NCHIP, ng, PG // NCHIP, PG)), _const((1, D))],
        out_specs=tile, out_shape=jax.ShapeDtypeStruct((s, D), F32),
        scratch_shapes=[pltpu.VMEM((HALO_P + ts, D), F32), pltpu.VMEM((ts, D), BF16), pltpu.VMEM((ng, PG, PG), BF16)],
        compiler_params=_cparams(1),
    )(x, g, w_pool, scale)


def _ffn_bwd_a(dxo, u, u0, wc, w_down, name, comm=None):
    s = dxo.shape[0]
    ts = min(TS_FFN, s)
    n = s // ts
    cw = FF2 // NCHIP
    rows = 32
    lw2 = 2 * LANES
    boffs = [K_S - 1 - k for k in range(K_S)]

    def body(dxo_ref, u_ref, u0_ref, wc_ref, wdn_ref, du0_ref, dwc_ref, dubuf, dact, dwacc):
        i = pl.program_id(0)

        @pl.when(i == 0)
        def _():
            dubuf[ts:ts + HALO_S, :] = jnp.zeros((HALO_S, FF2), F32)
            dwacc[...] = jnp.zeros(dwacc.shape, F32)

        df = dxo_ref[...].astype(BF16)
        for cg in range(0, FF, cw):
            dact[...] = lax.dot_general(df, wdn_ref[cg:cg + cw, :], NT_DIMS, preferred_element_type=F32)

            def chunk(ci, carry, cg=cg):
                rs = pl.ds(pl.multiple_of(ci * rows, rows), rows)
                for c in range(cg, cg + cw, LANES):
                    ug, uv = u_ref[rs, c:c + LANES], u_ref[rs, FF + c:FF + c + LANES]
                    sg = _sigmoid(ug)
                    da = dact[rs, c - cg:c - cg + LANES]
                    gs = ug * sg
                    dubuf[rs, c:c + LANES] = da * uv * (sg + gs * (1.0 - sg))
                    dubuf[rs, FF + c:FF + c + LANES] = da * gs
                return carry

            lax.fori_loop(0, ts // rows, chunk, 0)

        def chunk2(ci, carry):
            r0 = pl.multiple_of(ci * rows, rows)
            rs = pl.ds(r0, rows)
            for c in range(0, FF2, lw2):
                sh = _shifted(dubuf, r0, rows, c, lw2, boffs)
                du0_ref[rs, c:c + lw2] = _taps(wc_ref, sh, boffs, c, lw2).astype(BF16)
                u0v = u0_ref[rs, c:c + lw2].astype(F32)
                for k, o in enumerate(boffs):
                    dwacc[SUBLANES * k:SUBLANES * (k + 1), c:c + lw2] += _rowsum8(sh[o] * u0v)
            return carry

        lax.fori_loop(0, ts // rows, chunk2, 0)
        dubuf[ts:ts + HALO_S, :] = dubuf[0:HALO_S, :]

        @pl.when(i == n - 1)
        def _():
            _finish_tap_sums(dwacc, dwc_ref, K_S)

    rev = lambda w: pl.BlockSpec((ts, w), lambda i: (n - 1 - i, 0))
    return _call(
        body, name=name, grid=(n,), comm=comm, args=(dxo, u, u0, wc, w_down),
        in_specs=[rev(D), rev(FF2), rev(FF2), _const((K_S, FF2)), _const((FF, D))],
        out_specs=[rev(FF2), pl.BlockSpec((K_S, FF2), lambda i: (0, 0))],
        out_shape=[jax.ShapeDtypeStruct((s, FF2), BF16), jax.ShapeDtypeStruct((K_S, FF2), F32)],
        scratch=[pltpu.VMEM((ts + HALO_S, FF2), F32), pltpu.VMEM((ts, cw), F32), pltpu.VMEM((SUBLANES * K_S, FF2), F32)])


def _nt_rms_bwd(dy, w, x, g, dres, name, comm=None, tiles=None, dx_so_far=None):
    s = x.shape[0]
    ts = min(TS_MM, s)
    first, n = (0, s // ts) if tiles is None else tiles
    nw = dy.shape[1]
    bw = nw // NCHIP

    def body(dy_ref, w_ref, x_ref, g_ref, dres_ref, dx_ref, dg_ref):
        i = pl.program_id(0)

        @pl.when(i == 0)
        def _():
            dg_ref[...] = jnp.zeros((1, D), F32)

        dh = lax.dot_general(dy_ref[:, 0:bw], w_ref[0], NT_DIMS, preferred_element_type=F32)
        for j in range(1, NCHIP):
            dh = dh + lax.dot_general(dy_ref[:, j * bw:(j + 1) * bw], w_ref[j], NT_DIMS, preferred_element_type=F32)
        xv = x_ref[...]
        r = _rsqrt_mean_sq(xv)
        xh = xv * r
        dg_ref[...] += jnp.sum(dh * xh, axis=0, keepdims=True)
        dx_ref[...] = dres_ref[...] + _rms_bwd(dh, xh, r, g_ref[...])

    def body_with_alias(dy_ref, w_ref, x_ref, g_ref, dres_ref, _, dx_ref, dg_ref):
        body(dy_ref, w_ref, x_ref, g_ref, dres_ref, dx_ref, dg_ref)

    tile = lambda wd: pl.BlockSpec((ts, wd), lambda i: (first + i, 0))
    in_specs = [tile(nw), _const((NCHIP, D, bw)), tile(D), _const((1, D)), tile(D)]
    more = dx_so_far is not None
    return _call(
        body_with_alias if more else body, name=name, grid=(n,), comm=comm,
        args=(dy, w, x, g, dres) + ((dx_so_far,) if more else ()), in_specs=in_specs + [ANY] * more,
        out_specs=[tile(D), pl.BlockSpec((1, D), lambda i: (0, 0))], aliases={5: 0} if more else None,
        out_shape=[jax.ShapeDtypeStruct((s, D), F32), jax.ShapeDtypeStruct((1, D), F32)])


def _wgrad(a, b, n_blocks, name, comm=None):
    s, m = a.shape
    bw = b.shape[1] // n_blocks
    tk = min(TS_WGRAD, s)

    def body(a_ref, b_ref, o_ref):
        @pl.when(pl.program_id(0) == 0)
        def _():
            o_ref[...] = jnp.zeros(o_ref.shape, F32)

        for j in range(n_blocks):
            o_ref[j] += lax.dot_general(a_ref[...], b_ref[:, j * bw:(j + 1) * bw].astype(BF16), TN_DIMS,
                                        preferred_element_type=F32)

    (out,), comm_out = _call(
        body, name=name, grid=(s // tk,), comm=comm, args=(a, b),
        in_specs=[pl.BlockSpec((tk, m), lambda k: (k, 0)), pl.BlockSpec((tk, b.shape[1]), lambda k: (k, 0))],
        out_specs=[pl.BlockSpec((n_blocks, m, bw), lambda k: (0, 0, 0))],
        out_shape=[jax.ShapeDtypeStruct((n_blocks, m, bw), F32)])
    return out, comm_out


def _pool_bwd(dxo, x, g, w_pool, scale, comm=None):
    s = x.shape[0]
    ts = min(TS_POOL, s)
    n = s // ts
    ng = len(POOL_WINDOWS)
    rb = PG // NCHIP

    def body(dxo_ref, x_ref, halo_ref, g_ref, wp_ref, sc_ref, dx_ref, dwp_ref, dsc_ref, dg_ref,
             hbuf, pbuf, qbuf, dhbuf, wps, dwacc):
        i = pl.program_id(0)
        j = n - 1 - i

        @pl.when(i == 0)
        def _():
            qbuf[ts:ts + HALO_P, :] = jnp.zeros((HALO_P, D), F32)
            dwacc[...] = jnp.zeros(dwacc.shape, F32)
            dsc_ref[...] = jnp.zeros((1, D), F32)
            dg_ref[...] = jnp.zeros((1, D), F32)
            _assemble_wpool(wp_ref, wps)

        gv = g_ref[...]
        xl = halo_ref[...]
        hbuf[0:HALO_P, :] = jnp.where(j == 0, 0.0, xl * _rsqrt_mean_sq(xl) * gv)
        xv = x_ref[...]
        r = _rsqrt_mean_sq(xv)
        xh = xv * r
        hbuf[HALO_P:HALO_P + ts, :] = xh * gv
        _pool_windows(hbuf, pbuf, j * ts, ts)

        dy = dxo_ref[...]
        t1 = (j * ts + lax.broadcasted_iota(jnp.int32, (ts, 1), 0) + 1).astype(F32)
        for gi, w in enumerate(POOL_WINDOWS):
            cols = slice(gi * PG, (gi + 1) * PG)
            p = pbuf[:, cols]
            y = jnp.dot(p, wps[gi], preferred_element_type=F32)
            dsc_ref[:, cols] += jnp.sum(dy[:, cols] * y, axis=0, keepdims=True)
            dq = (dy[:, cols] * sc_ref[:, cols]).astype(BF16)
            dwacc[gi] += lax.dot_general(p, dq, TN_DIMS, preferred_element_type=F32)
            dp = lax.dot_general(dq, wps[gi], NT_DIMS, preferred_element_type=F32)
            qbuf[0:ts, cols] = dp / jnp.minimum(t1, float(w))

        def chunk(ci, carry):
            r0 = pl.multiple_of(ci * R_CHUNK, R_CHUNK)
            tc = (j * ts + r0 + lax.broadcasted_iota(jnp.int32, (R_CHUNK, 1), 0) + 1).astype(F32)
            for gi, w in enumerate(POOL_WINDOWS):
                cnt = jnp.minimum(tc, float(w))
                offs = list(range(w))
                for c0 in range(gi * PG, (gi + 1) * PG, LANES):
                    sh = _shifted(qbuf, r0, R_CHUNK, c0, LANES, offs)
                    tot = sh[0]
                    for o in offs[1:]:
                        tot = tot + sh[o]
                    dhbuf[pl.ds(r0, R_CHUNK), c0:c0 + LANES] = tot - sh[0] * cnt
            return carry

        lax.fori_loop(0, ts // R_CHUNK, chunk, 0)
        qbuf[ts:ts + HALO_P, :] = qbuf[0:HALO_P, :]
        dh = dhbuf[...]
        dg_ref[...] += jnp.sum(dh * xh, axis=0, keepdims=True)
        dx_ref[...] = dy + _rms_bwd(dh, xh, r, gv)

        @pl.when(i == n - 1)
        def _():
            for gi in range(ng):
                for jj in range(NCHIP):
                    dwp_ref[jj, gi] = dwacc[gi, jj * rb:(jj + 1) * rb, :]

    rev = pl.BlockSpec((ts, D), lambda i: (n - 1 - i, 0))
    halo = pl.BlockSpec((HALO_P, D), lambda i: (jnp.maximum((n - 1 - i) * (ts // HALO_P) - 1, 0), 0))
    vec = pl.BlockSpec((1, D), lambda i: (0, 0))
    return _call(
        body, name="pool_bwd", grid=(n,), comm=comm, args=(dxo, x, x, g, w_pool, scale),
        in_specs=[rev, rev, halo, _const((1, D)), _const((NCHIP, ng, rb, PG)), _const((1, D))],
        out_specs=[rev, pl.BlockSpec((NCHIP, ng, rb, PG), lambda i: (0, 0, 0, 0)), vec, vec],
        out_shape=[jax.ShapeDtypeStruct((s, D), F32), jax.ShapeDtypeStruct((NCHIP, ng, rb, PG), F32),
                   jax.ShapeDtypeStruct((1, D), F32), jax.ShapeDtypeStruct((1, D), F32)],
        scratch=[pltpu.VMEM((HALO_P + ts, D), F32), pltpu.VMEM((ts, D), BF16), pltpu.VMEM((ts + HALO_P, D), F32),
                 pltpu.VMEM((ts, D), F32), pltpu.VMEM((ng, PG, PG), BF16), pltpu.VMEM((ng, PG, PG), F32)])


def _mix0_bwd_a(dx1, z, ac, bconv, w_out, conv_a, ln_g, ln_b, conv_b, comm=None):
    s = dx1.shape[0]
    ts = min(TS_MIXB, s)
    n = s // ts
    rows = 32
    boffs_a = [K_A - 1 - k for k in range(K_A)]
    boffs_s = [K_S - 1 - k for k in range(K_S)]

    def body(dx1_ref, z_ref, ac_ref, bconv_ref, wout_ref, ca_ref, lg_ref, lb_ref, cb_ref,
             dz_ref, dca_ref, dlg_ref, dlb_ref, dcb_ref, dac_buf, dbc_buf, dca_acc, dcb_acc):
        i = pl.program_id(0)

        @pl.when(i == 0)
        def _():
            dac_buf[ts:ts + HALO_A, :] = jnp.zeros((HALO_A, A), F32)
            dbc_buf[ts:ts + HALO_S, :] = jnp.zeros((HALO_S, A), F32)
            dca_acc[...] = jnp.zeros(dca_acc.shape, F32)
            dcb_acc[...] = jnp.zeros(dcb_acc.shape, F32)
            dlg_ref[...] = jnp.zeros((1, A), F32)
            dlb_ref[...] = jnp.zeros((1, A), F32)

        dcat = lax.dot_general(dx1_ref[...].astype(BF16), wout_ref[...], NT_DIMS, preferred_element_type=F32)
        db = dcat[:, A:2 * A]
        dz_ref[:, 2 * A:3 * A] = (db * bconv_ref[...]).astype(BF16)
        dbc_buf[0:ts, :] = db * z_ref[:, 2 * A:3 * A]
        ac = ac_ref[...]
        xc = ac - jnp.mean(ac, axis=-1, keepdims=True)
        rstd = lax.rsqrt(jnp.mean(xc * xc, axis=-1, keepdims=True) + LN_EPS)
        xn = xc * rstd
        lg = lg_ref[...]
        ln = xn * lg + lb_ref[...]
        sl = _sigmoid(ln)
        dln = dcat[:, 0:A] * sl * (1.0 + ln * (1.0 - sl))
        dlg_ref[...] += jnp.sum(dln * xn, axis=0, keepdims=True)
        dlb_ref[...] += jnp.sum(dln, axis=0, keepdims=True)
        dxn = dln * lg
        dac_buf[0:ts, :] = rstd * (dxn - jnp.mean(dxn, axis=-1, keepdims=True)
                                   - xn * jnp.mean(dxn * xn, axis=-1, keepdims=True))

        def chunk(ci, carry):
            r0 = pl.multiple_of(ci * rows, rows)
            rs = pl.ds(r0, rows)
            for c0 in range(0, A, LANES):
                col = lambda grp: slice(grp * A + c0, grp * A + c0 + LANES)
                a_val, sg = z_ref[rs, col(0)], _sigmoid(z_ref[rs, col(1)])
                dglu = _conv_corr(dac_buf, ca_ref, a_val * sg, dca_acc, r0, rows, c0, LANES, boffs_a)
                dz_ref[rs, col(0)] = (dglu * sg).astype(BF16)
                dz_ref[rs, col(1)] = (dglu * a_val * sg * (1.0 - sg)).astype(BF16)
                c_gate, bc_val = z_ref[rs, col(3)], z_ref[rs, col(4)]
                dcv = _conv_corr(dbc_buf, cb_ref, c_gate * bc_val, dcb_acc, r0, rows, c0, LANES, boffs_s)
                dz_ref[rs, col(3)] = (dcv * bc_val).astype(BF16)
                dz_ref[rs, col(4)] = (dcv * c_gate).astype(BF16)
            return carry

        lax.fori_loop(0, ts // rows, chunk, 0)
        dac_buf[ts:ts + HALO_A, :] = dac_buf[0:HALO_A, :]
        dbc_buf[ts:ts + HALO_S, :] = dbc_buf[0:HALO_S, :]

        @pl.when(i == n - 1)
        def _():
            _finish_tap_sums(dca_acc, dca_ref, K_A)
            _finish_tap_sums(dcb_acc, dcb_ref, K_S)

    rev = lambda w: pl.BlockSpec((ts, w), lambda i: (n - 1 - i, 0))
    full = lambda r, c: pl.BlockSpec((r, c), lambda i: (0, 0))
    return _call(
        body, name="mix0_bwd_a", grid=(n,), comm=comm, args=(dx1, z, ac, bconv, w_out, conv_a, ln_g, ln_b, conv_b),
        in_specs=[rev(D), rev(NZ), rev(A), rev(A), _const((2 * A, D)), _const((K_A, A)), _const((1, A)), _const((1, A)),
                  _const((K_S, A))],
        out_specs=[rev(NZ), full(K_A, A), full(1, A), full(1, A), full(K_S, A)],
        out_shape=[jax.ShapeDtypeStruct((s, NZ), BF16), jax.ShapeDtypeStruct((K_A, A), F32), jax.ShapeDtypeStruct((1, A), F32),
                   jax.ShapeDtypeStruct((1, A), F32), jax.ShapeDtypeStruct((K_S, A), F32)],
        scratch=[pltpu.VMEM((ts + HALO_A, A), F32), pltpu.VMEM((ts + HALO_S, A), F32), pltpu.VMEM((SUBLANES * K_A, A), F32),
                 pltpu.VMEM((SUBLANES * K_S, A), F32)])


def _cast_later_weights(w_up, w_down, w_pool, comm):
    nl = w_up.shape[0]

    def body(up_ref, dn_ref, pool_ref, *outs):
        up_o, dn_o, pool_o = outs[:nl], outs[nl:2 * nl], outs[2 * nl]
        for layer in range(nl):
            @pl.when(pl.program_id(0) == layer)
            def _(layer=layer):
                up_o[layer][...] = up_ref[...].astype(BF16)
                dn_o[layer][...] = dn_ref[...].astype(BF16)
                if layer == 0:
                    pool_o[...] = pool_ref[...].astype(BF16)

    per_layer = lambda a: pl.BlockSpec((None,) + a.shape[1:], lambda l: (l,) + (0,) * (a.ndim - 1))
    whole = lambda shape: pl.BlockSpec(shape, lambda l: (0,) * len(shape))
    out_shapes = [w_up.shape[1:]] * nl + [w_down.shape[1:]] * nl + [w_pool.shape]
    return _call(body, name="cast_later_weights", grid=(nl,), comm=comm, args=(w_up, w_down, w_pool),
                 in_specs=[per_layer(w_up), per_layer(w_down), whole(w_pool.shape)],
                 out_specs=[whole(shape) for shape in out_shapes],
                 out_shape=[jax.ShapeDtypeStruct(shape, BF16) for shape in out_shapes])


def _train_step(x, target, place, shard, later, rep, small_pack, small_shapes, adamw):
    bw_up, bw_in = FF2 // NCHIP, NZ // NCHIP
    five = lambda a, k: a.reshape(NCHIP, 2, *HALF[k])

    half = lambda a, name: a.reshape((2,) + HALF[name])
    (up0, up1, dn0, dn1, pool_bf), (g_in, g_out, small_g) = _cast_later_weights(
        *later, comm=_gather_comm([shard["w_in"], shard["w_out"]], small_pack))
    shard = dict(shard, w_pool=half(pool_bf, "w_pool"), w_up0=half(up0, "w_up"), w_up1=half(up1, "w_up"),
                 w_down0=half(dn0, "w_down"), w_down1=half(dn1, "w_down"))
    whole = {}
    for k, part in zip(SMALL_SHARDED, _unpack(small_g, small_shapes, lead=(NCHIP,))):
        whole[k] = jnp.moveaxis(part, 0, 1).reshape(part.shape[1], NCHIP * part.shape[2])
    w_in, w_out = g_in.reshape(NCHIP, D, bw_in), g_out.reshape(2 * A, D)
    conv_ffn = whole["conv_ffn_w"].reshape(2, K_S, FF2)
    nffn = [rep["norm_ffn"][0:1], rep["norm_ffn"][1:2]]

    (h0, z, ac, bconv, cat, x1), (g_up0, g_dn0) = _mix0_fwd(
        x, rep["norm_mix_even"], w_in, whole["conv_a"], rep["ln_a_g"], rep["ln_a_b"], whole["conv_b"], w_out,
        comm=_gather_comm([shard["w_up0"], shard["w_down0"]], mid_steps=GATHER_MID_STEPS["mix0_fwd"]))
    w_up0, w_dn0 = g_up0.reshape(NCHIP, D, bw_up), g_dn0.reshape(FF, D)
    (hf0, u00, u0, act0, x2), (g_pool, g_up1, g_dn1) = _ffn_fwd(
        x1, nffn[0], w_up0, conv_ffn[0], w_dn0, "ffn0_fwd",
        comm=_gather_comm([shard["w_pool"], shard["w_up1"], shard["w_down1"]], mid_steps=GATHER_MID_STEPS["ffn0_fwd"]))
    w_pool = g_pool.reshape(NCHIP, len(POOL_WINDOWS), PG // NCHIP, PG)
    w_up1, w_dn1 = g_up1.reshape(NCHIP, D, bw_up), g_dn1.reshape(FF, D)
    x3 = _pool_fwd(x2, whole["norm_mix_odd"], w_pool, whole["pool_scale"])
    (hf1, u01, u1, act1, dx4, g_nfin, loss_part), _ = _ffn_fwd(x3, nffn[1], w_up1, conv_ffn[1], w_dn1, "ffn1_fwd",
                                                                head=(rep["norm_final"], target))

    psum = lambda k, g, ld, tag="": _pair_sum(place, g, ld, "pair_sum_" + k + tag)
    tot = lambda k, g, ld, p, layer=0, nl=1, prev=None: _chip_sum(place, g, ld, p, layer, nl, "chip_sum_%s%d" % (k, layer), prev)
    pair, chips, join = _pair_comm, _chips_comm, _join_comm

    gr_dn1 = five(_wgrad(act1, dx4, 1, "wgrad_down1")[0], "w_down")
    (du01, g_wc1), _ = _ffn_bwd_a(dx4, u1, u01, conv_ffn[1], w_dn1, "ffn1_bwd_a")
    g_up, (ld_dn1,) = _wgrad(hf1, du01, NCHIP, "wgrad_up1", comm=pair([gr_dn1]))
    gr_up1 = five(g_up, "w_up")
    s_dn1 = psum("w_down", gr_dn1, ld_dn1, "1")
    (dx3, g_nf1), (p_dn1, ld_up1) = _nt_rms_bwd(du01, w_up1, x3, nffn[1], dx4, "ffn1_bwd_b",
                                                comm=join(chips([s_dn1]), pair([gr_up1])))
    s_up1 = psum("w_up", gr_up1, ld_up1, "1")
    (dx2, g_wpool, g_scale, g_nmo), _ = _pool_bwd(dx3, x2, whole["norm_mix_odd"], w_pool, whole["pool_scale"])
    gr_pool = five(g_wpool, "w_pool")
    (du00, g_wc0), (p_up1,) = _ffn_bwd_a(dx2, u0, u00, conv_ffn[0], w_dn0, "ffn0_bwd_a", comm=chips([s_up1]))
    gr_dn0 = five(_wgrad(act0, dx2, 1, "wgrad_down0")[0], "w_down")
    g_up, (ld_dn0, ld_pool) = _wgrad(hf0, du00, NCHIP, "wgrad_up0", comm=pair([gr_dn0, gr_pool]))
    gr_up0 = five(g_up, "w_up")
    s_dn0, s_pool = psum("w_down", gr_dn0, ld_dn0, "0"), psum("w_pool", gr_pool, ld_pool)
    (dx1, g_nf0), (p_dn0, p_pool, ld_up0) = _nt_rms_bwd(du00, w_up0, x1, nffn[0], dx2, "ffn0_bwd_b",
                                                        comm=join(chips([s_dn0, s_pool]), pair([gr_up0])))
    s_up0 = psum("w_up", gr_up0, ld_up0, "0")
    gr_out = five(_wgrad(cat, dx1, 1, "wgrad_out")[0], "w_out")
    (dz, g_ca, g_lg, g_lb, g_cb), (p_up0, ld_out) = _mix0_bwd_a(
        dx1, z, ac, bconv, w_out, whole["conv_a"], rep["ln_a_g"], rep["ln_a_b"], whole["conv_b"],
        comm=join(chips([s_up0]), pair([gr_out])))
    s_out = psum("w_out", gr_out, ld_out)
    t_pool = tot("w_pool", gr_pool, ld_pool, p_pool)
    t_up = tot("w_up", gr_up0, ld_up0, p_up0, 0, 2, tot("w_up", gr_up1, ld_up1, p_up1, 1, 2))
    t_dn = tot("w_down", gr_dn0, ld_dn0, p_dn0, 0, 2, tot("w_down", gr_dn1, ld_dn1, p_dn1, 1, 2))
    small = {"norm_mix_odd": g_nmo, "pool_scale": g_scale, "norm_ffn": jnp.concatenate([g_nf0, g_nf1], axis=0),
             "conv_ffn_w": jnp.stack([g_wc0, g_wc1]), "norm_final": g_nfin, "loss": loss_part,
             "conv_a": g_ca, "ln_a_g": g_lg, "ln_a_b": g_lb, "conv_b": g_cb}
    g_in, (p_out, early_all, t_pool, t_up, t_dn) = _wgrad(
        h0, dz, NCHIP, "wgrad_in",
        comm=join(chips([s_out], _pack([small[k] for k in SMALL_EARLY])), _swap_comm([t_pool, t_up, t_dn])))
    gr_in = five(g_in, "w_in")
    n_mm = x.shape[0] // min(TS_MM, x.shape[0])
    (dx_a, g_nme_a), (ld_in,) = _nt_rms_bwd(dz, w_in, x, rep["norm_mix_even"], dx1, "mix0_bwd_b0", comm=pair([gr_in]),
                                            tiles=(0, n_mm // 2))
    s_in = psum("w_in", gr_in, ld_in)
    (grad_x, g_nme_b), (p_in,) = _nt_rms_bwd(dz, w_in, x, rep["norm_mix_even"], dx1, "mix0_bwd_b1", comm=chips([s_in]),
                                             tiles=(n_mm // 2, n_mm - n_mm // 2), dx_so_far=dx_a)
    small["norm_mix_even"] = g_nme_a + g_nme_b
    (late_all,) = _run_comm(chips([], _pack([small[k] for k in SMALL_LATE])), "rs_last")
    t_in, t_out = _run_comm(_swap_comm([tot("w_in", gr_in, ld_in, p_in), tot("w_out", gr_out, ld_out, p_out)]), "rs_swap")
    done = {k: adamw(k, t) for k, t in (("w_up", t_up), ("w_down", t_dn), ("w_pool", t_pool), ("w_in", t_in), ("w_out", t_out))}
    summed = dict(zip(SMALL_EARLY, _unpack(_sum_devices(early_all, "sum_small_early"), [small[k].shape for k in SMALL_EARLY])))
    summed.update(zip(SMALL_LATE, _unpack(_sum_devices(late_all, "sum_small_late"), [small[k].shape for k in SMALL_LATE])))
    return grad_x, done, summed


def _adamw_math(w, g, m, v):
    m = ADAM_B1 * m + (1.0 - ADAM_B1) * g
    v = ADAM_B2 * v + (1.0 - ADAM_B2) * (g * g)
    m_hat = m / (1.0 - ADAM_B1 ** ADAM_STEP)
    v_hat = v / (1.0 - ADAM_B2 ** ADAM_STEP)
    return -ADAM_LR * (m_hat / (jnp.sqrt(v_hat) + ADAM_EPS) + ADAM_WD * w), m, v


def _adamw_big(w, g, m, v, tr, name, comm=None):
    nl, rows, cols = w.shape

    def body(w_ref, g_ref, m_ref, v_ref, g2_ref, d_ref, m2_ref, v2_ref):
        gv = g_ref[...]
        g2_ref[...] = gv
        d_ref[...], m2_ref[...], v2_ref[...] = _adamw_math(w_ref[...], gv, m_ref[...], v_ref[...])

    spec = pl.BlockSpec((None, tr, cols), lambda l, r: (l, r, 0))
    return _call(body, name=name, grid=(nl, rows // tr), comm=comm, args=(w, g, m, v), in_specs=[spec] * 4,
                 out_specs=[spec] * 4, out_shape=[jax.ShapeDtypeStruct(w.shape, F32)] * 4)


def _adamw_small(ws, gs, ms, vs):
    n = len(ws)

    def body(*refs):
        for p in range(n):
            w_ref, g_ref, m_ref, v_ref = (refs[q * n + p] for q in range(4))
            d_ref, m2_ref, v2_ref = (refs[(4 + q) * n + p] for q in range(3))
            d_ref[...], m2_ref[...], v2_ref[...] = _adamw_math(w_ref[...], g_ref[...], m_ref[...], v_ref[...])

    whole = lambda a: pl.BlockSpec(a.shape, lambda: (0,) * a.ndim)
    outs = pl.pallas_call(
        body, name="adamw_small", in_specs=[whole(a) for a in ws] * 4, out_specs=[whole(a) for a in ws] * 3,
        out_shape=[jax.ShapeDtypeStruct(a.shape, F32) for a in ws] * 3,
        compiler_params=pltpu.CompilerParams(vmem_limit_bytes=VMEM_LIMIT_BYTES),
    )(*ws, *gs, *ms, *vs)
    return outs[0:n], outs[n:2 * n], outs[2 * n:3 * n]


def _place():
    x, y, c = lax.axis_index("x"), lax.axis_index("y"), lax.axis_index("c")
    chips = [(x, 1 - y), (1 - x, y), (1 - x, 1 - y)]
    blocks = [2 * cx + cy for cx, cy in chips]
    return x, y, c, 2 * x + y, chips, blocks


def _gather_comm(shards, small=None, mid_steps=None):
    na = len(shards)
    ns = 0 if small is None else 1

    def copies(ins, outs, sems):
        h1_send, h1_recv, h2_send, h2_recv, f1_send, f1_recv, f2_send, f2_recv, own_send, own_recv = sems[:10]
        x, y, c, j, chips, blocks = _place()
        sib = (x, y, 1 - c)
        piece = lambda a, p: pl.ds(p * (shards[a].shape[1] // 2), shards[a].shape[1] // 2)

        def remote(src, dst, send, recv, q, to):
            return pltpu.make_async_remote_copy(src_ref=src, dst_ref=dst, send_sem=send.at[q], recv_sem=recv.at[q],
                                                device_id=to, device_id_type=MESH)

        def hop1(a, k, arrival):
            dst = outs[a].at[blocks[k], c] if arrival else outs[a].at[j, c]
            return remote(dst if arrival else ins[a].at[c], dst, h1_send, h1_recv, 2 * a + k, (*chips[k], c))

        def hop2(a, p, arrival):
            ref = outs[a].at[blocks[2] if arrival else blocks[p], c, piece(a, p)]
            return remote(ref, ref, h2_send, h2_recv, 2 * a + p, (*chips[1 - p], c))

        def fwd1(a, k, half):
            ref = outs[a].at[blocks[k], half]
            return remote(ref, ref, f1_send, f1_recv, 2 * a + k, sib)

        def fwd2(a, p, half):
            ref = outs[a].at[blocks[2], half, piece(a, p)]
            return remote(ref, ref, f2_send, f2_recv, 2 * a + p, sib)

        own = [remote(ins[a], outs[a].at[j], own_send, own_recv, a, sib) for a in range(na)]
        small_copies = [remote(ins[na], outs[na].at[j], sems[10], sems[11], k, (*chips[k], c)) for k in range(3 * ns)]
        local = [pltpu.make_async_copy(ins[na], outs[na].at[j], sems[12])] if ns else []
        return hop1, hop2, fwd1, fwd2, own, small_copies, local, c

    pairs = [(a, k) for a in range(na) for k in range(2)]

    def start(ins, outs, sems):
        hop1, _, _, _, own, small_copies, local, _ = copies(ins, outs, sems)
        for cp in local + own + [hop1(a, k, False) for a, k in pairs] + small_copies:
            cp.start()

    def mid(ins, outs, sems):
        hop1, hop2, fwd1, _, _, _, _, c = copies(ins, outs, sems)
        for a, k in pairs:
            hop1(a, k, True).wait_recv()
            hop2(a, k, False).start()
            fwd1(a, k, c).start()

    def finish(ins, outs, sems):
        hop1, hop2, fwd1, fwd2, own, small_copies, local, c = copies(ins, outs, sems)
        for a, p in pairs:
            hop2(a, p, True).wait_recv()
            fwd2(a, p, c).start()
        for cp in small_copies:
            cp.wait()
        for a, k in pairs:
            hop1(a, k, False).wait_send()
            hop2(a, k, False).wait_send()
            fwd1(a, k, c).wait_send()
            fwd1(a, k, 1 - c).wait_recv()
            fwd2(a, k, c).wait_send()
            fwd2(a, k, 1 - c).wait_recv()
        for cp in own + local:
            cp.wait()

    out_shapes = [jax.ShapeDtypeStruct((NCHIP,) + s.shape, s.dtype) for s in shards]
    sems = [pltpu.SemaphoreType.DMA((2 * na,))] * 8 + [pltpu.SemaphoreType.DMA((na,))] * 2
    if ns:
        out_shapes.append(jax.ShapeDtypeStruct((NCHIP,) + small.shape, small.dtype))
        sems += [pltpu.SemaphoreType.DMA((3,)), pltpu.SemaphoreType.DMA((3,)), pltpu.SemaphoreType.DMA]
    if mid_steps is not None:
        return _Comm(list(shards) + [small] * ns, out_shapes, sems, start, finish, mid=mid, mid_steps=mid_steps)

    def forward_and_finish(ins, outs, sems):
        mid(ins, outs, sems)
        finish(ins, outs, sems)

    return _Comm(list(shards) + [small] * ns, out_shapes, sems, start, forward_and_finish)


def _simple_comm(inputs, out_shapes, make_copies, n_sems, aliases=None):
    def start(ins, outs, sems):
        for cp in make_copies(ins, outs, sems):
            cp.start()

    def finish(ins, outs, sems):
        for cp in make_copies(ins, outs, sems):
            cp.wait()

    return _Comm(inputs, out_shapes, [pltpu.SemaphoreType.DMA((n,)) for n in n_sems], start, finish, aliases)


def _pair_comm(grads):
    def make_copies(ins, outs, sems):
        x, y, c, _, _, _ = _place()
        return [pltpu.make_async_remote_copy(
            src_ref=ins[a].at[:, 1 - c], dst_ref=outs[a], send_sem=sems[0].at[a], recv_sem=sems[1].at[a],
            device_id=(x, y, 1 - c), device_id_type=MESH) for a in range(len(grads))]

    out_shapes = [jax.ShapeDtypeStruct(g.shape[:1] + g.shape[2:], F32) for g in grads]
    return _simple_comm(grads, out_shapes, make_copies, [len(grads)] * 2)


def _chips_comm(sums, small=None):
    na = len(sums)
    nk = NCHIP - 1

    def make_copies(ins, outs, sems):
        x, y, c, _, chips, _ = _place()
        copies = [pltpu.make_async_remote_copy(
            src_ref=ins[a].at[k], dst_ref=outs[a].at[k], send_sem=sems[0].at[a * nk + k],
            recv_sem=sems[1].at[a * nk + k], device_id=(*chips[k], c), device_id_type=MESH)
            for a in range(na) for k in range(nk)]
        if small is not None:
            me = 4 * x + 2 * y + c
            for r in range(1, NDEV):
                peer = (1 - x if r & 4 else x, 1 - y if r & 2 else y, 1 - c if r & 1 else c)
                copies.append(pltpu.make_async_remote_copy(
                    src_ref=ins[na], dst_ref=outs[na].at[me], send_sem=sems[2].at[r - 1], recv_sem=sems[3].at[r - 1],
                    device_id=peer, device_id_type=MESH))
            copies.append(pltpu.make_async_copy(ins[na], outs[na].at[me], sems[4].at[0]))
        return copies

    out_shapes = [jax.ShapeDtypeStruct(g.shape, BF16) for g in sums]
    chip_sems = [max(na * nk, 1)] * 2
    if small is None:
        return _simple_comm(sums, out_shapes, make_copies, chip_sems)
    out_shapes.append(jax.ShapeDtypeStruct((NDEV,) + small.shape, F32))
    return _simple_comm(list(sums) + [small], out_shapes, make_copies, chip_sems + [NDEV - 1] * 2 + [1])


def _swap_comm(totals):
    def make_copies(ins, outs, sems):
        x, y, c, _, _, _ = _place()
        return [pltpu.make_async_remote_copy(
            src_ref=outs[a].at[:, c], dst_ref=outs[a].at[:, c], send_sem=sems[0].at[a], recv_sem=sems[1].at[a],
            device_id=(x, y, 1 - c), device_id_type=MESH) for a in range(len(totals))]

    out_shapes = [jax.ShapeDtypeStruct(t.shape, F32) for t in totals]
    return _simple_comm(totals, out_shapes, make_copies, [len(totals)] * 2, aliases={a: a for a in range(len(totals))})


def _pair_sum(place, grad, landed, name):
    _, _, rows, cols = grad.shape

    def body(place_ref, g_ref, l_ref, o_ref):
        o_ref[...] = (g_ref[...] + l_ref[...]).astype(BF16)

    other = lambda k, p: jnp.bitwise_xor(p[0], k + 1)
    return pl.pallas_call(
        body, name=name,
        grid_spec=pltpu.PrefetchScalarGridSpec(
            num_scalar_prefetch=1, grid=(NCHIP - 1,),
            in_specs=[pl.BlockSpec((None, None, rows, cols), lambda k, p: (other(k, p), p[1], 0, 0)),
                      pl.BlockSpec((None, rows, cols), lambda k, p: (other(k, p), 0, 0))],
            out_specs=pl.BlockSpec((None, rows, cols), lambda k, p: (k, 0, 0))),
        out_shape=jax.ShapeDtypeStruct((NCHIP - 1, rows, cols), BF16), compiler_params=_cparams(1),
    )(place, grad, landed)


def _chip_sum(place, grad, landed, parts, layer, n_layers, name, prev=None):
    _, _, rows, cols = grad.shape

    def body(*refs):
        g_ref, l_ref, p_ref, o_ref = refs[1], refs[2], refs[3], refs[-1]
        tot = g_ref[...] + l_ref[...]
        for k in range(NCHIP - 1):
            tot = tot + p_ref[k].astype(F32)
        o_ref[...] = tot

    in_specs = [pl.BlockSpec((None, None, rows, cols), lambda i, p: (p[0], p[1], 0, 0)),
                pl.BlockSpec((None, rows, cols), lambda i, p: (p[0], 0, 0)),
                pl.BlockSpec((NCHIP - 1, rows, cols), lambda i, p: (0, 0, 0))]
    args = [place, grad, landed, parts]
    if prev is not None:
        in_specs.append(ANY)
        args.append(prev)
    return pl.pallas_call(
        body, name=name,
        grid_spec=pltpu.PrefetchScalarGridSpec(
            num_scalar_prefetch=1, grid=(1,), in_specs=in_specs,
            out_specs=pl.BlockSpec((None, None, rows, cols), lambda i, p: (layer, p[1], 0, 0))),
        out_shape=jax.ShapeDtypeStruct((n_layers, 2, rows, cols), F32),
        input_output_aliases={} if prev is None else {4: 0}, compiler_params=_cparams(1),
    )(*args)


def _sum_devices(parts, name):
    def body(p_ref, o_ref):
        tot = p_ref[0]
        for d in range(1, NDEV):
            tot = tot + p_ref[d]
        o_ref[...] = tot

    return pl.pallas_call(
        body, name=name, in_specs=[pl.BlockSpec(parts.shape, lambda: (0, 0, 0))],
        out_specs=pl.BlockSpec(parts.shape[1:], lambda: (0, 0)), out_shape=jax.ShapeDtypeStruct(parts.shape[1:], F32),
    )(parts)


def _pack(parts):
    rows = []
    for p in parts:
        p = p.reshape(-1, LANES)
        rows.append(jnp.pad(p, ((0, -p.shape[0] % SUBLANES), (0, 0))))
    return jnp.concatenate(rows, axis=0)


def _unpack(buf, shapes, lead=()):
    out, r0 = [], 0
    nl = len(lead)
    for shp in shapes:
        nrow = 1
        for d in shp:
            nrow *= d
        nrow //= LANES
        out.append(buf[(slice(None),) * nl + (slice(r0, r0 + nrow),)].reshape(lead + tuple(shp)))
        r0 += nrow + (-nrow % SUBLANES)
    return out


WEIGHT_ORDER = ("norm_mix_even", "w_in", "conv_a", "ln_a_g", "ln_a_b", "conv_b", "w_out", "norm_mix_odd", "w_pool",
                "pool_scale", "norm_ffn", "w_up", "conv_ffn_w", "w_down", "norm_final")
BIG = ("w_in", "w_out", "w_pool", "w_up", "w_down")
HALF = {"w_in": (D // 2, NZ // NCHIP), "w_out": (2 * A // NCHIP // 2, D), "w_pool": (PG // 2, PG),
        "w_up": (D // 2, FF2 // NCHIP), "w_down": (FF // NCHIP // 2, D)}
SMALL_SHARDED = ("conv_a", "conv_b", "conv_ffn_w", "norm_mix_odd", "pool_scale")
SMALL_ALL = ("norm_mix_even", "conv_a", "ln_a_g", "ln_a_b", "conv_b", "norm_mix_odd", "pool_scale", "norm_ffn", "conv_ffn_w",
             "norm_final")
SMALL_EARLY = ("norm_mix_odd", "pool_scale", "norm_ffn", "conv_ffn_w", "norm_final", "loss", "conv_a", "ln_a_g", "ln_a_b", "conv_b")
SMALL_LATE = ("norm_mix_even",)


def kernel(x, norm_mix_even, w_in, conv_a, ln_a_g, ln_a_b, conv_b, w_out, norm_mix_odd, w_pool, pool_scale, norm_ffn, w_up, conv_ffn_w, w_down, norm_final, loss_target, m_norm_mix_even, m_w_in, m_conv_a, m_ln_a_g, m_ln_a_b, m_conv_b, m_w_out, m_norm_mix_odd, m_w_pool, m_pool_scale, m_norm_ffn, m_w_up, m_conv_ffn_w, m_w_down, m_norm_final, v_norm_mix_even, v_w_in, v_conv_a, v_ln_a_g, v_ln_a_b, v_conv_b, v_w_out, v_norm_mix_odd, v_w_pool, v_pool_scale, v_norm_ffn, v_w_up, v_conv_ffn_w, v_w_down, v_norm_final):
    w = dict(norm_mix_even=norm_mix_even, w_in=w_in, conv_a=conv_a, ln_a_g=ln_a_g, ln_a_b=ln_a_b, conv_b=conv_b, w_out=w_out,
             norm_mix_odd=norm_mix_odd, w_pool=w_pool, pool_scale=pool_scale, norm_ffn=norm_ffn, w_up=w_up,
             conv_ffn_w=conv_ffn_w, w_down=w_down, norm_final=norm_final)
    m = dict(norm_mix_even=m_norm_mix_even, w_in=m_w_in, conv_a=m_conv_a, ln_a_g=m_ln_a_g, ln_a_b=m_ln_a_b, conv_b=m_conv_b,
             w_out=m_w_out, norm_mix_odd=m_norm_mix_odd, w_pool=m_w_pool, pool_scale=m_pool_scale, norm_ffn=m_norm_ffn,
             w_up=m_w_up, conv_ffn_w=m_conv_ffn_w, w_down=m_w_down, norm_final=m_norm_final)
    v = dict(norm_mix_even=v_norm_mix_even, w_in=v_w_in, conv_a=v_conv_a, ln_a_g=v_ln_a_g, ln_a_b=v_ln_a_b, conv_b=v_conv_b,
             w_out=v_w_out, norm_mix_odd=v_norm_mix_odd, w_pool=v_w_pool, pool_scale=v_pool_scale, norm_ffn=v_norm_ffn,
             w_up=v_w_up, conv_ffn_w=v_conv_ffn_w, w_down=v_w_down, norm_final=v_norm_final)
    chip = 2 * lax.axis_index("x") + lax.axis_index("y")
    place = jnp.stack([chip, lax.axis_index("c")]).astype(jnp.int32)

    half = lambda a, name: a.astype(BF16).reshape((2,) + HALF[name])
    shard = {"w_in": half(w_in[0], "w_in"), "w_out": half(w_out[0], "w_out")}
    small_shapes = [w[k].shape[-2:] if w[k].ndim == 3 and k != "conv_ffn_w" else (w[k].size // w[k].shape[-1], w[k].shape[-1])
                    for k in SMALL_SHARDED]
    rep = dict(norm_mix_even=norm_mix_even, ln_a_g=ln_a_g, ln_a_b=ln_a_b, norm_ffn=norm_ffn, norm_final=norm_final.reshape(1, D))
    as3 = lambda a: a.reshape(a.shape[0], -1, a.shape[-1])

    def adamw(k, total):
        outs, _ = _adamw_big(as3(w[k]), as3(total), as3(m[k]), as3(v[k]), ADAMW_ROWS[k], "adamw_" + k)
        return [a.reshape(w[k].shape) for a in outs]

    grad_x, done, summed = _train_step(
        x[0], loss_target[0], place, shard, (w_up, w_down, w_pool[0]), rep, _pack([w[k] for k in SMALL_SHARDED]), small_shapes,
        adamw)
    loss = summed["loss"][0, 0]

    grad, delta, new_m, new_v = ({k: done[k][q] for k in BIG} for q in range(4))
    for k in SMALL_ALL:
        gsum = summed[k]
        if k in SMALL_SHARDED:
            cols = w[k].shape[-1]
            gsum = lax.dynamic_slice_in_dim(gsum, chip * cols, cols, axis=gsum.ndim - 1)
        grad[k] = gsum.reshape(w[k].shape)

    as2 = lambda a: a.reshape(1, -1) if a.ndim == 1 else a[0] if a.ndim == 3 and a.shape[0] == 1 else a
    ds, ms, vs = _adamw_small(*[[as2(t[k]) for k in SMALL_ALL] for t in (w, grad, m, v)])
    for k, d2, m2, v2 in zip(SMALL_ALL, ds, ms, vs):
        delta[k], new_m[k], new_v[k] = (a.reshape(w[k].shape) for a in (d2, m2, v2))

    return (loss, grad_x[None], *[grad[k] for k in WEIGHT_ORDER], *[delta[k] for k in WEIGHT_ORDER],
            *[new_m[k] for k in WEIGHT_ORDER], *[new_v[k] for k in WEIGHT_ORDER])
```

```python
import jax
import jax.numpy as jnp
from jax import lax
from jax.experimental import pallas as pl
from jax.experimental.pallas import tpu as pltpu

F32, BF16 = jnp.float32, jnp.bfloat16

D = 1024
A = 512
NZ = 5 * A
FF = 2816
FF2 = 2 * FF
NCHIP = 4
NDEV = 8
K_A, K_S = 31, 3
POOL_WINDOWS = (2, 4, 8, 16)
PG = D // len(POOL_WINDOWS)
RMS_EPS, LN_EPS = 1e-6, 1e-5
ADAM_LR, ADAM_B1, ADAM_B2, ADAM_EPS, ADAM_WD, ADAM_STEP = 0.001, 0.9, 0.999, 1e-08, 0.01, 10

HALO_A, HALO_S, HALO_P = 32, 8, 16
SUBLANES = 8
LANES = 128
VMEM_LIMIT_BYTES = 56 * 1024 * 1024

TS_MIX = 512
TS_MIXB = 512
TS_FFN = 256
TS_POOL = 512
TS_MM = 512
TS_WGRAD = 1024
ADAMW_ROWS = {"w_in": 512, "w_out": 256, "w_pool": 256, "w_up": 256, "w_down": 352}
R_CHUNK = 64
GATHER_MID_STEPS = {"mix0_fwd": 2, "ffn0_fwd": 5}

MESH = pl.DeviceIdType.MESH
ANY = pl.BlockSpec(memory_space=pl.ANY)
NT_DIMS = (((1,), (1,)), ((), ()))
TN_DIMS = (((0,), (0,)), ((), ()))


def _cparams(n_axes):
    return pltpu.CompilerParams(dimension_semantics=("arbitrary",) * n_axes, vmem_limit_bytes=VMEM_LIMIT_BYTES)


def _const(shape):
    nd = len(shape)
    return pl.BlockSpec(shape, lambda *_: (0,) * nd, pipeline_mode=pl.Buffered(1))


def _sigmoid(v):
    return 1.0 / (1.0 + jnp.exp(-v))


def _rsqrt_mean_sq(x):
    return lax.rsqrt(jnp.mean(x * x, axis=-1, keepdims=True) + RMS_EPS)


def _rms_bwd(dh, xh, r, g):
    dxh = dh * g
    return r * (dxh - xh * jnp.mean(dxh * xh, axis=-1, keepdims=True))


def _shifted(buf_ref, row0, rows, col0, width, offsets):
    lo = (min(offsets) // SUBLANES) * SUBLANES
    hi = -(-(max(offsets) + rows) // SUBLANES) * SUBLANES
    start = row0 + lo if isinstance(row0, int) else pl.multiple_of(row0 + lo, SUBLANES)
    win = buf_ref[pl.ds(start, hi - lo), col0:col0 + width]
    out = {}
    for res in sorted({(o - lo) % SUBLANES for o in offsets}):
        qs = {o: (o - lo) // SUBLANES for o in offsets if (o - lo) % SUBLANES == res}
        base = pltpu.roll(win, hi - lo - res, 0) if res else win
        for o, q in qs.items():
            out[o] = base[SUBLANES * q:SUBLANES * q + rows, :]
    return out


def _rowsum8(v):
    acc = v[0:SUBLANES, :]
    for r in range(SUBLANES, v.shape[0], SUBLANES):
        acc = acc + v[r:r + SUBLANES, :]
    return acc


def _taps(w_ref, sh, offsets, col0, width):
    acc = None
    for k, o in enumerate(offsets):
        term = w_ref[k:k + 1, col0:col0 + width] * sh[o]
        acc = term if acc is None else acc + term
    return acc


def _window_bases(buf_ref, row0, rows, col0, width, offsets):
    lo = (min(offsets) // SUBLANES) * SUBLANES
    hi = -(-(max(offsets) + rows) // SUBLANES) * SUBLANES
    start = row0 + lo if isinstance(row0, int) else pl.multiple_of(row0 + lo, SUBLANES)
    win = buf_ref[pl.ds(start, hi - lo), col0:col0 + width]
    for res in sorted({(o - lo) % SUBLANES for o in offsets}):
        taps = [(k, (o - lo) // SUBLANES * SUBLANES) for k, o in enumerate(offsets) if (o - lo) % SUBLANES == res]
        yield (pltpu.roll(win, hi - lo - res, 0) if res else win), taps


def _conv_acc(buf_ref, w_ref, row0, rows, col0, width, offsets):
    acc = None
    for base, taps in _window_bases(buf_ref, row0, rows, col0, width, offsets):
        for k, q in taps:
            term = w_ref[k:k + 1, col0:col0 + width] * base[q:q + rows, :]
            acc = term if acc is None else acc + term
    return acc


def _conv_corr(buf_ref, w_ref, other, acc_ref, row0, rows, col0, width, offsets):
    acc = None
    for base, taps in _window_bases(buf_ref, row0, rows, col0, width, offsets):
        for k, q in taps:
            sl = base[q:q + rows, :]
            term = w_ref[k:k + 1, col0:col0 + width] * sl
            acc = term if acc is None else acc + term
            acc_ref[SUBLANES * k:SUBLANES * (k + 1), col0:col0 + width] += _rowsum8(sl * other)
    return acc


def _finish_tap_sums(acc_ref, out_ref, n_taps):
    for k in range(n_taps):
        out_ref[k:k + 1, :] = jnp.sum(acc_ref[SUBLANES * k:SUBLANES * (k + 1), :], axis=0, keepdims=True)


class _Comm:
    def __init__(self, inputs, out_shapes, sems, start, finish, aliases=None, mid=None, mid_steps=1):
        self.inputs, self.out_shapes, self.sems = list(inputs), list(out_shapes), list(sems)
        self.start, self.finish, self.mid, self.aliases = start, finish, mid, dict(aliases or {})
        self.mid_steps = mid_steps


def _join_comm(a, b):
    ni, no, ns = len(a.inputs), len(a.out_shapes), len(a.sems)

    def both(name):
        def run(ins, outs, sems):
            if getattr(a, name) is not None:
                getattr(a, name)(ins[:ni], outs[:no], sems[:ns])
            if getattr(b, name) is not None:
                getattr(b, name)(ins[ni:], outs[no:], sems[ns:])
        return run

    aliases = {**a.aliases, **{ni + i: no + o for i, o in b.aliases.items()}}
    mid = both("mid") if (a.mid is not None or b.mid is not None) else None
    return _Comm(a.inputs + b.inputs, a.out_shapes + b.out_shapes, a.sems + b.sems, both("start"), both("finish"), aliases, mid)


def _call(body, *, name, grid, in_specs, out_specs, out_shape, args, scratch=(), comm=None, aliases=None):
    n_in, n_out, n_scr, n_axes = len(in_specs), len(out_specs), len(scratch), len(grid)
    params = pltpu.CompilerParams(dimension_semantics=("arbitrary",) * n_axes, vmem_limit_bytes=VMEM_LIMIT_BYTES)
    aliases = dict(aliases or {})
    if comm is None:
        outs = pl.pallas_call(body, name=name, grid=grid, in_specs=list(in_specs), out_specs=list(out_specs),
                              out_shape=list(out_shape), scratch_shapes=list(scratch), input_output_aliases=aliases,
                              compiler_params=params)(*args)
        return list(outs), []
    ci, co = len(comm.inputs), len(comm.out_shapes)

    def wrapped(*refs):
        k_in, c_in = refs[:n_in], refs[n_in:n_in + ci]
        o0 = n_in + ci
        k_out, c_out = refs[o0:o0 + n_out], refs[o0 + n_out:o0 + n_out + co]
        s0 = o0 + n_out + co
        k_scr, c_sem = refs[s0:s0 + n_scr], refs[s0 + n_scr:]
        first = pl.program_id(0) == 0
        last = pl.program_id(0) == grid[0] - 1
        for ax in range(1, n_axes):
            first = jnp.logical_and(first, pl.program_id(ax) == 0)
            last = jnp.logical_and(last, pl.program_id(ax) == grid[ax] - 1)

        @pl.when(first)
        def _():
            comm.start(c_in, c_out, c_sem)

        mid_early = comm.mid is not None and n_axes == 1 and grid[0] > comm.mid_steps
        if mid_early:
            @pl.when(pl.program_id(0) == grid[0] - 1 - comm.mid_steps)
            def _():
                comm.mid(c_in, c_out, c_sem)

        body(*k_in, *k_out, *k_scr)

        @pl.when(last)
        def _():
            if comm.mid is not None and not mid_early:
                comm.mid(c_in, c_out, c_sem)
            comm.finish(c_in, c_out, c_sem)

    outs = pl.pallas_call(
        wrapped, name=name, grid=grid, in_specs=list(in_specs) + [ANY] * ci, out_specs=list(out_specs) + [ANY] * co,
        out_shape=list(out_shape) + comm.out_shapes, scratch_shapes=list(scratch) + comm.sems,
        input_output_aliases={**aliases, **{n_in + i: n_out + o for i, o in comm.aliases.items()}}, compiler_params=params,
    )(*args, *comm.inputs)
    return list(outs[:n_out]), list(outs[n_out:])


def _run_comm(comm, name):
    ci, co = len(comm.inputs), len(comm.out_shapes)

    def body(*refs):
        c_in, c_out, c_sem = refs[:ci], refs[ci:ci + co], refs[ci + co:]
        comm.start(c_in, c_out, c_sem)
        if comm.mid is not None:
            comm.mid(c_in, c_out, c_sem)
        comm.finish(c_in, c_out, c_sem)

    return list(pl.pallas_call(body, name=name, in_specs=[ANY] * ci, out_specs=[ANY] * co, out_shape=comm.out_shapes,
                               scratch_shapes=comm.sems, input_output_aliases=comm.aliases)(*comm.inputs))


def _mix0_fwd(x, g, w_in, conv_a, ln_g, ln_b, conv_b, w_out, comm=None):
    s = x.shape[0]
    ts = min(TS_MIX, s)
    n = s // ts
    bw = NZ // NCHIP
    offs_a = [HALO_A - (K_A - 1) + k for k in range(K_A)]
    offs_s = [HALO_S - (K_S - 1) + k for k in range(K_S)]

    def body(x_ref, g_ref, win_ref, ca_ref, lg_ref, lb_ref, cb_ref, wout_ref,
             h_ref, z_ref, ac_ref, bconv_buf, cat_ref, x1_ref, glu_buf, cv_buf):
        i = pl.program_id(0)

        @pl.when(i == 0)
        def _():
            glu_buf[0:HALO_A, :] = jnp.zeros((HALO_A, A), F32)
            cv_buf[0:HALO_S, :] = jnp.zeros((HALO_S, A), F32)

        xv = x_ref[...]
        h = (xv * _rsqrt_mean_sq(xv) * g_ref[...]).astype(BF16)
        h_ref[...] = h
        for j in range(NCHIP):
            z_ref[:, j * bw:(j + 1) * bw] = jnp.dot(h, win_ref[j], preferred_element_type=F32)
        glu_buf[HALO_A:HALO_A + ts, :] = z_ref[:, 0:A] * _sigmoid(z_ref[:, A:2 * A])
        cv_buf[HALO_S:HALO_S + ts, :] = z_ref[:, 3 * A:4 * A] * z_ref[:, 4 * A:5 * A]

        def chunk(ci, carry):
            r0 = pl.multiple_of(ci * R_CHUNK, R_CHUNK)
            for c0 in range(0, A, LANES):
                ac_ref[pl.ds(r0, R_CHUNK), c0:c0 + LANES] = _conv_acc(glu_buf, ca_ref, r0, R_CHUNK, c0, LANES, offs_a)
                bconv_buf[pl.ds(r0, R_CHUNK), c0:c0 + LANES] = _conv_acc(cv_buf, cb_ref, r0, R_CHUNK, c0, LANES, offs_s)
            return carry

        lax.fori_loop(0, ts // R_CHUNK, chunk, 0)
        glu_buf[0:HALO_A, :] = glu_buf[ts:ts + HALO_A, :]
        cv_buf[0:HALO_S, :] = cv_buf[ts:ts + HALO_S, :]

        ac = ac_ref[...]
        xc = ac - jnp.mean(ac, axis=-1, keepdims=True)
        xn = xc * lax.rsqrt(jnp.mean(xc * xc, axis=-1, keepdims=True) + LN_EPS)
        ln = xn * lg_ref[...] + lb_ref[...]
        cat_ref[:, 0:A] = (ln * _sigmoid(ln)).astype(BF16)
        cat_ref[:, A:2 * A] = (z_ref[:, 2 * A:3 * A] * bconv_buf[...]).astype(BF16)
        x1_ref[...] = xv + jnp.dot(cat_ref[...], wout_ref[...], preferred_element_type=F32)

    tile = lambda w: pl.BlockSpec((ts, w), lambda i: (i, 0))
    return _call(
        body, name="mix0_fwd", grid=(n,), comm=comm, args=(x, g, w_in, conv_a, ln_g, ln_b, conv_b, w_out),
        in_specs=[tile(D), _const((1, D)), _const((NCHIP, D, bw)), _const((K_A, A)), _const((1, A)), _const((1, A)),
                  _const((K_S, A)), _const((2 * A, D))],
        out_specs=[tile(D), tile(NZ), tile(A), tile(A), tile(2 * A), tile(D)],
        out_shape=[jax.ShapeDtypeStruct((s, D), BF16), jax.ShapeDtypeStruct((s, NZ), F32), jax.ShapeDtypeStruct((s, A), F32),
                   jax.ShapeDtypeStruct((s, A), F32), jax.ShapeDtypeStruct((s, 2 * A), BF16), jax.ShapeDtypeStruct((s, D), F32)],
        scratch=[pltpu.VMEM((HALO_A + ts, A), F32), pltpu.VMEM((HALO_S + ts, A), F32)])


def _ffn_fwd(x, g, w_up, wc, w_down, name, comm=None, head=None):
    s = x.shape[0]
    ts = min(TS_FFN, s)
    n = s // ts
    bw = FF2 // NCHIP
    rows = 32
    offs = [HALO_S - (K_S - 1) + k for k in range(K_S)]
    n_in = 5 + (2 if head else 0)

    def body(*refs):
        x_ref, g_ref, wup_ref, wc_ref, wdn_ref = refs[:5]
        h_ref, u0_ref, u_ref, act_ref, xo_ref = refs[n_in:n_in + 5]
        cbuf = refs[-1]
        i = pl.program_id(0)

        @pl.when(i == 0)
        def _():
            cbuf[0:HALO_S, :] = jnp.zeros((HALO_S, FF2), F32)
            if head:
                refs[n_in + 5][...] = jnp.zeros((1, D), F32)
                refs[n_in + 6][...] = jnp.zeros((1, LANES), F32)

        xv = x_ref[...]
        h = (xv * _rsqrt_mean_sq(xv) * g_ref[...]).astype(BF16)
        h_ref[...] = h
        f = None
        for p in range(NCHIP // 2):
            for j in (p, NCHIP // 2 + p):
                zc = jnp.dot(h, wup_ref[j], preferred_element_type=F32)
                u0_ref[:, j * bw:(j + 1) * bw] = zc.astype(BF16)
                cbuf[HALO_S:HALO_S + ts, j * bw:(j + 1) * bw] = zc
            for r0 in range(0, ts, rows):
                for c0 in range(p * bw, (p + 1) * bw, LANES):
                    ug = _taps(wc_ref, _shifted(cbuf, r0, rows, c0, LANES, offs), offs, c0, LANES)
                    uv = _taps(wc_ref, _shifted(cbuf, r0, rows, FF + c0, LANES, offs), offs, FF + c0, LANES)
                    u_ref[r0:r0 + rows, c0:c0 + LANES] = ug
                    u_ref[r0:r0 + rows, FF + c0:FF + c0 + LANES] = uv
                    act_ref[r0:r0 + rows, c0:c0 + LANES] = (ug * _sigmoid(ug) * uv).astype(BF16)
            fp = jnp.dot(act_ref[:, p * bw:(p + 1) * bw], wdn_ref[p * bw:(p + 1) * bw, :], preferred_element_type=F32)
            f = fp if f is None else f + fp
        cbuf[0:HALO_S, :] = cbuf[ts:ts + HALO_S, :]
        if not head:
            xo_ref[...] = xv + f
        else:
            gf_ref, t_ref, dgf_ref, loss_ref = refs[5], refs[6], refs[n_in + 5], refs[n_in + 6]
            xo = xv + f
            r = _rsqrt_mean_sq(xo)
            xh = xo * r
            gv = gf_ref[...]
            err = xh * gv - t_ref[...]
            loss_ref[...] += jnp.sum(jnp.sum(err * err, axis=1, keepdims=True), axis=0, keepdims=True) * (0.5 / D)
            dy = err * (1.0 / D)
            dgf_ref[...] += jnp.sum(dy * xh, axis=0, keepdims=True)
            xo_ref[...] = _rms_bwd(dy, xh, r, gv)

    tile = lambda w: pl.BlockSpec((ts, w), lambda i: (i, 0))
    one_row = lambda w: pl.BlockSpec((1, w), lambda i: (0, 0))
    return _call(
        body, name=name, grid=(n,), comm=comm, args=(x, g, w_up, wc, w_down) + (tuple(head) if head else ()),
        in_specs=[tile(D), _const((1, D)), _const((NCHIP, D, bw)), _const((K_S, FF2)), _const((FF, D))]
        + ([_const((1, D)), tile(D)] if head else []),
        out_specs=[tile(D), tile(FF2), tile(FF2), tile(FF), tile(D)] + ([one_row(D), one_row(LANES)] if head else []),
        out_shape=[jax.ShapeDtypeStruct((s, D), BF16), jax.ShapeDtypeStruct((s, FF2), BF16), jax.ShapeDtypeStruct((s, FF2), F32),
                   jax.ShapeDtypeStruct((s, FF), BF16), jax.ShapeDtypeStruct((s, D), F32)]
        + ([jax.ShapeDtypeStruct((1, D), F32), jax.ShapeDtypeStruct((1, LANES), F32)] if head else []),
        scratch=[pltpu.VMEM((HALO_S + ts, FF2), F32)])


def _pool_windows(hbuf, pbuf, tile_row0, ts):
    def chunk(ci, carry):
        r0 = pl.multiple_of(ci * R_CHUNK, R_CHUNK)
        t1 = (tile_row0 + r0 + lax.broadcasted_iota(jnp.int32, (R_CHUNK, 1), 0) + 1).astype(F32)
        for gi, w in enumerate(POOL_WINDOWS):
            cnt = jnp.minimum(t1, float(w))
            offs = [HALO_P - jj for jj in range(w)]
            for c0 in range(gi * PG, (gi + 1) * PG, LANES):
                sh = _shifted(hbuf, r0, R_CHUNK, c0, LANES, offs)
                tot = sh[offs[0]]
                for o in offs[1:]:
                    tot = tot + sh[o]
                pbuf[pl.ds(r0, R_CHUNK), c0:c0 + LANES] = (tot / cnt - sh[HALO_P]).astype(BF16)
        return carry

    lax.fori_loop(0, ts // R_CHUNK, chunk, 0)


def _assemble_wpool(wp_ref, wps):
    rb = PG // NCHIP
    for gi in range(len(POOL_WINDOWS)):
        for j in range(NCHIP):
            wps[gi, j * rb:(j + 1) * rb, :] = wp_ref[j, gi]


def _pool_fwd(x, g, w_pool, scale):
    s = x.shape[0]
    ts = min(TS_POOL, s)
    n = s // ts
    ng = len(POOL_WINDOWS)

    def body(x_ref, g_ref, wp_ref, sc_ref, xo_ref, hbuf, pbuf, wps):
        i = pl.program_id(0)

        @pl.when(i == 0)
        def _():
            hbuf[0:HALO_P, :] = jnp.zeros((HALO_P, D), F32)
            _assemble_wpool(wp_ref, wps)

        xv = x_ref[...]
        hbuf[HALO_P:HALO_P + ts, :] = xv * _rsqrt_mean_sq(xv) * g_ref[...]
        _pool_windows(hbuf, pbuf, i * ts, ts)
        hbuf[0:HALO_P, :] = hbuf[ts:ts + HALO_P, :]
        for gi in range(ng):
            cols = slice(gi * PG, (gi + 1) * PG)
            y = jnp.dot(pbuf[:, cols], wps[gi], preferred_element_type=F32)
            xo_ref[:, cols] = xv[:, cols] + y * sc_ref[:, cols]

    tile = pl.BlockSpec((ts, D), lambda i: (i, 0))
    return pl.pallas_call(
        body, name="pool_fwd", grid=(n,),
        in_specs=[tile, _const((1, D)), _const((NCHIP, ng, PG // NCHIP, PG)), _const((1, D))],
        out_specs=tile, out_shape=jax.ShapeDtypeStruct((s, D), F32),
        scratch_shapes=[pltpu.VMEM((HALO_P + ts, D), F32), pltpu.VMEM((ts, D), BF16), pltpu.VMEM((ng, PG, PG), BF16)],
        compiler_params=_cparams(1),
    )(x, g, w_pool, scale)


def _ffn_bwd_a(dxo, u, u0, wc, w_down, name, comm=None):
    s = dxo.shape[0]
    ts = min(TS_FFN, s)
    n = s // ts
    cw = FF2 // NCHIP
    rows = 32
    lw2 = 2 * LANES
    boffs = [K_S - 1 - k for k in range(K_S)]

    def body(dxo_ref, u_ref, u0_ref, wc_ref, wdn_ref, du0_ref, dwc_ref, dubuf, dact, dwacc):
        i = pl.program_id(0)

        @pl.when(i == 0)
        def _():
            dubuf[ts:ts + HALO_S, :] = jnp.zeros((HALO_S, FF2), F32)
            dwacc[...] = jnp.zeros(dwacc.shape, F32)

        df = dxo_ref[...].astype(BF16)
        for cg in range(0, FF, cw):
            dact[...] = lax.dot_general(df, wdn_ref[cg:cg + cw, :], NT_DIMS, preferred_element_type=F32)

            def chunk(ci, carry, cg=cg):
                rs = pl.ds(pl.multiple_of(ci * rows, rows), rows)
                for c in range(cg, cg + cw, LANES):
                    ug, uv = u_ref[rs, c:c + LANES], u_ref[rs, FF + c:FF + c + LANES]
                    sg = _sigmoid(ug)
                    da = dact[rs, c - cg:c - cg + LANES]
                    gs = ug * sg
                    dubuf[rs, c:c + LANES] = da * uv * (sg + gs * (1.0 - sg))
                    dubuf[rs, FF + c:FF + c + LANES] = da * gs
                return carry

            lax.fori_loop(0, ts // rows, chunk, 0)

        def chunk2(ci, carry):
            r0 = pl.multiple_of(ci * rows, rows)
            rs = pl.ds(r0, rows)
            for c in range(0, FF2, lw2):
                sh = _shifted(dubuf, r0, rows, c, lw2, boffs)
                du0_ref[rs, c:c + lw2] = _taps(wc_ref, sh, boffs, c, lw2).astype(BF16)
                u0v = u0_ref[rs, c:c + lw2].astype(F32)
                for k, o in enumerate(boffs):
                    dwacc[SUBLANES * k:SUBLANES * (k + 1), c:c + lw2] += _rowsum8(sh[o] * u0v)
            return carry

        lax.fori_loop(0, ts // rows, chunk2, 0)
        dubuf[ts:ts + HALO_S, :] = dubuf[0:HALO_S, :]

        @pl.when(i == n - 1)
        def _():
            _finish_tap_sums(dwacc, dwc_ref, K_S)

    rev = lambda w: pl.BlockSpec((ts, w), lambda i: (n - 1 - i, 0))
    return _call(
        body, name=name, grid=(n,), comm=comm, args=(dxo, u, u0, wc, w_down),
        in_specs=[rev(D), rev(FF2), rev(FF2), _const((K_S, FF2)), _const((FF, D))],
        out_specs=[rev(FF2), pl.BlockSpec((K_S, FF2), lambda i: (0, 0))],
        out_shape=[jax.ShapeDtypeStruct((s, FF2), BF16), jax.ShapeDtypeStruct((K_S, FF2), F32)],
        scratch=[pltpu.VMEM((ts + HALO_S, FF2), F32), pltpu.VMEM((ts, cw), F32), pltpu.VMEM((SUBLANES * K_S, FF2), F32)])


def _nt_rms_bwd(dy, w, x, g, dres, name, comm=None, tiles=None, dx_so_far=None):
    s = x.shape[0]
    ts = min(TS_MM, s)
    first, n = (0, s // ts) if tiles is None else tiles
    nw = dy.shape[1]
    bw = nw // NCHIP

    def body(dy_ref, w_ref, x_ref, g_ref, dres_ref, dx_ref, dg_ref):
        i = pl.program_id(0)

        @pl.when(i == 0)
        def _():
            dg_ref[...] = jnp.zeros((1, D), F32)

        dh = lax.dot_general(dy_ref[:, 0:bw], w_ref[0], NT_DIMS, preferred_element_type=F32)
        for j in range(1, NCHIP):
            dh = dh + lax.dot_general(dy_ref[:, j * bw:(j + 1) * bw], w_ref[j], NT_DIMS, preferred_element_type=F32)
        xv = x_ref[...]
        r = _rsqrt_mean_sq(xv)
        xh = xv * r
        dg_ref[...] += jnp.sum(dh * xh, axis=0, keepdims=True)
        dx_ref[...] = dres_ref[...] + _rms_bwd(dh, xh, r, g_ref[...])

    def body_with_alias(dy_ref, w_ref, x_ref, g_ref, dres_ref, _, dx_ref, dg_ref):
        body(dy_ref, w_ref, x_ref, g_ref, dres_ref, dx_ref, dg_ref)

    tile = lambda wd: pl.BlockSpec((ts, wd), lambda i: (first + i, 0))
    in_specs = [tile(nw), _const((NCHIP, D, bw)), tile(D), _const((1, D)), tile(D)]
    more = dx_so_far is not None
    return _call(
        body_with_alias if more else body, name=name, grid=(n,), comm=comm,
        args=(dy, w, x, g, dres) + ((dx_so_far,) if more else ()), in_specs=in_specs + [ANY] * more,
        out_specs=[tile(D), pl.BlockSpec((1, D), lambda i: (0, 0))], aliases={5: 0} if more else None,
        out_shape=[jax.ShapeDtypeStruct((s, D), F32), jax.ShapeDtypeStruct((1, D), F32)])


def _wgrad(a, b, n_blocks, name, comm=None):
    s, m = a.shape
    bw = b.shape[1] // n_blocks
    tk = min(TS_WGRAD, s)
    group = min(g for g in range(1, n_blocks + 1) if n_blocks % g == 0 and (g * bw >= m or g == n_blocks))

    def body(a_ref, b_ref, o_ref):
        part = lambda j: lax.dot_general(a_ref[...], b_ref[:, j * bw:(j + 1) * bw].astype(BF16), TN_DIMS,
                                         preferred_element_type=F32)

        @pl.when(pl.program_id(1) == 0)
        def _():
            for j in range(group):
                o_ref[j] = part(j)

        @pl.when(pl.program_id(1) != 0)
        def _():
            for j in range(group):
                o_ref[j] += part(j)

    (out,), comm_out = _call(
        body, name=name, grid=(n_blocks // group, s // tk), comm=comm, args=(a, b),
        in_specs=[pl.BlockSpec((tk, m), lambda j, k: (k, 0)), pl.BlockSpec((tk, group * bw), lambda j, k: (k, j))],
        out_specs=[pl.BlockSpec((group, m, bw), lambda j, k: (j, 0, 0))],
        out_shape=[jax.ShapeDtypeStruct((n_blocks, m, bw), F32)])
    return out, comm_out


def _pool_bwd(dxo, x, g, w_pool, scale, comm=None):
    s = x.shape[0]
    ts = min(TS_POOL, s)
    n = s // ts
    ng = len(POOL_WINDOWS)
    rb = PG // NCHIP

    def body(dxo_ref, x_ref, halo_ref, g_ref, wp_ref, sc_ref, dx_ref, dwp_ref, dsc_ref, dg_ref,
             hbuf, pbuf, qbuf, dhbuf, wps, dwacc):
        i = pl.program_id(0)
        j = n - 1 - i

        @pl.when(i == 0)
        def _():
            qbuf[ts:ts + HALO_P, :] = jnp.zeros((HALO_P, D), F32)
            dwacc[...] = jnp.zeros(dwacc.shape, F32)
            dsc_ref[...] = jnp.zeros((1, D), F32)
            dg_ref[...] = jnp.zeros((1, D), F32)
            _assemble_wpool(wp_ref, wps)

        gv = g_ref[...]
        xl = halo_ref[...]
        hbuf[0:HALO_P, :] = jnp.where(j == 0, 0.0, xl * _rsqrt_mean_sq(xl) * gv)
        xv = x_ref[...]
        r = _rsqrt_mean_sq(xv)
        xh = xv * r
        hbuf[HALO_P:HALO_P + ts, :] = xh * gv
        _pool_windows(hbuf, pbuf, j * ts, ts)

        dy = dxo_ref[...]
        t1 = (j * ts + lax.broadcasted_iota(jnp.int32, (ts, 1), 0) + 1).astype(F32)
        for gi, w in enumerate(POOL_WINDOWS):
            cols = slice(gi * PG, (gi + 1) * PG)
            p = pbuf[:, cols]
            y = jnp.dot(p, wps[gi], preferred_element_type=F32)
            dsc_ref[:, cols] += jnp.sum(dy[:, cols] * y, axis=0, keepdims=True)
            dq = (dy[:, cols] * sc_ref[:, cols]).astype(BF16)
            dwacc[gi] += lax.dot_general(p, dq, TN_DIMS, preferred_element_type=F32)
            dp = lax.dot_general(dq, wps[gi], NT_DIMS, preferred_element_type=F32)
            qbuf[0:ts, cols] = dp / jnp.minimum(t1, float(w))

        def chunk(ci, carry):
            r0 = pl.multiple_of(ci * R_CHUNK, R_CHUNK)
            tc = (j * ts + r0 + lax.broadcasted_iota(jnp.int32, (R_CHUNK, 1), 0) + 1).astype(F32)
            for gi, w in enumerate(POOL_WINDOWS):
                cnt = jnp.minimum(tc, float(w))
                offs = list(range(w))
                for c0 in range(gi * PG, (gi + 1) * PG, LANES):
                    sh = _shifted(qbuf, r0, R_CHUNK, c0, LANES, offs)
                    tot = sh[0]
                    for o in offs[1:]:
                        tot = tot + sh[o]
                    dhbuf[pl.ds(r0, R_CHUNK), c0:c0 + LANES] = tot - sh[0] * cnt
            return carry

        lax.fori_loop(0, ts // R_CHUNK, chunk, 0)
        qbuf[ts:ts + HALO_P, :] = qbuf[0:HALO_P, :]
        dh = dhbuf[...]
        dg_ref[...] += jnp.sum(dh * xh, axis=0, keepdims=True)
        dx_ref[...] = dy + _rms_bwd(dh, xh, r, gv)

        @pl.when(i == n - 1)
        def _():
            for gi in range(ng):
                for jj in range(NCHIP):
                    dwp_ref[jj, gi] = dwacc[gi, jj * rb:(jj + 1) * rb, :]

    rev = pl.BlockSpec((ts, D), lambda i: (n - 1 - i, 0))
    halo = pl.BlockSpec((HALO_P, D), lambda i: (jnp.maximum((n - 1 - i) * (ts // HALO_P) - 1, 0), 0))
    vec = pl.BlockSpec((1, D), lambda i: (0, 0))
    return _call(
        body, name="pool_bwd", grid=(n,), comm=comm, args=(dxo, x, x, g, w_pool, scale),
        in_specs=[rev, rev, halo, _const((1, D)), _const((NCHIP, ng, rb, PG)), _const((1, D))],
        out_specs=[rev, pl.BlockSpec((NCHIP, ng, rb, PG), lambda i: (0, 0, 0, 0)), vec, vec],
        out_shape=[jax.ShapeDtypeStruct((s, D), F32), jax.ShapeDtypeStruct((NCHIP, ng, rb, PG), F32),
                   jax.ShapeDtypeStruct((1, D), F32), jax.ShapeDtypeStruct((1, D), F32)],
        scratch=[pltpu.VMEM((HALO_P + ts, D), F32), pltpu.VMEM((ts, D), BF16), pltpu.VMEM((ts + HALO_P, D), F32),
                 pltpu.VMEM((ts, D), F32), pltpu.VMEM((ng, PG, PG), BF16), pltpu.VMEM((ng, PG, PG), F32)])


def _mix0_bwd_a(dx1, z, ac, bconv, w_out, conv_a, ln_g, ln_b, conv_b, comm=None):
    s = dx1.shape[0]
    ts = min(TS_MIXB, s)
    n = s // ts
    rows = 32
    boffs_a = [K_A - 1 - k for k in range(K_A)]
    boffs_s = [K_S - 1 - k for k in range(K_S)]

    def body(dx1_ref, z_ref, ac_ref, bconv_ref, wout_ref, ca_ref, lg_ref, lb_ref, cb_ref,
             dz_ref, dca_ref, dlg_ref, dlb_ref, dcb_ref, dac_buf, dbc_buf, dca_acc, dcb_acc):
        i = pl.program_id(0)

        @pl.when(i == 0)
        def _():
            dac_buf[ts:ts + HALO_A, :] = jnp.zeros((HALO_A, A), F32)
            dbc_buf[ts:ts + HALO_S, :] = jnp.zeros((HALO_S, A), F32)
            dca_acc[...] = jnp.zeros(dca_acc.shape, F32)
            dcb_acc[...] = jnp.zeros(dcb_acc.shape, F32)
            dlg_ref[...] = jnp.zeros((1, A), F32)
            dlb_ref[...] = jnp.zeros((1, A), F32)

        dcat = lax.dot_general(dx1_ref[...].astype(BF16), wout_ref[...], NT_DIMS, preferred_element_type=F32)
        db = dcat[:, A:2 * A]
        dz_ref[:, 2 * A:3 * A] = (db * bconv_ref[...]).astype(BF16)
        dbc_buf[0:ts, :] = db * z_ref[:, 2 * A:3 * A]
        ac = ac_ref[...]
        xc = ac - jnp.mean(ac, axis=-1, keepdims=True)
        rstd = lax.rsqrt(jnp.mean(xc * xc, axis=-1, keepdims=True) + LN_EPS)
        xn = xc * rstd
        lg = lg_ref[...]
        ln = xn * lg + lb_ref[...]
        sl = _sigmoid(ln)
        dln = dcat[:, 0:A] * sl * (1.0 + ln * (1.0 - sl))
        dlg_ref[...] += jnp.sum(dln * xn, axis=0, keepdims=True)
        dlb_ref[...] += jnp.sum(dln, axis=0, keepdims=True)
        dxn = dln * lg
        dac_buf[0:ts, :] = rstd * (dxn - jnp.mean(dxn, axis=-1, keepdims=True)
                                   - xn * jnp.mean(dxn * xn, axis=-1, keepdims=True))

        def chunk(ci, carry):
            r0 = pl.multiple_of(ci * rows, rows)
            rs = pl.ds(r0, rows)
            for c0 in range(0, A, LANES):
                col = lambda grp: slice(grp * A + c0, grp * A + c0 + LANES)
                a_val, sg = z_ref[rs, col(0)], _sigmoid(z_ref[rs, col(1)])
                dglu = _conv_corr(dac_buf, ca_ref, a_val * sg, dca_acc, r0, rows, c0, LANES, boffs_a)
                dz_ref[rs, col(0)] = (dglu * sg).astype(BF16)
                dz_ref[rs, col(1)] = (dglu * a_val * sg * (1.0 - sg)).astype(BF16)
                c_gate, bc_val = z_ref[rs, col(3)], z_ref[rs, col(4)]
                dcv = _conv_corr(dbc_buf, cb_ref, c_gate * bc_val, dcb_acc, r0, rows, c0, LANES, boffs_s)
                dz_ref[rs, col(3)] = (dcv * bc_val).astype(BF16)
                dz_ref[rs, col(4)] = (dcv * c_gate).astype(BF16)
            return carry

        lax.fori_loop(0, ts // rows, chunk, 0)
        dac_buf[ts:ts + HALO_A, :] = dac_buf[0:HALO_A, :]
        dbc_buf[ts:ts + HALO_S, :] = dbc_buf[0:HALO_S, :]

        @pl.when(i == n - 1)
        def _():
            _finish_tap_sums(dca_acc, dca_ref, K_A)
            _finish_tap_sums(dcb_acc, dcb_ref, K_S)

    rev = lambda w: pl.BlockSpec((ts, w), lambda i: (n - 1 - i, 0))
    full = lambda r, c: pl.BlockSpec((r, c), lambda i: (0, 0))
    return _call(
        body, name="mix0_bwd_a", grid=(n,), comm=comm, args=(dx1, z, ac, bconv, w_out, conv_a, ln_g, ln_b, conv_b),
        in_specs=[rev(D), rev(NZ), rev(A), rev(A), _const((2 * A, D)), _const((K_A, A)), _const((1, A)), _const((1, A)),
                  _const((K_S, A))],
        out_specs=[rev(NZ), full(K_A, A), full(1, A), full(1, A), full(K_S, A)],
        out_shape=[jax.ShapeDtypeStruct((s, NZ), BF16), jax.ShapeDtypeStruct((K_A, A), F32), jax.ShapeDtypeStruct((1, A), F32),
                   jax.ShapeDtypeStruct((1, A), F32), jax.ShapeDtypeStruct((K_S, A), F32)],
        scratch=[pltpu.VMEM((ts + HALO_A, A), F32), pltpu.VMEM((ts + HALO_S, A), F32), pltpu.VMEM((SUBLANES * K_A, A), F32),
                 pltpu.VMEM((SUBLANES * K_S, A), F32)])


def _cast_later_weights(w_up, w_down, w_pool, comm):
    nl = w_up.shape[0]

    def body(up_ref, dn_ref, pool_ref, *outs):
        up_o, dn_o, pool_o = outs[:nl], outs[nl:2 * nl], outs[2 * nl]
        for layer in range(nl):
            @pl.when(pl.program_id(0) == layer)
            def _(layer=layer):
                up_o[layer][...] = up_ref[...].astype(BF16)
                dn_o[layer][...] = dn_ref[...].astype(BF16)
                if layer == 0:
                    pool_o[...] = pool_ref[...].astype(BF16)

    per_layer = lambda a: pl.BlockSpec((None,) + a.shape[1:], lambda l: (l,) + (0,) * (a.ndim - 1))
    whole = lambda shape: pl.BlockSpec(shape, lambda l: (0,) * len(shape))
    out_shapes = [w_up.shape[1:]] * nl + [w_down.shape[1:]] * nl + [w_pool.shape]
    return _call(body, name="cast_later_weights", grid=(nl,), comm=comm, args=(w_up, w_down, w_pool),
                 in_specs=[per_layer(w_up), per_layer(w_down), whole(w_pool.shape)],
                 out_specs=[whole(shape) for shape in out_shapes],
                 out_shape=[jax.ShapeDtypeStruct(shape, BF16) for shape in out_shapes])


def _train_step(x, target, place, shard, later, rep, small_pack, small_shapes, adamw):
    bw_up, bw_in = FF2 // NCHIP, NZ // NCHIP
    five = lambda a, k: a.reshape(NCHIP, 2, *HALF[k])

    half = lambda a, name: a.reshape((2,) + HALF[name])
    (up0, up1, dn0, dn1, pool_bf), (g_in, g_out, small_g) = _cast_later_weights(
        *later, comm=_gather_comm([shard["w_in"], shard["w_out"]], small_pack))
    shard = dict(shard, w_pool=half(pool_bf, "w_pool"), w_up0=half(up0, "w_up"), w_up1=half(up1, "w_up"),
                 w_down0=half(dn0, "w_down"), w_down1=half(dn1, "w_down"))
    whole = {}
    for k, part in zip(SMALL_SHARDED, _unpack(small_g, small_shapes, lead=(NCHIP,))):
        whole[k] = jnp.moveaxis(part, 0, 1).reshape(part.shape[1], NCHIP * part.shape[2])
    w_in, w_out = g_in.reshape(NCHIP, D, bw_in), g_out.reshape(2 * A, D)
    conv_ffn = whole["conv_ffn_w"].reshape(2, K_S, FF2)
    nffn = [rep["norm_ffn"][0:1], rep["norm_ffn"][1:2]]

    (h0, z, ac, bconv, cat, x1), (g_up0, g_dn0) = _mix0_fwd(
        x, rep["norm_mix_even"], w_in, whole["conv_a"], rep["ln_a_g"], rep["ln_a_b"], whole["conv_b"], w_out,
        comm=_gather_comm([shard["w_up0"], shard["w_down0"]], mid_steps=GATHER_MID_STEPS["mix0_fwd"]))
    w_up0, w_dn0 = g_up0.reshape(NCHIP, D, bw_up), g_dn0.reshape(FF, D)
    (hf0, u00, u0, act0, x2), (g_pool, g_up1, g_dn1) = _ffn_fwd(
        x1, nffn[0], w_up0, conv_ffn[0], w_dn0, "ffn0_fwd",
        comm=_gather_comm([shard["w_pool"], shard["w_up1"], shard["w_down1"]], mid_steps=GATHER_MID_STEPS["ffn0_fwd"]))
    w_pool = g_pool.reshape(NCHIP, len(POOL_WINDOWS), PG // NCHIP, PG)
    w_up1, w_dn1 = g_up1.reshape(NCHIP, D, bw_up), g_dn1.reshape(FF, D)
    x3 = _pool_fwd(x2, whole["norm_mix_odd"], w_pool, whole["pool_scale"])
    (hf1, u01, u1, act1, dx4, g_nfin, loss_part), _ = _ffn_fwd(x3, nffn[1], w_up1, conv_ffn[1], w_dn1, "ffn1_fwd",
                                                                head=(rep["norm_final"], target))

    psum = lambda k, g, ld, tag="": _pair_sum(place, g, ld, "pair_sum_" + k + tag)
    tot = lambda k, g, ld, p, layer=0, nl=1, prev=None: _chip_sum(place, g, ld, p, layer, nl, "chip_sum_%s%d" % (k, layer), prev)
    pair, chips, join = _pair_comm, _chips_comm, _join_comm

    gr_dn1 = five(_wgrad(act1, dx4, 1, "wgrad_down1")[0], "w_down")
    (du01, g_wc1), _ = _ffn_bwd_a(dx4, u1, u01, conv_ffn[1], w_dn1, "ffn1_bwd_a")
    g_up, (ld_dn1,) = _wgrad(hf1, du01, NCHIP, "wgrad_up1", comm=pair([gr_dn1]))
    gr_up1 = five(g_up, "w_up")
    s_dn1 = psum("w_down", gr_dn1, ld_dn1, "1")
    (dx3, g_nf1), (p_dn1, ld_up1) = _nt_rms_bwd(du01, w_up1, x3, nffn[1], dx4, "ffn1_bwd_b",
                                                comm=join(chips([s_dn1]), pair([gr_up1])))
    s_up1 = psum("w_up", gr_up1, ld_up1, "1")
    (dx2, g_wpool, g_scale, g_nmo), _ = _pool_bwd(dx3, x2, whole["norm_mix_odd"], w_pool, whole["pool_scale"])
    gr_pool = five(g_wpool, "w_pool")
    (du00, g_wc0), (p_up1,) = _ffn_bwd_a(dx2, u0, u00, conv_ffn[0], w_dn0, "ffn0_bwd_a", comm=chips([s_up1]))
    gr_dn0 = five(_wgrad(act0, dx2, 1, "wgrad_down0")[0], "w_down")
    g_up, (ld_dn0, ld_pool) = _wgrad(hf0, du00, NCHIP, "wgrad_up0", comm=pair([gr_dn0, gr_pool]))
    gr_up0 = five(g_up, "w_up")
    s_dn0, s_pool = psum("w_down", gr_dn0, ld_dn0, "0"), psum("w_pool", gr_pool, ld_pool)
    (dx1, g_nf0), (p_dn0, p_pool, ld_up0) = _nt_rms_bwd(du00, w_up0, x1, nffn[0], dx2, "ffn0_bwd_b",
                                                        comm=join(chips([s_dn0, s_pool]), pair([gr_up0])))
    s_up0 = psum("w_up", gr_up0, ld_up0, "0")
    gr_out = five(_wgrad(cat, dx1, 1, "wgrad_out")[0], "w_out")
    (dz, g_ca, g_lg, g_lb, g_cb), (p_up0, ld_out) = _mix0_bwd_a(
        dx1, z, ac, bconv, w_out, whole["conv_a"], rep["ln_a_g"], rep["ln_a_b"], whole["conv_b"],
        comm=join(chips([s_up0]), pair([gr_out])))
    s_out = psum("w_out", gr_out, ld_out)
    t_pool = tot("w_pool", gr_pool, ld_pool, p_pool)
    t_up = tot("w_up", gr_up0, ld_up0, p_up0, 0, 2, tot("w_up", gr_up1, ld_up1, p_up1, 1, 2))
    t_dn = tot("w_down", gr_dn0, ld_dn0, p_dn0, 0, 2, tot("w_down", gr_dn1, ld_dn1, p_dn1, 1, 2))
    small = {"norm_mix_odd": g_nmo, "pool_scale": g_scale, "norm_ffn": jnp.concatenate([g_nf0, g_nf1], axis=0),
             "conv_ffn_w": jnp.stack([g_wc0, g_wc1]), "norm_final": g_nfin, "loss": loss_part,
             "conv_a": g_ca, "ln_a_g": g_lg, "ln_a_b": g_lb, "conv_b": g_cb}
    g_in, (p_out, early_all, t_pool, t_up, t_dn) = _wgrad(
        h0, dz, NCHIP, "wgrad_in",
        comm=join(chips([s_out], _pack([small[k] for k in SMALL_EARLY])), _swap_comm([t_pool, t_up, t_dn])))
    gr_in = five(g_in, "w_in")
    n_mm = x.shape[0] // min(TS_MM, x.shape[0])
    (dx_a, g_nme_a), (ld_in,) = _nt_rms_bwd(dz, w_in, x, rep["norm_mix_even"], dx1, "mix0_bwd_b0", comm=pair([gr_in]),
                                            tiles=(0, n_mm // 2))
    s_in = psum("w_in", gr_in, ld_in)
    (grad_x, g_nme_b), (p_in,) = _nt_rms_bwd(dz, w_in, x, rep["norm_mix_even"], dx1, "mix0_bwd_b1", comm=chips([s_in]),
                                             tiles=(n_mm // 2, n_mm - n_mm // 2), dx_so_far=dx_a)
    small["norm_mix_even"] = g_nme_a + g_nme_b
    (late_all,) = _run_comm(chips([], _pack([small[k] for k in SMALL_LATE])), "rs_last")
    t_in, t_out = _run_comm(_swap_comm([tot("w_in", gr_in, ld_in, p_in), tot("w_out", gr_out, ld_out, p_out)]), "rs_swap")
    done = {k: adamw(k, t) for k, t in (("w_up", t_up), ("w_down", t_dn), ("w_pool", t_pool), ("w_in", t_in), ("w_out", t_out))}
    summed = dict(zip(SMALL_EARLY, _unpack(_sum_devices(early_all, "sum_small_early"), [small[k].shape for k in SMALL_EARLY])))
    summed.update(zip(SMALL_LATE, _unpack(_sum_devices(late_all, "sum_small_late"), [small[k].shape for k in SMALL_LATE])))
    return grad_x, done, summed


def _adamw_math(w, g, m, v):
    m = ADAM_B1 * m + (1.0 - ADAM_B1) * g
    v = ADAM_B2 * v + (1.0 - ADAM_B2) * (g * g)
    m_hat = m / (1.0 - ADAM_B1 ** ADAM_STEP)
    v_hat = v / (1.0 - ADAM_B2 ** ADAM_STEP)
    return -ADAM_LR * (m_hat / (jnp.sqrt(v_hat) + ADAM_EPS) + ADAM_WD * w), m, v


def _adamw_big(w, g, m, v, tr, name, comm=None):
    nl, rows, cols = w.shape

    def body(w_ref, g_ref, m_ref, v_ref, g2_ref, d_ref, m2_ref, v2_ref):
        gv = g_ref[...]
        g2_ref[...] = gv
        d_ref[...], m2_ref[...], v2_ref[...] = _adamw_math(w_ref[...], gv, m_ref[...], v_ref[...])

    spec = pl.BlockSpec((None, tr, cols), lambda l, r: (l, r, 0))
    return _call(body, name=name, grid=(nl, rows // tr), comm=comm, args=(w, g, m, v), in_specs=[spec] * 4,
                 out_specs=[spec] * 4, out_shape=[jax.ShapeDtypeStruct(w.shape, F32)] * 4)


def _adamw_small(ws, gs, ms, vs):
    n = len(ws)

    def body(*refs):
        for p in range(n):
            w_ref, g_ref, m_ref, v_ref = (refs[q * n + p] for q in range(4))
            d_ref, m2_ref, v2_ref = (refs[(4 + q) * n + p] for q in range(3))
            d_ref[...], m2_ref[...], v2_ref[...] = _adamw_math(w_ref[...], g_ref[...], m_ref[...], v_ref[...])

    whole = lambda a: pl.BlockSpec(a.shape, lambda: (0,) * a.ndim)
    outs = pl.pallas_call(
        body, name="adamw_small", in_specs=[whole(a) for a in ws] * 4, out_specs=[whole(a) for a in ws] * 3,
        out_shape=[jax.ShapeDtypeStruct(a.shape, F32) for a in ws] * 3,
        compiler_params=pltpu.CompilerParams(vmem_limit_bytes=VMEM_LIMIT_BYTES),
    )(*ws, *gs, *ms, *vs)
    return outs[0:n], outs[n:2 * n], outs[2 * n:3 * n]


def _place():
    x, y, c = lax.axis_index("x"), lax.axis_index("y"), lax.axis_index("c")
    chips = [(x, 1 - y), (1 - x, y), (1 - x, 1 - y)]
    blocks = [2 * cx + cy for cx, cy in chips]
    return x, y, c, 2 * x + y, chips, blocks


def _gather_comm(shards, small=None, mid_steps=None):
    na = len(shards)
    ns = 0 if small is None else 1

    def copies(ins, outs, sems):
        h1_send, h1_recv, h2_send, h2_recv, f1_send, f1_recv, f2_send, f2_recv, own_send, own_recv = sems[:10]
        x, y, c, j, chips, blocks = _place()
        sib = (x, y, 1 - c)
        piece = lambda a, p: pl.ds(p * (shards[a].shape[1] // 2), shards[a].shape[1] // 2)

        def remote(src, dst, send, recv, q, to):
            return pltpu.make_async_remote_copy(src_ref=src, dst_ref=dst, send_sem=send.at[q], recv_sem=recv.at[q],
                                                device_id=to, device_id_type=MESH)

        def hop1(a, k, arrival):
            dst = outs[a].at[blocks[k], c] if arrival else outs[a].at[j, c]
            return remote(dst if arrival else ins[a].at[c], dst, h1_send, h1_recv, 2 * a + k, (*chips[k], c))

        def hop2(a, p, arrival):
            ref = outs[a].at[blocks[2] if arrival else blocks[p], c, piece(a, p)]
            return remote(ref, ref, h2_send, h2_recv, 2 * a + p, (*chips[1 - p], c))

        def fwd1(a, k, half):
            ref = outs[a].at[blocks[k], half]
            return remote(ref, ref, f1_send, f1_recv, 2 * a + k, sib)

        def fwd2(a, p, half):
            ref = outs[a].at[blocks[2], half, piece(a, p)]
            return remote(ref, ref, f2_send, f2_recv, 2 * a + p, sib)

        own = [remote(ins[a], outs[a].at[j], own_send, own_recv, a, sib) for a in range(na)]
        small_copies = [remote(ins[na], outs[na].at[j], sems[10], sems[11], k, (*chips[k], c)) for k in range(3 * ns)]
        local = [pltpu.make_async_copy(ins[na], outs[na].at[j], sems[12])] if ns else []
        return hop1, hop2, fwd1, fwd2, own, small_copies, local, c

    pairs = [(a, k) for a in range(na) for k in range(2)]

    def start(ins, outs, sems):
        hop1, _, _, _, own, small_copies, local, _ = copies(ins, outs, sems)
        for cp in local + own + [hop1(a, k, False) for a, k in pairs] + small_copies:
            cp.start()

    def mid(ins, outs, sems):
        hop1, hop2, fwd1, _, _, _, _, c = copies(ins, outs, sems)
        for a, k in pairs:
            hop1(a, k, True).wait_recv()
            hop2(a, k, False).start()
            fwd1(a, k, c).start()

    def finish(ins, outs, sems):
        hop1, hop2, fwd1, fwd2, own, small_copies, local, c = copies(ins, outs, sems)
        for a, p in pairs:
            hop2(a, p, True).wait_recv()
            fwd2(a, p, c).start()
        for cp in small_copies:
            cp.wait()
        for a, k in pairs:
            hop1(a, k, False).wait_send()
            hop2(a, k, False).wait_send()
            fwd1(a, k, c).wait_send()
            fwd1(a, k, 1 - c).wait_recv()
            fwd2(a, k, c).wait_send()
            fwd2(a, k, 1 - c).wait_recv()
        for cp in own + local:
            cp.wait()

    out_shapes = [jax.ShapeDtypeStruct((NCHIP,) + s.shape, s.dtype) for s in shards]
    sems = [pltpu.SemaphoreType.DMA((2 * na,))] * 8 + [pltpu.SemaphoreType.DMA((na,))] * 2
    if ns:
        out_shapes.append(jax.ShapeDtypeStruct((NCHIP,) + small.shape, small.dtype))
        sems += [pltpu.SemaphoreType.DMA((3,)), pltpu.SemaphoreType.DMA((3,)), pltpu.SemaphoreType.DMA]
    if mid_steps is not None:
        return _Comm(list(shards) + [small] * ns, out_shapes, sems, start, finish, mid=mid, mid_steps=mid_steps)

    def forward_and_finish(ins, outs, sems):
        mid(ins, outs, sems)
        finish(ins, outs, sems)

    return _Comm(list(shards) + [small] * ns, out_shapes, sems, start, forward_and_finish)


def _simple_comm(inputs, out_shapes, make_copies, n_sems, aliases=None):
    def start(ins, outs, sems):
        for cp in make_copies(ins, outs, sems):
            cp.start()

    def finish(ins, outs, sems):
        for cp in make_copies(ins, outs, sems):
            cp.wait()

    return _Comm(inputs, out_shapes, [pltpu.SemaphoreType.DMA((n,)) for n in n_sems], start, finish, aliases)


def _pair_comm(grads):
    def make_copies(ins, outs, sems):
        x, y, c, _, _, _ = _place()
        return [pltpu.make_async_remote_copy(
            src_ref=ins[a].at[:, 1 - c], dst_ref=outs[a], send_sem=sems[0].at[a], recv_sem=sems[1].at[a],
            device_id=(x, y, 1 - c), device_id_type=MESH) for a in range(len(grads))]

    out_shapes = [jax.ShapeDtypeStruct(g.shape[:1] + g.shape[2:], F32) for g in grads]
    return _simple_comm(grads, out_shapes, make_copies, [len(grads)] * 2)


def _chips_comm(sums, small=None):
    na = len(sums)
    nk = NCHIP - 1

    def make_copies(ins, outs, sems):
        x, y, c, _, chips, _ = _place()
        copies = [pltpu.make_async_remote_copy(
            src_ref=ins[a].at[k], dst_ref=outs[a].at[k], send_sem=sems[0].at[a * nk + k],
            recv_sem=sems[1].at[a * nk + k], device_id=(*chips[k], c), device_id_type=MESH)
            for a in range(na) for k in range(nk)]
        if small is not None:
            me = 4 * x + 2 * y + c
            for r in range(1, NDEV):
                peer = (1 - x if r & 4 else x, 1 - y if r & 2 else y, 1 - c if r & 1 else c)
                copies.append(pltpu.make_async_remote_copy(
                    src_ref=ins[na], dst_ref=outs[na].at[me], send_sem=sems[2].at[r - 1], recv_sem=sems[3].at[r - 1],
                    device_id=peer, device_id_type=MESH))
            copies.append(pltpu.make_async_copy(ins[na], outs[na].at[me], sems[4].at[0]))
        return copies

    out_shapes = [jax.ShapeDtypeStruct(g.shape, BF16) for g in sums]
    chip_sems = [max(na * nk, 1)] * 2
    if small is None:
        return _simple_comm(sums, out_shapes, make_copies, chip_sems)
    out_shapes.append(jax.ShapeDtypeStruct((NDEV,) + small.shape, F32))
    return _simple_comm(list(sums) + [small], out_shapes, make_copies, chip_sems + [NDEV - 1] * 2 + [1])


def _swap_comm(totals):
    def make_copies(ins, outs, sems):
        x, y, c, _, _, _ = _place()
        return [pltpu.make_async_remote_copy(
            src_ref=outs[a].at[:, c], dst_ref=outs[a].at[:, c], send_sem=sems[0].at[a], recv_sem=sems[1].at[a],
            device_id=(x, y, 1 - c), device_id_type=MESH) for a in range(len(totals))]

    out_shapes = [jax.ShapeDtypeStruct(t.shape, F32) for t in totals]
    return _simple_comm(totals, out_shapes, make_copies, [len(totals)] * 2, aliases={a: a for a in range(len(totals))})


def _pair_sum(place, grad, landed, name):
    _, _, rows, cols = grad.shape

    def body(place_ref, g_ref, l_ref, o_ref):
        o_ref[...] = (g_ref[...] + l_ref[...]).astype(BF16)

    other = lambda k, p: jnp.bitwise_xor(p[0], k + 1)
    return pl.pallas_call(
        body, name=name,
        grid_spec=pltpu.PrefetchScalarGridSpec(
            num_scalar_prefetch=1, grid=(NCHIP - 1,),
            in_specs=[pl.BlockSpec((None, None, rows, cols), lambda k, p: (other(k, p), p[1], 0, 0)),
                      pl.BlockSpec((None, rows, cols), lambda k, p: (other(k, p), 0, 0))],
            out_specs=pl.BlockSpec((None, rows, cols), lambda k, p: (k, 0, 0))),
        out_shape=jax.ShapeDtypeStruct((NCHIP - 1, rows, cols), BF16), compiler_params=_cparams(1),
    )(place, grad, landed)


def _chip_sum(place, grad, landed, parts, layer, n_layers, name, prev=None):
    _, _, rows, cols = grad.shape

    def body(*refs):
        g_ref, l_ref, p_ref, o_ref = refs[1], refs[2], refs[3], refs[-1]
        tot = g_ref[...] + l_ref[...]
        for k in range(NCHIP - 1):
            tot = tot + p_ref[k].astype(F32)
        o_ref[...] = tot

    in_specs = [pl.BlockSpec((None, None, rows, cols), lambda i, p: (p[0], p[1], 0, 0)),
                pl.BlockSpec((None, rows, cols), lambda i, p: (p[0], 0, 0)),
                pl.BlockSpec((NCHIP - 1, rows, cols), lambda i, p: (0, 0, 0))]
    args = [place, grad, landed, parts]
    if prev is not None:
        in_specs.append(ANY)
        args.append(prev)
    return pl.pallas_call(
        body, name=name,
        grid_spec=pltpu.PrefetchScalarGridSpec(
            num_scalar_prefetch=1, grid=(1,), in_specs=in_specs,
            out_specs=pl.BlockSpec((None, None, rows, cols), lambda i, p: (layer, p[1], 0, 0))),
        out_shape=jax.ShapeDtypeStruct((n_layers, 2, rows, cols), F32),
        input_output_aliases={} if prev is None else {4: 0}, compiler_params=_cparams(1),
    )(*args)


def _sum_devices(parts, name):
    def body(p_ref, o_ref):
        tot = p_ref[0]
        for d in range(1, NDEV):
            tot = tot + p_ref[d]
        o_ref[...] = tot

    return pl.pallas_call(
        body, name=name, in_specs=[pl.BlockSpec(parts.shape, lambda: (0, 0, 0))],
        out_specs=pl.BlockSpec(parts.shape[1:], lambda: (0, 0)), out_shape=jax.ShapeDtypeStruct(parts.shape[1:], F32),
    )(parts)


def _pack(parts):
    rows = []
    for p in parts:
        p = p.reshape(-1, LANES)
        rows.append(jnp.pad(p, ((0, -p.shape[0] % SUBLANES), (0, 0))))
    return jnp.concatenate(rows, axis=0)


def _unpack(buf, shapes, lead=()):
    out, r0 = [], 0
    nl = len(lead)
    for shp in shapes:
        nrow = 1
        for d in shp:
            nrow *= d
        nrow //= LANES
        out.append(buf[(slice(None),) * nl + (slice(r0, r0 + nrow),)].reshape(lead + tuple(shp)))
        r0 += nrow + (-nrow % SUBLANES)
    return out


WEIGHT_ORDER = ("norm_mix_even", "w_in", "conv_a", "ln_a_g", "ln_a_b", "conv_b", "w_out", "norm_mix_odd", "w_pool",
                "pool_scale", "norm_ffn", "w_up", "conv_ffn_w", "w_down", "norm_final")
BIG = ("w_in", "w_out", "w_pool", "w_up", "w_down")
HALF = {"w_in": (D // 2, NZ // NCHIP), "w_out": (2 * A // NCHIP // 2, D), "w_pool": (PG // 2, PG),
        "w_up": (D // 2, FF2 // NCHIP), "w_down": (FF // NCHIP // 2, D)}
SMALL_SHARDED = ("conv_a", "conv_b", "conv_ffn_w", "norm_mix_odd", "pool_scale")
SMALL_ALL = ("norm_mix_even", "conv_a", "ln_a_g", "ln_a_b", "conv_b", "norm_mix_odd", "pool_scale", "norm_ffn", "conv_ffn_w",
             "norm_final")
SMALL_EARLY = ("norm_mix_odd", "pool_scale", "norm_ffn", "conv_ffn_w", "norm_final", "loss", "conv_a", "ln_a_g", "ln_a_b", "conv_b")
SMALL_LATE = ("norm_mix_even",)


def kernel(x, norm_mix_even, w_in, conv_a, ln_a_g, ln_a_b, conv_b, w_out, norm_mix_odd, w_pool, pool_scale, norm_ffn, w_up, conv_ffn_w, w_down, norm_final, loss_target, m_norm_mix_even, m_w_in, m_conv_a, m_ln_a_g, m_ln_a_b, m_conv_b, m_w_out, m_norm_mix_odd, m_w_pool, m_pool_scale, m_norm_ffn, m_w_up, m_conv_ffn_w, m_w_down, m_norm_final, v_norm_mix_even, v_w_in, v_conv_a, v_ln_a_g, v_ln_a_b, v_conv_b, v_w_out, v_norm_mix_odd, v_w_pool, v_pool_scale, v_norm_ffn, v_w_up, v_conv_ffn_w, v_w_down, v_norm_final):
    w = dict(norm_mix_even=norm_mix_even, w_in=w_in, conv_a=conv_a, ln_a_g=ln_a_g, ln_a_b=ln_a_b, conv_b=conv_b, w_out=w_out,
             norm_mix_odd=norm_mix_odd, w_pool=w_pool, pool_scale=pool_scale, norm_ffn=norm_ffn, w_up=w_up,
             conv_ffn_w=conv_ffn_w, w_down=w_down, norm_final=norm_final)
    m = dict(norm_mix_even=m_norm_mix_even, w_in=m_w_in, conv_a=m_conv_a, ln_a_g=m_ln_a_g, ln_a_b=m_ln_a_b, conv_b=m_conv_b,
             w_out=m_w_out, norm_mix_odd=m_norm_mix_odd, w_pool=m_w_pool, pool_scale=m_pool_scale, norm_ffn=m_norm_ffn,
             w_up=m_w_up, conv_ffn_w=m_conv_ffn_w, w_down=m_w_down, norm_final=m_norm_final)
    v = dict(norm_mix_even=v_norm_mix_even, w_in=v_w_in, conv_a=v_conv_a, ln_a_g=v_ln_a_g, ln_a_b=v_ln_a_b, conv_b=v_conv_b,
             w_out=v_w_out, norm_mix_odd=v_norm_mix_odd, w_pool=v_w_pool, pool_scale=v_pool_scale, norm_ffn=v_norm_ffn,
             w_up=v_w_up, conv_ffn_w=v_conv_ffn_w, w_down=v_w_down, norm_final=v_norm_final)
    chip = 2 * lax.axis_index("x") + lax.axis_index("y")
    place = jnp.stack([chip, lax.axis_index("c")]).astype(jnp.int32)

    half = lambda a, name: a.astype(BF16).reshape((2,) + HALF[name])
    shard = {"w_in": half(w_in[0], "w_in"), "w_out": half(w_out[0], "w_out")}
    small_shapes = [w[k].shape[-2:] if w[k].ndim == 3 and k != "conv_ffn_w" else (w[k].size // w[k].shape[-1], w[k].shape[-1])
                    for k in SMALL_SHARDED]
    rep = dict(norm_mix_even=norm_mix_even, ln_a_g=ln_a_g, ln_a_b=ln_a_b, norm_ffn=norm_ffn, norm_final=norm_final.reshape(1, D))
    as3 = lambda a: a.reshape(a.shape[0], -1, a.shape[-1])

    def adamw(k, total):
        outs, _ = _adamw_big(as3(w[k]), as3(total), as3(m[k]), as3(v[k]), ADAMW_ROWS[k], "adamw_" + k)
        return [a.reshape(w[k].shape) for a in outs]

    grad_x, done, summed = _train_step(
        x[0], loss_target[0], place, shard, (w_up, w_down, w_pool[0]), rep, _pack([w[k] for k in SMALL_SHARDED]), small_shapes,
        adamw)
    loss = summed["loss"][0, 0]

    grad, delta, new_m, new_v = ({k: done[k][q] for k in BIG} for q in range(4))
    for k in SMALL_ALL:
        gsum = summed[k]
        if k in SMALL_SHARDED:
            cols = w[k].shape[-1]
            gsum = lax.dynamic_slice_in_dim(gsum, chip * cols, cols, axis=gsum.ndim - 1)
        grad[k] = gsum.reshape(w[k].shape)

    as2 = lambda a: a.reshape(1, -1) if a.ndim == 1 else a[0] if a.ndim == 3 and a.shape[0] == 1 else a
    ds, ms, vs = _adamw_small(*[[as2(t[k]) for k in SMALL_ALL] for t in (w, grad, m, v)])
    for k, d2, m2, v2 in zip(SMALL_ALL, ds, ms, vs):
        delta[k], new_m[k], new_v[k] = (a.reshape(w[k].shape) for a in (d2, m2, v2))

    return (loss, grad_x[None], *[grad[k] for k in WEIGHT_ORDER], *[delta[k] for k in WEIGHT_ORDER],
            *[new_m[k] for k in WEIGHT_ORDER], *[new_v[k] for k in WEIGHT_ORDER])
```

```python
import functools

import jax
import jax.numpy as jnp
from jax import lax
from jax.experimental import pallas as pl
from jax.experimental.pallas import tpu as pltpu

F32, BF16 = jnp.float32, jnp.bfloat16

D = 1024
A = 512
NZ = 5 * A
FF = 2816
FF2 = 2 * FF
NCHIP = 4
NDEV = 8
K_A, K_S = 31, 3
POOL_WINDOWS = (2, 4, 8, 16)
PG = D // len(POOL_WINDOWS)
RMS_EPS, LN_EPS = 1e-6, 1e-5
ADAM_LR, ADAM_B1, ADAM_B2, ADAM_EPS, ADAM_WD, ADAM_STEP = 0.001, 0.9, 0.999, 1e-08, 0.01, 10

HALO_A, HALO_S, HALO_P = 32, 8, 16
SUBLANES = 8
LANES = 128
VMEM_LIMIT_BYTES = 56 * 1024 * 1024

TS_MIX = 512
TS_MIXB = 512
TS_FFN = 256
TS_POOL = 512
TS_MM = 512
TS_WGRAD = 1024
ADAMW_ROWS = {"w_in": 512, "w_out": 256, "w_pool": 256, "w_up": 256, "w_down": 352}
R_CHUNK = 64
GATHER_MID_STEPS = {"mix0_fwd": 2, "ffn0_fwd": 5}

MESH = pl.DeviceIdType.MESH
ANY = pl.BlockSpec(memory_space=pl.ANY)
NT_DIMS = (((1,), (1,)), ((), ()))
TN_DIMS = (((0,), (0,)), ((), ()))


def _cparams(n_axes):
    return pltpu.CompilerParams(dimension_semantics=("arbitrary",) * n_axes, vmem_limit_bytes=VMEM_LIMIT_BYTES)


def _const(shape):
    nd = len(shape)
    return pl.BlockSpec(shape, lambda *_: (0,) * nd, pipeline_mode=pl.Buffered(1))


def _sigmoid(v):
    return 1.0 / (1.0 + jnp.exp(-v))


def _rsqrt_mean_sq(x):
    return lax.rsqrt(jnp.mean(x * x, axis=-1, keepdims=True) + RMS_EPS)


def _rms_bwd(dh, xh, r, g):
    dxh = dh * g
    return r * (dxh - xh * jnp.mean(dxh * xh, axis=-1, keepdims=True))


def _shifted(buf_ref, row0, rows, col0, width, offsets):
    lo = (min(offsets) // SUBLANES) * SUBLANES
    hi = -(-(max(offsets) + rows) // SUBLANES) * SUBLANES
    start = row0 + lo if isinstance(row0, int) else pl.multiple_of(row0 + lo, SUBLANES)
    win = buf_ref[pl.ds(start, hi - lo), col0:col0 + width]
    out = {}
    for res in sorted({(o - lo) % SUBLANES for o in offsets}):
        qs = {o: (o - lo) // SUBLANES for o in offsets if (o - lo) % SUBLANES == res}
        base = pltpu.roll(win, hi - lo - res, 0) if res else win
        for o, q in qs.items():
            out[o] = base[SUBLANES * q:SUBLANES * q + rows, :]
    return out


def _rowsum8(v):
    acc = v[0:SUBLANES, :]
    for r in range(SUBLANES, v.shape[0], SUBLANES):
        acc = acc + v[r:r + SUBLANES, :]
    return acc


def _taps(w_ref, sh, offsets, col0, width):
    acc = None
    for k, o in enumerate(offsets):
        term = w_ref[k:k + 1, col0:col0 + width] * sh[o]
        acc = term if acc is None else acc + term
    return acc


def _window_bases(buf_ref, row0, rows, col0, width, offsets):
    lo = (min(offsets) // SUBLANES) * SUBLANES
    hi = -(-(max(offsets) + rows) // SUBLANES) * SUBLANES
    start = row0 + lo if isinstance(row0, int) else pl.multiple_of(row0 + lo, SUBLANES)
    win = buf_ref[pl.ds(start, hi - lo), col0:col0 + width]
    for res in sorted({(o - lo) % SUBLANES for o in offsets}):
        taps = [(k, (o - lo) // SUBLANES * SUBLANES) for k, o in enumerate(offsets) if (o - lo) % SUBLANES == res]
        yield (pltpu.roll(win, hi - lo - res, 0) if res else win), taps


def _conv_acc(buf_ref, w_ref, row0, rows, col0, width, offsets):
    acc = None
    for base, taps in _window_bases(buf_ref, row0, rows, col0, width, offsets):
        for k, q in taps:
            term = w_ref[k:k + 1, col0:col0 + width] * base[q:q + rows, :]
            acc = term if acc is None else acc + term
    return acc


def _conv_corr(buf_ref, w_ref, other, acc_ref, row0, rows, col0, width, offsets):
    acc = None
    for base, taps in _window_bases(buf_ref, row0, rows, col0, width, offsets):
        for k, q in taps:
            sl = base[q:q + rows, :]
            term = w_ref[k:k + 1, col0:col0 + width] * sl
            acc = term if acc is None else acc + term
            acc_ref[SUBLANES * k:SUBLANES * (k + 1), col0:col0 + width] += _rowsum8(sl * other)
    return acc


def _finish_tap_sums(acc_ref, out_ref, n_taps):
    for k in range(n_taps):
        out_ref[k:k + 1, :] = jnp.sum(acc_ref[SUBLANES * k:SUBLANES * (k + 1), :], axis=0, keepdims=True)


class _Comm:
    def __init__(self, inputs, out_shapes, sems, start, finish, aliases=None, mid=None, mid_steps=1):
        self.inputs, self.out_shapes, self.sems = list(inputs), list(out_shapes), list(sems)
        self.start, self.finish, self.mid, self.aliases = start, finish, mid, dict(aliases or {})
        self.mid_steps = mid_steps


def _join_comm(a, b):
    ni, no, ns = len(a.inputs), len(a.out_shapes), len(a.sems)

    def both(name):
        def run(ins, outs, sems):
            if getattr(a, name) is not None:
                getattr(a, name)(ins[:ni], outs[:no], sems[:ns])
            if getattr(b, name) is not None:
                getattr(b, name)(ins[ni:], outs[no:], sems[ns:])
        return run

    aliases = {**a.aliases, **{ni + i: no + o for i, o in b.aliases.items()}}
    mid = both("mid") if (a.mid is not None or b.mid is not None) else None
    return _Comm(a.inputs + b.inputs, a.out_shapes + b.out_shapes, a.sems + b.sems, both("start"), both("finish"), aliases, mid)


def _call(body, *, name, grid, in_specs, out_specs, out_shape, args, scratch=(), comm=None, aliases=None):
    n_in, n_out, n_scr, n_axes = len(in_specs), len(out_specs), len(scratch), len(grid)
    params = pltpu.CompilerParams(dimension_semantics=("arbitrary",) * n_axes, vmem_limit_bytes=VMEM_LIMIT_BYTES)
    aliases = dict(aliases or {})
    if comm is None:
        outs = pl.pallas_call(body, name=name, grid=grid, in_specs=list(in_specs), out_specs=list(out_specs),
                              out_shape=list(out_shape), scratch_shapes=list(scratch), input_output_aliases=aliases,
                              compiler_params=params)(*args)
        return list(outs), []
    ci, co = len(comm.inputs), len(comm.out_shapes)

    def wrapped(*refs):
        k_in, c_in = refs[:n_in], refs[n_in:n_in + ci]
        o0 = n_in + ci
        k_out, c_out = refs[o0:o0 + n_out], refs[o0 + n_out:o0 + n_out + co]
        s0 = o0 + n_out + co
        k_scr, c_sem = refs[s0:s0 + n_scr], refs[s0 + n_scr:]
        first = pl.program_id(0) == 0
        last = pl.program_id(0) == grid[0] - 1
        for ax in range(1, n_axes):
            first = jnp.logical_and(first, pl.program_id(ax) == 0)
            last = jnp.logical_and(last, pl.program_id(ax) == grid[ax] - 1)

        @pl.when(first)
        def _():
            comm.start(c_in, c_out, c_sem)

        mid_early = comm.mid is not None and n_axes == 1 and grid[0] > comm.mid_steps
        if mid_early:
            @pl.when(pl.program_id(0) == grid[0] - 1 - comm.mid_steps)
            def _():
                comm.mid(c_in, c_out, c_sem)

        body(*k_in, *k_out, *k_scr)

        @pl.when(last)
        def _():
            if comm.mid is not None and not mid_early:
                comm.mid(c_in, c_out, c_sem)
            comm.finish(c_in, c_out, c_sem)

    outs = pl.pallas_call(
        wrapped, name=name, grid=grid, in_specs=list(in_specs) + [ANY] * ci, out_specs=list(out_specs) + [ANY] * co,
        out_shape=list(out_shape) + comm.out_shapes, scratch_shapes=list(scratch) + comm.sems,
        input_output_aliases={**aliases, **{n_in + i: n_out + o for i, o in comm.aliases.items()}}, compiler_params=params,
    )(*args, *comm.inputs)
    return list(outs[:n_out]), list(outs[n_out:])


def _run_comm(comm, name):
    ci, co = len(comm.inputs), len(comm.out_shapes)

    def body(*refs):
        c_in, c_out, c_sem = refs[:ci], refs[ci:ci + co], refs[ci + co:]
        comm.start(c_in, c_out, c_sem)
        if comm.mid is not None:
            comm.mid(c_in, c_out, c_sem)
        comm.finish(c_in, c_out, c_sem)

    return list(pl.pallas_call(body, name=name, in_specs=[ANY] * ci, out_specs=[ANY] * co, out_shape=comm.out_shapes,
                               scratch_shapes=comm.sems, input_output_aliases=comm.aliases)(*comm.inputs))


def _mix0_fwd(x, g, w_in, conv_a, ln_g, ln_b, conv_b, w_out, comm=None):
    s = x.shape[0]
    ts = min(TS_MIX, s)
    n = s // ts
    bw = NZ // NCHIP
    offs_a = [HALO_A - (K_A - 1) + k for k in range(K_A)]
    offs_s = [HALO_S - (K_S - 1) + k for k in range(K_S)]

    def body(x_ref, g_ref, win_ref, ca_ref, lg_ref, lb_ref, cb_ref, wout_ref,
             h_ref, z_ref, ac_ref, bconv_buf, cat_ref, x1_ref, glu_buf, cv_buf):
        i = pl.program_id(0)

        @pl.when(i == 0)
        def _():
            glu_buf[0:HALO_A, :] = jnp.zeros((HALO_A, A), F32)
            cv_buf[0:HALO_S, :] = jnp.zeros((HALO_S, A), F32)

        xv = x_ref[...]
        h = (xv * _rsqrt_mean_sq(xv) * g_ref[...]).astype(BF16)
        h_ref[...] = h
        for j in range(NCHIP):
            z_ref[:, j * bw:(j + 1) * bw] = jnp.dot(h, win_ref[j], preferred_element_type=F32)
        glu_buf[HALO_A:HALO_A + ts, :] = z_ref[:, 0:A] * _sigmoid(z_ref[:, A:2 * A])
        cv_buf[HALO_S:HALO_S + ts, :] = z_ref[:, 3 * A:4 * A] * z_ref[:, 4 * A:5 * A]

        def chunk(ci, carry):
            r0 = pl.multiple_of(ci * R_CHUNK, R_CHUNK)
            for c0 in range(0, A, LANES):
                ac_ref[pl.ds(r0, R_CHUNK), c0:c0 + LANES] = _conv_acc(glu_buf, ca_ref, r0, R_CHUNK, c0, LANES, offs_a)
                bconv_buf[pl.ds(r0, R_CHUNK), c0:c0 + LANES] = _conv_acc(cv_buf, cb_ref, r0, R_CHUNK, c0, LANES, offs_s)
            return carry

        lax.fori_loop(0, ts // R_CHUNK, chunk, 0)
        glu_buf[0:HALO_A, :] = glu_buf[ts:ts + HALO_A, :]
        cv_buf[0:HALO_S, :] = cv_buf[ts:ts + HALO_S, :]

        ac = ac_ref[...]
        xc = ac - jnp.mean(ac, axis=-1, keepdims=True)
        xn = xc * lax.rsqrt(jnp.mean(xc * xc, axis=-1, keepdims=True) + LN_EPS)
        ln = xn * lg_ref[...] + lb_ref[...]
        cat_ref[:, 0:A] = (ln * _sigmoid(ln)).astype(BF16)
        cat_ref[:, A:2 * A] = (z_ref[:, 2 * A:3 * A] * bconv_buf[...]).astype(BF16)
        x1_ref[...] = xv + jnp.dot(cat_ref[...], wout_ref[...], preferred_element_type=F32)

    tile = lambda w: pl.BlockSpec((ts, w), lambda i: (i, 0))
    return _call(
        body, name="mix0_fwd", grid=(n,), comm=comm, args=(x, g, w_in, conv_a, ln_g, ln_b, conv_b, w_out),
        in_specs=[tile(D), _const((1, D)), _const((NCHIP, D, bw)), _const((K_A, A)), _const((1, A)), _const((1, A)),
                  _const((K_S, A)), _const((2 * A, D))],
        out_specs=[tile(D), tile(NZ), tile(A), tile(A), tile(2 * A), tile(D)],
        out_shape=[jax.ShapeDtypeStruct((s, D), BF16), jax.ShapeDtypeStruct((s, NZ), F32), jax.ShapeDtypeStruct((s, A), F32),
                   jax.ShapeDtypeStruct((s, A), F32), jax.ShapeDtypeStruct((s, 2 * A), BF16), jax.ShapeDtypeStruct((s, D), F32)],
        scratch=[pltpu.VMEM((HALO_A + ts, A), F32), pltpu.VMEM((HALO_S + ts, A), F32)])


def _ffn_fwd(x, g, w_up, wc, w_down, name, comm=None, head=None):
    s = x.shape[0]
    ts = min(TS_FFN, s)
    n = s // ts
    bw = FF2 // NCHIP
    rows = 32
    offs = [HALO_S - (K_S - 1) + k for k in range(K_S)]
    n_in = 5 + (2 if head else 0)

    def body(*refs):
        x_ref, g_ref, wup_ref, wc_ref, wdn_ref = refs[:5]
        h_ref, u0_ref, u_ref, act_ref, xo_ref = refs[n_in:n_in + 5]
        cbuf = refs[-1]
        i = pl.program_id(0)

        @pl.when(i == 0)
        def _():
            cbuf[0:HALO_S, :] = jnp.zeros((HALO_S, FF2), F32)
            if head:
                refs[n_in + 5][...] = jnp.zeros((1, D), F32)
                refs[n_in + 6][...] = jnp.zeros((1, LANES), F32)

        xv = x_ref[...]
        h = (xv * _rsqrt_mean_sq(xv) * g_ref[...]).astype(BF16)
        h_ref[...] = h
        f = None
        for p in range(NCHIP // 2):
            for j in (p, NCHIP // 2 + p):
                zc = jnp.dot(h, wup_ref[j], preferred_element_type=F32)
                u0_ref[:, j * bw:(j + 1) * bw] = zc.astype(BF16)
                cbuf[HALO_S:HALO_S + ts, j * bw:(j + 1) * bw] = zc
            for r0 in range(0, ts, rows):
                for c0 in range(p * bw, (p + 1) * bw, LANES):
                    ug = _taps(wc_ref, _shifted(cbuf, r0, rows, c0, LANES, offs), offs, c0, LANES)
                    uv = _taps(wc_ref, _shifted(cbuf, r0, rows, FF + c0, LANES, offs), offs, FF + c0, LANES)
                    u_ref[r0:r0 + rows, c0:c0 + LANES] = ug
                    u_ref[r0:r0 + rows, FF + c0:FF + c0 + LANES] = uv
                    act_ref[r0:r0 + rows, c0:c0 + LANES] = (ug * _sigmoid(ug) * uv).astype(BF16)
            fp = jnp.dot(act_ref[:, p * bw:(p + 1) * bw], wdn_ref[p * bw:(p + 1) * bw, :], preferred_element_type=F32)
            f = fp if f is None else f + fp
        cbuf[0:HALO_S, :] = cbuf[ts:ts + HALO_S, :]
        if not head:
            xo_ref[...] = xv + f
        else:
            gf_ref, t_ref, dgf_ref, loss_ref = refs[5], refs[6], refs[n_in + 5], refs[n_in + 6]
            xo = xv + f
            r = _rsqrt_mean_sq(xo)
            xh = xo * r
            gv = gf_ref[...]
            err = xh * gv - t_ref[...]
            loss_ref[...] += jnp.sum(jnp.sum(err * err, axis=1, keepdims=True), axis=0, keepdims=True) * (0.5 / D)
            dy = err * (1.0 / D)
            dgf_ref[...] += jnp.sum(dy * xh, axis=0, keepdims=True)
            xo_ref[...] = _rms_bwd(dy, xh, r, gv)

    tile = lambda w: pl.BlockSpec((ts, w), lambda i: (i, 0))
    one_row = lambda w: pl.BlockSpec((1, w), lambda i: (0, 0))
    return _call(
        body, name=name, grid=(n,), comm=comm, args=(x, g, w_up, wc, w_down) + (tuple(head) if head else ()),
        in_specs=[tile(D), _const((1, D)), _const((NCHIP, D, bw)), _const((K_S, FF2)), _const((FF, D))]
        + ([_const((1, D)), tile(D)] if head else []),
        out_specs=[tile(D), tile(FF2), tile(FF2), tile(FF), tile(D)] + ([one_row(D), one_row(LANES)] if head else []),
        out_shape=[jax.ShapeDtypeStruct((s, D), BF16), jax.ShapeDtypeStruct((s, FF2), BF16), jax.ShapeDtypeStruct((s, FF2), F32),
                   jax.ShapeDtypeStruct((s, FF), BF16), jax.ShapeDtypeStruct((s, D), F32)]
        + ([jax.ShapeDtypeStruct((1, D), F32), jax.ShapeDtypeStruct((1, LANES), F32)] if head else []),
        scratch=[pltpu.VMEM((HALO_S + ts, FF2), F32)])


def _pool_windows(hbuf, pbuf, tile_row0, ts):
    def chunk(ci, carry):
        r0 = pl.multiple_of(ci * R_CHUNK, R_CHUNK)
        t1 = (tile_row0 + r0 + lax.broadcasted_iota(jnp.int32, (R_CHUNK, 1), 0) + 1).astype(F32)
        for gi, w in enumerate(POOL_WINDOWS):
            cnt = jnp.minimum(t1, float(w))
            offs = [HALO_P - jj for jj in range(w)]
            for c0 in range(gi * PG, (gi + 1) * PG, LANES):
                sh = _shifted(hbuf, r0, R_CHUNK, c0, LANES, offs)
                tot = sh[offs[0]]
                for o in offs[1:]:
                    tot = tot + sh[o]
                pbuf[pl.ds(r0, R_CHUNK), c0:c0 + LANES] = (tot / cnt - sh[HALO_P]).astype(BF16)
        return carry

    lax.fori_loop(0, ts // R_CHUNK, chunk, 0)


def _assemble_wpool(wp_ref, wps):
    rb = PG // NCHIP
    for gi in range(len(POOL_WINDOWS)):
        for j in range(NCHIP):
            wps[gi, j * rb:(j + 1) * rb, :] = wp_ref[j, gi]


def _pool_fwd(x, g, w_pool, scale):
    s = x.shape[0]
    ts = min(TS_POOL, s)
    n = s // ts
    ng = len(POOL_WINDOWS)

    def body(x_ref, g_ref, wp_ref, sc_ref, xo_ref, hbuf, pbuf, wps):
        i = pl.program_id(0)

        @pl.when(i == 0)
        def _():
            hbuf[0:HALO_P, :] = jnp.zeros((HALO_P, D), F32)
            _assemble_wpool(wp_ref, wps)

        xv = x_ref[...]
        hbuf[HALO_P:HALO_P + ts, :] = xv * _rsqrt_mean_sq(xv) * g_ref[...]
        _pool_windows(hbuf, pbuf, i * ts, ts)
        hbuf[0:HALO_P, :] = hbuf[ts:ts + HALO_P, :]
        for gi in range(ng):
            cols = slice(gi * PG, (gi + 1) * PG)
            y = jnp.dot(pbuf[:, cols], wps[gi], preferred_element_type=F32)
            xo_ref[:, cols] = xv[:, cols] + y * sc_ref[:, cols]

    tile = pl.BlockSpec((ts, D), lambda i: (i, 0))
    return pl.pallas_call(
        body, name="pool_fwd", grid=(n,),
        in_specs=[tile, _const((1, D)), _const((NCHIP, ng, PG // NCHIP, PG)), _const((1, D))],
        out_specs=tile, out_shape=jax.ShapeDtypeStruct((s, D), F32),
        scratch_shapes=[pltpu.VMEM((HALO_P + ts, D), F32), pltpu.VMEM((ts, D), BF16), pltpu.VMEM((ng, PG, PG), BF16)],
        compiler_params=_cparams(1),
    )(x, g, w_pool, scale)


def _ffn_bwd_a(dxo, u, u0, wc, w_down, name, comm=None):
    s = dxo.shape[0]
    ts = min(TS_FFN, s)
    n = s // ts
    cw = FF2 // NCHIP
    rows = 32
    lw2 = 2 * LANES
    boffs = [K_S - 1 - k for k in range(K_S)]

    def body(dxo_ref, u_ref, u0_ref, wc_ref, wdn_ref, du0_ref, dwc_ref, dubuf, dact, dwacc):
        i = pl.program_id(0)

        @pl.when(i == 0)
        def _():
            dubuf[ts:ts + HALO_S, :] = jnp.zeros((HALO_S, FF2), F32)
            dwacc[...] = jnp.zeros(dwacc.shape, F32)

        df = dxo_ref[...].astype(BF16)
        for cg in range(0, FF, cw):
            dact[...] = lax.dot_general(df, wdn_ref[cg:cg + cw, :], NT_DIMS, preferred_element_type=F32)

            def chunk(ci, carry, cg=cg):
                rs = pl.ds(pl.multiple_of(ci * rows, rows), rows)
                for c in range(cg, cg + cw, LANES):
                    ug, uv = u_ref[rs, c:c + LANES], u_ref[rs, FF + c:FF + c + LANES]
                    sg = _sigmoid(ug)
                    da = dact[rs, c - cg:c - cg + LANES]
                    gs = ug * sg
                    dubuf[rs, c:c + LANES] = da * uv * (sg + gs * (1.0 - sg))
                    dubuf[rs, FF + c:FF + c + LANES] = da * gs
                return carry

            lax.fori_loop(0, ts // rows, chunk, 0)

        def chunk2(ci, carry):
            r0 = pl.multiple_of(ci * rows, rows)
            rs = pl.ds(r0, rows)
            for c in range(0, FF2, lw2):
                sh = _shifted(dubuf, r0, rows, c, lw2, boffs)
                du0_ref[rs, c:c + lw2] = _taps(wc_ref, sh, boffs, c, lw2).astype(BF16)
                u0v = u0_ref[rs, c:c + lw2].astype(F32)
                for k, o in enumerate(boffs):
                    dwacc[SUBLANES * k:SUBLANES * (k + 1), c:c + lw2] += _rowsum8(sh[o] * u0v)
            return carry

        lax.fori_loop(0, ts // rows, chunk2, 0)
        dubuf[ts:ts + HALO_S, :] = dubuf[0:HALO_S, :]

        @pl.when(i == n - 1)
        def _():
            _finish_tap_sums(dwacc, dwc_ref, K_S)

    rev = lambda w: pl.BlockSpec((ts, w), lambda i: (n - 1 - i, 0))
    return _call(
        body, name=name, grid=(n,), comm=comm, args=(dxo, u, u0, wc, w_down),
        in_specs=[rev(D), rev(FF2), rev(FF2), _const((K_S, FF2)), _const((FF, D))],
        out_specs=[rev(FF2), pl.BlockSpec((K_S, FF2), lambda i: (0, 0))],
        out_shape=[jax.ShapeDtypeStruct((s, FF2), BF16), jax.ShapeDtypeStruct((K_S, FF2), F32)],
        scratch=[pltpu.VMEM((ts + HALO_S, FF2), F32), pltpu.VMEM((ts, cw), F32), pltpu.VMEM((SUBLANES * K_S, FF2), F32)])


def _nt_rms_bwd(dy, w, x, g, dres, name, comm=None, tiles=None, dx_so_far=None):
    s = x.shape[0]
    ts = min(TS_MM, s)
    first, n = (0, s // ts) if tiles is None else tiles
    nw = dy.shape[1]
    bw = nw // NCHIP

    def body(dy_ref, w_ref, x_ref, g_ref, dres_ref, dx_ref, dg_ref):
        i = pl.program_id(0)

        @pl.when(i == 0)
        def _():
            dg_ref[...] = jnp.zeros((1, D), F32)

        dh = lax.dot_general(dy_ref[:, 0:bw], w_ref[0], NT_DIMS, preferred_element_type=F32)
        for j in range(1, NCHIP):
            dh = dh + lax.dot_general(dy_ref[:, j * bw:(j + 1) * bw], w_ref[j], NT_DIMS, preferred_element_type=F32)
        xv = x_ref[...]
        r = _rsqrt_mean_sq(xv)
        xh = xv * r
        dg_ref[...] += jnp.sum(dh * xh, axis=0, keepdims=True)
        dx_ref[...] = dres_ref[...] + _rms_bwd(dh, xh, r, g_ref[...])

    def body_with_alias(dy_ref, w_ref, x_ref, g_ref, dres_ref, _, dx_ref, dg_ref):
        body(dy_ref, w_ref, x_ref, g_ref, dres_ref, dx_ref, dg_ref)

    tile = lambda wd: pl.BlockSpec((ts, wd), lambda i: (first + i, 0))
    in_specs = [tile(nw), _const((NCHIP, D, bw)), tile(D), _const((1, D)), tile(D)]
    more = dx_so_far is not None
    return _call(
        body_with_alias if more else body, name=name, grid=(n,), comm=comm,
        args=(dy, w, x, g, dres) + ((dx_so_far,) if more else ()), in_specs=in_specs + [ANY] * more,
        out_specs=[tile(D), pl.BlockSpec((1, D), lambda i: (0, 0))], aliases={5: 0} if more else None,
        out_shape=[jax.ShapeDtypeStruct((s, D), F32), jax.ShapeDtypeStruct((1, D), F32)])


def _wgrad(a, b, n_blocks, name, comm=None):
    s, m = a.shape
    bw = b.shape[1] // n_blocks
    tk = min(TS_WGRAD, s)
    group = min(g for g in range(1, n_blocks + 1) if n_blocks % g == 0 and (g * bw >= m or g == n_blocks))

    def body(a_ref, b_ref, o_ref):
        part = lambda j: lax.dot_general(a_ref[...], b_ref[:, j * bw:(j + 1) * bw].astype(BF16), TN_DIMS,
                                         preferred_element_type=F32)

        @pl.when(pl.program_id(1) == 0)
        def _():
            for j in range(group):
                o_ref[j] = part(j)

        @pl.when(pl.program_id(1) != 0)
        def _():
            for j in range(group):
                o_ref[j] += part(j)

    (out,), comm_out = _call(
        body, name=name, grid=(n_blocks // group, s // tk), comm=comm, args=(a, b),
        in_specs=[pl.BlockSpec((tk, m), lambda j, k: (k, 0)), pl.BlockSpec((tk, group * bw), lambda j, k: (k, j))],
        out_specs=[pl.BlockSpec((group, m, bw), lambda j, k: (j, 0, 0))],
        out_shape=[jax.ShapeDtypeStruct((n_blocks, m, bw), F32)])
    return out, comm_out


def _pool_bwd(dxo, x, g, w_pool, scale, comm=None):
    s = x.shape[0]
    ts = min(TS_POOL, s)
    n = s // ts
    ng = len(POOL_WINDOWS)
    rb = PG // NCHIP

    def body(dxo_ref, x_ref, halo_ref, g_ref, wp_ref, sc_ref, dx_ref, dwp_ref, dsc_ref, dg_ref,
             hbuf, pbuf, qbuf, dhbuf, wps, dwacc):
        i = pl.program_id(0)
        j = n - 1 - i

        @pl.when(i == 0)
        def _():
            qbuf[ts:ts + HALO_P, :] = jnp.zeros((HALO_P, D), F32)
            dwacc[...] = jnp.zeros(dwacc.shape, F32)
            dsc_ref[...] = jnp.zeros((1, D), F32)
            dg_ref[...] = jnp.zeros((1, D), F32)
            _assemble_wpool(wp_ref, wps)

        gv = g_ref[...]
        xl = halo_ref[...]
        hbuf[0:HALO_P, :] = jnp.where(j == 0, 0.0, xl * _rsqrt_mean_sq(xl) * gv)
        xv = x_ref[...]
        r = _rsqrt_mean_sq(xv)
        xh = xv * r
        hbuf[HALO_P:HALO_P + ts, :] = xh * gv
        _pool_windows(hbuf, pbuf, j * ts, ts)

        dy = dxo_ref[...]
        t1 = (j * ts + lax.broadcasted_iota(jnp.int32, (ts, 1), 0) + 1).astype(F32)
        for gi, w in enumerate(POOL_WINDOWS):
            cols = slice(gi * PG, (gi + 1) * PG)
            p = pbuf[:, cols]
            y = jnp.dot(p, wps[gi], preferred_element_type=F32)
            dsc_ref[:, cols] += jnp.sum(dy[:, cols] * y, axis=0, keepdims=True)
            dq = (dy[:, cols] * sc_ref[:, cols]).astype(BF16)
            dwacc[gi] += lax.dot_general(p, dq, TN_DIMS, preferred_element_type=F32)
            dp = lax.dot_general(dq, wps[gi], NT_DIMS, preferred_element_type=F32)
            qbuf[0:ts, cols] = dp / jnp.minimum(t1, float(w))

        def chunk(ci, carry):
            r0 = pl.multiple_of(ci * R_CHUNK, R_CHUNK)
            tc = (j * ts + r0 + lax.broadcasted_iota(jnp.int32, (R_CHUNK, 1), 0) + 1).astype(F32)
            for gi, w in enumerate(POOL_WINDOWS):
                cnt = jnp.minimum(tc, float(w))
                offs = list(range(w))
                for c0 in range(gi * PG, (gi + 1) * PG, LANES):
                    sh = _shifted(qbuf, r0, R_CHUNK, c0, LANES, offs)
                    tot = sh[0]
                    for o in offs[1:]:
                        tot = tot + sh[o]
                    dhbuf[pl.ds(r0, R_CHUNK), c0:c0 + LANES] = tot - sh[0] * cnt
            return carry

        lax.fori_loop(0, ts // R_CHUNK, chunk, 0)
        qbuf[ts:ts + HALO_P, :] = qbuf[0:HALO_P, :]
        dh = dhbuf[...]
        dg_ref[...] += jnp.sum(dh * xh, axis=0, keepdims=True)
        dx_ref[...] = dy + _rms_bwd(dh, xh, r, gv)

        @pl.when(i == n - 1)
        def _():
            for gi in range(ng):
                for jj in range(NCHIP):
                    dwp_ref[jj, gi] = dwacc[gi, jj * rb:(jj + 1) * rb, :]

    rev = pl.BlockSpec((ts, D), lambda i: (n - 1 - i, 0))
    halo = pl.BlockSpec((HALO_P, D), lambda i: (jnp.maximum((n - 1 - i) * (ts // HALO_P) - 1, 0), 0))
    vec = pl.BlockSpec((1, D), lambda i: (0, 0))
    return _call(
        body, name="pool_bwd", grid=(n,), comm=comm, args=(dxo, x, x, g, w_pool, scale),
        in_specs=[rev, rev, halo, _const((1, D)), _const((NCHIP, ng, rb, PG)), _const((1, D))],
        out_specs=[rev, pl.BlockSpec((NCHIP, ng, rb, PG), lambda i: (0, 0, 0, 0)), vec, vec],
        out_shape=[jax.ShapeDtypeStruct((s, D), F32), jax.ShapeDtypeStruct((NCHIP, ng, rb, PG), F32),
                   jax.ShapeDtypeStruct((1, D), F32), jax.ShapeDtypeStruct((1, D), F32)],
        scratch=[pltpu.VMEM((HALO_P + ts, D), F32), pltpu.VMEM((ts, D), BF16), pltpu.VMEM((ts + HALO_P, D), F32),
                 pltpu.VMEM((ts, D), F32), pltpu.VMEM((ng, PG, PG), BF16), pltpu.VMEM((ng, PG, PG), F32)])


def _mix0_bwd_a(dx1, z, ac, bconv, w_out, conv_a, ln_g, ln_b, conv_b, comm=None):
    s = dx1.shape[0]
    ts = min(TS_MIXB, s)
    n = s // ts
    rows = 32
    boffs_a = [K_A - 1 - k for k in range(K_A)]
    boffs_s = [K_S - 1 - k for k in range(K_S)]

    def body(dx1_ref, z_ref, ac_ref, bconv_ref, wout_ref, ca_ref, lg_ref, lb_ref, cb_ref,
             dz_ref, dca_ref, dlg_ref, dlb_ref, dcb_ref, dac_buf, dbc_buf, dca_acc, dcb_acc):
        i = pl.program_id(0)

        @pl.when(i == 0)
        def _():
            dac_buf[ts:ts + HALO_A, :] = jnp.zeros((HALO_A, A), F32)
            dbc_buf[ts:ts + HALO_S, :] = jnp.zeros((HALO_S, A), F32)
            dca_acc[...] = jnp.zeros(dca_acc.shape, F32)
            dcb_acc[...] = jnp.zeros(dcb_acc.shape, F32)
            dlg_ref[...] = jnp.zeros((1, A), F32)
            dlb_ref[...] = jnp.zeros((1, A), F32)

        dcat = lax.dot_general(dx1_ref[...].astype(BF16), wout_ref[...], NT_DIMS, preferred_element_type=F32)
        db = dcat[:, A:2 * A]
        dz_ref[:, 2 * A:3 * A] = (db * bconv_ref[...]).astype(BF16)
        dbc_buf[0:ts, :] = db * z_ref[:, 2 * A:3 * A]
        ac = ac_ref[...]
        xc = ac - jnp.mean(ac, axis=-1, keepdims=True)
        rstd = lax.rsqrt(jnp.mean(xc * xc, axis=-1, keepdims=True) + LN_EPS)
        xn = xc * rstd
        lg = lg_ref[...]
        ln = xn * lg + lb_ref[...]
        sl = _sigmoid(ln)
        dln = dcat[:, 0:A] * sl * (1.0 + ln * (1.0 - sl))
        dlg_ref[...] += jnp.sum(dln * xn, axis=0, keepdims=True)
        dlb_ref[...] += jnp.sum(dln, axis=0, keepdims=True)
        dxn = dln * lg
        dac_buf[0:ts, :] = rstd * (dxn - jnp.mean(dxn, axis=-1, keepdims=True)
                                   - xn * jnp.mean(dxn * xn, axis=-1, keepdims=True))

        def chunk(ci, carry):
            r0 = pl.multiple_of(ci * rows, rows)
            rs = pl.ds(r0, rows)
            for c0 in range(0, A, LANES):
                col = lambda grp: slice(grp * A + c0, grp * A + c0 + LANES)
                a_val, sg = z_ref[rs, col(0)], _sigmoid(z_ref[rs, col(1)])
                dglu = _conv_corr(dac_buf, ca_ref, a_val * sg, dca_acc, r0, rows, c0, LANES, boffs_a)
                dz_ref[rs, col(0)] = (dglu * sg).astype(BF16)
                dz_ref[rs, col(1)] = (dglu * a_val * sg * (1.0 - sg)).astype(BF16)
                c_gate, bc_val = z_ref[rs, col(3)], z_ref[rs, col(4)]
                dcv = _conv_corr(dbc_buf, cb_ref, c_gate * bc_val, dcb_acc, r0, rows, c0, LANES, boffs_s)
                dz_ref[rs, col(3)] = (dcv * bc_val).astype(BF16)
                dz_ref[rs, col(4)] = (dcv * c_gate).astype(BF16)
            return carry

        lax.fori_loop(0, ts // rows, chunk, 0)
        dac_buf[ts:ts + HALO_A, :] = dac_buf[0:HALO_A, :]
        dbc_buf[ts:ts + HALO_S, :] = dbc_buf[0:HALO_S, :]

        @pl.when(i == n - 1)
        def _():
            _finish_tap_sums(dca_acc, dca_ref, K_A)
            _finish_tap_sums(dcb_acc, dcb_ref, K_S)

    rev = lambda w: pl.BlockSpec((ts, w), lambda i: (n - 1 - i, 0))
    full = lambda r, c: pl.BlockSpec((r, c), lambda i: (0, 0))
    return _call(
        body, name="mix0_bwd_a", grid=(n,), comm=comm, args=(dx1, z, ac, bconv, w_out, conv_a, ln_g, ln_b, conv_b),
        in_specs=[rev(D), rev(NZ), rev(A), rev(A), _const((2 * A, D)), _const((K_A, A)), _const((1, A)), _const((1, A)),
                  _const((K_S, A))],
        out_specs=[rev(NZ), full(K_A, A), full(1, A), full(1, A), full(K_S, A)],
        out_shape=[jax.ShapeDtypeStruct((s, NZ), BF16), jax.ShapeDtypeStruct((K_A, A), F32), jax.ShapeDtypeStruct((1, A), F32),
                   jax.ShapeDtypeStruct((1, A), F32), jax.ShapeDtypeStruct((K_S, A), F32)],
        scratch=[pltpu.VMEM((ts + HALO_A, A), F32), pltpu.VMEM((ts + HALO_S, A), F32), pltpu.VMEM((SUBLANES * K_A, A), F32),
                 pltpu.VMEM((SUBLANES * K_S, A), F32)])


def _cast_later_weights(w_up, w_down, w_pool, comm):
    nl = w_up.shape[0]

    def body(up_ref, dn_ref, pool_ref, *outs):
        up_o, dn_o, pool_o = outs[:nl], outs[nl:2 * nl], outs[2 * nl]
        for layer in range(nl):
            @pl.when(pl.program_id(0) == layer)
            def _(layer=layer):
                up_o[layer][...] = up_ref[...].astype(BF16)
                dn_o[layer][...] = dn_ref[...].astype(BF16)
                if layer == 0:
                    pool_o[...] = pool_ref[...].astype(BF16)

    per_layer = lambda a: pl.BlockSpec((None,) + a.shape[1:], lambda l: (l,) + (0,) * (a.ndim - 1))
    whole = lambda shape: pl.BlockSpec(shape, lambda l: (0,) * len(shape))
    out_shapes = [w_up.shape[1:]] * nl + [w_down.shape[1:]] * nl + [w_pool.shape]
    return _call(body, name="cast_later_weights", grid=(nl,), comm=comm, args=(w_up, w_down, w_pool),
                 in_specs=[per_layer(w_up), per_layer(w_down), whole(w_pool.shape)],
                 out_specs=[whole(shape) for shape in out_shapes],
                 out_shape=[jax.ShapeDtypeStruct(shape, BF16) for shape in out_shapes])


def _train_step(x, target, place, shard, later, rep, small_pack, small_shapes, adamw):
    bw_up, bw_in = FF2 // NCHIP, NZ // NCHIP
    five = lambda a, k: a.reshape(NCHIP, 2, *HALF[k])

    half = lambda a, name: a.reshape((2,) + HALF[name])
    (up0, up1, dn0, dn1, pool_bf), (g_in, g_out, small_g) = _cast_later_weights(
        *later, comm=_gather_comm([shard["w_in"], shard["w_out"]], small_pack))
    shard = dict(shard, w_pool=half(pool_bf, "w_pool"), w_up0=half(up0, "w_up"), w_up1=half(up1, "w_up"),
                 w_down0=half(dn0, "w_down"), w_down1=half(dn1, "w_down"))
    whole = {}
    for k, part in zip(SMALL_SHARDED, _unpack(small_g, small_shapes, lead=(NCHIP,))):
        whole[k] = jnp.moveaxis(part, 0, 1).reshape(part.shape[1], NCHIP * part.shape[2])
    w_in, w_out = g_in.reshape(NCHIP, D, bw_in), g_out.reshape(2 * A, D)
    conv_ffn = whole["conv_ffn_w"].reshape(2, K_S, FF2)
    nffn = [rep["norm_ffn"][0:1], rep["norm_ffn"][1:2]]

    (h0, z, ac, bconv, cat, x1), (g_up0, g_dn0) = _mix0_fwd(
        x, rep["norm_mix_even"], w_in, whole["conv_a"], rep["ln_a_g"], rep["ln_a_b"], whole["conv_b"], w_out,
        comm=_gather_comm([shard["w_up0"], shard["w_down0"]], mid_steps=GATHER_MID_STEPS["mix0_fwd"]))
    w_up0, w_dn0 = g_up0.reshape(NCHIP, D, bw_up), g_dn0.reshape(FF, D)
    (hf0, u00, u0, act0, x2), (g_pool, g_up1, g_dn1) = _ffn_fwd(
        x1, nffn[0], w_up0, conv_ffn[0], w_dn0, "ffn0_fwd",
        comm=_gather_comm([shard["w_pool"], shard["w_up1"], shard["w_down1"]], mid_steps=GATHER_MID_STEPS["ffn0_fwd"]))
    w_pool = g_pool.reshape(NCHIP, len(POOL_WINDOWS), PG // NCHIP, PG)
    w_up1, w_dn1 = g_up1.reshape(NCHIP, D, bw_up), g_dn1.reshape(FF, D)
    x3 = _pool_fwd(x2, whole["norm_mix_odd"], w_pool, whole["pool_scale"])
    (hf1, u01, u1, act1, dx4, g_nfin, loss_part), _ = _ffn_fwd(x3, nffn[1], w_up1, conv_ffn[1], w_dn1, "ffn1_fwd",
                                                                head=(rep["norm_final"], target))

    psum = lambda k, g, ld, tag="": _pair_sum(place, g, ld, "pair_sum_" + k + tag)
    tot = lambda k, g, ld, p, layer=0, nl=1, prev=None: _chip_sum(place, g, ld, p, layer, nl, "chip_sum_%s%d" % (k, layer), prev)
    pair, chips, join = _pair_comm, _chips_comm, _join_comm

    gr_dn1 = five(_wgrad(act1, dx4, 1, "wgrad_down1")[0], "w_down")
    (du01, g_wc1), _ = _ffn_bwd_a(dx4, u1, u01, conv_ffn[1], w_dn1, "ffn1_bwd_a")
    g_up, (ld_dn1,) = _wgrad(hf1, du01, NCHIP, "wgrad_up1", comm=pair([gr_dn1]))
    gr_up1 = five(g_up, "w_up")
    s_dn1 = psum("w_down", gr_dn1, ld_dn1, "1")
    (dx3, g_nf1), (p_dn1, ld_up1) = _nt_rms_bwd(du01, w_up1, x3, nffn[1], dx4, "ffn1_bwd_b",
                                                comm=join(chips([s_dn1]), pair([gr_up1])))
    s_up1 = psum("w_up", gr_up1, ld_up1, "1")
    (dx2, g_wpool, g_scale, g_nmo), _ = _pool_bwd(dx3, x2, whole["norm_mix_odd"], w_pool, whole["pool_scale"])
    gr_pool = five(g_wpool, "w_pool")
    (du00, g_wc0), (p_up1,) = _ffn_bwd_a(dx2, u0, u00, conv_ffn[0], w_dn0, "ffn0_bwd_a", comm=chips([s_up1]))
    gr_dn0 = five(_wgrad(act0, dx2, 1, "wgrad_down0")[0], "w_down")
    g_up, (ld_dn0, ld_pool) = _wgrad(hf0, du00, NCHIP, "wgrad_up0", comm=pair([gr_dn0, gr_pool]))
    gr_up0 = five(g_up, "w_up")
    s_dn0, s_pool = psum("w_down", gr_dn0, ld_dn0, "0"), psum("w_pool", gr_pool, ld_pool)
    (dx1, g_nf0), (p_dn0, p_pool, ld_up0) = _nt_rms_bwd(du00, w_up0, x1, nffn[0], dx2, "ffn0_bwd_b",
                                                        comm=join(chips([s_dn0, s_pool]), pair([gr_up0])))
    s_up0 = psum("w_up", gr_up0, ld_up0, "0")
    gr_out = five(_wgrad(cat, dx1, 1, "wgrad_out")[0], "w_out")
    (dz, g_ca, g_lg, g_lb, g_cb), (p_up0, ld_out) = _mix0_bwd_a(
        dx1, z, ac, bconv, w_out, whole["conv_a"], rep["ln_a_g"], rep["ln_a_b"], whole["conv_b"],
        comm=join(chips([s_up0]), pair([gr_out])))
    s_out = psum("w_out", gr_out, ld_out)
    t_pool = tot("w_pool", gr_pool, ld_pool, p_pool)
    t_up = tot("w_up", gr_up0, ld_up0, p_up0, 0, 2, tot("w_up", gr_up1, ld_up1, p_up1, 1, 2))
    t_dn = tot("w_down", gr_dn0, ld_dn0, p_dn0, 0, 2, tot("w_down", gr_dn1, ld_dn1, p_dn1, 1, 2))
    small = {"norm_mix_odd": g_nmo, "pool_scale": g_scale, "norm_ffn": jnp.concatenate([g_nf0, g_nf1], axis=0),
             "conv_ffn_w": jnp.stack([g_wc0, g_wc1]), "norm_final": g_nfin, "loss": loss_part,
             "conv_a": g_ca, "ln_a_g": g_lg, "ln_a_b": g_lb, "conv_b": g_cb}
    g_in, (p_out, early_all, t_pool, t_up, t_dn) = _wgrad(
        h0, dz, NCHIP, "wgrad_in",
        comm=join(chips([s_out], _pack([small[k] for k in SMALL_EARLY])), _swap_comm([t_pool, t_up, t_dn])))
    gr_in = five(g_in, "w_in")
    n_mm = x.shape[0] // min(TS_MM, x.shape[0])
    n_a = max(1, n_mm // 4)
    last_bwd = functools.partial(_nt_rms_bwd, dz, w_in, x, rep["norm_mix_even"], dx1)
    (dx_a, g_nme_a), (ld_in,) = last_bwd("mix0_bwd_b0", comm=pair([gr_in]), tiles=(0, n_a))
    s_in = psum("w_in", gr_in, ld_in)
    (grad_x, g_nme_b), (p_in,) = last_bwd("mix0_bwd_b1", comm=chips([s_in]), tiles=(n_a, n_mm - n_a), dx_so_far=dx_a)
    small["norm_mix_even"] = g_nme_a + g_nme_b
    (late_all,) = _run_comm(chips([], _pack([small[k] for k in SMALL_LATE])), "rs_last")
    t_in, t_out = _run_comm(_swap_comm([tot("w_in", gr_in, ld_in, p_in), tot("w_out", gr_out, ld_out, p_out)]), "rs_swap")
    done = {k: adamw(k, t) for k, t in (("w_up", t_up), ("w_down", t_dn), ("w_pool", t_pool), ("w_in", t_in), ("w_out", t_out))}
    summed = dict(zip(SMALL_EARLY, _unpack(_sum_devices(early_all, "sum_small_early"), [small[k].shape for k in SMALL_EARLY])))
    summed.update(zip(SMALL_LATE, _unpack(_sum_devices(late_all, "sum_small_late"), [small[k].shape for k in SMALL_LATE])))
    return grad_x, done, summed


def _adamw_math(w, g, m, v):
    m = ADAM_B1 * m + (1.0 - ADAM_B1) * g
    v = ADAM_B2 * v + (1.0 - ADAM_B2) * (g * g)
    m_hat = m / (1.0 - ADAM_B1 ** ADAM_STEP)
    v_hat = v / (1.0 - ADAM_B2 ** ADAM_STEP)
    return -ADAM_LR * (m_hat / (jnp.sqrt(v_hat) + ADAM_EPS) + ADAM_WD * w), m, v


def _adamw_big(w, g, m, v, tr, name, comm=None):
    nl, rows, cols = w.shape

    def body(w_ref, g_ref, m_ref, v_ref, g2_ref, d_ref, m2_ref, v2_ref):
        gv = g_ref[...]
        g2_ref[...] = gv
        d_ref[...], m2_ref[...], v2_ref[...] = _adamw_math(w_ref[...], gv, m_ref[...], v_ref[...])

    spec = pl.BlockSpec((None, tr, cols), lambda l, r: (l, r, 0))
    return _call(body, name=name, grid=(nl, rows // tr), comm=comm, args=(w, g, m, v), in_specs=[spec] * 4,
                 out_specs=[spec] * 4, out_shape=[jax.ShapeDtypeStruct(w.shape, F32)] * 4)


def _adamw_small(ws, gs, ms, vs):
    n = len(ws)

    def body(*refs):
        for p in range(n):
            w_ref, g_ref, m_ref, v_ref = (refs[q * n + p] for q in range(4))
            d_ref, m2_ref, v2_ref = (refs[(4 + q) * n + p] for q in range(3))
            d_ref[...], m2_ref[...], v2_ref[...] = _adamw_math(w_ref[...], g_ref[...], m_ref[...], v_ref[...])

    whole = lambda a: pl.BlockSpec(a.shape, lambda: (0,) * a.ndim)
    outs = pl.pallas_call(
        body, name="adamw_small", in_specs=[whole(a) for a in ws] * 4, out_specs=[whole(a) for a in ws] * 3,
        out_shape=[jax.ShapeDtypeStruct(a.shape, F32) for a in ws] * 3,
        compiler_params=pltpu.CompilerParams(vmem_limit_bytes=VMEM_LIMIT_BYTES),
    )(*ws, *gs, *ms, *vs)
    return outs[0:n], outs[n:2 * n], outs[2 * n:3 * n]


def _place():
    x, y, c = lax.axis_index("x"), lax.axis_index("y"), lax.axis_index("c")
    chips = [(x, 1 - y), (1 - x, y), (1 - x, 1 - y)]
    blocks = [2 * cx + cy for cx, cy in chips]
    return x, y, c, 2 * x + y, chips, blocks


def _gather_comm(shards, small=None, mid_steps=None):
    na = len(shards)
    ns = 0 if small is None else 1

    def copies(ins, outs, sems):
        h1_send, h1_recv, h2_send, h2_recv, f1_send, f1_recv, f2_send, f2_recv, own_send, own_recv = sems[:10]
        x, y, c, j, chips, blocks = _place()
        sib = (x, y, 1 - c)
        piece = lambda a, p: pl.ds(p * (shards[a].shape[1] // 2), shards[a].shape[1] // 2)

        def remote(src, dst, send, recv, q, to):
            return pltpu.make_async_remote_copy(src_ref=src, dst_ref=dst, send_sem=send.at[q], recv_sem=recv.at[q],
                                                device_id=to, device_id_type=MESH)

        def hop1(a, k, arrival):
            dst = outs[a].at[blocks[k], c] if arrival else outs[a].at[j, c]
            return remote(dst if arrival else ins[a].at[c], dst, h1_send, h1_recv, 2 * a + k, (*chips[k], c))

        def hop2(a, p, arrival):
            ref = outs[a].at[blocks[2] if arrival else blocks[p], c, piece(a, p)]
            return remote(ref, ref, h2_send, h2_recv, 2 * a + p, (*chips[1 - p], c))

        def fwd1(a, k, half):
            ref = outs[a].at[blocks[k], half]
            return remote(ref, ref, f1_send, f1_recv, 2 * a + k, sib)

        def fwd2(a, p, half):
            ref = outs[a].at[blocks[2], half, piece(a, p)]
            return remote(ref, ref, f2_send, f2_recv, 2 * a + p, sib)

        own = [remote(ins[a], outs[a].at[j], own_send, own_recv, a, sib) for a in range(na)]
        small_copies = [remote(ins[na], outs[na].at[j], sems[10], sems[11], k, (*chips[k], c)) for k in range(3 * ns)]
        local = [pltpu.make_async_copy(ins[na], outs[na].at[j], sems[12])] if ns else []
        return hop1, hop2, fwd1, fwd2, own, small_copies, local, c

    pairs = [(a, k) for a in range(na) for k in range(2)]

    def start(ins, outs, sems):
        hop1, _, _, _, own, small_copies, local, _ = copies(ins, outs, sems)
        for cp in local + own + [hop1(a, k, False) for a, k in pairs] + small_copies:
            cp.start()

    def mid(ins, outs, sems):
        hop1, hop2, fwd1, _, _, _, _, c = copies(ins, outs, sems)
        for a, k in pairs:
            hop1(a, k, True).wait_recv()
            hop2(a, k, False).start()
            fwd1(a, k, c).start()

    def finish(ins, outs, sems):
        hop1, hop2, fwd1, fwd2, own, small_copies, local, c = copies(ins, outs, sems)
        for a, p in pairs:
            hop2(a, p, True).wait_recv()
            fwd2(a, p, c).start()
        for cp in small_copies:
            cp.wait()
        for a, k in pairs:
            hop1(a, k, False).wait_send()
            hop2(a, k, False).wait_send()
            fwd1(a, k, c).wait_send()
            fwd1(a, k, 1 - c).wait_recv()
            fwd2(a, k, c).wait_send()
            fwd2(a, k, 1 - c).wait_recv()
        for cp in own + local:
            cp.wait()

    out_shapes = [jax.ShapeDtypeStruct((NCHIP,) + s.shape, s.dtype) for s in shards]
    sems = [pltpu.SemaphoreType.DMA((2 * na,))] * 8 + [pltpu.SemaphoreType.DMA((na,))] * 2
    if ns:
        out_shapes.append(jax.ShapeDtypeStruct((NCHIP,) + small.shape, small.dtype))
        sems += [pltpu.SemaphoreType.DMA((3,)), pltpu.SemaphoreType.DMA((3,)), pltpu.SemaphoreType.DMA]
    if mid_steps is not None:
        return _Comm(list(shards) + [small] * ns, out_shapes, sems, start, finish, mid=mid, mid_steps=mid_steps)

    def forward_and_finish(ins, outs, sems):
        mid(ins, outs, sems)
        finish(ins, outs, sems)

    return _Comm(list(shards) + [small] * ns, out_shapes, sems, start, forward_and_finish)


def _simple_comm(inputs, out_shapes, make_copies, n_sems, aliases=None):
    def start(ins, outs, sems):
        for cp in make_copies(ins, outs, sems):
            cp.start()

    def finish(ins, outs, sems):
        for cp in make_copies(ins, outs, sems):
            cp.wait()

    return _Comm(inputs, out_shapes, [pltpu.SemaphoreType.DMA((n,)) for n in n_sems], start, finish, aliases)


def _pair_comm(grads):
    def make_copies(ins, outs, sems):
        x, y, c, _, _, _ = _place()
        return [pltpu.make_async_remote_copy(
            src_ref=ins[a].at[:, 1 - c], dst_ref=outs[a], send_sem=sems[0].at[a], recv_sem=sems[1].at[a],
            device_id=(x, y, 1 - c), device_id_type=MESH) for a in range(len(grads))]

    out_shapes = [jax.ShapeDtypeStruct(g.shape[:1] + g.shape[2:], F32) for g in grads]
    return _simple_comm(grads, out_shapes, make_copies, [len(grads)] * 2)


def _chips_comm(sums, small=None):
    na = len(sums)
    nk = NCHIP - 1

    def make_copies(ins, outs, sems):
        x, y, c, _, chips, _ = _place()
        copies = [pltpu.make_async_remote_copy(
            src_ref=ins[a].at[k], dst_ref=outs[a].at[k], send_sem=sems[0].at[a * nk + k],
            recv_sem=sems[1].at[a * nk + k], device_id=(*chips[k], c), device_id_type=MESH)
            for a in range(na) for k in range(nk)]
        if small is not None:
            me = 4 * x + 2 * y + c
            for r in range(1, NDEV):
                peer = (1 - x if r & 4 else x, 1 - y if r & 2 else y, 1 - c if r & 1 else c)
                copies.append(pltpu.make_async_remote_copy(
                    src_ref=ins[na], dst_ref=outs[na].at[me], send_sem=sems[2].at[r - 1], recv_sem=sems[3].at[r - 1],
                    device_id=peer, device_id_type=MESH))
            copies.append(pltpu.make_async_copy(ins[na], outs[na].at[me], sems[4].at[0]))
        return copies

    out_shapes = [jax.ShapeDtypeStruct(g.shape, BF16) for g in sums]
    chip_sems = [max(na * nk, 1)] * 2
    if small is None:
        return _simple_comm(sums, out_shapes, make_copies, chip_sems)
    out_shapes.append(jax.ShapeDtypeStruct((NDEV,) + small.shape, F32))
    return _simple_comm(list(sums) + [small], out_shapes, make_copies, chip_sems + [NDEV - 1] * 2 + [1])


def _swap_comm(totals):
    def make_copies(ins, outs, sems):
        x, y, c, _, _, _ = _place()
        return [pltpu.make_async_remote_copy(
            src_ref=outs[a].at[:, c], dst_ref=outs[a].at[:, c], send_sem=sems[0].at[a], recv_sem=sems[1].at[a],
            device_id=(x, y, 1 - c), device_id_type=MESH) for a in range(len(totals))]

    out_shapes = [jax.ShapeDtypeStruct(t.shape, F32) for t in totals]
    return _simple_comm(totals, out_shapes, make_copies, [len(totals)] * 2, aliases={a: a for a in range(len(totals))})


def _pair_sum(place, grad, landed, name):
    _, _, rows, cols = grad.shape

    def body(place_ref, g_ref, l_ref, o_ref):
        o_ref[...] = (g_ref[...] + l_ref[...]).astype(BF16)

    other = lambda k, p: jnp.bitwise_xor(p[0], k + 1)
    return pl.pallas_call(
        body, name=name,
        grid_spec=pltpu.PrefetchScalarGridSpec(
            num_scalar_prefetch=1, grid=(NCHIP - 1,),
            in_specs=[pl.BlockSpec((None, None, rows, cols), lambda k, p: (other(k, p), p[1], 0, 0)),
                      pl.BlockSpec((None, rows, cols), lambda k, p: (other(k, p), 0, 0))],
            out_specs=pl.BlockSpec((None, rows, cols), lambda k, p: (k, 0, 0))),
        out_shape=jax.ShapeDtypeStruct((NCHIP - 1, rows, cols), BF16), compiler_params=_cparams(1),
    )(place, grad, landed)


def _chip_sum(place, grad, landed, parts, layer, n_layers, name, prev=None):
    _, _, rows, cols = grad.shape

    def body(*refs):
        g_ref, l_ref, p_ref, o_ref = refs[1], refs[2], refs[3], refs[-1]
        tot = g_ref[...] + l_ref[...]
        for k in range(NCHIP - 1):
            tot = tot + p_ref[k].astype(F32)
        o_ref[...] = tot

    in_specs = [pl.BlockSpec((None, None, rows, cols), lambda i, p: (p[0], p[1], 0, 0)),
                pl.BlockSpec((None, rows, cols), lambda i, p: (p[0], 0, 0)),
                pl.BlockSpec((NCHIP - 1, rows, cols), lambda i, p: (0, 0, 0))]
    args = [place, grad, landed, parts]
    if prev is not None:
        in_specs.append(ANY)
        args.append(prev)
    return pl.pallas_call(
        body, name=name,
        grid_spec=pltpu.PrefetchScalarGridSpec(
            num_scalar_prefetch=1, grid=(1,), in_specs=in_specs,
            out_specs=pl.BlockSpec((None, None, rows, cols), lambda i, p: (layer, p[1], 0, 0))),
        out_shape=jax.ShapeDtypeStruct((n_layers, 2, rows, cols), F32),
        input_output_aliases={} if prev is None else {4: 0}, compiler_params=_cparams(1),
    )(*args)


def _sum_devices(parts, name):
    def body(p_ref, o_ref):
        tot = p_ref[0]
        for d in range(1, NDEV):
            tot = tot + p_ref[d]
        o_ref[...] = tot

    return pl.pallas_call(
        body, name=name, in_specs=[pl.BlockSpec(parts.shape, lambda: (0, 0, 0))],
        out_specs=pl.BlockSpec(parts.shape[1:], lambda: (0, 0)), out_shape=jax.ShapeDtypeStruct(parts.shape[1:], F32),
    )(parts)


def _pack(parts):
    rows = []
    for p in parts:
        p = p.reshape(-1, LANES)
        rows.append(jnp.pad(p, ((0, -p.shape[0] % SUBLANES), (0, 0))))
    return jnp.concatenate(rows, axis=0)


def _unpack(buf, shapes, lead=()):
    out, r0 = [], 0
    nl = len(lead)
    for shp in shapes:
        nrow = 1
        for d in shp:
            nrow *= d
        nrow //= LANES
        out.append(buf[(slice(None),) * nl + (slice(r0, r0 + nrow),)].reshape(lead + tuple(shp)))
        r0 += nrow + (-nrow % SUBLANES)
    return out


WEIGHT_ORDER = ("norm_mix_even", "w_in", "conv_a", "ln_a_g", "ln_a_b", "conv_b", "w_out", "norm_mix_odd", "w_pool",
                "pool_scale", "norm_ffn", "w_up", "conv_ffn_w", "w_down", "norm_final")
BIG = ("w_in", "w_out", "w_pool", "w_up", "w_down")
HALF = {"w_in": (D // 2, NZ // NCHIP), "w_out": (2 * A // NCHIP // 2, D), "w_pool": (PG // 2, PG),
        "w_up": (D // 2, FF2 // NCHIP), "w_down": (FF // NCHIP // 2, D)}
SMALL_SHARDED = ("conv_a", "conv_b", "conv_ffn_w", "norm_mix_odd", "pool_scale")
SMALL_ALL = ("norm_mix_even", "conv_a", "ln_a_g", "ln_a_b", "conv_b", "norm_mix_odd", "pool_scale", "norm_ffn", "conv_ffn_w",
             "norm_final")
SMALL_EARLY = ("norm_mix_odd", "pool_scale", "norm_ffn", "conv_ffn_w", "norm_final", "loss", "conv_a", "ln_a_g", "ln_a_b", "conv_b")
SMALL_LATE = ("norm_mix_even",)


def kernel(x, norm_mix_even, w_in, conv_a, ln_a_g, ln_a_b, conv_b, w_out, norm_mix_odd, w_pool, pool_scale, norm_ffn, w_up, conv_ffn_w, w_down, norm_final, loss_target, m_norm_mix_even, m_w_in, m_conv_a, m_ln_a_g, m_ln_a_b, m_conv_b, m_w_out, m_norm_mix_odd, m_w_pool, m_pool_scale, m_norm_ffn, m_w_up, m_conv_ffn_w, m_w_down, m_norm_final, v_norm_mix_even, v_w_in, v_conv_a, v_ln_a_g, v_ln_a_b, v_conv_b, v_w_out, v_norm_mix_odd, v_w_pool, v_pool_scale, v_norm_ffn, v_w_up, v_conv_ffn_w, v_w_down, v_norm_final):
    w = dict(norm_mix_even=norm_mix_even, w_in=w_in, conv_a=conv_a, ln_a_g=ln_a_g, ln_a_b=ln_a_b, conv_b=conv_b, w_out=w_out,
             norm_mix_odd=norm_mix_odd, w_pool=w_pool, pool_scale=pool_scale, norm_ffn=norm_ffn, w_up=w_up,
             conv_ffn_w=conv_ffn_w, w_down=w_down, norm_final=norm_final)
    m = dict(norm_mix_even=m_norm_mix_even, w_in=m_w_in, conv_a=m_conv_a, ln_a_g=m_ln_a_g, ln_a_b=m_ln_a_b, conv_b=m_conv_b,
             w_out=m_w_out, norm_mix_odd=m_norm_mix_odd, w_pool=m_w_pool, pool_scale=m_pool_scale, norm_ffn=m_norm_ffn,
             w_up=m_w_up, conv_ffn_w=m_conv_ffn_w, w_down=m_w_down, norm_final=m_norm_final)
    v = dict(norm_mix_even=v_norm_mix_even, w_in=v_w_in, conv_a=v_conv_a, ln_a_g=v_ln_a_g, ln_a_b=v_ln_a_b, conv_b=v_conv_b,
             w_out=v_w_out, norm_mix_odd=v_norm_mix_odd, w_pool=v_w_pool, pool_scale=v_pool_scale, norm_ffn=v_norm_ffn,
             w_up=v_w_up, conv_ffn_w=v_conv_ffn_w, w_down=v_w_down, norm_final=v_norm_final)
    chip = 2 * lax.axis_index("x") + lax.axis_index("y")
    place = jnp.stack([chip, lax.axis_index("c")]).astype(jnp.int32)

    half = lambda a, name: a.astype(BF16).reshape((2,) + HALF[name])
    shard = {"w_in": half(w_in[0], "w_in"), "w_out": half(w_out[0], "w_out")}
    small_shapes = [w[k].shape[-2:] if w[k].ndim == 3 and k != "conv_ffn_w" else (w[k].size // w[k].shape[-1], w[k].shape[-1])
                    for k in SMALL_SHARDED]
    rep = dict(norm_mix_even=norm_mix_even, ln_a_g=ln_a_g, ln_a_b=ln_a_b, norm_ffn=norm_ffn, norm_final=norm_final.reshape(1, D))
    as3 = lambda a: a.reshape(a.shape[0], -1, a.shape[-1])

    def adamw(k, total):
        outs, _ = _adamw_big(as3(w[k]), as3(total), as3(m[k]), as3(v[k]), ADAMW_ROWS[k], "adamw_" + k)
        return [a.reshape(w[k].shape) for a in outs]

    grad_x, done, summed = _train_step(
        x[0], loss_target[0], place, shard, (w_up, w_down, w_pool[0]), rep, _pack([w[k] for k in SMALL_SHARDED]), small_shapes,
        adamw)
    loss = summed["loss"][0, 0]

    grad, delta, new_m, new_v = ({k: done[k][q] for k in BIG} for q in range(4))
    for k in SMALL_ALL:
        gsum = summed[k]
        if k in SMALL_SHARDED:
            cols = w[k].shape[-1]
            gsum = lax.dynamic_slice_in_dim(gsum, chip * cols, cols, axis=gsum.ndim - 1)
        grad[k] = gsum.reshape(w[k].shape)

    as2 = lambda a: a.reshape(1, -1) if a.ndim == 1 else a[0] if a.ndim == 3 and a.shape[0] == 1 else a
    ds, ms, vs = _adamw_small(*[[as2(t[k]) for k in SMALL_ALL] for t in (w, grad, m, v)])
    for k, d2, m2, v2 in zip(SMALL_ALL, ds, ms, vs):
        delta[k], new_m[k], new_v[k] = (a.reshape(w[k].shape) for a in (d2, m2, v2))

    return (loss, grad_x[None], *[grad[k] for k in WEIGHT_ORDER], *[delta[k] for k in WEIGHT_ORDER],
            *[new_m[k] for k in WEIGHT_ORDER], *[new_v[k] for k in WEIGHT_ORDER])
```

```python
import functools

import jax
import jax.numpy as jnp
from jax import lax
from jax.experimental import pallas as pl
from jax.experimental.pallas import tpu as pltpu

F32, BF16 = jnp.float32, jnp.bfloat16

D = 1024
A = 512
NZ = 5 * A
FF = 2816
FF2 = 2 * FF
NCHIP = 4
NDEV = 8
K_A, K_S = 31, 3
POOL_WINDOWS = (2, 4, 8, 16)
PG = D // len(POOL_WINDOWS)
RMS_EPS, LN_EPS = 1e-6, 1e-5
ADAM_LR, ADAM_B1, ADAM_B2, ADAM_EPS, ADAM_WD, ADAM_STEP = 0.001, 0.9, 0.999, 1e-08, 0.01, 10

HALO_A, HALO_S, HALO_P = 32, 8, 16
SUBLANES = 8
LANES = 128
VMEM_LIMIT_BYTES = 56 * 1024 * 1024

TS_MIX = 512
TS_MIXB = 512
TS_FFN = 256
TS_POOL = 512
TS_MM = 512
TS_WGRAD = 1024
CHIP_SUM_TILE_BYTES = 1024 * 1024
ADAMW_ROWS = {"w_in": 512, "w_out": 256, "w_pool": 256, "w_up": 256, "w_down": 352}
R_CHUNK = 64
GATHER_MID_STEPS = {"mix0_fwd": 2, "ffn0_fwd": 5}

MESH = pl.DeviceIdType.MESH
ANY = pl.BlockSpec(memory_space=pl.ANY)
NT_DIMS = (((1,), (1,)), ((), ()))
TN_DIMS = (((0,), (0,)), ((), ()))


def _cparams(n_axes):
    return pltpu.CompilerParams(dimension_semantics=("arbitrary",) * n_axes, vmem_limit_bytes=VMEM_LIMIT_BYTES)


def _const(shape):
    nd = len(shape)
    return pl.BlockSpec(shape, lambda *_: (0,) * nd, pipeline_mode=pl.Buffered(1))


def _sigmoid(v):
    return 1.0 / (1.0 + jnp.exp(-v))


def _rsqrt_mean_sq(x):
    return lax.rsqrt(jnp.mean(x * x, axis=-1, keepdims=True) + RMS_EPS)


def _rms_bwd(dh, xh, r, g):
    dxh = dh * g
    return r * (dxh - xh * jnp.mean(dxh * xh, axis=-1, keepdims=True))


def _shifted(buf_ref, row0, rows, col0, width, offsets):
    lo = (min(offsets) // SUBLANES) * SUBLANES
    hi = -(-(max(offsets) + rows) // SUBLANES) * SUBLANES
    start = row0 + lo if isinstance(row0, int) else pl.multiple_of(row0 + lo, SUBLANES)
    win = buf_ref[pl.ds(start, hi - lo), col0:col0 + width]
    out = {}
    for res in sorted({(o - lo) % SUBLANES for o in offsets}):
        qs = {o: (o - lo) // SUBLANES for o in offsets if (o - lo) % SUBLANES == res}
        base = pltpu.roll(win, hi - lo - res, 0) if res else win
        for o, q in qs.items():
            out[o] = base[SUBLANES * q:SUBLANES * q + rows, :]
    return out


def _rowsum8(v):
    acc = v[0:SUBLANES, :]
    for r in range(SUBLANES, v.shape[0], SUBLANES):
        acc = acc + v[r:r + SUBLANES, :]
    return acc


def _taps(w_ref, sh, offsets, col0, width):
    acc = None
    for k, o in enumerate(offsets):
        term = w_ref[k:k + 1, col0:col0 + width] * sh[o]
        acc = term if acc is None else acc + term
    return acc


def _window_bases(buf_ref, row0, rows, col0, width, offsets):
    lo = (min(offsets) // SUBLANES) * SUBLANES
    hi = -(-(max(offsets) + rows) // SUBLANES) * SUBLANES
    start = row0 + lo if isinstance(row0, int) else pl.multiple_of(row0 + lo, SUBLANES)
    win = buf_ref[pl.ds(start, hi - lo), col0:col0 + width]
    for res in sorted({(o - lo) % SUBLANES for o in offsets}):
        taps = [(k, (o - lo) // SUBLANES * SUBLANES) for k, o in enumerate(offsets) if (o - lo) % SUBLANES == res]
        yield (pltpu.roll(win, hi - lo - res, 0) if res else win), taps


def _conv_acc(buf_ref, w_ref, row0, rows, col0, width, offsets):
    acc = None
    for base, taps in _window_bases(buf_ref, row0, rows, col0, width, offsets):
        for k, q in taps:
            term = w_ref[k:k + 1, col0:col0 + width] * base[q:q + rows, :]
            acc = term if acc is None else acc + term
    return acc


def _conv_corr(buf_ref, w_ref, other, acc_ref, row0, rows, col0, width, offsets):
    acc = None
    for base, taps in _window_bases(buf_ref, row0, rows, col0, width, offsets):
        for k, q in taps:
            sl = base[q:q + rows, :]
            term = w_ref[k:k + 1, col0:col0 + width] * sl
            acc = term if acc is None else acc + term
            acc_ref[SUBLANES * k:SUBLANES * (k + 1), col0:col0 + width] += _rowsum8(sl * other)
    return acc


def _finish_tap_sums(acc_ref, out_ref, n_taps):
    for k in range(n_taps):
        out_ref[k:k + 1, :] = jnp.sum(acc_ref[SUBLANES * k:SUBLANES * (k + 1), :], axis=0, keepdims=True)


class _Comm:
    def __init__(self, inputs, out_shapes, sems, start, finish, aliases=None, mid=None, mid_steps=1):
        self.inputs, self.out_shapes, self.sems = list(inputs), list(out_shapes), list(sems)
        self.start, self.finish, self.mid, self.aliases = start, finish, mid, dict(aliases or {})
        self.mid_steps = mid_steps


def _join_comm(a, b):
    ni, no, ns = len(a.inputs), len(a.out_shapes), len(a.sems)

    def both(name):
        def run(ins, outs, sems):
            if getattr(a, name) is not None:
                getattr(a, name)(ins[:ni], outs[:no], sems[:ns])
            if getattr(b, name) is not None:
                getattr(b, name)(ins[ni:], outs[no:], sems[ns:])
        return run

    aliases = {**a.aliases, **{ni + i: no + o for i, o in b.aliases.items()}}
    mid = both("mid") if (a.mid is not None or b.mid is not None) else None
    return _Comm(a.inputs + b.inputs, a.out_shapes + b.out_shapes, a.sems + b.sems, both("start"), both("finish"), aliases, mid)


def _call(body, *, name, grid, in_specs, out_specs, out_shape, args, scratch=(), comm=None, aliases=None):
    n_in, n_out, n_scr, n_axes = len(in_specs), len(out_specs), len(scratch), len(grid)
    params = pltpu.CompilerParams(dimension_semantics=("arbitrary",) * n_axes, vmem_limit_bytes=VMEM_LIMIT_BYTES)
    aliases = dict(aliases or {})
    if comm is None:
        outs = pl.pallas_call(body, name=name, grid=grid, in_specs=list(in_specs), out_specs=list(out_specs),
                              out_shape=list(out_shape), scratch_shapes=list(scratch), input_output_aliases=aliases,
                              compiler_params=params)(*args)
        return list(outs), []
    ci, co = len(comm.inputs), len(comm.out_shapes)

    def wrapped(*refs):
        k_in, c_in = refs[:n_in], refs[n_in:n_in + ci]
        o0 = n_in + ci
        k_out, c_out = refs[o0:o0 + n_out], refs[o0 + n_out:o0 + n_out + co]
        s0 = o0 + n_out + co
        k_scr, c_sem = refs[s0:s0 + n_scr], refs[s0 + n_scr:]
        first = pl.program_id(0) == 0
        last = pl.program_id(0) == grid[0] - 1
        for ax in range(1, n_axes):
            first = jnp.logical_and(first, pl.program_id(ax) == 0)
            last = jnp.logical_and(last, pl.program_id(ax) == grid[ax] - 1)

        @pl.when(first)
        def _():
            comm.start(c_in, c_out, c_sem)

        mid_early = comm.mid is not None and n_axes == 1 and grid[0] > comm.mid_steps
        if mid_early:
            @pl.when(pl.program_id(0) == grid[0] - 1 - comm.mid_steps)
            def _():
                comm.mid(c_in, c_out, c_sem)

        body(*k_in, *k_out, *k_scr)

        @pl.when(last)
        def _():
            if comm.mid is not None and not mid_early:
                comm.mid(c_in, c_out, c_sem)
            comm.finish(c_in, c_out, c_sem)

    outs = pl.pallas_call(
        wrapped, name=name, grid=grid, in_specs=list(in_specs) + [ANY] * ci, out_specs=list(out_specs) + [ANY] * co,
        out_shape=list(out_shape) + comm.out_shapes, scratch_shapes=list(scratch) + comm.sems,
        input_output_aliases={**aliases, **{n_in + i: n_out + o for i, o in comm.aliases.items()}}, compiler_params=params,
    )(*args, *comm.inputs)
    return list(outs[:n_out]), list(outs[n_out:])


def _run_comm(comm, name):
    ci, co = len(comm.inputs), len(comm.out_shapes)

    def body(*refs):
        c_in, c_out, c_sem = refs[:ci], refs[ci:ci + co], refs[ci + co:]
        comm.start(c_in, c_out, c_sem)
        if comm.mid is not None:
            comm.mid(c_in, c_out, c_sem)
        comm.finish(c_in, c_out, c_sem)

    return list(pl.pallas_call(body, name=name, in_specs=[ANY] * ci, out_specs=[ANY] * co, out_shape=comm.out_shapes,
                               scratch_shapes=comm.sems, input_output_aliases=comm.aliases)(*comm.inputs))


def _mix0_fwd(x, g, w_in, conv_a, ln_g, ln_b, conv_b, w_out, comm=None):
    s = x.shape[0]
    ts = min(TS_MIX, s)
    n = s // ts
    bw = NZ // NCHIP
    offs_a = [HALO_A - (K_A - 1) + k for k in range(K_A)]
    offs_s = [HALO_S - (K_S - 1) + k for k in range(K_S)]

    def body(x_ref, g_ref, win_ref, ca_ref, lg_ref, lb_ref, cb_ref, wout_ref,
             h_ref, z_ref, ac_ref, bconv_buf, cat_ref, x1_ref, glu_buf, cv_buf):
        i = pl.program_id(0)

        @pl.when(i == 0)
        def _():
            glu_buf[0:HALO_A, :] = jnp.zeros((HALO_A, A), F32)
            cv_buf[0:HALO_S, :] = jnp.zeros((HALO_S, A), F32)

        xv = x_ref[...]
        h = (xv * _rsqrt_mean_sq(xv) * g_ref[...]).astype(BF16)
        h_ref[...] = h
        for j in range(NCHIP):
            z_ref[:, j * bw:(j + 1) * bw] = jnp.dot(h, win_ref[j], preferred_element_type=F32)
        glu_buf[HALO_A:HALO_A + ts, :] = z_ref[:, 0:A] * _sigmoid(z_ref[:, A:2 * A])
        cv_buf[HALO_S:HALO_S + ts, :] = z_ref[:, 3 * A:4 * A] * z_ref[:, 4 * A:5 * A]

        def chunk(ci, carry):
            r0 = pl.multiple_of(ci * R_CHUNK, R_CHUNK)
            for c0 in range(0, A, LANES):
                ac_ref[pl.ds(r0, R_CHUNK), c0:c0 + LANES] = _conv_acc(glu_buf, ca_ref, r0, R_CHUNK, c0, LANES, offs_a)
                bconv_buf[pl.ds(r0, R_CHUNK), c0:c0 + LANES] = _conv_acc(cv_buf, cb_ref, r0, R_CHUNK, c0, LANES, offs_s)
            return carry

        lax.fori_loop(0, ts // R_CHUNK, chunk, 0)
        glu_buf[0:HALO_A, :] = glu_buf[ts:ts + HALO_A, :]
        cv_buf[0:HALO_S, :] = cv_buf[ts:ts + HALO_S, :]

        ac = ac_ref[...]
        xc = ac - jnp.mean(ac, axis=-1, keepdims=True)
        xn = xc * lax.rsqrt(jnp.mean(xc * xc, axis=-1, keepdims=True) + LN_EPS)
        ln = xn * lg_ref[...] + lb_ref[...]
        cat_ref[:, 0:A] = (ln * _sigmoid(ln)).astype(BF16)
        cat_ref[:, A:2 * A] = (z_ref[:, 2 * A:3 * A] * bconv_buf[...]).astype(BF16)
        x1_ref[...] = xv + jnp.dot(cat_ref[...], wout_ref[...], preferred_element_type=F32)

    tile = lambda w: pl.BlockSpec((ts, w), lambda i: (i, 0))
    return _call(
        body, name="mix0_fwd", grid=(n,), comm=comm, args=(x, g, w_in, conv_a, ln_g, ln_b, conv_b, w_out),
        in_specs=[tile(D), _const((1, D)), _const((NCHIP, D, bw)), _const((K_A, A)), _const((1, A)), _const((1, A)),
                  _const((K_S, A)), _const((2 * A, D))],
        out_specs=[tile(D), tile(NZ), tile(A), tile(A), tile(2 * A), tile(D)],
        out_shape=[jax.ShapeDtypeStruct((s, D), BF16), jax.ShapeDtypeStruct((s, NZ), F32), jax.ShapeDtypeStruct((s, A), F32),
                   jax.ShapeDtypeStruct((s, A), F32), jax.ShapeDtypeStruct((s, 2 * A), BF16), jax.ShapeDtypeStruct((s, D), F32)],
        scratch=[pltpu.VMEM((HALO_A + ts, A), F32), pltpu.VMEM((HALO_S + ts, A), F32)])


def _ffn_fwd(x, g, w_up, wc, w_down, name, comm=None, head=None):
    s = x.shape[0]
    ts = min(TS_FFN, s)
    n = s // ts
    bw = FF2 // NCHIP
    rows = 32
    offs = [HALO_S - (K_S - 1) + k for k in range(K_S)]
    n_in = 5 + (2 if head else 0)

    def body(*refs):
        x_ref, g_ref, wup_ref, wc_ref, wdn_ref = refs[:5]
        h_ref, u0_ref, u_ref, act_ref, xo_ref = refs[n_in:n_in + 5]
        cbuf = refs[-1]
        i = pl.program_id(0)

        @pl.when(i == 0)
        def _():
            cbuf[0:HALO_S, :] = jnp.zeros((HALO_S, FF2), F32)
            if head:
                refs[n_in + 5][...] = jnp.zeros((1, D), F32)
                refs[n_in + 6][...] = jnp.zeros((1, LANES), F32)

        xv = x_ref[...]
        h = (xv * _rsqrt_mean_sq(xv) * g_ref[...]).astype(BF16)
        h_ref[...] = h
        f = None
        for p in range(NCHIP // 2):
            for j in (p, NCHIP // 2 + p):
                zc = jnp.dot(h, wup_ref[j], preferred_element_type=F32)
                u0_ref[:, j * bw:(j + 1) * bw] = zc.astype(BF16)
                cbuf[HALO_S:HALO_S + ts, j * bw:(j + 1) * bw] = zc
            for r0 in range(0, ts, rows):
                for c0 in range(p * bw, (p + 1) * bw, LANES):
                    ug = _taps(wc_ref, _shifted(cbuf, r0, rows, c0, LANES, offs), offs, c0, LANES)
                    uv = _taps(wc_ref, _shifted(cbuf, r0, rows, FF + c0, LANES, offs), offs, FF + c0, LANES)
                    u_ref[r0:r0 + rows, c0:c0 + LANES] = ug
                    u_ref[r0:r0 + rows, FF + c0:FF + c0 + LANES] = uv
                    act_ref[r0:r0 + rows, c0:c0 + LANES] = (ug * _sigmoid(ug) * uv).astype(BF16)
            fp = jnp.dot(act_ref[:, p * bw:(p + 1) * bw], wdn_ref[p * bw:(p + 1) * bw, :], preferred_element_type=F32)
            f = fp if f is None else f + fp
        cbuf[0:HALO_S, :] = cbuf[ts:ts + HALO_S, :]
        if not head:
            xo_ref[...] = xv + f
        else:
            gf_ref, t_ref, dgf_ref, loss_ref = refs[5], refs[6], refs[n_in + 5], refs[n_in + 6]
            xo = xv + f
            r = _rsqrt_mean_sq(xo)
            xh = xo * r
            gv = gf_ref[...]
            err = xh * gv - t_ref[...]
            loss_ref[...] += jnp.sum(jnp.sum(err * err, axis=1, keepdims=True), axis=0, keepdims=True) * (0.5 / D)
            dy = err * (1.0 / D)
            dgf_ref[...] += jnp.sum(dy * xh, axis=0, keepdims=True)
            xo_ref[...] = _rms_bwd(dy, xh, r, gv)

    tile = lambda w: pl.BlockSpec((ts, w), lambda i: (i, 0))
    one_row = lambda w: pl.BlockSpec((1, w), lambda i: (0, 0))
    return _call(
        body, name=name, grid=(n,), comm=comm, args=(x, g, w_up, wc, w_down) + (tuple(head) if head else ()),
        in_specs=[tile(D), _const((1, D)), _const((NCHIP, D, bw)), _const((K_S, FF2)), _const((FF, D))]
        + ([_const((1, D)), tile(D)] if head else []),
        out_specs=[tile(D), tile(FF2), tile(FF2), tile(FF), tile(D)] + ([one_row(D), one_row(LANES)] if head else []),
        out_shape=[jax.ShapeDtypeStruct((s, D), BF16), jax.ShapeDtypeStruct((s, FF2), BF16), jax.ShapeDtypeStruct((s, FF2), F32),
                   jax.ShapeDtypeStruct((s, FF), BF16), jax.ShapeDtypeStruct((s, D), F32)]
        + ([jax.ShapeDtypeStruct((1, D), F32), jax.ShapeDtypeStruct((1, LANES), F32)] if head else []),
        scratch=[pltpu.VMEM((HALO_S + ts, FF2), F32)])


def _pool_windows(hbuf, pbuf, tile_row0, ts):
    def chunk(ci, carry):
        r0 = pl.multiple_of(ci * R_CHUNK, R_CHUNK)
        t1 = (tile_row0 + r0 + lax.broadcasted_iota(jnp.int32, (R_CHUNK, 1), 0) + 1).astype(F32)
        for gi, w in enumerate(POOL_WINDOWS):
            cnt = jnp.minimum(t1, float(w))
            offs = [HALO_P - jj for jj in range(w)]
            for c0 in range(gi * PG, (gi + 1) * PG, LANES):
                sh = _shifted(hbuf, r0, R_CHUNK, c0, LANES, offs)
                tot = sh[offs[0]]
                for o in offs[1:]:
                    tot = tot + sh[o]
                pbuf[pl.ds(r0, R_CHUNK), c0:c0 + LANES] = (tot / cnt - sh[HALO_P]).astype(BF16)
        return carry

    lax.fori_loop(0, ts // R_CHUNK, chunk, 0)


def _assemble_wpool(wp_ref, wps):
    rb = PG // NCHIP
    for gi in range(len(POOL_WINDOWS)):
        for j in range(NCHIP):
            wps[gi, j * rb:(j + 1) * rb, :] = wp_ref[j, gi]


def _pool_fwd(x, g, w_pool, scale):
    s = x.shape[0]
    ts = min(TS_POOL, s)
    n = s // ts
    ng = len(POOL_WINDOWS)

    def body(x_ref, g_ref, wp_ref, sc_ref, xo_ref, hbuf, pbuf, wps):
        i = pl.program_id(0)

        @pl.when(i == 0)
        def _():
            hbuf[0:HALO_P, :] = jnp.zeros((HALO_P, D), F32)
            _assemble_wpool(wp_ref, wps)

        xv = x_ref[...]
        hbuf[HALO_P:HALO_P + ts, :] = xv * _rsqrt_mean_sq(xv) * g_ref[...]
        _pool_windows(hbuf, pbuf, i * ts, ts)
        hbuf[0:HALO_P, :] = hbuf[ts:ts + HALO_P, :]
        for gi in range(ng):
            cols = slice(gi * PG, (gi + 1) * PG)
            y = jnp.dot(pbuf[:, cols], wps[gi], preferred_element_type=F32)
            xo_ref[:, cols] = xv[:, cols] + y * sc_ref[:, cols]

    tile = pl.BlockSpec((ts, D), lambda i: (i, 0))
    return pl.pallas_call(
        body, name="pool_fwd", grid=(n,),
        in_specs=[tile, _const((1, D)), _const((NCHIP, ng, PG // NCHIP, PG)), _const((1, D))],
        out_specs=tile, out_shape=jax.ShapeDtypeStruct((s, D), F32),
        scratch_shapes=[pltpu.VMEM((HALO_P + ts, D), F32), pltpu.VMEM((ts, D), BF16), pltpu.VMEM((ng, PG, PG), BF16)],
        compiler_params=_cparams(1),
    )(x, g, w_pool, scale)


def _ffn_bwd_a(dxo, u, u0, wc, w_down, name, comm=None):
    s = dxo.shape[0]
    ts = min(TS_FFN, s)
    n = s // ts
    cw = FF2 // NCHIP
    rows = 32
    lw2 = 2 * LANES
    boffs = [K_S - 1 - k for k in range(K_S)]

    def body(dxo_ref, u_ref, u0_ref, wc_ref, wdn_ref, du0_ref, dwc_ref, dubuf, dact, dwacc):
        i = pl.program_id(0)

        @pl.when(i == 0)
        def _():
            dubuf[ts:ts + HALO_S, :] = jnp.zeros((HALO_S, FF2), F32)
            dwacc[...] = jnp.zeros(dwacc.shape, F32)

        df = dxo_ref[...].astype(BF16)
        for cg in range(0, FF, cw):
            dact[...] = lax.dot_general(df, wdn_ref[cg:cg + cw, :], NT_DIMS, preferred_element_type=F32)

            def chunk(ci, carry, cg=cg):
                rs = pl.ds(pl.multiple_of(ci * rows, rows), rows)
                for c in range(cg, cg + cw, LANES):
                    ug, uv = u_ref[rs, c:c + LANES], u_ref[rs, FF + c:FF + c + LANES]
                    sg = _sigmoid(ug)
                    da = dact[rs, c - cg:c - cg + LANES]
                    gs = ug * sg
                    dubuf[rs, c:c + LANES] = da * uv * (sg + gs * (1.0 - sg))
                    dubuf[rs, FF + c:FF + c + LANES] = da * gs
                return carry

            lax.fori_loop(0, ts // rows, chunk, 0)

        def chunk2(ci, carry):
            r0 = pl.multiple_of(ci * rows, rows)
            rs = pl.ds(r0, rows)
            for c in range(0, FF2, lw2):
                sh = _shifted(dubuf, r0, rows, c, lw2, boffs)
                du0_ref[rs, c:c + lw2] = _taps(wc_ref, sh, boffs, c, lw2).astype(BF16)
                u0v = u0_ref[rs, c:c + lw2].astype(F32)
                for k, o in enumerate(boffs):
                    dwacc[SUBLANES * k:SUBLANES * (k + 1), c:c + lw2] += _rowsum8(sh[o] * u0v)
            return carry

        lax.fori_loop(0, ts // rows, chunk2, 0)
        dubuf[ts:ts + HALO_S, :] = dubuf[0:HALO_S, :]

        @pl.when(i == n - 1)
        def _():
            _finish_tap_sums(dwacc, dwc_ref, K_S)

    rev = lambda w: pl.BlockSpec((ts, w), lambda i: (n - 1 - i, 0))
    return _call(
        body, name=name, grid=(n,), comm=comm, args=(dxo, u, u0, wc, w_down),
        in_specs=[rev(D), rev(FF2), rev(FF2), _const((K_S, FF2)), _const((FF, D))],
        out_specs=[rev(FF2), pl.BlockSpec((K_S, FF2), lambda i: (0, 0))],
        out_shape=[jax.ShapeDtypeStruct((s, FF2), BF16), jax.ShapeDtypeStruct((K_S, FF2), F32)],
        scratch=[pltpu.VMEM((ts + HALO_S, FF2), F32), pltpu.VMEM((ts, cw), F32), pltpu.VMEM((SUBLANES * K_S, FF2), F32)])


def _nt_rms_bwd(dy, w, x, g, dres, name, comm=None, tiles=None, dx_so_far=None):
    s = x.shape[0]
    ts = min(TS_MM, s)
    first, n = (0, s // ts) if tiles is None else tiles
    nw = dy.shape[1]
    bw = nw // NCHIP

    def body(dy_ref, w_ref, x_ref, g_ref, dres_ref, dx_ref, dg_ref):
        i = pl.program_id(0)

        @pl.when(i == 0)
        def _():
            dg_ref[...] = jnp.zeros((1, D), F32)

        dh = lax.dot_general(dy_ref[:, 0:bw], w_ref[0], NT_DIMS, preferred_element_type=F32)
        for j in range(1, NCHIP):
            dh = dh + lax.dot_general(dy_ref[:, j * bw:(j + 1) * bw], w_ref[j], NT_DIMS, preferred_element_type=F32)
        xv = x_ref[...]
        r = _rsqrt_mean_sq(xv)
        xh = xv * r
        dg_ref[...] += jnp.sum(dh * xh, axis=0, keepdims=True)
        dx_ref[...] = dres_ref[...] + _rms_bwd(dh, xh, r, g_ref[...])

    def body_with_alias(dy_ref, w_ref, x_ref, g_ref, dres_ref, _, dx_ref, dg_ref):
        body(dy_ref, w_ref, x_ref, g_ref, dres_ref, dx_ref, dg_ref)

    tile = lambda wd: pl.BlockSpec((ts, wd), lambda i: (first + i, 0))
    in_specs = [tile(nw), _const((NCHIP, D, bw)), tile(D), _const((1, D)), tile(D)]
    more = dx_so_far is not None
    return _call(
        body_with_alias if more else body, name=name, grid=(n,), comm=comm,
        args=(dy, w, x, g, dres) + ((dx_so_far,) if more else ()), in_specs=in_specs + [ANY] * more,
        out_specs=[tile(D), pl.BlockSpec((1, D), lambda i: (0, 0))], aliases={5: 0} if more else None,
        out_shape=[jax.ShapeDtypeStruct((s, D), F32), jax.ShapeDtypeStruct((1, D), F32)])


def _wgrad(a, b, n_blocks, name, comm=None):
    s, m = a.shape
    bw = b.shape[1] // n_blocks
    tk = min(TS_WGRAD, s)
    group = min(g for g in range(1, n_blocks + 1) if n_blocks % g == 0 and (g * bw >= m or g == n_blocks))

    def body(a_ref, b_ref, o_ref):
        part = lambda j: lax.dot_general(a_ref[...], b_ref[:, j * bw:(j + 1) * bw].astype(BF16), TN_DIMS,
                                         preferred_element_type=F32)

        @pl.when(pl.program_id(1) == 0)
        def _():
            for j in range(group):
                o_ref[j] = part(j)

        @pl.when(pl.program_id(1) != 0)
        def _():
            for j in range(group):
                o_ref[j] += part(j)

    (out,), comm_out = _call(
        body, name=name, grid=(n_blocks // group, s // tk), comm=comm, args=(a, b),
        in_specs=[pl.BlockSpec((tk, m), lambda j, k: (k, 0)), pl.BlockSpec((tk, group * bw), lambda j, k: (k, j))],
        out_specs=[pl.BlockSpec((group, m, bw), lambda j, k: (j, 0, 0))],
        out_shape=[jax.ShapeDtypeStruct((n_blocks, m, bw), F32)])
    return out, comm_out


def _pool_bwd(dxo, x, g, w_pool, scale, comm=None):
    s = x.shape[0]
    ts = min(TS_POOL, s)
    n = s // ts
    ng = len(POOL_WINDOWS)
    rb = PG // NCHIP

    def body(dxo_ref, x_ref, halo_ref, g_ref, wp_ref, sc_ref, dx_ref, dwp_ref, dsc_ref, dg_ref,
             hbuf, pbuf, qbuf, dhbuf, wps, dwacc):
        i = pl.program_id(0)
        j = n - 1 - i

        @pl.when(i == 0)
        def _():
            qbuf[ts:ts + HALO_P, :] = jnp.zeros((HALO_P, D), F32)
            dwacc[...] = jnp.zeros(dwacc.shape, F32)
            dsc_ref[...] = jnp.zeros((1, D), F32)
            dg_ref[...] = jnp.zeros((1, D), F32)
            _assemble_wpool(wp_ref, wps)

        gv = g_ref[...]
        xl = halo_ref[...]
        hbuf[0:HALO_P, :] = jnp.where(j == 0, 0.0, xl * _rsqrt_mean_sq(xl) * gv)
        xv = x_ref[...]
        r = _rsqrt_mean_sq(xv)
        xh = xv * r
        hbuf[HALO_P:HALO_P + ts, :] = xh * gv
        _pool_windows(hbuf, pbuf, j * ts, ts)

        dy = dxo_ref[...]
        t1 = (j * ts + lax.broadcasted_iota(jnp.int32, (ts, 1), 0) + 1).astype(F32)
        for gi, w in enumerate(POOL_WINDOWS):
            cols = slice(gi * PG, (gi + 1) * PG)
            p = pbuf[:, cols]
            y = jnp.dot(p, wps[gi], preferred_element_type=F32)
            dsc_ref[:, cols] += jnp.sum(dy[:, cols] * y, axis=0, keepdims=True)
            dq = (dy[:, cols] * sc_ref[:, cols]).astype(BF16)
            dwacc[gi] += lax.dot_general(p, dq, TN_DIMS, preferred_element_type=F32)
            dp = lax.dot_general(dq, wps[gi], NT_DIMS, preferred_element_type=F32)
            qbuf[0:ts, cols] = dp / jnp.minimum(t1, float(w))

        def chunk(ci, carry):
            r0 = pl.multiple_of(ci * R_CHUNK, R_CHUNK)
            tc = (j * ts + r0 + lax.broadcasted_iota(jnp.int32, (R_CHUNK, 1), 0) + 1).astype(F32)
            for gi, w in enumerate(POOL_WINDOWS):
                cnt = jnp.minimum(tc, float(w))
                offs = list(range(w))
                for c0 in range(gi * PG, (gi + 1) * PG, LANES):
                    sh = _shifted(qbuf, r0, R_CHUNK, c0, LANES, offs)
                    tot = sh[0]
                    for o in offs[1:]:
                        tot = tot + sh[o]
                    dhbuf[pl.ds(r0, R_CHUNK), c0:c0 + LANES] = tot - sh[0] * cnt
            return carry

        lax.fori_loop(0, ts // R_CHUNK, chunk, 0)
        qbuf[ts:ts + HALO_P, :] = qbuf[0:HALO_P, :]
        dh = dhbuf[...]
        dg_ref[...] += jnp.sum(dh * xh, axis=0, keepdims=True)
        dx_ref[...] = dy + _rms_bwd(dh, xh, r, gv)

        @pl.when(i == n - 1)
        def _():
            for gi in range(ng):
                for jj in range(NCHIP):
                    dwp_ref[jj, gi] = dwacc[gi, jj * rb:(jj + 1) * rb, :]

    rev = pl.BlockSpec((ts, D), lambda i: (n - 1 - i, 0))
    halo = pl.BlockSpec((HALO_P, D), lambda i: (jnp.maximum((n - 1 - i) * (ts // HALO_P) - 1, 0), 0))
    vec = pl.BlockSpec((1, D), lambda i: (0, 0))
    return _call(
        body, name="pool_bwd", grid=(n,), comm=comm, args=(dxo, x, x, g, w_pool, scale),
        in_specs=[rev, rev, halo, _const((1, D)), _const((NCHIP, ng, rb, PG)), _const((1, D))],
        out_specs=[rev, pl.BlockSpec((NCHIP, ng, rb, PG), lambda i: (0, 0, 0, 0)), vec, vec],
        out_shape=[jax.ShapeDtypeStruct((s, D), F32), jax.ShapeDtypeStruct((NCHIP, ng, rb, PG), F32),
                   jax.ShapeDtypeStruct((1, D), F32), jax.ShapeDtypeStruct((1, D), F32)],
        scratch=[pltpu.VMEM((HALO_P + ts, D), F32), pltpu.VMEM((ts, D), BF16), pltpu.VMEM((ts + HALO_P, D), F32),
                 pltpu.VMEM((ts, D), F32), pltpu.VMEM((ng, PG, PG), BF16), pltpu.VMEM((ng, PG, PG), F32)])


def _mix0_bwd_a(dx1, z, ac, bconv, w_out, conv_a, ln_g, ln_b, conv_b, comm=None):
    s = dx1.shape[0]
    ts = min(TS_MIXB, s)
    n = s // ts
    rows = 32
    boffs_a = [K_A - 1 - k for k in range(K_A)]
    boffs_s = [K_S - 1 - k for k in range(K_S)]

    def body(dx1_ref, z_ref, ac_ref, bconv_ref, wout_ref, ca_ref, lg_ref, lb_ref, cb_ref,
             dz_ref, dca_ref, dlg_ref, dlb_ref, dcb_ref, dac_buf, dbc_buf, dca_acc, dcb_acc):
        i = pl.program_id(0)

        @pl.when(i == 0)
        def _():
            dac_buf[ts:ts + HALO_A, :] = jnp.zeros((HALO_A, A), F32)
            dbc_buf[ts:ts + HALO_S, :] = jnp.zeros((HALO_S, A), F32)
            dca_acc[...] = jnp.zeros(dca_acc.shape, F32)
            dcb_acc[...] = jnp.zeros(dcb_acc.shape, F32)
            dlg_ref[...] = jnp.zeros((1, A), F32)
            dlb_ref[...] = jnp.zeros((1, A), F32)

        dcat = lax.dot_general(dx1_ref[...].astype(BF16), wout_ref[...], NT_DIMS, preferred_element_type=F32)
        db = dcat[:, A:2 * A]
        dz_ref[:, 2 * A:3 * A] = (db * bconv_ref[...]).astype(BF16)
        dbc_buf[0:ts, :] = db * z_ref[:, 2 * A:3 * A]
        ac = ac_ref[...]
        xc = ac - jnp.mean(ac, axis=-1, keepdims=True)
        rstd = lax.rsqrt(jnp.mean(xc * xc, axis=-1, keepdims=True) + LN_EPS)
        xn = xc * rstd
        lg = lg_ref[...]
        ln = xn * lg + lb_ref[...]
        sl = _sigmoid(ln)
        dln = dcat[:, 0:A] * sl * (1.0 + ln * (1.0 - sl))
        dlg_ref[...] += jnp.sum(dln * xn, axis=0, keepdims=True)
        dlb_ref[...] += jnp.sum(dln, axis=0, keepdims=True)
        dxn = dln * lg
        dac_buf[0:ts, :] = rstd * (dxn - jnp.mean(dxn, axis=-1, keepdims=True)
                                   - xn * jnp.mean(dxn * xn, axis=-1, keepdims=True))

        def chunk(ci, carry):
            r0 = pl.multiple_of(ci * rows, rows)
            rs = pl.ds(r0, rows)
            for c0 in range(0, A, LANES):
                col = lambda grp: slice(grp * A + c0, grp * A + c0 + LANES)
                a_val, sg = z_ref[rs, col(0)], _sigmoid(z_ref[rs, col(1)])
                dglu = _conv_corr(dac_buf, ca_ref, a_val * sg, dca_acc, r0, rows, c0, LANES, boffs_a)
                dz_ref[rs, col(0)] = (dglu * sg).astype(BF16)
                dz_ref[rs, col(1)] = (dglu * a_val * sg * (1.0 - sg)).astype(BF16)
                c_gate, bc_val = z_ref[rs, col(3)], z_ref[rs, col(4)]
                dcv = _conv_corr(dbc_buf, cb_ref, c_gate * bc_val, dcb_acc, r0, rows, c0, LANES, boffs_s)
                dz_ref[rs, col(3)] = (dcv * bc_val).astype(BF16)
                dz_ref[rs, col(4)] = (dcv * c_gate).astype(BF16)
            return carry

        lax.fori_loop(0, ts // rows, chunk, 0)
        dac_buf[ts:ts + HALO_A, :] = dac_buf[0:HALO_A, :]
        dbc_buf[ts:ts + HALO_S, :] = dbc_buf[0:HALO_S, :]

        @pl.when(i == n - 1)
        def _():
            _finish_tap_sums(dca_acc, dca_ref, K_A)
            _finish_tap_sums(dcb_acc, dcb_ref, K_S)

    rev = lambda w: pl.BlockSpec((ts, w), lambda i: (n - 1 - i, 0))
    full = lambda r, c: pl.BlockSpec((r, c), lambda i: (0, 0))
    return _call(
        body, name="mix0_bwd_a", grid=(n,), comm=comm, args=(dx1, z, ac, bconv, w_out, conv_a, ln_g, ln_b, conv_b),
        in_specs=[rev(D), rev(NZ), rev(A), rev(A), _const((2 * A, D)), _const((K_A, A)), _const((1, A)), _const((1, A)),
                  _const((K_S, A))],
        out_specs=[rev(NZ), full(K_A, A), full(1, A), full(1, A), full(K_S, A)],
        out_shape=[jax.ShapeDtypeStruct((s, NZ), BF16), jax.ShapeDtypeStruct((K_A, A), F32), jax.ShapeDtypeStruct((1, A), F32),
                   jax.ShapeDtypeStruct((1, A), F32), jax.ShapeDtypeStruct((K_S, A), F32)],
        scratch=[pltpu.VMEM((ts + HALO_A, A), F32), pltpu.VMEM((ts + HALO_S, A), F32), pltpu.VMEM((SUBLANES * K_A, A), F32),
                 pltpu.VMEM((SUBLANES * K_S, A), F32)])


def _cast_later_weights(w_up, w_down, w_pool, comm):
    nl = w_up.shape[0]

    def body(up_ref, dn_ref, pool_ref, *outs):
        up_o, dn_o, pool_o = outs[:nl], outs[nl:2 * nl], outs[2 * nl]
        for layer in range(nl):
            @pl.when(pl.program_id(0) == layer)
            def _(layer=layer):
                up_o[layer][...] = up_ref[...].astype(BF16)
                dn_o[layer][...] = dn_ref[...].astype(BF16)
                if layer == 0:
                    pool_o[...] = pool_ref[...].astype(BF16)

    per_layer = lambda a: pl.BlockSpec((None,) + a.shape[1:], lambda l: (l,) + (0,) * (a.ndim - 1))
    whole = lambda shape: pl.BlockSpec(shape, lambda l: (0,) * len(shape))
    out_shapes = [w_up.shape[1:]] * nl + [w_down.shape[1:]] * nl + [w_pool.shape]
    return _call(body, name="cast_later_weights", grid=(nl,), comm=comm, args=(w_up, w_down, w_pool),
                 in_specs=[per_layer(w_up), per_layer(w_down), whole(w_pool.shape)],
                 out_specs=[whole(shape) for shape in out_shapes],
                 out_shape=[jax.ShapeDtypeStruct(shape, BF16) for shape in out_shapes])


def _train_step(x, target, place, shard, later, rep, small_pack, small_shapes, adamw):
    bw_up, bw_in = FF2 // NCHIP, NZ // NCHIP
    five = lambda a, k: a.reshape(NCHIP, 2, *HALF[k])

    half = lambda a, name: a.reshape((2,) + HALF[name])
    (up0, up1, dn0, dn1, pool_bf), (g_in, g_out, small_g) = _cast_later_weights(
        *later, comm=_gather_comm([shard["w_in"], shard["w_out"]], small_pack))
    shard = dict(shard, w_pool=half(pool_bf, "w_pool"), w_up0=half(up0, "w_up"), w_up1=half(up1, "w_up"),
                 w_down0=half(dn0, "w_down"), w_down1=half(dn1, "w_down"))
    whole = {}
    for k, part in zip(SMALL_SHARDED, _unpack(small_g, small_shapes, lead=(NCHIP,))):
        whole[k] = jnp.moveaxis(part, 0, 1).reshape(part.shape[1], NCHIP * part.shape[2])
    w_in, w_out = g_in.reshape(NCHIP, D, bw_in), g_out.reshape(2 * A, D)
    conv_ffn = whole["conv_ffn_w"].reshape(2, K_S, FF2)
    nffn = [rep["norm_ffn"][0:1], rep["norm_ffn"][1:2]]

    (h0, z, ac, bconv, cat, x1), (g_up0, g_dn0) = _mix0_fwd(
        x, rep["norm_mix_even"], w_in, whole["conv_a"], rep["ln_a_g"], rep["ln_a_b"], whole["conv_b"], w_out,
        comm=_gather_comm([shard["w_up0"], shard["w_down0"]], mid_steps=GATHER_MID_STEPS["mix0_fwd"]))
    w_up0, w_dn0 = g_up0.reshape(NCHIP, D, bw_up), g_dn0.reshape(FF, D)
    (hf0, u00, u0, act0, x2), (g_pool, g_up1, g_dn1) = _ffn_fwd(
        x1, nffn[0], w_up0, conv_ffn[0], w_dn0, "ffn0_fwd",
        comm=_gather_comm([shard["w_pool"], shard["w_up1"], shard["w_down1"]], mid_steps=GATHER_MID_STEPS["ffn0_fwd"]))
    w_pool = g_pool.reshape(NCHIP, len(POOL_WINDOWS), PG // NCHIP, PG)
    w_up1, w_dn1 = g_up1.reshape(NCHIP, D, bw_up), g_dn1.reshape(FF, D)
    x3 = _pool_fwd(x2, whole["norm_mix_odd"], w_pool, whole["pool_scale"])
    (hf1, u01, u1, act1, dx4, g_nfin, loss_part), _ = _ffn_fwd(x3, nffn[1], w_up1, conv_ffn[1], w_dn1, "ffn1_fwd",
                                                                head=(rep["norm_final"], target))

    psum = lambda k, g, ld, tag="": _pair_sum(place, g, ld, "pair_sum_" + k + tag)
    tot = lambda k, g, ld, p, layer=0, nl=1, prev=None: _chip_sum(place, g, ld, p, layer, nl, "chip_sum_%s%d" % (k, layer), prev)
    pair, chips, join = _pair_comm, _chips_comm, _join_comm

    gr_dn1 = five(_wgrad(act1, dx4, 1, "wgrad_down1")[0], "w_down")
    (du01, g_wc1), _ = _ffn_bwd_a(dx4, u1, u01, conv_ffn[1], w_dn1, "ffn1_bwd_a")
    g_up, (ld_dn1,) = _wgrad(hf1, du01, NCHIP, "wgrad_up1", comm=pair([gr_dn1]))
    gr_up1 = five(g_up, "w_up")
    s_dn1 = psum("w_down", gr_dn1, ld_dn1, "1")
    (dx3, g_nf1), (p_dn1, ld_up1) = _nt_rms_bwd(du01, w_up1, x3, nffn[1], dx4, "ffn1_bwd_b",
                                                comm=join(chips([s_dn1]), pair([gr_up1])))
    s_up1 = psum("w_up", gr_up1, ld_up1, "1")
    (dx2, g_wpool, g_scale, g_nmo), _ = _pool_bwd(dx3, x2, whole["norm_mix_odd"], w_pool, whole["pool_scale"])
    gr_pool = five(g_wpool, "w_pool")
    (du00, g_wc0), (p_up1,) = _ffn_bwd_a(dx2, u0, u00, conv_ffn[0], w_dn0, "ffn0_bwd_a", comm=chips([s_up1]))
    gr_dn0 = five(_wgrad(act0, dx2, 1, "wgrad_down0")[0], "w_down")
    g_up, (ld_dn0, ld_pool) = _wgrad(hf0, du00, NCHIP, "wgrad_up0", comm=pair([gr_dn0, gr_pool]))
    gr_up0 = five(g_up, "w_up")
    s_dn0, s_pool = psum("w_down", gr_dn0, ld_dn0, "0"), psum("w_pool", gr_pool, ld_pool)
    (dx1, g_nf0), (p_dn0, p_pool, ld_up0) = _nt_rms_bwd(du00, w_up0, x1, nffn[0], dx2, "ffn0_bwd_b",
                                                        comm=join(chips([s_dn0, s_pool]), pair([gr_up0])))
    s_up0 = psum("w_up", gr_up0, ld_up0, "0")
    gr_out = five(_wgrad(cat, dx1, 1, "wgrad_out")[0], "w_out")
    (dz, g_ca, g_lg, g_lb, g_cb), (p_up0, ld_out) = _mix0_bwd_a(
        dx1, z, ac, bconv, w_out, whole["conv_a"], rep["ln_a_g"], rep["ln_a_b"], whole["conv_b"],
        comm=join(chips([s_up0]), pair([gr_out])))
    s_out = psum("w_out", gr_out, ld_out)
    t_pool = tot("w_pool", gr_pool, ld_pool, p_pool)
    t_up = tot("w_up", gr_up0, ld_up0, p_up0, 0, 2, tot("w_up", gr_up1, ld_up1, p_up1, 1, 2))
    t_dn = tot("w_down", gr_dn0, ld_dn0, p_dn0, 0, 2, tot("w_down", gr_dn1, ld_dn1, p_dn1, 1, 2))
    small = {"norm_mix_odd": g_nmo, "pool_scale": g_scale, "norm_ffn": jnp.concatenate([g_nf0, g_nf1], axis=0),
             "conv_ffn_w": jnp.stack([g_wc0, g_wc1]), "norm_final": g_nfin, "loss": loss_part,
             "conv_a": g_ca, "ln_a_g": g_lg, "ln_a_b": g_lb, "conv_b": g_cb}
    g_in, (p_out, early_all, t_pool, t_up, t_dn) = _wgrad(
        h0, dz, NCHIP, "wgrad_in",
        comm=join(chips([s_out], _pack([small[k] for k in SMALL_EARLY])), _swap_comm([t_pool, t_up, t_dn])))
    gr_in = five(g_in, "w_in")
    n_mm = x.shape[0] // min(TS_MM, x.shape[0])
    n_a = max(1, n_mm // 4)
    last_bwd = functools.partial(_nt_rms_bwd, dz, w_in, x, rep["norm_mix_even"], dx1)
    (dx_a, g_nme_a), (ld_in,) = last_bwd("mix0_bwd_b0", comm=pair([gr_in]), tiles=(0, n_a))
    s_in = psum("w_in", gr_in, ld_in)
    (grad_x, g_nme_b), (p_in,) = last_bwd("mix0_bwd_b1", comm=chips([s_in]), tiles=(n_a, n_mm - n_a), dx_so_far=dx_a)
    small["norm_mix_even"] = g_nme_a + g_nme_b
    (late_all,) = _run_comm(chips([], _pack([small[k] for k in SMALL_LATE])), "rs_last")
    t_in, t_out = _run_comm(_swap_comm([tot("w_in", gr_in, ld_in, p_in), tot("w_out", gr_out, ld_out, p_out)]), "rs_swap")
    done = {k: adamw(k, t) for k, t in (("w_up", t_up), ("w_down", t_dn), ("w_pool", t_pool), ("w_in", t_in), ("w_out", t_out))}
    summed = dict(zip(SMALL_EARLY, _unpack(_sum_devices(early_all, "sum_small_early"), [small[k].shape for k in SMALL_EARLY])))
    summed.update(zip(SMALL_LATE, _unpack(_sum_devices(late_all, "sum_small_late"), [small[k].shape for k in SMALL_LATE])))
    return grad_x, done, summed


def _adamw_math(w, g, m, v):
    m = ADAM_B1 * m + (1.0 - ADAM_B1) * g
    v = ADAM_B2 * v + (1.0 - ADAM_B2) * (g * g)
    m_hat = m / (1.0 - ADAM_B1 ** ADAM_STEP)
    v_hat = v / (1.0 - ADAM_B2 ** ADAM_STEP)
    return -ADAM_LR * (m_hat / (jnp.sqrt(v_hat) + ADAM_EPS) + ADAM_WD * w), m, v


def _adamw_big(w, g, m, v, tr, name, comm=None):
    nl, rows, cols = w.shape

    def body(w_ref, g_ref, m_ref, v_ref, g2_ref, d_ref, m2_ref, v2_ref):
        gv = g_ref[...]
        g2_ref[...] = gv
        d_ref[...], m2_ref[...], v2_ref[...] = _adamw_math(w_ref[...], gv, m_ref[...], v_ref[...])

    spec = pl.BlockSpec((None, tr, cols), lambda l, r: (l, r, 0))
    return _call(body, name=name, grid=(nl, rows // tr), comm=comm, args=(w, g, m, v), in_specs=[spec] * 4,
                 out_specs=[spec] * 4, out_shape=[jax.ShapeDtypeStruct(w.shape, F32)] * 4)


def _adamw_small(ws, gs, ms, vs):
    n = len(ws)

    def body(*refs):
        for p in range(n):
            w_ref, g_ref, m_ref, v_ref = (refs[q * n + p] for q in range(4))
            d_ref, m2_ref, v2_ref = (refs[(4 + q) * n + p] for q in range(3))
            d_ref[...], m2_ref[...], v2_ref[...] = _adamw_math(w_ref[...], g_ref[...], m_ref[...], v_ref[...])

    whole = lambda a: pl.BlockSpec(a.shape, lambda: (0,) * a.ndim)
    outs = pl.pallas_call(
        body, name="adamw_small", in_specs=[whole(a) for a in ws] * 4, out_specs=[whole(a) for a in ws] * 3,
        out_shape=[jax.ShapeDtypeStruct(a.shape, F32) for a in ws] * 3,
        compiler_params=pltpu.CompilerParams(vmem_limit_bytes=VMEM_LIMIT_BYTES),
    )(*ws, *gs, *ms, *vs)
    return outs[0:n], outs[n:2 * n], outs[2 * n:3 * n]


def _place():
    x, y, c = lax.axis_index("x"), lax.axis_index("y"), lax.axis_index("c")
    chips = [(x, 1 - y), (1 - x, y), (1 - x, 1 - y)]
    blocks = [2 * cx + cy for cx, cy in chips]
    return x, y, c, 2 * x + y, chips, blocks


def _gather_comm(shards, small=None, mid_steps=None):
    na = len(shards)
    ns = 0 if small is None else 1

    def copies(ins, outs, sems):
        h1_send, h1_recv, h2_send, h2_recv, f1_send, f1_recv, f2_send, f2_recv, own_send, own_recv = sems[:10]
        x, y, c, j, chips, blocks = _place()
        sib = (x, y, 1 - c)
        piece = lambda a, p: pl.ds(p * (shards[a].shape[1] // 2), shards[a].shape[1] // 2)

        def remote(src, dst, send, recv, q, to):
            return pltpu.make_async_remote_copy(src_ref=src, dst_ref=dst, send_sem=send.at[q], recv_sem=recv.at[q],
                                                device_id=to, device_id_type=MESH)

        def hop1(a, k, arrival):
            dst = outs[a].at[blocks[k], c] if arrival else outs[a].at[j, c]
            return remote(dst if arrival else ins[a].at[c], dst, h1_send, h1_recv, 2 * a + k, (*chips[k], c))

        def hop2(a, p, arrival):
            ref = outs[a].at[blocks[2] if arrival else blocks[p], c, piece(a, p)]
            return remote(ref, ref, h2_send, h2_recv, 2 * a + p, (*chips[1 - p], c))

        def fwd1(a, k, half):
            ref = outs[a].at[blocks[k], half]
            return remote(ref, ref, f1_send, f1_recv, 2 * a + k, sib)

        def fwd2(a, p, half):
            ref = outs[a].at[blocks[2], half, piece(a, p)]
            return remote(ref, ref, f2_send, f2_recv, 2 * a + p, sib)

        own = [remote(ins[a], outs[a].at[j], own_send, own_recv, a, sib) for a in range(na)]
        small_copies = [remote(ins[na], outs[na].at[j], sems[10], sems[11], k, (*chips[k], c)) for k in range(3 * ns)]
        local = [pltpu.make_async_copy(ins[na], outs[na].at[j], sems[12])] if ns else []
        return hop1, hop2, fwd1, fwd2, own, small_copies, local, c

    pairs = [(a, k) for a in range(na) for k in range(2)]

    def start(ins, outs, sems):
        hop1, _, _, _, own, small_copies, local, _ = copies(ins, outs, sems)
        for cp in local + own + [hop1(a, k, False) for a, k in pairs] + small_copies:
            cp.start()

    def mid(ins, outs, sems):
        hop1, hop2, fwd1, _, _, _, _, c = copies(ins, outs, sems)
        for a, k in pairs:
            hop1(a, k, True).wait_recv()
            hop2(a, k, False).start()
            fwd1(a, k, c).start()

    def finish(ins, outs, sems):
        hop1, hop2, fwd1, fwd2, own, small_copies, local, c = copies(ins, outs, sems)
        for a, p in pairs:
            hop2(a, p, True).wait_recv()
            fwd2(a, p, c).start()
        for cp in small_copies:
            cp.wait()
        for a, k in pairs:
            hop1(a, k, False).wait_send()
            hop2(a, k, False).wait_send()
            fwd1(a, k, c).wait_send()
            fwd1(a, k, 1 - c).wait_recv()
            fwd2(a, k, c).wait_send()
            fwd2(a, k, 1 - c).wait_recv()
        for cp in own + local:
            cp.wait()

    out_shapes = [jax.ShapeDtypeStruct((NCHIP,) + s.shape, s.dtype) for s in shards]
    sems = [pltpu.SemaphoreType.DMA((2 * na,))] * 8 + [pltpu.SemaphoreType.DMA((na,))] * 2
    if ns:
        out_shapes.append(jax.ShapeDtypeStruct((NCHIP,) + small.shape, small.dtype))
        sems += [pltpu.SemaphoreType.DMA((3,)), pltpu.SemaphoreType.DMA((3,)), pltpu.SemaphoreType.DMA]
    if mid_steps is not None:
        return _Comm(list(shards) + [small] * ns, out_shapes, sems, start, finish, mid=mid, mid_steps=mid_steps)

    def forward_and_finish(ins, outs, sems):
        mid(ins, outs, sems)
        finish(ins, outs, sems)

    return _Comm(list(shards) + [small] * ns, out_shapes, sems, start, forward_and_finish)


def _simple_comm(inputs, out_shapes, make_copies, n_sems, aliases=None):
    def start(ins, outs, sems):
        for cp in make_copies(ins, outs, sems):
            cp.start()

    def finish(ins, outs, sems):
        for cp in make_copies(ins, outs, sems):
            cp.wait()

    return _Comm(inputs, out_shapes, [pltpu.SemaphoreType.DMA((n,)) for n in n_sems], start, finish, aliases)


def _pair_comm(grads):
    def make_copies(ins, outs, sems):
        x, y, c, _, _, _ = _place()
        return [pltpu.make_async_remote_copy(
            src_ref=ins[a].at[:, 1 - c], dst_ref=outs[a], send_sem=sems[0].at[a], recv_sem=sems[1].at[a],
            device_id=(x, y, 1 - c), device_id_type=MESH) for a in range(len(grads))]

    out_shapes = [jax.ShapeDtypeStruct(g.shape[:1] + g.shape[2:], F32) for g in grads]
    return _simple_comm(grads, out_shapes, make_copies, [len(grads)] * 2)


def _chips_comm(sums, small=None):
    na = len(sums)
    nk = NCHIP - 1

    def make_copies(ins, outs, sems):
        x, y, c, _, chips, _ = _place()
        copies = [pltpu.make_async_remote_copy(
            src_ref=ins[a].at[k], dst_ref=outs[a].at[k], send_sem=sems[0].at[a * nk + k],
            recv_sem=sems[1].at[a * nk + k], device_id=(*chips[k], c), device_id_type=MESH)
            for a in range(na) for k in range(nk)]
        if small is not None:
            me = 4 * x + 2 * y + c
            for r in range(1, NDEV):
                peer = (1 - x if r & 4 else x, 1 - y if r & 2 else y, 1 - c if r & 1 else c)
                copies.append(pltpu.make_async_remote_copy(
                    src_ref=ins[na], dst_ref=outs[na].at[me], send_sem=sems[2].at[r - 1], recv_sem=sems[3].at[r - 1],
                    device_id=peer, device_id_type=MESH))
            copies.append(pltpu.make_async_copy(ins[na], outs[na].at[me], sems[4].at[0]))
        return copies

    out_shapes = [jax.ShapeDtypeStruct(g.shape, BF16) for g in sums]
    chip_sems = [max(na * nk, 1)] * 2
    if small is None:
        return _simple_comm(sums, out_shapes, make_copies, chip_sems)
    out_shapes.append(jax.ShapeDtypeStruct((NDEV,) + small.shape, F32))
    return _simple_comm(list(sums) + [small], out_shapes, make_copies, chip_sems + [NDEV - 1] * 2 + [1])


def _swap_comm(totals):
    def make_copies(ins, outs, sems):
        x, y, c, _, _, _ = _place()
        return [pltpu.make_async_remote_copy(
            src_ref=outs[a].at[:, c], dst_ref=outs[a].at[:, c], send_sem=sems[0].at[a], recv_sem=sems[1].at[a],
            device_id=(x, y, 1 - c), device_id_type=MESH) for a in range(len(totals))]

    out_shapes = [jax.ShapeDtypeStruct(t.shape, F32) for t in totals]
    return _simple_comm(totals, out_shapes, make_copies, [len(totals)] * 2, aliases={a: a for a in range(len(totals))})


def _pair_sum(place, grad, landed, name):
    _, _, rows, cols = grad.shape

    def body(place_ref, g_ref, l_ref, o_ref):
        o_ref[...] = (g_ref[...] + l_ref[...]).astype(BF16)

    other = lambda k, p: jnp.bitwise_xor(p[0], k + 1)
    return pl.pallas_call(
        body, name=name,
        grid_spec=pltpu.PrefetchScalarGridSpec(
            num_scalar_prefetch=1, grid=(NCHIP - 1,),
            in_specs=[pl.BlockSpec((None, None, rows, cols), lambda k, p: (other(k, p), p[1], 0, 0)),
                      pl.BlockSpec((None, rows, cols), lambda k, p: (other(k, p), 0, 0))],
            out_specs=pl.BlockSpec((None, rows, cols), lambda k, p: (k, 0, 0))),
        out_shape=jax.ShapeDtypeStruct((NCHIP - 1, rows, cols), BF16), compiler_params=_cparams(1),
    )(place, grad, landed)


def _chip_sum(place, grad, landed, parts, layer, n_layers, name, prev=None):
    _, _, rows, cols = grad.shape
    tr = max([t for t in range(2 * SUBLANES, rows + 1, 2 * SUBLANES)
              if rows % t == 0 and t * cols * 4 <= CHIP_SUM_TILE_BYTES] or [rows])

    def body(*refs):
        g_ref, l_ref, p_ref, o_ref = refs[1], refs[2], refs[3], refs[-1]
        tot = g_ref[...] + l_ref[...]
        for k in range(NCHIP - 1):
            tot = tot + p_ref[k].astype(F32)
        o_ref[...] = tot

    in_specs = [pl.BlockSpec((None, None, tr, cols), lambda i, p: (p[0], p[1], i, 0)),
                pl.BlockSpec((None, tr, cols), lambda i, p: (p[0], i, 0)),
                pl.BlockSpec((NCHIP - 1, tr, cols), lambda i, p: (0, i, 0))]
    args = [place, grad, landed, parts]
    if prev is not None:
        in_specs.append(ANY)
        args.append(prev)
    return pl.pallas_call(
        body, name=name,
        grid_spec=pltpu.PrefetchScalarGridSpec(
            num_scalar_prefetch=1, grid=(rows // tr,), in_specs=in_specs,
            out_specs=pl.BlockSpec((None, None, tr, cols), lambda i, p: (layer, p[1], i, 0))),
        out_shape=jax.ShapeDtypeStruct((n_layers, 2, rows, cols), F32),
        input_output_aliases={} if prev is None else {4: 0}, compiler_params=_cparams(1),
    )(*args)


def _sum_devices(parts, name):
    def body(p_ref, o_ref):
        tot = p_ref[0]
        for d in range(1, NDEV):
            tot = tot + p_ref[d]
        o_ref[...] = tot

    return pl.pallas_call(
        body, name=name, in_specs=[pl.BlockSpec(parts.shape, lambda: (0, 0, 0))],
        out_specs=pl.BlockSpec(parts.shape[1:], lambda: (0, 0)), out_shape=jax.ShapeDtypeStruct(parts.shape[1:], F32),
    )(parts)


def _pack(parts):
    rows = []
    for p in parts:
        p = p.reshape(-1, LANES)
        rows.append(jnp.pad(p, ((0, -p.shape[0] % SUBLANES), (0, 0))))
    return jnp.concatenate(rows, axis=0)


def _unpack(buf, shapes, lead=()):
    out, r0 = [], 0
    nl = len(lead)
    for shp in shapes:
        nrow = 1
        for d in shp:
            nrow *= d
        nrow //= LANES
        out.append(buf[(slice(None),) * nl + (slice(r0, r0 + nrow),)].reshape(lead + tuple(shp)))
        r0 += nrow + (-nrow % SUBLANES)
    return out


WEIGHT_ORDER = ("norm_mix_even", "w_in", "conv_a", "ln_a_g", "ln_a_b", "conv_b", "w_out", "norm_mix_odd", "w_pool",
                "pool_scale", "norm_ffn", "w_up", "conv_ffn_w", "w_down", "norm_final")
BIG = ("w_in", "w_out", "w_pool", "w_up", "w_down")
HALF = {"w_in": (D // 2, NZ // NCHIP), "w_out": (2 * A // NCHIP // 2, D), "w_pool": (PG // 2, PG),
        "w_up": (D // 2, FF2 // NCHIP), "w_down": (FF // NCHIP // 2, D)}
SMALL_SHARDED = ("conv_a", "conv_b", "conv_ffn_w", "norm_mix_odd", "pool_scale")
SMALL_ALL = ("norm_mix_even", "conv_a", "ln_a_g", "ln_a_b", "conv_b", "norm_mix_odd", "pool_scale", "norm_ffn", "conv_ffn_w",
             "norm_final")
SMALL_EARLY = ("norm_mix_odd", "pool_scale", "norm_ffn", "conv_ffn_w", "norm_final", "loss", "conv_a", "ln_a_g", "ln_a_b", "conv_b")
SMALL_LATE = ("norm_mix_even",)


def kernel(x, norm_mix_even, w_in, conv_a, ln_a_g, ln_a_b, conv_b, w_out, norm_mix_odd, w_pool, pool_scale, norm_ffn, w_up, conv_ffn_w, w_down, norm_final, loss_target, m_norm_mix_even, m_w_in, m_conv_a, m_ln_a_g, m_ln_a_b, m_conv_b, m_w_out, m_norm_mix_odd, m_w_pool, m_pool_scale, m_norm_ffn, m_w_up, m_conv_ffn_w, m_w_down, m_norm_final, v_norm_mix_even, v_w_in, v_conv_a, v_ln_a_g, v_ln_a_b, v_conv_b, v_w_out, v_norm_mix_odd, v_w_pool, v_pool_scale, v_norm_ffn, v_w_up, v_conv_ffn_w, v_w_down, v_norm_final):
    w = dict(norm_mix_even=norm_mix_even, w_in=w_in, conv_a=conv_a, ln_a_g=ln_a_g, ln_a_b=ln_a_b, conv_b=conv_b, w_out=w_out,
             norm_mix_odd=norm_mix_odd, w_pool=w_pool, pool_scale=pool_scale, norm_ffn=norm_ffn, w_up=w_up,
             conv_ffn_w=conv_ffn_w, w_down=w_down, norm_final=norm_final)
    m = dict(norm_mix_even=m_norm_mix_even, w_in=m_w_in, conv_a=m_conv_a, ln_a_g=m_ln_a_g, ln_a_b=m_ln_a_b, conv_b=m_conv_b,
             w_out=m_w_out, norm_mix_odd=m_norm_mix_odd, w_pool=m_w_pool, pool_scale=m_pool_scale, norm_ffn=m_norm_ffn,
             w_up=m_w_up, conv_ffn_w=m_conv_ffn_w, w_down=m_w_down, norm_final=m_norm_final)
    v = dict(norm_mix_even=v_norm_mix_even, w_in=v_w_in, conv_a=v_conv_a, ln_a_g=v_ln_a_g, ln_a_b=v_ln_a_b, conv_b=v_conv_b,
             w_out=v_w_out, norm_mix_odd=v_norm_mix_odd, w_pool=v_w_pool, pool_scale=v_pool_scale, norm_ffn=v_norm_ffn,
             w_up=v_w_up, conv_ffn_w=v_conv_ffn_w, w_down=v_w_down, norm_final=v_norm_final)
    chip = 2 * lax.axis_index("x") + lax.axis_index("y")
    place = jnp.stack([chip, lax.axis_index("c")]).astype(jnp.int32)

    half = lambda a, name: a.astype(BF16).reshape((2,) + HALF[name])
    shard = {"w_in": half(w_in[0], "w_in"), "w_out": half(w_out[0], "w_out")}
    small_shapes = [w[k].shape[-2:] if w[k].ndim == 3 and k != "conv_ffn_w" else (w[k].size // w[k].shape[-1], w[k].shape[-1])
                    for k in SMALL_SHARDED]
    rep = dict(norm_mix_even=norm_mix_even, ln_a_g=ln_a_g, ln_a_b=ln_a_b, norm_ffn=norm_ffn, norm_final=norm_final.reshape(1, D))
    as3 = lambda a: a.reshape(a.shape[0], -1, a.shape[-1])

    def adamw(k, total):
        outs, _ = _adamw_big(as3(w[k]), as3(total), as3(m[k]), as3(v[k]), ADAMW_ROWS[k], "adamw_" + k)
        return [a.reshape(w[k].shape) for a in outs]

    grad_x, done, summed = _train_step(
        x[0], loss_target[0], place, shard, (w_up, w_down, w_pool[0]), rep, _pack([w[k] for k in SMALL_SHARDED]), small_shapes,
        adamw)
    loss = summed["loss"][0, 0]

    grad, delta, new_m, new_v = ({k: done[k][q] for k in BIG} for q in range(4))
    for k in SMALL_ALL:
        gsum = summed[k]
        if k in SMALL_SHARDED:
            cols = w[k].shape[-1]
            gsum = lax.dynamic_slice_in_dim(gsum, chip * cols, cols, axis=gsum.ndim - 1)
        grad[k] = gsum.reshape(w[k].shape)

    as2 = lambda a: a.reshape(1, -1) if a.ndim == 1 else a[0] if a.ndim == 3 and a.shape[0] == 1 else a
    ds, ms, vs = _adamw_small(*[[as2(t[k]) for k in SMALL_ALL] for t in (w, grad, m, v)])
    for k, d2, m2, v2 in zip(SMALL_ALL, ds, ms, vs):
        delta[k], new_m[k], new_v[k] = (a.reshape(w[k].shape) for a in (d2, m2, v2))

    return (loss, grad_x[None], *[grad[k] for k in WEIGHT_ORDER], *[delta[k] for k in WEIGHT_ORDER],
            *[new_m[k] for k in WEIGHT_ORDER], *[new_v[k] for k in WEIGHT_ORDER])
```

```python
import functools

import jax
import jax.numpy as jnp
from jax import lax
from jax.experimental import pallas as pl
from jax.experimental.pallas import tpu as pltpu

F32, BF16 = jnp.float32, jnp.bfloat16

D = 1024
A = 512
NZ = 5 * A
FF = 2816
FF2 = 2 * FF
NCHIP = 4
NDEV = 8
K_A, K_S = 31, 3
POOL_WINDOWS = (2, 4, 8, 16)
PG = D // len(POOL_WINDOWS)
RMS_EPS, LN_EPS = 1e-6, 1e-5
ADAM_LR, ADAM_B1, ADAM_B2, ADAM_EPS, ADAM_WD, ADAM_STEP = 0.001, 0.9, 0.999, 1e-08, 0.01, 10

HALO_A, HALO_S, HALO_P = 32, 8, 16
SUBLANES = 8
LANES = 128
VMEM_LIMIT_BYTES = 56 * 1024 * 1024

TS_MIX = 512
TS_MIXB = 512
TS_FFN = 256
TS_POOL = 512
TS_MM = 512
TS_WGRAD = 1024
ADAMW_ROWS = {"w_in": 512, "w_out": 256, "w_pool": 256, "w_up": 256, "w_down": 352}
R_CHUNK = 64
GATHER_MID_STEPS = {"mix0_fwd": 2, "ffn0_fwd": 5}

MESH = pl.DeviceIdType.MESH
ANY = pl.BlockSpec(memory_space=pl.ANY)
NT_DIMS = (((1,), (1,)), ((), ()))
TN_DIMS = (((0,), (0,)), ((), ()))


def _cparams(n_axes):
    return pltpu.CompilerParams(dimension_semantics=("arbitrary",) * n_axes, vmem_limit_bytes=VMEM_LIMIT_BYTES)


def _const(shape):
    nd = len(shape)
    return pl.BlockSpec(shape, lambda *_: (0,) * nd, pipeline_mode=pl.Buffered(1))


def _sigmoid(v):
    return 1.0 / (1.0 + jnp.exp(-v))


def _rsqrt_mean_sq(x):
    return lax.rsqrt(jnp.mean(x * x, axis=-1, keepdims=True) + RMS_EPS)


def _rms_bwd(dh, xh, r, g):
    dxh = dh * g
    return r * (dxh - xh * jnp.mean(dxh * xh, axis=-1, keepdims=True))


def _shifted(buf_ref, row0, rows, col0, width, offsets):
    lo = (min(offsets) // SUBLANES) * SUBLANES
    hi = -(-(max(offsets) + rows) // SUBLANES) * SUBLANES
    start = row0 + lo if isinstance(row0, int) else pl.multiple_of(row0 + lo, SUBLANES)
    win = buf_ref[pl.ds(start, hi - lo), col0:col0 + width]
    out = {}
    for res in sorted({(o - lo) % SUBLANES for o in offsets}):
        qs = {o: (o - lo) // SUBLANES for o in offsets if (o - lo) % SUBLANES == res}
        base = pltpu.roll(win, hi - lo - res, 0) if res else win
        for o, q in qs.items():
            out[o] = base[SUBLANES * q:SUBLANES * q + rows, :]
    return out


def _rowsum8(v):
    acc = v[0:SUBLANES, :]
    for r in range(SUBLANES, v.shape[0], SUBLANES):
        acc = acc + v[r:r + SUBLANES, :]
    return acc


def _taps(w_ref, sh, offsets, col0, width):
    acc = None
    for k, o in enumerate(offsets):
        term = w_ref[k:k + 1, col0:col0 + width] * sh[o]
        acc = term if acc is None else acc + term
    return acc


def _window_bases(buf_ref, row0, rows, col0, width, offsets):
    lo = (min(offsets) // SUBLANES) * SUBLANES
    hi = -(-(max(offsets) + rows) // SUBLANES) * SUBLANES
    start = row0 + lo if isinstance(row0, int) else pl.multiple_of(row0 + lo, SUBLANES)
    win = buf_ref[pl.ds(start, hi - lo), col0:col0 + width]
    for res in sorted({(o - lo) % SUBLANES for o in offsets}):
        taps = [(k, (o - lo) // SUBLANES * SUBLANES) for k, o in enumerate(offsets) if (o - lo) % SUBLANES == res]
        yield (pltpu.roll(win, hi - lo - res, 0) if res else win), taps


def _conv_acc(buf_ref, w_ref, row0, rows, col0, width, offsets):
    acc = None
    for base, taps in _window_bases(buf_ref, row0, rows, col0, width, offsets):
        for k, q in taps:
            term = w_ref[k:k + 1, col0:col0 + width] * base[q:q + rows, :]
            acc = term if acc is None else acc + term
    return acc


def _conv_corr(buf_ref, w_ref, other, acc_ref, row0, rows, col0, width, offsets):
    acc = None
    for base, taps in _window_bases(buf_ref, row0, rows, col0, width, offsets):
        for k, q in taps:
            sl = base[q:q + rows, :]
            term = w_ref[k:k + 1, col0:col0 + width] * sl
            acc = term if acc is None else acc + term
            acc_ref[SUBLANES * k:SUBLANES * (k + 1), col0:col0 + width] += _rowsum8(sl * other)
    return acc


def _finish_tap_sums(acc_ref, out_ref, n_taps):
    for k in range(n_taps):
        out_ref[k:k + 1, :] = jnp.sum(acc_ref[SUBLANES * k:SUBLANES * (k + 1), :], axis=0, keepdims=True)


class _Comm:
    def __init__(self, inputs, out_shapes, sems, start, finish, aliases=None, mid=None, mid_steps=1):
        self.inputs, self.out_shapes, self.sems = list(inputs), list(out_shapes), list(sems)
        self.start, self.finish, self.mid, self.aliases = start, finish, mid, dict(aliases or {})
        self.mid_steps = mid_steps


def _join_comm(a, b):
    ni, no, ns = len(a.inputs), len(a.out_shapes), len(a.sems)

    def both(name):
        def run(ins, outs, sems):
            if getattr(a, name) is not None:
                getattr(a, name)(ins[:ni], outs[:no], sems[:ns])
            if getattr(b, name) is not None:
                getattr(b, name)(ins[ni:], outs[no:], sems[ns:])
        return run

    aliases = {**a.aliases, **{ni + i: no + o for i, o in b.aliases.items()}}
    mid = both("mid") if (a.mid is not None or b.mid is not None) else None
    return _Comm(a.inputs + b.inputs, a.out_shapes + b.out_shapes, a.sems + b.sems, both("start"), both("finish"), aliases, mid)


def _call(body, *, name, grid, in_specs, out_specs, out_shape, args, scratch=(), comm=None, aliases=None):
    n_in, n_out, n_scr, n_axes = len(in_specs), len(out_specs), len(scratch), len(grid)
    params = pltpu.CompilerParams(dimension_semantics=("arbitrary",) * n_axes, vmem_limit_bytes=VMEM_LIMIT_BYTES)
    aliases = dict(aliases or {})
    if comm is None:
        outs = pl.pallas_call(body, name=name, grid=grid, in_specs=list(in_specs), out_specs=list(out_specs),
                              out_shape=list(out_shape), scratch_shapes=list(scratch), input_output_aliases=aliases,
                              compiler_params=params)(*args)
        return list(outs), []
    ci, co = len(comm.inputs), len(comm.out_shapes)

    def wrapped(*refs):
        k_in, c_in = refs[:n_in], refs[n_in:n_in + ci]
        o0 = n_in + ci
        k_out, c_out = refs[o0:o0 + n_out], refs[o0 + n_out:o0 + n_out + co]
        s0 = o0 + n_out + co
        k_scr, c_sem = refs[s0:s0 + n_scr], refs[s0 + n_scr:]
        first = pl.program_id(0) == 0
        last = pl.program_id(0) == grid[0] - 1
        for ax in range(1, n_axes):
            first = jnp.logical_and(first, pl.program_id(ax) == 0)
            last = jnp.logical_and(last, pl.program_id(ax) == grid[ax] - 1)

        @pl.when(first)
        def _():
            comm.start(c_in, c_out, c_sem)

        mid_early = comm.mid is not None and n_axes == 1 and grid[0] > comm.mid_steps
        if mid_early:
            @pl.when(pl.program_id(0) == grid[0] - 1 - comm.mid_steps)
            def _():
                comm.mid(c_in, c_out, c_sem)

        body(*k_in, *k_out, *k_scr)

        @pl.when(last)
        def _():
            if comm.mid is not None and not mid_early:
                comm.mid(c_in, c_out, c_sem)
            comm.finish(c_in, c_out, c_sem)

    outs = pl.pallas_call(
        wrapped, name=name, grid=grid, in_specs=list(in_specs) + [ANY] * ci, out_specs=list(out_specs) + [ANY] * co,
        out_shape=list(out_shape) + comm.out_shapes, scratch_shapes=list(scratch) + comm.sems,
        input_output_aliases={**aliases, **{n_in + i: n_out + o for i, o in comm.aliases.items()}}, compiler_params=params,
    )(*args, *comm.inputs)
    return list(outs[:n_out]), list(outs[n_out:])


def _run_comm(comm, name):
    ci, co = len(comm.inputs), len(comm.out_shapes)

    def body(*refs):
        c_in, c_out, c_sem = refs[:ci], refs[ci:ci + co], refs[ci + co:]
        comm.start(c_in, c_out, c_sem)
        if comm.mid is not None:
            comm.mid(c_in, c_out, c_sem)
        comm.finish(c_in, c_out, c_sem)

    return list(pl.pallas_call(body, name=name, in_specs=[ANY] * ci, out_specs=[ANY] * co, out_shape=comm.out_shapes,
                               scratch_shapes=comm.sems, input_output_aliases=comm.aliases)(*comm.inputs))


def _mix0_fwd(x, g, w_in, conv_a, ln_g, ln_b, conv_b, w_out, comm=None):
    s = x.shape[0]
    ts = min(TS_MIX, s)
    n = s // ts
    bw = NZ // NCHIP
    offs_a = [HALO_A - (K_A - 1) + k for k in range(K_A)]
    offs_s = [HALO_S - (K_S - 1) + k for k in range(K_S)]

    def body(x_ref, g_ref, win_ref, ca_ref, lg_ref, lb_ref, cb_ref, wout_ref,
             h_ref, z_ref, ac_ref, bconv_buf, cat_ref, x1_ref, glu_buf, cv_buf):
        i = pl.program_id(0)

        @pl.when(i == 0)
        def _():
            glu_buf[0:HALO_A, :] = jnp.zeros((HALO_A, A), F32)
            cv_buf[0:HALO_S, :] = jnp.zeros((HALO_S, A), F32)

        xv = x_ref[...]
        h = (xv * _rsqrt_mean_sq(xv) * g_ref[...]).astype(BF16)
        h_ref[...] = h
        for j in range(NCHIP):
            z_ref[:, j * bw:(j + 1) * bw] = jnp.dot(h, win_ref[j], preferred_element_type=F32)
        glu_buf[HALO_A:HALO_A + ts, :] = z_ref[:, 0:A] * _sigmoid(z_ref[:, A:2 * A])
        cv_buf[HALO_S:HALO_S + ts, :] = z_ref[:, 3 * A:4 * A] * z_ref[:, 4 * A:5 * A]

        def chunk(ci, carry):
            r0 = pl.multiple_of(ci * R_CHUNK, R_CHUNK)
            for c0 in range(0, A, LANES):
                ac_ref[pl.ds(r0, R_CHUNK), c0:c0 + LANES] = _conv_acc(glu_buf, ca_ref, r0, R_CHUNK, c0, LANES, offs_a)
                bconv_buf[pl.ds(r0, R_CHUNK), c0:c0 + LANES] = _conv_acc(cv_buf, cb_ref, r0, R_CHUNK, c0, LANES, offs_s)
            return carry

        lax.fori_loop(0, ts // R_CHUNK, chunk, 0)
        glu_buf[0:HALO_A, :] = glu_buf[ts:ts + HALO_A, :]
        cv_buf[0:HALO_S, :] = cv_buf[ts:ts + HALO_S, :]

        ac = ac_ref[...]
        xc = ac - jnp.mean(ac, axis=-1, keepdims=True)
        xn = xc * lax.rsqrt(jnp.mean(xc * xc, axis=-1, keepdims=True) + LN_EPS)
        ln = xn * lg_ref[...] + lb_ref[...]
        cat_ref[:, 0:A] = (ln * _sigmoid(ln)).astype(BF16)
        cat_ref[:, A:2 * A] = (z_ref[:, 2 * A:3 * A] * bconv_buf[...]).astype(BF16)
        x1_ref[...] = xv + jnp.dot(cat_ref[...], wout_ref[...], preferred_element_type=F32)

    tile = lambda w: pl.BlockSpec((ts, w), lambda i: (i, 0))
    return _call(
        body, name="mix0_fwd", grid=(n,), comm=comm, args=(x, g, w_in, conv_a, ln_g, ln_b, conv_b, w_out),
        in_specs=[tile(D), _const((1, D)), _const((NCHIP, D, bw)), _const((K_A, A)), _const((1, A)), _const((1, A)),
                  _const((K_S, A)), _const((2 * A, D))],
        out_specs=[tile(D), tile(NZ), tile(A), tile(A), tile(2 * A), tile(D)],
        out_shape=[jax.ShapeDtypeStruct((s, D), BF16), jax.ShapeDtypeStruct((s, NZ), F32), jax.ShapeDtypeStruct((s, A), F32),
                   jax.ShapeDtypeStruct((s, A), F32), jax.ShapeDtypeStruct((s, 2 * A), BF16), jax.ShapeDtypeStruct((s, D), F32)],
        scratch=[pltpu.VMEM((HALO_A + ts, A), F32), pltpu.VMEM((HALO_S + ts, A), F32)])


def _ffn_fwd(x, g, w_up, wc, w_down, name, comm=None, head=None):
    s = x.shape[0]
    ts = min(TS_FFN, s)
    n = s // ts
    bw = FF2 // NCHIP
    rows = 32
    offs = [HALO_S - (K_S - 1) + k for k in range(K_S)]
    n_in = 5 + (2 if head else 0)

    def body(*refs):
        x_ref, g_ref, wup_ref, wc_ref, wdn_ref = refs[:5]
        h_ref, u0_ref, u_ref, act_ref, xo_ref = refs[n_in:n_in + 5]
        cbuf = refs[-1]
        i = pl.program_id(0)

        @pl.when(i == 0)
        def _():
            cbuf[0:HALO_S, :] = jnp.zeros((HALO_S, FF2), F32)
            if head:
                refs[n_in + 5][...] = jnp.zeros((1, D), F32)
                refs[n_in + 6][...] = jnp.zeros((1, LANES), F32)

        xv = x_ref[...]
        h = (xv * _rsqrt_mean_sq(xv) * g_ref[...]).astype(BF16)
        h_ref[...] = h
        f = None
        for p in range(NCHIP // 2):
            for j in (p, NCHIP // 2 + p):
                zc = jnp.dot(h, wup_ref[j], preferred_element_type=F32)
                u0_ref[:, j * bw:(j + 1) * bw] = zc.astype(BF16)
                cbuf[HALO_S:HALO_S + ts, j * bw:(j + 1) * bw] = zc
            for r0 in range(0, ts, rows):
                for c0 in range(p * bw, (p + 1) * bw, LANES):
                    ug = _taps(wc_ref, _shifted(cbuf, r0, rows, c0, LANES, offs), offs, c0, LANES)
                    uv = _taps(wc_ref, _shifted(cbuf, r0, rows, FF + c0, LANES, offs), offs, FF + c0, LANES)
                    u_ref[r0:r0 + rows, c0:c0 + LANES] = ug
                    u_ref[r0:r0 + rows, FF + c0:FF + c0 + LANES] = uv
                    act_ref[r0:r0 + rows, c0:c0 + LANES] = (ug * _sigmoid(ug) * uv).astype(BF16)
            fp = jnp.dot(act_ref[:, p * bw:(p + 1) * bw], wdn_ref[p * bw:(p + 1) * bw, :], preferred_element_type=F32)
            f = fp if f is None else f + fp
        cbuf[0:HALO_S, :] = cbuf[ts:ts + HALO_S, :]
        if not head:
            xo_ref[...] = xv + f
        else:
            gf_ref, t_ref, dgf_ref, loss_ref = refs[5], refs[6], refs[n_in + 5], refs[n_in + 6]
            xo = xv + f
            r = _rsqrt_mean_sq(xo)
            xh = xo * r
            gv = gf_ref[...]
            err = xh * gv - t_ref[...]
            loss_ref[...] += jnp.sum(jnp.sum(err * err, axis=1, keepdims=True), axis=0, keepdims=True) * (0.5 / D)
            dy = err * (1.0 / D)
            dgf_ref[...] += jnp.sum(dy * xh, axis=0, keepdims=True)
            xo_ref[...] = _rms_bwd(dy, xh, r, gv)

    tile = lambda w: pl.BlockSpec((ts, w), lambda i: (i, 0))
    one_row = lambda w: pl.BlockSpec((1, w), lambda i: (0, 0))
    return _call(
        body, name=name, grid=(n,), comm=comm, args=(x, g, w_up, wc, w_down) + (tuple(head) if head else ()),
        in_specs=[tile(D), _const((1, D)), _const((NCHIP, D, bw)), _const((K_S, FF2)), _const((FF, D))]
        + ([_const((1, D)), tile(D)] if head else []),
        out_specs=[tile(D), tile(FF2), tile(FF2), tile(FF), tile(D)] + ([one_row(D), one_row(LANES)] if head else []),
        out_shape=[jax.ShapeDtypeStruct((s, D), BF16), jax.ShapeDtypeStruct((s, FF2), BF16), jax.ShapeDtypeStruct((s, FF2), F32),
                   jax.ShapeDtypeStruct((s, FF), BF16), jax.ShapeDtypeStruct((s, D), F32)]
        + ([jax.ShapeDtypeStruct((1, D), F32), jax.ShapeDtypeStruct((1, LANES), F32)] if head else []),
        scratch=[pltpu.VMEM((HALO_S + ts, FF2), F32)])


def _pool_windows(hbuf, pbuf, tile_row0, ts):
    def chunk(ci, carry):
        r0 = pl.multiple_of(ci * R_CHUNK, R_CHUNK)
        t1 = (tile_row0 + r0 + lax.broadcasted_iota(jnp.int32, (R_CHUNK, 1), 0) + 1).astype(F32)
        for gi, w in enumerate(POOL_WINDOWS):
            cnt = jnp.minimum(t1, float(w))
            offs = [HALO_P - jj for jj in range(w)]
            for c0 in range(gi * PG, (gi + 1) * PG, LANES):
                sh = _shifted(hbuf, r0, R_CHUNK, c0, LANES, offs)
                tot = sh[offs[0]]
                for o in offs[1:]:
                    tot = tot + sh[o]
                pbuf[pl.ds(r0, R_CHUNK), c0:c0 + LANES] = (tot / cnt - sh[HALO_P]).astype(BF16)
        return carry

    lax.fori_loop(0, ts // R_CHUNK, chunk, 0)


def _assemble_wpool(wp_ref, wps):
    rb = PG // NCHIP
    for gi in range(len(POOL_WINDOWS)):
        for j in range(NCHIP):
            wps[gi, j * rb:(j + 1) * rb, :] = wp_ref[j, gi]


def _pool_fwd(x, g, w_pool, scale):
    s = x.shape[0]
    ts = min(TS_POOL, s)
    n = s // ts
    ng = len(POOL_WINDOWS)

    def body(x_ref, g_ref, wp_ref, sc_ref, xo_ref, hbuf, pbuf, wps):
        i = pl.program_id(0)

        @pl.when(i == 0)
        def _():
            hbuf[0:HALO_P, :] = jnp.zeros((HALO_P, D), F32)
            _assemble_wpool(wp_ref, wps)

        xv = x_ref[...]
        hbuf[HALO_P:HALO_P + ts, :] = xv * _rsqrt_mean_sq(xv) * g_ref[...]
        _pool_windows(hbuf, pbuf, i * ts, ts)
        hbuf[0:HALO_P, :] = hbuf[ts:ts + HALO_P, :]
        for gi in range(ng):
            cols = slice(gi * PG, (gi + 1) * PG)
            y = jnp.dot(pbuf[:, cols], wps[gi], preferred_element_type=F32)
            xo_ref[:, cols] = xv[:, cols] + y * sc_ref[:, cols]

    tile = pl.BlockSpec((ts, D), lambda i: (i, 0))
    return pl.pallas_call(
        body, name="pool_fwd", grid=(n,),
        in_specs=[tile, _const((1, D)), _const((NCHIP, ng, PG // NCHIP, PG)), _const((1, D))],
        out_specs=tile, out_shape=jax.ShapeDtypeStruct((s, D), F32),
        scratch_shapes=[pltpu.VMEM((HALO_P + ts, D), F32), pltpu.VMEM((ts, D), BF16), pltpu.VMEM((ng, PG, PG), BF16)],
        compiler_params=_cparams(1),
    )(x, g, w_pool, scale)


def _ffn_bwd_a(dxo, u, u0, wc, w_down, name, comm=None):
    s = dxo.shape[0]
    ts = min(TS_FFN, s)
    n = s // ts
    cw = FF2 // NCHIP
    rows = 32
    lw2 = 2 * LANES
    boffs = [K_S - 1 - k for k in range(K_S)]

    def body(dxo_ref, u_ref, u0_ref, wc_ref, wdn_ref, du0_ref, dwc_ref, dubuf, dact, dwacc):
        i = pl.program_id(0)

        @pl.when(i == 0)
        def _():
            dubuf[ts:ts + HALO_S, :] = jnp.zeros((HALO_S, FF2), F32)
            dwacc[...] = jnp.zeros(dwacc.shape, F32)

        df = dxo_ref[...].astype(BF16)
        for cg in range(0, FF, cw):
            dact[...] = lax.dot_general(df, wdn_ref[cg:cg + cw, :], NT_DIMS, preferred_element_type=F32)

            def chunk(ci, carry, cg=cg):
                rs = pl.ds(pl.multiple_of(ci * rows, rows), rows)
                for c in range(cg, cg + cw, LANES):
                    ug, uv = u_ref[rs, c:c + LANES], u_ref[rs, FF + c:FF + c + LANES]
                    sg = _sigmoid(ug)
                    da = dact[rs, c - cg:c - cg + LANES]
                    gs = ug * sg
                    dubuf[rs, c:c + LANES] = da * uv * (sg + gs * (1.0 - sg))
                    dubuf[rs, FF + c:FF + c + LANES] = da * gs
                return carry

            lax.fori_loop(0, ts // rows, chunk, 0)

        def chunk2(ci, carry):
            r0 = pl.multiple_of(ci * rows, rows)
            rs = pl.ds(r0, rows)
            for c in range(0, FF2, lw2):
                sh = _shifted(dubuf, r0, rows, c, lw2, boffs)
                du0_ref[rs, c:c + lw2] = _taps(wc_ref, sh, boffs, c, lw2).astype(BF16)
                u0v = u0_ref[rs, c:c + lw2].astype(F32)
                for k, o in enumerate(boffs):
                    dwacc[SUBLANES * k:SUBLANES * (k + 1), c:c + lw2] += _rowsum8(sh[o] * u0v)
            return carry

        lax.fori_loop(0, ts // rows, chunk2, 0)
        dubuf[ts:ts + HALO_S, :] = dubuf[0:HALO_S, :]

        @pl.when(i == n - 1)
        def _():
            _finish_tap_sums(dwacc, dwc_ref, K_S)

    rev = lambda w: pl.BlockSpec((ts, w), lambda i: (n - 1 - i, 0))
    return _call(
        body, name=name, grid=(n,), comm=comm, args=(dxo, u, u0, wc, w_down),
        in_specs=[rev(D), rev(FF2), rev(FF2), _const((K_S, FF2)), _const((FF, D))],
        out_specs=[rev(FF2), pl.BlockSpec((K_S, FF2), lambda i: (0, 0))],
        out_shape=[jax.ShapeDtypeStruct((s, FF2), BF16), jax.ShapeDtypeStruct((K_S, FF2), F32)],
        scratch=[pltpu.VMEM((ts + HALO_S, FF2), F32), pltpu.VMEM((ts, cw), F32), pltpu.VMEM((SUBLANES * K_S, FF2), F32)])


def _nt_rms_bwd(dy, w, x, g, dres, name, comm=None, tiles=None, dx_so_far=None):
    s = x.shape[0]
    ts = min(TS_MM, s)
    first, n = (0, s // ts) if tiles is None else tiles
    nw = dy.shape[1]
    bw = nw // NCHIP

    def body(dy_ref, w_ref, x_ref, g_ref, dres_ref, dx_ref, dg_ref):
        i = pl.program_id(0)

        @pl.when(i == 0)
        def _():
            dg_ref[...] = jnp.zeros((1, D), F32)

        dh = lax.dot_general(dy_ref[:, 0:bw], w_ref[0], NT_DIMS, preferred_element_type=F32)
        for j in range(1, NCHIP):
            dh = dh + lax.dot_general(dy_ref[:, j * bw:(j + 1) * bw], w_ref[j], NT_DIMS, preferred_element_type=F32)
        xv = x_ref[...]
        r = _rsqrt_mean_sq(xv)
        xh = xv * r
        dg_ref[...] += jnp.sum(dh * xh, axis=0, keepdims=True)
        dx_ref[...] = dres_ref[...] + _rms_bwd(dh, xh, r, g_ref[...])

    def body_with_alias(dy_ref, w_ref, x_ref, g_ref, dres_ref, _, dx_ref, dg_ref):
        body(dy_ref, w_ref, x_ref, g_ref, dres_ref, dx_ref, dg_ref)

    tile = lambda wd: pl.BlockSpec((ts, wd), lambda i: (first + i, 0))
    in_specs = [tile(nw), _const((NCHIP, D, bw)), tile(D), _const((1, D)), tile(D)]
    more = dx_so_far is not None
    return _call(
        body_with_alias if more else body, name=name, grid=(n,), comm=comm,
        args=(dy, w, x, g, dres) + ((dx_so_far,) if more else ()), in_specs=in_specs + [ANY] * more,
        out_specs=[tile(D), pl.BlockSpec((1, D), lambda i: (0, 0))], aliases={5: 0} if more else None,
        out_shape=[jax.ShapeDtypeStruct((s, D), F32), jax.ShapeDtypeStruct((1, D), F32)])


def _wgrad(a, b, n_blocks, name, comm=None):
    s, m = a.shape
    bw = b.shape[1] // n_blocks
    tk = min(TS_WGRAD, s)
    group = min(g for g in range(1, n_blocks + 1) if n_blocks % g == 0 and (g * bw >= m or g == n_blocks))

    def body(a_ref, b_ref, o_ref):
        part = lambda j: lax.dot_general(a_ref[...], b_ref[:, j * bw:(j + 1) * bw].astype(BF16), TN_DIMS,
                                         preferred_element_type=F32)

        @pl.when(pl.program_id(1) == 0)
        def _():
            for j in range(group):
                o_ref[j] = part(j)

        @pl.when(pl.program_id(1) != 0)
        def _():
            for j in range(group):
                o_ref[j] += part(j)

    (out,), comm_out = _call(
        body, name=name, grid=(n_blocks // group, s // tk), comm=comm, args=(a, b),
        in_specs=[pl.BlockSpec((tk, m), lambda j, k: (k, 0)), pl.BlockSpec((tk, group * bw), lambda j, k: (k, j))],
        out_specs=[pl.BlockSpec((group, m, bw), lambda j, k: (j, 0, 0))],
        out_shape=[jax.ShapeDtypeStruct((n_blocks, m, bw), F32)])
    return out, comm_out


def _pool_bwd(dxo, x, g, w_pool, scale, comm=None):
    s = x.shape[0]
    ts = min(TS_POOL, s)
    n = s // ts
    ng = len(POOL_WINDOWS)
    rb = PG // NCHIP

    def body(dxo_ref, x_ref, halo_ref, g_ref, wp_ref, sc_ref, dx_ref, dwp_ref, dsc_ref, dg_ref,
             hbuf, pbuf, qbuf, dhbuf, wps, dwacc):
        i = pl.program_id(0)
        j = n - 1 - i

        @pl.when(i == 0)
        def _():
            qbuf[ts:ts + HALO_P, :] = jnp.zeros((HALO_P, D), F32)
            dwacc[...] = jnp.zeros(dwacc.shape, F32)
            dsc_ref[...] = jnp.zeros((1, D), F32)
            dg_ref[...] = jnp.zeros((1, D), F32)
            _assemble_wpool(wp_ref, wps)

        gv = g_ref[...]
        xl = halo_ref[...]
        hbuf[0:HALO_P, :] = jnp.where(j == 0, 0.0, xl * _rsqrt_mean_sq(xl) * gv)
        xv = x_ref[...]
        r = _rsqrt_mean_sq(xv)
        xh = xv * r
        hbuf[HALO_P:HALO_P + ts, :] = xh * gv
        _pool_windows(hbuf, pbuf, j * ts, ts)

        dy = dxo_ref[...]
        t1 = (j * ts + lax.broadcasted_iota(jnp.int32, (ts, 1), 0) + 1).astype(F32)
        for gi, w in enumerate(POOL_WINDOWS):
            cols = slice(gi * PG, (gi + 1) * PG)
            p = pbuf[:, cols]
            y = jnp.dot(p, wps[gi], preferred_element_type=F32)
            dsc_ref[:, cols] += jnp.sum(dy[:, cols] * y, axis=0, keepdims=True)
            dq = (dy[:, cols] * sc_ref[:, cols]).astype(BF16)
            dwacc[gi] += lax.dot_general(p, dq, TN_DIMS, preferred_element_type=F32)
            dp = lax.dot_general(dq, wps[gi], NT_DIMS, preferred_element_type=F32)
            qbuf[0:ts, cols] = dp / jnp.minimum(t1, float(w))

        def chunk(ci, carry):
            r0 = pl.multiple_of(ci * R_CHUNK, R_CHUNK)
            tc = (j * ts + r0 + lax.broadcasted_iota(jnp.int32, (R_CHUNK, 1), 0) + 1).astype(F32)
            for gi, w in enumerate(POOL_WINDOWS):
                cnt = jnp.minimum(tc, float(w))
                offs = list(range(w))
                for c0 in range(gi * PG, (gi + 1) * PG, LANES):
                    sh = _shifted(qbuf, r0, R_CHUNK, c0, LANES, offs)
                    tot = sh[0]
                    for o in offs[1:]:
                        tot = tot + sh[o]
                    dhbuf[pl.ds(r0, R_CHUNK), c0:c0 + LANES] = tot - sh[0] * cnt
            return carry

        lax.fori_loop(0, ts // R_CHUNK, chunk, 0)
        qbuf[ts:ts + HALO_P, :] = qbuf[0:HALO_P, :]
        dh = dhbuf[...]
        dg_ref[...] += jnp.sum(dh * xh, axis=0, keepdims=True)
        dx_ref[...] = dy + _rms_bwd(dh, xh, r, gv)

        @pl.when(i == n - 1)
        def _():
            for gi in range(ng):
                for jj in range(NCHIP):
                    dwp_ref[jj, gi] = dwacc[gi, jj * rb:(jj + 1) * rb, :]

    rev = pl.BlockSpec((ts, D), lambda i: (n - 1 - i, 0))
    halo = pl.BlockSpec((HALO_P, D), lambda i: (jnp.maximum((n - 1 - i) * (ts // HALO_P) - 1, 0), 0))
    vec = pl.BlockSpec((1, D), lambda i: (0, 0))
    return _call(
        body, name="pool_bwd", grid=(n,), comm=comm, args=(dxo, x, x, g, w_pool, scale),
        in_specs=[rev, rev, halo, _const((1, D)), _const((NCHIP, ng, rb, PG)), _const((1, D))],
        out_specs=[rev, pl.BlockSpec((NCHIP, ng, rb, PG), lambda i: (0, 0, 0, 0)), vec, vec],
        out_shape=[jax.ShapeDtypeStruct((s, D), F32), jax.ShapeDtypeStruct((NCHIP, ng, rb, PG), F32),
                   jax.ShapeDtypeStruct((1, D), F32), jax.ShapeDtypeStruct((1, D), F32)],
        scratch=[pltpu.VMEM((HALO_P + ts, D), F32), pltpu.VMEM((ts, D), BF16), pltpu.VMEM((ts + HALO_P, D), F32),
                 pltpu.VMEM((ts, D), F32), pltpu.VMEM((ng, PG, PG), BF16), pltpu.VMEM((ng, PG, PG), F32)])


def _mix0_bwd_a(dx1, z, ac, bconv, w_out, conv_a, ln_g, ln_b, conv_b, comm=None):
    s = dx1.shape[0]
    ts = min(TS_MIXB, s)
    n = s // ts
    rows = 32
    boffs_a = [K_A - 1 - k for k in range(K_A)]
    boffs_s = [K_S - 1 - k for k in range(K_S)]

    def body(dx1_ref, z_ref, ac_ref, bconv_ref, wout_ref, ca_ref, lg_ref, lb_ref, cb_ref,
             dz_ref, dca_ref, dlg_ref, dlb_ref, dcb_ref, dac_buf, dbc_buf, dca_acc, dcb_acc):
        i = pl.program_id(0)

        @pl.when(i == 0)
        def _():
            dac_buf[ts:ts + HALO_A, :] = jnp.zeros((HALO_A, A), F32)
            dbc_buf[ts:ts + HALO_S, :] = jnp.zeros((HALO_S, A), F32)
            dca_acc[...] = jnp.zeros(dca_acc.shape, F32)
            dcb_acc[...] = jnp.zeros(dcb_acc.shape, F32)
            dlg_ref[...] = jnp.zeros((1, A), F32)
            dlb_ref[...] = jnp.zeros((1, A), F32)

        dcat = lax.dot_general(dx1_ref[...].astype(BF16), wout_ref[...], NT_DIMS, preferred_element_type=F32)
        db = dcat[:, A:2 * A]
        dz_ref[:, 2 * A:3 * A] = (db * bconv_ref[...]).astype(BF16)
        dbc_buf[0:ts, :] = db * z_ref[:, 2 * A:3 * A]
        ac = ac_ref[...]
        xc = ac - jnp.mean(ac, axis=-1, keepdims=True)
        rstd = lax.rsqrt(jnp.mean(xc * xc, axis=-1, keepdims=True) + LN_EPS)
        xn = xc * rstd
        lg = lg_ref[...]
        ln = xn * lg + lb_ref[...]
        sl = _sigmoid(ln)
        dln = dcat[:, 0:A] * sl * (1.0 + ln * (1.0 - sl))
        dlg_ref[...] += jnp.sum(dln * xn, axis=0, keepdims=True)
        dlb_ref[...] += jnp.sum(dln, axis=0, keepdims=True)
        dxn = dln * lg
        dac_buf[0:ts, :] = rstd * (dxn - jnp.mean(dxn, axis=-1, keepdims=True)
                                   - xn * jnp.mean(dxn * xn, axis=-1, keepdims=True))

        def chunk(ci, carry):
            r0 = pl.multiple_of(ci * rows, rows)
            rs = pl.ds(r0, rows)
            for c0 in range(0, A, LANES):
                col = lambda grp: slice(grp * A + c0, grp * A + c0 + LANES)
                a_val, sg = z_ref[rs, col(0)], _sigmoid(z_ref[rs, col(1)])
                dglu = _conv_corr(dac_buf, ca_ref, a_val * sg, dca_acc, r0, rows, c0, LANES, boffs_a)
                dz_ref[rs, col(0)] = (dglu * sg).astype(BF16)
                dz_ref[rs, col(1)] = (dglu * a_val * sg * (1.0 - sg)).astype(BF16)
                c_gate, bc_val = z_ref[rs, col(3)], z_ref[rs, col(4)]
                dcv = _conv_corr(dbc_buf, cb_ref, c_gate * bc_val, dcb_acc, r0, rows, c0, LANES, boffs_s)
                dz_ref[rs, col(3)] = (dcv * bc_val).astype(BF16)
                dz_ref[rs, col(4)] = (dcv * c_gate).astype(BF16)
            return carry

        lax.fori_loop(0, ts // rows, chunk, 0)
        dac_buf[ts:ts + HALO_A, :] = dac_buf[0:HALO_A, :]
        dbc_buf[ts:ts + HALO_S, :] = dbc_buf[0:HALO_S, :]

        @pl.when(i == n - 1)
        def _():
            _finish_tap_sums(dca_acc, dca_ref, K_A)
            _finish_tap_sums(dcb_acc, dcb_ref, K_S)

    rev = lambda w: pl.BlockSpec((ts, w), lambda i: (n - 1 - i, 0))
    full = lambda r, c: pl.BlockSpec((r, c), lambda i: (0, 0))
    return _call(
        body, name="mix0_bwd_a", grid=(n,), comm=comm, args=(dx1, z, ac, bconv, w_out, conv_a, ln_g, ln_b, conv_b),
        in_specs=[rev(D), rev(NZ), rev(A), rev(A), _const((2 * A, D)), _const((K_A, A)), _const((1, A)), _const((1, A)),
                  _const((K_S, A))],
        out_specs=[rev(NZ), full(K_A, A), full(1, A), full(1, A), full(K_S, A)],
        out_shape=[jax.ShapeDtypeStruct((s, NZ), BF16), jax.ShapeDtypeStruct((K_A, A), F32), jax.ShapeDtypeStruct((1, A), F32),
                   jax.ShapeDtypeStruct((1, A), F32), jax.ShapeDtypeStruct((K_S, A), F32)],
        scratch=[pltpu.VMEM((ts + HALO_A, A), F32), pltpu.VMEM((ts + HALO_S, A), F32), pltpu.VMEM((SUBLANES * K_A, A), F32),
                 pltpu.VMEM((SUBLANES * K_S, A), F32)])


def _cast_later_weights(w_up, w_down, w_pool, comm):
    nl = w_up.shape[0]

    def body(up_ref, dn_ref, pool_ref, *outs):
        up_o, dn_o, pool_o = outs[:nl], outs[nl:2 * nl], outs[2 * nl]
        for layer in range(nl):
            @pl.when(pl.program_id(0) == layer)
            def _(layer=layer):
                up_o[layer][...] = up_ref[...].astype(BF16)
                dn_o[layer][...] = dn_ref[...].astype(BF16)
                if layer == 0:
                    pool_o[...] = pool_ref[...].astype(BF16)

    per_layer = lambda a: pl.BlockSpec((None,) + a.shape[1:], lambda l: (l,) + (0,) * (a.ndim - 1))
    whole = lambda shape: pl.BlockSpec(shape, lambda l: (0,) * len(shape))
    out_shapes = [w_up.shape[1:]] * nl + [w_down.shape[1:]] * nl + [w_pool.shape]
    return _call(body, name="cast_later_weights", grid=(nl,), comm=comm, args=(w_up, w_down, w_pool),
                 in_specs=[per_layer(w_up), per_layer(w_down), whole(w_pool.shape)],
                 out_specs=[whole(shape) for shape in out_shapes],
                 out_shape=[jax.ShapeDtypeStruct(shape, BF16) for shape in out_shapes])


def _train_step(x, target, place, shard, later, rep, small_pack, small_shapes, adamw):
    bw_up, bw_in = FF2 // NCHIP, NZ // NCHIP
    five = lambda a, k: a.reshape(NCHIP, 2, *HALF[k])

    half = lambda a, name: a.reshape((2,) + HALF[name])
    (up0, up1, dn0, dn1, pool_bf), (g_in, g_out, small_g) = _cast_later_weights(
        *later, comm=_gather_comm([shard["w_in"], shard["w_out"]], small_pack))
    shard = dict(shard, w_pool=half(pool_bf, "w_pool"), w_up0=half(up0, "w_up"), w_up1=half(up1, "w_up"),
                 w_down0=half(dn0, "w_down"), w_down1=half(dn1, "w_down"))
    whole = {}
    for k, part in zip(SMALL_SHARDED, _unpack(small_g, small_shapes, lead=(NCHIP,))):
        whole[k] = jnp.moveaxis(part, 0, 1).reshape(part.shape[1], NCHIP * part.shape[2])
    w_in, w_out = g_in.reshape(NCHIP, D, bw_in), g_out.reshape(2 * A, D)
    conv_ffn = whole["conv_ffn_w"].reshape(2, K_S, FF2)
    nffn = [rep["norm_ffn"][0:1], rep["norm_ffn"][1:2]]

    (h0, z, ac, bconv, cat, x1), (g_up0, g_dn0) = _mix0_fwd(
        x, rep["norm_mix_even"], w_in, whole["conv_a"], rep["ln_a_g"], rep["ln_a_b"], whole["conv_b"], w_out,
        comm=_gather_comm([shard["w_up0"], shard["w_down0"]], mid_steps=GATHER_MID_STEPS["mix0_fwd"]))
    w_up0, w_dn0 = g_up0.reshape(NCHIP, D, bw_up), g_dn0.reshape(FF, D)
    (hf0, u00, u0, act0, x2), (g_pool, g_up1, g_dn1) = _ffn_fwd(
        x1, nffn[0], w_up0, conv_ffn[0], w_dn0, "ffn0_fwd",
        comm=_gather_comm([shard["w_pool"], shard["w_up1"], shard["w_down1"]], mid_steps=GATHER_MID_STEPS["ffn0_fwd"]))
    w_pool = g_pool.reshape(NCHIP, len(POOL_WINDOWS), PG // NCHIP, PG)
    w_up1, w_dn1 = g_up1.reshape(NCHIP, D, bw_up), g_dn1.reshape(FF, D)
    x3 = _pool_fwd(x2, whole["norm_mix_odd"], w_pool, whole["pool_scale"])
    (hf1, u01, u1, act1, dx4, g_nfin, loss_part), _ = _ffn_fwd(x3, nffn[1], w_up1, conv_ffn[1], w_dn1, "ffn1_fwd",
                                                                head=(rep["norm_final"], target))

    psum = lambda k, g, ld, tag="": _pair_sum(place, g, ld, "pair_sum_" + k + tag)
    tot = lambda k, g, ld, p, layer=0, nl=1, prev=None: _chip_sum(place, g, ld, p, layer, nl, "chip_sum_%s%d" % (k, layer), prev)
    pair, chips, join = _pair_comm, _chips_comm, _join_comm

    gr_dn1 = five(_wgrad(act1, dx4, 1, "wgrad_down1")[0], "w_down")
    (du01, g_wc1), _ = _ffn_bwd_a(dx4, u1, u01, conv_ffn[1], w_dn1, "ffn1_bwd_a")
    g_up, (ld_dn1,) = _wgrad(hf1, du01, NCHIP, "wgrad_up1", comm=pair([gr_dn1]))
    gr_up1 = five(g_up, "w_up")
    s_dn1 = psum("w_down", gr_dn1, ld_dn1, "1")
    (dx3, g_nf1), (p_dn1, ld_up1) = _nt_rms_bwd(du01, w_up1, x3, nffn[1], dx4, "ffn1_bwd_b",
                                                comm=join(chips([s_dn1]), pair([gr_up1])))
    s_up1 = psum("w_up", gr_up1, ld_up1, "1")
    (dx2, g_wpool, g_scale, g_nmo), _ = _pool_bwd(dx3, x2, whole["norm_mix_odd"], w_pool, whole["pool_scale"])
    gr_pool = five(g_wpool, "w_pool")
    (du00, g_wc0), (p_up1,) = _ffn_bwd_a(dx2, u0, u00, conv_ffn[0], w_dn0, "ffn0_bwd_a", comm=chips([s_up1]))
    gr_dn0 = five(_wgrad(act0, dx2, 1, "wgrad_down0")[0], "w_down")
    g_up, (ld_dn0, ld_pool) = _wgrad(hf0, du00, NCHIP, "wgrad_up0", comm=pair([gr_dn0, gr_pool]))
    gr_up0 = five(g_up, "w_up")
    s_dn0, s_pool = psum("w_down", gr_dn0, ld_dn0, "0"), psum("w_pool", gr_pool, ld_pool)
    (dx1, g_nf0), (p_dn0, p_pool, ld_up0) = _nt_rms_bwd(du00, w_up0, x1, nffn[0], dx2, "ffn0_bwd_b",
                                                        comm=join(chips([s_dn0, s_pool]), pair([gr_up0])))
    s_up0 = psum("w_up", gr_up0, ld_up0, "0")
    gr_out = five(_wgrad(cat, dx1, 1, "wgrad_out")[0], "w_out")
    (dz, g_ca, g_lg, g_lb, g_cb), (p_up0, ld_out) = _mix0_bwd_a(
        dx1, z, ac, bconv, w_out, whole["conv_a"], rep["ln_a_g"], rep["ln_a_b"], whole["conv_b"],
        comm=join(chips([s_up0]), pair([gr_out])))
    s_out = psum("w_out", gr_out, ld_out)
    t_pool = tot("w_pool", gr_pool, ld_pool, p_pool)
    t_up = tot("w_up", gr_up0, ld_up0, p_up0, 0, 2, tot("w_up", gr_up1, ld_up1, p_up1, 1, 2))
    t_dn = tot("w_down", gr_dn0, ld_dn0, p_dn0, 0, 2, tot("w_down", gr_dn1, ld_dn1, p_dn1, 1, 2))
    small = {"norm_mix_odd": g_nmo, "pool_scale": g_scale, "norm_ffn": jnp.concatenate([g_nf0, g_nf1], axis=0),
             "conv_ffn_w": jnp.stack([g_wc0, g_wc1]), "norm_final": g_nfin, "loss": loss_part,
             "conv_a": g_ca, "ln_a_g": g_lg, "ln_a_b": g_lb, "conv_b": g_cb}
    g_in, (p_out, early_all, t_pool, t_up, t_dn) = _wgrad(
        h0, dz, NCHIP, "wgrad_in",
        comm=join(chips([s_out], _pack([small[k] for k in SMALL_EARLY])), _swap_comm([t_pool, t_up, t_dn])))
    gr_in = five(g_in, "w_in")
    n_mm = x.shape[0] // min(TS_MM, x.shape[0])
    n_a = max(1, n_mm // 4)
    last_bwd = functools.partial(_nt_rms_bwd, dz, w_in, x, rep["norm_mix_even"], dx1)
    (dx_a, g_nme_a), (ld_in,) = last_bwd("mix0_bwd_b0", comm=pair([gr_in]), tiles=(0, n_a))
    s_in = psum("w_in", gr_in, ld_in)
    (grad_x, g_nme_b), (p_in,) = last_bwd("mix0_bwd_b1", comm=chips([s_in]), tiles=(n_a, n_mm - n_a), dx_so_far=dx_a)
    small["norm_mix_even"] = g_nme_a + g_nme_b
    late_all, t_in, t_out = _run_comm(
        join(chips([], _pack([small[k] for k in SMALL_LATE])),
             _swap_comm([tot("w_in", gr_in, ld_in, p_in), tot("w_out", gr_out, ld_out, p_out)])), "rs_last")
    done = {k: adamw(k, t) for k, t in (("w_up", t_up), ("w_down", t_dn), ("w_pool", t_pool), ("w_in", t_in), ("w_out", t_out))}
    summed = dict(zip(SMALL_EARLY, _unpack(_sum_devices(early_all, "sum_small_early"), [small[k].shape for k in SMALL_EARLY])))
    summed.update(zip(SMALL_LATE, _unpack(_sum_devices(late_all, "sum_small_late"), [small[k].shape for k in SMALL_LATE])))
    return grad_x, done, summed


def _adamw_math(w, g, m, v):
    m = ADAM_B1 * m + (1.0 - ADAM_B1) * g
    v = ADAM_B2 * v + (1.0 - ADAM_B2) * (g * g)
    m_hat = m / (1.0 - ADAM_B1 ** ADAM_STEP)
    v_hat = v / (1.0 - ADAM_B2 ** ADAM_STEP)
    return -ADAM_LR * (m_hat / (jnp.sqrt(v_hat) + ADAM_EPS) + ADAM_WD * w), m, v


def _adamw_big(w, g, m, v, tr, name, comm=None):
    nl, rows, cols = w.shape

    def body(w_ref, g_ref, m_ref, v_ref, g2_ref, d_ref, m2_ref, v2_ref):
        gv = g_ref[...]
        g2_ref[...] = gv
        d_ref[...], m2_ref[...], v2_ref[...] = _adamw_math(w_ref[...], gv, m_ref[...], v_ref[...])

    spec = pl.BlockSpec((None, tr, cols), lambda l, r: (l, r, 0))
    return _call(body, name=name, grid=(nl, rows // tr), comm=comm, args=(w, g, m, v), in_specs=[spec] * 4,
                 out_specs=[spec] * 4, out_shape=[jax.ShapeDtypeStruct(w.shape, F32)] * 4)


def _adamw_small(ws, gs, ms, vs):
    n = len(ws)

    def body(*refs):
        for p in range(n):
            w_ref, g_ref, m_ref, v_ref = (refs[q * n + p] for q in range(4))
            d_ref, m2_ref, v2_ref = (refs[(4 + q) * n + p] for q in range(3))
            d_ref[...], m2_ref[...], v2_ref[...] = _adamw_math(w_ref[...], g_ref[...], m_ref[...], v_ref[...])

    whole = lambda a: pl.BlockSpec(a.shape, lambda: (0,) * a.ndim)
    outs = pl.pallas_call(
        body, name="adamw_small", in_specs=[whole(a) for a in ws] * 4, out_specs=[whole(a) for a in ws] * 3,
        out_shape=[jax.ShapeDtypeStruct(a.shape, F32) for a in ws] * 3,
        compiler_params=pltpu.CompilerParams(vmem_limit_bytes=VMEM_LIMIT_BYTES),
    )(*ws, *gs, *ms, *vs)
    return outs[0:n], outs[n:2 * n], outs[2 * n:3 * n]


def _place():
    x, y, c = lax.axis_index("x"), lax.axis_index("y"), lax.axis_index("c")
    chips = [(x, 1 - y), (1 - x, y), (1 - x, 1 - y)]
    blocks = [2 * cx + cy for cx, cy in chips]
    return x, y, c, 2 * x + y, chips, blocks


def _gather_comm(shards, small=None, mid_steps=None):
    na = len(shards)
    ns = 0 if small is None else 1

    def copies(ins, outs, sems):
        h1_send, h1_recv, h2_send, h2_recv, f1_send, f1_recv, f2_send, f2_recv, own_send, own_recv = sems[:10]
        x, y, c, j, chips, blocks = _place()
        sib = (x, y, 1 - c)
        piece = lambda a, p: pl.ds(p * (shards[a].shape[1] // 2), shards[a].shape[1] // 2)

        def remote(src, dst, send, recv, q, to):
            return pltpu.make_async_remote_copy(src_ref=src, dst_ref=dst, send_sem=send.at[q], recv_sem=recv.at[q],
                                                device_id=to, device_id_type=MESH)

        def hop1(a, k, arrival):
            dst = outs[a].at[blocks[k], c] if arrival else outs[a].at[j, c]
            return remote(dst if arrival else ins[a].at[c], dst, h1_send, h1_recv, 2 * a + k, (*chips[k], c))

        def hop2(a, p, arrival):
            ref = outs[a].at[blocks[2] if arrival else blocks[p], c, piece(a, p)]
            return remote(ref, ref, h2_send, h2_recv, 2 * a + p, (*chips[1 - p], c))

        def fwd1(a, k, half):
            ref = outs[a].at[blocks[k], half]
            return remote(ref, ref, f1_send, f1_recv, 2 * a + k, sib)

        def fwd2(a, p, half):
            ref = outs[a].at[blocks[2], half, piece(a, p)]
            return remote(ref, ref, f2_send, f2_recv, 2 * a + p, sib)

        own = [remote(ins[a], outs[a].at[j], own_send, own_recv, a, sib) for a in range(na)]
        small_copies = [remote(ins[na], outs[na].at[j], sems[10], sems[11], k, (*chips[k], c)) for k in range(3 * ns)]
        local = [pltpu.make_async_copy(ins[na], outs[na].at[j], sems[12])] if ns else []
        return hop1, hop2, fwd1, fwd2, own, small_copies, local, c

    pairs = [(a, k) for a in range(na) for k in range(2)]

    def start(ins, outs, sems):
        hop1, _, _, _, own, small_copies, local, _ = copies(ins, outs, sems)
        for cp in local + own + [hop1(a, k, False) for a, k in pairs] + small_copies:
            cp.start()

    def mid(ins, outs, sems):
        hop1, hop2, fwd1, _, _, _, _, c = copies(ins, outs, sems)
        for a, k in pairs:
            hop1(a, k, True).wait_recv()
            hop2(a, k, False).start()
            fwd1(a, k, c).start()

    def finish(ins, outs, sems):
        hop1, hop2, fwd1, fwd2, own, small_copies, local, c = copies(ins, outs, sems)
        for a, p in pairs:
            hop2(a, p, True).wait_recv()
            fwd2(a, p, c).start()
        for cp in small_copies:
            cp.wait()
        for a, k in pairs:
            hop1(a, k, False).wait_send()
            hop2(a, k, False).wait_send()
            fwd1(a, k, c).wait_send()
            fwd1(a, k, 1 - c).wait_recv()
            fwd2(a, k, c).wait_send()
            fwd2(a, k, 1 - c).wait_recv()
        for cp in own + local:
            cp.wait()

    out_shapes = [jax.ShapeDtypeStruct((NCHIP,) + s.shape, s.dtype) for s in shards]
    sems = [pltpu.SemaphoreType.DMA((2 * na,))] * 8 + [pltpu.SemaphoreType.DMA((na,))] * 2
    if ns:
        out_shapes.append(jax.ShapeDtypeStruct((NCHIP,) + small.shape, small.dtype))
        sems += [pltpu.SemaphoreType.DMA((3,)), pltpu.SemaphoreType.DMA((3,)), pltpu.SemaphoreType.DMA]
    if mid_steps is not None:
        return _Comm(list(shards) + [small] * ns, out_shapes, sems, start, finish, mid=mid, mid_steps=mid_steps)

    def forward_and_finish(ins, outs, sems):
        mid(ins, outs, sems)
        finish(ins, outs, sems)

    return _Comm(list(shards) + [small] * ns, out_shapes, sems, start, forward_and_finish)


def _simple_comm(inputs, out_shapes, make_copies, n_sems, aliases=None):
    def start(ins, outs, sems):
        for cp in make_copies(ins, outs, sems):
            cp.start()

    def finish(ins, outs, sems):
        for cp in make_copies(ins, outs, sems):
            cp.wait()

    return _Comm(inputs, out_shapes, [pltpu.SemaphoreType.DMA((n,)) for n in n_sems], start, finish, aliases)


def _pair_comm(grads):
    def make_copies(ins, outs, sems):
        x, y, c, _, _, _ = _place()
        return [pltpu.make_async_remote_copy(
            src_ref=ins[a].at[:, 1 - c], dst_ref=outs[a], send_sem=sems[0].at[a], recv_sem=sems[1].at[a],
            device_id=(x, y, 1 - c), device_id_type=MESH) for a in range(len(grads))]

    out_shapes = [jax.ShapeDtypeStruct(g.shape[:1] + g.shape[2:], F32) for g in grads]
    return _simple_comm(grads, out_shapes, make_copies, [len(grads)] * 2)


def _chips_comm(sums, small=None):
    na = len(sums)
    nk = NCHIP - 1

    def make_copies(ins, outs, sems):
        x, y, c, _, chips, _ = _place()
        copies = [pltpu.make_async_remote_copy(
            src_ref=ins[a].at[k], dst_ref=outs[a].at[k], send_sem=sems[0].at[a * nk + k],
            recv_sem=sems[1].at[a * nk + k], device_id=(*chips[k], c), device_id_type=MESH)
            for a in range(na) for k in range(nk)]
        if small is not None:
            me = 4 * x + 2 * y + c
            for r in range(1, NDEV):
                peer = (1 - x if r & 4 else x, 1 - y if r & 2 else y, 1 - c if r & 1 else c)
                copies.append(pltpu.make_async_remote_copy(
                    src_ref=ins[na], dst_ref=outs[na].at[me], send_sem=sems[2].at[r - 1], recv_sem=sems[3].at[r - 1],
                    device_id=peer, device_id_type=MESH))
            copies.append(pltpu.make_async_copy(ins[na], outs[na].at[me], sems[4].at[0]))
        return copies

    out_shapes = [jax.ShapeDtypeStruct(g.shape, BF16) for g in sums]
    chip_sems = [max(na * nk, 1)] * 2
    if small is None:
        return _simple_comm(sums, out_shapes, make_copies, chip_sems)
    out_shapes.append(jax.ShapeDtypeStruct((NDEV,) + small.shape, F32))
    return _simple_comm(list(sums) + [small], out_shapes, make_copies, chip_sems + [NDEV - 1] * 2 + [1])


def _swap_comm(totals):
    def make_copies(ins, outs, sems):
        x, y, c, _, _, _ = _place()
        return [pltpu.make_async_remote_copy(
            src_ref=outs[a].at[:, c], dst_ref=outs[a].at[:, c], send_sem=sems[0].at[a], recv_sem=sems[1].at[a],
            device_id=(x, y, 1 - c), device_id_type=MESH) for a in range(len(totals))]

    out_shapes = [jax.ShapeDtypeStruct(t.shape, F32) for t in totals]
    return _simple_comm(totals, out_shapes, make_copies, [len(totals)] * 2, aliases={a: a for a in range(len(totals))})


def _pair_sum(place, grad, landed, name):
    _, _, rows, cols = grad.shape

    def body(place_ref, g_ref, l_ref, o_ref):
        o_ref[...] = (g_ref[...] + l_ref[...]).astype(BF16)

    other = lambda k, p: jnp.bitwise_xor(p[0], k + 1)
    return pl.pallas_call(
        body, name=name,
        grid_spec=pltpu.PrefetchScalarGridSpec(
            num_scalar_prefetch=1, grid=(NCHIP - 1,),
            in_specs=[pl.BlockSpec((None, None, rows, cols), lambda k, p: (other(k, p), p[1], 0, 0)),
                      pl.BlockSpec((None, rows, cols), lambda k, p: (other(k, p), 0, 0))],
            out_specs=pl.BlockSpec((None, rows, cols), lambda k, p: (k, 0, 0))),
        out_shape=jax.ShapeDtypeStruct((NCHIP - 1, rows, cols), BF16), compiler_params=_cparams(1),
    )(place, grad, landed)


def _chip_sum(place, grad, landed, parts, layer, n_layers, name, prev=None):
    _, _, rows, cols = grad.shape

    def body(*refs):
        g_ref, l_ref, p_ref, o_ref = refs[1], refs[2], refs[3], refs[-1]
        tot = g_ref[...] + l_ref[...]
        for k in range(NCHIP - 1):
            tot = tot + p_ref[k].astype(F32)
        o_ref[...] = tot

    in_specs = [pl.BlockSpec((None, None, rows, cols), lambda i, p: (p[0], p[1], 0, 0)),
                pl.BlockSpec((None, rows, cols), lambda i, p: (p[0], 0, 0)),
                pl.BlockSpec((NCHIP - 1, rows, cols), lambda i, p: (0, 0, 0))]
    args = [place, grad, landed, parts]
    if prev is not None:
        in_specs.append(ANY)
        args.append(prev)
    return pl.pallas_call(
        body, name=name,
        grid_spec=pltpu.PrefetchScalarGridSpec(
            num_scalar_prefetch=1, grid=(1,), in_specs=in_specs,
            out_specs=pl.BlockSpec((None, None, rows, cols), lambda i, p: (layer, p[1], 0, 0))),
        out_shape=jax.ShapeDtypeStruct((n_layers, 2, rows, cols), F32),
        input_output_aliases={} if prev is None else {4: 0}, compiler_params=_cparams(1),
    )(*args)


def _sum_devices(parts, name):
    def body(p_ref, o_ref):
        tot = p_ref[0]
        for d in range(1, NDEV):
            tot = tot + p_ref[d]
        o_ref[...] = tot

    return pl.pallas_call(
        body, name=name, in_specs=[pl.BlockSpec(parts.shape, lambda: (0, 0, 0))],
        out_specs=pl.BlockSpec(parts.shape[1:], lambda: (0, 0)), out_shape=jax.ShapeDtypeStruct(parts.shape[1:], F32),
    )(parts)


def _pack(parts):
    rows = []
    for p in parts:
        p = p.reshape(-1, LANES)
        rows.append(jnp.pad(p, ((0, -p.shape[0] % SUBLANES), (0, 0))))
    return jnp.concatenate(rows, axis=0)


def _unpack(buf, shapes, lead=()):
    out, r0 = [], 0
    nl = len(lead)
    for shp in shapes:
        nrow = 1
        for d in shp:
            nrow *= d
        nrow //= LANES
        out.append(buf[(slice(None),) * nl + (slice(r0, r0 + nrow),)].reshape(lead + tuple(shp)))
        r0 += nrow + (-nrow % SUBLANES)
    return out


WEIGHT_ORDER = ("norm_mix_even", "w_in", "conv_a", "ln_a_g", "ln_a_b", "conv_b", "w_out", "norm_mix_odd", "w_pool",
                "pool_scale", "norm_ffn", "w_up", "conv_ffn_w", "w_down", "norm_final")
BIG = ("w_in", "w_out", "w_pool", "w_up", "w_down")
HALF = {"w_in": (D // 2, NZ // NCHIP), "w_out": (2 * A // NCHIP // 2, D), "w_pool": (PG // 2, PG),
        "w_up": (D // 2, FF2 // NCHIP), "w_down": (FF // NCHIP // 2, D)}
SMALL_SHARDED = ("conv_a", "conv_b", "conv_ffn_w", "norm_mix_odd", "pool_scale")
SMALL_ALL = ("norm_mix_even", "conv_a", "ln_a_g", "ln_a_b", "conv_b", "norm_mix_odd", "pool_scale", "norm_ffn", "conv_ffn_w",
             "norm_final")
SMALL_EARLY = ("norm_mix_odd", "pool_scale", "norm_ffn", "conv_ffn_w", "norm_final", "loss", "conv_a", "ln_a_g", "ln_a_b", "conv_b")
SMALL_LATE = ("norm_mix_even",)


def kernel(x, norm_mix_even, w_in, conv_a, ln_a_g, ln_a_b, conv_b, w_out, norm_mix_odd, w_pool, pool_scale, norm_ffn, w_up, conv_ffn_w, w_down, norm_final, loss_target, m_norm_mix_even, m_w_in, m_conv_a, m_ln_a_g, m_ln_a_b, m_conv_b, m_w_out, m_norm_mix_odd, m_w_pool, m_pool_scale, m_norm_ffn, m_w_up, m_conv_ffn_w, m_w_down, m_norm_final, v_norm_mix_even, v_w_in, v_conv_a, v_ln_a_g, v_ln_a_b, v_conv_b, v_w_out, v_norm_mix_odd, v_w_pool, v_pool_scale, v_norm_ffn, v_w_up, v_conv_ffn_w, v_w_down, v_norm_final):
    w = dict(norm_mix_even=norm_mix_even, w_in=w_in, conv_a=conv_a, ln_a_g=ln_a_g, ln_a_b=ln_a_b, conv_b=conv_b, w_out=w_out,
             norm_mix_odd=norm_mix_odd, w_pool=w_pool, pool_scale=pool_scale, norm_ffn=norm_ffn, w_up=w_up,
             conv_ffn_w=conv_ffn_w, w_down=w_down, norm_final=norm_final)
    m = dict(norm_mix_even=m_norm_mix_even, w_in=m_w_in, conv_a=m_conv_a, ln_a_g=m_ln_a_g, ln_a_b=m_ln_a_b, conv_b=m_conv_b,
             w_out=m_w_out, norm_mix_odd=m_norm_mix_odd, w_pool=m_w_pool, pool_scale=m_pool_scale, norm_ffn=m_norm_ffn,
             w_up=m_w_up, conv_ffn_w=m_conv_ffn_w, w_down=m_w_down, norm_final=m_norm_final)
    v = dict(norm_mix_even=v_norm_mix_even, w_in=v_w_in, conv_a=v_conv_a, ln_a_g=v_ln_a_g, ln_a_b=v_ln_a_b, conv_b=v_conv_b,
             w_out=v_w_out, norm_mix_odd=v_norm_mix_odd, w_pool=v_w_pool, pool_scale=v_pool_scale, norm_ffn=v_norm_ffn,
             w_up=v_w_up, conv_ffn_w=v_conv_ffn_w, w_down=v_w_down, norm_final=v_norm_final)
    chip = 2 * lax.axis_index("x") + lax.axis_index("y")
    place = jnp.stack([chip, lax.axis_index("c")]).astype(jnp.int32)

    half = lambda a, name: a.astype(BF16).reshape((2,) + HALF[name])
    shard = {"w_in": half(w_in[0], "w_in"), "w_out": half(w_out[0], "w_out")}
    small_shapes = [w[k].shape[-2:] if w[k].ndim == 3 and k != "conv_ffn_w" else (w[k].size // w[k].shape[-1], w[k].shape[-1])
                    for k in SMALL_SHARDED]
    rep = dict(norm_mix_even=norm_mix_even, ln_a_g=ln_a_g, ln_a_b=ln_a_b, norm_ffn=norm_ffn, norm_final=norm_final.reshape(1, D))
    as3 = lambda a: a.reshape(a.shape[0], -1, a.shape[-1])

    def adamw(k, total):
        outs, _ = _adamw_big(as3(w[k]), as3(total), as3(m[k]), as3(v[k]), ADAMW_ROWS[k], "adamw_" + k)
        return [a.reshape(w[k].shape) for a in outs]

    grad_x, done, summed = _train_step(
        x[0], loss_target[0], place, shard, (w_up, w_down, w_pool[0]), rep, _pack([w[k] for k in SMALL_SHARDED]), small_shapes,
        adamw)
    loss = summed["loss"][0, 0]

    grad, delta, new_m, new_v = ({k: done[k][q] for k in BIG} for q in range(4))
    for k in SMALL_ALL:
        gsum = summed[k]
        if k in SMALL_SHARDED:
            cols = w[k].shape[-1]
            gsum = lax.dynamic_slice_in_dim(gsum, chip * cols, cols, axis=gsum.ndim - 1)
        grad[k] = gsum.reshape(w[k].shape)

    as2 = lambda a: a.reshape(1, -1) if a.ndim == 1 else a[0] if a.ndim == 3 and a.shape[0] == 1 else a
    ds, ms, vs = _adamw_small(*[[as2(t[k]) for k in SMALL_ALL] for t in (w, grad, m, v)])
    for k, d2, m2, v2 in zip(SMALL_ALL, ds, ms, vs):
        delta[k], new_m[k], new_v[k] = (a.reshape(w[k].shape) for a in (d2, m2, v2))

    return (loss, grad_x[None], *[grad[k] for k in WEIGHT_ORDER], *[delta[k] for k in WEIGHT_ORDER],
            *[new_m[k] for k in WEIGHT_ORDER], *[new_v[k] for k in WEIGHT_ORDER])
```

```python
import functools

import jax
import jax.numpy as jnp
from jax import lax
from jax.experimental import pallas as pl
from jax.experimental.pallas import tpu as pltpu

F32, BF16 = jnp.float32, jnp.bfloat16

D = 1024
A = 512
NZ = 5 * A
FF = 2816
FF2 = 2 * FF
NCHIP = 4
NDEV = 8
K_A, K_S = 31, 3
POOL_WINDOWS = (2, 4, 8, 16)
PG = D // len(POOL_WINDOWS)
RMS_EPS, LN_EPS = 1e-6, 1e-5
ADAM_LR, ADAM_B1, ADAM_B2, ADAM_EPS, ADAM_WD, ADAM_STEP = 0.001, 0.9, 0.999, 1e-08, 0.01, 10

HALO_A, HALO_S, HALO_P = 32, 8, 16
SUBLANES = 8
LANES = 128
VMEM_LIMIT_BYTES = 56 * 1024 * 1024

TS_MIX = 512
TS_MIXB = 512
TS_FFN = 256
TS_POOL = 512
TS_MM = 512
TS_WGRAD = 1024
ADAMW_ROWS = {"w_in": 512, "w_out": 256, "w_pool": 256, "w_up": 256, "w_down": 352}
R_CHUNK = 64
GATHER_MID_STEPS = {"mix0_fwd": 2, "ffn0_fwd": 5}

MESH = pl.DeviceIdType.MESH
ANY = pl.BlockSpec(memory_space=pl.ANY)
NT_DIMS = (((1,), (1,)), ((), ()))
TN_DIMS = (((0,), (0,)), ((), ()))


def _cparams(n_axes):
    return pltpu.CompilerParams(dimension_semantics=("arbitrary",) * n_axes, vmem_limit_bytes=VMEM_LIMIT_BYTES)


def _const(shape):
    nd = len(shape)
    return pl.BlockSpec(shape, lambda *_: (0,) * nd, pipeline_mode=pl.Buffered(1))


def _sigmoid(v):
    return 1.0 / (1.0 + jnp.exp(-v))


def _rsqrt_mean_sq(x):
    return lax.rsqrt(jnp.mean(x * x, axis=-1, keepdims=True) + RMS_EPS)


def _rms_bwd(dh, xh, r, g):
    dxh = dh * g
    return r * (dxh - xh * jnp.mean(dxh * xh, axis=-1, keepdims=True))


def _shifted(buf_ref, row0, rows, col0, width, offsets):
    lo = (min(offsets) // SUBLANES) * SUBLANES
    hi = -(-(max(offsets) + rows) // SUBLANES) * SUBLANES
    start = row0 + lo if isinstance(row0, int) else pl.multiple_of(row0 + lo, SUBLANES)
    win = buf_ref[pl.ds(start, hi - lo), col0:col0 + width]
    out = {}
    for res in sorted({(o - lo) % SUBLANES for o in offsets}):
        qs = {o: (o - lo) // SUBLANES for o in offsets if (o - lo) % SUBLANES == res}
        base = pltpu.roll(win, hi - lo - res, 0) if res else win
        for o, q in qs.items():
            out[o] = base[SUBLANES * q:SUBLANES * q + rows, :]
    return out


def _rowsum8(v):
    acc = v[0:SUBLANES, :]
    for r in range(SUBLANES, v.shape[0], SUBLANES):
        acc = acc + v[r:r + SUBLANES, :]
    return acc


def _taps(w_ref, sh, offsets, col0, width):
    acc = None
    for k, o in enumerate(offsets):
        term = w_ref[k:k + 1, col0:col0 + width] * sh[o]
        acc = term if acc is None else acc + term
    return acc


def _window_bases(buf_ref, row0, rows, col0, width, offsets):
    lo = (min(offsets) // SUBLANES) * SUBLANES
    hi = -(-(max(offsets) + rows) // SUBLANES) * SUBLANES
    start = row0 + lo if isinstance(row0, int) else pl.multiple_of(row0 + lo, SUBLANES)
    win = buf_ref[pl.ds(start, hi - lo), col0:col0 + width]
    for res in sorted({(o - lo) % SUBLANES for o in offsets}):
        taps = [(k, (o - lo) // SUBLANES * SUBLANES) for k, o in enumerate(offsets) if (o - lo) % SUBLANES == res]
        yield (pltpu.roll(win, hi - lo - res, 0) if res else win), taps


def _conv_acc(buf_ref, w_ref, row0, rows, col0, width, offsets):
    acc = None
    for base, taps in _window_bases(buf_ref, row0, rows, col0, width, offsets):
        for k, q in taps:
            term = w_ref[k:k + 1, col0:col0 + width] * base[q:q + rows, :]
            acc = term if acc is None else acc + term
    return acc


def _conv_corr(buf_ref, w_ref, other, acc_ref, row0, rows, col0, width, offsets):
    acc = None
    for base, taps in _window_bases(buf_ref, row0, rows, col0, width, offsets):
        for k, q in taps:
            sl = base[q:q + rows, :]
            term = w_ref[k:k + 1, col0:col0 + width] * sl
            acc = term if acc is None else acc + term
            acc_ref[SUBLANES * k:SUBLANES * (k + 1), col0:col0 + width] += _rowsum8(sl * other)
    return acc


def _finish_tap_sums(acc_ref, out_ref, n_taps):
    for k in range(n_taps):
        out_ref[k:k + 1, :] = jnp.sum(acc_ref[SUBLANES * k:SUBLANES * (k + 1), :], axis=0, keepdims=True)


class _Comm:
    def __init__(self, inputs, out_shapes, sems, start, finish, aliases=None, mid=None, mid_steps=1):
        self.inputs, self.out_shapes, self.sems = list(inputs), list(out_shapes), list(sems)
        self.start, self.finish, self.mid, self.aliases = start, finish, mid, dict(aliases or {})
        self.mid_steps = mid_steps


def _join_comm(a, b):
    ni, no, ns = len(a.inputs), len(a.out_shapes), len(a.sems)

    def both(name):
        def run(ins, outs, sems):
            if getattr(a, name) is not None:
                getattr(a, name)(ins[:ni], outs[:no], sems[:ns])
            if getattr(b, name) is not None:
                getattr(b, name)(ins[ni:], outs[no:], sems[ns:])
        return run

    aliases = {**a.aliases, **{ni + i: no + o for i, o in b.aliases.items()}}
    mid = both("mid") if (a.mid is not None or b.mid is not None) else None
    return _Comm(a.inputs + b.inputs, a.out_shapes + b.out_shapes, a.sems + b.sems, both("start"), both("finish"), aliases, mid)


def _call(body, *, name, grid, in_specs, out_specs, out_shape, args, scratch=(), comm=None, aliases=None):
    n_in, n_out, n_scr, n_axes = len(in_specs), len(out_specs), len(scratch), len(grid)
    params = pltpu.CompilerParams(dimension_semantics=("arbitrary",) * n_axes, vmem_limit_bytes=VMEM_LIMIT_BYTES)
    aliases = dict(aliases or {})
    if comm is None:
        outs = pl.pallas_call(body, name=name, grid=grid, in_specs=list(in_specs), out_specs=list(out_specs),
                              out_shape=list(out_shape), scratch_shapes=list(scratch), input_output_aliases=aliases,
                              compiler_params=params)(*args)
        return list(outs), []
    ci, co = len(comm.inputs), len(comm.out_shapes)

    def wrapped(*refs):
        k_in, c_in = refs[:n_in], refs[n_in:n_in + ci]
        o0 = n_in + ci
        k_out, c_out = refs[o0:o0 + n_out], refs[o0 + n_out:o0 + n_out + co]
        s0 = o0 + n_out + co
        k_scr, c_sem = refs[s0:s0 + n_scr], refs[s0 + n_scr:]
        first = pl.program_id(0) == 0
        last = pl.program_id(0) == grid[0] - 1
        for ax in range(1, n_axes):
            first = jnp.logical_and(first, pl.program_id(ax) == 0)
            last = jnp.logical_and(last, pl.program_id(ax) == grid[ax] - 1)

        @pl.when(first)
        def _():
            comm.start(c_in, c_out, c_sem)

        mid_early = comm.mid is not None and n_axes == 1 and grid[0] > comm.mid_steps
        if mid_early:
            @pl.when(pl.program_id(0) == grid[0] - 1 - comm.mid_steps)
            def _():
                comm.mid(c_in, c_out, c_sem)

        body(*k_in, *k_out, *k_scr)

        @pl.when(last)
        def _():
            if comm.mid is not None and not mid_early:
                comm.mid(c_in, c_out, c_sem)
            comm.finish(c_in, c_out, c_sem)

    outs = pl.pallas_call(
        wrapped, name=name, grid=grid, in_specs=list(in_specs) + [ANY] * ci, out_specs=list(out_specs) + [ANY] * co,
        out_shape=list(out_shape) + comm.out_shapes, scratch_shapes=list(scratch) + comm.sems,
        input_output_aliases={**aliases, **{n_in + i: n_out + o for i, o in comm.aliases.items()}}, compiler_params=params,
    )(*args, *comm.inputs)
    return list(outs[:n_out]), list(outs[n_out:])


def _run_comm(comm, name):
    ci, co = len(comm.inputs), len(comm.out_shapes)

    def body(*refs):
        c_in, c_out, c_sem = refs[:ci], refs[ci:ci + co], refs[ci + co:]
        comm.start(c_in, c_out, c_sem)
        if comm.mid is not None:
            comm.mid(c_in, c_out, c_sem)
        comm.finish(c_in, c_out, c_sem)

    return list(pl.pallas_call(body, name=name, in_specs=[ANY] * ci, out_specs=[ANY] * co, out_shape=comm.out_shapes,
                               scratch_shapes=comm.sems, input_output_aliases=comm.aliases)(*comm.inputs))


def _mix0_fwd(x, g, w_in, conv_a, ln_g, ln_b, conv_b, w_out, comm=None):
    s = x.shape[0]
    ts = min(TS_MIX, s)
    n = s // ts
    bw = NZ // NCHIP
    offs_a = [HALO_A - (K_A - 1) + k for k in range(K_A)]
    offs_s = [HALO_S - (K_S - 1) + k for k in range(K_S)]

    def body(x_ref, g_ref, win_ref, ca_ref, lg_ref, lb_ref, cb_ref, wout_ref,
             h_ref, z_ref, ac_ref, bconv_buf, cat_ref, x1_ref, glu_buf, cv_buf):
        i = pl.program_id(0)

        @pl.when(i == 0)
        def _():
            glu_buf[0:HALO_A, :] = jnp.zeros((HALO_A, A), F32)
            cv_buf[0:HALO_S, :] = jnp.zeros((HALO_S, A), F32)

        xv = x_ref[...]
        h = (xv * _rsqrt_mean_sq(xv) * g_ref[...]).astype(BF16)
        h_ref[...] = h
        for j in range(NCHIP):
            z_ref[:, j * bw:(j + 1) * bw] = jnp.dot(h, win_ref[j], preferred_element_type=F32)
        glu_buf[HALO_A:HALO_A + ts, :] = z_ref[:, 0:A] * _sigmoid(z_ref[:, A:2 * A])
        cv_buf[HALO_S:HALO_S + ts, :] = z_ref[:, 3 * A:4 * A] * z_ref[:, 4 * A:5 * A]

        def chunk(ci, carry):
            r0 = pl.multiple_of(ci * R_CHUNK, R_CHUNK)
            for c0 in range(0, A, LANES):
                ac_ref[pl.ds(r0, R_CHUNK), c0:c0 + LANES] = _conv_acc(glu_buf, ca_ref, r0, R_CHUNK, c0, LANES, offs_a)
                bconv_buf[pl.ds(r0, R_CHUNK), c0:c0 + LANES] = _conv_acc(cv_buf, cb_ref, r0, R_CHUNK, c0, LANES, offs_s)
            return carry

        lax.fori_loop(0, ts // R_CHUNK, chunk, 0)
        glu_buf[0:HALO_A, :] = glu_buf[ts:ts + HALO_A, :]
        cv_buf[0:HALO_S, :] = cv_buf[ts:ts + HALO_S, :]

        ac = ac_ref[...]
        xc = ac - jnp.mean(ac, axis=-1, keepdims=True)
        xn = xc * lax.rsqrt(jnp.mean(xc * xc, axis=-1, keepdims=True) + LN_EPS)
        ln = xn * lg_ref[...] + lb_ref[...]
        cat_ref[:, 0:A] = (ln * _sigmoid(ln)).astype(BF16)
        cat_ref[:, A:2 * A] = (z_ref[:, 2 * A:3 * A] * bconv_buf[...]).astype(BF16)
        x1_ref[...] = xv + jnp.dot(cat_ref[...], wout_ref[...], preferred_element_type=F32)

    tile = lambda w: pl.BlockSpec((ts, w), lambda i: (i, 0))
    return _call(
        body, name="mix0_fwd", grid=(n,), comm=comm, args=(x, g, w_in, conv_a, ln_g, ln_b, conv_b, w_out),
        in_specs=[tile(D), _const((1, D)), _const((NCHIP, D, bw)), _const((K_A, A)), _const((1, A)), _const((1, A)),
                  _const((K_S, A)), _const((2 * A, D))],
        out_specs=[tile(D), tile(NZ), tile(A), tile(A), tile(2 * A), tile(D)],
        out_shape=[jax.ShapeDtypeStruct((s, D), BF16), jax.ShapeDtypeStruct((s, NZ), F32), jax.ShapeDtypeStruct((s, A), F32),
                   jax.ShapeDtypeStruct((s, A), F32), jax.ShapeDtypeStruct((s, 2 * A), BF16), jax.ShapeDtypeStruct((s, D), F32)],
        scratch=[pltpu.VMEM((HALO_A + ts, A), F32), pltpu.VMEM((HALO_S + ts, A), F32)])


def _ffn_fwd(x, g, w_up, wc, w_down, name, comm=None, head=None):
    s = x.shape[0]
    ts = min(TS_FFN, s)
    n = s // ts
    bw = FF2 // NCHIP
    rows = 32
    offs = [HALO_S - (K_S - 1) + k for k in range(K_S)]
    n_in = 5 + (2 if head else 0)

    def body(*refs):
        x_ref, g_ref, wup_ref, wc_ref, wdn_ref = refs[:5]
        h_ref, u0_ref, u_ref, act_ref, xo_ref = refs[n_in:n_in + 5]
        cbuf = refs[-1]
        i = pl.program_id(0)

        @pl.when(i == 0)
        def _():
            cbuf[0:HALO_S, :] = jnp.zeros((HALO_S, FF2), F32)
            if head:
                refs[n_in + 5][...] = jnp.zeros((1, D), F32)
                refs[n_in + 6][...] = jnp.zeros((1, LANES), F32)

        xv = x_ref[...]
        h = (xv * _rsqrt_mean_sq(xv) * g_ref[...]).astype(BF16)
        h_ref[...] = h
        f = None
        for p in range(NCHIP // 2):
            for j in (p, NCHIP // 2 + p):
                zc = jnp.dot(h, wup_ref[j], preferred_element_type=F32)
                u0_ref[:, j * bw:(j + 1) * bw] = zc.astype(BF16)
                cbuf[HALO_S:HALO_S + ts, j * bw:(j + 1) * bw] = zc
            for r0 in range(0, ts, rows):
                for c0 in range(p * bw, (p + 1) * bw, LANES):
                    ug = _taps(wc_ref, _shifted(cbuf, r0, rows, c0, LANES, offs), offs, c0, LANES)
                    uv = _taps(wc_ref, _shifted(cbuf, r0, rows, FF + c0, LANES, offs), offs, FF + c0, LANES)
                    u_ref[r0:r0 + rows, c0:c0 + LANES] = ug
                    u_ref[r0:r0 + rows, FF + c0:FF + c0 + LANES] = uv
                    act_ref[r0:r0 + rows, c0:c0 + LANES] = (ug * _sigmoid(ug) * uv).astype(BF16)
            fp = jnp.dot(act_ref[:, p * bw:(p + 1) * bw], wdn_ref[p * bw:(p + 1) * bw, :], preferred_element_type=F32)
            f = fp if f is None else f + fp
        cbuf[0:HALO_S, :] = cbuf[ts:ts + HALO_S, :]
        if not head:
            xo_ref[...] = xv + f
        else:
            gf_ref, t_ref, dgf_ref, loss_ref = refs[5], refs[6], refs[n_in + 5], refs[n_in + 6]
            xo = xv + f
            r = _rsqrt_mean_sq(xo)
            xh = xo * r
            gv = gf_ref[...]
            err = xh * gv - t_ref[...]
            loss_ref[...] += jnp.sum(jnp.sum(err * err, axis=1, keepdims=True), axis=0, keepdims=True) * (0.5 / D)
            dy = err * (1.0 / D)
            dgf_ref[...] += jnp.sum(dy * xh, axis=0, keepdims=True)
            xo_ref[...] = _rms_bwd(dy, xh, r, gv)

    tile = lambda w: pl.BlockSpec((ts, w), lambda i: (i, 0))
    one_row = lambda w: pl.BlockSpec((1, w), lambda i: (0, 0))
    return _call(
        body, name=name, grid=(n,), comm=comm, args=(x, g, w_up, wc, w_down) + (tuple(head) if head else ()),
        in_specs=[tile(D), _const((1, D)), _const((NCHIP, D, bw)), _const((K_S, FF2)), _const((FF, D))]
        + ([_const((1, D)), tile(D)] if head else []),
        out_specs=[tile(D), tile(FF2), tile(FF2), tile(FF), tile(D)] + ([one_row(D), one_row(LANES)] if head else []),
        out_shape=[jax.ShapeDtypeStruct((s, D), BF16), jax.ShapeDtypeStruct((s, FF2), BF16), jax.ShapeDtypeStruct((s, FF2), F32),
                   jax.ShapeDtypeStruct((s, FF), BF16), jax.ShapeDtypeStruct((s, D), F32)]
        + ([jax.ShapeDtypeStruct((1, D), F32), jax.ShapeDtypeStruct((1, LANES), F32)] if head else []),
        scratch=[pltpu.VMEM((HALO_S + ts, FF2), F32)])


def _pool_windows(hbuf, pbuf, tile_row0, ts):
    def chunk(ci, carry):
        r0 = pl.multiple_of(ci * R_CHUNK, R_CHUNK)
        t1 = (tile_row0 + r0 + lax.broadcasted_iota(jnp.int32, (R_CHUNK, 1), 0) + 1).astype(F32)
        for gi, w in enumerate(POOL_WINDOWS):
            cnt = jnp.minimum(t1, float(w))
            offs = [HALO_P - jj for jj in range(w)]
            for c0 in range(gi * PG, (gi + 1) * PG, LANES):
                sh = _shifted(hbuf, r0, R_CHUNK, c0, LANES, offs)
                tot = sh[offs[0]]
                for o in offs[1:]:
                    tot = tot + sh[o]
                pbuf[pl.ds(r0, R_CHUNK), c0:c0 + LANES] = (tot / cnt - sh[HALO_P]).astype(BF16)
        return carry

    lax.fori_loop(0, ts // R_CHUNK, chunk, 0)


def _assemble_wpool(wp_ref, wps):
    rb = PG // NCHIP
    for gi in range(len(POOL_WINDOWS)):
        for j in range(NCHIP):
            wps[gi, j * rb:(j + 1) * rb, :] = wp_ref[j, gi]


def _pool_fwd(x, g, w_pool, scale):
    s = x.shape[0]
    ts = min(TS_POOL, s)
    n = s // ts
    ng = len(POOL_WINDOWS)

    def body(x_ref, g_ref, wp_ref, sc_ref, xo_ref, hbuf, pbuf, wps):
        i = pl.program_id(0)

        @pl.when(i == 0)
        def _():
            hbuf[0:HALO_P, :] = jnp.zeros((HALO_P, D), F32)
            _assemble_wpool(wp_ref, wps)

        xv = x_ref[...]
        hbuf[HALO_P:HALO_P + ts, :] = xv * _rsqrt_mean_sq(xv) * g_ref[...]
        _pool_windows(hbuf, pbuf, i * ts, ts)
        hbuf[0:HALO_P, :] = hbuf[ts:ts + HALO_P, :]
        for gi in range(ng):
            cols = slice(gi * PG, (gi + 1) * PG)
            y = jnp.dot(pbuf[:, cols], wps[gi], preferred_element_type=F32)
            xo_ref[:, cols] = xv[:, cols] + y * sc_ref[:, cols]

    tile = pl.BlockSpec((ts, D), lambda i: (i, 0))
    return pl.pallas_call(
        body, name="pool_fwd", grid=(n,),
        in_specs=[tile, _const((1, D)), _const((NCHIP, ng, PG // NCHIP, PG)), _const((1, D))],
        out_specs=tile, out_shape=jax.ShapeDtypeStruct((s, D), F32),
        scratch_shapes=[pltpu.VMEM((HALO_P + ts, D), F32), pltpu.VMEM((ts, D), BF16), pltpu.VMEM((ng, PG, PG), BF16)],
        compiler_params=_cparams(1),
    )(x, g, w_pool, scale)


def _ffn_bwd_a(dxo, u, u0, wc, w_down, name, comm=None):
    s = dxo.shape[0]
    ts = min(TS_FFN, s)
    n = s // ts
    cw = FF2 // NCHIP
    rows = 32
    lw2 = 2 * LANES
    boffs = [K_S - 1 - k for k in range(K_S)]

    def body(dxo_ref, u_ref, u0_ref, wc_ref, wdn_ref, du0_ref, dwc_ref, dubuf, dact, dwacc):
        i = pl.program_id(0)

        @pl.when(i == 0)
        def _():
            dubuf[ts:ts + HALO_S, :] = jnp.zeros((HALO_S, FF2), F32)
            dwacc[...] = jnp.zeros(dwacc.shape, F32)

        df = dxo_ref[...].astype(BF16)
        for cg in range(0, FF, cw):
            dact[...] = lax.dot_general(df, wdn_ref[cg:cg + cw, :], NT_DIMS, preferred_element_type=F32)

            def chunk(ci, carry, cg=cg):
                rs = pl.ds(pl.multiple_of(ci * rows, rows), rows)
                for c in range(cg, cg + cw, LANES):
                    ug, uv = u_ref[rs, c:c + LANES], u_ref[rs, FF + c:FF + c + LANES]
                    sg = _sigmoid(ug)
                    da = dact[rs, c - cg:c - cg + LANES]
                    gs = ug * sg
                    dubuf[rs, c:c + LANES] = da * uv * (sg + gs * (1.0 - sg))
                    dubuf[rs, FF + c:FF + c + LANES] = da * gs
                return carry

            lax.fori_loop(0, ts // rows, chunk, 0)

        def chunk2(ci, carry):
            r0 = pl.multiple_of(ci * rows, rows)
            rs = pl.ds(r0, rows)
            for c in range(0, FF2, lw2):
                sh = _shifted(dubuf, r0, rows, c, lw2, boffs)
                du0_ref[rs, c:c + lw2] = _taps(wc_ref, sh, boffs, c, lw2).astype(BF16)
                u0v = u0_ref[rs, c:c + lw2].astype(F32)
                for k, o in enumerate(boffs):
                    dwacc[SUBLANES * k:SUBLANES * (k + 1), c:c + lw2] += _rowsum8(sh[o] * u0v)
            return carry

        lax.fori_loop(0, ts // rows, chunk2, 0)
        dubuf[ts:ts + HALO_S, :] = dubuf[0:HALO_S, :]

        @pl.when(i == n - 1)
        def _():
            _finish_tap_sums(dwacc, dwc_ref, K_S)

    rev = lambda w: pl.BlockSpec((ts, w), lambda i: (n - 1 - i, 0))
    return _call(
        body, name=name, grid=(n,), comm=comm, args=(dxo, u, u0, wc, w_down),
        in_specs=[rev(D), rev(FF2), rev(FF2), _const((K_S, FF2)), _const((FF, D))],
        out_specs=[rev(FF2), pl.BlockSpec((K_S, FF2), lambda i: (0, 0))],
        out_shape=[jax.ShapeDtypeStruct((s, FF2), BF16), jax.ShapeDtypeStruct((K_S, FF2), F32)],
        scratch=[pltpu.VMEM((ts + HALO_S, FF2), F32), pltpu.VMEM((ts, cw), F32), pltpu.VMEM((SUBLANES * K_S, FF2), F32)])


def _nt_rms_bwd(dy, w, x, g, dres, name, comm=None, tiles=None, dx_so_far=None):
    s = x.shape[0]
    ts = min(TS_MM, s)
    first, n = (0, s // ts) if tiles is None else tiles
    nw = dy.shape[1]
    bw = nw // NCHIP

    def body(dy_ref, w_ref, x_ref, g_ref, dres_ref, dx_ref, dg_ref):
        i = pl.program_id(0)

        @pl.when(i == 0)
        def _():
            dg_ref[...] = jnp.zeros((1, D), F32)

        dh = lax.dot_general(dy_ref[:, 0:bw], w_ref[0], NT_DIMS, preferred_element_type=F32)
        for j in range(1, NCHIP):
            dh = dh + lax.dot_general(dy_ref[:, j * bw:(j + 1) * bw], w_ref[j], NT_DIMS, preferred_element_type=F32)
        xv = x_ref[...]
        r = _rsqrt_mean_sq(xv)
        xh = xv * r
        dg_ref[...] += jnp.sum(dh * xh, axis=0, keepdims=True)
        dx_ref[...] = dres_ref[...] + _rms_bwd(dh, xh, r, g_ref[...])

    def body_with_alias(dy_ref, w_ref, x_ref, g_ref, dres_ref, _, dx_ref, dg_ref):
        body(dy_ref, w_ref, x_ref, g_ref, dres_ref, dx_ref, dg_ref)

    tile = lambda wd: pl.BlockSpec((ts, wd), lambda i: (first + i, 0))
    in_specs = [tile(nw), _const((NCHIP, D, bw)), tile(D), _const((1, D)), tile(D)]
    more = dx_so_far is not None
    return _call(
        body_with_alias if more else body, name=name, grid=(n,), comm=comm,
        args=(dy, w, x, g, dres) + ((dx_so_far,) if more else ()), in_specs=in_specs + [ANY] * more,
        out_specs=[tile(D), pl.BlockSpec((1, D), lambda i: (0, 0))], aliases={5: 0} if more else None,
        out_shape=[jax.ShapeDtypeStruct((s, D), F32), jax.ShapeDtypeStruct((1, D), F32)])


def _wgrad(a, b, n_blocks, name, comm=None):
    s, m = a.shape
    bw = b.shape[1] // n_blocks
    tk = min(TS_WGRAD, s)
    group = min(g for g in range(1, n_blocks + 1) if n_blocks % g == 0 and (g * bw >= m or g == n_blocks))

    def body(a_ref, b_ref, o_ref):
        part = lambda j: lax.dot_general(a_ref[...], b_ref[:, j * bw:(j + 1) * bw].astype(BF16), TN_DIMS,
                                         preferred_element_type=F32)

        @pl.when(pl.program_id(1) == 0)
        def _():
            for j in range(group):
                o_ref[j] = part(j)

        @pl.when(pl.program_id(1) != 0)
        def _():
            for j in range(group):
                o_ref[j] += part(j)

    (out,), comm_out = _call(
        body, name=name, grid=(n_blocks // group, s // tk), comm=comm, args=(a, b),
        in_specs=[pl.BlockSpec((tk, m), lambda j, k: (k, 0)), pl.BlockSpec((tk, group * bw), lambda j, k: (k, j))],
        out_specs=[pl.BlockSpec((group, m, bw), lambda j, k: (j, 0, 0))],
        out_shape=[jax.ShapeDtypeStruct((n_blocks, m, bw), F32)])
    return out, comm_out


def _pool_bwd(dxo, x, g, w_pool, scale, comm=None):
    s = x.shape[0]
    ts = min(TS_POOL, s)
    n = s // ts
    ng = len(POOL_WINDOWS)
    rb = PG // NCHIP

    def body(dxo_ref, x_ref, halo_ref, g_ref, wp_ref, sc_ref, dx_ref, dwp_ref, dsc_ref, dg_ref,
             hbuf, pbuf, qbuf, dhbuf, wps, dwacc):
        i = pl.program_id(0)
        j = n - 1 - i

        @pl.when(i == 0)
        def _():
            qbuf[ts:ts + HALO_P, :] = jnp.zeros((HALO_P, D), F32)
            dwacc[...] = jnp.zeros(dwacc.shape, F32)
            dsc_ref[...] = jnp.zeros((1, D), F32)
            dg_ref[...] = jnp.zeros((1, D), F32)
            _assemble_wpool(wp_ref, wps)

        gv = g_ref[...]
        xl = halo_ref[...]
        hbuf[0:HALO_P, :] = jnp.where(j == 0, 0.0, xl * _rsqrt_mean_sq(xl) * gv)
        xv = x_ref[...]
        r = _rsqrt_mean_sq(xv)
        xh = xv * r
        hbuf[HALO_P:HALO_P + ts, :] = xh * gv
        _pool_windows(hbuf, pbuf, j * ts, ts)

        dy = dxo_ref[...]
        t1 = (j * ts + lax.broadcasted_iota(jnp.int32, (ts, 1), 0) + 1).astype(F32)
        for gi, w in enumerate(POOL_WINDOWS):
            cols = slice(gi * PG, (gi + 1) * PG)
            p = pbuf[:, cols]
            y = jnp.dot(p, wps[gi], preferred_element_type=F32)
            dsc_ref[:, cols] += jnp.sum(dy[:, cols] * y, axis=0, keepdims=True)
            dq = (dy[:, cols] * sc_ref[:, cols]).astype(BF16)
            dwacc[gi] += lax.dot_general(p, dq, TN_DIMS, preferred_element_type=F32)
            dp = lax.dot_general(dq, wps[gi], NT_DIMS, preferred_element_type=F32)
            qbuf[0:ts, cols] = dp / jnp.minimum(t1, float(w))

        def chunk(ci, carry):
            r0 = pl.multiple_of(ci * R_CHUNK, R_CHUNK)
            tc = (j * ts + r0 + lax.broadcasted_iota(jnp.int32, (R_CHUNK, 1), 0) + 1).astype(F32)
            for gi, w in enumerate(POOL_WINDOWS):
                cnt = jnp.minimum(tc, float(w))
                offs = list(range(w))
                for c0 in range(gi * PG, (gi + 1) * PG, LANES):
                    sh = _shifted(qbuf, r0, R_CHUNK, c0, LANES, offs)
                    tot = sh[0]
                    for o in offs[1:]:
                        tot = tot + sh[o]
                    dhbuf[pl.ds(r0, R_CHUNK), c0:c0 + LANES] = tot - sh[0] * cnt
            return carry

        lax.fori_loop(0, ts // R_CHUNK, chunk, 0)
        qbuf[ts:ts + HALO_P, :] = qbuf[0:HALO_P, :]
        dh = dhbuf[...]
        dg_ref[...] += jnp.sum(dh * xh, axis=0, keepdims=True)
        dx_ref[...] = dy + _rms_bwd(dh, xh, r, gv)

        @pl.when(i == n - 1)
        def _():
            for gi in range(ng):
                for jj in range(NCHIP):
                    dwp_ref[jj, gi] = dwacc[gi, jj * rb:(jj + 1) * rb, :]

    rev = pl.BlockSpec((ts, D), lambda i: (n - 1 - i, 0))
    halo = pl.BlockSpec((HALO_P, D), lambda i: (jnp.maximum((n - 1 - i) * (ts // HALO_P) - 1, 0), 0))
    vec = pl.BlockSpec((1, D), lambda i: (0, 0))
    return _call(
        body, name="pool_bwd", grid=(n,), comm=comm, args=(dxo, x, x, g, w_pool, scale),
        in_specs=[rev, rev, halo, _const((1, D)), _const((NCHIP, ng, rb, PG)), _const((1, D))],
        out_specs=[rev, pl.BlockSpec((NCHIP, ng, rb, PG), lambda i: (0, 0, 0, 0)), vec, vec],
        out_shape=[jax.ShapeDtypeStruct((s, D), F32), jax.ShapeDtypeStruct((NCHIP, ng, rb, PG), F32),
                   jax.ShapeDtypeStruct((1, D), F32), jax.ShapeDtypeStruct((1, D), F32)],
        scratch=[pltpu.VMEM((HALO_P + ts, D), F32), pltpu.VMEM((ts, D), BF16), pltpu.VMEM((ts + HALO_P, D), F32),
                 pltpu.VMEM((ts, D), F32), pltpu.VMEM((ng, PG, PG), BF16), pltpu.VMEM((ng, PG, PG), F32)])


def _mix0_bwd_a(dx1, z, ac, bconv, w_out, conv_a, ln_g, ln_b, conv_b, comm=None):
    s = dx1.shape[0]
    ts = min(TS_MIXB, s)
    n = s // ts
    rows = 32
    boffs_a = [K_A - 1 - k for k in range(K_A)]
    boffs_s = [K_S - 1 - k for k in range(K_S)]

    def body(dx1_ref, z_ref, ac_ref, bconv_ref, wout_ref, ca_ref, lg_ref, lb_ref, cb_ref,
             dz_ref, dca_ref, dlg_ref, dlb_ref, dcb_ref, dac_buf, dbc_buf, dca_acc, dcb_acc):
        i = pl.program_id(0)

        @pl.when(i == 0)
        def _():
            dac_buf[ts:ts + HALO_A, :] = jnp.zeros((HALO_A, A), F32)
            dbc_buf[ts:ts + HALO_S, :] = jnp.zeros((HALO_S, A), F32)
            dca_acc[...] = jnp.zeros(dca_acc.shape, F32)
            dcb_acc[...] = jnp.zeros(dcb_acc.shape, F32)
            dlg_ref[...] = jnp.zeros((1, A), F32)
            dlb_ref[...] = jnp.zeros((1, A), F32)

        dcat = lax.dot_general(dx1_ref[...].astype(BF16), wout_ref[...], NT_DIMS, preferred_element_type=F32)
        db = dcat[:, A:2 * A]
        dz_ref[:, 2 * A:3 * A] = (db * bconv_ref[...]).astype(BF16)
        dbc_buf[0:ts, :] = db * z_ref[:, 2 * A:3 * A]
        ac = ac_ref[...]
        xc = ac - jnp.mean(ac, axis=-1, keepdims=True)
        rstd = lax.rsqrt(jnp.mean(xc * xc, axis=-1, keepdims=True) + LN_EPS)
        xn = xc * rstd
        lg = lg_ref[...]
        ln = xn * lg + lb_ref[...]
        sl = _sigmoid(ln)
        dln = dcat[:, 0:A] * sl * (1.0 + ln * (1.0 - sl))
        dlg_ref[...] += jnp.sum(dln * xn, axis=0, keepdims=True)
        dlb_ref[...] += jnp.sum(dln, axis=0, keepdims=True)
        dxn = dln * lg
        dac_buf[0:ts, :] = rstd * (dxn - jnp.mean(dxn, axis=-1, keepdims=True)
                                   - xn * jnp.mean(dxn * xn, axis=-1, keepdims=True))

        def chunk(ci, carry):
            r0 = pl.multiple_of(ci * rows, rows)
            rs = pl.ds(r0, rows)
            for c0 in range(0, A, LANES):
                col = lambda grp: slice(grp * A + c0, grp * A + c0 + LANES)
                a_val, sg = z_ref[rs, col(0)], _sigmoid(z_ref[rs, col(1)])
                dglu = _conv_corr(dac_buf, ca_ref, a_val * sg, dca_acc, r0, rows, c0, LANES, boffs_a)
                dz_ref[rs, col(0)] = (dglu * sg).astype(BF16)
                dz_ref[rs, col(1)] = (dglu * a_val * sg * (1.0 - sg)).astype(BF16)
                c_gate, bc_val = z_ref[rs, col(3)], z_ref[rs, col(4)]
                dcv = _conv_corr(dbc_buf, cb_ref, c_gate * bc_val, dcb_acc, r0, rows, c0, LANES, boffs_s)
                dz_ref[rs, col(3)] = (dcv * bc_val).astype(BF16)
                dz_ref[rs, col(4)] = (dcv * c_gate).astype(BF16)
            return carry

        lax.fori_loop(0, ts // rows, chunk, 0)
        dac_buf[ts:ts + HALO_A, :] = dac_buf[0:HALO_A, :]
        dbc_buf[ts:ts + HALO_S, :] = dbc_buf[0:HALO_S, :]

        @pl.when(i == n - 1)
        def _():
            _finish_tap_sums(dca_acc, dca_ref, K_A)
            _finish_tap_sums(dcb_acc, dcb_ref, K_S)

    rev = lambda w: pl.BlockSpec((ts, w), lambda i: (n - 1 - i, 0))
    full = lambda r, c: pl.BlockSpec((r, c), lambda i: (0, 0))
    return _call(
        body, name="mix0_bwd_a", grid=(n,), comm=comm, args=(dx1, z, ac, bconv, w_out, conv_a, ln_g, ln_b, conv_b),
        in_specs=[rev(D), rev(NZ), rev(A), rev(A), _const((2 * A, D)), _const((K_A, A)), _const((1, A)), _const((1, A)),
                  _const((K_S, A))],
        out_specs=[rev(NZ), full(K_A, A), full(1, A), full(1, A), full(K_S, A)],
        out_shape=[jax.ShapeDtypeStruct((s, NZ), BF16), jax.ShapeDtypeStruct((K_A, A), F32), jax.ShapeDtypeStruct((1, A), F32),
                   jax.ShapeDtypeStruct((1, A), F32), jax.ShapeDtypeStruct((K_S, A), F32)],
        scratch=[pltpu.VMEM((ts + HALO_A, A), F32), pltpu.VMEM((ts + HALO_S, A), F32), pltpu.VMEM((SUBLANES * K_A, A), F32),
                 pltpu.VMEM((SUBLANES * K_S, A), F32)])


def _cast_later_weights(w_up, w_down, w_pool, comm):
    nl = w_up.shape[0]

    def body(up_ref, dn_ref, pool_ref, *outs):
        up_o, dn_o, pool_o = outs[:nl], outs[nl:2 * nl], outs[2 * nl]
        for layer in range(nl):
            @pl.when(pl.program_id(0) == layer)
            def _(layer=layer):
                up_o[layer][...] = up_ref[...].astype(BF16)
                dn_o[layer][...] = dn_ref[...].astype(BF16)
                if layer == 0:
                    pool_o[...] = pool_ref[...].astype(BF16)

    per_layer = lambda a: pl.BlockSpec((None,) + a.shape[1:], lambda l: (l,) + (0,) * (a.ndim - 1))
    whole = lambda shape: pl.BlockSpec(shape, lambda l: (0,) * len(shape))
    out_shapes = [w_up.shape[1:]] * nl + [w_down.shape[1:]] * nl + [w_pool.shape]
    return _call(body, name="cast_later_weights", grid=(nl,), comm=comm, args=(w_up, w_down, w_pool),
                 in_specs=[per_layer(w_up), per_layer(w_down), whole(w_pool.shape)],
                 out_specs=[whole(shape) for shape in out_shapes],
                 out_shape=[jax.ShapeDtypeStruct(shape, BF16) for shape in out_shapes])


def _train_step(x, target, place, shard, later, rep, small_pack, small_shapes, adamw):
    bw_up, bw_in = FF2 // NCHIP, NZ // NCHIP
    five = lambda a, k: a.reshape(NCHIP, 2, *HALF[k])

    half = lambda a, name: a.reshape((2,) + HALF[name])
    (up0, up1, dn0, dn1, pool_bf), (g_in, g_out, small_g) = _cast_later_weights(
        *later, comm=_gather_comm([shard["w_in"], shard["w_out"]], small_pack))
    shard = dict(shard, w_pool=half(pool_bf, "w_pool"), w_up0=half(up0, "w_up"), w_up1=half(up1, "w_up"),
                 w_down0=half(dn0, "w_down"), w_down1=half(dn1, "w_down"))
    whole = {}
    for k, part in zip(SMALL_SHARDED, _unpack(small_g, small_shapes, lead=(NCHIP,))):
        whole[k] = jnp.moveaxis(part, 0, 1).reshape(part.shape[1], NCHIP * part.shape[2])
    w_in, w_out = g_in.reshape(NCHIP, D, bw_in), g_out.reshape(2 * A, D)
    conv_ffn = whole["conv_ffn_w"].reshape(2, K_S, FF2)
    nffn = [rep["norm_ffn"][0:1], rep["norm_ffn"][1:2]]

    (h0, z, ac, bconv, cat, x1), (g_up0, g_dn0) = _mix0_fwd(
        x, rep["norm_mix_even"], w_in, whole["conv_a"], rep["ln_a_g"], rep["ln_a_b"], whole["conv_b"], w_out,
        comm=_gather_comm([shard["w_up0"], shard["w_down0"]], mid_steps=GATHER_MID_STEPS["mix0_fwd"]))
    w_up0, w_dn0 = g_up0.reshape(NCHIP, D, bw_up), g_dn0.reshape(FF, D)
    (hf0, u00, u0, act0, x2), (g_pool, g_up1, g_dn1) = _ffn_fwd(
        x1, nffn[0], w_up0, conv_ffn[0], w_dn0, "ffn0_fwd",
        comm=_gather_comm([shard["w_pool"], shard["w_up1"], shard["w_down1"]], mid_steps=GATHER_MID_STEPS["ffn0_fwd"]))
    w_pool = g_pool.reshape(NCHIP, len(POOL_WINDOWS), PG // NCHIP, PG)
    w_up1, w_dn1 = g_up1.reshape(NCHIP, D, bw_up), g_dn1.reshape(FF, D)
    x3 = _pool_fwd(x2, whole["norm_mix_odd"], w_pool, whole["pool_scale"])
    (hf1, u01, u1, act1, dx4, g_nfin, loss_part), _ = _ffn_fwd(x3, nffn[1], w_up1, conv_ffn[1], w_dn1, "ffn1_fwd",
                                                                head=(rep["norm_final"], target))

    psum = lambda k, g, ld, tag="": _pair_sum(place, g, ld, "pair_sum_" + k + tag)
    tot = lambda k, g, ld, p, layer=0, nl=1, prev=None: _chip_sum(place, g, ld, p, layer, nl, "chip_sum_%s%d" % (k, layer), prev)
    pair, chips, join = _pair_comm, _chips_comm, _join_comm

    gr_dn1 = five(_wgrad(act1, dx4, 1, "wgrad_down1")[0], "w_down")
    (du01, g_wc1), _ = _ffn_bwd_a(dx4, u1, u01, conv_ffn[1], w_dn1, "ffn1_bwd_a")
    g_up, (ld_dn1,) = _wgrad(hf1, du01, NCHIP, "wgrad_up1", comm=pair([gr_dn1]))
    gr_up1 = five(g_up, "w_up")
    s_dn1 = psum("w_down", gr_dn1, ld_dn1, "1")
    (dx3, g_nf1), (p_dn1, ld_up1) = _nt_rms_bwd(du01, w_up1, x3, nffn[1], dx4, "ffn1_bwd_b",
                                                comm=join(chips([s_dn1]), pair([gr_up1])))
    s_up1 = psum("w_up", gr_up1, ld_up1, "1")
    (dx2, g_wpool, g_scale, g_nmo), _ = _pool_bwd(dx3, x2, whole["norm_mix_odd"], w_pool, whole["pool_scale"])
    gr_pool = five(g_wpool, "w_pool")
    (du00, g_wc0), (p_up1,) = _ffn_bwd_a(dx2, u0, u00, conv_ffn[0], w_dn0, "ffn0_bwd_a", comm=chips([s_up1]))
    gr_dn0 = five(_wgrad(act0, dx2, 1, "wgrad_down0")[0], "w_down")
    g_up, (ld_dn0, ld_pool) = _wgrad(hf0, du00, NCHIP, "wgrad_up0", comm=pair([gr_dn0, gr_pool]))
    gr_up0 = five(g_up, "w_up")
    s_dn0, s_pool = psum("w_down", gr_dn0, ld_dn0, "0"), psum("w_pool", gr_pool, ld_pool)
    (dx1, g_nf0), (p_dn0, p_pool, ld_up0) = _nt_rms_bwd(du00, w_up0, x1, nffn[0], dx2, "ffn0_bwd_b",
                                                        comm=join(chips([s_dn0, s_pool]), pair([gr_up0])))
    s_up0 = psum("w_up", gr_up0, ld_up0, "0")
    r_early = s_up0.shape[1] // 4
    g_out, (p_up0_early,) = _wgrad(cat, dx1, 1, "wgrad_out", comm=chips([s_up0[:, :r_early]]))
    gr_out = five(g_out, "w_out")
    (dz, g_ca, g_lg, g_lb, g_cb), (p_up0_rest, ld_out) = _mix0_bwd_a(
        dx1, z, ac, bconv, w_out, whole["conv_a"], rep["ln_a_g"], rep["ln_a_b"], whole["conv_b"],
        comm=join(chips([s_up0[:, r_early:]]), pair([gr_out])))
    p_up0 = jnp.concatenate([p_up0_early, p_up0_rest], axis=1)
    s_out = psum("w_out", gr_out, ld_out)
    t_pool = tot("w_pool", gr_pool, ld_pool, p_pool)
    t_up = tot("w_up", gr_up0, ld_up0, p_up0, 0, 2, tot("w_up", gr_up1, ld_up1, p_up1, 1, 2))
    t_dn = tot("w_down", gr_dn0, ld_dn0, p_dn0, 0, 2, tot("w_down", gr_dn1, ld_dn1, p_dn1, 1, 2))
    small = {"norm_mix_odd": g_nmo, "pool_scale": g_scale, "norm_ffn": jnp.concatenate([g_nf0, g_nf1], axis=0),
             "conv_ffn_w": jnp.stack([g_wc0, g_wc1]), "norm_final": g_nfin, "loss": loss_part,
             "conv_a": g_ca, "ln_a_g": g_lg, "ln_a_b": g_lb, "conv_b": g_cb}
    g_in, (p_out, early_all, t_pool, t_up, t_dn) = _wgrad(
        h0, dz, NCHIP, "wgrad_in",
        comm=join(chips([s_out], _pack([small[k] for k in SMALL_EARLY])), _swap_comm([t_pool, t_up, t_dn])))
    gr_in = five(g_in, "w_in")
    n_mm = x.shape[0] // min(TS_MM, x.shape[0])
    n_a = max(1, n_mm // 4)
    last_bwd = functools.partial(_nt_rms_bwd, dz, w_in, x, rep["norm_mix_even"], dx1)
    (dx_a, g_nme_a), (ld_in,) = last_bwd("mix0_bwd_b0", comm=pair([gr_in]), tiles=(0, n_a))
    s_in = psum("w_in", gr_in, ld_in)
    (grad_x, g_nme_b), (p_in,) = last_bwd("mix0_bwd_b1", comm=chips([s_in]), tiles=(n_a, n_mm - n_a), dx_so_far=dx_a)
    small["norm_mix_even"] = g_nme_a + g_nme_b
    late_all, t_in, t_out = _run_comm(
        join(chips([], _pack([small[k] for k in SMALL_LATE])),
             _swap_comm([tot("w_in", gr_in, ld_in, p_in), tot("w_out", gr_out, ld_out, p_out)])), "rs_last")
    done = {k: adamw(k, t) for k, t in (("w_up", t_up), ("w_down", t_dn), ("w_pool", t_pool), ("w_in", t_in), ("w_out", t_out))}
    summed = dict(zip(SMALL_EARLY, _unpack(_sum_devices(early_all, "sum_small_early"), [small[k].shape for k in SMALL_EARLY])))
    summed.update(zip(SMALL_LATE, _unpack(_sum_devices(late_all, "sum_small_late"), [small[k].shape for k in SMALL_LATE])))
    return grad_x, done, summed


def _adamw_math(w, g, m, v):
    m = ADAM_B1 * m + (1.0 - ADAM_B1) * g
    v = ADAM_B2 * v + (1.0 - ADAM_B2) * (g * g)
    m_hat = m / (1.0 - ADAM_B1 ** ADAM_STEP)
    v_hat = v / (1.0 - ADAM_B2 ** ADAM_STEP)
    return -ADAM_LR * (m_hat / (jnp.sqrt(v_hat) + ADAM_EPS) + ADAM_WD * w), m, v


def _adamw_big(w, g, m, v, tr, name, comm=None):
    nl, rows, cols = w.shape

    def body(w_ref, g_ref, m_ref, v_ref, g2_ref, d_ref, m2_ref, v2_ref):
        gv = g_ref[...]
        g2_ref[...] = gv
        d_ref[...], m2_ref[...], v2_ref[...] = _adamw_math(w_ref[...], gv, m_ref[...], v_ref[...])

    spec = pl.BlockSpec((None, tr, cols), lambda l, r: (l, r, 0))
    return _call(body, name=name, grid=(nl, rows // tr), comm=comm, args=(w, g, m, v), in_specs=[spec] * 4,
                 out_specs=[spec] * 4, out_shape=[jax.ShapeDtypeStruct(w.shape, F32)] * 4)


def _adamw_small(ws, gs, ms, vs):
    n = len(ws)

    def body(*refs):
        for p in range(n):
            w_ref, g_ref, m_ref, v_ref = (refs[q * n + p] for q in range(4))
            d_ref, m2_ref, v2_ref = (refs[(4 + q) * n + p] for q in range(3))
            d_ref[...], m2_ref[...], v2_ref[...] = _adamw_math(w_ref[...], g_ref[...], m_ref[...], v_ref[...])

    whole = lambda a: pl.BlockSpec(a.shape, lambda: (0,) * a.ndim)
    outs = pl.pallas_call(
        body, name="adamw_small", in_specs=[whole(a) for a in ws] * 4, out_specs=[whole(a) for a in ws] * 3,
        out_shape=[jax.ShapeDtypeStruct(a.shape, F32) for a in ws] * 3,
        compiler_params=pltpu.CompilerParams(vmem_limit_bytes=VMEM_LIMIT_BYTES),
    )(*ws, *gs, *ms, *vs)
    return outs[0:n], outs[n:2 * n], outs[2 * n:3 * n]


def _place():
    x, y, c = lax.axis_index("x"), lax.axis_index("y"), lax.axis_index("c")
    chips = [(x, 1 - y), (1 - x, y), (1 - x, 1 - y)]
    blocks = [2 * cx + cy for cx, cy in chips]
    return x, y, c, 2 * x + y, chips, blocks


def _gather_comm(shards, small=None, mid_steps=None):
    na = len(shards)
    ns = 0 if small is None else 1

    def copies(ins, outs, sems):
        h1_send, h1_recv, h2_send, h2_recv, f1_send, f1_recv, f2_send, f2_recv, own_send, own_recv = sems[:10]
        x, y, c, j, chips, blocks = _place()
        sib = (x, y, 1 - c)
        piece = lambda a, p: pl.ds(p * (shards[a].shape[1] // 2), shards[a].shape[1] // 2)

        def remote(src, dst, send, recv, q, to):
            return pltpu.make_async_remote_copy(src_ref=src, dst_ref=dst, send_sem=send.at[q], recv_sem=recv.at[q],
                                                device_id=to, device_id_type=MESH)

        def hop1(a, k, arrival):
            dst = outs[a].at[blocks[k], c] if arrival else outs[a].at[j, c]
            return remote(dst if arrival else ins[a].at[c], dst, h1_send, h1_recv, 2 * a + k, (*chips[k], c))

        def hop2(a, p, arrival):
            ref = outs[a].at[blocks[2] if arrival else blocks[p], c, piece(a, p)]
            return remote(ref, ref, h2_send, h2_recv, 2 * a + p, (*chips[1 - p], c))

        def fwd1(a, k, half):
            ref = outs[a].at[blocks[k], half]
            return remote(ref, ref, f1_send, f1_recv, 2 * a + k, sib)

        def fwd2(a, p, half):
            ref = outs[a].at[blocks[2], half, piece(a, p)]
            return remote(ref, ref, f2_send, f2_recv, 2 * a + p, sib)

        own = [remote(ins[a], outs[a].at[j], own_send, own_recv, a, sib) for a in range(na)]
        small_copies = [remote(ins[na], outs[na].at[j], sems[10], sems[11], k, (*chips[k], c)) for k in range(3 * ns)]
        local = [pltpu.make_async_copy(ins[na], outs[na].at[j], sems[12])] if ns else []
        return hop1, hop2, fwd1, fwd2, own, small_copies, local, c

    pairs = [(a, k) for a in range(na) for k in range(2)]

    def start(ins, outs, sems):
        hop1, _, _, _, own, small_copies, local, _ = copies(ins, outs, sems)
        for cp in local + own + [hop1(a, k, False) for a, k in pairs] + small_copies:
            cp.start()

    def mid(ins, outs, sems):
        hop1, hop2, fwd1, _, _, _, _, c = copies(ins, outs, sems)
        for a, k in pairs:
            hop1(a, k, True).wait_recv()
            hop2(a, k, False).start()
            fwd1(a, k, c).start()

    def finish(ins, outs, sems):
        hop1, hop2, fwd1, fwd2, own, small_copies, local, c = copies(ins, outs, sems)
        for a, p in pairs:
            hop2(a, p, True).wait_recv()
            fwd2(a, p, c).start()
        for cp in small_copies:
            cp.wait()
        for a, k in pairs:
            hop1(a, k, False).wait_send()
            hop2(a, k, False).wait_send()
            fwd1(a, k, c).wait_send()
            fwd1(a, k, 1 - c).wait_recv()
            fwd2(a, k, c).wait_send()
            fwd2(a, k, 1 - c).wait_recv()
        for cp in own + local:
            cp.wait()

    out_shapes = [jax.ShapeDtypeStruct((NCHIP,) + s.shape, s.dtype) for s in shards]
    sems = [pltpu.SemaphoreType.DMA((2 * na,))] * 8 + [pltpu.SemaphoreType.DMA((na,))] * 2
    if ns:
        out_shapes.append(jax.ShapeDtypeStruct((NCHIP,) + small.shape, small.dtype))
        sems += [pltpu.SemaphoreType.DMA((3,)), pltpu.SemaphoreType.DMA((3,)), pltpu.SemaphoreType.DMA]
    if mid_steps is not None:
        return _Comm(list(shards) + [small] * ns, out_shapes, sems, start, finish, mid=mid, mid_steps=mid_steps)

    def forward_and_finish(ins, outs, sems):
        mid(ins, outs, sems)
        finish(ins, outs, sems)

    return _Comm(list(shards) + [small] * ns, out_shapes, sems, start, forward_and_finish)


def _simple_comm(inputs, out_shapes, make_copies, n_sems, aliases=None):
    def start(ins, outs, sems):
        for cp in make_copies(ins, outs, sems):
            cp.start()

    def finish(ins, outs, sems):
        for cp in make_copies(ins, outs, sems):
            cp.wait()

    return _Comm(inputs, out_shapes, [pltpu.SemaphoreType.DMA((n,)) for n in n_sems], start, finish, aliases)


def _pair_comm(grads):
    def make_copies(ins, outs, sems):
        x, y, c, _, _, _ = _place()
        return [pltpu.make_async_remote_copy(
            src_ref=ins[a].at[:, 1 - c], dst_ref=outs[a], send_sem=sems[0].at[a], recv_sem=sems[1].at[a],
            device_id=(x, y, 1 - c), device_id_type=MESH) for a in range(len(grads))]

    out_shapes = [jax.ShapeDtypeStruct(g.shape[:1] + g.shape[2:], F32) for g in grads]
    return _simple_comm(grads, out_shapes, make_copies, [len(grads)] * 2)


def _chips_comm(sums, small=None):
    na = len(sums)
    nk = NCHIP - 1

    def make_copies(ins, outs, sems):
        x, y, c, _, chips, _ = _place()
        copies = [pltpu.make_async_remote_copy(
            src_ref=ins[a].at[k], dst_ref=outs[a].at[k], send_sem=sems[0].at[a * nk + k],
            recv_sem=sems[1].at[a * nk + k], device_id=(*chips[k], c), device_id_type=MESH)
            for a in range(na) for k in range(nk)]
        if small is not None:
            me = 4 * x + 2 * y + c
            for r in range(1, NDEV):
                peer = (1 - x if r & 4 else x, 1 - y if r & 2 else y, 1 - c if r & 1 else c)
                copies.append(pltpu.make_async_remote_copy(
                    src_ref=ins[na], dst_ref=outs[na].at[me], send_sem=sems[2].at[r - 1], recv_sem=sems[3].at[r - 1],
                    device_id=peer, device_id_type=MESH))
            copies.append(pltpu.make_async_copy(ins[na], outs[na].at[me], sems[4].at[0]))
        return copies

    out_shapes = [jax.ShapeDtypeStruct(g.shape, BF16) for g in sums]
    chip_sems = [max(na * nk, 1)] * 2
    if small is None:
        return _simple_comm(sums, out_shapes, make_copies, chip_sems)
    out_shapes.append(jax.ShapeDtypeStruct((NDEV,) + small.shape, F32))
    return _simple_comm(list(sums) + [small], out_shapes, make_copies, chip_sems + [NDEV - 1] * 2 + [1])


def _swap_comm(totals):
    def make_copies(ins, outs, sems):
        x, y, c, _, _, _ = _place()
        return [pltpu.make_async_remote_copy(
            src_ref=outs[a].at[:, c], dst_ref=outs[a].at[:, c], send_sem=sems[0].at[a], recv_sem=sems[1].at[a],
            device_id=(x, y, 1 - c), device_id_type=MESH) for a in range(len(totals))]

    out_shapes = [jax.ShapeDtypeStruct(t.shape, F32) for t in totals]
    return _simple_comm(totals, out_shapes, make_copies, [len(totals)] * 2, aliases={a: a for a in range(len(totals))})


def _pair_sum(place, grad, landed, name):
    _, _, rows, cols = grad.shape

    def body(place_ref, g_ref, l_ref, o_ref):
        o_ref[...] = (g_ref[...] + l_ref[...]).astype(BF16)

    other = lambda k, p: jnp.bitwise_xor(p[0], k + 1)
    return pl.pallas_call(
        body, name=name,
        grid_spec=pltpu.PrefetchScalarGridSpec(
            num_scalar_prefetch=1, grid=(NCHIP - 1,),
            in_specs=[pl.BlockSpec((None, None, rows, cols), lambda k, p: (other(k, p), p[1], 0, 0)),
                      pl.BlockSpec((None, rows, cols), lambda k, p: (other(k, p), 0, 0))],
            out_specs=pl.BlockSpec((None, rows, cols), lambda k, p: (k, 0, 0))),
        out_shape=jax.ShapeDtypeStruct((NCHIP - 1, rows, cols), BF16), compiler_params=_cparams(1),
    )(place, grad, landed)


def _chip_sum(place, grad, landed, parts, layer, n_layers, name, prev=None):
    _, _, rows, cols = grad.shape

    def body(*refs):
        g_ref, l_ref, p_ref, o_ref = refs[1], refs[2], refs[3], refs[-1]
        tot = g_ref[...] + l_ref[...]
        for k in range(NCHIP - 1):
            tot = tot + p_ref[k].astype(F32)
        o_ref[...] = tot

    in_specs = [pl.BlockSpec((None, None, rows, cols), lambda i, p: (p[0], p[1], 0, 0)),
                pl.BlockSpec((None, rows, cols), lambda i, p: (p[0], 0, 0)),
                pl.BlockSpec((NCHIP - 1, rows, cols), lambda i, p: (0, 0, 0))]
    args = [place, grad, landed, parts]
    if prev is not None:
        in_specs.append(ANY)
        args.append(prev)
    return pl.pallas_call(
        body, name=name,
        grid_spec=pltpu.PrefetchScalarGridSpec(
            num_scalar_prefetch=1, grid=(1,), in_specs=in_specs,
            out_specs=pl.BlockSpec((None, None, rows, cols), lambda i, p: (layer, p[1], 0, 0))),
        out_shape=jax.ShapeDtypeStruct((n_layers, 2, rows, cols), F32),
        input_output_aliases={} if prev is None else {4: 0}, compiler_params=_cparams(1),
    )(*args)


def _sum_devices(parts, name):
    def body(p_ref, o_ref):
        tot = p_ref[0]
        for d in range(1, NDEV):
            tot = tot + p_ref[d]
        o_ref[...] = tot

    return pl.pallas_call(
        body, name=name, in_specs=[pl.BlockSpec(parts.shape, lambda: (0, 0, 0))],
        out_specs=pl.BlockSpec(parts.shape[1:], lambda: (0, 0)), out_shape=jax.ShapeDtypeStruct(parts.shape[1:], F32),
    )(parts)


def _pack(parts):
    rows = []
    for p in parts:
        p = p.reshape(-1, LANES)
        rows.append(jnp.pad(p, ((0, -p.shape[0] % SUBLANES), (0, 0))))
    return jnp.concatenate(rows, axis=0)


def _unpack(buf, shapes, lead=()):
    out, r0 = [], 0
    nl = len(lead)
    for shp in shapes:
        nrow = 1
        for d in shp:
            nrow *= d
        nrow //= LANES
        out.append(buf[(slice(None),) * nl + (slice(r0, r0 + nrow),)].reshape(lead + tuple(shp)))
        r0 += nrow + (-nrow % SUBLANES)
    return out


WEIGHT_ORDER = ("norm_mix_even", "w_in", "conv_a", "ln_a_g", "ln_a_b", "conv_b", "w_out", "norm_mix_odd", "w_pool",
                "pool_scale", "norm_ffn", "w_up", "conv_ffn_w", "w_down", "norm_final")
BIG = ("w_in", "w_out", "w_pool", "w_up", "w_down")
HALF = {"w_in": (D // 2, NZ // NCHIP), "w_out": (2 * A // NCHIP // 2, D), "w_pool": (PG // 2, PG),
        "w_up": (D // 2, FF2 // NCHIP), "w_down": (FF // NCHIP // 2, D)}
SMALL_SHARDED = ("conv_a", "conv_b", "conv_ffn_w", "norm_mix_odd", "pool_scale")
SMALL_ALL = ("norm_mix_even", "conv_a", "ln_a_g", "ln_a_b", "conv_b", "norm_mix_odd", "pool_scale", "norm_ffn", "conv_ffn_w",
             "norm_final")
SMALL_EARLY = ("norm_mix_odd", "pool_scale", "norm_ffn", "conv_ffn_w", "norm_final", "loss", "conv_a", "ln_a_g", "ln_a_b", "conv_b")
SMALL_LATE = ("norm_mix_even",)


def kernel(x, norm_mix_even, w_in, conv_a, ln_a_g, ln_a_b, conv_b, w_out, norm_mix_odd, w_pool, pool_scale, norm_ffn, w_up, conv_ffn_w, w_down, norm_final, loss_target, m_norm_mix_even, m_w_in, m_conv_a, m_ln_a_g, m_ln_a_b, m_conv_b, m_w_out, m_norm_mix_odd, m_w_pool, m_pool_scale, m_norm_ffn, m_w_up, m_conv_ffn_w, m_w_down, m_norm_final, v_norm_mix_even, v_w_in, v_conv_a, v_ln_a_g, v_ln_a_b, v_conv_b, v_w_out, v_norm_mix_odd, v_w_pool, v_pool_scale, v_norm_ffn, v_w_up, v_conv_ffn_w, v_w_down, v_norm_final):
    w = dict(norm_mix_even=norm_mix_even, w_in=w_in, conv_a=conv_a, ln_a_g=ln_a_g, ln_a_b=ln_a_b, conv_b=conv_b, w_out=w_out,
             norm_mix_odd=norm_mix_odd, w_pool=w_pool, pool_scale=pool_scale, norm_ffn=norm_ffn, w_up=w_up,
             conv_ffn_w=conv_ffn_w, w_down=w_down, norm_final=norm_final)
    m = dict(norm_mix_even=m_norm_mix_even, w_in=m_w_in, conv_a=m_conv_a, ln_a_g=m_ln_a_g, ln_a_b=m_ln_a_b, conv_b=m_conv_b,
             w_out=m_w_out, norm_mix_odd=m_norm_mix_odd, w_pool=m_w_pool, pool_scale=m_pool_scale, norm_ffn=m_norm_ffn,
             w_up=m_w_up, conv_ffn_w=m_conv_ffn_w, w_down=m_w_down, norm_final=m_norm_final)
    v = dict(norm_mix_even=v_norm_mix_even, w_in=v_w_in, conv_a=v_conv_a, ln_a_g=v_ln_a_g, ln_a_b=v_ln_a_b, conv_b=v_conv_b,
             w_out=v_w_out, norm_mix_odd=v_norm_mix_odd, w_pool=v_w_pool, pool_scale=v_pool_scale, norm_ffn=v_norm_ffn,
             w_up=v_w_up, conv_ffn_w=v_conv_ffn_w, w_down=v_w_down, norm_final=v_norm_final)
    chip = 2 * lax.axis_index("x") + lax.axis_index("y")
    place = jnp.stack([chip, lax.axis_index("c")]).astype(jnp.int32)

    half = lambda a, name: a.astype(BF16).reshape((2,) + HALF[name])
    shard = {"w_in": half(w_in[0], "w_in"), "w_out": half(w_out[0], "w_out")}
    small_shapes = [w[k].shape[-2:] if w[k].ndim == 3 and k != "conv_ffn_w" else (w[k].size // w[k].shape[-1], w[k].shape[-1])
                    for k in SMALL_SHARDED]
    rep = dict(norm_mix_even=norm_mix_even, ln_a_g=ln_a_g, ln_a_b=ln_a_b, norm_ffn=norm_ffn, norm_final=norm_final.reshape(1, D))
    as3 = lambda a: a.reshape(a.shape[0], -1, a.shape[-1])

    def adamw(k, total):
        outs, _ = _adamw_big(as3(w[k]), as3(total), as3(m[k]), as3(v[k]), ADAMW_ROWS[k], "adamw_" + k)
        return [a.reshape(w[k].shape) for a in outs]

    grad_x, done, summed = _train_step(
        x[0], loss_target[0], place, shard, (w_up, w_down, w_pool[0]), rep, _pack([w[k] for k in SMALL_SHARDED]), small_shapes,
        adamw)
    loss = summed["loss"][0, 0]

    grad, delta, new_m, new_v = ({k: done[k][q] for k in BIG} for q in range(4))
    for k in SMALL_ALL:
        gsum = summed[k]
        if k in SMALL_SHARDED:
            cols = w[k].shape[-1]
            gsum = lax.dynamic_slice_in_dim(gsum, chip * cols, cols, axis=gsum.ndim - 1)
        grad[k] = gsum.reshape(w[k].shape)

    as2 = lambda a: a.reshape(1, -1) if a.ndim == 1 else a[0] if a.ndim == 3 and a.shape[0] == 1 else a
    ds, ms, vs = _adamw_small(*[[as2(t[k]) for k in SMALL_ALL] for t in (w, grad, m, v)])
    for k, d2, m2, v2 in zip(SMALL_ALL, ds, ms, vs):
        delta[k], new_m[k], new_v[k] = (a.reshape(w[k].shape) for a in (d2, m2, v2))

    return (loss, grad_x[None], *[grad[k] for k in WEIGHT_ORDER], *[delta[k] for k in WEIGHT_ORDER],
            *[new_m[k] for k in WEIGHT_ORDER], *[new_v[k] for k in WEIGHT_ORDER])
```
